```python
import jax, jax.numpy as jnp
from jax import lax
import numpy as np

D_MODEL = 1024
BATCH = 8
SEQ = 2048
DEPTH = 2
DEC_BATCH = 128
DEC_SEQ = 8
PAST_LEN = 16384
PAGE_SIZE = 128

N_EVEN = (DEPTH + 1) // 2
N_ODD = DEPTH // 2
DEEPNORM_ALPHA = (2.0 * DEPTH) ** 0.25
DEEPNORM_BETA = (8.0 * DEPTH) ** -0.25
LN_EPS = 1e-5
RMS_EPS = 1e-6
CHUNK = 64

GLA_HEADS = 4
GLA_DK = 64
GLA_DV = 128
GLA_RANK = 16
GLA_TAU = 16.0
GLA_QK_DIM = GLA_HEADS * GLA_DK
GLA_V_DIM = GLA_HEADS * GLA_DV
CONF_DIM = 512
CONF_WIDTH = 31
HGRN_HEADS = 4
HGRN_DK = 128
HGRN_DV = 128
HGRN_K_DIM = HGRN_HEADS * HGRN_DK
HGRN_V_DIM = HGRN_HEADS * HGRN_DV
SSM_HEADS = 8
SSM_HEADDIM = 64
SSM_STATE = 128
SSM_GROUPS = 2
SSM_CONV = 4
SSM_INNER = SSM_HEADS * SSM_HEADDIM
SSM_CONV_DIM = SSM_INNER + 2 * SSM_GROUPS * SSM_STATE
N_EXPERTS = 32
TOP_K = 4
D_FF = D_MODEL
SWIGLU_ALPHA = 1.702
SWIGLU_LIMIT = 7.0
MOE_BLOCK = 128

SPLIT_EVEN = (GLA_QK_DIM, GLA_QK_DIM, GLA_V_DIM, GLA_V_DIM, GLA_RANK, 2 * CONF_DIM)
IN_EVEN = sum(SPLIT_EVEN)
MIX_EVEN = GLA_V_DIM + CONF_DIM
SPLIT_ODD = (HGRN_K_DIM, HGRN_K_DIM, HGRN_V_DIM, HGRN_V_DIM, SSM_INNER, SSM_CONV_DIM, SSM_HEADS)
IN_ODD = sum(SPLIT_ODD)
MIX_ODD = HGRN_V_DIM + SSM_INNER

kernel_name = 'hybrid_gla_conformer_hgrn2_mamba2_moe_decode_step'


def _split(h, sizes):
    return jnp.split(h, np.cumsum(sizes)[:-1].tolist(), axis=-1)


def _layernorm(x, g, b):
    xf = x.astype(jnp.float32)
    mu = jnp.mean(xf, axis=-1, keepdims=True)
    var = jnp.mean(jnp.square(xf - mu), axis=-1, keepdims=True)
    return ((xf - mu) * lax.rsqrt(var + LN_EPS) * g.astype(jnp.float32) + b.astype(jnp.float32)).astype(x.dtype)


def _rmsnorm(x, w):
    xf = x.astype(jnp.float32)
    return (xf * lax.rsqrt(jnp.mean(xf * xf, axis=-1, keepdims=True) + RMS_EPS) * w.astype(jnp.float32)).astype(x.dtype)


def _causal_dwconv(u, hist, w, b):
    full = jnp.concatenate([hist.astype(u.dtype), u], axis=1)
    y = lax.conv_general_dilated(full, w[:, None, :].astype(u.dtype), window_strides=(1,), padding='VALID',
                                 dimension_numbers=('NWC', 'WIO', 'NWC'), feature_group_count=u.shape[-1])
    return y + b.astype(u.dtype), full[:, -(w.shape[0] - 1):]


def _gated_linear_scan(q, k, v, log_a, s0):
    bsz, seq, n_heads = q.shape[:3]
    dv = v.shape[-1]
    c = min(CHUNK, seq)
    n = -(-seq // c)
    pad = n * c - seq

    def prep(t):
        t = jnp.pad(t.astype(jnp.float32), ((0, 0), (0, pad), (0, 0), (0, 0)))
        t = t.reshape(bsz, n, c, t.shape[2], t.shape[3])
        return jnp.transpose(t, (1, 0, 3, 2, 4))

    qs, ks, vs, gs = prep(q), prep(k), prep(v), prep(log_a)
    causal = jnp.tril(jnp.ones((c, c), dtype=bool))

    def step(S, inp):
        qc, kc, vc, gc = inp
        b = jnp.cumsum(gc, axis=-2)
        diff = b[..., :, None, :] - b[..., None, :, :]
        decay = jnp.where(causal[:, :, None], jnp.exp(jnp.minimum(diff, 0.0)), 0.0)
        scores = jnp.sum(qc[..., :, None, :] * kc[..., None, :, :] * decay, axis=-1)
        o = jnp.einsum('bhij,bhjv->bhiv', scores, vc) + jnp.einsum('bhik,bhkv->bhiv', qc * jnp.exp(b), S)
        b_last = b[..., -1:, :]
        S = jnp.exp(b_last[..., 0, :])[..., None] * S + jnp.einsum('bhjk,bhjv->bhkv', kc * jnp.exp(b_last - b), vc)
        return S, o

    S, o = lax.scan(step, s0.astype(jnp.float32), (qs, ks, vs, gs))
    o = jnp.transpose(o, (1, 0, 3, 2, 4)).reshape(bsz, n * c, n_heads, dv)[:, :seq]
    return o, S


def _moe(x, w_router, b_router, w_gate, b_gate, w_up, b_up, w_down, b_down):
    shp = x.shape
    xt = x.reshape(-1, D_MODEL)
    T = xt.shape[0]
    logits = (xt @ w_router).astype(jnp.float32) + b_router.astype(jnp.float32)
    top_v, top_e = lax.top_k(logits, TOP_K)
    gates = jax.nn.softmax(top_v, axis=-1)
    n_assign = T * TOP_K
    flat_e = top_e.reshape(-1).astype(jnp.int32)
    flat_tok = jnp.arange(n_assign, dtype=jnp.int32) // TOP_K
    flat_g = gates.reshape(-1)
    order = jnp.argsort(flat_e)
    e_sorted = flat_e[order]
    counts = jnp.zeros((N_EXPERTS,), jnp.int32).at[flat_e].add(1)
    padded = (counts + MOE_BLOCK - 1) // MOE_BLOCK * MOE_BLOCK
    pad_end = jnp.cumsum(padded)
    pad_start = pad_end - padded
    grp_start = jnp.cumsum(counts) - counts
    dest = pad_start[e_sorted] + jnp.arange(n_assign, dtype=jnp.int32) - grp_start[e_sorted]
    n_blocks = -(-n_assign // MOE_BLOCK) + N_EXPERTS
    n_rows = n_blocks * MOE_BLOCK
    row_tok = jnp.full((n_rows,), T, jnp.int32).at[dest].set(flat_tok[order])
    row_gate = jnp.zeros((n_rows,), jnp.float32).at[dest].set(flat_g[order])
    block_start = jnp.arange(n_blocks, dtype=jnp.int32) * MOE_BLOCK
    block_e = jnp.minimum(jnp.searchsorted(pad_end, block_start, side='right'), N_EXPERTS - 1)
    x_pad = jnp.concatenate([xt, jnp.zeros((1, D_MODEL), xt.dtype)], axis=0)

    def expert_block(args):
        tok, e = args
        xb = x_pad[tok]
        g = jnp.minimum(xb @ w_gate[e] + b_gate[e], SWIGLU_LIMIT)
        u = jnp.clip(xb @ w_up[e] + b_up[e], -SWIGLU_LIMIT, SWIGLU_LIMIT)
        hmid = (u + 1.0) * (g * jax.nn.sigmoid(SWIGLU_ALPHA * g))
        return hmid @ w_down[e] + b_down[e]

    out = lax.map(expert_block, (row_tok.reshape(n_blocks, MOE_BLOCK), block_e))
    out = out.reshape(n_rows, D_MODEL) * row_gate[:, None].astype(out.dtype)
    y = jnp.zeros((T + 1, D_MODEL), out.dtype).at[row_tok].add(out)[:T]
    return y.reshape(shp).astype(x.dtype)


def _mix_even(x, i, p, s_gla, c_conf):
    bsz, seq = x.shape[:2]
    h = x @ p['w_in_even'][i]
    hq, hk, hv, hg, hlr, hglu = _split(h, SPLIT_EVEN)
    q = hq.reshape(bsz, seq, GLA_HEADS, GLA_DK) * (GLA_DK ** -0.5)
    k = hk.reshape(bsz, seq, GLA_HEADS, GLA_DK)
    v = hv.reshape(bsz, seq, GLA_HEADS, GLA_DV)
    log_a = jax.nn.log_sigmoid((hlr @ p['w_gla_gate_lr'][i] + p['b_gla_gate'][i]).astype(jnp.float32)) / GLA_TAU
    o, s_new = _gated_linear_scan(q, k, v, log_a.reshape(bsz, seq, GLA_HEADS, GLA_DK), s_gla)
    o = _rmsnorm(o, p['gla_norm_w'][i].reshape(GLA_HEADS, GLA_DV)).reshape(bsz, seq, GLA_V_DIM)
    out_a = o.astype(x.dtype) * jax.nn.silu(hg)
    a, gate = jnp.split(hglu, 2, axis=-1)
    u = a * jax.nn.sigmoid(gate)
    cv, c_new = _causal_dwconv(u, c_conf, p['conf_conv_w'][i], p['conf_conv_b'][i])
    out_b = jax.nn.silu(_layernorm(cv, p['conf_ln_g'][i], p['conf_ln_b'][i]))
    mix = jnp.concatenate([out_a, out_b.astype(x.dtype)], axis=-1) @ p['w_out_even'][i]
    return mix, s_new, c_new


def _mix_odd(x, i, lb, p, s_hgrn, s_ssm, c_mconv):
    bsz, seq = x.shape[:2]
    h = x @ p['w_in_odd'][i]
    hq, hf, hi, hg, hz, hxbc, hdt = _split(h, SPLIT_ODD)
    q = jax.nn.silu(hq).reshape(bsz, seq, HGRN_HEADS, HGRN_DK) * (HGRN_DK ** -0.5)
    f = lb + (1.0 - lb) * jax.nn.sigmoid(hf.astype(jnp.float32))
    f = f.reshape(bsz, seq, HGRN_HEADS, HGRN_DK)
    vi = hi.reshape(bsz, seq, HGRN_HEADS, HGRN_DV)
    o, sh_new = _gated_linear_scan(q, 1.0 - f, vi, jnp.log(f), s_hgrn)
    o = _rmsnorm(o, p['hgrn_norm_w'][i].reshape(HGRN_HEADS, HGRN_DV)).reshape(bsz, seq, HGRN_V_DIM)
    out_c = o.astype(x.dtype) * jax.nn.silu(hg)
    xbc, cm_new = _causal_dwconv(hxbc, c_mconv, p['mamba_conv_w'][i], p['mamba_conv_b'][i])
    xbc = jax.nn.silu(xbc)
    xs, bm, cm = _split(xbc, (SSM_INNER, SSM_GROUPS * SSM_STATE, SSM_GROUPS * SSM_STATE))
    rep = SSM_HEADS // SSM_GROUPS
    xs = xs.reshape(bsz, seq, SSM_HEADS, SSM_HEADDIM).astype(jnp.float32)
    bm = jnp.repeat(bm.reshape(bsz, seq, SSM_GROUPS, SSM_STATE), rep, axis=2)
    cm = jnp.repeat(cm.reshape(bsz, seq, SSM_GROUPS, SSM_STATE), rep, axis=2)
    dt = jax.nn.softplus(hdt.astype(jnp.float32) + p['mamba_dt_bias'][i].astype(jnp.float32))
    log_a = (dt * -jnp.exp(p['mamba_a_log'][i].astype(jnp.float32)))[..., None]
    o, ss_new = _gated_linear_scan(cm, bm, xs * dt[..., None], log_a, s_ssm)
    y = o + p['mamba_d'][i].astype(jnp.float32)[:, None] * xs
    y = y.reshape(bsz, seq, SSM_INNER) * jax.nn.silu(hz.astype(jnp.float32))
    y = _rmsnorm(y.reshape(bsz, seq, SSM_GROUPS, SSM_INNER // SSM_GROUPS),
                 p['mamba_norm_w'][i].reshape(SSM_GROUPS, SSM_INNER // SSM_GROUPS)).reshape(bsz, seq, SSM_INNER)
    mix = jnp.concatenate([out_c, y.astype(x.dtype)], axis=-1) @ p['w_out_odd'][i]
    return mix, sh_new, ss_new, cm_new


def _trunk(x, s_gla, c_conf, s_hgrn, s_ssm, c_mconv, lower_bound, p):
    gla_o, conf_o, hgrn_o, ssm_o, mconv_o = [], [], [], [], []
    for layer in range(DEPTH):
        i = layer // 2
        if layer % 2 == 0:
            mix, sg, cc = _mix_even(x, i, p, s_gla[i], c_conf[i])
            gla_o.append(sg)
            conf_o.append(cc)
        else:
            mix, sh, ss, cm = _mix_odd(x, i, lower_bound[layer], p, s_hgrn[i], s_ssm[i], c_mconv[i])
            hgrn_o.append(sh)
            ssm_o.append(ss)
            mconv_o.append(cm)
        x = _layernorm(DEEPNORM_ALPHA * x + mix.astype(x.dtype), p['ln1_g'][layer], p['ln1_b'][layer])
        ff = _moe(x, p['router_w'][layer], p['router_b'][layer], p['expert_w_gate'][layer], p['expert_b_gate'][layer],
                  p['expert_w_up'][layer], p['expert_b_up'][layer], p['expert_w_down'][layer], p['expert_b_down'][layer])
        x = _layernorm(DEEPNORM_ALPHA * x + ff, p['ln2_g'][layer], p['ln2_b'][layer])
    return (x,
            jnp.stack(gla_o).astype(s_gla.dtype), jnp.stack(conf_o).astype(c_conf.dtype),
            jnp.stack(hgrn_o).astype(s_hgrn.dtype), jnp.stack(ssm_o).astype(s_ssm.dtype),
            jnp.stack(mconv_o).astype(c_mconv.dtype))


def setup_inputs(seed: int = 0) -> dict:
    key = jax.random.key(seed)
    ks = iter(jax.random.split(key, 48))

    def nrm(shape, scale):
        return jax.random.normal(next(ks), shape, jnp.float32) * scale

    def gain(shape):
        return 1.0 + nrm(shape, 0.02)

    dt_u = jax.random.uniform(next(ks), (N_ODD, SSM_HEADS), jnp.float32)
    dt0 = jnp.exp(dt_u * (jnp.log(0.1) - jnp.log(0.001)) + jnp.log(0.001))
    a_u = jax.random.uniform(next(ks), (N_ODD, SSM_HEADS), jnp.float32, 1.0, 16.0)
    return {
        'x_prompt': nrm((BATCH, SEQ, D_MODEL), 1.0),
        'x_sample': nrm((DEC_BATCH, DEC_SEQ, D_MODEL), 1.0),
        'state_gla': nrm((N_EVEN, DEC_BATCH, GLA_HEADS, GLA_DK, GLA_DV), 1.0),
        'cache_conformer': nrm((N_EVEN, DEC_BATCH, CONF_WIDTH - 1, CONF_DIM), 0.5),
        'state_hgrn': nrm((N_ODD, DEC_BATCH, HGRN_HEADS, HGRN_DK, HGRN_DV), 1.0),
        'state_ssm': nrm((N_ODD, DEC_BATCH, SSM_HEADS, SSM_STATE, SSM_HEADDIM), 0.1),
        'cache_mamba_conv': nrm((N_ODD, DEC_BATCH, SSM_CONV - 1, SSM_CONV_DIM), 1.0),
        'w_in_even': nrm((N_EVEN, D_MODEL, IN_EVEN), D_MODEL ** -0.5),
        'w_gla_gate_lr': nrm((N_EVEN, GLA_RANK, GLA_QK_DIM), GLA_RANK ** -0.5),
        'b_gla_gate': nrm((N_EVEN, GLA_QK_DIM), 0.1),
        'gla_norm_w': gain((N_EVEN, GLA_V_DIM)),
        'conf_conv_w': nrm((N_EVEN, CONF_WIDTH, CONF_DIM), CONF_WIDTH ** -0.5),
        'conf_conv_b': nrm((N_EVEN, CONF_DIM), 0.02),
        'conf_ln_g': gain((N_EVEN, CONF_DIM)),
        'conf_ln_b': nrm((N_EVEN, CONF_DIM), 0.02),
        'w_out_even': nrm((N_EVEN, MIX_EVEN, D_MODEL), MIX_EVEN ** -0.5 * DEEPNORM_BETA),
        'w_in_odd': nrm((N_ODD, D_MODEL, IN_ODD), D_MODEL ** -0.5),
        'hgrn_lower_bounds': nrm((DEPTH, HGRN_K_DIM), 0.1),
        'hgrn_norm_w': gain((N_ODD, HGRN_V_DIM)),
        'mamba_conv_w': nrm((N_ODD, SSM_CONV, SSM_CONV_DIM), SSM_CONV ** -0.5),
        'mamba_conv_b': nrm((N_ODD, SSM_CONV_DIM), 0.02),
        'mamba_dt_bias': dt0 + jnp.log(-jnp.expm1(-dt0)),
        'mamba_a_log': jnp.log(a_u),
        'mamba_d': 1.0 + nrm((N_ODD, SSM_HEADS), 0.1),
        'mamba_norm_w': gain((N_ODD, SSM_INNER)),
        'w_out_odd': nrm((N_ODD, MIX_ODD, D_MODEL), MIX_ODD ** -0.5 * DEEPNORM_BETA),
        'ln1_g': gain((DEPTH, D_MODEL)),
        'ln1_b': nrm((DEPTH, D_MODEL), 0.02),
        'ln2_g': gain((DEPTH, D_MODEL)),
        'ln2_b': nrm((DEPTH, D_MODEL), 0.02),
        'router_w': nrm((DEPTH, D_MODEL, N_EXPERTS), D_MODEL ** -0.5),
        'router_b': nrm((DEPTH, N_EXPERTS), 0.01),
        'expert_w_gate': nrm((DEPTH, N_EXPERTS, D_MODEL, D_FF), D_MODEL ** -0.5),
        'expert_b_gate': nrm((DEPTH, N_EXPERTS, D_FF), 0.02),
        'expert_w_up': nrm((DEPTH, N_EXPERTS, D_MODEL, D_FF), D_MODEL ** -0.5),
        'expert_b_up': nrm((DEPTH, N_EXPERTS, D_FF), 0.02),
        'expert_w_down': nrm((DEPTH, N_EXPERTS, D_FF, D_MODEL), D_FF ** -0.5 * DEEPNORM_BETA),
        'expert_b_down': nrm((DEPTH, N_EXPERTS, D_MODEL), 0.02),
    }


def reference(x_prompt, x_sample, state_gla, cache_conformer, state_hgrn, state_ssm, cache_mamba_conv,
              w_in_even, w_gla_gate_lr, b_gla_gate, gla_norm_w, conf_conv_w, conf_conv_b, conf_ln_g, conf_ln_b,
              w_out_even, w_in_odd, hgrn_lower_bounds, hgrn_norm_w, mamba_conv_w, mamba_conv_b, mamba_dt_bias,
              mamba_a_log, mamba_d, mamba_norm_w, w_out_odd, ln1_g, ln1_b, ln2_g, ln2_b, router_w, router_b,
              expert_w_gate, expert_b_gate, expert_w_up, expert_b_up, expert_w_down, expert_b_down):
    p = {
        'w_in_even': w_in_even, 'w_gla_gate_lr': w_gla_gate_lr, 'b_gla_gate': b_gla_gate, 'gla_norm_w': gla_norm_w,
        'conf_conv_w': conf_conv_w, 'conf_conv_b': conf_conv_b, 'conf_ln_g': conf_ln_g, 'conf_ln_b': conf_ln_b,
        'w_out_even': w_out_even, 'w_in_odd': w_in_odd, 'hgrn_norm_w': hgrn_norm_w,
        'mamba_conv_w': mamba_conv_w, 'mamba_conv_b': mamba_conv_b, 'mamba_dt_bias': mamba_dt_bias,
        'mamba_a_log': mamba_a_log, 'mamba_d': mamba_d, 'mamba_norm_w': mamba_norm_w, 'w_out_odd': w_out_odd,
        'ln1_g': ln1_g, 'ln1_b': ln1_b, 'ln2_g': ln2_g, 'ln2_b': ln2_b,
        'router_w': router_w, 'router_b': router_b, 'expert_w_gate': expert_w_gate, 'expert_b_gate': expert_b_gate,
        'expert_w_up': expert_w_up, 'expert_b_up': expert_b_up, 'expert_w_down': expert_w_down,
        'expert_b_down': expert_b_down,
    }
    lb_cum = jnp.cumsum(jax.nn.softmax(hgrn_lower_bounds.astype(jnp.float32), axis=0), axis=0)
    lower_bound = lb_cum - lb_cum[0]

    bp = x_prompt.shape[0]

    def fresh(s):
        return jnp.zeros((s.shape[0], bp) + s.shape[2:], s.dtype)

    y_prompt, gla_p, conf_p, hgrn_p, ssm_p, mconv_p = _trunk(
        x_prompt, fresh(state_gla), fresh(cache_conformer), fresh(state_hgrn), fresh(state_ssm),
        fresh(cache_mamba_conv), lower_bound, p)
    y_sample, gla_s, conf_s, hgrn_s, ssm_s, mconv_s = _trunk(
        x_sample, state_gla, cache_conformer, state_hgrn, state_ssm, cache_mamba_conv, lower_bound, p)
    return (y_prompt, y_sample, gla_p, gla_s, conf_p, conf_s, hgrn_p, hgrn_s, ssm_p, ssm_s, mconv_p, mconv_s)
```

```python
import functools

import jax
import jax.numpy as jnp
from jax import lax
from jax.experimental import pallas as pl
from jax.experimental.pallas import tpu as pltpu

F32 = jnp.float32
BF16 = jnp.bfloat16
I32 = jnp.int32
U32 = jnp.uint32
HIGHEST = lax.Precision.HIGHEST

D_MODEL = 1024
DEPTH = 2
DEEPNORM_ALPHA = (2.0 * DEPTH) ** 0.25
LN_EPS = 1e-5
RMS_EPS = 1e-6
LANE = 128
HEAD_W = 128
GLA_HEADS, GLA_DK, GLA_RANK, GLA_TAU = 4, 64, 16, 16.0
CONF_DIM, CONF_WIDTH = 512, 31
HGRN_HEADS, HGRN_DK = 4, 128
SSM_HEADS, SSM_HEADDIM, SSM_STATE, SSM_GROUPS, SSM_CONV = 8, 64, 128, 2, 4
SSM_INNER = SSM_HEADS * SSM_HEADDIM
SSM_CONV_DIM = SSM_INNER + 2 * SSM_GROUPS * SSM_STATE
N_EXPERTS, TOP_K = 32, 4
SWIGLU_ALPHA, SWIGLU_LIMIT = 1.702, 7.0
SCAN_CHUNK = 64
PROMPT_TILE = 256
SAMPLE_SEQS = 16
TOKEN_TILE = 512
MOE_ROWS = 256
VMEM_LIMIT = 56 * 1024 * 1024


def _cparams(n_axes):
    return pltpu.CompilerParams(dimension_semantics=("arbitrary",) * n_axes, vmem_limit_bytes=VMEM_LIMIT)


def _silu(x):
    return x * jax.nn.sigmoid(x)


def _softplus(x):
    return jnp.maximum(x, 0.0) + jnp.log(1.0 + jnp.exp(-jnp.abs(x)))


def _log_sigmoid(x):
    return jnp.minimum(x, 0.0) - jnp.log(1.0 + jnp.exp(-jnp.abs(x)))


def _layernorm(y, g, b):
    mu = jnp.mean(y, axis=-1, keepdims=True)
    d = y - mu
    var = jnp.mean(d * d, axis=-1, keepdims=True)
    return d * lax.rsqrt(var + LN_EPS) * g + b


def _dot(a, b):
    return jnp.dot(a.astype(BF16), b.astype(BF16), preferred_element_type=F32)


def _dot_nt(a, b):
    return lax.dot_general(a.astype(BF16), b.astype(BF16), (((1,), (1,)), ((), ())), preferred_element_type=F32)


def _dot_tn(a, b):
    return lax.dot_general(a.astype(BF16), b.astype(BF16), (((0,), (0,)), ((), ())), preferred_element_type=F32)


def _tri(c):
    r = lax.broadcasted_iota(I32, (c, c), 0)
    k = lax.broadcasted_iota(I32, (c, c), 1)
    return r >= k


def _inproj_body(x_ref, w_ref, o_ref):
    xb = x_ref[...].astype(BF16)
    n = w_ref.shape[1]
    for c0 in range(0, n, 512):
        c1 = min(c0 + 512, n)
        o_ref[:, c0:c1] = jnp.dot(xb, w_ref[:, c0:c1], preferred_element_type=F32)


def _inproj(x, w):
    t, k = x.shape
    n = w.shape[1]
    return pl.pallas_call(
        _inproj_body,
        grid=(t // TOKEN_TILE,),
        in_specs=[pl.BlockSpec((TOKEN_TILE, k), lambda i: (i, 0)),
                  pl.BlockSpec((k, n), lambda i: (0, 0))],
        out_specs=pl.BlockSpec((TOKEN_TILE, n), lambda i: (i, 0)),
        out_shape=jax.ShapeDtypeStruct((t, n), F32),
        compiler_params=_cparams(1),
        name="inproj",
    )(x, w)


def _pack_halves(y):
    half = y.shape[1] // 2
    hi = lax.bitcast_convert_type(y[:, :half].astype(BF16).astype(F32), U32)
    lo = lax.bitcast_convert_type(y[:, half:].astype(BF16).astype(F32), U32)
    return (hi & jnp.uint32(0xFFFF0000)) | (lo >> 16)


def _unpack_halves(w):
    hi = lax.bitcast_convert_type(w & jnp.uint32(0xFFFF0000), F32)
    lo = lax.bitcast_convert_type(w << 16, F32)
    return hi, lo


def _outproj_body(a_ref, b_ref, x_ref, wa_ref, wb_ref, g_ref, be_ref, rw_ref, rb_ref,
                  x1_ref, xp_ref, gate_ref, idx_ref):
    mix = (jnp.dot(a_ref[...], wa_ref[...], preferred_element_type=F32)
           + jnp.dot(b_ref[...], wb_ref[...], preferred_element_type=F32))
    x1 = _layernorm(DEEPNORM_ALPHA * x_ref[...] + mix, g_ref[...], be_ref[...])
    x1_ref[...] = x1
    xp_ref[...] = _pack_halves(x1)
    logits = _dot(x1, rw_ref[...]) + rb_ref[...]
    lane = lax.broadcasted_iota(I32, logits.shape, 1)
    vals, idxs = [], []
    for _ in range(TOP_K):
        m = jnp.max(logits, axis=-1, keepdims=True)
        sel = jnp.min(jnp.where(logits == m, lane, LANE), axis=-1, keepdims=True)
        vals.append(m)
        idxs.append(sel)
        logits = jnp.where(lane == sel, -jnp.inf, logits)
    exps = [jnp.exp(v - vals[0]) for v in vals]
    inv = 1.0 / functools.reduce(lambda p, q: p + q, exps)
    gates = jnp.zeros(logits.shape, F32)
    eidx = jnp.zeros(logits.shape, I32)
    for k in range(TOP_K):
        gates = jnp.where(lane == k, exps[k] * inv, gates)
        eidx = jnp.where(lane == k, idxs[k], eidx)
    gate_ref[...] = gates
    idx_ref[...] = eidx


def _outproj_ln_router(a, b, x, wa, wb, g, be, rw, rb):
    t = x.shape[0]
    tm = TOKEN_TILE
    row = lambda i: (i, 0)
    fix = lambda i: (0, 0)
    return pl.pallas_call(
        _outproj_body,
        grid=(t // tm,),
        in_specs=[pl.BlockSpec((tm, 512), row), pl.BlockSpec((tm, 512), row), pl.BlockSpec((tm, D_MODEL), row),
                  pl.BlockSpec((512, D_MODEL), fix), pl.BlockSpec((512, D_MODEL), fix),
                  pl.BlockSpec((1, D_MODEL), fix), pl.BlockSpec((1, D_MODEL), fix),
                  pl.BlockSpec((D_MODEL, LANE), fix), pl.BlockSpec((1, LANE), fix)],
        out_specs=[pl.BlockSpec((tm, D_MODEL), row), pl.BlockSpec((tm, 512), row),
                   pl.BlockSpec((tm, LANE), row), pl.BlockSpec((tm, LANE), row)],
        out_shape=[jax.ShapeDtypeStruct((t, D_MODEL), F32), jax.ShapeDtypeStruct((t, 512), U32),
                   jax.ShapeDtypeStruct((t, LANE), F32), jax.ShapeDtypeStruct((t, LANE), I32)],
        compiler_params=_cparams(1),
        name="outproj_ln_router",
    )(a, b, x, wa, wb, g, be, rw, rb)


def _gather_rows_body(idx_ref, src_ref, o_ref, buf, sem):
    n = buf.shape[0]

    def row_copy(r, src_row):
        return pltpu.make_async_copy(src_ref.at[pl.ds(src_row, 1)], buf.at[pl.ds(r, 1)], sem)

    def issue(r, carry):
        row_copy(r, idx_ref[0, 0, r]).start()
        return carry

    lax.fori_loop(0, n, issue, 0)

    def drain(r, carry):
        row_copy(r, 0).wait()
        return carry

    lax.fori_loop(0, n, drain, 0)
    o_ref[...] = buf[...]


def _gather_rows(src, idx, rows_per_step=256):
    n = idx.shape[0]
    w = src.shape[1]
    steps = n // rows_per_step
    return pl.pallas_call(
        _gather_rows_body,
        grid=(steps,),
        in_specs=[pl.BlockSpec((1, 1, rows_per_step), lambda i: (i, 0, 0), memory_space=pltpu.SMEM),
                  pl.BlockSpec(memory_space=pl.ANY)],
        out_specs=pl.BlockSpec((rows_per_step, w), lambda i: (i, 0)),
        out_shape=jax.ShapeDtypeStruct((n, w), src.dtype),
        scratch_shapes=[pltpu.VMEM((rows_per_step, w), src.dtype), pltpu.SemaphoreType.DMA],
        compiler_params=_cparams(1),
        name="gather_rows",
    )(idx.reshape(steps, 1, rows_per_step), src)


def _experts_body(be_ref, nu_ref, xs_ref, rg_ref, wg_ref, bg_ref, wu_ref, bu_ref, wd_ref, bd_ref,
                  o_ref, wg_s, wu_s, wd_s):
    i = pl.program_id(0)
    prev = be_ref[jnp.maximum(i - 1, 0)]

    @pl.when((i == 0) | (be_ref[i] != prev))
    def _():
        wg_s[...] = wg_ref[...].astype(BF16)
        wu_s[...] = wu_ref[...].astype(BF16)
        wd_s[...] = wd_ref[...].astype(BF16)

    @pl.when(i < nu_ref[0])
    def _():
        half = D_MODEL // 2
        x_hi, x_lo = _unpack_halves(xs_ref[...])
        x_hi = x_hi.astype(BF16)
        x_lo = x_lo.astype(BF16)
        g = (jnp.dot(x_hi, wg_s[:half, :], preferred_element_type=F32)
             + jnp.dot(x_lo, wg_s[half:, :], preferred_element_type=F32) + bg_ref[...])
        u = (jnp.dot(x_hi, wu_s[:half, :], preferred_element_type=F32)
             + jnp.dot(x_lo, wu_s[half:, :], preferred_element_type=F32) + bu_ref[...])
        g = jnp.minimum(g, SWIGLU_LIMIT)
        u = jnp.clip(u, -SWIGLU_LIMIT, SWIGLU_LIMIT)
        hmid = (u + 1.0) * (g * jax.nn.sigmoid(SWIGLU_ALPHA * g))
        out = jnp.dot(hmid.astype(BF16), wd_s[...], preferred_element_type=F32) + bd_ref[...]
        o_ref[...] = _pack_halves(out * rg_ref[...])

    @pl.when(i >= nu_ref[0])
    def _():
        o_ref[...] = jnp.zeros(o_ref.shape, o_ref.dtype)


def _experts(block_e, n_used, xs, row_gate, wg, bg, wu, bu, wd, bd):
    n_rows = xs.shape[0]
    bm = MOE_ROWS
    row = lambda i, be, nu: (i, 0)
    wsel = lambda i, be, nu: (be[i], 0, 0)
    return pl.pallas_call(
        _experts_body,
        grid_spec=pltpu.PrefetchScalarGridSpec(
            num_scalar_prefetch=2,
            grid=(n_rows // bm,),
            in_specs=[pl.BlockSpec((bm, 512), row), pl.BlockSpec((bm, 1), row),
                      pl.BlockSpec((None, D_MODEL, D_MODEL), wsel), pl.BlockSpec((None, 1, D_MODEL), wsel),
                      pl.BlockSpec((None, D_MODEL, D_MODEL), wsel), pl.BlockSpec((None, 1, D_MODEL), wsel),
                      pl.BlockSpec((None, D_MODEL, D_MODEL), wsel), pl.BlockSpec((None, 1, D_MODEL), wsel)],
            out_specs=pl.BlockSpec((bm, 512), row),
            scratch_shapes=[pltpu.VMEM((D_MODEL, D_MODEL), BF16)] * 3,
        ),
        out_shape=jax.ShapeDtypeStruct((n_rows, 512), U32),
        compiler_params=_cparams(1),
        name="experts",
    )(block_e, n_used, xs, row_gate, wg, bg, wu, bu, wd, bd)


def _combine_body(o4_ref, x_ref, g_ref, b_ref, y_ref):
    half = D_MODEL // 2
    hi = jnp.zeros((x_ref.shape[0], half), F32)
    lo = jnp.zeros((x_ref.shape[0], half), F32)
    for k in range(TOP_K):
        h, l = _unpack_halves(o4_ref[:, k * half:(k + 1) * half])
        hi = hi + h
        lo = lo + l
    x = x_ref[...]
    y_hi = DEEPNORM_ALPHA * x[:, :half] + hi
    y_lo = DEEPNORM_ALPHA * x[:, half:] + lo
    mu = (jnp.sum(y_hi, axis=-1, keepdims=True) + jnp.sum(y_lo, axis=-1, keepdims=True)) * (1.0 / D_MODEL)
    d_hi = y_hi - mu
    d_lo = y_lo - mu
    var = (jnp.sum(d_hi * d_hi, axis=-1, keepdims=True) + jnp.sum(d_lo * d_lo, axis=-1, keepdims=True)) * (1.0 / D_MODEL)
    r = lax.rsqrt(var + LN_EPS)
    y_ref[:, :half] = d_hi * r * g_ref[:, :half] + b_ref[:, :half]
    y_ref[:, half:] = d_lo * r * g_ref[:, half:] + b_ref[:, half:]


def _combine_ln(o4, x, g, b):
    t = x.shape[0]
    tm = TOKEN_TILE
    row = lambda i: (i, 0)
    fix = lambda i: (0, 0)
    return pl.pallas_call(
        _combine_body,
        grid=(t // tm,),
        in_specs=[pl.BlockSpec((tm, TOP_K * 512), row), pl.BlockSpec((tm, D_MODEL), row),
                  pl.BlockSpec((1, D_MODEL), fix), pl.BlockSpec((1, D_MODEL), fix)],
        out_specs=pl.BlockSpec((tm, D_MODEL), row),
        out_shape=jax.ShapeDtypeStruct((t, D_MODEL), F32),
        compiler_params=_cparams(1),
        name="combine_ln",
    )(o4, x, g, b)


def _moe(x1, xp, gates, eidx, ln_g, ln_b, wg, bg, wu, bu, wd, bd):
    t = x1.shape[0]
    n_assign = t * TOP_K
    bm = MOE_ROWS
    n_blocks = -(-n_assign // bm) + N_EXPERTS
    n_rows = n_blocks * bm
    flat_e = eidx.reshape(-1)
    flat_g = gates.reshape(-1)
    flat_tok = jnp.arange(n_assign, dtype=I32) // TOP_K
    onehot = (flat_e[:, None] == jnp.arange(N_EXPERTS, dtype=I32)[None, :]).astype(I32)
    csum = jnp.cumsum(onehot, axis=0)
    rank = jnp.sum(onehot * csum, axis=1) - 1
    counts = csum[-1]
    padded = (counts + bm - 1) // bm * bm
    pad_end = jnp.cumsum(padded)
    pad_start = pad_end - padded
    dest = pad_start[flat_e] + rank
    row_tok = jnp.zeros((n_rows,), I32).at[dest].set(flat_tok)
    row_gate = jnp.zeros((n_rows,), F32).at[dest].set(flat_g)
    n_used = (pad_end[-1] // bm).astype(I32)
    blk = jnp.arange(n_blocks, dtype=I32)
    block_e = jnp.minimum(jnp.searchsorted(pad_end, jnp.minimum(blk, n_used - 1) * bm, side='right'),
                          N_EXPERTS - 1).astype(I32)
    xs = _gather_rows(xp, row_tok)
    outs = _experts(block_e, n_used.reshape(1), xs, row_gate.reshape(n_rows, 1), wg, bg, wu, bu, wd, bd)
    o4 = _gather_rows(outs, dest).reshape(t, TOP_K * 512)
    return _combine_ln(o4, x1, ln_g, ln_b)


def _gla_body(mode, n_seq, n_chunk, C, *refs):
    if mode == "gla":
        hq_ref, hk_ref, hv_ref, hg_ref, hlr_ref, wlr_ref, blr_ref, nw_ref, s0_ref, o_ref, so_ref, st, o_scr = refs
    else:
        hq_ref, hk_ref, hv_ref, hg_ref, lb_ref, nw_ref, s0_ref, o_ref, so_ref, st, o_scr = refs
    n_heads = 4
    tstep = pl.program_id(1)

    @pl.when(tstep == 0)
    def _():
        for s in range(n_seq):
            for h in range(n_heads):
                st[s, h] = s0_ref[s, h].T

    if mode == "gla":
        q = hq_ref[...] * (GLA_DK ** -0.5)
        k = hk_ref[...]
        z = _dot(hlr_ref[...], wlr_ref[...]) + blr_ref[...]
        g = _log_sigmoid(z) * (1.0 / GLA_TAU)
    else:
        q = _silu(hq_ref[...]) * (HGRN_DK ** -0.5)
        lb = lb_ref[...]
        f = lb + (1.0 - lb) * jax.nn.sigmoid(hk_ref[...])
        k = 1.0 - f
        g = jnp.log(f)
    v = hv_ref[...]
    causal = _tri(C)
    tri = causal.astype(F32)
    mid = max(C // 2 - 1, 0)
    for s in range(n_seq):
        for c in range(n_chunk):
            r0 = (s * n_chunk + c) * C
            for h in range(n_heads):
                cs = slice(h * HEAD_W, (h + 1) * HEAD_W)
                qh, kh, vh, gh = q[r0:r0 + C, cs], k[r0:r0 + C, cs], v[r0:r0 + C, cs], g[r0:r0 + C, cs]
                b = jnp.dot(tri, gh, preferred_element_type=F32, precision=HIGHEST)
                b_last = b[C - 1:C, :]
                b_mid = b[mid:mid + 1, :]
                scores = lax.dot_general(qh * jnp.exp(b - b_mid), kh * jnp.exp(b_mid - b), (((1,), (1,)), ((), ())),
                                         preferred_element_type=F32, precision=HIGHEST)
                scores = jnp.where(causal, scores, 0.0)
                s_t = st[s, h]
                o = _dot(scores, vh) + _dot_nt(qh * jnp.exp(b), s_t)
                st[s, h] = s_t * jnp.exp(b_last) + _dot_tn(vh, kh * jnp.exp(b_last - b))
                ms = jnp.mean(o * o, axis=-1, keepdims=True)
                o_scr[r0:r0 + C, cs] = o * lax.rsqrt(ms + RMS_EPS)
    o_ref[...] = (o_scr[...] * nw_ref[...] * _silu(hg_ref[...])).astype(BF16)

    @pl.when(tstep == pl.num_programs(1) - 1)
    def _():
        for s in range(n_seq):
            for h in range(n_heads):
                so_ref[s, h] = st[s, h].T


def _seq_layout(n_batch, seq_len, row_off, sample):
    if sample:
        n_seq, n_chunk, C = SAMPLE_SEQS, 1, seq_len
        rows = n_seq * C
        grid = (n_batch // n_seq, 1)
        blk0 = row_off // rows
        rb = lambda i, t: blk0 + i
    else:
        n_seq, n_chunk, C = 1, PROMPT_TILE // SCAN_CHUNK, SCAN_CHUNK
        rows = PROMPT_TILE
        tiles = seq_len // rows
        grid = (n_batch, tiles)
        blk0 = row_off // rows
        rb = lambda i, t: blk0 + i * tiles + t
    return n_seq, n_chunk, C, rows, grid, rb


def _gla_call(mode, h, cols, extra, nw, s0, n_batch, seq_len, row_off, sample):
    n_seq, n_chunk, C, rows, grid, rb = _seq_layout(n_batch, seq_len, row_off, sample)
    colspec = lambda c0, w: pl.BlockSpec((rows, w), lambda i, t: (rb(i, t), c0 // w))
    fix2 = lambda i, t: (0, 0)
    in_specs = [colspec(cols[0], 512), colspec(cols[1], 512), colspec(cols[2], 512), colspec(cols[3], 512)]
    args = [h, h, h, h]
    if mode == "gla":
        wlr, blr = extra
        in_specs += [colspec(cols[4], LANE), pl.BlockSpec((LANE, 512), fix2), pl.BlockSpec((1, 512), fix2)]
        args += [h, wlr, blr]
    else:
        in_specs += [pl.BlockSpec((1, 512), fix2)]
        args += [extra]
    st_spec = pl.BlockSpec((n_seq, 4, HEAD_W, HEAD_W), lambda i, t: (i, 0, 0, 0))
    in_specs += [pl.BlockSpec((1, 512), fix2), st_spec]
    args += [nw, s0]
    t_rows = n_batch * seq_len
    out_blk0 = 0
    o_spec = pl.BlockSpec((rows, 512), lambda i, t: (rb(i, t) - row_off // rows, 0))
    return pl.pallas_call(
        functools.partial(_gla_body, mode, n_seq, n_chunk, C),
        grid=grid,
        in_specs=in_specs,
        out_specs=[o_spec, st_spec],
        out_shape=[jax.ShapeDtypeStruct((t_rows, 512), BF16),
                   jax.ShapeDtypeStruct((n_batch, 4, HEAD_W, HEAD_W), F32)],
        scratch_shapes=[pltpu.VMEM((n_seq, 4, HEAD_W, HEAD_W), F32), pltpu.VMEM((rows, 512), F32)],
        compiler_params=_cparams(2),
        name=mode + ("_sample" if sample else "_prompt"),
    )(*args)


def _round_bf16(x, on=True):
    return x.astype(BF16).astype(F32) if on else x


def _conf_body(n_seq, L, round_x, round_w, a_ref, gt_ref, hist_ref, w_ref, b_ref, g_ref, be_ref, o_ref, co_ref, buf, bufr):
    tstep = pl.program_id(1)
    hist = CONF_WIDTH - 1
    pad = 32 - hist

    @pl.when(tstep == 0)
    def _():
        for s in range(n_seq):
            buf[s, pad:32, :] = hist_ref[s]
            bufr[s, pad:32, :] = _round_bf16(hist_ref[s], round_x)

    u = a_ref[...] * jax.nn.sigmoid(gt_ref[...])
    ur = _round_bf16(u, round_x)
    for s in range(n_seq):
        buf[s, 32:32 + L, :] = u[s * L:(s + 1) * L, :]
        bufr[s, 32:32 + L, :] = ur[s * L:(s + 1) * L, :]
    w = _round_bf16(w_ref[...], round_w)
    for s in range(n_seq):
        acc = jnp.zeros((L, CONF_DIM), F32)
        for j in range(CONF_WIDTH):
            acc = acc + bufr[s, pad + j:pad + j + L, :] * w[j:j + 1, :]
        y = _silu(_layernorm(acc + b_ref[...], g_ref[...], be_ref[...]))
        o_ref[s * L:(s + 1) * L, :] = y.astype(o_ref.dtype)
        tail = buf[s, L + pad:L + 32, :]
        buf[s, pad:32, :] = tail
        tailr = bufr[s, L + pad:L + 32, :]
        bufr[s, pad:32, :] = tailr

    @pl.when(tstep == pl.num_programs(1) - 1)
    def _():
        for s in range(n_seq):
            co_ref[s] = buf[s, pad:32, :]


def _conf_call(h, col_a, col_g, cache, w, b, g, be, n_batch, seq_len, row_off, sample):
    n_seq, n_chunk, C, rows, grid, rb = _seq_layout(n_batch, seq_len, row_off, sample)
    L = rows // n_seq
    hist = CONF_WIDTH - 1
    colspec = lambda c0: pl.BlockSpec((rows, 512), lambda i, t: (rb(i, t), c0 // 512))
    fix2 = lambda i, t: (0, 0)
    c_spec = pl.BlockSpec((n_seq, hist, CONF_DIM), lambda i, t: (i, 0, 0))
    o_dtype = F32 if sample else BF16
    return pl.pallas_call(
        functools.partial(_conf_body, n_seq, L, True, sample),
        grid=grid,
        in_specs=[colspec(col_a), colspec(col_g), c_spec,
                  pl.BlockSpec((CONF_WIDTH, CONF_DIM), fix2), pl.BlockSpec((1, CONF_DIM), fix2),
                  pl.BlockSpec((1, CONF_DIM), fix2), pl.BlockSpec((1, CONF_DIM), fix2)],
        out_specs=[pl.BlockSpec((rows, 512), lambda i, t: (rb(i, t) - row_off // rows, 0)), c_spec],
        out_shape=[jax.ShapeDtypeStruct((n_batch * seq_len, 512), o_dtype),
                   jax.ShapeDtypeStruct((n_batch, hist, CONF_DIM), F32)],
        scratch_shapes=[pltpu.VMEM((n_seq, 32 + L, CONF_DIM), F32)] * 2,
        compiler_params=_cparams(2),
        name="conformer" + ("_sample" if sample else "_prompt"),
    )(h, h, cache, w, b, g, be)


def _ssd_body(n_seq, n_chunk, C, round_x, round_w, hz_ref, hx_ref, hdt_ref, hist_ref, s0_ref, cw_ref, cb_ref, dtb_ref, alog_ref,
              dvec_ref, nw_ref, o_ref, co_ref, so_ref, st, buf, bufr, xbc, y_scr):
    tstep = pl.program_id(1)
    L = n_chunk * C
    hist = SSM_CONV - 1
    pad = 8 - hist
    n_pairs = SSM_HEADS // 2

    @pl.when(tstep == 0)
    def _():
        for s in range(n_seq):
            buf[s, pad:8, :] = hist_ref[s]
            bufr[s, pad:8, :] = _round_bf16(hist_ref[s], round_x)
            for m in range(n_pairs):
                st[s, m] = s0_ref[s, m]

    cw = _round_bf16(cw_ref[...], round_w)
    for s in range(n_seq):
        hx = hx_ref[s * L:(s + 1) * L, :]
        buf[s, 8:8 + L, :] = hx
        bufr[s, 8:8 + L, :] = _round_bf16(hx, round_x)
        acc = jnp.zeros((L, SSM_CONV_DIM), F32)
        for j in range(SSM_CONV):
            acc = acc + bufr[s, pad + j:pad + j + L, :] * cw[j:j + 1, :]
        xbc[s * L:(s + 1) * L, :] = _silu(acc + cb_ref[...])
        tail = buf[s, L + pad:L + 8, :]
        buf[s, pad:8, :] = tail
        tailr = bufr[s, L + pad:L + 8, :]
        bufr[s, pad:8, :] = tailr

    dt = _softplus(hdt_ref[...] + dtb_ref[...])
    la = dt * (-jnp.exp(alog_ref[...]))
    hrow = lax.broadcasted_iota(I32, (LANE, SSM_INNER), 0)
    hcol = lax.broadcasted_iota(I32, (LANE, SSM_INNER), 1) // SSM_HEADDIM
    expand = (hrow == hcol).astype(F32)
    dtx = jnp.dot(dt, expand, preferred_element_type=F32, precision=HIGHEST)
    causal = _tri(C)
    tri = causal.astype(F32)
    lane = lax.broadcasted_iota(I32, (C, HEAD_W), 1)
    for s in range(n_seq):
        for c in range(n_chunk):
            r0 = (s * n_chunk + c) * C
            rs = slice(r0, r0 + C)
            la_c = la[rs, :]
            bcol = jnp.dot(tri, la_c, preferred_element_type=F32, precision=HIGHEST)
            brow = lax.dot_general(la_c, tri, (((0,), (1,)), ((), ())), preferred_element_type=F32,
                                   precision=HIGHEST)
            xs_c = xbc[rs, 0:SSM_INNER]
            v_c = xs_c * dtx[rs, :]
            for m in range(n_pairs):
                grp = (2 * m) // (SSM_HEADS // SSM_GROUPS)
                bm = xbc[rs, SSM_INNER + grp * SSM_STATE:SSM_INNER + (grp + 1) * SSM_STATE]
                cm = xbc[rs, SSM_INNER + (SSM_GROUPS + grp) * SSM_STATE:SSM_INNER + (SSM_GROUPS + grp + 1) * SSM_STATE]
                gmat = lax.dot_general(cm, bm, (((1,), (1,)), ((), ())), preferred_element_type=F32,
                                       precision=HIGHEST)
                ps = slice(m * HEAD_W, (m + 1) * HEAD_W)
                vp = v_c[:, ps]
                s_t = st[s, m]
                o_halves = []
                for hh in range(2):
                    hd = 2 * m + hh
                    bc = bcol[:, hd:hd + 1]
                    br = brow[hd:hd + 1, :]
                    dec = jnp.where(causal, jnp.exp(jnp.minimum(bc - br, 0.0)), 0.0)
                    b_last = bcol[C - 1:C, hd:hd + 1]
                    o_halves.append(_dot(gmat * dec, vp) + _dot_nt(cm * jnp.exp(bc), s_t))
                    kv = _dot_tn(vp, bm * jnp.exp(b_last - bc))
                    vs = slice(hh * SSM_HEADDIM, (hh + 1) * SSM_HEADDIM)
                    st[s, m, vs, :] = s_t[vs, :] * jnp.exp(b_last) + kv[vs, :]
                o_pair = jnp.where(lane < SSM_HEADDIM, o_halves[0], o_halves[1])
                y_scr[rs, ps] = o_pair + dvec_ref[:, ps] * xs_c[:, ps]
    y = y_scr[...] * _silu(hz_ref[...])
    gw = SSM_INNER // SSM_GROUPS
    for grp in range(SSM_GROUPS):
        gs = slice(grp * gw, (grp + 1) * gw)
        yg = y[:, gs]
        ms = jnp.mean(yg * yg, axis=-1, keepdims=True)
        o_ref[:, gs] = (yg * lax.rsqrt(ms + RMS_EPS) * nw_ref[:, gs]).astype(BF16)

    @pl.when(tstep == pl.num_programs(1) - 1)
    def _():
        for s in range(n_seq):
            co_ref[s] = buf[s, pad:8, :]
            for m in range(n_pairs):
                so_ref[s, m] = st[s, m]


def _ssd_call(h, col_z, col_x, col_dt, cache, s0, cw, cb, dtb, alog, dvec, nw, n_batch, seq_len, row_off, sample):
    n_seq, n_chunk, C, rows, grid, rb = _seq_layout(n_batch, seq_len, row_off, sample)
    L = rows // n_seq
    hist = SSM_CONV - 1
    n_pairs = SSM_HEADS // 2
    colspec = lambda c0, w: pl.BlockSpec((rows, w), lambda i, t: (rb(i, t), c0 // w))
    fix2 = lambda i, t: (0, 0)
    c_spec = pl.BlockSpec((n_seq, hist, SSM_CONV_DIM), lambda i, t: (i, 0, 0))
    st_spec = pl.BlockSpec((n_seq, n_pairs, HEAD_W, SSM_STATE), lambda i, t: (i, 0, 0, 0))
    return pl.pallas_call(
        functools.partial(_ssd_body, n_seq, n_chunk, C, sample, True),
        grid=grid,
        in_specs=[colspec(col_z, 512), colspec(col_x, SSM_CONV_DIM), colspec(col_dt, LANE), c_spec, st_spec,
                  pl.BlockSpec((SSM_CONV, SSM_CONV_DIM), fix2), pl.BlockSpec((1, SSM_CONV_DIM), fix2),
                  pl.BlockSpec((1, LANE), fix2), pl.BlockSpec((1, LANE), fix2),
                  pl.BlockSpec((1, SSM_INNER), fix2), pl.BlockSpec((1, SSM_INNER), fix2)],
        out_specs=[pl.BlockSpec((rows, 512), lambda i, t: (rb(i, t) - row_off // rows, 0)), c_spec, st_spec],
        out_shape=[jax.ShapeDtypeStruct((n_batch * seq_len, 512), BF16),
                   jax.ShapeDtypeStruct((n_batch, hist, SSM_CONV_DIM), F32),
                   jax.ShapeDtypeStruct((n_batch, n_pairs, HEAD_W, SSM_STATE), F32)],
        scratch_shapes=[pltpu.VMEM((n_seq, n_pairs, HEAD_W, SSM_STATE), F32),
                        pltpu.VMEM((n_seq, 8 + L, SSM_CONV_DIM), F32),
                        pltpu.VMEM((n_seq, 8 + L, SSM_CONV_DIM), F32),
                        pltpu.VMEM((rows, SSM_CONV_DIM), F32),
                        pltpu.VMEM((rows, SSM_INNER), F32)],
        compiler_params=_cparams(2),
        name="ssd" + ("_sample" if sample else "_prompt"),
    )(h, h, h, cache, s0, cw, cb, dtb, alog, dvec, nw)


def _pad_heads(w, n_heads, width):
    lead = w.shape[:-1]
    w = w.reshape(lead + (n_heads, width))
    w = jnp.pad(w, [(0, 0)] * len(lead) + [(0, 0), (0, HEAD_W - width)])
    return w.reshape(lead + (n_heads * HEAD_W,))


def _row(v):
    return v.reshape(1, -1).astype(F32)


def kernel(x_prompt, x_sample, state_gla, cache_conformer, state_hgrn, state_ssm, cache_mamba_conv, w_in_even, w_gla_gate_lr, b_gla_gate, gla_norm_w, conf_conv_w, conf_conv_b, conf_ln_g, conf_ln_b, w_out_even, w_in_odd, hgrn_lower_bounds, hgrn_norm_w, mamba_conv_w, mamba_conv_b, mamba_dt_bias, mamba_a_log, mamba_d, mamba_norm_w, w_out_odd, ln1_g, ln1_b, ln2_g, ln2_b, router_w, router_b, expert_w_gate, expert_b_gate, expert_w_up, expert_b_up, expert_w_down, expert_b_down):
    bp, lp, _ = x_prompt.shape
    bs, ls, _ = x_sample.shape
    tp, ts = bp * lp, bs * ls
    x = jnp.concatenate([x_prompt.reshape(tp, D_MODEL), x_sample.reshape(ts, D_MODEL)], axis=0)

    def router_params(layer):
        rw = jnp.pad(router_w[layer], ((0, 0), (0, LANE - N_EXPERTS)))
        rb = jnp.pad(router_b[layer].astype(F32), (0, LANE - N_EXPERTS), constant_values=-1e30)
        return rw, rb.reshape(1, LANE)

    def finish_layer(layer, x, mix_a, mix_b, w_out):
        rw, rb = router_params(layer)
        x1, xp, gates, eidx = _outproj_ln_router(
            mix_a, mix_b, x, w_out[:512].astype(BF16), w_out[512:].astype(BF16),
            _row(ln1_g[layer]), _row(ln1_b[layer]), rw, rb)
        return _moe(x1, xp, gates[:, :TOP_K], eidx[:, :TOP_K], _row(ln2_g[layer]), _row(ln2_b[layer]),
                    expert_w_gate[layer], expert_b_gate[layer][:, None, :], expert_w_up[layer],
                    expert_b_up[layer][:, None, :], expert_w_down[layer], expert_b_down[layer][:, None, :])

    wi = w_in_even[0]
    wq, wk, wv, wg, wlr, wglu = jnp.split(wi, [256, 512, 1024, 1536, 1552], axis=1)
    w_even = jnp.concatenate([_pad_heads(wq, GLA_HEADS, GLA_DK), _pad_heads(wk, GLA_HEADS, GLA_DK), wv, wg, wglu,
                              jnp.pad(wlr, ((0, 0), (0, LANE - GLA_RANK)))], axis=1).astype(BF16)
    cols_gla = (0, 512, 1024, 1536, 3072)
    col_a, col_gate = 2048, 2560
    h = _inproj(x, w_even)
    wlr_p = jnp.pad(_pad_heads(w_gla_gate_lr[0], GLA_HEADS, GLA_DK), ((0, LANE - GLA_RANK), (0, 0)))
    blr_p = _row(_pad_heads(b_gla_gate[0], GLA_HEADS, GLA_DK))
    nw = _row(gla_norm_w[0])
    conf_args = (conf_conv_w[0], _row(conf_conv_b[0]), _row(conf_ln_g[0]), _row(conf_ln_b[0]))
    s0_p = jnp.zeros((bp, GLA_HEADS, HEAD_W, HEAD_W), F32)
    s0_s = jnp.pad(state_gla[0], ((0, 0), (0, 0), (0, HEAD_W - GLA_DK), (0, 0)))
    oa_p, sg_p = _gla_call("gla", h, cols_gla, (wlr_p, blr_p), nw, s0_p, bp, lp, 0, False)
    oa_s, sg_s = _gla_call("gla", h, cols_gla, (wlr_p, blr_p), nw, s0_s, bs, ls, tp, True)
    ob_p, cc_p = _conf_call(h, col_a, col_gate, jnp.zeros((bp,) + cache_conformer.shape[2:], F32), *conf_args,
                            bp, lp, 0, False)
    ob_s, cc_s = _conf_call(h, col_a, col_gate, cache_conformer[0], *conf_args, bs, ls, tp, True)
    mix_a = jnp.concatenate([oa_p, oa_s], axis=0)
    mix_b = jnp.concatenate([ob_p, ob_s.astype(BF16)], axis=0)
    x = finish_layer(0, x, mix_a, mix_b, w_out_even[0])
    gla_p, gla_s = sg_p[:, :, :GLA_DK, :][None], sg_s[:, :, :GLA_DK, :][None]
    conf_p, conf_s = cc_p[None], cc_s[None]

    lb_cum = jnp.cumsum(jax.nn.softmax(hgrn_lower_bounds.astype(F32), axis=0), axis=0)
    lower_bound = _row((lb_cum - lb_cum[0])[1])
    wo = w_in_odd[0]
    w_odd = jnp.concatenate([wo[:, 2560:3584], wo[:, :2560],
                             jnp.pad(wo[:, 3584:], ((0, 0), (0, LANE - SSM_HEADS)))], axis=1).astype(BF16)
    h = _inproj(x, w_odd)
    cols_hgrn = (1024, 1536, 2048, 2560)
    col_z, col_x, col_dt = 3072, 0, 3584
    nw = _row(hgrn_norm_w[0])
    oc_p, sh_p = _gla_call("hgrn", h, cols_hgrn, lower_bound, nw,
                           jnp.zeros((bp, HGRN_HEADS, HEAD_W, HEAD_W), F32), bp, lp, 0, False)
    oc_s, sh_s = _gla_call("hgrn", h, cols_hgrn, lower_bound, nw, state_hgrn[0], bs, ls, tp, True)

    def pair_states(s):
        return jnp.swapaxes(s, 2, 3).reshape(s.shape[0], SSM_HEADS // 2, HEAD_W, SSM_STATE)

    def unpair_states(s):
        return jnp.swapaxes(s.reshape(s.shape[0], SSM_HEADS, SSM_HEADDIM, SSM_STATE), 2, 3)

    pad8 = lambda v: jnp.pad(v.astype(F32), (0, LANE - SSM_HEADS)).reshape(1, LANE)
    ssd_args = (mamba_conv_w[0], _row(mamba_conv_b[0]), pad8(mamba_dt_bias[0]), pad8(mamba_a_log[0]),
                _row(jnp.repeat(mamba_d[0], SSM_HEADDIM)), _row(mamba_norm_w[0]))
    od_p, cm_p, ss_p = _ssd_call(h, col_z, col_x, col_dt, jnp.zeros((bp,) + cache_mamba_conv.shape[2:], F32),
                                 jnp.zeros((bp, SSM_HEADS // 2, HEAD_W, SSM_STATE), F32), *ssd_args,
                                 bp, lp, 0, False)
    od_s, cm_s, ss_s = _ssd_call(h, col_z, col_x, col_dt, cache_mamba_conv[0], pair_states(state_ssm[0]),
                                 *ssd_args, bs, ls, tp, True)
    mix_a = jnp.concatenate([oc_p, oc_s], axis=0)
    mix_b = jnp.concatenate([od_p, od_s], axis=0)
    x = finish_layer(1, x, mix_a, mix_b, w_out_odd[0])

    y_prompt = x[:tp].reshape(bp, lp, D_MODEL)
    y_sample = x[tp:].reshape(bs, ls, D_MODEL)
    return (y_prompt, y_sample, gla_p, gla_s, conf_p, conf_s, sh_p[None], sh_s[None],
            unpair_states(ss_p)[None], unpair_states(ss_s)[None], cm_p[None], cm_s[None])
```

```python
import functools

import jax
import jax.numpy as jnp
from jax import lax
from jax.experimental import pallas as pl
from jax.experimental.pallas import tpu as pltpu
from jax.experimental.pallas import tpu_sc as plsc

F32 = jnp.float32
BF16 = jnp.bfloat16
I32 = jnp.int32
U32 = jnp.uint32
HIGHEST = lax.Precision.HIGHEST

D_MODEL = 1024
DEPTH = 2
DEEPNORM_ALPHA = (2.0 * DEPTH) ** 0.25
LN_EPS = 1e-5
RMS_EPS = 1e-6
LANE = 128
HEAD_W = 128
GLA_HEADS, GLA_DK, GLA_RANK, GLA_TAU = 4, 64, 16, 16.0
CONF_DIM, CONF_WIDTH = 512, 31
HGRN_HEADS, HGRN_DK = 4, 128
SSM_HEADS, SSM_HEADDIM, SSM_STATE, SSM_GROUPS, SSM_CONV = 8, 64, 128, 2, 4
SSM_INNER = SSM_HEADS * SSM_HEADDIM
SSM_CONV_DIM = SSM_INNER + 2 * SSM_GROUPS * SSM_STATE
N_EXPERTS, TOP_K = 32, 4
SWIGLU_ALPHA, SWIGLU_LIMIT = 1.702, 7.0
SCAN_CHUNK = 64
PROMPT_TILE = 256
SAMPLE_SEQS = 16
TOKEN_TILE = 512
MOE_ROWS = 256
SC_CORES, SC_SUBCORES = 2, 16
SC_ROWS = 64
VMEM_LIMIT = 56 * 1024 * 1024


def _cparams(n_axes):
    return pltpu.CompilerParams(dimension_semantics=("arbitrary",) * n_axes, vmem_limit_bytes=VMEM_LIMIT)


def _silu(x):
    return x * jax.nn.sigmoid(x)


def _softplus(x):
    return jnp.maximum(x, 0.0) + jnp.log(1.0 + jnp.exp(-jnp.abs(x)))


def _log_sigmoid(x):
    return jnp.minimum(x, 0.0) - jnp.log(1.0 + jnp.exp(-jnp.abs(x)))


def _layernorm(y, g, b):
    mu = jnp.mean(y, axis=-1, keepdims=True)
    d = y - mu
    var = jnp.mean(d * d, axis=-1, keepdims=True)
    return d * lax.rsqrt(var + LN_EPS) * g + b


def _dot(a, b):
    return jnp.dot(a.astype(BF16), b.astype(BF16), preferred_element_type=F32)


def _dot_nt(a, b):
    return lax.dot_general(a.astype(BF16), b.astype(BF16), (((1,), (1,)), ((), ())), preferred_element_type=F32)


def _dot_tn(a, b):
    return lax.dot_general(a.astype(BF16), b.astype(BF16), (((0,), (0,)), ((), ())), preferred_element_type=F32)


def _tri(c):
    r = lax.broadcasted_iota(I32, (c, c), 0)
    k = lax.broadcasted_iota(I32, (c, c), 1)
    return r >= k


def _inproj_body(x_ref, w_ref, o_ref):
    xb = x_ref[...].astype(BF16)
    n = w_ref.shape[1]
    for c0 in range(0, n, 512):
        c1 = min(c0 + 512, n)
        o_ref[:, c0:c1] = jnp.dot(xb, w_ref[:, c0:c1], preferred_element_type=F32)


def _inproj(x, w):
    t, k = x.shape
    n = w.shape[1]
    return pl.pallas_call(
        _inproj_body,
        grid=(t // TOKEN_TILE,),
        in_specs=[pl.BlockSpec((TOKEN_TILE, k), lambda i: (i, 0)),
                  pl.BlockSpec((k, n), lambda i: (0, 0))],
        out_specs=pl.BlockSpec((TOKEN_TILE, n), lambda i: (i, 0)),
        out_shape=jax.ShapeDtypeStruct((t, n), F32),
        compiler_params=_cparams(1),
        name="inproj",
    )(x, w)


def _pack_halves(y):
    half = y.shape[1] // 2
    hi = lax.bitcast_convert_type(y[:, :half].astype(BF16).astype(F32), U32)
    lo = lax.bitcast_convert_type(y[:, half:].astype(BF16).astype(F32), U32)
    return (hi & jnp.uint32(0xFFFF0000)) | (lo >> 16)


def _unpack_halves(w):
    hi = lax.bitcast_convert_type(w & jnp.uint32(0xFFFF0000), F32)
    lo = lax.bitcast_convert_type(w << 16, F32)
    return hi, lo


def _outproj_body(a_ref, b_ref, x_ref, wa_ref, wb_ref, g_ref, be_ref, rw_ref, rb_ref,
                  x1_ref, xp_ref, gate_ref, idx_ref, rank_ref, cnt_ref, carry):
    @pl.when(pl.program_id(0) == 0)
    def _():
        carry[...] = jnp.zeros(carry.shape, F32)

    mix = (jnp.dot(a_ref[...], wa_ref[...], preferred_element_type=F32)
           + jnp.dot(b_ref[...], wb_ref[...], preferred_element_type=F32))
    x1 = _layernorm(DEEPNORM_ALPHA * x_ref[...] + mix, g_ref[...], be_ref[...])
    x1_ref[...] = x1
    xp_ref[...] = _pack_halves(x1)
    logits = _dot(x1, rw_ref[...]) + rb_ref[...]
    lane = lax.broadcasted_iota(I32, logits.shape, 1)
    vals, idxs = [], []
    for _ in range(TOP_K):
        m = jnp.max(logits, axis=-1, keepdims=True)
        sel = jnp.min(jnp.where(logits == m, lane, LANE), axis=-1, keepdims=True)
        vals.append(m)
        idxs.append(sel)
        logits = jnp.where(lane == sel, -jnp.inf, logits)
    exps = [jnp.exp(v - vals[0]) for v in vals]
    inv = 1.0 / functools.reduce(lambda p, q: p + q, exps)
    gates = jnp.zeros(logits.shape, F32)
    eidx = jnp.zeros(logits.shape, I32)
    for k in range(TOP_K):
        gates = jnp.where(lane == k, exps[k] * inv, gates)
        eidx = jnp.where(lane == k, idxs[k], eidx)
    gate_ref[...] = gates
    idx_ref[...] = eidx
    tm = logits.shape[0]
    chosen = jnp.zeros(logits.shape, F32)
    for k in range(TOP_K):
        chosen = chosen + (lane == idxs[k]).astype(F32)
    earlier = lax.broadcasted_iota(I32, (tm, tm), 0) > lax.broadcasted_iota(I32, (tm, tm), 1)
    before = carry[...] + jnp.dot(earlier.astype(BF16), chosen.astype(BF16), preferred_element_type=F32)
    ranks = jnp.zeros(logits.shape, F32)
    for k in range(TOP_K):
        rk = jnp.sum(jnp.where(lane == idxs[k], before, 0.0), axis=-1, keepdims=True)
        ranks = jnp.where(lane == k, rk, ranks)
    rank_ref[...] = ranks.astype(I32)
    carry[...] = carry[...] + jnp.sum(chosen, axis=0, keepdims=True)
    cnt_ref[...] = carry[...].astype(I32)


def _outproj_ln_router(a, b, x, wa, wb, g, be, rw, rb):
    t = x.shape[0]
    tm = TOKEN_TILE
    row = lambda i: (i, 0)
    fix = lambda i: (0, 0)
    return pl.pallas_call(
        _outproj_body,
        grid=(t // tm,),
        in_specs=[pl.BlockSpec((tm, 512), row), pl.BlockSpec((tm, 512), row), pl.BlockSpec((tm, D_MODEL), row),
                  pl.BlockSpec((512, D_MODEL), fix), pl.BlockSpec((512, D_MODEL), fix),
                  pl.BlockSpec((1, D_MODEL), fix), pl.BlockSpec((1, D_MODEL), fix),
                  pl.BlockSpec((D_MODEL, LANE), fix), pl.BlockSpec((1, LANE), fix)],
        out_specs=[pl.BlockSpec((tm, D_MODEL), row), pl.BlockSpec((tm, 512), row),
                   pl.BlockSpec((tm, LANE), row), pl.BlockSpec((tm, LANE), row), pl.BlockSpec((tm, LANE), row),
                   pl.BlockSpec((1, LANE), fix)],
        out_shape=[jax.ShapeDtypeStruct((t, D_MODEL), F32), jax.ShapeDtypeStruct((t, 512), U32),
                   jax.ShapeDtypeStruct((t, LANE), F32), jax.ShapeDtypeStruct((t, LANE), I32),
                   jax.ShapeDtypeStruct((t, LANE), I32), jax.ShapeDtypeStruct((1, LANE), I32)],
        scratch_shapes=[pltpu.VMEM((1, LANE), F32)],
        compiler_params=_cparams(1),
        name="outproj_ln_router",
    )(a, b, x, wa, wb, g, be, rw, rb)


def _sc_worker_rows(n):
    per_worker = n // (SC_CORES * SC_SUBCORES)
    worker = lax.axis_index("s") * SC_CORES + lax.axis_index("c")
    return worker * per_worker, per_worker // SC_ROWS


def _sc_mesh():
    return plsc.VectorSubcoreMesh(core_axis_name="c", subcore_axis_name="s")


def _sc_scatter_rows(src, idx, n_out):
    n, (n_src, w) = idx.shape[0], src.shape
    assert n % (SC_CORES * SC_SUBCORES * SC_ROWS) == 0 and n_src % SC_ROWS == 0

    @functools.partial(
        pl.kernel, mesh=_sc_mesh(), out_type=jax.ShapeDtypeStruct((n_out, w), src.dtype),
        scratch_types=[pltpu.VMEM((SC_ROWS,), I32), pltpu.VMEM((SC_ROWS, w), src.dtype), pltpu.SemaphoreType.DMA])
    def scatter(src_hbm, idx_hbm, out_hbm, idx_v, rows_v, sem):
        base, chunks = _sc_worker_rows(n)

        @pl.loop(0, chunks)
        def _(c):
            off = pl.multiple_of(base + c * SC_ROWS, 8)
            pltpu.sync_copy(idx_hbm.at[pl.ds(off, SC_ROWS)], idx_v)
            pltpu.sync_copy(src_hbm.at[pl.ds(pl.multiple_of(lax.rem(off, n_src), 8), SC_ROWS)], rows_v)
            pltpu.async_copy(rows_v, out_hbm.at[idx_v], sem).wait()

    return scatter(src, idx)


def _sc_gather_rows(table, idx):
    n, w = idx.shape[0], table.shape[1]
    assert n % (SC_CORES * SC_SUBCORES * SC_ROWS) == 0

    @functools.partial(
        pl.kernel, mesh=_sc_mesh(), out_type=jax.ShapeDtypeStruct((n, w), table.dtype),
        scratch_types=[pltpu.VMEM((SC_ROWS,), I32), pltpu.VMEM((SC_ROWS, w), table.dtype), pltpu.SemaphoreType.DMA])
    def gather(table_hbm, idx_hbm, out_hbm, idx_v, rows_v, sem):
        base, chunks = _sc_worker_rows(n)

        @pl.loop(0, chunks)
        def _(c):
            off = pl.multiple_of(base + c * SC_ROWS, 8)
            pltpu.sync_copy(idx_hbm.at[pl.ds(off, SC_ROWS)], idx_v)
            pltpu.async_copy(table_hbm.at[idx_v], rows_v, sem).wait()
            pltpu.sync_copy(rows_v, out_hbm.at[pl.ds(off, SC_ROWS)])

    return gather(table, idx)


def _experts_body(be_ref, nu_ref, xs_ref, wg_ref, bg_ref, wu_ref, bu_ref, wd_ref, bd_ref, o_ref, wg_s, wu_s, wd_s):
    i = pl.program_id(0)
    prev = be_ref[jnp.maximum(i - 1, 0)]

    @pl.when((i == 0) | (be_ref[i] != prev))
    def _():
        wg_s[...] = wg_ref[...].astype(BF16)
        wu_s[...] = wu_ref[...].astype(BF16)
        wd_s[...] = wd_ref[...].astype(BF16)

    @pl.when(i < nu_ref[0])
    def _():
        half = D_MODEL // 2
        x_hi, x_lo = _unpack_halves(xs_ref[...])
        x_hi = x_hi.astype(BF16)
        x_lo = x_lo.astype(BF16)
        g = (jnp.dot(x_hi, wg_s[:half, :], preferred_element_type=F32)
             + jnp.dot(x_lo, wg_s[half:, :], preferred_element_type=F32) + bg_ref[...])
        u = (jnp.dot(x_hi, wu_s[:half, :], preferred_element_type=F32)
             + jnp.dot(x_lo, wu_s[half:, :], preferred_element_type=F32) + bu_ref[...])
        g = jnp.minimum(g, SWIGLU_LIMIT)
        u = jnp.clip(u, -SWIGLU_LIMIT, SWIGLU_LIMIT)
        hmid = (u + 1.0) * (g * jax.nn.sigmoid(SWIGLU_ALPHA * g))
        out = jnp.dot(hmid.astype(BF16), wd_s[...], preferred_element_type=F32) + bd_ref[...]
        o_ref[...] = _pack_halves(out)

    @pl.when(i >= nu_ref[0])
    def _():
        o_ref[...] = jnp.zeros(o_ref.shape, o_ref.dtype)


def _experts(layer, block_e, n_used, xs, wg, bg, wu, bu, wd, bd):
    n_rows = xs.shape[0]
    bm = MOE_ROWS
    row = lambda i, be, nu: (i, 0)
    wsel = lambda i, be, nu: (layer, be[i], 0, 0)
    wspec = pl.BlockSpec((None, None, D_MODEL, D_MODEL), wsel)
    bspec = pl.BlockSpec((None, None, 1, D_MODEL), wsel)
    bias = lambda b: b.reshape(b.shape[0], b.shape[1], 1, b.shape[2])
    return pl.pallas_call(
        _experts_body,
        grid_spec=pltpu.PrefetchScalarGridSpec(
            num_scalar_prefetch=2,
            grid=(n_rows // bm,),
            in_specs=[pl.BlockSpec((bm, 512), row), wspec, bspec, wspec, bspec, wspec, bspec],
            out_specs=pl.BlockSpec((bm, 512), row),
            scratch_shapes=[pltpu.VMEM((D_MODEL, D_MODEL), BF16)] * 3,
        ),
        out_shape=jax.ShapeDtypeStruct((n_rows, 512), U32),
        compiler_params=_cparams(1),
        name="experts",
    )(block_e, n_used, xs, wg, bias(bg), wu, bias(bu), wd, bias(bd))


def _combine_body(o0_ref, o1_ref, o2_ref, o3_ref, gt_ref, x_ref, g_ref, b_ref, y_ref):
    half = D_MODEL // 2
    gates = gt_ref[...]
    hi = jnp.zeros((x_ref.shape[0], half), F32)
    lo = jnp.zeros((x_ref.shape[0], half), F32)
    for k, o_ref in enumerate((o0_ref, o1_ref, o2_ref, o3_ref)):
        h, l = _unpack_halves(o_ref[...])
        gk = gates[:, k:k + 1]
        hi = hi + gk * h
        lo = lo + gk * l
    x = x_ref[...]
    y_hi = DEEPNORM_ALPHA * x[:, :half] + hi
    y_lo = DEEPNORM_ALPHA * x[:, half:] + lo
    mu = (jnp.sum(y_hi, axis=-1, keepdims=True) + jnp.sum(y_lo, axis=-1, keepdims=True)) * (1.0 / D_MODEL)
    d_hi = y_hi - mu
    d_lo = y_lo - mu
    var = (jnp.sum(d_hi * d_hi, axis=-1, keepdims=True) + jnp.sum(d_lo * d_lo, axis=-1, keepdims=True)) * (1.0 / D_MODEL)
    r = lax.rsqrt(var + LN_EPS)
    y_ref[:, :half] = d_hi * r * g_ref[:, :half] + b_ref[:, :half]
    y_ref[:, half:] = d_lo * r * g_ref[:, half:] + b_ref[:, half:]


def _combine_ln(o4, gates, x, g, b):
    t = x.shape[0]
    tm = TOKEN_TILE
    row = lambda i: (i, 0)
    fix = lambda i: (0, 0)
    choice = lambda k: pl.BlockSpec((tm, 512), lambda i: (k * (t // tm) + i, 0))
    return pl.pallas_call(
        _combine_body,
        grid=(t // tm,),
        in_specs=[choice(0), choice(1), choice(2), choice(3), pl.BlockSpec((tm, LANE), row),
                  pl.BlockSpec((tm, D_MODEL), row), pl.BlockSpec((1, D_MODEL), fix), pl.BlockSpec((1, D_MODEL), fix)],
        out_specs=pl.BlockSpec((tm, D_MODEL), row),
        out_shape=jax.ShapeDtypeStruct((t, D_MODEL), F32),
        compiler_params=_cparams(1),
        name="combine_ln",
    )(o4, o4, o4, o4, gates, x, g, b)


def _moe(layer, x1, xp, gates, eidx, rank, counts, ln_g, ln_b, wg, bg, wu, bu, wd, bd):
    t = x1.shape[0]
    bm = MOE_ROWS
    n_blocks = t * TOP_K // bm + N_EXPERTS
    n_rows = n_blocks * bm
    cnt = counts[0, :N_EXPERTS]
    padded = (cnt + bm - 1) // bm * bm
    pad_end = jnp.cumsum(padded)
    pad_start = pad_end - padded
    e = eidx[:, :TOP_K]
    start = jnp.sum(jnp.where(e[:, :, None] == jnp.arange(N_EXPERTS, dtype=I32), pad_start, 0), axis=-1)
    dest = (start + rank[:, :TOP_K]).T.reshape(-1)
    n_used = (pad_end[-1] // bm).astype(I32)
    blk = jnp.arange(n_blocks, dtype=I32)
    block_e = jnp.minimum(jnp.searchsorted(pad_end, jnp.minimum(blk, n_used - 1) * bm, side='right'),
                          N_EXPERTS - 1).astype(I32)
    xs = _sc_scatter_rows(xp, dest, n_rows)
    outs = _experts(layer, block_e, n_used.reshape(1), xs, wg, bg, wu, bu, wd, bd)
    o4 = _sc_gather_rows(outs, dest)
    return _combine_ln(o4, gates, x1, ln_g, ln_b)


def _gla_body(mode, n_seq, n_chunk, C, *refs):
    if mode == "gla":
        hq_ref, hk_ref, hv_ref, hg_ref, hlr_ref, wlr_ref, blr_ref, nw_ref, s0_ref, o_ref, so_ref, st, o_scr = refs
    else:
        hq_ref, hk_ref, hv_ref, hg_ref, lb_ref, nw_ref, s0_ref, o_ref, so_ref, st, o_scr = refs
    n_heads = 4
    tstep = pl.program_id(1)

    @pl.when(tstep == 0)
    def _():
        for s in range(n_seq):
            for h in range(n_heads):
                st[s, h] = s0_ref[s, h].T

    if mode == "gla":
        q = hq_ref[...] * (GLA_DK ** -0.5)
        k = hk_ref[...]
        z = _dot(hlr_ref[...], wlr_ref[...]) + blr_ref[...]
        g = _log_sigmoid(z) * (1.0 / GLA_TAU)
    else:
        q = _silu(hq_ref[...]) * (HGRN_DK ** -0.5)
        lb = lb_ref[...]
        f = lb + (1.0 - lb) * jax.nn.sigmoid(hk_ref[...])
        k = 1.0 - f
        g = jnp.log(f)
    v = hv_ref[...]
    causal = _tri(C)
    tri = causal.astype(F32)
    mid = max(C // 2 - 1, 0)
    for s in range(n_seq):
        for c in range(n_chunk):
            r0 = (s * n_chunk + c) * C
            for h in range(n_heads):
                cs = slice(h * HEAD_W, (h + 1) * HEAD_W)
                qh, kh, vh, gh = q[r0:r0 + C, cs], k[r0:r0 + C, cs], v[r0:r0 + C, cs], g[r0:r0 + C, cs]
                b = jnp.dot(tri, gh, preferred_element_type=F32, precision=HIGHEST)
                b_last = b[C - 1:C, :]
                b_mid = b[mid:mid + 1, :]
                scores = lax.dot_general(qh * jnp.exp(b - b_mid), kh * jnp.exp(b_mid - b), (((1,), (1,)), ((), ())),
                                         preferred_element_type=F32, precision=HIGHEST)
                scores = jnp.where(causal, scores, 0.0)
                s_t = st[s, h]
                o = _dot(scores, vh) + _dot_nt(qh * jnp.exp(b), s_t)
                st[s, h] = s_t * jnp.exp(b_last) + _dot_tn(vh, kh * jnp.exp(b_last - b))
                ms = jnp.mean(o * o, axis=-1, keepdims=True)
                o_scr[r0:r0 + C, cs] = o * lax.rsqrt(ms + RMS_EPS)
    o_ref[...] = (o_scr[...] * nw_ref[...] * _silu(hg_ref[...])).astype(BF16)

    @pl.when(tstep == pl.num_programs(1) - 1)
    def _():
        for s in range(n_seq):
            for h in range(n_heads):
                so_ref[s, h] = st[s, h].T


def _seq_layout(n_batch, seq_len, row_off, sample):
    if sample:
        n_seq, n_chunk, C = SAMPLE_SEQS, 1, seq_len
        rows = n_seq * C
        grid = (n_batch // n_seq, 1)
        blk0 = row_off // rows
        rb = lambda i, t: blk0 + i
    else:
        n_seq, n_chunk, C = 1, PROMPT_TILE // SCAN_CHUNK, SCAN_CHUNK
        rows = PROMPT_TILE
        tiles = seq_len // rows
        grid = (n_batch, tiles)
        blk0 = row_off // rows
        rb = lambda i, t: blk0 + i * tiles + t
    return n_seq, n_chunk, C, rows, grid, rb


def _gla_call(mode, h, cols, extra, nw, s0, n_batch, seq_len, row_off, sample):
    n_seq, n_chunk, C, rows, grid, rb = _seq_layout(n_batch, seq_len, row_off, sample)
    colspec = lambda c0, w: pl.BlockSpec((rows, w), lambda i, t: (rb(i, t), c0 // w))
    fix2 = lambda i, t: (0, 0)
    in_specs = [colspec(cols[0], 512), colspec(cols[1], 512), colspec(cols[2], 512), colspec(cols[3], 512)]
    args = [h, h, h, h]
    if mode == "gla":
        wlr, blr = extra
        in_specs += [colspec(cols[4], LANE), pl.BlockSpec((LANE, 512), fix2), pl.BlockSpec((1, 512), fix2)]
        args += [h, wlr, blr]
    else:
        in_specs += [pl.BlockSpec((1, 512), fix2)]
        args += [extra]
    st_spec = pl.BlockSpec((n_seq, 4, HEAD_W, HEAD_W), lambda i, t: (i, 0, 0, 0))
    in_specs += [pl.BlockSpec((1, 512), fix2), st_spec]
    args += [nw, s0]
    t_rows = n_batch * seq_len
    out_blk0 = 0
    o_spec = pl.BlockSpec((rows, 512), lambda i, t: (rb(i, t) - row_off // rows, 0))
    return pl.pallas_call(
        functools.partial(_gla_body, mode, n_seq, n_chunk, C),
        grid=grid,
        in_specs=in_specs,
        out_specs=[o_spec, st_spec],
        out_shape=[jax.ShapeDtypeStruct((t_rows, 512), BF16),
                   jax.ShapeDtypeStruct((n_batch, 4, HEAD_W, HEAD_W), F32)],
        scratch_shapes=[pltpu.VMEM((n_seq, 4, HEAD_W, HEAD_W), F32), pltpu.VMEM((rows, 512), F32)],
        compiler_params=_cparams(2),
        name=mode + ("_sample" if sample else "_prompt"),
    )(*args)


def _round_bf16(x, on=True):
    return x.astype(BF16).astype(F32) if on else x


def _conf_body(n_seq, L, round_x, round_w, a_ref, gt_ref, hist_ref, w_ref, b_ref, g_ref, be_ref, o_ref, co_ref, buf, bufr):
    tstep = pl.program_id(1)
    hist = CONF_WIDTH - 1
    pad = 32 - hist

    @pl.when(tstep == 0)
    def _():
        for s in range(n_seq):
            buf[s, pad:32, :] = hist_ref[s]
            bufr[s, pad:32, :] = _round_bf16(hist_ref[s], round_x)

    u = a_ref[...] * jax.nn.sigmoid(gt_ref[...])
    ur = _round_bf16(u, round_x)
    for s in range(n_seq):
        buf[s, 32:32 + L, :] = u[s * L:(s + 1) * L, :]
        bufr[s, 32:32 + L, :] = ur[s * L:(s + 1) * L, :]
    w = _round_bf16(w_ref[...], round_w)
    for s in range(n_seq):
        acc = jnp.zeros((L, CONF_DIM), F32)
        for j in range(CONF_WIDTH):
            acc = acc + bufr[s, pad + j:pad + j + L, :] * w[j:j + 1, :]
        y = _silu(_layernorm(acc + b_ref[...], g_ref[...], be_ref[...]))
        o_ref[s * L:(s + 1) * L, :] = y.astype(o_ref.dtype)
        tail = buf[s, L + pad:L + 32, :]
        buf[s, pad:32, :] = tail
        tailr = bufr[s, L + pad:L + 32, :]
        bufr[s, pad:32, :] = tailr

    @pl.when(tstep == pl.num_programs(1) - 1)
    def _():
        for s in range(n_seq):
            co_ref[s] = buf[s, pad:32, :]


def _conf_call(h, col_a, col_g, cache, w, b, g, be, n_batch, seq_len, row_off, sample):
    n_seq, n_chunk, C, rows, grid, rb = _seq_layout(n_batch, seq_len, row_off, sample)
    L = rows // n_seq
    hist = CONF_WIDTH - 1
    colspec = lambda c0: pl.BlockSpec((rows, 512), lambda i, t: (rb(i, t), c0 // 512))
    fix2 = lambda i, t: (0, 0)
    c_spec = pl.BlockSpec((n_seq, hist, CONF_DIM), lambda i, t: (i, 0, 0))
    o_dtype = F32 if sample else BF16
    return pl.pallas_call(
        functools.partial(_conf_body, n_seq, L, True, sample),
        grid=grid,
        in_specs=[colspec(col_a), colspec(col_g), c_spec,
                  pl.BlockSpec((CONF_WIDTH, CONF_DIM), fix2), pl.BlockSpec((1, CONF_DIM), fix2),
                  pl.BlockSpec((1, CONF_DIM), fix2), pl.BlockSpec((1, CONF_DIM), fix2)],
        out_specs=[pl.BlockSpec((rows, 512), lambda i, t: (rb(i, t) - row_off // rows, 0)), c_spec],
        out_shape=[jax.ShapeDtypeStruct((n_batch * seq_len, 512), o_dtype),
                   jax.ShapeDtypeStruct((n_batch, hist, CONF_DIM), F32)],
        scratch_shapes=[pltpu.VMEM((n_seq, 32 + L, CONF_DIM), F32)] * 2,
        compiler_params=_cparams(2),
        name="conformer" + ("_sample" if sample else "_prompt"),
    )(h, h, cache, w, b, g, be)


def _ssd_body(n_seq, n_chunk, C, round_x, round_w, hz_ref, hx_ref, hdt_ref, hist_ref, s0_ref, cw_ref, cb_ref, dtb_ref, alog_ref,
              dvec_ref, nw_ref, o_ref, co_ref, so_ref, st, buf, bufr, xbc, y_scr):
    tstep = pl.program_id(1)
    L = n_chunk * C
    hist = SSM_CONV - 1
    pad = 8 - hist
    n_pairs = SSM_HEADS // 2

    @pl.when(tstep == 0)
    def _():
        for s in range(n_seq):
            buf[s, pad:8, :] = hist_ref[s]
            bufr[s, pad:8, :] = _round_bf16(hist_ref[s], round_x)
            for m in range(n_pairs):
                st[s, m] = s0_ref[s, m]

    cw = _round_bf16(cw_ref[...], round_w)
    for s in range(n_seq):
        hx = hx_ref[s * L:(s + 1) * L, :]
        buf[s, 8:8 + L, :] = hx
        bufr[s, 8:8 + L, :] = _round_bf16(hx, round_x)
        acc = jnp.zeros((L, SSM_CONV_DIM), F32)
        for j in range(SSM_CONV):
            acc = acc + bufr[s, pad + j:pad + j + L, :] * cw[j:j + 1, :]
        xbc[s * L:(s + 1) * L, :] = _silu(acc + cb_ref[...])
        tail = buf[s, L + pad:L + 8, :]
        buf[s, pad:8, :] = tail
        tailr = bufr[s, L + pad:L + 8, :]
        bufr[s, pad:8, :] = tailr

    dt = _softplus(hdt_ref[...] + dtb_ref[...])
    la = dt * (-jnp.exp(alog_ref[...]))
    hrow = lax.broadcasted_iota(I32, (LANE, SSM_INNER), 0)
    hcol = lax.broadcasted_iota(I32, (LANE, SSM_INNER), 1) // SSM_HEADDIM
    expand = (hrow == hcol).astype(F32)
    dtx = jnp.dot(dt, expand, preferred_element_type=F32, precision=HIGHEST)
    causal = _tri(C)
    tri = causal.astype(F32)
    lane = lax.broadcasted_iota(I32, (C, HEAD_W), 1)
    for s in range(n_seq):
        for c in range(n_chunk):
            r0 = (s * n_chunk + c) * C
            rs = slice(r0, r0 + C)
            la_c = la[rs, :]
            bcol = jnp.dot(tri, la_c, preferred_element_type=F32, precision=HIGHEST)
            brow = lax.dot_general(la_c, tri, (((0,), (1,)), ((), ())), preferred_element_type=F32,
                                   precision=HIGHEST)
            xs_c = xbc[rs, 0:SSM_INNER]
            v_c = xs_c * dtx[rs, :]
            for m in range(n_pairs):
                grp = (2 * m) // (SSM_HEADS // SSM_GROUPS)
                bm = xbc[rs, SSM_INNER + grp * SSM_STATE:SSM_INNER + (grp + 1) * SSM_STATE]
                cm = xbc[rs, SSM_INNER + (SSM_GROUPS + grp) * SSM_STATE:SSM_INNER + (SSM_GROUPS + grp + 1) * SSM_STATE]
                gmat = lax.dot_general(cm, bm, (((1,), (1,)), ((), ())), preferred_element_type=F32,
                                       precision=HIGHEST)
                ps = slice(m * HEAD_W, (m + 1) * HEAD_W)
                vp = v_c[:, ps]
                s_t = st[s, m]
                o_halves = []
                for hh in range(2):
                    hd = 2 * m + hh
                    bc = bcol[:, hd:hd + 1]
                    br = brow[hd:hd + 1, :]
                    dec = jnp.where(causal, jnp.exp(jnp.minimum(bc - br, 0.0)), 0.0)
                    b_last = bcol[C - 1:C, hd:hd + 1]
                    o_halves.append(_dot(gmat * dec, vp) + _dot_nt(cm * jnp.exp(bc), s_t))
                    kv = _dot_tn(vp, bm * jnp.exp(b_last - bc))
                    vs = slice(hh * SSM_HEADDIM, (hh + 1) * SSM_HEADDIM)
                    st[s, m, vs, :] = s_t[vs, :] * jnp.exp(b_last) + kv[vs, :]
                o_pair = jnp.where(lane < SSM_HEADDIM, o_halves[0], o_halves[1])
                y_scr[rs, ps] = o_pair + dvec_ref[:, ps] * xs_c[:, ps]
    y = y_scr[...] * _silu(hz_ref[...])
    gw = SSM_INNER // SSM_GROUPS
    for grp in range(SSM_GROUPS):
        gs = slice(grp * gw, (grp + 1) * gw)
        yg = y[:, gs]
        ms = jnp.mean(yg * yg, axis=-1, keepdims=True)
        o_ref[:, gs] = (yg * lax.rsqrt(ms + RMS_EPS) * nw_ref[:, gs]).astype(BF16)

    @pl.when(tstep == pl.num_programs(1) - 1)
    def _():
        for s in range(n_seq):
            co_ref[s] = buf[s, pad:8, :]
            for m in range(n_pairs):
                so_ref[s, m] = st[s, m]


def _ssd_call(h, col_z, col_x, col_dt, cache, s0, cw, cb, dtb, alog, dvec, nw, n_batch, seq_len, row_off, sample):
    n_seq, n_chunk, C, rows, grid, rb = _seq_layout(n_batch, seq_len, row_off, sample)
    L = rows // n_seq
    hist = SSM_CONV - 1
    n_pairs = SSM_HEADS // 2
    colspec = lambda c0, w: pl.BlockSpec((rows, w), lambda i, t: (rb(i, t), c0 // w))
    fix2 = lambda i, t: (0, 0)
    c_spec = pl.BlockSpec((n_seq, hist, SSM_CONV_DIM), lambda i, t: (i, 0, 0))
    st_spec = pl.BlockSpec((n_seq, n_pairs, HEAD_W, SSM_STATE), lambda i, t: (i, 0, 0, 0))
    return pl.pallas_call(
        functools.partial(_ssd_body, n_seq, n_chunk, C, sample, True),
        grid=grid,
        in_specs=[colspec(col_z, 512), colspec(col_x, SSM_CONV_DIM), colspec(col_dt, LANE), c_spec, st_spec,
                  pl.BlockSpec((SSM_CONV, SSM_CONV_DIM), fix2), pl.BlockSpec((1, SSM_CONV_DIM), fix2),
                  pl.BlockSpec((1, LANE), fix2), pl.BlockSpec((1, LANE), fix2),
                  pl.BlockSpec((1, SSM_INNER), fix2), pl.BlockSpec((1, SSM_INNER), fix2)],
        out_specs=[pl.BlockSpec((rows, 512), lambda i, t: (rb(i, t) - row_off // rows, 0)), c_spec, st_spec],
        out_shape=[jax.ShapeDtypeStruct((n_batch * seq_len, 512), BF16),
                   jax.ShapeDtypeStruct((n_batch, hist, SSM_CONV_DIM), F32),
                   jax.ShapeDtypeStruct((n_batch, n_pairs, HEAD_W, SSM_STATE), F32)],
        scratch_shapes=[pltpu.VMEM((n_seq, n_pairs, HEAD_W, SSM_STATE), F32),
                        pltpu.VMEM((n_seq, 8 + L, SSM_CONV_DIM), F32),
                        pltpu.VMEM((n_seq, 8 + L, SSM_CONV_DIM), F32),
                        pltpu.VMEM((rows, SSM_CONV_DIM), F32),
                        pltpu.VMEM((rows, SSM_INNER), F32)],
        compiler_params=_cparams(2),
        name="ssd" + ("_sample" if sample else "_prompt"),
    )(h, h, h, cache, s0, cw, cb, dtb, alog, dvec, nw)


def _pad_heads(w, n_heads, width):
    lead = w.shape[:-1]
    w = w.reshape(lead + (n_heads, width))
    w = jnp.pad(w, [(0, 0)] * len(lead) + [(0, 0), (0, HEAD_W - width)])
    return w.reshape(lead + (n_heads * HEAD_W,))


def _row(v):
    return v.reshape(1, -1).astype(F32)


def kernel(x_prompt, x_sample, state_gla, cache_conformer, state_hgrn, state_ssm, cache_mamba_conv, w_in_even, w_gla_gate_lr, b_gla_gate, gla_norm_w, conf_conv_w, conf_conv_b, conf_ln_g, conf_ln_b, w_out_even, w_in_odd, hgrn_lower_bounds, hgrn_norm_w, mamba_conv_w, mamba_conv_b, mamba_dt_bias, mamba_a_log, mamba_d, mamba_norm_w, w_out_odd, ln1_g, ln1_b, ln2_g, ln2_b, router_w, router_b, expert_w_gate, expert_b_gate, expert_w_up, expert_b_up, expert_w_down, expert_b_down):
    bp, lp, _ = x_prompt.shape
    bs, ls, _ = x_sample.shape
    tp, ts = bp * lp, bs * ls
    x = jnp.concatenate([x_prompt.reshape(tp, D_MODEL), x_sample.reshape(ts, D_MODEL)], axis=0)

    def router_params(layer):
        rw = jnp.pad(router_w[layer], ((0, 0), (0, LANE - N_EXPERTS)))
        rb = jnp.pad(router_b[layer].astype(F32), (0, LANE - N_EXPERTS), constant_values=-1e30)
        return rw, rb.reshape(1, LANE)

    def finish_layer(layer, x, mix_a, mix_b, w_out):
        rw, rb = router_params(layer)
        x1, xp, gates, eidx, rank, counts = _outproj_ln_router(
            mix_a, mix_b, x, w_out[:512].astype(BF16), w_out[512:].astype(BF16),
            _row(ln1_g[layer]), _row(ln1_b[layer]), rw, rb)
        return _moe(layer, x1, xp, gates, eidx, rank, counts, _row(ln2_g[layer]), _row(ln2_b[layer]),
                    expert_w_gate, expert_b_gate, expert_w_up, expert_b_up, expert_w_down, expert_b_down)

    wi = w_in_even[0]
    wq, wk, wv, wg, wlr, wglu = jnp.split(wi, [256, 512, 1024, 1536, 1552], axis=1)
    w_even = jnp.concatenate([_pad_heads(wq, GLA_HEADS, GLA_DK), _pad_heads(wk, GLA_HEADS, GLA_DK), wv, wg, wglu,
                              jnp.pad(wlr, ((0, 0), (0, LANE - GLA_RANK)))], axis=1).astype(BF16)
    cols_gla = (0, 512, 1024, 1536, 3072)
    col_a, col_gate = 2048, 2560
    h = _inproj(x, w_even)
    wlr_p = jnp.pad(_pad_heads(w_gla_gate_lr[0], GLA_HEADS, GLA_DK), ((0, LANE - GLA_RANK), (0, 0)))
    blr_p = _row(_pad_heads(b_gla_gate[0], GLA_HEADS, GLA_DK))
    nw = _row(gla_norm_w[0])
    conf_args = (conf_conv_w[0], _row(conf_conv_b[0]), _row(conf_ln_g[0]), _row(conf_ln_b[0]))
    s0_p = jnp.zeros((bp, GLA_HEADS, HEAD_W, HEAD_W), F32)
    s0_s = jnp.pad(state_gla[0], ((0, 0), (0, 0), (0, HEAD_W - GLA_DK), (0, 0)))
    oa_p, sg_p = _gla_call("gla", h, cols_gla, (wlr_p, blr_p), nw, s0_p, bp, lp, 0, False)
    oa_s, sg_s = _gla_call("gla", h, cols_gla, (wlr_p, blr_p), nw, s0_s, bs, ls, tp, True)
    ob_p, cc_p = _conf_call(h, col_a, col_gate, jnp.zeros((bp,) + cache_conformer.shape[2:], F32), *conf_args,
                            bp, lp, 0, False)
    ob_s, cc_s = _conf_call(h, col_a, col_gate, cache_conformer[0], *conf_args, bs, ls, tp, True)
    mix_a = jnp.concatenate([oa_p, oa_s], axis=0)
    mix_b = jnp.concatenate([ob_p, ob_s.astype(BF16)], axis=0)
    x = finish_layer(0, x, mix_a, mix_b, w_out_even[0])
    gla_p, gla_s = sg_p[:, :, :GLA_DK, :][None], sg_s[:, :, :GLA_DK, :][None]
    conf_p, conf_s = cc_p[None], cc_s[None]

    lb_cum = jnp.cumsum(jax.nn.softmax(hgrn_lower_bounds.astype(F32), axis=0), axis=0)
    lower_bound = _row((lb_cum - lb_cum[0])[1])
    wo = w_in_odd[0]
    w_odd = jnp.concatenate([wo[:, 2560:3584], wo[:, :2560],
                             jnp.pad(wo[:, 3584:], ((0, 0), (0, LANE - SSM_HEADS)))], axis=1).astype(BF16)
    h = _inproj(x, w_odd)
    cols_hgrn = (1024, 1536, 2048, 2560)
    col_z, col_x, col_dt = 3072, 0, 3584
    nw = _row(hgrn_norm_w[0])
    oc_p, sh_p = _gla_call("hgrn", h, cols_hgrn, lower_bound, nw,
                           jnp.zeros((bp, HGRN_HEADS, HEAD_W, HEAD_W), F32), bp, lp, 0, False)
    oc_s, sh_s = _gla_call("hgrn", h, cols_hgrn, lower_bound, nw, state_hgrn[0], bs, ls, tp, True)

    def pair_states(s):
        return jnp.swapaxes(s, 2, 3).reshape(s.shape[0], SSM_HEADS // 2, HEAD_W, SSM_STATE)

    def unpair_states(s):
        return jnp.swapaxes(s.reshape(s.shape[0], SSM_HEADS, SSM_HEADDIM, SSM_STATE), 2, 3)

    pad8 = lambda v: jnp.pad(v.astype(F32), (0, LANE - SSM_HEADS)).reshape(1, LANE)
    ssd_args = (mamba_conv_w[0], _row(mamba_conv_b[0]), pad8(mamba_dt_bias[0]), pad8(mamba_a_log[0]),
                _row(jnp.repeat(mamba_d[0], SSM_HEADDIM)), _row(mamba_norm_w[0]))
    od_p, cm_p, ss_p = _ssd_call(h, col_z, col_x, col_dt, jnp.zeros((bp,) + cache_mamba_conv.shape[2:], F32),
                                 jnp.zeros((bp, SSM_HEADS // 2, HEAD_W, SSM_STATE), F32), *ssd_args,
                                 bp, lp, 0, False)
    od_s, cm_s, ss_s = _ssd_call(h, col_z, col_x, col_dt, cache_mamba_conv[0], pair_states(state_ssm[0]),
                                 *ssd_args, bs, ls, tp, True)
    mix_a = jnp.concatenate([oc_p, oc_s], axis=0)
    mix_b = jnp.concatenate([od_p, od_s], axis=0)
    x = finish_layer(1, x, mix_a, mix_b, w_out_odd[0])

    y_prompt = x[:tp].reshape(bp, lp, D_MODEL)
    y_sample = x[tp:].reshape(bs, ls, D_MODEL)
    return (y_prompt, y_sample, gla_p, gla_s, conf_p, conf_s, sh_p[None], sh_s[None],
            unpair_states(ss_p)[None], unpair_states(ss_s)[None], cm_p[None], cm_s[None])
```

```python
import functools

import jax
import jax.numpy as jnp
from jax import lax
from jax.experimental import pallas as pl
from jax.experimental.pallas import tpu as pltpu
from jax.experimental.pallas import tpu_sc as plsc

F32 = jnp.float32
BF16 = jnp.bfloat16
I32 = jnp.int32
U32 = jnp.uint32
HIGHEST = lax.Precision.HIGHEST

D_MODEL = 1024
DEPTH = 2
DEEPNORM_ALPHA = (2.0 * DEPTH) ** 0.25
LN_EPS = 1e-5
RMS_EPS = 1e-6
LANE = 128
HEAD_W = 128
GLA_HEADS, GLA_DK, GLA_RANK, GLA_TAU = 4, 64, 16, 16.0
CONF_DIM, CONF_WIDTH = 512, 31
HGRN_HEADS, HGRN_DK = 4, 128
SSM_HEADS, SSM_HEADDIM, SSM_STATE, SSM_GROUPS, SSM_CONV = 8, 64, 128, 2, 4
SSM_INNER = SSM_HEADS * SSM_HEADDIM
SSM_CONV_DIM = SSM_INNER + 2 * SSM_GROUPS * SSM_STATE
N_EXPERTS, TOP_K = 32, 4
SWIGLU_ALPHA, SWIGLU_LIMIT = 1.702, 7.0
SCAN_CHUNK = 64
PROMPT_TILE = 256
SAMPLE_SEQS = 16
TOKEN_TILE = 512
MOE_ROWS = 256
SC_CORES, SC_SUBCORES = 2, 16
SC_ROWS = 64
VMEM_LIMIT = 56 * 1024 * 1024


def _cparams(n_axes):
    return pltpu.CompilerParams(dimension_semantics=("arbitrary",) * n_axes, vmem_limit_bytes=VMEM_LIMIT)


def _silu(x):
    return x * jax.nn.sigmoid(x)


def _softplus(x):
    return jnp.maximum(x, 0.0) + jnp.log(1.0 + jnp.exp(-jnp.abs(x)))


def _log_sigmoid(x):
    return jnp.minimum(x, 0.0) - jnp.log(1.0 + jnp.exp(-jnp.abs(x)))


def _layernorm(y, g, b):
    mu = jnp.mean(y, axis=-1, keepdims=True)
    d = y - mu
    var = jnp.mean(d * d, axis=-1, keepdims=True)
    return d * lax.rsqrt(var + LN_EPS) * g + b


def _dot(a, b):
    return jnp.dot(a.astype(BF16), b.astype(BF16), preferred_element_type=F32)


def _dot_nt(a, b):
    return lax.dot_general(a.astype(BF16), b.astype(BF16), (((1,), (1,)), ((), ())), preferred_element_type=F32)


def _dot_tn(a, b):
    return lax.dot_general(a.astype(BF16), b.astype(BF16), (((0,), (0,)), ((), ())), preferred_element_type=F32)


def _tri(c):
    r = lax.broadcasted_iota(I32, (c, c), 0)
    k = lax.broadcasted_iota(I32, (c, c), 1)
    return r >= k


def _inproj_body(x_ref, w_ref, o_ref):
    xb = x_ref[...].astype(BF16)
    n = w_ref.shape[1]
    for c0 in range(0, n, 512):
        c1 = min(c0 + 512, n)
        o_ref[:, c0:c1] = jnp.dot(xb, w_ref[:, c0:c1], preferred_element_type=F32)


def _inproj(x, w):
    t, k = x.shape
    n = w.shape[1]
    return pl.pallas_call(
        _inproj_body,
        grid=(t // TOKEN_TILE,),
        in_specs=[pl.BlockSpec((TOKEN_TILE, k), lambda i: (i, 0)),
                  pl.BlockSpec((k, n), lambda i: (0, 0))],
        out_specs=pl.BlockSpec((TOKEN_TILE, n), lambda i: (i, 0)),
        out_shape=jax.ShapeDtypeStruct((t, n), F32),
        compiler_params=_cparams(1),
        name="inproj",
    )(x, w)


def _pack_halves(y):
    half = y.shape[1] // 2
    hi = lax.bitcast_convert_type(y[:, :half].astype(BF16).astype(F32), U32)
    lo = lax.bitcast_convert_type(y[:, half:].astype(BF16).astype(F32), U32)
    return (hi & jnp.uint32(0xFFFF0000)) | (lo >> 16)


def _unpack_halves(w):
    hi = lax.bitcast_convert_type(w & jnp.uint32(0xFFFF0000), F32)
    lo = lax.bitcast_convert_type(w << 16, F32)
    return hi, lo


def _outproj_body(a_ref, b_ref, x_ref, wa_ref, wb_ref, g_ref, be_ref, rw_ref, rb_ref,
                  x1_ref, xp_ref, gate_ref, idx_ref, rank_ref, cnt_ref, carry):
    @pl.when(pl.program_id(0) == 0)
    def _():
        carry[...] = jnp.zeros(carry.shape, F32)

    mix = (jnp.dot(a_ref[...], wa_ref[...], preferred_element_type=F32)
           + jnp.dot(b_ref[...], wb_ref[...], preferred_element_type=F32))
    x1 = _layernorm(DEEPNORM_ALPHA * x_ref[...] + mix, g_ref[...], be_ref[...])
    x1_ref[...] = x1
    xp_ref[...] = _pack_halves(x1)
    logits = _dot(x1, rw_ref[...]) + rb_ref[...]
    lane = lax.broadcasted_iota(I32, logits.shape, 1)
    vals, idxs = [], []
    for _ in range(TOP_K):
        m = jnp.max(logits, axis=-1, keepdims=True)
        sel = jnp.min(jnp.where(logits == m, lane, LANE), axis=-1, keepdims=True)
        vals.append(m)
        idxs.append(sel)
        logits = jnp.where(lane == sel, -jnp.inf, logits)
    exps = [jnp.exp(v - vals[0]) for v in vals]
    inv = 1.0 / functools.reduce(lambda p, q: p + q, exps)
    gates = jnp.zeros(logits.shape, F32)
    eidx = jnp.zeros(logits.shape, I32)
    for k in range(TOP_K):
        gates = jnp.where(lane == k, exps[k] * inv, gates)
        eidx = jnp.where(lane == k, idxs[k], eidx)
    gate_ref[...] = gates
    idx_ref[...] = eidx
    tm = logits.shape[0]
    chosen = jnp.zeros(logits.shape, F32)
    for k in range(TOP_K):
        chosen = chosen + (lane == idxs[k]).astype(F32)
    earlier = lax.broadcasted_iota(I32, (tm, tm), 0) > lax.broadcasted_iota(I32, (tm, tm), 1)
    before = carry[...] + jnp.dot(earlier.astype(BF16), chosen.astype(BF16), preferred_element_type=F32)
    ranks = jnp.zeros(logits.shape, F32)
    for k in range(TOP_K):
        rk = jnp.sum(jnp.where(lane == idxs[k], before, 0.0), axis=-1, keepdims=True)
        ranks = jnp.where(lane == k, rk, ranks)
    rank_ref[...] = ranks.astype(I32)
    carry[...] = carry[...] + jnp.sum(chosen, axis=0, keepdims=True)
    cnt_ref[...] = carry[...].astype(I32)


def _outproj_ln_router(a, b, x, wa, wb, g, be, rw, rb):
    t = x.shape[0]
    tm = TOKEN_TILE
    row = lambda i: (i, 0)
    fix = lambda i: (0, 0)
    return pl.pallas_call(
        _outproj_body,
        grid=(t // tm,),
        in_specs=[pl.BlockSpec((tm, 512), row), pl.BlockSpec((tm, 512), row), pl.BlockSpec((tm, D_MODEL), row),
                  pl.BlockSpec((512, D_MODEL), fix), pl.BlockSpec((512, D_MODEL), fix),
                  pl.BlockSpec((1, D_MODEL), fix), pl.BlockSpec((1, D_MODEL), fix),
                  pl.BlockSpec((D_MODEL, LANE), fix), pl.BlockSpec((1, LANE), fix)],
        out_specs=[pl.BlockSpec((tm, D_MODEL), row), pl.BlockSpec((tm, 512), row),
                   pl.BlockSpec((tm, LANE), row), pl.BlockSpec((tm, LANE), row), pl.BlockSpec((tm, LANE), row),
                   pl.BlockSpec((1, LANE), fix)],
        out_shape=[jax.ShapeDtypeStruct((t, D_MODEL), F32), jax.ShapeDtypeStruct((t, 512), U32),
                   jax.ShapeDtypeStruct((t, LANE), F32), jax.ShapeDtypeStruct((t, LANE), I32),
                   jax.ShapeDtypeStruct((t, LANE), I32), jax.ShapeDtypeStruct((1, LANE), I32)],
        scratch_shapes=[pltpu.VMEM((1, LANE), F32)],
        compiler_params=_cparams(1),
        name="outproj_ln_router",
    )(a, b, x, wa, wb, g, be, rw, rb)


def _sc_worker_rows(n):
    per_worker = n // (SC_CORES * SC_SUBCORES)
    worker = lax.axis_index("s") * SC_CORES + lax.axis_index("c")
    return worker * per_worker, per_worker // SC_ROWS


def _sc_mesh():
    return plsc.VectorSubcoreMesh(core_axis_name="c", subcore_axis_name="s")


def _sc_scatter_rows(src, idx, n_out):
    n, (n_src, w) = idx.shape[0], src.shape
    assert n % (SC_CORES * SC_SUBCORES * SC_ROWS) == 0 and n_src % SC_ROWS == 0

    @functools.partial(
        pl.kernel, mesh=_sc_mesh(), out_type=jax.ShapeDtypeStruct((n_out, w), src.dtype),
        scratch_types=[pltpu.VMEM((SC_ROWS,), I32), pltpu.VMEM((SC_ROWS, w), src.dtype), pltpu.SemaphoreType.DMA])
    def scatter(src_hbm, idx_hbm, out_hbm, idx_v, rows_v, sem):
        base, chunks = _sc_worker_rows(n)

        @pl.loop(0, chunks)
        def _(c):
            off = pl.multiple_of(base + c * SC_ROWS, 8)
            pltpu.sync_copy(idx_hbm.at[pl.ds(off, SC_ROWS)], idx_v)
            pltpu.sync_copy(src_hbm.at[pl.ds(pl.multiple_of(lax.rem(off, n_src), 8), SC_ROWS)], rows_v)
            pltpu.async_copy(rows_v, out_hbm.at[idx_v], sem).wait()

    return scatter(src, idx)


def _sc_gather_rows(table, idx):
    n, w = idx.shape[0], table.shape[1]
    assert n % (SC_CORES * SC_SUBCORES * SC_ROWS) == 0

    @functools.partial(
        pl.kernel, mesh=_sc_mesh(), out_type=jax.ShapeDtypeStruct((n, w), table.dtype),
        scratch_types=[pltpu.VMEM((SC_ROWS,), I32), pltpu.VMEM((SC_ROWS, w), table.dtype), pltpu.SemaphoreType.DMA])
    def gather(table_hbm, idx_hbm, out_hbm, idx_v, rows_v, sem):
        base, chunks = _sc_worker_rows(n)

        @pl.loop(0, chunks)
        def _(c):
            off = pl.multiple_of(base + c * SC_ROWS, 8)
            pltpu.sync_copy(idx_hbm.at[pl.ds(off, SC_ROWS)], idx_v)
            pltpu.async_copy(table_hbm.at[idx_v], rows_v, sem).wait()
            pltpu.sync_copy(rows_v, out_hbm.at[pl.ds(off, SC_ROWS)])

    return gather(table, idx)


def _experts_body(be_ref, nu_ref, xs_ref, wg_ref, bg_ref, wu_ref, bu_ref, wd_ref, bd_ref, o_ref, wg_s, wu_s, wd_s):
    i = pl.program_id(0)
    prev = be_ref[jnp.maximum(i - 1, 0)]

    @pl.when((i == 0) | (be_ref[i] != prev))
    def _():
        wg_s[...] = wg_ref[...].astype(BF16)
        wu_s[...] = wu_ref[...].astype(BF16)
        wd_s[...] = wd_ref[...].astype(BF16)

    @pl.when(i < nu_ref[0])
    def _():
        half = D_MODEL // 2
        x_hi, x_lo = _unpack_halves(xs_ref[...])
        x_hi = x_hi.astype(BF16)
        x_lo = x_lo.astype(BF16)
        g = (jnp.dot(x_hi, wg_s[:half, :], preferred_element_type=F32)
             + jnp.dot(x_lo, wg_s[half:, :], preferred_element_type=F32) + bg_ref[...])
        u = (jnp.dot(x_hi, wu_s[:half, :], preferred_element_type=F32)
             + jnp.dot(x_lo, wu_s[half:, :], preferred_element_type=F32) + bu_ref[...])
        g = jnp.minimum(g, SWIGLU_LIMIT)
        u = jnp.clip(u, -SWIGLU_LIMIT, SWIGLU_LIMIT)
        hmid = (u + 1.0) * (g * jax.nn.sigmoid(SWIGLU_ALPHA * g))
        out = jnp.dot(hmid.astype(BF16), wd_s[...], preferred_element_type=F32) + bd_ref[...]
        o_ref[...] = _pack_halves(out)

    @pl.when(i >= nu_ref[0])
    def _():
        o_ref[...] = jnp.zeros(o_ref.shape, o_ref.dtype)


def _experts(layer, block_e, n_used, xs, wg, bg, wu, bu, wd, bd):
    n_rows = xs.shape[0]
    bm = MOE_ROWS
    row = lambda i, be, nu: (i, 0)
    wsel = lambda i, be, nu: (layer, be[i], 0, 0)
    wspec = pl.BlockSpec((None, None, D_MODEL, D_MODEL), wsel)
    bspec = pl.BlockSpec((None, None, 1, D_MODEL), wsel)
    bias = lambda b: b.reshape(b.shape[0], b.shape[1], 1, b.shape[2])
    return pl.pallas_call(
        _experts_body,
        grid_spec=pltpu.PrefetchScalarGridSpec(
            num_scalar_prefetch=2,
            grid=(n_rows // bm,),
            in_specs=[pl.BlockSpec((bm, 512), row), wspec, bspec, wspec, bspec, wspec, bspec],
            out_specs=pl.BlockSpec((bm, 512), row),
            scratch_shapes=[pltpu.VMEM((D_MODEL, D_MODEL), BF16)] * 3,
        ),
        out_shape=jax.ShapeDtypeStruct((n_rows, 512), U32),
        compiler_params=_cparams(1),
        name="experts",
    )(block_e, n_used, xs, wg, bias(bg), wu, bias(bu), wd, bias(bd))


def _combine_body(o0_ref, o1_ref, o2_ref, o3_ref, gt_ref, x_ref, g_ref, b_ref, y_ref):
    half = D_MODEL // 2
    gates = gt_ref[...]
    hi = jnp.zeros((x_ref.shape[0], half), F32)
    lo = jnp.zeros((x_ref.shape[0], half), F32)
    for k, o_ref in enumerate((o0_ref, o1_ref, o2_ref, o3_ref)):
        h, l = _unpack_halves(o_ref[...])
        gk = gates[:, k:k + 1]
        hi = hi + gk * h
        lo = lo + gk * l
    x = x_ref[...]
    y_hi = DEEPNORM_ALPHA * x[:, :half] + hi
    y_lo = DEEPNORM_ALPHA * x[:, half:] + lo
    mu = (jnp.sum(y_hi, axis=-1, keepdims=True) + jnp.sum(y_lo, axis=-1, keepdims=True)) * (1.0 / D_MODEL)
    d_hi = y_hi - mu
    d_lo = y_lo - mu
    var = (jnp.sum(d_hi * d_hi, axis=-1, keepdims=True) + jnp.sum(d_lo * d_lo, axis=-1, keepdims=True)) * (1.0 / D_MODEL)
    r = lax.rsqrt(var + LN_EPS)
    y_ref[:, :half] = d_hi * r * g_ref[:, :half] + b_ref[:, :half]
    y_ref[:, half:] = d_lo * r * g_ref[:, half:] + b_ref[:, half:]


def _combine_ln(o4, gates, x, g, b):
    t = x.shape[0]
    tm = TOKEN_TILE
    row = lambda i: (i, 0)
    fix = lambda i: (0, 0)
    choice = lambda k: pl.BlockSpec((tm, 512), lambda i: (k * (t // tm) + i, 0))
    return pl.pallas_call(
        _combine_body,
        grid=(t // tm,),
        in_specs=[choice(0), choice(1), choice(2), choice(3), pl.BlockSpec((tm, LANE), row),
                  pl.BlockSpec((tm, D_MODEL), row), pl.BlockSpec((1, D_MODEL), fix), pl.BlockSpec((1, D_MODEL), fix)],
        out_specs=pl.BlockSpec((tm, D_MODEL), row),
        out_shape=jax.ShapeDtypeStruct((t, D_MODEL), F32),
        compiler_params=_cparams(1),
        name="combine_ln",
    )(o4, o4, o4, o4, gates, x, g, b)


def _moe(layer, x1, xp, gates, eidx, rank, counts, ln_g, ln_b, wg, bg, wu, bu, wd, bd):
    t = x1.shape[0]
    bm = MOE_ROWS
    n_blocks = t * TOP_K // bm + N_EXPERTS
    n_rows = n_blocks * bm
    cnt = counts[0, :N_EXPERTS]
    padded = (cnt + bm - 1) // bm * bm
    pad_end = jnp.cumsum(padded)
    pad_start = pad_end - padded
    e = eidx[:, :TOP_K]
    start = jnp.sum(jnp.where(e[:, :, None] == jnp.arange(N_EXPERTS, dtype=I32), pad_start, 0), axis=-1)
    dest = (start + rank[:, :TOP_K]).T.reshape(-1)
    n_used = (pad_end[-1] // bm).astype(I32)
    blk = jnp.arange(n_blocks, dtype=I32)
    first_row = jnp.minimum(blk, n_used - 1) * bm
    block_e = jnp.minimum(jnp.sum((pad_end[None, :] <= first_row[:, None]).astype(I32), axis=1), N_EXPERTS - 1)
    xs = _sc_scatter_rows(xp, dest, n_rows)
    outs = _experts(layer, block_e, n_used.reshape(1), xs, wg, bg, wu, bu, wd, bd)
    o4 = _sc_gather_rows(outs, dest)
    return _combine_ln(o4, gates, x1, ln_g, ln_b)


def _gla_body(mode, n_seq, n_chunk, C, *refs):
    if mode == "gla":
        hq_ref, hk_ref, hv_ref, hg_ref, hlr_ref, wlr_ref, blr_ref, nw_ref, s0_ref, o_ref, so_ref, st, o_scr = refs
    else:
        hq_ref, hk_ref, hv_ref, hg_ref, lb_ref, nw_ref, s0_ref, o_ref, so_ref, st, o_scr = refs
    n_heads = 4
    tstep = pl.program_id(1)

    @pl.when(tstep == 0)
    def _():
        for s in range(n_seq):
            for h in range(n_heads):
                st[s, h] = s0_ref[s, h].T

    if mode == "gla":
        q = hq_ref[...] * (GLA_DK ** -0.5)
        k = hk_ref[...]
        z = _dot(hlr_ref[...], wlr_ref[...]) + blr_ref[...]
        g = _log_sigmoid(z) * (1.0 / GLA_TAU)
    else:
        q = _silu(hq_ref[...]) * (HGRN_DK ** -0.5)
        lb = lb_ref[...]
        f = lb + (1.0 - lb) * jax.nn.sigmoid(hk_ref[...])
        k = 1.0 - f
        g = jnp.log(f)
    v = hv_ref[...]
    causal = _tri(C)
    tri = causal.astype(F32)
    mid = max(C // 2 - 1, 0)
    for s in range(n_seq):
        for c in range(n_chunk):
            r0 = (s * n_chunk + c) * C
            for h in range(n_heads):
                cs = slice(h * HEAD_W, (h + 1) * HEAD_W)
                qh, kh, vh, gh = q[r0:r0 + C, cs], k[r0:r0 + C, cs], v[r0:r0 + C, cs], g[r0:r0 + C, cs]
                b = jnp.dot(tri, gh, preferred_element_type=F32, precision=HIGHEST)
                b_last = b[C - 1:C, :]
                b_mid = b[mid:mid + 1, :]
                scores = lax.dot_general(qh * jnp.exp(b - b_mid), kh * jnp.exp(b_mid - b), (((1,), (1,)), ((), ())),
                                         preferred_element_type=F32, precision=HIGHEST)
                scores = jnp.where(causal, scores, 0.0)
                s_t = st[s, h]
                o = _dot(scores, vh) + _dot_nt(qh * jnp.exp(b), s_t)
                st[s, h] = s_t * jnp.exp(b_last) + _dot_tn(vh, kh * jnp.exp(b_last - b))
                ms = jnp.mean(o * o, axis=-1, keepdims=True)
                o_scr[r0:r0 + C, cs] = o * lax.rsqrt(ms + RMS_EPS)
    o_ref[...] = (o_scr[...] * nw_ref[...] * _silu(hg_ref[...])).astype(BF16)

    @pl.when(tstep == pl.num_programs(1) - 1)
    def _():
        for s in range(n_seq):
            for h in range(n_heads):
                so_ref[s, h] = st[s, h].T


def _seq_layout(n_batch, seq_len, row_off, sample):
    if sample:
        n_seq, n_chunk, C = SAMPLE_SEQS, 1, seq_len
        rows = n_seq * C
        grid = (n_batch // n_seq, 1)
        blk0 = row_off // rows
        rb = lambda i, t: blk0 + i
    else:
        n_seq, n_chunk, C = 1, PROMPT_TILE // SCAN_CHUNK, SCAN_CHUNK
        rows = PROMPT_TILE
        tiles = seq_len // rows
        grid = (n_batch, tiles)
        blk0 = row_off // rows
        rb = lambda i, t: blk0 + i * tiles + t
    return n_seq, n_chunk, C, rows, grid, rb


def _gla_call(mode, h, cols, extra, nw, s0, n_batch, seq_len, row_off, sample):
    n_seq, n_chunk, C, rows, grid, rb = _seq_layout(n_batch, seq_len, row_off, sample)
    colspec = lambda c0, w: pl.BlockSpec((rows, w), lambda i, t: (rb(i, t), c0 // w))
    fix2 = lambda i, t: (0, 0)
    in_specs = [colspec(cols[0], 512), colspec(cols[1], 512), colspec(cols[2], 512), colspec(cols[3], 512)]
    args = [h, h, h, h]
    if mode == "gla":
        wlr, blr = extra
        in_specs += [colspec(cols[4], LANE), pl.BlockSpec((LANE, 512), fix2), pl.BlockSpec((1, 512), fix2)]
        args += [h, wlr, blr]
    else:
        in_specs += [pl.BlockSpec((1, 512), fix2)]
        args += [extra]
    st_spec = pl.BlockSpec((n_seq, 4, HEAD_W, HEAD_W), lambda i, t: (i, 0, 0, 0))
    in_specs += [pl.BlockSpec((1, 512), fix2), st_spec]
    args += [nw, s0]
    t_rows = n_batch * seq_len
    out_blk0 = 0
    o_spec = pl.BlockSpec((rows, 512), lambda i, t: (rb(i, t) - row_off // rows, 0))
    return pl.pallas_call(
        functools.partial(_gla_body, mode, n_seq, n_chunk, C),
        grid=grid,
        in_specs=in_specs,
        out_specs=[o_spec, st_spec],
        out_shape=[jax.ShapeDtypeStruct((t_rows, 512), BF16),
                   jax.ShapeDtypeStruct((n_batch, 4, HEAD_W, HEAD_W), F32)],
        scratch_shapes=[pltpu.VMEM((n_seq, 4, HEAD_W, HEAD_W), F32), pltpu.VMEM((rows, 512), F32)],
        compiler_params=_cparams(2),
        name=mode + ("_sample" if sample else "_prompt"),
    )(*args)


def _round_bf16(x, on=True):
    return x.astype(BF16).astype(F32) if on else x


def _conf_body(n_seq, L, round_x, round_w, a_ref, gt_ref, hist_ref, w_ref, b_ref, g_ref, be_ref, o_ref, co_ref, buf, bufr):
    tstep = pl.program_id(1)
    hist = CONF_WIDTH - 1
    pad = 32 - hist

    @pl.when(tstep == 0)
    def _():
        for s in range(n_seq):
            buf[s, pad:32, :] = hist_ref[s]
            bufr[s, pad:32, :] = _round_bf16(hist_ref[s], round_x)

    u = a_ref[...] * jax.nn.sigmoid(gt_ref[...])
    ur = _round_bf16(u, round_x)
    for s in range(n_seq):
        buf[s, 32:32 + L, :] = u[s * L:(s + 1) * L, :]
        bufr[s, 32:32 + L, :] = ur[s * L:(s + 1) * L, :]
    w = _round_bf16(w_ref[...], round_w)
    for s in range(n_seq):
        acc = jnp.zeros((L, CONF_DIM), F32)
        for j in range(CONF_WIDTH):
            acc = acc + bufr[s, pad + j:pad + j + L, :] * w[j:j + 1, :]
        y = _silu(_layernorm(acc + b_ref[...], g_ref[...], be_ref[...]))
        o_ref[s * L:(s + 1) * L, :] = y.astype(o_ref.dtype)
        tail = buf[s, L + pad:L + 32, :]
        buf[s, pad:32, :] = tail
        tailr = bufr[s, L + pad:L + 32, :]
        bufr[s, pad:32, :] = tailr

    @pl.when(tstep == pl.num_programs(1) - 1)
    def _():
        for s in range(n_seq):
            co_ref[s] = buf[s, pad:32, :]


def _conf_call(h, col_a, col_g, cache, w, b, g, be, n_batch, seq_len, row_off, sample):
    n_seq, n_chunk, C, rows, grid, rb = _seq_layout(n_batch, seq_len, row_off, sample)
    L = rows // n_seq
    hist = CONF_WIDTH - 1
    colspec = lambda c0: pl.BlockSpec((rows, 512), lambda i, t: (rb(i, t), c0 // 512))
    fix2 = lambda i, t: (0, 0)
    c_spec = pl.BlockSpec((n_seq, hist, CONF_DIM), lambda i, t: (i, 0, 0))
    o_dtype = F32 if sample else BF16
    return pl.pallas_call(
        functools.partial(_conf_body, n_seq, L, True, sample),
        grid=grid,
        in_specs=[colspec(col_a), colspec(col_g), c_spec,
                  pl.BlockSpec((CONF_WIDTH, CONF_DIM), fix2), pl.BlockSpec((1, CONF_DIM), fix2),
                  pl.BlockSpec((1, CONF_DIM), fix2), pl.BlockSpec((1, CONF_DIM), fix2)],
        out_specs=[pl.BlockSpec((rows, 512), lambda i, t: (rb(i, t) - row_off // rows, 0)), c_spec],
        out_shape=[jax.ShapeDtypeStruct((n_batch * seq_len, 512), o_dtype),
                   jax.ShapeDtypeStruct((n_batch, hist, CONF_DIM), F32)],
        scratch_shapes=[pltpu.VMEM((n_seq, 32 + L, CONF_DIM), F32)] * 2,
        compiler_params=_cparams(2),
        name="conformer" + ("_sample" if sample else "_prompt"),
    )(h, h, cache, w, b, g, be)


def _ssd_body(n_seq, n_chunk, C, round_x, round_w, hz_ref, hx_ref, hdt_ref, hist_ref, s0_ref, cw_ref, cb_ref, dtb_ref, alog_ref,
              dvec_ref, nw_ref, o_ref, co_ref, so_ref, st, buf, bufr, xbc, y_scr):
    tstep = pl.program_id(1)
    L = n_chunk * C
    hist = SSM_CONV - 1
    pad = 8 - hist
    n_pairs = SSM_HEADS // 2

    @pl.when(tstep == 0)
    def _():
        for s in range(n_seq):
            buf[s, pad:8, :] = hist_ref[s]
            bufr[s, pad:8, :] = _round_bf16(hist_ref[s], round_x)
            for m in range(n_pairs):
                st[s, m] = s0_ref[s, m]

    cw = _round_bf16(cw_ref[...], round_w)
    for s in range(n_seq):
        hx = hx_ref[s * L:(s + 1) * L, :]
        buf[s, 8:8 + L, :] = hx
        bufr[s, 8:8 + L, :] = _round_bf16(hx, round_x)
        acc = jnp.zeros((L, SSM_CONV_DIM), F32)
        for j in range(SSM_CONV):
            acc = acc + bufr[s, pad + j:pad + j + L, :] * cw[j:j + 1, :]
        xbc[s * L:(s + 1) * L, :] = _silu(acc + cb_ref[...])
        tail = buf[s, L + pad:L + 8, :]
        buf[s, pad:8, :] = tail
        tailr = bufr[s, L + pad:L + 8, :]
        bufr[s, pad:8, :] = tailr

    dt = _softplus(hdt_ref[...] + dtb_ref[...])
    la = dt * (-jnp.exp(alog_ref[...]))
    hrow = lax.broadcasted_iota(I32, (LANE, SSM_INNER), 0)
    hcol = lax.broadcasted_iota(I32, (LANE, SSM_INNER), 1) // SSM_HEADDIM
    expand = (hrow == hcol).astype(F32)
    dtx = jnp.dot(dt, expand, preferred_element_type=F32, precision=HIGHEST)
    causal = _tri(C)
    tri = causal.astype(F32)
    lane = lax.broadcasted_iota(I32, (C, HEAD_W), 1)
    for s in range(n_seq):
        for c in range(n_chunk):
            r0 = (s * n_chunk + c) * C
            rs = slice(r0, r0 + C)
            la_c = la[rs, :]
            bcol = jnp.dot(tri, la_c, preferred_element_type=F32, precision=HIGHEST)
            brow = lax.dot_general(la_c, tri, (((0,), (1,)), ((), ())), preferred_element_type=F32,
                                   precision=HIGHEST)
            xs_c = xbc[rs, 0:SSM_INNER]
            v_c = xs_c * dtx[rs, :]
            for m in range(n_pairs):
                grp = (2 * m) // (SSM_HEADS // SSM_GROUPS)
                bm = xbc[rs, SSM_INNER + grp * SSM_STATE:SSM_INNER + (grp + 1) * SSM_STATE]
                cm = xbc[rs, SSM_INNER + (SSM_GROUPS + grp) * SSM_STATE:SSM_INNER + (SSM_GROUPS + grp + 1) * SSM_STATE]
                gmat = lax.dot_general(cm, bm, (((1,), (1,)), ((), ())), preferred_element_type=F32,
                                       precision=HIGHEST)
                ps = slice(m * HEAD_W, (m + 1) * HEAD_W)
                vp = v_c[:, ps]
                s_t = st[s, m]
                o_halves = []
                for hh in range(2):
                    hd = 2 * m + hh
                    bc = bcol[:, hd:hd + 1]
                    br = brow[hd:hd + 1, :]
                    dec = jnp.where(causal, jnp.exp(jnp.minimum(bc - br, 0.0)), 0.0)
                    b_last = bcol[C - 1:C, hd:hd + 1]
                    o_halves.append(_dot(gmat * dec, vp) + _dot_nt(cm * jnp.exp(bc), s_t))
                    kv = _dot_tn(vp, bm * jnp.exp(b_last - bc))
                    vs = slice(hh * SSM_HEADDIM, (hh + 1) * SSM_HEADDIM)
                    st[s, m, vs, :] = s_t[vs, :] * jnp.exp(b_last) + kv[vs, :]
                o_pair = jnp.where(lane < SSM_HEADDIM, o_halves[0], o_halves[1])
                y_scr[rs, ps] = o_pair + dvec_ref[:, ps] * xs_c[:, ps]
    y = y_scr[...] * _silu(hz_ref[...])
    gw = SSM_INNER // SSM_GROUPS
    for grp in range(SSM_GROUPS):
        gs = slice(grp * gw, (grp + 1) * gw)
        yg = y[:, gs]
        ms = jnp.mean(yg * yg, axis=-1, keepdims=True)
        o_ref[:, gs] = (yg * lax.rsqrt(ms + RMS_EPS) * nw_ref[:, gs]).astype(BF16)

    @pl.when(tstep == pl.num_programs(1) - 1)
    def _():
        for s in range(n_seq):
            co_ref[s] = buf[s, pad:8, :]
            for m in range(n_pairs):
                so_ref[s, m] = st[s, m]


def _ssd_call(h, col_z, col_x, col_dt, cache, s0, cw, cb, dtb, alog, dvec, nw, n_batch, seq_len, row_off, sample):
    n_seq, n_chunk, C, rows, grid, rb = _seq_layout(n_batch, seq_len, row_off, sample)
    L = rows // n_seq
    hist = SSM_CONV - 1
    n_pairs = SSM_HEADS // 2
    colspec = lambda c0, w: pl.BlockSpec((rows, w), lambda i, t: (rb(i, t), c0 // w))
    fix2 = lambda i, t: (0, 0)
    c_spec = pl.BlockSpec((n_seq, hist, SSM_CONV_DIM), lambda i, t: (i, 0, 0))
    st_spec = pl.BlockSpec((n_seq, n_pairs, HEAD_W, SSM_STATE), lambda i, t: (i, 0, 0, 0))
    return pl.pallas_call(
        functools.partial(_ssd_body, n_seq, n_chunk, C, sample, True),
        grid=grid,
        in_specs=[colspec(col_z, 512), colspec(col_x, SSM_CONV_DIM), colspec(col_dt, LANE), c_spec, st_spec,
                  pl.BlockSpec((SSM_CONV, SSM_CONV_DIM), fix2), pl.BlockSpec((1, SSM_CONV_DIM), fix2),
                  pl.BlockSpec((1, LANE), fix2), pl.BlockSpec((1, LANE), fix2),
                  pl.BlockSpec((1, SSM_INNER), fix2), pl.BlockSpec((1, SSM_INNER), fix2)],
        out_specs=[pl.BlockSpec((rows, 512), lambda i, t: (rb(i, t) - row_off // rows, 0)), c_spec, st_spec],
        out_shape=[jax.ShapeDtypeStruct((n_batch * seq_len, 512), BF16),
                   jax.ShapeDtypeStruct((n_batch, hist, SSM_CONV_DIM), F32),
                   jax.ShapeDtypeStruct((n_batch, n_pairs, HEAD_W, SSM_STATE), F32)],
        scratch_shapes=[pltpu.VMEM((n_seq, n_pairs, HEAD_W, SSM_STATE), F32),
                        pltpu.VMEM((n_seq, 8 + L, SSM_CONV_DIM), F32),
                        pltpu.VMEM((n_seq, 8 + L, SSM_CONV_DIM), F32),
                        pltpu.VMEM((rows, SSM_CONV_DIM), F32),
                        pltpu.VMEM((rows, SSM_INNER), F32)],
        compiler_params=_cparams(2),
        name="ssd" + ("_sample" if sample else "_prompt"),
    )(h, h, h, cache, s0, cw, cb, dtb, alog, dvec, nw)


def _pad_heads(w, n_heads, width):
    lead = w.shape[:-1]
    w = w.reshape(lead + (n_heads, width))
    w = jnp.pad(w, [(0, 0)] * len(lead) + [(0, 0), (0, HEAD_W - width)])
    return w.reshape(lead + (n_heads * HEAD_W,))


def _row(v):
    return v.reshape(1, -1).astype(F32)


def kernel(x_prompt, x_sample, state_gla, cache_conformer, state_hgrn, state_ssm, cache_mamba_conv, w_in_even, w_gla_gate_lr, b_gla_gate, gla_norm_w, conf_conv_w, conf_conv_b, conf_ln_g, conf_ln_b, w_out_even, w_in_odd, hgrn_lower_bounds, hgrn_norm_w, mamba_conv_w, mamba_conv_b, mamba_dt_bias, mamba_a_log, mamba_d, mamba_norm_w, w_out_odd, ln1_g, ln1_b, ln2_g, ln2_b, router_w, router_b, expert_w_gate, expert_b_gate, expert_w_up, expert_b_up, expert_w_down, expert_b_down):
    bp, lp, _ = x_prompt.shape
    bs, ls, _ = x_sample.shape
    tp, ts = bp * lp, bs * ls
    x = jnp.concatenate([x_prompt.reshape(tp, D_MODEL), x_sample.reshape(ts, D_MODEL)], axis=0)

    def router_params(layer):
        rw = jnp.pad(router_w[layer], ((0, 0), (0, LANE - N_EXPERTS)))
        rb = jnp.pad(router_b[layer].astype(F32), (0, LANE - N_EXPERTS), constant_values=-1e30)
        return rw, rb.reshape(1, LANE)

    def finish_layer(layer, x, mix_a, mix_b, w_out):
        rw, rb = router_params(layer)
        x1, xp, gates, eidx, rank, counts = _outproj_ln_router(
            mix_a, mix_b, x, w_out[:512].astype(BF16), w_out[512:].astype(BF16),
            _row(ln1_g[layer]), _row(ln1_b[layer]), rw, rb)
        return _moe(layer, x1, xp, gates, eidx, rank, counts, _row(ln2_g[layer]), _row(ln2_b[layer]),
                    expert_w_gate, expert_b_gate, expert_w_up, expert_b_up, expert_w_down, expert_b_down)

    wi = w_in_even[0]
    wq, wk, wv, wg, wlr, wglu = jnp.split(wi, [256, 512, 1024, 1536, 1552], axis=1)
    w_even = jnp.concatenate([_pad_heads(wq, GLA_HEADS, GLA_DK), _pad_heads(wk, GLA_HEADS, GLA_DK), wv, wg, wglu,
                              jnp.pad(wlr, ((0, 0), (0, LANE - GLA_RANK)))], axis=1).astype(BF16)
    cols_gla = (0, 512, 1024, 1536, 3072)
    col_a, col_gate = 2048, 2560
    h = _inproj(x, w_even)
    wlr_p = jnp.pad(_pad_heads(w_gla_gate_lr[0], GLA_HEADS, GLA_DK), ((0, LANE - GLA_RANK), (0, 0)))
    blr_p = _row(_pad_heads(b_gla_gate[0], GLA_HEADS, GLA_DK))
    nw = _row(gla_norm_w[0])
    conf_args = (conf_conv_w[0], _row(conf_conv_b[0]), _row(conf_ln_g[0]), _row(conf_ln_b[0]))
    s0_p = jnp.zeros((bp, GLA_HEADS, HEAD_W, HEAD_W), F32)
    s0_s = jnp.pad(state_gla[0], ((0, 0), (0, 0), (0, HEAD_W - GLA_DK), (0, 0)))
    oa_p, sg_p = _gla_call("gla", h, cols_gla, (wlr_p, blr_p), nw, s0_p, bp, lp, 0, False)
    oa_s, sg_s = _gla_call("gla", h, cols_gla, (wlr_p, blr_p), nw, s0_s, bs, ls, tp, True)
    ob_p, cc_p = _conf_call(h, col_a, col_gate, jnp.zeros((bp,) + cache_conformer.shape[2:], F32), *conf_args,
                            bp, lp, 0, False)
    ob_s, cc_s = _conf_call(h, col_a, col_gate, cache_conformer[0], *conf_args, bs, ls, tp, True)
    mix_a = jnp.concatenate([oa_p, oa_s], axis=0)
    mix_b = jnp.concatenate([ob_p, ob_s.astype(BF16)], axis=0)
    x = finish_layer(0, x, mix_a, mix_b, w_out_even[0])
    gla_p, gla_s = sg_p[:, :, :GLA_DK, :][None], sg_s[:, :, :GLA_DK, :][None]
    conf_p, conf_s = cc_p[None], cc_s[None]

    lb_cum = jnp.cumsum(jax.nn.softmax(hgrn_lower_bounds.astype(F32), axis=0), axis=0)
    lower_bound = _row((lb_cum - lb_cum[0])[1])
    wo = w_in_odd[0]
    w_odd = jnp.concatenate([wo[:, 2560:3584], wo[:, :2560],
                             jnp.pad(wo[:, 3584:], ((0, 0), (0, LANE - SSM_HEADS)))], axis=1).astype(BF16)
    h = _inproj(x, w_odd)
    cols_hgrn = (1024, 1536, 2048, 2560)
    col_z, col_x, col_dt = 3072, 0, 3584
    nw = _row(hgrn_norm_w[0])
    oc_p, sh_p = _gla_call("hgrn", h, cols_hgrn, lower_bound, nw,
                           jnp.zeros((bp, HGRN_HEADS, HEAD_W, HEAD_W), F32), bp, lp, 0, False)
    oc_s, sh_s = _gla_call("hgrn", h, cols_hgrn, lower_bound, nw, state_hgrn[0], bs, ls, tp, True)

    def pair_states(s):
        return jnp.swapaxes(s, 2, 3).reshape(s.shape[0], SSM_HEADS // 2, HEAD_W, SSM_STATE)

    def unpair_states(s):
        return jnp.swapaxes(s.reshape(s.shape[0], SSM_HEADS, SSM_HEADDIM, SSM_STATE), 2, 3)

    pad8 = lambda v: jnp.pad(v.astype(F32), (0, LANE - SSM_HEADS)).reshape(1, LANE)
    ssd_args = (mamba_conv_w[0], _row(mamba_conv_b[0]), pad8(mamba_dt_bias[0]), pad8(mamba_a_log[0]),
                _row(jnp.repeat(mamba_d[0], SSM_HEADDIM)), _row(mamba_norm_w[0]))
    od_p, cm_p, ss_p = _ssd_call(h, col_z, col_x, col_dt, jnp.zeros((bp,) + cache_mamba_conv.shape[2:], F32),
                                 jnp.zeros((bp, SSM_HEADS // 2, HEAD_W, SSM_STATE), F32), *ssd_args,
                                 bp, lp, 0, False)
    od_s, cm_s, ss_s = _ssd_call(h, col_z, col_x, col_dt, cache_mamba_conv[0], pair_states(state_ssm[0]),
                                 *ssd_args, bs, ls, tp, True)
    mix_a = jnp.concatenate([oc_p, oc_s], axis=0)
    mix_b = jnp.concatenate([od_p, od_s], axis=0)
    x = finish_layer(1, x, mix_a, mix_b, w_out_odd[0])

    y_prompt = x[:tp].reshape(bp, lp, D_MODEL)
    y_sample = x[tp:].reshape(bs, ls, D_MODEL)
    return (y_prompt, y_sample, gla_p, gla_s, conf_p, conf_s, sh_p[None], sh_s[None],
            unpair_states(ss_p)[None], unpair_states(ss_s)[None], cm_p[None], cm_s[None])
```

```python
import functools

import jax
import jax.numpy as jnp
from jax import lax
from jax.experimental import pallas as pl
from jax.experimental.pallas import tpu as pltpu
from jax.experimental.pallas import tpu_sc as plsc

F32 = jnp.float32
BF16 = jnp.bfloat16
I32 = jnp.int32
U32 = jnp.uint32
HIGHEST = lax.Precision.HIGHEST

D_MODEL = 1024
DEPTH = 2
DEEPNORM_ALPHA = (2.0 * DEPTH) ** 0.25
LN_EPS = 1e-5
RMS_EPS = 1e-6
LANE = 128
HEAD_W = 128
GLA_HEADS, GLA_DK, GLA_RANK, GLA_TAU = 4, 64, 16, 16.0
CONF_DIM, CONF_WIDTH = 512, 31
HGRN_HEADS, HGRN_DK = 4, 128
SSM_HEADS, SSM_HEADDIM, SSM_STATE, SSM_GROUPS, SSM_CONV = 8, 64, 128, 2, 4
SSM_INNER = SSM_HEADS * SSM_HEADDIM
SSM_CONV_DIM = SSM_INNER + 2 * SSM_GROUPS * SSM_STATE
N_EXPERTS, TOP_K = 32, 4
SWIGLU_ALPHA, SWIGLU_LIMIT = 1.702, 7.0
SCAN_CHUNK = 64
PROMPT_TILE = 256
SAMPLE_SEQS = 16
TOKEN_TILE = 512
MOE_ROWS = 256
SC_CORES, SC_SUBCORES = 2, 16
SC_ROWS = 64
VMEM_LIMIT = 56 * 1024 * 1024


def _cparams(n_axes):
    return pltpu.CompilerParams(dimension_semantics=("arbitrary",) * n_axes, vmem_limit_bytes=VMEM_LIMIT)


def _silu(x):
    return x * jax.nn.sigmoid(x)


def _softplus(x):
    return jnp.maximum(x, 0.0) + jnp.log(1.0 + jnp.exp(-jnp.abs(x)))


def _log_sigmoid(x):
    return jnp.minimum(x, 0.0) - jnp.log(1.0 + jnp.exp(-jnp.abs(x)))


def _layernorm(y, g, b):
    mu = jnp.mean(y, axis=-1, keepdims=True)
    d = y - mu
    var = jnp.mean(d * d, axis=-1, keepdims=True)
    return d * lax.rsqrt(var + LN_EPS) * g + b


def _dot(a, b):
    return jnp.dot(a.astype(BF16), b.astype(BF16), preferred_element_type=F32)


def _dot_nt(a, b):
    return lax.dot_general(a.astype(BF16), b.astype(BF16), (((1,), (1,)), ((), ())), preferred_element_type=F32)


def _dot_tn(a, b):
    return lax.dot_general(a.astype(BF16), b.astype(BF16), (((0,), (0,)), ((), ())), preferred_element_type=F32)


def _tri(c):
    r = lax.broadcasted_iota(I32, (c, c), 0)
    k = lax.broadcasted_iota(I32, (c, c), 1)
    return r >= k


def _inproj_body(x_ref, w_ref, o_ref):
    xb = x_ref[...].astype(BF16)
    n = w_ref.shape[1]
    for c0 in range(0, n, 512):
        c1 = min(c0 + 512, n)
        o_ref[:, c0:c1] = jnp.dot(xb, w_ref[:, c0:c1], preferred_element_type=F32)


def _inproj(x, w):
    t, k = x.shape
    n = w.shape[1]
    return pl.pallas_call(
        _inproj_body,
        grid=(t // TOKEN_TILE,),
        in_specs=[pl.BlockSpec((TOKEN_TILE, k), lambda i: (i, 0)),
                  pl.BlockSpec((k, n), lambda i: (0, 0))],
        out_specs=pl.BlockSpec((TOKEN_TILE, n), lambda i: (i, 0)),
        out_shape=jax.ShapeDtypeStruct((t, n), F32),
        compiler_params=_cparams(1),
        name="inproj",
    )(x, w)


def _pack_halves(y):
    half = y.shape[1] // 2
    hi = lax.bitcast_convert_type(y[:, :half].astype(BF16).astype(F32), U32)
    lo = lax.bitcast_convert_type(y[:, half:].astype(BF16).astype(F32), U32)
    return (hi & jnp.uint32(0xFFFF0000)) | (lo >> 16)


def _unpack_halves(w):
    hi = lax.bitcast_convert_type(w & jnp.uint32(0xFFFF0000), F32)
    lo = lax.bitcast_convert_type(w << 16, F32)
    return hi, lo


def _outproj_body(a_ref, b_ref, x_ref, wa_ref, wb_ref, g_ref, be_ref, rw_ref, rb_ref,
                  x1_ref, xp_ref, gate_ref, idx_ref, rank_ref, cnt_ref, carry):
    @pl.when(pl.program_id(0) == 0)
    def _():
        carry[...] = jnp.zeros(carry.shape, F32)

    mix = (jnp.dot(a_ref[...], wa_ref[...], preferred_element_type=F32)
           + jnp.dot(b_ref[...], wb_ref[...], preferred_element_type=F32))
    x1 = _layernorm(DEEPNORM_ALPHA * x_ref[...] + mix, g_ref[...], be_ref[...])
    x1_ref[...] = x1
    xp_ref[...] = _pack_halves(x1)
    logits = _dot(x1, rw_ref[...]) + rb_ref[...]
    lane = lax.broadcasted_iota(I32, logits.shape, 1)
    vals, idxs = [], []
    for _ in range(TOP_K):
        m = jnp.max(logits, axis=-1, keepdims=True)
        sel = jnp.min(jnp.where(logits == m, lane, LANE), axis=-1, keepdims=True)
        vals.append(m)
        idxs.append(sel)
        logits = jnp.where(lane == sel, -jnp.inf, logits)
    exps = [jnp.exp(v - vals[0]) for v in vals]
    inv = 1.0 / functools.reduce(lambda p, q: p + q, exps)
    gates = jnp.zeros(logits.shape, F32)
    eidx = jnp.zeros(logits.shape, I32)
    for k in range(TOP_K):
        gates = jnp.where(lane == k, exps[k] * inv, gates)
        eidx = jnp.where(lane == k, idxs[k], eidx)
    gate_ref[...] = gates
    idx_ref[...] = eidx
    tm = logits.shape[0]
    chosen = jnp.zeros(logits.shape, F32)
    for k in range(TOP_K):
        chosen = chosen + (lane == idxs[k]).astype(F32)
    earlier = lax.broadcasted_iota(I32, (tm, tm), 0) > lax.broadcasted_iota(I32, (tm, tm), 1)
    before = carry[...] + jnp.dot(earlier.astype(BF16), chosen.astype(BF16), preferred_element_type=F32)
    ranks = jnp.zeros(logits.shape, F32)
    for k in range(TOP_K):
        rk = jnp.sum(jnp.where(lane == idxs[k], before, 0.0), axis=-1, keepdims=True)
        ranks = jnp.where(lane == k, rk, ranks)
    rank_ref[...] = ranks.astype(I32)
    carry[...] = carry[...] + jnp.sum(chosen, axis=0, keepdims=True)
    cnt_ref[...] = carry[...].astype(I32)


def _outproj_ln_router(a, b, x, wa, wb, g, be, rw, rb):
    t = x.shape[0]
    tm = TOKEN_TILE
    row = lambda i: (i, 0)
    fix = lambda i: (0, 0)
    return pl.pallas_call(
        _outproj_body,
        grid=(t // tm,),
        in_specs=[pl.BlockSpec((tm, 512), row), pl.BlockSpec((tm, 512), row), pl.BlockSpec((tm, D_MODEL), row),
                  pl.BlockSpec((512, D_MODEL), fix), pl.BlockSpec((512, D_MODEL), fix),
                  pl.BlockSpec((1, D_MODEL), fix), pl.BlockSpec((1, D_MODEL), fix),
                  pl.BlockSpec((D_MODEL, LANE), fix), pl.BlockSpec((1, LANE), fix)],
        out_specs=[pl.BlockSpec((tm, D_MODEL), row), pl.BlockSpec((tm, 512), row),
                   pl.BlockSpec((tm, LANE), row), pl.BlockSpec((tm, LANE), row), pl.BlockSpec((tm, LANE), row),
                   pl.BlockSpec((1, LANE), fix)],
        out_shape=[jax.ShapeDtypeStruct((t, D_MODEL), F32), jax.ShapeDtypeStruct((t, 512), U32),
                   jax.ShapeDtypeStruct((t, LANE), F32), jax.ShapeDtypeStruct((t, LANE), I32),
                   jax.ShapeDtypeStruct((t, LANE), I32), jax.ShapeDtypeStruct((1, LANE), I32)],
        scratch_shapes=[pltpu.VMEM((1, LANE), F32)],
        compiler_params=_cparams(1),
        name="outproj_ln_router",
    )(a, b, x, wa, wb, g, be, rw, rb)


def _sc_worker_rows(n):
    per_worker = n // (SC_CORES * SC_SUBCORES)
    worker = lax.axis_index("s") * SC_CORES + lax.axis_index("c")
    return worker * per_worker, per_worker // SC_ROWS


def _sc_mesh():
    return plsc.VectorSubcoreMesh(core_axis_name="c", subcore_axis_name="s")


def _sc_scatter_rows(src, idx, n_out):
    n, (n_src, w) = idx.shape[0], src.shape
    assert n % (SC_CORES * SC_SUBCORES * SC_ROWS) == 0 and n_src % SC_ROWS == 0

    @functools.partial(
        pl.kernel, mesh=_sc_mesh(), out_type=jax.ShapeDtypeStruct((n_out, w), src.dtype),
        scratch_types=[pltpu.VMEM((SC_ROWS,), I32), pltpu.VMEM((SC_ROWS, w), src.dtype), pltpu.SemaphoreType.DMA])
    def scatter(src_hbm, idx_hbm, out_hbm, idx_v, rows_v, sem):
        base, chunks = _sc_worker_rows(n)

        @pl.loop(0, chunks)
        def _(c):
            off = pl.multiple_of(base + c * SC_ROWS, 8)
            pltpu.sync_copy(idx_hbm.at[pl.ds(off, SC_ROWS)], idx_v)
            pltpu.sync_copy(src_hbm.at[pl.ds(pl.multiple_of(lax.rem(off, n_src), 8), SC_ROWS)], rows_v)
            pltpu.async_copy(rows_v, out_hbm.at[idx_v], sem).wait()

    return scatter(src, idx)


def _sc_gather_rows(table, idx):
    n, w = idx.shape[0], table.shape[1]
    assert n % (SC_CORES * SC_SUBCORES * SC_ROWS) == 0

    @functools.partial(
        pl.kernel, mesh=_sc_mesh(), out_type=jax.ShapeDtypeStruct((n, w), table.dtype),
        scratch_types=[pltpu.VMEM((SC_ROWS,), I32), pltpu.VMEM((SC_ROWS, w), table.dtype), pltpu.SemaphoreType.DMA])
    def gather(table_hbm, idx_hbm, out_hbm, idx_v, rows_v, sem):
        base, chunks = _sc_worker_rows(n)

        @pl.loop(0, chunks)
        def _(c):
            off = pl.multiple_of(base + c * SC_ROWS, 8)
            pltpu.sync_copy(idx_hbm.at[pl.ds(off, SC_ROWS)], idx_v)
            pltpu.async_copy(table_hbm.at[idx_v], rows_v, sem).wait()
            pltpu.sync_copy(rows_v, out_hbm.at[pl.ds(off, SC_ROWS)])

    return gather(table, idx)


def _experts_body(be_ref, nu_ref, xs_ref, wg_ref, bg_ref, wu_ref, bu_ref, wd_ref, bd_ref, o_ref, wg_s, wu_s, wd_s):
    i = pl.program_id(0)
    prev = be_ref[jnp.maximum(i - 1, 0)]

    @pl.when((i == 0) | (be_ref[i] != prev))
    def _():
        wg_s[...] = wg_ref[...].astype(BF16)
        wu_s[...] = wu_ref[...].astype(BF16)
        wd_s[...] = wd_ref[...].astype(BF16)

    @pl.when(i < nu_ref[0])
    def _():
        half = D_MODEL // 2
        x_hi, x_lo = _unpack_halves(xs_ref[...])
        x_hi = x_hi.astype(BF16)
        x_lo = x_lo.astype(BF16)
        g = (jnp.dot(x_hi, wg_s[:half, :], preferred_element_type=F32)
             + jnp.dot(x_lo, wg_s[half:, :], preferred_element_type=F32) + bg_ref[...])
        u = (jnp.dot(x_hi, wu_s[:half, :], preferred_element_type=F32)
             + jnp.dot(x_lo, wu_s[half:, :], preferred_element_type=F32) + bu_ref[...])
        g = jnp.minimum(g, SWIGLU_LIMIT)
        u = jnp.clip(u, -SWIGLU_LIMIT, SWIGLU_LIMIT)
        hmid = (u + 1.0) * (g * jax.nn.sigmoid(SWIGLU_ALPHA * g))
        out = jnp.dot(hmid.astype(BF16), wd_s[...], preferred_element_type=F32) + bd_ref[...]
        o_ref[...] = _pack_halves(out)

    @pl.when(i >= nu_ref[0])
    def _():
        o_ref[...] = jnp.zeros(o_ref.shape, o_ref.dtype)


def _experts(layer, block_e, n_used, xs, wg, bg, wu, bu, wd, bd):
    n_rows = xs.shape[0]
    bm = MOE_ROWS
    row = lambda i, be, nu: (i, 0)
    wsel = lambda i, be, nu: (layer, be[i], 0, 0)
    wspec = pl.BlockSpec((None, None, D_MODEL, D_MODEL), wsel)
    bspec = pl.BlockSpec((None, None, 1, D_MODEL), wsel)
    bias = lambda b: b.reshape(b.shape[0], b.shape[1], 1, b.shape[2])
    return pl.pallas_call(
        _experts_body,
        grid_spec=pltpu.PrefetchScalarGridSpec(
            num_scalar_prefetch=2,
            grid=(n_rows // bm,),
            in_specs=[pl.BlockSpec((bm, 512), row), wspec, bspec, wspec, bspec, wspec, bspec],
            out_specs=pl.BlockSpec((bm, 512), row),
            scratch_shapes=[pltpu.VMEM((D_MODEL, D_MODEL), BF16)] * 3,
        ),
        out_shape=jax.ShapeDtypeStruct((n_rows, 512), U32),
        compiler_params=_cparams(1),
        name="experts",
    )(block_e, n_used, xs, wg, bias(bg), wu, bias(bu), wd, bias(bd))


def _combine_body(o0_ref, o1_ref, o2_ref, o3_ref, gt_ref, x_ref, g_ref, b_ref, y_ref):
    half = D_MODEL // 2
    gates = gt_ref[...]
    hi = jnp.zeros((x_ref.shape[0], half), F32)
    lo = jnp.zeros((x_ref.shape[0], half), F32)
    for k, o_ref in enumerate((o0_ref, o1_ref, o2_ref, o3_ref)):
        h, l = _unpack_halves(o_ref[...])
        gk = gates[:, k:k + 1]
        hi = hi + gk * h
        lo = lo + gk * l
    x = x_ref[...]
    y_hi = DEEPNORM_ALPHA * x[:, :half] + hi
    y_lo = DEEPNORM_ALPHA * x[:, half:] + lo
    mu = (jnp.sum(y_hi, axis=-1, keepdims=True) + jnp.sum(y_lo, axis=-1, keepdims=True)) * (1.0 / D_MODEL)
    d_hi = y_hi - mu
    d_lo = y_lo - mu
    var = (jnp.sum(d_hi * d_hi, axis=-1, keepdims=True) + jnp.sum(d_lo * d_lo, axis=-1, keepdims=True)) * (1.0 / D_MODEL)
    r = lax.rsqrt(var + LN_EPS)
    y_ref[:, :half] = d_hi * r * g_ref[:, :half] + b_ref[:, :half]
    y_ref[:, half:] = d_lo * r * g_ref[:, half:] + b_ref[:, half:]


def _combine_ln(o4, gates, x, g, b):
    t = x.shape[0]
    tm = TOKEN_TILE
    row = lambda i: (i, 0)
    fix = lambda i: (0, 0)
    choice = lambda k: pl.BlockSpec((tm, 512), lambda i: (k * (t // tm) + i, 0))
    return pl.pallas_call(
        _combine_body,
        grid=(t // tm,),
        in_specs=[choice(0), choice(1), choice(2), choice(3), pl.BlockSpec((tm, LANE), row),
                  pl.BlockSpec((tm, D_MODEL), row), pl.BlockSpec((1, D_MODEL), fix), pl.BlockSpec((1, D_MODEL), fix)],
        out_specs=pl.BlockSpec((tm, D_MODEL), row),
        out_shape=jax.ShapeDtypeStruct((t, D_MODEL), F32),
        compiler_params=_cparams(1),
        name="combine_ln",
    )(o4, o4, o4, o4, gates, x, g, b)


def _moe(layer, x1, xp, gates, eidx, rank, counts, ln_g, ln_b, wg, bg, wu, bu, wd, bd):
    t = x1.shape[0]
    bm = MOE_ROWS
    n_blocks = t * TOP_K // bm + N_EXPERTS
    n_rows = n_blocks * bm
    cnt = counts[0, :N_EXPERTS]
    padded = (cnt + bm - 1) // bm * bm
    pad_end = jnp.cumsum(padded)
    pad_start = pad_end - padded
    e = eidx[:, :TOP_K]
    start = jnp.sum(jnp.where(e[:, :, None] == jnp.arange(N_EXPERTS, dtype=I32), pad_start, 0), axis=-1)
    dest = (start + rank[:, :TOP_K]).T.reshape(-1)
    n_used = (pad_end[-1] // bm).astype(I32)
    blk = jnp.arange(n_blocks, dtype=I32)
    first_row = jnp.minimum(blk, n_used - 1) * bm
    block_e = jnp.minimum(jnp.sum((pad_end[None, :] <= first_row[:, None]).astype(I32), axis=1), N_EXPERTS - 1)
    xs = _sc_scatter_rows(xp, dest, n_rows)
    outs = _experts(layer, block_e, n_used.reshape(1), xs, wg, bg, wu, bu, wd, bd)
    o4 = _sc_gather_rows(outs, dest)
    return _combine_ln(o4, gates, x1, ln_g, ln_b)


def _split2(x):
    hi = x.astype(BF16)
    return hi, (x - hi.astype(F32)).astype(BF16)


def _split3(x):
    hi = x.astype(BF16)
    rem = x - hi.astype(F32)
    mid = rem.astype(BF16)
    return hi, mid, (rem - mid.astype(F32)).astype(BF16)


def _chunk_cumsum(g, C):
    rows = g.shape[0]
    r = lax.broadcasted_iota(I32, (rows, rows), 0)
    c = lax.broadcasted_iota(I32, (rows, rows), 1)
    tri = ((r >= c) & (r // C == c // C)).astype(BF16)
    hi, mid, lo = _split3(g)
    dot = lambda part: jnp.dot(tri, part, preferred_element_type=F32)
    return dot(hi) + dot(mid) + dot(lo)


def _gla_body(mode, n_seq, n_chunk, C, *refs):
    if mode == "gla":
        hq_ref, hk_ref, hv_ref, hg_ref, hlr_ref, wlr_ref, blr_ref, nw_ref, s0_ref, o_ref, so_ref, st, o_scr = refs
    else:
        hq_ref, hk_ref, hv_ref, hg_ref, lb_ref, nw_ref, s0_ref, o_ref, so_ref, st, o_scr = refs
    n_heads = 4
    tstep = pl.program_id(1)

    @pl.when(tstep == 0)
    def _():
        for s in range(n_seq):
            for h in range(n_heads):
                st[s, h] = s0_ref[s, h].T

    if mode == "gla":
        q = hq_ref[...] * (GLA_DK ** -0.5)
        k = hk_ref[...]
        z = _dot(hlr_ref[...], wlr_ref[...]) + blr_ref[...]
        g = _log_sigmoid(z) * (1.0 / GLA_TAU)
    else:
        q = _silu(hq_ref[...]) * (HGRN_DK ** -0.5)
        lb = lb_ref[...]
        f = lb + (1.0 - lb) * jax.nn.sigmoid(hk_ref[...])
        k = 1.0 - f
        g = jnp.log(f)
    v = hv_ref[...]
    causal = _tri(C)
    mid = max(C // 2 - 1, 0)
    b_all = _chunk_cumsum(g, C)
    for s in range(n_seq):
        states = [st[s, h] for h in range(n_heads)]
        for c in range(n_chunk):
            r0 = (s * n_chunk + c) * C
            rs = slice(r0, r0 + C)
            b, qc, kc = b_all[rs, :], q[rs, :], k[rs, :]
            b_last = b[C - 1:C, :]
            b_mid = b[mid:mid + 1, :]
            qe_hi, qe_lo = _split2(qc * jnp.exp(b - b_mid))
            ke_hi, ke_lo = _split2(kc * jnp.exp(b_mid - b))
            q_state = (qc * jnp.exp(b)).astype(BF16)
            k_state = (kc * jnp.exp(b_last - b)).astype(BF16)
            decay = jnp.exp(b_last)
            for h in range(n_heads):
                cs = slice(h * HEAD_W, (h + 1) * HEAD_W)
                lhs = jnp.concatenate([qe_hi[:, cs], qe_hi[:, cs], qe_lo[:, cs]], axis=1)
                rhs = jnp.concatenate([ke_hi[:, cs], ke_lo[:, cs], ke_hi[:, cs]], axis=1)
                scores = jnp.where(causal, _dot_nt(lhs, rhs), 0.0)
                vh = v[rs, cs].astype(BF16)
                o = _dot(scores, vh) + _dot_nt(q_state[:, cs], states[h])
                states[h] = states[h] * decay[:, cs] + _dot_tn(vh, k_state[:, cs])
                ms = jnp.mean(o * o, axis=-1, keepdims=True)
                o_scr[rs, cs] = o * lax.rsqrt(ms + RMS_EPS)
        for h in range(n_heads):
            st[s, h] = states[h]
    o_ref[...] = (o_scr[...] * nw_ref[...] * _silu(hg_ref[...])).astype(BF16)

    @pl.when(tstep == pl.num_programs(1) - 1)
    def _():
        for s in range(n_seq):
            for h in range(n_heads):
                so_ref[s, h] = st[s, h].T


def _seq_layout(n_batch, seq_len, row_off, sample):
    if sample:
        n_seq, n_chunk, C = SAMPLE_SEQS, 1, seq_len
        rows = n_seq * C
        grid = (n_batch // n_seq, 1)
        blk0 = row_off // rows
        rb = lambda i, t: blk0 + i
    else:
        n_seq, n_chunk, C = 1, PROMPT_TILE // SCAN_CHUNK, SCAN_CHUNK
        rows = PROMPT_TILE
        tiles = seq_len // rows
        grid = (n_batch, tiles)
        blk0 = row_off // rows
        rb = lambda i, t: blk0 + i * tiles + t
    return n_seq, n_chunk, C, rows, grid, rb


def _gla_call(mode, h, cols, extra, nw, s0, n_batch, seq_len, row_off, sample):
    n_seq, n_chunk, C, rows, grid, rb = _seq_layout(n_batch, seq_len, row_off, sample)
    colspec = lambda c0, w: pl.BlockSpec((rows, w), lambda i, t: (rb(i, t), c0 // w))
    fix2 = lambda i, t: (0, 0)
    in_specs = [colspec(cols[0], 512), colspec(cols[1], 512), colspec(cols[2], 512), colspec(cols[3], 512)]
    args = [h, h, h, h]
    if mode == "gla":
        wlr, blr = extra
        in_specs += [colspec(cols[4], LANE), pl.BlockSpec((LANE, 512), fix2), pl.BlockSpec((1, 512), fix2)]
        args += [h, wlr, blr]
    else:
        in_specs += [pl.BlockSpec((1, 512), fix2)]
        args += [extra]
    st_spec = pl.BlockSpec((n_seq, 4, HEAD_W, HEAD_W), lambda i, t: (i, 0, 0, 0))
    in_specs += [pl.BlockSpec((1, 512), fix2), st_spec]
    args += [nw, s0]
    t_rows = n_batch * seq_len
    out_blk0 = 0
    o_spec = pl.BlockSpec((rows, 512), lambda i, t: (rb(i, t) - row_off // rows, 0))
    return pl.pallas_call(
        functools.partial(_gla_body, mode, n_seq, n_chunk, C),
        grid=grid,
        in_specs=in_specs,
        out_specs=[o_spec, st_spec],
        out_shape=[jax.ShapeDtypeStruct((t_rows, 512), BF16),
                   jax.ShapeDtypeStruct((n_batch, 4, HEAD_W, HEAD_W), F32)],
        scratch_shapes=[pltpu.VMEM((n_seq, 4, HEAD_W, HEAD_W), F32), pltpu.VMEM((rows, 512), F32)],
        compiler_params=_cparams(2),
        name=mode + ("_sample" if sample else "_prompt"),
    )(*args)


def _round_bf16(x, on=True):
    return x.astype(BF16).astype(F32) if on else x


def _conf_body(n_seq, L, round_x, round_w, a_ref, gt_ref, hist_ref, w_ref, b_ref, g_ref, be_ref, o_ref, co_ref, buf, bufr):
    tstep = pl.program_id(1)
    hist = CONF_WIDTH - 1
    pad = 32 - hist

    @pl.when(tstep == 0)
    def _():
        for s in range(n_seq):
            buf[s, pad:32, :] = hist_ref[s]
            bufr[s, pad:32, :] = _round_bf16(hist_ref[s], round_x)

    u = a_ref[...] * jax.nn.sigmoid(gt_ref[...])
    ur = _round_bf16(u, round_x)
    for s in range(n_seq):
        buf[s, 32:32 + L, :] = u[s * L:(s + 1) * L, :]
        bufr[s, 32:32 + L, :] = ur[s * L:(s + 1) * L, :]
    w = _round_bf16(w_ref[...], round_w)
    for s in range(n_seq):
        acc = jnp.zeros((L, CONF_DIM), F32)
        for j in range(CONF_WIDTH):
            acc = acc + bufr[s, pad + j:pad + j + L, :] * w[j:j + 1, :]
        y = _silu(_layernorm(acc + b_ref[...], g_ref[...], be_ref[...]))
        o_ref[s * L:(s + 1) * L, :] = y.astype(o_ref.dtype)
        tail = buf[s, L + pad:L + 32, :]
        buf[s, pad:32, :] = tail
        tailr = bufr[s, L + pad:L + 32, :]
        bufr[s, pad:32, :] = tailr

    @pl.when(tstep == pl.num_programs(1) - 1)
    def _():
        for s in range(n_seq):
            co_ref[s] = buf[s, pad:32, :]


def _conf_call(h, col_a, col_g, cache, w, b, g, be, n_batch, seq_len, row_off, sample):
    n_seq, n_chunk, C, rows, grid, rb = _seq_layout(n_batch, seq_len, row_off, sample)
    L = rows // n_seq
    hist = CONF_WIDTH - 1
    colspec = lambda c0: pl.BlockSpec((rows, 512), lambda i, t: (rb(i, t), c0 // 512))
    fix2 = lambda i, t: (0, 0)
    c_spec = pl.BlockSpec((n_seq, hist, CONF_DIM), lambda i, t: (i, 0, 0))
    o_dtype = F32 if sample else BF16
    return pl.pallas_call(
        functools.partial(_conf_body, n_seq, L, True, sample),
        grid=grid,
        in_specs=[colspec(col_a), colspec(col_g), c_spec,
                  pl.BlockSpec((CONF_WIDTH, CONF_DIM), fix2), pl.BlockSpec((1, CONF_DIM), fix2),
                  pl.BlockSpec((1, CONF_DIM), fix2), pl.BlockSpec((1, CONF_DIM), fix2)],
        out_specs=[pl.BlockSpec((rows, 512), lambda i, t: (rb(i, t) - row_off // rows, 0)), c_spec],
        out_shape=[jax.ShapeDtypeStruct((n_batch * seq_len, 512), o_dtype),
                   jax.ShapeDtypeStruct((n_batch, hist, CONF_DIM), F32)],
        scratch_shapes=[pltpu.VMEM((n_seq, 32 + L, CONF_DIM), F32)] * 2,
        compiler_params=_cparams(2),
        name="conformer" + ("_sample" if sample else "_prompt"),
    )(h, h, cache, w, b, g, be)


def _ssd_body(n_seq, n_chunk, C, round_x, round_w, hz_ref, hx_ref, hdt_ref, hist_ref, s0_ref, cw_ref, cb_ref, dtb_ref, alog_ref,
              dvec_ref, nw_ref, o_ref, co_ref, so_ref, st, buf, bufr, xbc, y_scr):
    tstep = pl.program_id(1)
    L = n_chunk * C
    hist = SSM_CONV - 1
    pad = 8 - hist
    n_pairs = SSM_HEADS // 2

    @pl.when(tstep == 0)
    def _():
        for s in range(n_seq):
            buf[s, pad:8, :] = hist_ref[s]
            bufr[s, pad:8, :] = _round_bf16(hist_ref[s], round_x)
            for m in range(n_pairs):
                st[s, m] = s0_ref[s, m]

    cw = _round_bf16(cw_ref[...], round_w)
    for s in range(n_seq):
        hx = hx_ref[s * L:(s + 1) * L, :]
        buf[s, 8:8 + L, :] = hx
        bufr[s, 8:8 + L, :] = _round_bf16(hx, round_x)
        acc = jnp.zeros((L, SSM_CONV_DIM), F32)
        for j in range(SSM_CONV):
            acc = acc + bufr[s, pad + j:pad + j + L, :] * cw[j:j + 1, :]
        xbc[s * L:(s + 1) * L, :] = _silu(acc + cb_ref[...])
        tail = buf[s, L + pad:L + 8, :]
        buf[s, pad:8, :] = tail
        tailr = bufr[s, L + pad:L + 8, :]
        bufr[s, pad:8, :] = tailr

    dt = _softplus(hdt_ref[...] + dtb_ref[...])
    la = dt * (-jnp.exp(alog_ref[...]))
    hrow = lax.broadcasted_iota(I32, (LANE, SSM_INNER), 0)
    hcol = lax.broadcasted_iota(I32, (LANE, SSM_INNER), 1) // SSM_HEADDIM
    expand = (hrow == hcol).astype(F32)
    dtx = jnp.dot(dt, expand, preferred_element_type=F32, precision=HIGHEST)
    causal = _tri(C)
    tri = causal.astype(BF16)
    lane = lax.broadcasted_iota(I32, (C, HEAD_W), 1)
    bcol_all = _chunk_cumsum(la, C)
    heads_per_group = SSM_HEADS // SSM_GROUPS
    for s in range(n_seq):
        states = [st[s, m] for m in range(n_pairs)]
        for c in range(n_chunk):
            r0 = (s * n_chunk + c) * C
            rs = slice(r0, r0 + C)
            bcol = bcol_all[rs, :]
            brow = functools.reduce(lambda p, q: p + q, [
                lax.dot_general(part, tri, (((0,), (1,)), ((), ())), preferred_element_type=F32)
                for part in _split3(la[rs, :])])
            xs_c = xbc[rs, 0:SSM_INNER]
            v_c = (xs_c * dtx[rs, :]).astype(BF16)
            gmats, bms, cms = [], [], []
            for grp in range(SSM_GROUPS):
                bm = xbc[rs, SSM_INNER + grp * SSM_STATE:SSM_INNER + (grp + 1) * SSM_STATE]
                cm = xbc[rs, SSM_INNER + (SSM_GROUPS + grp) * SSM_STATE:SSM_INNER + (SSM_GROUPS + grp + 1) * SSM_STATE]
                cm_hi, cm_lo = _split2(cm)
                bm_hi, bm_lo = _split2(bm)
                gmats.append(_dot_nt(jnp.concatenate([cm_hi, cm_hi, cm_lo], axis=1),
                                     jnp.concatenate([bm_hi, bm_lo, bm_hi], axis=1)))
                bms.append(bm)
                cms.append(cm)
            for m in range(n_pairs):
                grp = (2 * m) // heads_per_group
                bm, cm, gmat = bms[grp], cms[grp], gmats[grp]
                ps = slice(m * HEAD_W, (m + 1) * HEAD_W)
                vp = v_c[:, ps]
                s_t = states[m]
                o_halves, new_rows = [], []
                for hh in range(2):
                    hd = 2 * m + hh
                    bc = bcol[:, hd:hd + 1]
                    br = brow[hd:hd + 1, :]
                    dec = jnp.where(causal, jnp.exp(jnp.minimum(bc - br, 0.0)), 0.0)
                    b_last = bcol[C - 1:C, hd:hd + 1]
                    o_halves.append(_dot(gmat * dec, vp) + _dot_nt(cm * jnp.exp(bc), s_t))
                    kv = _dot_tn(vp, bm * jnp.exp(b_last - bc))
                    vs = slice(hh * SSM_HEADDIM, (hh + 1) * SSM_HEADDIM)
                    new_rows.append(s_t[vs, :] * jnp.exp(b_last) + kv[vs, :])
                states[m] = jnp.concatenate(new_rows, axis=0)
                o_pair = jnp.where(lane < SSM_HEADDIM, o_halves[0], o_halves[1])
                y_scr[rs, ps] = o_pair + dvec_ref[:, ps] * xs_c[:, ps]
        for m in range(n_pairs):
            st[s, m] = states[m]
    y = y_scr[...] * _silu(hz_ref[...])
    gw = SSM_INNER // SSM_GROUPS
    for grp in range(SSM_GROUPS):
        gs = slice(grp * gw, (grp + 1) * gw)
        yg = y[:, gs]
        ms = jnp.mean(yg * yg, axis=-1, keepdims=True)
        o_ref[:, gs] = (yg * lax.rsqrt(ms + RMS_EPS) * nw_ref[:, gs]).astype(BF16)

    @pl.when(tstep == pl.num_programs(1) - 1)
    def _():
        for s in range(n_seq):
            co_ref[s] = buf[s, pad:8, :]
            for m in range(n_pairs):
                so_ref[s, m] = st[s, m]


def _ssd_call(h, col_z, col_x, col_dt, cache, s0, cw, cb, dtb, alog, dvec, nw, n_batch, seq_len, row_off, sample):
    n_seq, n_chunk, C, rows, grid, rb = _seq_layout(n_batch, seq_len, row_off, sample)
    L = rows // n_seq
    hist = SSM_CONV - 1
    n_pairs = SSM_HEADS // 2
    colspec = lambda c0, w: pl.BlockSpec((rows, w), lambda i, t: (rb(i, t), c0 // w))
    fix2 = lambda i, t: (0, 0)
    c_spec = pl.BlockSpec((n_seq, hist, SSM_CONV_DIM), lambda i, t: (i, 0, 0))
    st_spec = pl.BlockSpec((n_seq, n_pairs, HEAD_W, SSM_STATE), lambda i, t: (i, 0, 0, 0))
    return pl.pallas_call(
        functools.partial(_ssd_body, n_seq, n_chunk, C, sample, True),
        grid=grid,
        in_specs=[colspec(col_z, 512), colspec(col_x, SSM_CONV_DIM), colspec(col_dt, LANE), c_spec, st_spec,
                  pl.BlockSpec((SSM_CONV, SSM_CONV_DIM), fix2), pl.BlockSpec((1, SSM_CONV_DIM), fix2),
                  pl.BlockSpec((1, LANE), fix2), pl.BlockSpec((1, LANE), fix2),
                  pl.BlockSpec((1, SSM_INNER), fix2), pl.BlockSpec((1, SSM_INNER), fix2)],
        out_specs=[pl.BlockSpec((rows, 512), lambda i, t: (rb(i, t) - row_off // rows, 0)), c_spec, st_spec],
        out_shape=[jax.ShapeDtypeStruct((n_batch * seq_len, 512), BF16),
                   jax.ShapeDtypeStruct((n_batch, hist, SSM_CONV_DIM), F32),
                   jax.ShapeDtypeStruct((n_batch, n_pairs, HEAD_W, SSM_STATE), F32)],
        scratch_shapes=[pltpu.VMEM((n_seq, n_pairs, HEAD_W, SSM_STATE), F32),
                        pltpu.VMEM((n_seq, 8 + L, SSM_CONV_DIM), F32),
                        pltpu.VMEM((n_seq, 8 + L, SSM_CONV_DIM), F32),
                        pltpu.VMEM((rows, SSM_CONV_DIM), F32),
                        pltpu.VMEM((rows, SSM_INNER), F32)],
        compiler_params=_cparams(2),
        name="ssd" + ("_sample" if sample else "_prompt"),
    )(h, h, h, cache, s0, cw, cb, dtb, alog, dvec, nw)


def _pad_heads(w, n_heads, width):
    lead = w.shape[:-1]
    w = w.reshape(lead + (n_heads, width))
    w = jnp.pad(w, [(0, 0)] * len(lead) + [(0, 0), (0, HEAD_W - width)])
    return w.reshape(lead + (n_heads * HEAD_W,))


def _row(v):
    return v.reshape(1, -1).astype(F32)


def kernel(x_prompt, x_sample, state_gla, cache_conformer, state_hgrn, state_ssm, cache_mamba_conv, w_in_even, w_gla_gate_lr, b_gla_gate, gla_norm_w, conf_conv_w, conf_conv_b, conf_ln_g, conf_ln_b, w_out_even, w_in_odd, hgrn_lower_bounds, hgrn_norm_w, mamba_conv_w, mamba_conv_b, mamba_dt_bias, mamba_a_log, mamba_d, mamba_norm_w, w_out_odd, ln1_g, ln1_b, ln2_g, ln2_b, router_w, router_b, expert_w_gate, expert_b_gate, expert_w_up, expert_b_up, expert_w_down, expert_b_down):
    bp, lp, _ = x_prompt.shape
    bs, ls, _ = x_sample.shape
    tp, ts = bp * lp, bs * ls
    x = jnp.concatenate([x_prompt.reshape(tp, D_MODEL), x_sample.reshape(ts, D_MODEL)], axis=0)

    def router_params(layer):
        rw = jnp.pad(router_w[layer], ((0, 0), (0, LANE - N_EXPERTS)))
        rb = jnp.pad(router_b[layer].astype(F32), (0, LANE - N_EXPERTS), constant_values=-1e30)
        return rw, rb.reshape(1, LANE)

    def finish_layer(layer, x, mix_a, mix_b, w_out):
        rw, rb = router_params(layer)
        x1, xp, gates, eidx, rank, counts = _outproj_ln_router(
            mix_a, mix_b, x, w_out[:512].astype(BF16), w_out[512:].astype(BF16),
            _row(ln1_g[layer]), _row(ln1_b[layer]), rw, rb)
        return _moe(layer, x1, xp, gates, eidx, rank, counts, _row(ln2_g[layer]), _row(ln2_b[layer]),
                    expert_w_gate, expert_b_gate, expert_w_up, expert_b_up, expert_w_down, expert_b_down)

    wi = w_in_even[0]
    wq, wk, wv, wg, wlr, wglu = jnp.split(wi, [256, 512, 1024, 1536, 1552], axis=1)
    w_even = jnp.concatenate([_pad_heads(wq, GLA_HEADS, GLA_DK), _pad_heads(wk, GLA_HEADS, GLA_DK), wv, wg, wglu,
                              jnp.pad(wlr, ((0, 0), (0, LANE - GLA_RANK)))], axis=1).astype(BF16)
    cols_gla = (0, 512, 1024, 1536, 3072)
    col_a, col_gate = 2048, 2560
    h = _inproj(x, w_even)
    wlr_p = jnp.pad(_pad_heads(w_gla_gate_lr[0], GLA_HEADS, GLA_DK), ((0, LANE - GLA_RANK), (0, 0)))
    blr_p = _row(_pad_heads(b_gla_gate[0], GLA_HEADS, GLA_DK))
    nw = _row(gla_norm_w[0])
    conf_args = (conf_conv_w[0], _row(conf_conv_b[0]), _row(conf_ln_g[0]), _row(conf_ln_b[0]))
    s0_p = jnp.zeros((bp, GLA_HEADS, HEAD_W, HEAD_W), F32)
    s0_s = jnp.pad(state_gla[0], ((0, 0), (0, 0), (0, HEAD_W - GLA_DK), (0, 0)))
    oa_p, sg_p = _gla_call("gla", h, cols_gla, (wlr_p, blr_p), nw, s0_p, bp, lp, 0, False)
    oa_s, sg_s = _gla_call("gla", h, cols_gla, (wlr_p, blr_p), nw, s0_s, bs, ls, tp, True)
    ob_p, cc_p = _conf_call(h, col_a, col_gate, jnp.zeros((bp,) + cache_conformer.shape[2:], F32), *conf_args,
                            bp, lp, 0, False)
    ob_s, cc_s = _conf_call(h, col_a, col_gate, cache_conformer[0], *conf_args, bs, ls, tp, True)
    mix_a = jnp.concatenate([oa_p, oa_s], axis=0)
    mix_b = jnp.concatenate([ob_p, ob_s.astype(BF16)], axis=0)
    x = finish_layer(0, x, mix_a, mix_b, w_out_even[0])
    gla_p, gla_s = sg_p[:, :, :GLA_DK, :][None], sg_s[:, :, :GLA_DK, :][None]
    conf_p, conf_s = cc_p[None], cc_s[None]

    lb_cum = jnp.cumsum(jax.nn.softmax(hgrn_lower_bounds.astype(F32), axis=0), axis=0)
    lower_bound = _row((lb_cum - lb_cum[0])[1])
    wo = w_in_odd[0]
    w_odd = jnp.concatenate([wo[:, 2560:3584], wo[:, :2560],
                             jnp.pad(wo[:, 3584:], ((0, 0), (0, LANE - SSM_HEADS)))], axis=1).astype(BF16)
    h = _inproj(x, w_odd)
    cols_hgrn = (1024, 1536, 2048, 2560)
    col_z, col_x, col_dt = 3072, 0, 3584
    nw = _row(hgrn_norm_w[0])
    oc_p, sh_p = _gla_call("hgrn", h, cols_hgrn, lower_bound, nw,
                           jnp.zeros((bp, HGRN_HEADS, HEAD_W, HEAD_W), F32), bp, lp, 0, False)
    oc_s, sh_s = _gla_call("hgrn", h, cols_hgrn, lower_bound, nw, state_hgrn[0], bs, ls, tp, True)

    def pair_states(s):
        return jnp.swapaxes(s, 2, 3).reshape(s.shape[0], SSM_HEADS // 2, HEAD_W, SSM_STATE)

    def unpair_states(s):
        return jnp.swapaxes(s.reshape(s.shape[0], SSM_HEADS, SSM_HEADDIM, SSM_STATE), 2, 3)

    pad8 = lambda v: jnp.pad(v.astype(F32), (0, LANE - SSM_HEADS)).reshape(1, LANE)
    ssd_args = (mamba_conv_w[0], _row(mamba_conv_b[0]), pad8(mamba_dt_bias[0]), pad8(mamba_a_log[0]),
                _row(jnp.repeat(mamba_d[0], SSM_HEADDIM)), _row(mamba_norm_w[0]))
    od_p, cm_p, ss_p = _ssd_call(h, col_z, col_x, col_dt, jnp.zeros((bp,) + cache_mamba_conv.shape[2:], F32),
                                 jnp.zeros((bp, SSM_HEADS // 2, HEAD_W, SSM_STATE), F32), *ssd_args,
                                 bp, lp, 0, False)
    od_s, cm_s, ss_s = _ssd_call(h, col_z, col_x, col_dt, cache_mamba_conv[0], pair_states(state_ssm[0]),
                                 *ssd_args, bs, ls, tp, True)
    mix_a = jnp.concatenate([oc_p, oc_s], axis=0)
    mix_b = jnp.concatenate([od_p, od_s], axis=0)
    x = finish_layer(1, x, mix_a, mix_b, w_out_odd[0])

    y_prompt = x[:tp].reshape(bp, lp, D_MODEL)
    y_sample = x[tp:].reshape(bs, ls, D_MODEL)
    return (y_prompt, y_sample, gla_p, gla_s, conf_p, conf_s, sh_p[None], sh_s[None],
            unpair_states(ss_p)[None], unpair_states(ss_s)[None], cm_p[None], cm_s[None])
```

```python
import functools

import jax
import jax.numpy as jnp
from jax import lax
from jax.experimental import pallas as pl
from jax.experimental.pallas import tpu as pltpu
from jax.experimental.pallas import tpu_sc as plsc

F32 = jnp.float32
BF16 = jnp.bfloat16
I32 = jnp.int32
U32 = jnp.uint32
HIGHEST = lax.Precision.HIGHEST

D_MODEL = 1024
DEPTH = 2
DEEPNORM_ALPHA = (2.0 * DEPTH) ** 0.25
LN_EPS = 1e-5
RMS_EPS = 1e-6
LANE = 128
HEAD_W = 128
GLA_HEADS, GLA_DK, GLA_RANK, GLA_TAU = 4, 64, 16, 16.0
CONF_DIM, CONF_WIDTH = 512, 31
HGRN_HEADS, HGRN_DK = 4, 128
SSM_HEADS, SSM_HEADDIM, SSM_STATE, SSM_GROUPS, SSM_CONV = 8, 64, 128, 2, 4
SSM_INNER = SSM_HEADS * SSM_HEADDIM
SSM_CONV_DIM = SSM_INNER + 2 * SSM_GROUPS * SSM_STATE
N_EXPERTS, TOP_K = 32, 4
SWIGLU_ALPHA, SWIGLU_LIMIT = 1.702, 7.0
SCAN_CHUNK = 64
PROMPT_TILE = 256
SAMPLE_SEQS = 16
TOKEN_TILE = 512
MOE_ROWS = 512
SC_CORES, SC_SUBCORES = 2, 16
SC_ROWS = 64
VMEM_LIMIT = 56 * 1024 * 1024


def _cparams(n_axes):
    return pltpu.CompilerParams(dimension_semantics=("arbitrary",) * n_axes, vmem_limit_bytes=VMEM_LIMIT)


def _silu(x):
    return x * jax.nn.sigmoid(x)


def _softplus(x):
    return jnp.maximum(x, 0.0) + jnp.log(1.0 + jnp.exp(-jnp.abs(x)))


def _log_sigmoid(x):
    return jnp.minimum(x, 0.0) - jnp.log(1.0 + jnp.exp(-jnp.abs(x)))


def _layernorm(y, g, b):
    mu = jnp.mean(y, axis=-1, keepdims=True)
    d = y - mu
    var = jnp.mean(d * d, axis=-1, keepdims=True)
    return d * lax.rsqrt(var + LN_EPS) * g + b


def _dot(a, b):
    return jnp.dot(a.astype(BF16), b.astype(BF16), preferred_element_type=F32)


def _dot_nt(a, b):
    return lax.dot_general(a.astype(BF16), b.astype(BF16), (((1,), (1,)), ((), ())), preferred_element_type=F32)


def _dot_tn(a, b):
    return lax.dot_general(a.astype(BF16), b.astype(BF16), (((0,), (0,)), ((), ())), preferred_element_type=F32)


def _tri(c):
    r = lax.broadcasted_iota(I32, (c, c), 0)
    k = lax.broadcasted_iota(I32, (c, c), 1)
    return r >= k


def _inproj_body(x_ref, w_ref, o_ref):
    xb = x_ref[...].astype(BF16)
    n = w_ref.shape[1]
    for c0 in range(0, n, 512):
        c1 = min(c0 + 512, n)
        o_ref[:, c0:c1] = jnp.dot(xb, w_ref[:, c0:c1], preferred_element_type=F32)


def _inproj(x, w):
    t, k = x.shape
    n = w.shape[1]
    return pl.pallas_call(
        _inproj_body,
        grid=(t // TOKEN_TILE,),
        in_specs=[pl.BlockSpec((TOKEN_TILE, k), lambda i: (i, 0)),
                  pl.BlockSpec((k, n), lambda i: (0, 0))],
        out_specs=pl.BlockSpec((TOKEN_TILE, n), lambda i: (i, 0)),
        out_shape=jax.ShapeDtypeStruct((t, n), F32),
        compiler_params=_cparams(1),
        name="inproj",
    )(x, w)


def _pack_halves(y):
    half = y.shape[1] // 2
    hi = lax.bitcast_convert_type(y[:, :half].astype(BF16).astype(F32), U32)
    lo = lax.bitcast_convert_type(y[:, half:].astype(BF16).astype(F32), U32)
    return (hi & jnp.uint32(0xFFFF0000)) | (lo >> 16)


def _unpack_halves(w):
    hi = lax.bitcast_convert_type(w & jnp.uint32(0xFFFF0000), F32)
    lo = lax.bitcast_convert_type(w << 16, F32)
    return hi, lo


def _outproj_body(a_ref, b_ref, x_ref, wa_ref, wb_ref, g_ref, be_ref, rw_ref, rb_ref,
                  x1_ref, xp_ref, gate_ref, idx_ref, rank_ref, cnt_ref, carry):
    @pl.when(pl.program_id(0) == 0)
    def _():
        carry[...] = jnp.zeros(carry.shape, F32)

    mix = (jnp.dot(a_ref[...], wa_ref[...], preferred_element_type=F32)
           + jnp.dot(b_ref[...], wb_ref[...], preferred_element_type=F32))
    x1 = _layernorm(DEEPNORM_ALPHA * x_ref[...] + mix, g_ref[...], be_ref[...])
    x1_ref[...] = x1
    xp_ref[...] = _pack_halves(x1)
    logits = _dot(x1, rw_ref[...]) + rb_ref[...]
    lane = lax.broadcasted_iota(I32, logits.shape, 1)
    vals, idxs = [], []
    for _ in range(TOP_K):
        m = jnp.max(logits, axis=-1, keepdims=True)
        sel = jnp.min(jnp.where(logits == m, lane, LANE), axis=-1, keepdims=True)
        vals.append(m)
        idxs.append(sel)
        logits = jnp.where(lane == sel, -jnp.inf, logits)
    exps = [jnp.exp(v - vals[0]) for v in vals]
    inv = 1.0 / functools.reduce(lambda p, q: p + q, exps)
    gates = jnp.zeros(logits.shape, F32)
    eidx = jnp.zeros(logits.shape, I32)
    for k in range(TOP_K):
        gates = jnp.where(lane == k, exps[k] * inv, gates)
        eidx = jnp.where(lane == k, idxs[k], eidx)
    gate_ref[...] = gates
    idx_ref[...] = eidx
    tm = logits.shape[0]
    chosen = jnp.zeros(logits.shape, F32)
    for k in range(TOP_K):
        chosen = chosen + (lane == idxs[k]).astype(F32)
    earlier = lax.broadcasted_iota(I32, (tm, tm), 0) > lax.broadcasted_iota(I32, (tm, tm), 1)
    before = carry[...] + jnp.dot(earlier.astype(BF16), chosen.astype(BF16), preferred_element_type=F32)
    ranks = jnp.zeros(logits.shape, F32)
    for k in range(TOP_K):
        rk = jnp.sum(jnp.where(lane == idxs[k], before, 0.0), axis=-1, keepdims=True)
        ranks = jnp.where(lane == k, rk, ranks)
    rank_ref[...] = ranks.astype(I32)
    carry[...] = carry[...] + jnp.sum(chosen, axis=0, keepdims=True)
    cnt_ref[...] = carry[...].astype(I32)


def _outproj_ln_router(a, b, x, wa, wb, g, be, rw, rb):
    t = x.shape[0]
    tm = TOKEN_TILE
    row = lambda i: (i, 0)
    fix = lambda i: (0, 0)
    return pl.pallas_call(
        _outproj_body,
        grid=(t // tm,),
        in_specs=[pl.BlockSpec((tm, 512), row), pl.BlockSpec((tm, 512), row), pl.BlockSpec((tm, D_MODEL), row),
                  pl.BlockSpec((512, D_MODEL), fix), pl.BlockSpec((512, D_MODEL), fix),
                  pl.BlockSpec((1, D_MODEL), fix), pl.BlockSpec((1, D_MODEL), fix),
                  pl.BlockSpec((D_MODEL, LANE), fix), pl.BlockSpec((1, LANE), fix)],
        out_specs=[pl.BlockSpec((tm, D_MODEL), row), pl.BlockSpec((tm, 512), row),
                   pl.BlockSpec((tm, LANE), row), pl.BlockSpec((tm, LANE), row), pl.BlockSpec((tm, LANE), row),
                   pl.BlockSpec((1, LANE), fix)],
        out_shape=[jax.ShapeDtypeStruct((t, D_MODEL), F32), jax.ShapeDtypeStruct((t, 512), U32),
                   jax.ShapeDtypeStruct((t, LANE), F32), jax.ShapeDtypeStruct((t, LANE), I32),
                   jax.ShapeDtypeStruct((t, LANE), I32), jax.ShapeDtypeStruct((1, LANE), I32)],
        scratch_shapes=[pltpu.VMEM((1, LANE), F32)],
        compiler_params=_cparams(1),
        name="outproj_ln_router",
    )(a, b, x, wa, wb, g, be, rw, rb)


def _sc_worker_rows(n):
    per_worker = n // (SC_CORES * SC_SUBCORES)
    worker = lax.axis_index("s") * SC_CORES + lax.axis_index("c")
    return worker * per_worker, per_worker // SC_ROWS


def _sc_mesh():
    return plsc.VectorSubcoreMesh(core_axis_name="c", subcore_axis_name="s")


def _sc_scatter_rows(src, idx, n_out):
    n, (n_src, w) = idx.shape[0], src.shape
    assert n % (SC_CORES * SC_SUBCORES * SC_ROWS) == 0 and n_src % SC_ROWS == 0

    @functools.partial(
        pl.kernel, mesh=_sc_mesh(), out_type=jax.ShapeDtypeStruct((n_out, w), src.dtype),
        scratch_types=[pltpu.VMEM((SC_ROWS,), I32), pltpu.VMEM((SC_ROWS, w), src.dtype), pltpu.SemaphoreType.DMA])
    def scatter(src_hbm, idx_hbm, out_hbm, idx_v, rows_v, sem):
        base, chunks = _sc_worker_rows(n)

        @pl.loop(0, chunks)
        def _(c):
            off = pl.multiple_of(base + c * SC_ROWS, 8)
            pltpu.sync_copy(idx_hbm.at[pl.ds(off, SC_ROWS)], idx_v)
            pltpu.sync_copy(src_hbm.at[pl.ds(pl.multiple_of(lax.rem(off, n_src), 8), SC_ROWS)], rows_v)
            pltpu.async_copy(rows_v, out_hbm.at[idx_v], sem).wait()

    return scatter(src, idx)


def _sc_gather_rows(table, idx):
    n, w = idx.shape[0], table.shape[1]
    assert n % (SC_CORES * SC_SUBCORES * SC_ROWS) == 0

    @functools.partial(
        pl.kernel, mesh=_sc_mesh(), out_type=jax.ShapeDtypeStruct((n, w), table.dtype),
        scratch_types=[pltpu.VMEM((SC_ROWS,), I32), pltpu.VMEM((SC_ROWS, w), table.dtype), pltpu.SemaphoreType.DMA])
    def gather(table_hbm, idx_hbm, out_hbm, idx_v, rows_v, sem):
        base, chunks = _sc_worker_rows(n)

        @pl.loop(0, chunks)
        def _(c):
            off = pl.multiple_of(base + c * SC_ROWS, 8)
            pltpu.sync_copy(idx_hbm.at[pl.ds(off, SC_ROWS)], idx_v)
            pltpu.async_copy(table_hbm.at[idx_v], rows_v, sem).wait()
            pltpu.sync_copy(rows_v, out_hbm.at[pl.ds(off, SC_ROWS)])

    return gather(table, idx)


def _experts_body(be_ref, nv_ref, xs_ref, wg_ref, bg_ref, wu_ref, bu_ref, wd_ref, bd_ref, o_ref, wg_s, wu_s, wd_s):
    i = pl.program_id(0)
    prev = be_ref[jnp.maximum(i - 1, 0)]
    valid = nv_ref[i]
    half_rows = MOE_ROWS // 2

    @pl.when((valid > 0) & ((i == 0) | (be_ref[i] != prev)))
    def _():
        wg_s[...] = wg_ref[...].astype(BF16)
        wu_s[...] = wu_ref[...].astype(BF16)
        wd_s[...] = wd_ref[...].astype(BF16)

    def compute(rows):
        half = D_MODEL // 2
        x_hi, x_lo = _unpack_halves(xs_ref[0:rows, :])
        x_hi = x_hi.astype(BF16)
        x_lo = x_lo.astype(BF16)
        g = (jnp.dot(x_hi, wg_s[:half, :], preferred_element_type=F32)
             + jnp.dot(x_lo, wg_s[half:, :], preferred_element_type=F32) + bg_ref[...])
        u = (jnp.dot(x_hi, wu_s[:half, :], preferred_element_type=F32)
             + jnp.dot(x_lo, wu_s[half:, :], preferred_element_type=F32) + bu_ref[...])
        g = jnp.minimum(g, SWIGLU_LIMIT)
        u = jnp.clip(u, -SWIGLU_LIMIT, SWIGLU_LIMIT)
        hmid = (u + 1.0) * (g * jax.nn.sigmoid(SWIGLU_ALPHA * g))
        out = jnp.dot(hmid.astype(BF16), wd_s[...], preferred_element_type=F32) + bd_ref[...]
        o_ref[0:rows, :] = _pack_halves(out)

    @pl.when(valid > half_rows)
    def _():
        compute(MOE_ROWS)

    @pl.when((valid > 0) & (valid <= half_rows))
    def _():
        compute(half_rows)
        o_ref[half_rows:, :] = jnp.zeros((MOE_ROWS - half_rows, o_ref.shape[1]), o_ref.dtype)

    @pl.when(valid == 0)
    def _():
        o_ref[...] = jnp.zeros(o_ref.shape, o_ref.dtype)


def _experts(layer, block_e, block_valid, xs, wg, bg, wu, bu, wd, bd):
    n_rows = xs.shape[0]
    bm = MOE_ROWS
    row = lambda i, be, nu: (i, 0)
    wsel = lambda i, be, nu: (layer, be[i], 0, 0)
    wspec = pl.BlockSpec((None, None, D_MODEL, D_MODEL), wsel)
    bspec = pl.BlockSpec((None, None, 1, D_MODEL), wsel)
    bias = lambda b: b.reshape(b.shape[0], b.shape[1], 1, b.shape[2])
    return pl.pallas_call(
        _experts_body,
        grid_spec=pltpu.PrefetchScalarGridSpec(
            num_scalar_prefetch=2,
            grid=(n_rows // bm,),
            in_specs=[pl.BlockSpec((bm, 512), row), wspec, bspec, wspec, bspec, wspec, bspec],
            out_specs=pl.BlockSpec((bm, 512), row),
            scratch_shapes=[pltpu.VMEM((D_MODEL, D_MODEL), BF16)] * 3,
        ),
        out_shape=jax.ShapeDtypeStruct((n_rows, 512), U32),
        compiler_params=_cparams(1),
        name="experts",
    )(block_e, block_valid, xs, wg, bias(bg), wu, bias(bu), wd, bias(bd))


def _combine_body(o0_ref, o1_ref, o2_ref, o3_ref, gt_ref, x_ref, g_ref, b_ref, y_ref):
    half = D_MODEL // 2
    gates = gt_ref[...]
    hi = jnp.zeros((x_ref.shape[0], half), F32)
    lo = jnp.zeros((x_ref.shape[0], half), F32)
    for k, o_ref in enumerate((o0_ref, o1_ref, o2_ref, o3_ref)):
        h, l = _unpack_halves(o_ref[...])
        gk = gates[:, k:k + 1]
        hi = hi + gk * h
        lo = lo + gk * l
    x = x_ref[...]
    y_hi = DEEPNORM_ALPHA * x[:, :half] + hi
    y_lo = DEEPNORM_ALPHA * x[:, half:] + lo
    mu = (jnp.sum(y_hi, axis=-1, keepdims=True) + jnp.sum(y_lo, axis=-1, keepdims=True)) * (1.0 / D_MODEL)
    d_hi = y_hi - mu
    d_lo = y_lo - mu
    var = (jnp.sum(d_hi * d_hi, axis=-1, keepdims=True) + jnp.sum(d_lo * d_lo, axis=-1, keepdims=True)) * (1.0 / D_MODEL)
    r = lax.rsqrt(var + LN_EPS)
    y_ref[:, :half] = d_hi * r * g_ref[:, :half] + b_ref[:, :half]
    y_ref[:, half:] = d_lo * r * g_ref[:, half:] + b_ref[:, half:]


def _combine_ln(o4, gates, x, g, b):
    t = x.shape[0]
    tm = TOKEN_TILE
    row = lambda i: (i, 0)
    fix = lambda i: (0, 0)
    choice = lambda k: pl.BlockSpec((tm, 512), lambda i: (k * (t // tm) + i, 0))
    return pl.pallas_call(
        _combine_body,
        grid=(t // tm,),
        in_specs=[choice(0), choice(1), choice(2), choice(3), pl.BlockSpec((tm, LANE), row),
                  pl.BlockSpec((tm, D_MODEL), row), pl.BlockSpec((1, D_MODEL), fix), pl.BlockSpec((1, D_MODEL), fix)],
        out_specs=pl.BlockSpec((tm, D_MODEL), row),
        out_shape=jax.ShapeDtypeStruct((t, D_MODEL), F32),
        compiler_params=_cparams(1),
        name="combine_ln",
    )(o4, o4, o4, o4, gates, x, g, b)


def _moe(layer, x1, xp, gates, eidx, rank, counts, ln_g, ln_b, wg, bg, wu, bu, wd, bd):
    t = x1.shape[0]
    bm = MOE_ROWS
    n_blocks = t * TOP_K // bm + N_EXPERTS
    n_rows = n_blocks * bm
    cnt = counts[0, :N_EXPERTS]
    padded = (cnt + bm - 1) // bm * bm
    pad_end = jnp.cumsum(padded)
    pad_start = pad_end - padded
    e = eidx[:, :TOP_K]
    start = jnp.sum(jnp.where(e[:, :, None] == jnp.arange(N_EXPERTS, dtype=I32), pad_start, 0), axis=-1)
    dest = (start + rank[:, :TOP_K]).T.reshape(-1)
    n_used = (pad_end[-1] // bm).astype(I32)
    blk = jnp.arange(n_blocks, dtype=I32)
    first_row = jnp.minimum(blk, n_used - 1) * bm
    block_e = jnp.minimum(jnp.sum((pad_end[None, :] <= first_row[:, None]).astype(I32), axis=1), N_EXPERTS - 1)
    is_e = block_e[:, None] == jnp.arange(N_EXPERTS, dtype=I32)
    real_end = jnp.sum(jnp.where(is_e, pad_start + cnt, 0), axis=1)
    block_valid = jnp.where(blk < n_used, jnp.clip(real_end - blk * bm, 0, bm), 0).astype(I32)
    xs = _sc_scatter_rows(xp, dest, n_rows)
    outs = _experts(layer, block_e, block_valid, xs, wg, bg, wu, bu, wd, bd)
    o4 = _sc_gather_rows(outs, dest)
    return _combine_ln(o4, gates, x1, ln_g, ln_b)


def _split2(x):
    hi = x.astype(BF16)
    return hi, (x - hi.astype(F32)).astype(BF16)


def _split3(x):
    hi = x.astype(BF16)
    rem = x - hi.astype(F32)
    mid = rem.astype(BF16)
    return hi, mid, (rem - mid.astype(F32)).astype(BF16)


def _chunk_cumsum(g, C):
    rows = g.shape[0]
    r = lax.broadcasted_iota(I32, (rows, rows), 0)
    c = lax.broadcasted_iota(I32, (rows, rows), 1)
    tri = ((r >= c) & (r // C == c // C)).astype(BF16)
    hi, mid, lo = _split3(g)
    dot = lambda part: jnp.dot(tri, part, preferred_element_type=F32)
    return dot(hi) + dot(mid) + dot(lo)


def _gla_body(mode, n_seq, n_chunk, C, *refs):
    if mode == "gla":
        hq_ref, hk_ref, hv_ref, hg_ref, hlr_ref, wlr_ref, blr_ref, nw_ref, s0_ref, o_ref, so_ref, st, o_scr = refs
    else:
        hq_ref, hk_ref, hv_ref, hg_ref, lb_ref, nw_ref, s0_ref, o_ref, so_ref, st, o_scr = refs
    n_heads = 4
    tstep = pl.program_id(1)

    @pl.when(tstep == 0)
    def _():
        for s in range(n_seq):
            for h in range(n_heads):
                st[s, h] = s0_ref[s, h].T

    if mode == "gla":
        q = hq_ref[...] * (GLA_DK ** -0.5)
        k = hk_ref[...]
        z = _dot(hlr_ref[...], wlr_ref[...]) + blr_ref[...]
        g = _log_sigmoid(z) * (1.0 / GLA_TAU)
    else:
        q = _silu(hq_ref[...]) * (HGRN_DK ** -0.5)
        lb = lb_ref[...]
        f = lb + (1.0 - lb) * jax.nn.sigmoid(hk_ref[...])
        k = 1.0 - f
        g = jnp.log(f)
    v = hv_ref[...]
    causal = _tri(C)
    mid = max(C // 2 - 1, 0)
    b_all = _chunk_cumsum(g, C)
    for s in range(n_seq):
        states = [st[s, h] for h in range(n_heads)]
        for c in range(n_chunk):
            r0 = (s * n_chunk + c) * C
            rs = slice(r0, r0 + C)
            b, qc, kc = b_all[rs, :], q[rs, :], k[rs, :]
            b_last = b[C - 1:C, :]
            b_mid = b[mid:mid + 1, :]
            qe_hi, qe_lo = _split2(qc * jnp.exp(b - b_mid))
            ke_hi, ke_lo = _split2(kc * jnp.exp(b_mid - b))
            q_state = (qc * jnp.exp(b)).astype(BF16)
            k_state = (kc * jnp.exp(b_last - b)).astype(BF16)
            decay = jnp.exp(b_last)
            for h in range(n_heads):
                cs = slice(h * HEAD_W, (h + 1) * HEAD_W)
                lhs = jnp.concatenate([qe_hi[:, cs], qe_hi[:, cs], qe_lo[:, cs]], axis=1)
                rhs = jnp.concatenate([ke_hi[:, cs], ke_lo[:, cs], ke_hi[:, cs]], axis=1)
                scores = jnp.where(causal, _dot_nt(lhs, rhs), 0.0)
                vh = v[rs, cs].astype(BF16)
                o = _dot(scores, vh) + _dot_nt(q_state[:, cs], states[h])
                states[h] = states[h] * decay[:, cs] + _dot_tn(vh, k_state[:, cs])
                ms = jnp.mean(o * o, axis=-1, keepdims=True)
                o_scr[rs, cs] = o * lax.rsqrt(ms + RMS_EPS)
        for h in range(n_heads):
            st[s, h] = states[h]
    o_ref[...] = (o_scr[...] * nw_ref[...] * _silu(hg_ref[...])).astype(BF16)

    @pl.when(tstep == pl.num_programs(1) - 1)
    def _():
        for s in range(n_seq):
            for h in range(n_heads):
                so_ref[s, h] = st[s, h].T


def _seq_layout(n_batch, seq_len, row_off, sample):
    if sample:
        n_seq, n_chunk, C = SAMPLE_SEQS, 1, seq_len
        rows = n_seq * C
        grid = (n_batch // n_seq, 1)
        blk0 = row_off // rows
        rb = lambda i, t: blk0 + i
    else:
        n_seq, n_chunk, C = 1, PROMPT_TILE // SCAN_CHUNK, SCAN_CHUNK
        rows = PROMPT_TILE
        tiles = seq_len // rows
        grid = (n_batch, tiles)
        blk0 = row_off // rows
        rb = lambda i, t: blk0 + i * tiles + t
    return n_seq, n_chunk, C, rows, grid, rb


def _gla_call(mode, h, cols, extra, nw, s0, n_batch, seq_len, row_off, sample):
    n_seq, n_chunk, C, rows, grid, rb = _seq_layout(n_batch, seq_len, row_off, sample)
    colspec = lambda c0, w: pl.BlockSpec((rows, w), lambda i, t: (rb(i, t), c0 // w))
    fix2 = lambda i, t: (0, 0)
    in_specs = [colspec(cols[0], 512), colspec(cols[1], 512), colspec(cols[2], 512), colspec(cols[3], 512)]
    args = [h, h, h, h]
    if mode == "gla":
        wlr, blr = extra
        in_specs += [colspec(cols[4], LANE), pl.BlockSpec((LANE, 512), fix2), pl.BlockSpec((1, 512), fix2)]
        args += [h, wlr, blr]
    else:
        in_specs += [pl.BlockSpec((1, 512), fix2)]
        args += [extra]
    st_spec = pl.BlockSpec((n_seq, 4, HEAD_W, HEAD_W), lambda i, t: (i, 0, 0, 0))
    in_specs += [pl.BlockSpec((1, 512), fix2), st_spec]
    args += [nw, s0]
    t_rows = n_batch * seq_len
    out_blk0 = 0
    o_spec = pl.BlockSpec((rows, 512), lambda i, t: (rb(i, t) - row_off // rows, 0))
    return pl.pallas_call(
        functools.partial(_gla_body, mode, n_seq, n_chunk, C),
        grid=grid,
        in_specs=in_specs,
        out_specs=[o_spec, st_spec],
        out_shape=[jax.ShapeDtypeStruct((t_rows, 512), BF16),
                   jax.ShapeDtypeStruct((n_batch, 4, HEAD_W, HEAD_W), F32)],
        scratch_shapes=[pltpu.VMEM((n_seq, 4, HEAD_W, HEAD_W), F32), pltpu.VMEM((rows, 512), F32)],
        compiler_params=_cparams(2),
        name=mode + ("_sample" if sample else "_prompt"),
    )(*args)


def _round_bf16(x, on=True):
    return x.astype(BF16).astype(F32) if on else x


def _conf_body(n_seq, L, round_x, round_w, a_ref, gt_ref, hist_ref, w_ref, b_ref, g_ref, be_ref, o_ref, co_ref, buf, bufr):
    tstep = pl.program_id(1)
    hist = CONF_WIDTH - 1
    pad = 32 - hist

    @pl.when(tstep == 0)
    def _():
        for s in range(n_seq):
            buf[s, pad:32, :] = hist_ref[s]
            bufr[s, pad:32, :] = _round_bf16(hist_ref[s], round_x)

    u = a_ref[...] * jax.nn.sigmoid(gt_ref[...])
    ur = _round_bf16(u, round_x)
    for s in range(n_seq):
        buf[s, 32:32 + L, :] = u[s * L:(s + 1) * L, :]
        bufr[s, 32:32 + L, :] = ur[s * L:(s + 1) * L, :]
    w = _round_bf16(w_ref[...], round_w)
    for s in range(n_seq):
        acc = jnp.zeros((L, CONF_DIM), F32)
        for j in range(CONF_WIDTH):
            acc = acc + bufr[s, pad + j:pad + j + L, :] * w[j:j + 1, :]
        y = _silu(_layernorm(acc + b_ref[...], g_ref[...], be_ref[...]))
        o_ref[s * L:(s + 1) * L, :] = y.astype(o_ref.dtype)
        tail = buf[s, L + pad:L + 32, :]
        buf[s, pad:32, :] = tail
        tailr = bufr[s, L + pad:L + 32, :]
        bufr[s, pad:32, :] = tailr

    @pl.when(tstep == pl.num_programs(1) - 1)
    def _():
        for s in range(n_seq):
            co_ref[s] = buf[s, pad:32, :]


def _conf_call(h, col_a, col_g, cache, w, b, g, be, n_batch, seq_len, row_off, sample):
    n_seq, n_chunk, C, rows, grid, rb = _seq_layout(n_batch, seq_len, row_off, sample)
    L = rows // n_seq
    hist = CONF_WIDTH - 1
    colspec = lambda c0: pl.BlockSpec((rows, 512), lambda i, t: (rb(i, t), c0 // 512))
    fix2 = lambda i, t: (0, 0)
    c_spec = pl.BlockSpec((n_seq, hist, CONF_DIM), lambda i, t: (i, 0, 0))
    o_dtype = F32 if sample else BF16
    return pl.pallas_call(
        functools.partial(_conf_body, n_seq, L, True, sample),
        grid=grid,
        in_specs=[colspec(col_a), colspec(col_g), c_spec,
                  pl.BlockSpec((CONF_WIDTH, CONF_DIM), fix2), pl.BlockSpec((1, CONF_DIM), fix2),
                  pl.BlockSpec((1, CONF_DIM), fix2), pl.BlockSpec((1, CONF_DIM), fix2)],
        out_specs=[pl.BlockSpec((rows, 512), lambda i, t: (rb(i, t) - row_off // rows, 0)), c_spec],
        out_shape=[jax.ShapeDtypeStruct((n_batch * seq_len, 512), o_dtype),
                   jax.ShapeDtypeStruct((n_batch, hist, CONF_DIM), F32)],
        scratch_shapes=[pltpu.VMEM((n_seq, 32 + L, CONF_DIM), F32)] * 2,
        compiler_params=_cparams(2),
        name="conformer" + ("_sample" if sample else "_prompt"),
    )(h, h, cache, w, b, g, be)


def _ssd_body(n_seq, n_chunk, C, round_x, round_w, hz_ref, hx_ref, hdt_ref, hist_ref, s0_ref, cw_ref, cb_ref, dtb_ref, alog_ref,
              dvec_ref, nw_ref, o_ref, co_ref, so_ref, st, buf, bufr, xbc, y_scr):
    tstep = pl.program_id(1)
    L = n_chunk * C
    hist = SSM_CONV - 1
    pad = 8 - hist
    n_pairs = SSM_HEADS // 2

    @pl.when(tstep == 0)
    def _():
        for s in range(n_seq):
            buf[s, pad:8, :] = hist_ref[s]
            bufr[s, pad:8, :] = _round_bf16(hist_ref[s], round_x)
            for m in range(n_pairs):
                st[s, m] = s0_ref[s, m]

    cw = _round_bf16(cw_ref[...], round_w)
    for s in range(n_seq):
        hx = hx_ref[s * L:(s + 1) * L, :]
        buf[s, 8:8 + L, :] = hx
        bufr[s, 8:8 + L, :] = _round_bf16(hx, round_x)
        acc = jnp.zeros((L, SSM_CONV_DIM), F32)
        for j in range(SSM_CONV):
            acc = acc + bufr[s, pad + j:pad + j + L, :] * cw[j:j + 1, :]
        xbc[s * L:(s + 1) * L, :] = _silu(acc + cb_ref[...])
        tail = buf[s, L + pad:L + 8, :]
        buf[s, pad:8, :] = tail
        tailr = bufr[s, L + pad:L + 8, :]
        bufr[s, pad:8, :] = tailr

    dt = _softplus(hdt_ref[...] + dtb_ref[...])
    la = dt * (-jnp.exp(alog_ref[...]))
    hrow = lax.broadcasted_iota(I32, (LANE, SSM_INNER), 0)
    hcol = lax.broadcasted_iota(I32, (LANE, SSM_INNER), 1) // SSM_HEADDIM
    expand = (hrow == hcol).astype(F32)
    dtx = jnp.dot(dt, expand, preferred_element_type=F32, precision=HIGHEST)
    causal = _tri(C)
    tri = causal.astype(BF16)
    lane = lax.broadcasted_iota(I32, (C, HEAD_W), 1)
    bcol_all = _chunk_cumsum(la, C)
    heads_per_group = SSM_HEADS // SSM_GROUPS
    for s in range(n_seq):
        states = [st[s, m] for m in range(n_pairs)]
        for c in range(n_chunk):
            r0 = (s * n_chunk + c) * C
            rs = slice(r0, r0 + C)
            bcol = bcol_all[rs, :]
            brow = functools.reduce(lambda p, q: p + q, [
                lax.dot_general(part, tri, (((0,), (1,)), ((), ())), preferred_element_type=F32)
                for part in _split3(la[rs, :])])
            xs_c = xbc[rs, 0:SSM_INNER]
            v_c = (xs_c * dtx[rs, :]).astype(BF16)
            gmats, bms, cms = [], [], []
            for grp in range(SSM_GROUPS):
                bm = xbc[rs, SSM_INNER + grp * SSM_STATE:SSM_INNER + (grp + 1) * SSM_STATE]
                cm = xbc[rs, SSM_INNER + (SSM_GROUPS + grp) * SSM_STATE:SSM_INNER + (SSM_GROUPS + grp + 1) * SSM_STATE]
                cm_hi, cm_lo = _split2(cm)
                bm_hi, bm_lo = _split2(bm)
                gmats.append(_dot_nt(jnp.concatenate([cm_hi, cm_hi, cm_lo], axis=1),
                                     jnp.concatenate([bm_hi, bm_lo, bm_hi], axis=1)))
                bms.append(bm)
                cms.append(cm)
            for m in range(n_pairs):
                grp = (2 * m) // heads_per_group
                bm, cm, gmat = bms[grp], cms[grp], gmats[grp]
                ps = slice(m * HEAD_W, (m + 1) * HEAD_W)
                vp = v_c[:, ps]
                s_t = states[m]
                o_halves, new_rows = [], []
                for hh in range(2):
                    hd = 2 * m + hh
                    bc = bcol[:, hd:hd + 1]
                    br = brow[hd:hd + 1, :]
                    dec = jnp.where(causal, jnp.exp(jnp.minimum(bc - br, 0.0)), 0.0)
                    b_last = bcol[C - 1:C, hd:hd + 1]
                    o_halves.append(_dot(gmat * dec, vp) + _dot_nt(cm * jnp.exp(bc), s_t))
                    kv = _dot_tn(vp, bm * jnp.exp(b_last - bc))
                    vs = slice(hh * SSM_HEADDIM, (hh + 1) * SSM_HEADDIM)
                    new_rows.append(s_t[vs, :] * jnp.exp(b_last) + kv[vs, :])
                states[m] = jnp.concatenate(new_rows, axis=0)
                o_pair = jnp.where(lane < SSM_HEADDIM, o_halves[0], o_halves[1])
                y_scr[rs, ps] = o_pair + dvec_ref[:, ps] * xs_c[:, ps]
        for m in range(n_pairs):
            st[s, m] = states[m]
    y = y_scr[...] * _silu(hz_ref[...])
    gw = SSM_INNER // SSM_GROUPS
    for grp in range(SSM_GROUPS):
        gs = slice(grp * gw, (grp + 1) * gw)
        yg = y[:, gs]
        ms = jnp.mean(yg * yg, axis=-1, keepdims=True)
        o_ref[:, gs] = (yg * lax.rsqrt(ms + RMS_EPS) * nw_ref[:, gs]).astype(BF16)

    @pl.when(tstep == pl.num_programs(1) - 1)
    def _():
        for s in range(n_seq):
            co_ref[s] = buf[s, pad:8, :]
            for m in range(n_pairs):
                so_ref[s, m] = st[s, m]


def _ssd_call(h, col_z, col_x, col_dt, cache, s0, cw, cb, dtb, alog, dvec, nw, n_batch, seq_len, row_off, sample):
    n_seq, n_chunk, C, rows, grid, rb = _seq_layout(n_batch, seq_len, row_off, sample)
    L = rows // n_seq
    hist = SSM_CONV - 1
    n_pairs = SSM_HEADS // 2
    colspec = lambda c0, w: pl.BlockSpec((rows, w), lambda i, t: (rb(i, t), c0 // w))
    fix2 = lambda i, t: (0, 0)
    c_spec = pl.BlockSpec((n_seq, hist, SSM_CONV_DIM), lambda i, t: (i, 0, 0))
    st_spec = pl.BlockSpec((n_seq, n_pairs, HEAD_W, SSM_STATE), lambda i, t: (i, 0, 0, 0))
    return pl.pallas_call(
        functools.partial(_ssd_body, n_seq, n_chunk, C, sample, True),
        grid=grid,
        in_specs=[colspec(col_z, 512), colspec(col_x, SSM_CONV_DIM), colspec(col_dt, LANE), c_spec, st_spec,
                  pl.BlockSpec((SSM_CONV, SSM_CONV_DIM), fix2), pl.BlockSpec((1, SSM_CONV_DIM), fix2),
                  pl.BlockSpec((1, LANE), fix2), pl.BlockSpec((1, LANE), fix2),
                  pl.BlockSpec((1, SSM_INNER), fix2), pl.BlockSpec((1, SSM_INNER), fix2)],
        out_specs=[pl.BlockSpec((rows, 512), lambda i, t: (rb(i, t) - row_off // rows, 0)), c_spec, st_spec],
        out_shape=[jax.ShapeDtypeStruct((n_batch * seq_len, 512), BF16),
                   jax.ShapeDtypeStruct((n_batch, hist, SSM_CONV_DIM), F32),
                   jax.ShapeDtypeStruct((n_batch, n_pairs, HEAD_W, SSM_STATE), F32)],
        scratch_shapes=[pltpu.VMEM((n_seq, n_pairs, HEAD_W, SSM_STATE), F32),
                        pltpu.VMEM((n_seq, 8 + L, SSM_CONV_DIM), F32),
                        pltpu.VMEM((n_seq, 8 + L, SSM_CONV_DIM), F32),
                        pltpu.VMEM((rows, SSM_CONV_DIM), F32),
                        pltpu.VMEM((rows, SSM_INNER), F32)],
        compiler_params=_cparams(2),
        name="ssd" + ("_sample" if sample else "_prompt"),
    )(h, h, h, cache, s0, cw, cb, dtb, alog, dvec, nw)


def _pad_heads(w, n_heads, width):
    lead = w.shape[:-1]
    w = w.reshape(lead + (n_heads, width))
    w = jnp.pad(w, [(0, 0)] * len(lead) + [(0, 0), (0, HEAD_W - width)])
    return w.reshape(lead + (n_heads * HEAD_W,))


def _row(v):
    return v.reshape(1, -1).astype(F32)


def kernel(x_prompt, x_sample, state_gla, cache_conformer, state_hgrn, state_ssm, cache_mamba_conv, w_in_even, w_gla_gate_lr, b_gla_gate, gla_norm_w, conf_conv_w, conf_conv_b, conf_ln_g, conf_ln_b, w_out_even, w_in_odd, hgrn_lower_bounds, hgrn_norm_w, mamba_conv_w, mamba_conv_b, mamba_dt_bias, mamba_a_log, mamba_d, mamba_norm_w, w_out_odd, ln1_g, ln1_b, ln2_g, ln2_b, router_w, router_b, expert_w_gate, expert_b_gate, expert_w_up, expert_b_up, expert_w_down, expert_b_down):
    bp, lp, _ = x_prompt.shape
    bs, ls, _ = x_sample.shape
    tp, ts = bp * lp, bs * ls
    x = jnp.concatenate([x_prompt.reshape(tp, D_MODEL), x_sample.reshape(ts, D_MODEL)], axis=0)

    def router_params(layer):
        rw = jnp.pad(router_w[layer], ((0, 0), (0, LANE - N_EXPERTS)))
        rb = jnp.pad(router_b[layer].astype(F32), (0, LANE - N_EXPERTS), constant_values=-1e30)
        return rw, rb.reshape(1, LANE)

    def finish_layer(layer, x, mix_a, mix_b, w_out):
        rw, rb = router_params(layer)
        x1, xp, gates, eidx, rank, counts = _outproj_ln_router(
            mix_a, mix_b, x, w_out[:512].astype(BF16), w_out[512:].astype(BF16),
            _row(ln1_g[layer]), _row(ln1_b[layer]), rw, rb)
        return _moe(layer, x1, xp, gates, eidx, rank, counts, _row(ln2_g[layer]), _row(ln2_b[layer]),
                    expert_w_gate, expert_b_gate, expert_w_up, expert_b_up, expert_w_down, expert_b_down)

    wi = w_in_even[0]
    wq, wk, wv, wg, wlr, wglu = jnp.split(wi, [256, 512, 1024, 1536, 1552], axis=1)
    w_even = jnp.concatenate([_pad_heads(wq, GLA_HEADS, GLA_DK), _pad_heads(wk, GLA_HEADS, GLA_DK), wv, wg, wglu,
                              jnp.pad(wlr, ((0, 0), (0, LANE - GLA_RANK)))], axis=1).astype(BF16)
    cols_gla = (0, 512, 1024, 1536, 3072)
    col_a, col_gate = 2048, 2560
    h = _inproj(x, w_even)
    wlr_p = jnp.pad(_pad_heads(w_gla_gate_lr[0], GLA_HEADS, GLA_DK), ((0, LANE - GLA_RANK), (0, 0)))
    blr_p = _row(_pad_heads(b_gla_gate[0], GLA_HEADS, GLA_DK))
    nw = _row(gla_norm_w[0])
    conf_args = (conf_conv_w[0], _row(conf_conv_b[0]), _row(conf_ln_g[0]), _row(conf_ln_b[0]))
    s0_p = jnp.zeros((bp, GLA_HEADS, HEAD_W, HEAD_W), F32)
    s0_s = jnp.pad(state_gla[0], ((0, 0), (0, 0), (0, HEAD_W - GLA_DK), (0, 0)))
    oa_p, sg_p = _gla_call("gla", h, cols_gla, (wlr_p, blr_p), nw, s0_p, bp, lp, 0, False)
    oa_s, sg_s = _gla_call("gla", h, cols_gla, (wlr_p, blr_p), nw, s0_s, bs, ls, tp, True)
    ob_p, cc_p = _conf_call(h, col_a, col_gate, jnp.zeros((bp,) + cache_conformer.shape[2:], F32), *conf_args,
                            bp, lp, 0, False)
    ob_s, cc_s = _conf_call(h, col_a, col_gate, cache_conformer[0], *conf_args, bs, ls, tp, True)
    mix_a = jnp.concatenate([oa_p, oa_s], axis=0)
    mix_b = jnp.concatenate([ob_p, ob_s.astype(BF16)], axis=0)
    x = finish_layer(0, x, mix_a, mix_b, w_out_even[0])
    gla_p, gla_s = sg_p[:, :, :GLA_DK, :][None], sg_s[:, :, :GLA_DK, :][None]
    conf_p, conf_s = cc_p[None], cc_s[None]

    lb_cum = jnp.cumsum(jax.nn.softmax(hgrn_lower_bounds.astype(F32), axis=0), axis=0)
    lower_bound = _row((lb_cum - lb_cum[0])[1])
    wo = w_in_odd[0]
    w_odd = jnp.concatenate([wo[:, 2560:3584], wo[:, :2560],
                             jnp.pad(wo[:, 3584:], ((0, 0), (0, LANE - SSM_HEADS)))], axis=1).astype(BF16)
    h = _inproj(x, w_odd)
    cols_hgrn = (1024, 1536, 2048, 2560)
    col_z, col_x, col_dt = 3072, 0, 3584
    nw = _row(hgrn_norm_w[0])
    oc_p, sh_p = _gla_call("hgrn", h, cols_hgrn, lower_bound, nw,
                           jnp.zeros((bp, HGRN_HEADS, HEAD_W, HEAD_W), F32), bp, lp, 0, False)
    oc_s, sh_s = _gla_call("hgrn", h, cols_hgrn, lower_bound, nw, state_hgrn[0], bs, ls, tp, True)

    def pair_states(s):
        return jnp.swapaxes(s, 2, 3).reshape(s.shape[0], SSM_HEADS // 2, HEAD_W, SSM_STATE)

    def unpair_states(s):
        return jnp.swapaxes(s.reshape(s.shape[0], SSM_HEADS, SSM_HEADDIM, SSM_STATE), 2, 3)

    pad8 = lambda v: jnp.pad(v.astype(F32), (0, LANE - SSM_HEADS)).reshape(1, LANE)
    ssd_args = (mamba_conv_w[0], _row(mamba_conv_b[0]), pad8(mamba_dt_bias[0]), pad8(mamba_a_log[0]),
                _row(jnp.repeat(mamba_d[0], SSM_HEADDIM)), _row(mamba_norm_w[0]))
    od_p, cm_p, ss_p = _ssd_call(h, col_z, col_x, col_dt, jnp.zeros((bp,) + cache_mamba_conv.shape[2:], F32),
                                 jnp.zeros((bp, SSM_HEADS // 2, HEAD_W, SSM_STATE), F32), *ssd_args,
                                 bp, lp, 0, False)
    od_s, cm_s, ss_s = _ssd_call(h, col_z, col_x, col_dt, cache_mamba_conv[0], pair_states(state_ssm[0]),
                                 *ssd_args, bs, ls, tp, True)
    mix_a = jnp.concatenate([oc_p, oc_s], axis=0)
    mix_b = jnp.concatenate([od_p, od_s], axis=0)
    x = finish_layer(1, x, mix_a, mix_b, w_out_odd[0])

    y_prompt = x[:tp].reshape(bp, lp, D_MODEL)
    y_sample = x[tp:].reshape(bs, ls, D_MODEL)
    return (y_prompt, y_sample, gla_p, gla_s, conf_p, conf_s, sh_p[None], sh_s[None],
            unpair_states(ss_p)[None], unpair_states(ss_s)[None], cm_p[None], cm_s[None])
```

```python
import functools

import jax
import jax.numpy as jnp
from jax import lax
from jax.experimental import pallas as pl
from jax.experimental.pallas import tpu as pltpu
from jax.experimental.pallas import tpu_sc as plsc

F32 = jnp.float32
BF16 = jnp.bfloat16
I32 = jnp.int32
U32 = jnp.uint32
HIGHEST = lax.Precision.HIGHEST

D_MODEL = 1024
DEPTH = 2
DEEPNORM_ALPHA = (2.0 * DEPTH) ** 0.25
LN_EPS = 1e-5
RMS_EPS = 1e-6
LANE = 128
HEAD_W = 128
GLA_HEADS, GLA_DK, GLA_RANK, GLA_TAU = 4, 64, 16, 16.0
CONF_DIM, CONF_WIDTH = 512, 31
HGRN_HEADS, HGRN_DK = 4, 128
SSM_HEADS, SSM_HEADDIM, SSM_STATE, SSM_GROUPS, SSM_CONV = 8, 64, 128, 2, 4
SSM_INNER = SSM_HEADS * SSM_HEADDIM
SSM_CONV_DIM = SSM_INNER + 2 * SSM_GROUPS * SSM_STATE
N_EXPERTS, TOP_K = 32, 4
SWIGLU_ALPHA, SWIGLU_LIMIT = 1.702, 7.0
SCAN_CHUNK = 64
PROMPT_TILE = 256
SAMPLE_SEQS = 16
TOKEN_TILE = 512
MOE_ROWS = 512
SC_CORES, SC_SUBCORES = 2, 16
SC_ROWS = 64
VMEM_LIMIT = 56 * 1024 * 1024


def _cparams(n_axes):
    return pltpu.CompilerParams(dimension_semantics=("arbitrary",) * n_axes, vmem_limit_bytes=VMEM_LIMIT)


def _silu(x):
    return x * jax.nn.sigmoid(x)


def _softplus(x):
    return jnp.maximum(x, 0.0) + jnp.log(1.0 + jnp.exp(-jnp.abs(x)))


def _log_sigmoid(x):
    return jnp.minimum(x, 0.0) - jnp.log(1.0 + jnp.exp(-jnp.abs(x)))


def _layernorm(y, g, b):
    mu = jnp.mean(y, axis=-1, keepdims=True)
    d = y - mu
    var = jnp.mean(d * d, axis=-1, keepdims=True)
    return d * lax.rsqrt(var + LN_EPS) * g + b


def _dot(a, b):
    return jnp.dot(a.astype(BF16), b.astype(BF16), preferred_element_type=F32)


def _dot_nt(a, b):
    return lax.dot_general(a.astype(BF16), b.astype(BF16), (((1,), (1,)), ((), ())), preferred_element_type=F32)


def _dot_tn(a, b):
    return lax.dot_general(a.astype(BF16), b.astype(BF16), (((0,), (0,)), ((), ())), preferred_element_type=F32)


def _tri(c):
    r = lax.broadcasted_iota(I32, (c, c), 0)
    k = lax.broadcasted_iota(I32, (c, c), 1)
    return r >= k


def _inproj_body(x_ref, w_ref, o_ref):
    xb = x_ref[...].astype(BF16)
    n = w_ref.shape[1]
    for c0 in range(0, n, 512):
        c1 = min(c0 + 512, n)
        o_ref[:, c0:c1] = jnp.dot(xb, w_ref[:, c0:c1], preferred_element_type=F32)


def _inproj(x, w):
    t, k = x.shape
    n = w.shape[1]
    return pl.pallas_call(
        _inproj_body,
        grid=(t // TOKEN_TILE,),
        in_specs=[pl.BlockSpec((TOKEN_TILE, k), lambda i: (i, 0)),
                  pl.BlockSpec((k, n), lambda i: (0, 0))],
        out_specs=pl.BlockSpec((TOKEN_TILE, n), lambda i: (i, 0)),
        out_shape=jax.ShapeDtypeStruct((t, n), F32),
        compiler_params=_cparams(1),
        name="inproj",
    )(x, w)


def _pack_halves(y):
    half = y.shape[1] // 2
    hi = lax.bitcast_convert_type(y[:, :half].astype(BF16).astype(F32), U32)
    lo = lax.bitcast_convert_type(y[:, half:].astype(BF16).astype(F32), U32)
    return (hi & jnp.uint32(0xFFFF0000)) | (lo >> 16)


def _unpack_halves(w):
    hi = lax.bitcast_convert_type(w & jnp.uint32(0xFFFF0000), F32)
    lo = lax.bitcast_convert_type(w << 16, F32)
    return hi, lo


def _outproj_body(a_ref, b_ref, x_ref, wa_ref, wb_ref, g_ref, be_ref, rw_ref, rb_ref,
                  x1_ref, xp_ref, gate_ref, idx_ref, rank_ref, cnt_ref, carry):
    @pl.when(pl.program_id(0) == 0)
    def _():
        carry[...] = jnp.zeros(carry.shape, F32)

    mix = (jnp.dot(a_ref[...], wa_ref[...], preferred_element_type=F32)
           + jnp.dot(b_ref[...], wb_ref[...], preferred_element_type=F32))
    x1 = _layernorm(DEEPNORM_ALPHA * x_ref[...] + mix, g_ref[...], be_ref[...])
    x1_ref[...] = x1
    xp_ref[...] = _pack_halves(x1)
    logits = _dot(x1, rw_ref[...]) + rb_ref[...]
    lane = lax.broadcasted_iota(I32, logits.shape, 1)
    vals, idxs = [], []
    for _ in range(TOP_K):
        m = jnp.max(logits, axis=-1, keepdims=True)
        sel = jnp.min(jnp.where(logits == m, lane, LANE), axis=-1, keepdims=True)
        vals.append(m)
        idxs.append(sel)
        logits = jnp.where(lane == sel, -jnp.inf, logits)
    exps = [jnp.exp(v - vals[0]) for v in vals]
    inv = 1.0 / functools.reduce(lambda p, q: p + q, exps)
    gates = jnp.zeros(logits.shape, F32)
    eidx = jnp.zeros(logits.shape, I32)
    for k in range(TOP_K):
        gates = jnp.where(lane == k, exps[k] * inv, gates)
        eidx = jnp.where(lane == k, idxs[k], eidx)
    gate_ref[...] = gates
    idx_ref[...] = eidx
    tm = logits.shape[0]
    chosen = jnp.zeros(logits.shape, F32)
    for k in range(TOP_K):
        chosen = chosen + (lane == idxs[k]).astype(F32)
    earlier = lax.broadcasted_iota(I32, (tm, tm), 0) > lax.broadcasted_iota(I32, (tm, tm), 1)
    before = carry[...] + jnp.dot(earlier.astype(BF16), chosen.astype(BF16), preferred_element_type=F32)
    ranks = jnp.zeros(logits.shape, F32)
    for k in range(TOP_K):
        rk = jnp.sum(jnp.where(lane == idxs[k], before, 0.0), axis=-1, keepdims=True)
        ranks = jnp.where(lane == k, rk, ranks)
    rank_ref[...] = ranks.astype(I32)
    carry[...] = carry[...] + jnp.sum(chosen, axis=0, keepdims=True)
    cnt_ref[...] = carry[...].astype(I32)


def _outproj_ln_router(a, b, x, wa, wb, g, be, rw, rb):
    t = x.shape[0]
    tm = TOKEN_TILE
    row = lambda i: (i, 0)
    fix = lambda i: (0, 0)
    return pl.pallas_call(
        _outproj_body,
        grid=(t // tm,),
        in_specs=[pl.BlockSpec((tm, 512), row), pl.BlockSpec((tm, 512), row), pl.BlockSpec((tm, D_MODEL), row),
                  pl.BlockSpec((512, D_MODEL), fix), pl.BlockSpec((512, D_MODEL), fix),
                  pl.BlockSpec((1, D_MODEL), fix), pl.BlockSpec((1, D_MODEL), fix),
                  pl.BlockSpec((D_MODEL, LANE), fix), pl.BlockSpec((1, LANE), fix)],
        out_specs=[pl.BlockSpec((tm, D_MODEL), row), pl.BlockSpec((tm, 512), row),
                   pl.BlockSpec((tm, LANE), row), pl.BlockSpec((tm, LANE), row), pl.BlockSpec((tm, LANE), row),
                   pl.BlockSpec((1, LANE), fix)],
        out_shape=[jax.ShapeDtypeStruct((t, D_MODEL), F32), jax.ShapeDtypeStruct((t, 512), U32),
                   jax.ShapeDtypeStruct((t, LANE), F32), jax.ShapeDtypeStruct((t, LANE), I32),
                   jax.ShapeDtypeStruct((t, LANE), I32), jax.ShapeDtypeStruct((1, LANE), I32)],
        scratch_shapes=[pltpu.VMEM((1, LANE), F32)],
        compiler_params=_cparams(1),
        name="outproj_ln_router",
    )(a, b, x, wa, wb, g, be, rw, rb)


def _sc_worker_rows(n):
    per_worker = n // (SC_CORES * SC_SUBCORES)
    worker = lax.axis_index("s") * SC_CORES + lax.axis_index("c")
    return worker * per_worker, per_worker // SC_ROWS


def _sc_mesh():
    return plsc.VectorSubcoreMesh(core_axis_name="c", subcore_axis_name="s")


def _sc_scratch(w, dtype):
    return ([pltpu.VMEM((SC_ROWS,), I32)] * 2 + [pltpu.VMEM((SC_ROWS, w), dtype)] * 2
            + [pltpu.SemaphoreType.DMA] * 4)


def _sc_chunk_pair(base, p):
    off_a = pl.multiple_of(base + 2 * p * SC_ROWS, 8)
    return off_a, pl.multiple_of(off_a + SC_ROWS, 8)


def _sc_scatter_rows(src, idx, n_out):
    n, (n_src, w) = idx.shape[0], src.shape
    assert n % (SC_CORES * SC_SUBCORES * SC_ROWS * 2) == 0 and n_src % SC_ROWS == 0

    @functools.partial(pl.kernel, mesh=_sc_mesh(), out_type=jax.ShapeDtypeStruct((n_out, w), src.dtype),
                       scratch_types=_sc_scratch(w, src.dtype))
    def scatter(src_hbm, idx_hbm, out_hbm, idx_a, idx_b, rows_a, rows_b, sem_ra, sem_rb, sem_wa, sem_wb):
        base, chunks = _sc_worker_rows(n)

        def src_rows(off):
            return src_hbm.at[pl.ds(pl.multiple_of(lax.rem(off, n_src), 8), SC_ROWS)]

        @pl.loop(0, chunks // 2)
        def _(p):
            off_a, off_b = _sc_chunk_pair(base, p)
            read_a = pltpu.async_copy(src_rows(off_a), rows_a, sem_ra)
            read_b = pltpu.async_copy(src_rows(off_b), rows_b, sem_rb)
            pltpu.sync_copy(idx_hbm.at[pl.ds(off_a, SC_ROWS)], idx_a)
            pltpu.sync_copy(idx_hbm.at[pl.ds(off_b, SC_ROWS)], idx_b)
            read_a.wait()
            write_a = pltpu.async_copy(rows_a, out_hbm.at[idx_a], sem_wa)
            read_b.wait()
            write_b = pltpu.async_copy(rows_b, out_hbm.at[idx_b], sem_wb)
            write_a.wait()
            write_b.wait()

    return scatter(src, idx)


def _sc_gather_rows(table, idx):
    n, w = idx.shape[0], table.shape[1]
    assert n % (SC_CORES * SC_SUBCORES * SC_ROWS * 2) == 0

    @functools.partial(pl.kernel, mesh=_sc_mesh(), out_type=jax.ShapeDtypeStruct((n, w), table.dtype),
                       scratch_types=_sc_scratch(w, table.dtype))
    def gather(table_hbm, idx_hbm, out_hbm, idx_a, idx_b, rows_a, rows_b, sem_ra, sem_rb, sem_wa, sem_wb):
        base, chunks = _sc_worker_rows(n)

        @pl.loop(0, chunks // 2)
        def _(p):
            off_a, off_b = _sc_chunk_pair(base, p)
            pltpu.sync_copy(idx_hbm.at[pl.ds(off_a, SC_ROWS)], idx_a)
            read_a = pltpu.async_copy(table_hbm.at[idx_a], rows_a, sem_ra)
            pltpu.sync_copy(idx_hbm.at[pl.ds(off_b, SC_ROWS)], idx_b)
            read_b = pltpu.async_copy(table_hbm.at[idx_b], rows_b, sem_rb)
            read_a.wait()
            write_a = pltpu.async_copy(rows_a, out_hbm.at[pl.ds(off_a, SC_ROWS)], sem_wa)
            read_b.wait()
            write_b = pltpu.async_copy(rows_b, out_hbm.at[pl.ds(off_b, SC_ROWS)], sem_wb)
            write_a.wait()
            write_b.wait()

    return gather(table, idx)


def _experts_body(be_ref, nv_ref, xs_ref, wg_ref, bg_ref, wu_ref, bu_ref, wd_ref, bd_ref, o_ref, wg_s, wu_s, wd_s):
    i = pl.program_id(0)
    prev = be_ref[jnp.maximum(i - 1, 0)]
    valid = nv_ref[i]
    half_rows = MOE_ROWS // 2

    @pl.when((valid > 0) & ((i == 0) | (be_ref[i] != prev)))
    def _():
        wg_s[...] = wg_ref[...].astype(BF16)
        wu_s[...] = wu_ref[...].astype(BF16)
        wd_s[...] = wd_ref[...].astype(BF16)

    def compute(rows):
        half = D_MODEL // 2
        x_hi, x_lo = _unpack_halves(xs_ref[0:rows, :])
        x_hi = x_hi.astype(BF16)
        x_lo = x_lo.astype(BF16)
        g = (jnp.dot(x_hi, wg_s[:half, :], preferred_element_type=F32)
             + jnp.dot(x_lo, wg_s[half:, :], preferred_element_type=F32) + bg_ref[...])
        u = (jnp.dot(x_hi, wu_s[:half, :], preferred_element_type=F32)
             + jnp.dot(x_lo, wu_s[half:, :], preferred_element_type=F32) + bu_ref[...])
        g = jnp.minimum(g, SWIGLU_LIMIT)
        u = jnp.clip(u, -SWIGLU_LIMIT, SWIGLU_LIMIT)
        hmid = (u + 1.0) * (g * jax.nn.sigmoid(SWIGLU_ALPHA * g))
        out = jnp.dot(hmid.astype(BF16), wd_s[...], preferred_element_type=F32) + bd_ref[...]
        o_ref[0:rows, :] = _pack_halves(out)

    @pl.when(valid > half_rows)
    def _():
        compute(MOE_ROWS)

    @pl.when((valid > 0) & (valid <= half_rows))
    def _():
        compute(half_rows)
        o_ref[half_rows:, :] = jnp.zeros((MOE_ROWS - half_rows, o_ref.shape[1]), o_ref.dtype)

    @pl.when(valid == 0)
    def _():
        o_ref[...] = jnp.zeros(o_ref.shape, o_ref.dtype)


def _experts(layer, block_e, block_valid, xs, wg, bg, wu, bu, wd, bd):
    n_rows = xs.shape[0]
    bm = MOE_ROWS
    row = lambda i, be, nu: (i, 0)
    wsel = lambda i, be, nu: (layer, be[i], 0, 0)
    wspec = pl.BlockSpec((None, None, D_MODEL, D_MODEL), wsel)
    bspec = pl.BlockSpec((None, None, 1, D_MODEL), wsel)
    bias = lambda b: b.reshape(b.shape[0], b.shape[1], 1, b.shape[2])
    return pl.pallas_call(
        _experts_body,
        grid_spec=pltpu.PrefetchScalarGridSpec(
            num_scalar_prefetch=2,
            grid=(n_rows // bm,),
            in_specs=[pl.BlockSpec((bm, 512), row), wspec, bspec, wspec, bspec, wspec, bspec],
            out_specs=pl.BlockSpec((bm, 512), row),
            scratch_shapes=[pltpu.VMEM((D_MODEL, D_MODEL), BF16)] * 3,
        ),
        out_shape=jax.ShapeDtypeStruct((n_rows, 512), U32),
        compiler_params=_cparams(1),
        name="experts",
    )(block_e, block_valid, xs, wg, bias(bg), wu, bias(bu), wd, bias(bd))


def _combine_body(n_first, o0_ref, o1_ref, o2_ref, o3_ref, gt_ref, x_ref, g_ref, b_ref, ya_ref, yb_ref=None):
    half = D_MODEL // 2
    gates = gt_ref[...]
    hi = jnp.zeros((x_ref.shape[0], half), F32)
    lo = jnp.zeros((x_ref.shape[0], half), F32)
    for k, o_ref in enumerate((o0_ref, o1_ref, o2_ref, o3_ref)):
        h, l = _unpack_halves(o_ref[...])
        gk = gates[:, k:k + 1]
        hi = hi + gk * h
        lo = lo + gk * l
    x = x_ref[...]
    y_hi = DEEPNORM_ALPHA * x[:, :half] + hi
    y_lo = DEEPNORM_ALPHA * x[:, half:] + lo
    mu = (jnp.sum(y_hi, axis=-1, keepdims=True) + jnp.sum(y_lo, axis=-1, keepdims=True)) * (1.0 / D_MODEL)
    d_hi = y_hi - mu
    d_lo = y_lo - mu
    var = (jnp.sum(d_hi * d_hi, axis=-1, keepdims=True) + jnp.sum(d_lo * d_lo, axis=-1, keepdims=True)) * (1.0 / D_MODEL)
    r = lax.rsqrt(var + LN_EPS)
    out_hi = d_hi * r * g_ref[:, :half] + b_ref[:, :half]
    out_lo = d_lo * r * g_ref[:, half:] + b_ref[:, half:]

    def write(y_ref):
        y_ref[:, :half] = out_hi
        y_ref[:, half:] = out_lo

    if yb_ref is None:
        write(ya_ref)
    else:
        pl.when(pl.program_id(0) < n_first)(lambda: write(ya_ref))
        pl.when(pl.program_id(0) >= n_first)(lambda: write(yb_ref))


def _combine_ln(o4, gates, x, g, b, t_first=None):
    t = x.shape[0]
    tm = TOKEN_TILE
    row = lambda i: (i, 0)
    fix = lambda i: (0, 0)
    choice = lambda k: pl.BlockSpec((tm, 512), lambda i: (k * (t // tm) + i, 0))
    if t_first is None:
        n_first = None
        out_specs = pl.BlockSpec((tm, D_MODEL), row)
        out_shape = jax.ShapeDtypeStruct((t, D_MODEL), F32)
    else:
        n_first = t_first // tm
        out_specs = [pl.BlockSpec((tm, D_MODEL), lambda i: (jnp.minimum(i, n_first - 1), 0)),
                     pl.BlockSpec((tm, D_MODEL), lambda i: (jnp.maximum(i - n_first, 0), 0))]
        out_shape = [jax.ShapeDtypeStruct((t_first, D_MODEL), F32), jax.ShapeDtypeStruct((t - t_first, D_MODEL), F32)]
    return pl.pallas_call(
        functools.partial(_combine_body, n_first),
        grid=(t // tm,),
        in_specs=[choice(0), choice(1), choice(2), choice(3), pl.BlockSpec((tm, LANE), row),
                  pl.BlockSpec((tm, D_MODEL), row), pl.BlockSpec((1, D_MODEL), fix), pl.BlockSpec((1, D_MODEL), fix)],
        out_specs=out_specs,
        out_shape=out_shape,
        compiler_params=_cparams(1),
        name="combine_ln",
    )(o4, o4, o4, o4, gates, x, g, b)


def _moe(layer, x1, xp, gates, eidx, rank, counts, ln_g, ln_b, wg, bg, wu, bu, wd, bd, t_first=None):
    t = x1.shape[0]
    bm = MOE_ROWS
    n_blocks = t * TOP_K // bm + N_EXPERTS
    n_rows = n_blocks * bm
    cnt = counts[0, :N_EXPERTS]
    padded = (cnt + bm - 1) // bm * bm
    pad_end = jnp.cumsum(padded)
    pad_start = pad_end - padded
    e = eidx[:, :TOP_K]
    start = jnp.sum(jnp.where(e[:, :, None] == jnp.arange(N_EXPERTS, dtype=I32), pad_start, 0), axis=-1)
    dest = (start + rank[:, :TOP_K]).T.reshape(-1)
    n_used = (pad_end[-1] // bm).astype(I32)
    blk = jnp.arange(n_blocks, dtype=I32)
    first_row = jnp.minimum(blk, n_used - 1) * bm
    block_e = jnp.minimum(jnp.sum((pad_end[None, :] <= first_row[:, None]).astype(I32), axis=1), N_EXPERTS - 1)
    is_e = block_e[:, None] == jnp.arange(N_EXPERTS, dtype=I32)
    real_end = jnp.sum(jnp.where(is_e, pad_start + cnt, 0), axis=1)
    block_valid = jnp.where(blk < n_used, jnp.clip(real_end - blk * bm, 0, bm), 0).astype(I32)
    xs = _sc_scatter_rows(xp, dest, n_rows)
    outs = _experts(layer, block_e, block_valid, xs, wg, bg, wu, bu, wd, bd)
    o4 = _sc_gather_rows(outs, dest)
    return _combine_ln(o4, gates, x1, ln_g, ln_b, t_first)


def _split2(x):
    hi = x.astype(BF16)
    return hi, (x - hi.astype(F32)).astype(BF16)


def _split3(x):
    hi = x.astype(BF16)
    rem = x - hi.astype(F32)
    mid = rem.astype(BF16)
    return hi, mid, (rem - mid.astype(F32)).astype(BF16)


def _chunk_cumsum(g, C):
    rows = g.shape[0]
    r = lax.broadcasted_iota(I32, (rows, rows), 0)
    c = lax.broadcasted_iota(I32, (rows, rows), 1)
    tri = ((r >= c) & (r // C == c // C)).astype(BF16)
    hi, mid, lo = _split3(g)
    dot = lambda part: jnp.dot(tri, part, preferred_element_type=F32)
    return dot(hi) + dot(mid) + dot(lo)


def _gla_body(mode, n_seq, n_chunk, C, *refs):
    if mode == "gla":
        hq_ref, hk_ref, hv_ref, hg_ref, hlr_ref, wlr_ref, blr_ref, nw_ref, s0_ref, _, o_ref, so_ref, st, o_scr = refs
    else:
        hq_ref, hk_ref, hv_ref, hg_ref, lb_ref, nw_ref, s0_ref, _, o_ref, so_ref, st, o_scr = refs
    n_heads = 4
    tstep = pl.program_id(1)

    @pl.when(tstep == 0)
    def _():
        for s in range(n_seq):
            for h in range(n_heads):
                st[s, h] = s0_ref[s, h].T

    if mode == "gla":
        q = hq_ref[...] * (GLA_DK ** -0.5)
        k = hk_ref[...]
        z = _dot(hlr_ref[...], wlr_ref[...]) + blr_ref[...]
        g = _log_sigmoid(z) * (1.0 / GLA_TAU)
    else:
        q = _silu(hq_ref[...]) * (HGRN_DK ** -0.5)
        lb = lb_ref[...]
        f = lb + (1.0 - lb) * jax.nn.sigmoid(hk_ref[...])
        k = 1.0 - f
        g = jnp.log(f)
    v = hv_ref[...]
    causal = _tri(C)
    mid = max(C // 2 - 1, 0)
    b_all = _chunk_cumsum(g, C)
    for s in range(n_seq):
        states = [st[s, h] for h in range(n_heads)]
        for c in range(n_chunk):
            r0 = (s * n_chunk + c) * C
            rs = slice(r0, r0 + C)
            b, qc, kc = b_all[rs, :], q[rs, :], k[rs, :]
            b_last = b[C - 1:C, :]
            b_mid = b[mid:mid + 1, :]
            qe_hi, qe_lo = _split2(qc * jnp.exp(b - b_mid))
            ke_hi, ke_lo = _split2(kc * jnp.exp(b_mid - b))
            q_state = (qc * jnp.exp(b)).astype(BF16)
            k_state = (kc * jnp.exp(b_last - b)).astype(BF16)
            decay = jnp.exp(b_last)
            for h in range(n_heads):
                cs = slice(h * HEAD_W, (h + 1) * HEAD_W)
                lhs = jnp.concatenate([qe_hi[:, cs], qe_hi[:, cs], qe_lo[:, cs]], axis=1)
                rhs = jnp.concatenate([ke_hi[:, cs], ke_lo[:, cs], ke_hi[:, cs]], axis=1)
                scores = jnp.where(causal, _dot_nt(lhs, rhs), 0.0)
                vh = v[rs, cs].astype(BF16)
                o = _dot(scores, vh) + _dot_nt(q_state[:, cs], states[h])
                states[h] = states[h] * decay[:, cs] + _dot_tn(vh, k_state[:, cs])
                ms = jnp.mean(o * o, axis=-1, keepdims=True)
                o_scr[rs, cs] = o * lax.rsqrt(ms + RMS_EPS)
        for h in range(n_heads):
            st[s, h] = states[h]
    o_ref[...] = (o_scr[...] * nw_ref[...] * _silu(hg_ref[...])).astype(BF16)

    @pl.when(tstep == pl.num_programs(1) - 1)
    def _():
        for s in range(n_seq):
            for h in range(n_heads):
                so_ref[s, h] = st[s, h].T


def _seq_layout(n_batch, seq_len, row_off, sample):
    if sample:
        n_seq, n_chunk, C = SAMPLE_SEQS, 1, seq_len
        rows = n_seq * C
        grid = (n_batch // n_seq, 1)
        blk0 = row_off // rows
        rb = lambda i, t: blk0 + i
    else:
        n_seq, n_chunk, C = 1, PROMPT_TILE // SCAN_CHUNK, SCAN_CHUNK
        rows = PROMPT_TILE
        tiles = seq_len // rows
        grid = (n_batch, tiles)
        blk0 = row_off // rows
        rb = lambda i, t: blk0 + i * tiles + t
    return n_seq, n_chunk, C, rows, grid, rb


def _gla_call(mode, h, cols, extra, nw, s0, out_buf, n_batch, seq_len, row_off, sample):
    n_seq, n_chunk, C, rows, grid, rb = _seq_layout(n_batch, seq_len, row_off, sample)
    colspec = lambda c0, w: pl.BlockSpec((rows, w), lambda i, t: (rb(i, t), c0 // w))
    fix2 = lambda i, t: (0, 0)
    in_specs = [colspec(cols[0], 512), colspec(cols[1], 512), colspec(cols[2], 512), colspec(cols[3], 512)]
    args = [h, h, h, h]
    if mode == "gla":
        wlr, blr = extra
        in_specs += [colspec(cols[4], LANE), pl.BlockSpec((LANE, 512), fix2), pl.BlockSpec((1, 512), fix2)]
        args += [h, wlr, blr]
    else:
        in_specs += [pl.BlockSpec((1, 512), fix2)]
        args += [extra]
    st_spec = pl.BlockSpec((n_seq, 4, HEAD_W, HEAD_W), lambda i, t: (i, 0, 0, 0))
    in_specs += [pl.BlockSpec((1, 512), fix2), st_spec, pl.BlockSpec(memory_space=pl.ANY)]
    args += [nw, s0, out_buf]
    o_spec = pl.BlockSpec((rows, 512), lambda i, t: (rb(i, t), 0))
    return pl.pallas_call(
        functools.partial(_gla_body, mode, n_seq, n_chunk, C),
        grid=grid,
        in_specs=in_specs,
        out_specs=[o_spec, st_spec],
        out_shape=[jax.ShapeDtypeStruct(out_buf.shape, out_buf.dtype),
                   jax.ShapeDtypeStruct((n_batch, 4, HEAD_W, HEAD_W), F32)],
        scratch_shapes=[pltpu.VMEM((n_seq, 4, HEAD_W, HEAD_W), F32), pltpu.VMEM((rows, 512), F32)],
        input_output_aliases={len(args) - 1: 0},
        compiler_params=_cparams(2),
        name=mode + ("_sample" if sample else "_prompt"),
    )(*args)


def _round_bf16(x, on=True):
    return x.astype(BF16).astype(F32) if on else x


def _conf_body(n_seq, L, round_x, round_w, a_ref, gt_ref, hist_ref, w_ref, b_ref, g_ref, be_ref, _, o_ref, co_ref,
               buf, bufr, y_scr):
    tstep = pl.program_id(1)
    hist = CONF_WIDTH - 1
    pad = 32 - hist

    @pl.when(tstep == 0)
    def _():
        for s in range(n_seq):
            buf[s, pad:32, :] = hist_ref[s]
            bufr[s, pad:32, :] = _round_bf16(hist_ref[s], round_x)

    u = a_ref[...] * jax.nn.sigmoid(gt_ref[...])
    ur = _round_bf16(u, round_x)
    for s in range(n_seq):
        buf[s, 32:32 + L, :] = u[s * L:(s + 1) * L, :]
        bufr[s, 32:32 + L, :] = ur[s * L:(s + 1) * L, :]
    w = _round_bf16(w_ref[...], round_w)
    for s in range(n_seq):
        acc = jnp.zeros((L, CONF_DIM), F32)
        for j in range(CONF_WIDTH):
            acc = acc + bufr[s, pad + j:pad + j + L, :] * w[j:j + 1, :]
        y_scr[s * L:(s + 1) * L, :] = _silu(_layernorm(acc + b_ref[...], g_ref[...], be_ref[...]))
        tail = buf[s, L + pad:L + 32, :]
        buf[s, pad:32, :] = tail
        tailr = bufr[s, L + pad:L + 32, :]
        bufr[s, pad:32, :] = tailr
    o_ref[...] = y_scr[...].astype(o_ref.dtype)

    @pl.when(tstep == pl.num_programs(1) - 1)
    def _():
        for s in range(n_seq):
            co_ref[s] = buf[s, pad:32, :]


def _conf_call(h, col_a, col_g, cache, w, b, g, be, out_buf, n_batch, seq_len, row_off, sample):
    n_seq, n_chunk, C, rows, grid, rb = _seq_layout(n_batch, seq_len, row_off, sample)
    L = rows // n_seq
    hist = CONF_WIDTH - 1
    colspec = lambda c0: pl.BlockSpec((rows, 512), lambda i, t: (rb(i, t), c0 // 512))
    fix2 = lambda i, t: (0, 0)
    c_spec = pl.BlockSpec((n_seq, hist, CONF_DIM), lambda i, t: (i, 0, 0))
    return pl.pallas_call(
        functools.partial(_conf_body, n_seq, L, True, sample),
        grid=grid,
        in_specs=[colspec(col_a), colspec(col_g), c_spec,
                  pl.BlockSpec((CONF_WIDTH, CONF_DIM), fix2), pl.BlockSpec((1, CONF_DIM), fix2),
                  pl.BlockSpec((1, CONF_DIM), fix2), pl.BlockSpec((1, CONF_DIM), fix2),
                  pl.BlockSpec(memory_space=pl.ANY)],
        out_specs=[pl.BlockSpec((rows, 512), lambda i, t: (rb(i, t), 0)), c_spec],
        out_shape=[jax.ShapeDtypeStruct(out_buf.shape, out_buf.dtype),
                   jax.ShapeDtypeStruct((n_batch, hist, CONF_DIM), F32)],
        scratch_shapes=[pltpu.VMEM((n_seq, 32 + L, CONF_DIM), F32)] * 2 + [pltpu.VMEM((rows, CONF_DIM), F32)],
        input_output_aliases={7: 0},
        compiler_params=_cparams(2),
        name="conformer" + ("_sample" if sample else "_prompt"),
    )(h, h, cache, w, b, g, be, out_buf)


def _ssd_body(n_seq, n_chunk, C, round_x, round_w, hz_ref, hx_ref, hdt_ref, hist_ref, s0_ref, cw_ref, cb_ref, dtb_ref, alog_ref,
              dvec_ref, nw_ref, _, o_ref, co_ref, so_ref, st, buf, bufr, xbc, y_scr):
    tstep = pl.program_id(1)
    L = n_chunk * C
    hist = SSM_CONV - 1
    pad = 8 - hist
    n_pairs = SSM_HEADS // 2

    @pl.when(tstep == 0)
    def _():
        for s in range(n_seq):
            buf[s, pad:8, :] = hist_ref[s]
            bufr[s, pad:8, :] = _round_bf16(hist_ref[s], round_x)
            for m in range(n_pairs):
                st[s, m] = s0_ref[s, m]

    cw = _round_bf16(cw_ref[...], round_w)
    for s in range(n_seq):
        hx = hx_ref[s * L:(s + 1) * L, :]
        buf[s, 8:8 + L, :] = hx
        bufr[s, 8:8 + L, :] = _round_bf16(hx, round_x)
        acc = jnp.zeros((L, SSM_CONV_DIM), F32)
        for j in range(SSM_CONV):
            acc = acc + bufr[s, pad + j:pad + j + L, :] * cw[j:j + 1, :]
        xbc[s * L:(s + 1) * L, :] = _silu(acc + cb_ref[...])
        tail = buf[s, L + pad:L + 8, :]
        buf[s, pad:8, :] = tail
        tailr = bufr[s, L + pad:L + 8, :]
        bufr[s, pad:8, :] = tailr

    dt = _softplus(hdt_ref[...] + dtb_ref[...])
    la = dt * (-jnp.exp(alog_ref[...]))
    hrow = lax.broadcasted_iota(I32, (LANE, SSM_INNER), 0)
    hcol = lax.broadcasted_iota(I32, (LANE, SSM_INNER), 1) // SSM_HEADDIM
    expand = (hrow == hcol).astype(F32)
    dtx = jnp.dot(dt, expand, preferred_element_type=F32, precision=HIGHEST)
    causal = _tri(C)
    tri = causal.astype(BF16)
    lane = lax.broadcasted_iota(I32, (C, HEAD_W), 1)
    bcol_all = _chunk_cumsum(la, C)
    heads_per_group = SSM_HEADS // SSM_GROUPS
    for s in range(n_seq):
        states = [st[s, m] for m in range(n_pairs)]
        for c in range(n_chunk):
            r0 = (s * n_chunk + c) * C
            rs = slice(r0, r0 + C)
            bcol = bcol_all[rs, :]
            brow = functools.reduce(lambda p, q: p + q, [
                lax.dot_general(part, tri, (((0,), (1,)), ((), ())), preferred_element_type=F32)
                for part in _split3(la[rs, :])])
            xs_c = xbc[rs, 0:SSM_INNER]
            v_c = (xs_c * dtx[rs, :]).astype(BF16)
            gmats, bms, cms = [], [], []
            for grp in range(SSM_GROUPS):
                bm = xbc[rs, SSM_INNER + grp * SSM_STATE:SSM_INNER + (grp + 1) * SSM_STATE]
                cm = xbc[rs, SSM_INNER + (SSM_GROUPS + grp) * SSM_STATE:SSM_INNER + (SSM_GROUPS + grp + 1) * SSM_STATE]
                cm_hi, cm_lo = _split2(cm)
                bm_hi, bm_lo = _split2(bm)
                gmats.append(_dot_nt(jnp.concatenate([cm_hi, cm_hi, cm_lo], axis=1),
                                     jnp.concatenate([bm_hi, bm_lo, bm_hi], axis=1)))
                bms.append(bm)
                cms.append(cm)
            for m in range(n_pairs):
                grp = (2 * m) // heads_per_group
                bm, cm, gmat = bms[grp], cms[grp], gmats[grp]
                ps = slice(m * HEAD_W, (m + 1) * HEAD_W)
                vp = v_c[:, ps]
                s_t = states[m]
                o_halves, new_rows = [], []
                for hh in range(2):
                    hd = 2 * m + hh
                    bc = bcol[:, hd:hd + 1]
                    br = brow[hd:hd + 1, :]
                    dec = jnp.where(causal, jnp.exp(jnp.minimum(bc - br, 0.0)), 0.0)
                    b_last = bcol[C - 1:C, hd:hd + 1]
                    o_halves.append(_dot(gmat * dec, vp) + _dot_nt(cm * jnp.exp(bc), s_t))
                    kv = _dot_tn(vp, bm * jnp.exp(b_last - bc))
                    vs = slice(hh * SSM_HEADDIM, (hh + 1) * SSM_HEADDIM)
                    new_rows.append(s_t[vs, :] * jnp.exp(b_last) + kv[vs, :])
                states[m] = jnp.concatenate(new_rows, axis=0)
                o_pair = jnp.where(lane < SSM_HEADDIM, o_halves[0], o_halves[1])
                y_scr[rs, ps] = o_pair + dvec_ref[:, ps] * xs_c[:, ps]
        for m in range(n_pairs):
            st[s, m] = states[m]
    y = y_scr[...] * _silu(hz_ref[...])
    gw = SSM_INNER // SSM_GROUPS
    for grp in range(SSM_GROUPS):
        gs = slice(grp * gw, (grp + 1) * gw)
        yg = y[:, gs]
        ms = jnp.mean(yg * yg, axis=-1, keepdims=True)
        o_ref[:, gs] = (yg * lax.rsqrt(ms + RMS_EPS) * nw_ref[:, gs]).astype(BF16)

    @pl.when(tstep == pl.num_programs(1) - 1)
    def _():
        for s in range(n_seq):
            co_ref[s] = buf[s, pad:8, :]
            for m in range(n_pairs):
                so_ref[s, m] = st[s, m]


def _ssd_call(h, col_z, col_x, col_dt, cache, s0, cw, cb, dtb, alog, dvec, nw, out_buf, n_batch, seq_len, row_off,
              sample):
    n_seq, n_chunk, C, rows, grid, rb = _seq_layout(n_batch, seq_len, row_off, sample)
    L = rows // n_seq
    hist = SSM_CONV - 1
    n_pairs = SSM_HEADS // 2
    colspec = lambda c0, w: pl.BlockSpec((rows, w), lambda i, t: (rb(i, t), c0 // w))
    fix2 = lambda i, t: (0, 0)
    c_spec = pl.BlockSpec((n_seq, hist, SSM_CONV_DIM), lambda i, t: (i, 0, 0))
    st_spec = pl.BlockSpec((n_seq, n_pairs, HEAD_W, SSM_STATE), lambda i, t: (i, 0, 0, 0))
    return pl.pallas_call(
        functools.partial(_ssd_body, n_seq, n_chunk, C, sample, True),
        grid=grid,
        in_specs=[colspec(col_z, 512), colspec(col_x, SSM_CONV_DIM), colspec(col_dt, LANE), c_spec, st_spec,
                  pl.BlockSpec((SSM_CONV, SSM_CONV_DIM), fix2), pl.BlockSpec((1, SSM_CONV_DIM), fix2),
                  pl.BlockSpec((1, LANE), fix2), pl.BlockSpec((1, LANE), fix2),
                  pl.BlockSpec((1, SSM_INNER), fix2), pl.BlockSpec((1, SSM_INNER), fix2),
                  pl.BlockSpec(memory_space=pl.ANY)],
        out_specs=[pl.BlockSpec((rows, 512), lambda i, t: (rb(i, t), 0)), c_spec, st_spec],
        out_shape=[jax.ShapeDtypeStruct(out_buf.shape, out_buf.dtype),
                   jax.ShapeDtypeStruct((n_batch, hist, SSM_CONV_DIM), F32),
                   jax.ShapeDtypeStruct((n_batch, n_pairs, HEAD_W, SSM_STATE), F32)],
        scratch_shapes=[pltpu.VMEM((n_seq, n_pairs, HEAD_W, SSM_STATE), F32),
                        pltpu.VMEM((n_seq, 8 + L, SSM_CONV_DIM), F32),
                        pltpu.VMEM((n_seq, 8 + L, SSM_CONV_DIM), F32),
                        pltpu.VMEM((rows, SSM_CONV_DIM), F32),
                        pltpu.VMEM((rows, SSM_INNER), F32)],
        input_output_aliases={11: 0},
        compiler_params=_cparams(2),
        name="ssd" + ("_sample" if sample else "_prompt"),
    )(h, h, h, cache, s0, cw, cb, dtb, alog, dvec, nw, out_buf)


def _pad_heads(w, n_heads, width):
    lead = w.shape[:-1]
    w = w.reshape(lead + (n_heads, width))
    w = jnp.pad(w, [(0, 0)] * len(lead) + [(0, 0), (0, HEAD_W - width)])
    return w.reshape(lead + (n_heads * HEAD_W,))


def _row(v):
    return v.reshape(1, -1).astype(F32)


def kernel(x_prompt, x_sample, state_gla, cache_conformer, state_hgrn, state_ssm, cache_mamba_conv, w_in_even, w_gla_gate_lr, b_gla_gate, gla_norm_w, conf_conv_w, conf_conv_b, conf_ln_g, conf_ln_b, w_out_even, w_in_odd, hgrn_lower_bounds, hgrn_norm_w, mamba_conv_w, mamba_conv_b, mamba_dt_bias, mamba_a_log, mamba_d, mamba_norm_w, w_out_odd, ln1_g, ln1_b, ln2_g, ln2_b, router_w, router_b, expert_w_gate, expert_b_gate, expert_w_up, expert_b_up, expert_w_down, expert_b_down):
    bp, lp, _ = x_prompt.shape
    bs, ls, _ = x_sample.shape
    tp, ts = bp * lp, bs * ls
    x = jnp.concatenate([x_prompt.reshape(tp, D_MODEL), x_sample.reshape(ts, D_MODEL)], axis=0)

    def router_params(layer):
        rw = jnp.pad(router_w[layer], ((0, 0), (0, LANE - N_EXPERTS)))
        rb = jnp.pad(router_b[layer].astype(F32), (0, LANE - N_EXPERTS), constant_values=-1e30)
        return rw, rb.reshape(1, LANE)

    def finish_layer(layer, x, mix_a, mix_b, w_out, t_first=None):
        rw, rb = router_params(layer)
        x1, xp, gates, eidx, rank, counts = _outproj_ln_router(
            mix_a, mix_b, x, w_out[:512].astype(BF16), w_out[512:].astype(BF16),
            _row(ln1_g[layer]), _row(ln1_b[layer]), rw, rb)
        return _moe(layer, x1, xp, gates, eidx, rank, counts, _row(ln2_g[layer]), _row(ln2_b[layer]),
                    expert_w_gate, expert_b_gate, expert_w_up, expert_b_up, expert_w_down, expert_b_down, t_first)

    def mix_buffer():
        return jnp.zeros((tp + ts, 512), BF16)

    wi = w_in_even[0]
    wq, wk, wv, wg, wlr, wglu = jnp.split(wi, [256, 512, 1024, 1536, 1552], axis=1)
    w_even = jnp.concatenate([_pad_heads(wq, GLA_HEADS, GLA_DK), _pad_heads(wk, GLA_HEADS, GLA_DK), wv, wg, wglu,
                              jnp.pad(wlr, ((0, 0), (0, LANE - GLA_RANK)))], axis=1).astype(BF16)
    cols_gla = (0, 512, 1024, 1536, 3072)
    col_a, col_gate = 2048, 2560
    h = _inproj(x, w_even)
    wlr_p = jnp.pad(_pad_heads(w_gla_gate_lr[0], GLA_HEADS, GLA_DK), ((0, LANE - GLA_RANK), (0, 0)))
    blr_p = _row(_pad_heads(b_gla_gate[0], GLA_HEADS, GLA_DK))
    nw = _row(gla_norm_w[0])
    conf_args = (conf_conv_w[0], _row(conf_conv_b[0]), _row(conf_ln_g[0]), _row(conf_ln_b[0]))
    s0_p = jnp.zeros((bp, GLA_HEADS, HEAD_W, HEAD_W), F32)
    s0_s = jnp.pad(state_gla[0], ((0, 0), (0, 0), (0, HEAD_W - GLA_DK), (0, 0)))
    mix_a, sg_p = _gla_call("gla", h, cols_gla, (wlr_p, blr_p), nw, s0_p, mix_buffer(), bp, lp, 0, False)
    mix_a, sg_s = _gla_call("gla", h, cols_gla, (wlr_p, blr_p), nw, s0_s, mix_a, bs, ls, tp, True)
    mix_b, cc_p = _conf_call(h, col_a, col_gate, jnp.zeros((bp,) + cache_conformer.shape[2:], F32), *conf_args,
                             mix_buffer(), bp, lp, 0, False)
    mix_b, cc_s = _conf_call(h, col_a, col_gate, cache_conformer[0], *conf_args, mix_b, bs, ls, tp, True)
    x = finish_layer(0, x, mix_a, mix_b, w_out_even[0])
    gla_p, gla_s = sg_p[:, :, :GLA_DK, :][None], sg_s[:, :, :GLA_DK, :][None]
    conf_p, conf_s = cc_p[None], cc_s[None]

    lb_cum = jnp.cumsum(jax.nn.softmax(hgrn_lower_bounds.astype(F32), axis=0), axis=0)
    lower_bound = _row((lb_cum - lb_cum[0])[1])
    wo = w_in_odd[0]
    w_odd = jnp.concatenate([wo[:, 2560:3584], wo[:, :2560],
                             jnp.pad(wo[:, 3584:], ((0, 0), (0, LANE - SSM_HEADS)))], axis=1).astype(BF16)
    h = _inproj(x, w_odd)
    cols_hgrn = (1024, 1536, 2048, 2560)
    col_z, col_x, col_dt = 3072, 0, 3584
    nw = _row(hgrn_norm_w[0])
    mix_a, sh_p = _gla_call("hgrn", h, cols_hgrn, lower_bound, nw,
                            jnp.zeros((bp, HGRN_HEADS, HEAD_W, HEAD_W), F32), mix_buffer(), bp, lp, 0, False)
    mix_a, sh_s = _gla_call("hgrn", h, cols_hgrn, lower_bound, nw, state_hgrn[0], mix_a, bs, ls, tp, True)

    def pair_states(s):
        return jnp.swapaxes(s, 2, 3).reshape(s.shape[0], SSM_HEADS // 2, HEAD_W, SSM_STATE)

    def unpair_states(s):
        return jnp.swapaxes(s.reshape(s.shape[0], SSM_HEADS, SSM_HEADDIM, SSM_STATE), 2, 3)

    pad8 = lambda v: jnp.pad(v.astype(F32), (0, LANE - SSM_HEADS)).reshape(1, LANE)
    ssd_args = (mamba_conv_w[0], _row(mamba_conv_b[0]), pad8(mamba_dt_bias[0]), pad8(mamba_a_log[0]),
                _row(jnp.repeat(mamba_d[0], SSM_HEADDIM)), _row(mamba_norm_w[0]))
    mix_b, cm_p, ss_p = _ssd_call(h, col_z, col_x, col_dt, jnp.zeros((bp,) + cache_mamba_conv.shape[2:], F32),
                                  jnp.zeros((bp, SSM_HEADS // 2, HEAD_W, SSM_STATE), F32), *ssd_args,
                                  mix_buffer(), bp, lp, 0, False)
    mix_b, cm_s, ss_s = _ssd_call(h, col_z, col_x, col_dt, cache_mamba_conv[0], pair_states(state_ssm[0]),
                                  *ssd_args, mix_b, bs, ls, tp, True)
    y_prompt, y_sample = finish_layer(1, x, mix_a, mix_b, w_out_odd[0], t_first=tp)
    y_prompt = y_prompt.reshape(bp, lp, D_MODEL)
    y_sample = y_sample.reshape(bs, ls, D_MODEL)
    return (y_prompt, y_sample, gla_p, gla_s, conf_p, conf_s, sh_p[None], sh_s[None],
            unpair_states(ss_p)[None], unpair_states(ss_s)[None], cm_p[None], cm_s[None])
```

```python
import functools

import jax
import jax.numpy as jnp
from jax import lax
from jax.experimental import pallas as pl
from jax.experimental.pallas import tpu as pltpu
from jax.experimental.pallas import tpu_sc as plsc

F32 = jnp.float32
BF16 = jnp.bfloat16
I32 = jnp.int32
U32 = jnp.uint32
HIGHEST = lax.Precision.HIGHEST

D_MODEL = 1024
DEPTH = 2
DEEPNORM_ALPHA = (2.0 * DEPTH) ** 0.25
LN_EPS = 1e-5
RMS_EPS = 1e-6
LANE = 128
HEAD_W = 128
GLA_HEADS, GLA_DK, GLA_RANK, GLA_TAU = 4, 64, 16, 16.0
CONF_DIM, CONF_WIDTH = 512, 31
HGRN_HEADS, HGRN_DK = 4, 128
SSM_HEADS, SSM_HEADDIM, SSM_STATE, SSM_GROUPS, SSM_CONV = 8, 64, 128, 2, 4
SSM_INNER = SSM_HEADS * SSM_HEADDIM
SSM_CONV_DIM = SSM_INNER + 2 * SSM_GROUPS * SSM_STATE
N_EXPERTS, TOP_K = 32, 4
SWIGLU_ALPHA, SWIGLU_LIMIT = 1.702, 7.0
SCAN_CHUNK = 64
PROMPT_TILE = 256
SAMPLE_SEQS = 16
TOKEN_TILE = 512
MOE_ROWS = 512
SC_CORES, SC_SUBCORES = 2, 16
SC_ROWS = 64
VMEM_LIMIT = 56 * 1024 * 1024


def _cparams(n_axes):
    return pltpu.CompilerParams(dimension_semantics=("arbitrary",) * n_axes, vmem_limit_bytes=VMEM_LIMIT)


def _silu(x):
    return x * jax.nn.sigmoid(x)


def _softplus(x):
    return jnp.maximum(x, 0.0) + jnp.log(1.0 + jnp.exp(-jnp.abs(x)))


def _log_sigmoid(x):
    return jnp.minimum(x, 0.0) - jnp.log(1.0 + jnp.exp(-jnp.abs(x)))


def _layernorm(y, g, b):
    mu = jnp.mean(y, axis=-1, keepdims=True)
    d = y - mu
    var = jnp.mean(d * d, axis=-1, keepdims=True)
    return d * lax.rsqrt(var + LN_EPS) * g + b


def _dot(a, b):
    return jnp.dot(a.astype(BF16), b.astype(BF16), preferred_element_type=F32)


def _dot_nt(a, b):
    return lax.dot_general(a.astype(BF16), b.astype(BF16), (((1,), (1,)), ((), ())), preferred_element_type=F32)


def _dot_tn(a, b):
    return lax.dot_general(a.astype(BF16), b.astype(BF16), (((0,), (0,)), ((), ())), preferred_element_type=F32)


def _tri(c):
    r = lax.broadcasted_iota(I32, (c, c), 0)
    k = lax.broadcasted_iota(I32, (c, c), 1)
    return r >= k


def _inproj_body(x_ref, w_ref, o_ref):
    xb = x_ref[...].astype(BF16)
    n = w_ref.shape[1]
    for c0 in range(0, n, 512):
        c1 = min(c0 + 512, n)
        o_ref[:, c0:c1] = jnp.dot(xb, w_ref[:, c0:c1], preferred_element_type=F32)


def _inproj(x, w):
    t, k = x.shape
    n = w.shape[1]
    return pl.pallas_call(
        _inproj_body,
        grid=(t // TOKEN_TILE,),
        in_specs=[pl.BlockSpec((TOKEN_TILE, k), lambda i: (i, 0)),
                  pl.BlockSpec((k, n), lambda i: (0, 0))],
        out_specs=pl.BlockSpec((TOKEN_TILE, n), lambda i: (i, 0)),
        out_shape=jax.ShapeDtypeStruct((t, n), F32),
        compiler_params=_cparams(1),
        name="inproj",
    )(x, w)


def _pack_halves(y):
    half = y.shape[1] // 2
    hi = lax.bitcast_convert_type(y[:, :half].astype(BF16).astype(F32), U32)
    lo = lax.bitcast_convert_type(y[:, half:].astype(BF16).astype(F32), U32)
    return (hi & jnp.uint32(0xFFFF0000)) | (lo >> 16)


def _unpack_halves(w):
    hi = lax.bitcast_convert_type(w & jnp.uint32(0xFFFF0000), F32)
    lo = lax.bitcast_convert_type(w << 16, F32)
    return hi, lo


def _outproj_body(a_ref, b_ref, x_ref, wa_ref, wb_ref, g_ref, be_ref, rw_ref, rb_ref,
                  x1_ref, xp_ref, gate_ref, idx_ref, rank_ref, cnt_ref, carry):
    @pl.when(pl.program_id(0) == 0)
    def _():
        carry[...] = jnp.zeros(carry.shape, F32)

    mix = (jnp.dot(a_ref[...], wa_ref[...], preferred_element_type=F32)
           + jnp.dot(b_ref[...], wb_ref[...], preferred_element_type=F32))
    x1 = _layernorm(DEEPNORM_ALPHA * x_ref[...] + mix, g_ref[...], be_ref[...])
    x1_ref[...] = x1
    xp_ref[...] = _pack_halves(x1)
    logits = _dot(x1, rw_ref[...]) + rb_ref[...]
    lane = lax.broadcasted_iota(I32, logits.shape, 1)
    vals, idxs = [], []
    for _ in range(TOP_K):
        m = jnp.max(logits, axis=-1, keepdims=True)
        sel = jnp.min(jnp.where(logits == m, lane, LANE), axis=-1, keepdims=True)
        vals.append(m)
        idxs.append(sel)
        logits = jnp.where(lane == sel, -jnp.inf, logits)
    exps = [jnp.exp(v - vals[0]) for v in vals]
    inv = 1.0 / functools.reduce(lambda p, q: p + q, exps)
    gates = jnp.zeros(logits.shape, F32)
    eidx = jnp.zeros(logits.shape, I32)
    for k in range(TOP_K):
        gates = jnp.where(lane == k, exps[k] * inv, gates)
        eidx = jnp.where(lane == k, idxs[k], eidx)
    gate_ref[...] = gates
    idx_ref[...] = eidx
    tm = logits.shape[0]
    chosen = jnp.zeros(logits.shape, F32)
    for k in range(TOP_K):
        chosen = chosen + (lane == idxs[k]).astype(F32)
    earlier = lax.broadcasted_iota(I32, (tm, tm), 0) > lax.broadcasted_iota(I32, (tm, tm), 1)
    before = carry[...] + jnp.dot(earlier.astype(BF16), chosen.astype(BF16), preferred_element_type=F32)
    ranks = jnp.zeros(logits.shape, F32)
    for k in range(TOP_K):
        rk = jnp.sum(jnp.where(lane == idxs[k], before, 0.0), axis=-1, keepdims=True)
        ranks = jnp.where(lane == k, rk, ranks)
    rank_ref[...] = ranks.astype(I32)
    carry[...] = carry[...] + jnp.sum(chosen, axis=0, keepdims=True)
    cnt_ref[...] = carry[...].astype(I32)


def _outproj_ln_router(a, b, x, wa, wb, g, be, rw, rb):
    t = x.shape[0]
    tm = TOKEN_TILE
    row = lambda i: (i, 0)
    fix = lambda i: (0, 0)
    return pl.pallas_call(
        _outproj_body,
        grid=(t // tm,),
        in_specs=[pl.BlockSpec((tm, 512), row), pl.BlockSpec((tm, 512), row), pl.BlockSpec((tm, D_MODEL), row),
                  pl.BlockSpec((512, D_MODEL), fix), pl.BlockSpec((512, D_MODEL), fix),
                  pl.BlockSpec((1, D_MODEL), fix), pl.BlockSpec((1, D_MODEL), fix),
                  pl.BlockSpec((D_MODEL, LANE), fix), pl.BlockSpec((1, LANE), fix)],
        out_specs=[pl.BlockSpec((tm, D_MODEL), row), pl.BlockSpec((tm, 512), row),
                   pl.BlockSpec((tm, LANE), row), pl.BlockSpec((tm, LANE), row), pl.BlockSpec((tm, LANE), row),
                   pl.BlockSpec((1, LANE), fix)],
        out_shape=[jax.ShapeDtypeStruct((t, D_MODEL), F32), jax.ShapeDtypeStruct((t, 512), U32),
                   jax.ShapeDtypeStruct((t, LANE), F32), jax.ShapeDtypeStruct((t, LANE), I32),
                   jax.ShapeDtypeStruct((t, LANE), I32), jax.ShapeDtypeStruct((1, LANE), I32)],
        scratch_shapes=[pltpu.VMEM((1, LANE), F32)],
        compiler_params=_cparams(1),
        name="outproj_ln_router",
    )(a, b, x, wa, wb, g, be, rw, rb)


def _sc_worker_rows(n):
    per_worker = n // (SC_CORES * SC_SUBCORES)
    worker = lax.axis_index("s") * SC_CORES + lax.axis_index("c")
    return worker * per_worker, per_worker // SC_ROWS


def _sc_mesh():
    return plsc.VectorSubcoreMesh(core_axis_name="c", subcore_axis_name="s")


def _sc_scratch(w, dtype):
    return ([pltpu.VMEM((SC_ROWS,), I32)] * 2 + [pltpu.VMEM((SC_ROWS, w), dtype)] * 2
            + [pltpu.SemaphoreType.DMA] * 4)


def _sc_chunk_pair(base, p):
    off_a = pl.multiple_of(base + 2 * p * SC_ROWS, 8)
    return off_a, pl.multiple_of(off_a + SC_ROWS, 8)


def _sc_scatter_rows(src, idx, n_out):
    n, (n_src, w) = idx.shape[0], src.shape
    assert n % (SC_CORES * SC_SUBCORES * SC_ROWS * 2) == 0 and n_src % SC_ROWS == 0

    @functools.partial(pl.kernel, mesh=_sc_mesh(), out_type=jax.ShapeDtypeStruct((n_out, w), src.dtype),
                       scratch_types=_sc_scratch(w, src.dtype))
    def scatter(src_hbm, idx_hbm, out_hbm, idx_a, idx_b, rows_a, rows_b, sem_ra, sem_rb, sem_wa, sem_wb):
        base, chunks = _sc_worker_rows(n)

        def src_rows(off):
            return src_hbm.at[pl.ds(pl.multiple_of(lax.rem(off, n_src), 8), SC_ROWS)]

        @pl.loop(0, chunks // 2)
        def _(p):
            off_a, off_b = _sc_chunk_pair(base, p)
            read_a = pltpu.async_copy(src_rows(off_a), rows_a, sem_ra)
            read_b = pltpu.async_copy(src_rows(off_b), rows_b, sem_rb)
            pltpu.sync_copy(idx_hbm.at[pl.ds(off_a, SC_ROWS)], idx_a)
            pltpu.sync_copy(idx_hbm.at[pl.ds(off_b, SC_ROWS)], idx_b)
            read_a.wait()
            write_a = pltpu.async_copy(rows_a, out_hbm.at[idx_a], sem_wa)
            read_b.wait()
            write_b = pltpu.async_copy(rows_b, out_hbm.at[idx_b], sem_wb)
            write_a.wait()
            write_b.wait()

    return scatter(src, idx)


def _sc_gather_rows(table, idx):
    n, w = idx.shape[0], table.shape[1]
    assert n % (SC_CORES * SC_SUBCORES * SC_ROWS * 2) == 0

    @functools.partial(pl.kernel, mesh=_sc_mesh(), out_type=jax.ShapeDtypeStruct((n, w), table.dtype),
                       scratch_types=_sc_scratch(w, table.dtype))
    def gather(table_hbm, idx_hbm, out_hbm, idx_a, idx_b, rows_a, rows_b, sem_ra, sem_rb, sem_wa, sem_wb):
        base, chunks = _sc_worker_rows(n)

        @pl.loop(0, chunks // 2)
        def _(p):
            off_a, off_b = _sc_chunk_pair(base, p)
            pltpu.sync_copy(idx_hbm.at[pl.ds(off_a, SC_ROWS)], idx_a)
            read_a = pltpu.async_copy(table_hbm.at[idx_a], rows_a, sem_ra)
            pltpu.sync_copy(idx_hbm.at[pl.ds(off_b, SC_ROWS)], idx_b)
            read_b = pltpu.async_copy(table_hbm.at[idx_b], rows_b, sem_rb)
            read_a.wait()
            write_a = pltpu.async_copy(rows_a, out_hbm.at[pl.ds(off_a, SC_ROWS)], sem_wa)
            read_b.wait()
            write_b = pltpu.async_copy(rows_b, out_hbm.at[pl.ds(off_b, SC_ROWS)], sem_wb)
            write_a.wait()
            write_b.wait()

    return gather(table, idx)


def _experts_body(b0_ref, nb_ref, last_ref, xs_hbm, wg_ref, bg_ref, wu_ref, bu_ref, wd_ref, bd_ref, o_hbm,
                  wg_s, wu_s, wd_s, xbuf, obuf, sem_in, sem_out):
    e = pl.program_id(0)
    first_blk, n_blk, last_valid = b0_ref[e], nb_ref[e], last_ref[e]
    half_rows = MOE_ROWS // 2

    def rows_of(j):
        return pl.ds(pl.multiple_of((first_blk + j) * MOE_ROWS, MOE_ROWS), MOE_ROWS)

    def fetch(j, slot):
        return pltpu.make_async_copy(xs_hbm.at[rows_of(j)], xbuf.at[slot], sem_in.at[slot])

    def put(j, slot):
        return pltpu.make_async_copy(obuf.at[slot], o_hbm.at[rows_of(j)], sem_out.at[slot])

    @pl.when(n_blk > 0)
    def _():
        fetch(0, 0).start()
        wg_s[...] = wg_ref[...].astype(BF16)
        wu_s[...] = wu_ref[...].astype(BF16)
        wd_s[...] = wd_ref[...].astype(BF16)

    def compute(slot, rows):
        half = D_MODEL // 2
        x_hi, x_lo = _unpack_halves(xbuf[slot, 0:rows, :])
        x_hi = x_hi.astype(BF16)
        x_lo = x_lo.astype(BF16)
        g = (jnp.dot(x_hi, wg_s[:half, :], preferred_element_type=F32)
             + jnp.dot(x_lo, wg_s[half:, :], preferred_element_type=F32) + bg_ref[...])
        u = (jnp.dot(x_hi, wu_s[:half, :], preferred_element_type=F32)
             + jnp.dot(x_lo, wu_s[half:, :], preferred_element_type=F32) + bu_ref[...])
        g = jnp.minimum(g, SWIGLU_LIMIT)
        u = jnp.clip(u, -SWIGLU_LIMIT, SWIGLU_LIMIT)
        hmid = (u + 1.0) * (g * jax.nn.sigmoid(SWIGLU_ALPHA * g))
        out = jnp.dot(hmid.astype(BF16), wd_s[...], preferred_element_type=F32) + bd_ref[...]
        obuf[slot, 0:rows, :] = _pack_halves(out)

    def block(j, carry):
        slot = lax.rem(j, 2)
        fetch(j, slot).wait()

        @pl.when(j + 1 < n_blk)
        def _():
            fetch(j + 1, 1 - slot).start()

        @pl.when(j >= 2)
        def _():
            put(j - 2, slot).wait()

        valid = jnp.where(j == n_blk - 1, last_valid, MOE_ROWS)

        @pl.when(valid > half_rows)
        def _():
            compute(slot, MOE_ROWS)

        @pl.when(valid <= half_rows)
        def _():
            compute(slot, half_rows)
            obuf[slot, half_rows:, :] = jnp.zeros((MOE_ROWS - half_rows, obuf.shape[2]), obuf.dtype)

        put(j, slot).start()
        return carry

    lax.fori_loop(0, n_blk, block, 0)

    @pl.when(n_blk >= 2)
    def _():
        put(n_blk - 2, lax.rem(n_blk, 2)).wait()

    @pl.when(n_blk >= 1)
    def _():
        put(n_blk - 1, lax.rem(n_blk - 1, 2)).wait()


def _experts(layer, first_blk, n_blk, last_valid, xs, wg, bg, wu, bu, wd, bd):
    n_rows, w = xs.shape
    wsel = lambda e, b0, nb, lv: (layer, e, 0, 0)
    wspec = pl.BlockSpec((None, None, D_MODEL, D_MODEL), wsel)
    bspec = pl.BlockSpec((None, None, 1, D_MODEL), wsel)
    bias = lambda b: b.reshape(b.shape[0], b.shape[1], 1, b.shape[2])
    return pl.pallas_call(
        _experts_body,
        grid_spec=pltpu.PrefetchScalarGridSpec(
            num_scalar_prefetch=3,
            grid=(N_EXPERTS,),
            in_specs=[pl.BlockSpec(memory_space=pl.ANY), wspec, bspec, wspec, bspec, wspec, bspec],
            out_specs=pl.BlockSpec(memory_space=pl.ANY),
            scratch_shapes=[pltpu.VMEM((D_MODEL, D_MODEL), BF16)] * 3
            + [pltpu.VMEM((2, MOE_ROWS, w), U32)] * 2 + [pltpu.SemaphoreType.DMA((2,))] * 2,
        ),
        out_shape=jax.ShapeDtypeStruct((n_rows, w), U32),
        compiler_params=_cparams(1),
        name="experts",
    )(first_blk, n_blk, last_valid, xs, wg, bias(bg), wu, bias(bu), wd, bias(bd))


def _combine_body(n_first, o0_ref, o1_ref, o2_ref, o3_ref, gt_ref, x_ref, g_ref, b_ref, ya_ref, yb_ref=None):
    half = D_MODEL // 2
    gates = gt_ref[...]
    hi = jnp.zeros((x_ref.shape[0], half), F32)
    lo = jnp.zeros((x_ref.shape[0], half), F32)
    for k, o_ref in enumerate((o0_ref, o1_ref, o2_ref, o3_ref)):
        h, l = _unpack_halves(o_ref[...])
        gk = gates[:, k:k + 1]
        hi = hi + gk * h
        lo = lo + gk * l
    x = x_ref[...]
    y_hi = DEEPNORM_ALPHA * x[:, :half] + hi
    y_lo = DEEPNORM_ALPHA * x[:, half:] + lo
    mu = (jnp.sum(y_hi, axis=-1, keepdims=True) + jnp.sum(y_lo, axis=-1, keepdims=True)) * (1.0 / D_MODEL)
    d_hi = y_hi - mu
    d_lo = y_lo - mu
    var = (jnp.sum(d_hi * d_hi, axis=-1, keepdims=True) + jnp.sum(d_lo * d_lo, axis=-1, keepdims=True)) * (1.0 / D_MODEL)
    r = lax.rsqrt(var + LN_EPS)
    out_hi = d_hi * r * g_ref[:, :half] + b_ref[:, :half]
    out_lo = d_lo * r * g_ref[:, half:] + b_ref[:, half:]

    def write(y_ref):
        y_ref[:, :half] = out_hi
        y_ref[:, half:] = out_lo

    if yb_ref is None:
        write(ya_ref)
    else:
        pl.when(pl.program_id(0) < n_first)(lambda: write(ya_ref))
        pl.when(pl.program_id(0) >= n_first)(lambda: write(yb_ref))


def _combine_ln(o4, gates, x, g, b, t_first=None):
    t = x.shape[0]
    tm = TOKEN_TILE
    row = lambda i: (i, 0)
    fix = lambda i: (0, 0)
    choice = lambda k: pl.BlockSpec((tm, 512), lambda i: (k * (t // tm) + i, 0))
    if t_first is None:
        n_first = None
        out_specs = pl.BlockSpec((tm, D_MODEL), row)
        out_shape = jax.ShapeDtypeStruct((t, D_MODEL), F32)
    else:
        n_first = t_first // tm
        out_specs = [pl.BlockSpec((tm, D_MODEL), lambda i: (jnp.minimum(i, n_first - 1), 0)),
                     pl.BlockSpec((tm, D_MODEL), lambda i: (jnp.maximum(i - n_first, 0), 0))]
        out_shape = [jax.ShapeDtypeStruct((t_first, D_MODEL), F32), jax.ShapeDtypeStruct((t - t_first, D_MODEL), F32)]
    return pl.pallas_call(
        functools.partial(_combine_body, n_first),
        grid=(t // tm,),
        in_specs=[choice(0), choice(1), choice(2), choice(3), pl.BlockSpec((tm, LANE), row),
                  pl.BlockSpec((tm, D_MODEL), row), pl.BlockSpec((1, D_MODEL), fix), pl.BlockSpec((1, D_MODEL), fix)],
        out_specs=out_specs,
        out_shape=out_shape,
        compiler_params=_cparams(1),
        name="combine_ln",
    )(o4, o4, o4, o4, gates, x, g, b)


def _moe(layer, x1, xp, gates, eidx, rank, counts, ln_g, ln_b, wg, bg, wu, bu, wd, bd, t_first=None):
    t = x1.shape[0]
    bm = MOE_ROWS
    n_blocks = t * TOP_K // bm + N_EXPERTS
    n_rows = n_blocks * bm
    cnt = counts[0, :N_EXPERTS]
    padded = (cnt + bm - 1) // bm * bm
    pad_end = jnp.cumsum(padded)
    pad_start = pad_end - padded
    e = eidx[:, :TOP_K]
    start = jnp.sum(jnp.where(e[:, :, None] == jnp.arange(N_EXPERTS, dtype=I32), pad_start, 0), axis=-1)
    dest = (start + rank[:, :TOP_K]).T.reshape(-1)
    n_blk = padded // bm
    last_valid = cnt - (n_blk - 1) * bm
    xs = _sc_scatter_rows(xp, dest, n_rows)
    outs = _experts(layer, pad_start // bm, n_blk, last_valid, xs, wg, bg, wu, bu, wd, bd)
    o4 = _sc_gather_rows(outs, dest)
    return _combine_ln(o4, gates, x1, ln_g, ln_b, t_first)


def _split2(x):
    hi = x.astype(BF16)
    return hi, (x - hi.astype(F32)).astype(BF16)


def _split3(x):
    hi = x.astype(BF16)
    rem = x - hi.astype(F32)
    mid = rem.astype(BF16)
    return hi, mid, (rem - mid.astype(F32)).astype(BF16)


def _chunk_cumsum(g, C):
    rows = g.shape[0]
    r = lax.broadcasted_iota(I32, (rows, rows), 0)
    c = lax.broadcasted_iota(I32, (rows, rows), 1)
    tri = ((r >= c) & (r // C == c // C)).astype(BF16)
    hi, mid, lo = _split3(g)
    dot = lambda part: jnp.dot(tri, part, preferred_element_type=F32)
    return dot(hi) + dot(mid) + dot(lo)


def _gla_body(mode, n_seq, n_chunk, C, *refs):
    if mode == "gla":
        hq_ref, hk_ref, hv_ref, hg_ref, hlr_ref, wlr_ref, blr_ref, nw_ref, s0_ref, _, o_ref, so_ref, st, o_scr = refs
    else:
        hq_ref, hk_ref, hv_ref, hg_ref, lb_ref, nw_ref, s0_ref, _, o_ref, so_ref, st, o_scr = refs
    n_heads = 4
    tstep = pl.program_id(1)

    @pl.when(tstep == 0)
    def _():
        for s in range(n_seq):
            for h in range(n_heads):
                st[s, h] = s0_ref[s, h].T

    if mode == "gla":
        q = hq_ref[...] * (GLA_DK ** -0.5)
        k = hk_ref[...]
        z = _dot(hlr_ref[...], wlr_ref[...]) + blr_ref[...]
        g = _log_sigmoid(z) * (1.0 / GLA_TAU)
    else:
        q = _silu(hq_ref[...]) * (HGRN_DK ** -0.5)
        lb = lb_ref[...]
        f = lb + (1.0 - lb) * jax.nn.sigmoid(hk_ref[...])
        k = 1.0 - f
        g = jnp.log(f)
    v = hv_ref[...]
    causal = _tri(C)
    mid = max(C // 2 - 1, 0)
    b_all = _chunk_cumsum(g, C)
    for s in range(n_seq):
        states = [st[s, h] for h in range(n_heads)]
        for c in range(n_chunk):
            r0 = (s * n_chunk + c) * C
            rs = slice(r0, r0 + C)
            b, qc, kc = b_all[rs, :], q[rs, :], k[rs, :]
            b_last = b[C - 1:C, :]
            b_mid = b[mid:mid + 1, :]
            qe_hi, qe_lo = _split2(qc * jnp.exp(b - b_mid))
            ke_hi, ke_lo = _split2(kc * jnp.exp(b_mid - b))
            q_state = (qc * jnp.exp(b)).astype(BF16)
            k_state = (kc * jnp.exp(b_last - b)).astype(BF16)
            decay = jnp.exp(b_last)
            for h in range(n_heads):
                cs = slice(h * HEAD_W, (h + 1) * HEAD_W)
                lhs = jnp.concatenate([qe_hi[:, cs], qe_hi[:, cs], qe_lo[:, cs]], axis=1)
                rhs = jnp.concatenate([ke_hi[:, cs], ke_lo[:, cs], ke_hi[:, cs]], axis=1)
                scores = jnp.where(causal, _dot_nt(lhs, rhs), 0.0)
                vh = v[rs, cs].astype(BF16)
                o = _dot(scores, vh) + _dot_nt(q_state[:, cs], states[h])
                states[h] = states[h] * decay[:, cs] + _dot_tn(vh, k_state[:, cs])
                ms = jnp.mean(o * o, axis=-1, keepdims=True)
                o_scr[rs, cs] = o * lax.rsqrt(ms + RMS_EPS)
        for h in range(n_heads):
            st[s, h] = states[h]
    o_ref[...] = (o_scr[...] * nw_ref[...] * _silu(hg_ref[...])).astype(BF16)

    @pl.when(tstep == pl.num_programs(1) - 1)
    def _():
        for s in range(n_seq):
            for h in range(n_heads):
                so_ref[s, h] = st[s, h].T


def _seq_layout(n_batch, seq_len, row_off, sample):
    if sample:
        n_seq, n_chunk, C = SAMPLE_SEQS, 1, seq_len
        rows = n_seq * C
        grid = (n_batch // n_seq, 1)
        blk0 = row_off // rows
        rb = lambda i, t: blk0 + i
    else:
        n_seq, n_chunk, C = 1, PROMPT_TILE // SCAN_CHUNK, SCAN_CHUNK
        rows = PROMPT_TILE
        tiles = seq_len // rows
        grid = (n_batch, tiles)
        blk0 = row_off // rows
        rb = lambda i, t: blk0 + i * tiles + t
    return n_seq, n_chunk, C, rows, grid, rb


def _gla_call(mode, h, cols, extra, nw, s0, out_buf, n_batch, seq_len, row_off, sample):
    n_seq, n_chunk, C, rows, grid, rb = _seq_layout(n_batch, seq_len, row_off, sample)
    colspec = lambda c0, w: pl.BlockSpec((rows, w), lambda i, t: (rb(i, t), c0 // w))
    fix2 = lambda i, t: (0, 0)
    in_specs = [colspec(cols[0], 512), colspec(cols[1], 512), colspec(cols[2], 512), colspec(cols[3], 512)]
    args = [h, h, h, h]
    if mode == "gla":
        wlr, blr = extra
        in_specs += [colspec(cols[4], LANE), pl.BlockSpec((LANE, 512), fix2), pl.BlockSpec((1, 512), fix2)]
        args += [h, wlr, blr]
    else:
        in_specs += [pl.BlockSpec((1, 512), fix2)]
        args += [extra]
    st_spec = pl.BlockSpec((n_seq, 4, HEAD_W, HEAD_W), lambda i, t: (i, 0, 0, 0))
    in_specs += [pl.BlockSpec((1, 512), fix2), st_spec, pl.BlockSpec(memory_space=pl.ANY)]
    args += [nw, s0, out_buf]
    o_spec = pl.BlockSpec((rows, 512), lambda i, t: (rb(i, t), 0))
    return pl.pallas_call(
        functools.partial(_gla_body, mode, n_seq, n_chunk, C),
        grid=grid,
        in_specs=in_specs,
        out_specs=[o_spec, st_spec],
        out_shape=[jax.ShapeDtypeStruct(out_buf.shape, out_buf.dtype),
                   jax.ShapeDtypeStruct((n_batch, 4, HEAD_W, HEAD_W), F32)],
        scratch_shapes=[pltpu.VMEM((n_seq, 4, HEAD_W, HEAD_W), F32), pltpu.VMEM((rows, 512), F32)],
        input_output_aliases={len(args) - 1: 0},
        compiler_params=_cparams(2),
        name=mode + ("_sample" if sample else "_prompt"),
    )(*args)


def _round_bf16(x, on=True):
    return x.astype(BF16).astype(F32) if on else x


def _conf_body(n_seq, L, round_x, round_w, a_ref, gt_ref, hist_ref, w_ref, b_ref, g_ref, be_ref, _, o_ref, co_ref,
               buf, bufr, y_scr):
    tstep = pl.program_id(1)
    hist = CONF_WIDTH - 1
    pad = 32 - hist

    @pl.when(tstep == 0)
    def _():
        for s in range(n_seq):
            buf[s, pad:32, :] = hist_ref[s]
            bufr[s, pad:32, :] = _round_bf16(hist_ref[s], round_x)

    u = a_ref[...] * jax.nn.sigmoid(gt_ref[...])
    ur = _round_bf16(u, round_x)
    for s in range(n_seq):
        buf[s, 32:32 + L, :] = u[s * L:(s + 1) * L, :]
        bufr[s, 32:32 + L, :] = ur[s * L:(s + 1) * L, :]
    w = _round_bf16(w_ref[...], round_w)
    for s in range(n_seq):
        acc = jnp.zeros((L, CONF_DIM), F32)
        for j in range(CONF_WIDTH):
            acc = acc + bufr[s, pad + j:pad + j + L, :] * w[j:j + 1, :]
        y_scr[s * L:(s + 1) * L, :] = _silu(_layernorm(acc + b_ref[...], g_ref[...], be_ref[...]))
        tail = buf[s, L + pad:L + 32, :]
        buf[s, pad:32, :] = tail
        tailr = bufr[s, L + pad:L + 32, :]
        bufr[s, pad:32, :] = tailr
    o_ref[...] = y_scr[...].astype(o_ref.dtype)

    @pl.when(tstep == pl.num_programs(1) - 1)
    def _():
        for s in range(n_seq):
            co_ref[s] = buf[s, pad:32, :]


def _conf_call(h, col_a, col_g, cache, w, b, g, be, out_buf, n_batch, seq_len, row_off, sample):
    n_seq, n_chunk, C, rows, grid, rb = _seq_layout(n_batch, seq_len, row_off, sample)
    L = rows // n_seq
    hist = CONF_WIDTH - 1
    colspec = lambda c0: pl.BlockSpec((rows, 512), lambda i, t: (rb(i, t), c0 // 512))
    fix2 = lambda i, t: (0, 0)
    c_spec = pl.BlockSpec((n_seq, hist, CONF_DIM), lambda i, t: (i, 0, 0))
    return pl.pallas_call(
        functools.partial(_conf_body, n_seq, L, True, sample),
        grid=grid,
        in_specs=[colspec(col_a), colspec(col_g), c_spec,
                  pl.BlockSpec((CONF_WIDTH, CONF_DIM), fix2), pl.BlockSpec((1, CONF_DIM), fix2),
                  pl.BlockSpec((1, CONF_DIM), fix2), pl.BlockSpec((1, CONF_DIM), fix2),
                  pl.BlockSpec(memory_space=pl.ANY)],
        out_specs=[pl.BlockSpec((rows, 512), lambda i, t: (rb(i, t), 0)), c_spec],
        out_shape=[jax.ShapeDtypeStruct(out_buf.shape, out_buf.dtype),
                   jax.ShapeDtypeStruct((n_batch, hist, CONF_DIM), F32)],
        scratch_shapes=[pltpu.VMEM((n_seq, 32 + L, CONF_DIM), F32)] * 2 + [pltpu.VMEM((rows, CONF_DIM), F32)],
        input_output_aliases={7: 0},
        compiler_params=_cparams(2),
        name="conformer" + ("_sample" if sample else "_prompt"),
    )(h, h, cache, w, b, g, be, out_buf)


def _ssd_body(n_seq, n_chunk, C, round_x, round_w, hz_ref, hx_ref, hdt_ref, hist_ref, s0_ref, cw_ref, cb_ref, dtb_ref, alog_ref,
              dvec_ref, nw_ref, _, o_ref, co_ref, so_ref, st, buf, bufr, xbc, y_scr):
    tstep = pl.program_id(1)
    L = n_chunk * C
    hist = SSM_CONV - 1
    pad = 8 - hist
    n_pairs = SSM_HEADS // 2

    @pl.when(tstep == 0)
    def _():
        for s in range(n_seq):
            buf[s, pad:8, :] = hist_ref[s]
            bufr[s, pad:8, :] = _round_bf16(hist_ref[s], round_x)
            for m in range(n_pairs):
                st[s, m] = s0_ref[s, m]

    cw = _round_bf16(cw_ref[...], round_w)
    for s in range(n_seq):
        hx = hx_ref[s * L:(s + 1) * L, :]
        buf[s, 8:8 + L, :] = hx
        bufr[s, 8:8 + L, :] = _round_bf16(hx, round_x)
        acc = jnp.zeros((L, SSM_CONV_DIM), F32)
        for j in range(SSM_CONV):
            acc = acc + bufr[s, pad + j:pad + j + L, :] * cw[j:j + 1, :]
        xbc[s * L:(s + 1) * L, :] = _silu(acc + cb_ref[...])
        tail = buf[s, L + pad:L + 8, :]
        buf[s, pad:8, :] = tail
        tailr = bufr[s, L + pad:L + 8, :]
        bufr[s, pad:8, :] = tailr

    dt = _softplus(hdt_ref[...] + dtb_ref[...])
    la = dt * (-jnp.exp(alog_ref[...]))
    hrow = lax.broadcasted_iota(I32, (LANE, SSM_INNER), 0)
    hcol = lax.broadcasted_iota(I32, (LANE, SSM_INNER), 1) // SSM_HEADDIM
    expand = (hrow == hcol).astype(F32)
    dtx = jnp.dot(dt, expand, preferred_element_type=F32, precision=HIGHEST)
    causal = _tri(C)
    tri = causal.astype(BF16)
    lane = lax.broadcasted_iota(I32, (C, HEAD_W), 1)
    bcol_all = _chunk_cumsum(la, C)
    heads_per_group = SSM_HEADS // SSM_GROUPS
    for s in range(n_seq):
        states = [st[s, m] for m in range(n_pairs)]
        for c in range(n_chunk):
            r0 = (s * n_chunk + c) * C
            rs = slice(r0, r0 + C)
            bcol = bcol_all[rs, :]
            brow = functools.reduce(lambda p, q: p + q, [
                lax.dot_general(part, tri, (((0,), (1,)), ((), ())), preferred_element_type=F32)
                for part in _split3(la[rs, :])])
            xs_c = xbc[rs, 0:SSM_INNER]
            v_c = (xs_c * dtx[rs, :]).astype(BF16)
            gmats, bms, cms = [], [], []
            for grp in range(SSM_GROUPS):
                bm = xbc[rs, SSM_INNER + grp * SSM_STATE:SSM_INNER + (grp + 1) * SSM_STATE]
                cm = xbc[rs, SSM_INNER + (SSM_GROUPS + grp) * SSM_STATE:SSM_INNER + (SSM_GROUPS + grp + 1) * SSM_STATE]
                cm_hi, cm_lo = _split2(cm)
                bm_hi, bm_lo = _split2(bm)
                gmats.append(_dot_nt(jnp.concatenate([cm_hi, cm_hi, cm_lo], axis=1),
                                     jnp.concatenate([bm_hi, bm_lo, bm_hi], axis=1)))
                bms.append(bm)
                cms.append(cm)
            for m in range(n_pairs):
                grp = (2 * m) // heads_per_group
                bm, cm, gmat = bms[grp], cms[grp], gmats[grp]
                ps = slice(m * HEAD_W, (m + 1) * HEAD_W)
                vp = v_c[:, ps]
                s_t = states[m]
                o_halves, new_rows = [], []
                for hh in range(2):
                    hd = 2 * m + hh
                    bc = bcol[:, hd:hd + 1]
                    br = brow[hd:hd + 1, :]
                    dec = jnp.where(causal, jnp.exp(jnp.minimum(bc - br, 0.0)), 0.0)
                    b_last = bcol[C - 1:C, hd:hd + 1]
                    o_halves.append(_dot(gmat * dec, vp) + _dot_nt(cm * jnp.exp(bc), s_t))
                    kv = _dot_tn(vp, bm * jnp.exp(b_last - bc))
                    vs = slice(hh * SSM_HEADDIM, (hh + 1) * SSM_HEADDIM)
                    new_rows.append(s_t[vs, :] * jnp.exp(b_last) + kv[vs, :])
                states[m] = jnp.concatenate(new_rows, axis=0)
                o_pair = jnp.where(lane < SSM_HEADDIM, o_halves[0], o_halves[1])
                y_scr[rs, ps] = o_pair + dvec_ref[:, ps] * xs_c[:, ps]
        for m in range(n_pairs):
            st[s, m] = states[m]
    y = y_scr[...] * _silu(hz_ref[...])
    gw = SSM_INNER // SSM_GROUPS
    for grp in range(SSM_GROUPS):
        gs = slice(grp * gw, (grp + 1) * gw)
        yg = y[:, gs]
        ms = jnp.mean(yg * yg, axis=-1, keepdims=True)
        o_ref[:, gs] = (yg * lax.rsqrt(ms + RMS_EPS) * nw_ref[:, gs]).astype(BF16)

    @pl.when(tstep == pl.num_programs(1) - 1)
    def _():
        for s in range(n_seq):
            co_ref[s] = buf[s, pad:8, :]
            for m in range(n_pairs):
                so_ref[s, m] = st[s, m]


def _ssd_call(h, col_z, col_x, col_dt, cache, s0, cw, cb, dtb, alog, dvec, nw, out_buf, n_batch, seq_len, row_off,
              sample):
    n_seq, n_chunk, C, rows, grid, rb = _seq_layout(n_batch, seq_len, row_off, sample)
    L = rows // n_seq
    hist = SSM_CONV - 1
    n_pairs = SSM_HEADS // 2
    colspec = lambda c0, w: pl.BlockSpec((rows, w), lambda i, t: (rb(i, t), c0 // w))
    fix2 = lambda i, t: (0, 0)
    c_spec = pl.BlockSpec((n_seq, hist, SSM_CONV_DIM), lambda i, t: (i, 0, 0))
    st_spec = pl.BlockSpec((n_seq, n_pairs, HEAD_W, SSM_STATE), lambda i, t: (i, 0, 0, 0))
    return pl.pallas_call(
        functools.partial(_ssd_body, n_seq, n_chunk, C, sample, True),
        grid=grid,
        in_specs=[colspec(col_z, 512), colspec(col_x, SSM_CONV_DIM), colspec(col_dt, LANE), c_spec, st_spec,
                  pl.BlockSpec((SSM_CONV, SSM_CONV_DIM), fix2), pl.BlockSpec((1, SSM_CONV_DIM), fix2),
                  pl.BlockSpec((1, LANE), fix2), pl.BlockSpec((1, LANE), fix2),
                  pl.BlockSpec((1, SSM_INNER), fix2), pl.BlockSpec((1, SSM_INNER), fix2),
                  pl.BlockSpec(memory_space=pl.ANY)],
        out_specs=[pl.BlockSpec((rows, 512), lambda i, t: (rb(i, t), 0)), c_spec, st_spec],
        out_shape=[jax.ShapeDtypeStruct(out_buf.shape, out_buf.dtype),
                   jax.ShapeDtypeStruct((n_batch, hist, SSM_CONV_DIM), F32),
                   jax.ShapeDtypeStruct((n_batch, n_pairs, HEAD_W, SSM_STATE), F32)],
        scratch_shapes=[pltpu.VMEM((n_seq, n_pairs, HEAD_W, SSM_STATE), F32),
                        pltpu.VMEM((n_seq, 8 + L, SSM_CONV_DIM), F32),
                        pltpu.VMEM((n_seq, 8 + L, SSM_CONV_DIM), F32),
                        pltpu.VMEM((rows, SSM_CONV_DIM), F32),
                        pltpu.VMEM((rows, SSM_INNER), F32)],
        input_output_aliases={11: 0},
        compiler_params=_cparams(2),
        name="ssd" + ("_sample" if sample else "_prompt"),
    )(h, h, h, cache, s0, cw, cb, dtb, alog, dvec, nw, out_buf)


def _pad_heads(w, n_heads, width):
    lead = w.shape[:-1]
    w = w.reshape(lead + (n_heads, width))
    w = jnp.pad(w, [(0, 0)] * len(lead) + [(0, 0), (0, HEAD_W - width)])
    return w.reshape(lead + (n_heads * HEAD_W,))


def _row(v):
    return v.reshape(1, -1).astype(F32)


def kernel(x_prompt, x_sample, state_gla, cache_conformer, state_hgrn, state_ssm, cache_mamba_conv, w_in_even, w_gla_gate_lr, b_gla_gate, gla_norm_w, conf_conv_w, conf_conv_b, conf_ln_g, conf_ln_b, w_out_even, w_in_odd, hgrn_lower_bounds, hgrn_norm_w, mamba_conv_w, mamba_conv_b, mamba_dt_bias, mamba_a_log, mamba_d, mamba_norm_w, w_out_odd, ln1_g, ln1_b, ln2_g, ln2_b, router_w, router_b, expert_w_gate, expert_b_gate, expert_w_up, expert_b_up, expert_w_down, expert_b_down):
    bp, lp, _ = x_prompt.shape
    bs, ls, _ = x_sample.shape
    tp, ts = bp * lp, bs * ls
    x = jnp.concatenate([x_prompt.reshape(tp, D_MODEL), x_sample.reshape(ts, D_MODEL)], axis=0)

    def router_params(layer):
        rw = jnp.pad(router_w[layer], ((0, 0), (0, LANE - N_EXPERTS)))
        rb = jnp.pad(router_b[layer].astype(F32), (0, LANE - N_EXPERTS), constant_values=-1e30)
        return rw, rb.reshape(1, LANE)

    def finish_layer(layer, x, mix_a, mix_b, w_out, t_first=None):
        rw, rb = router_params(layer)
        x1, xp, gates, eidx, rank, counts = _outproj_ln_router(
            mix_a, mix_b, x, w_out[:512].astype(BF16), w_out[512:].astype(BF16),
            _row(ln1_g[layer]), _row(ln1_b[layer]), rw, rb)
        return _moe(layer, x1, xp, gates, eidx, rank, counts, _row(ln2_g[layer]), _row(ln2_b[layer]),
                    expert_w_gate, expert_b_gate, expert_w_up, expert_b_up, expert_w_down, expert_b_down, t_first)

    def mix_buffer():
        return jnp.zeros((tp + ts, 512), BF16)

    wi = w_in_even[0]
    wq, wk, wv, wg, wlr, wglu = jnp.split(wi, [256, 512, 1024, 1536, 1552], axis=1)
    w_even = jnp.concatenate([_pad_heads(wq, GLA_HEADS, GLA_DK), _pad_heads(wk, GLA_HEADS, GLA_DK), wv, wg, wglu,
                              jnp.pad(wlr, ((0, 0), (0, LANE - GLA_RANK)))], axis=1).astype(BF16)
    cols_gla = (0, 512, 1024, 1536, 3072)
    col_a, col_gate = 2048, 2560
    h = _inproj(x, w_even)
    wlr_p = jnp.pad(_pad_heads(w_gla_gate_lr[0], GLA_HEADS, GLA_DK), ((0, LANE - GLA_RANK), (0, 0)))
    blr_p = _row(_pad_heads(b_gla_gate[0], GLA_HEADS, GLA_DK))
    nw = _row(gla_norm_w[0])
    conf_args = (conf_conv_w[0], _row(conf_conv_b[0]), _row(conf_ln_g[0]), _row(conf_ln_b[0]))
    s0_p = jnp.zeros((bp, GLA_HEADS, HEAD_W, HEAD_W), F32)
    s0_s = jnp.pad(state_gla[0], ((0, 0), (0, 0), (0, HEAD_W - GLA_DK), (0, 0)))
    mix_a, sg_p = _gla_call("gla", h, cols_gla, (wlr_p, blr_p), nw, s0_p, mix_buffer(), bp, lp, 0, False)
    mix_a, sg_s = _gla_call("gla", h, cols_gla, (wlr_p, blr_p), nw, s0_s, mix_a, bs, ls, tp, True)
    mix_b, cc_p = _conf_call(h, col_a, col_gate, jnp.zeros((bp,) + cache_conformer.shape[2:], F32), *conf_args,
                             mix_buffer(), bp, lp, 0, False)
    mix_b, cc_s = _conf_call(h, col_a, col_gate, cache_conformer[0], *conf_args, mix_b, bs, ls, tp, True)
    x = finish_layer(0, x, mix_a, mix_b, w_out_even[0])
    gla_p, gla_s = sg_p[:, :, :GLA_DK, :][None], sg_s[:, :, :GLA_DK, :][None]
    conf_p, conf_s = cc_p[None], cc_s[None]

    lb_cum = jnp.cumsum(jax.nn.softmax(hgrn_lower_bounds.astype(F32), axis=0), axis=0)
    lower_bound = _row((lb_cum - lb_cum[0])[1])
    wo = w_in_odd[0]
    w_odd = jnp.concatenate([wo[:, 2560:3584], wo[:, :2560],
                             jnp.pad(wo[:, 3584:], ((0, 0), (0, LANE - SSM_HEADS)))], axis=1).astype(BF16)
    h = _inproj(x, w_odd)
    cols_hgrn = (1024, 1536, 2048, 2560)
    col_z, col_x, col_dt = 3072, 0, 3584
    nw = _row(hgrn_norm_w[0])
    mix_a, sh_p = _gla_call("hgrn", h, cols_hgrn, lower_bound, nw,
                            jnp.zeros((bp, HGRN_HEADS, HEAD_W, HEAD_W), F32), mix_buffer(), bp, lp, 0, False)
    mix_a, sh_s = _gla_call("hgrn", h, cols_hgrn, lower_bound, nw, state_hgrn[0], mix_a, bs, ls, tp, True)

    def pair_states(s):
        return jnp.swapaxes(s, 2, 3).reshape(s.shape[0], SSM_HEADS // 2, HEAD_W, SSM_STATE)

    def unpair_states(s):
        return jnp.swapaxes(s.reshape(s.shape[0], SSM_HEADS, SSM_HEADDIM, SSM_STATE), 2, 3)

    pad8 = lambda v: jnp.pad(v.astype(F32), (0, LANE - SSM_HEADS)).reshape(1, LANE)
    ssd_args = (mamba_conv_w[0], _row(mamba_conv_b[0]), pad8(mamba_dt_bias[0]), pad8(mamba_a_log[0]),
                _row(jnp.repeat(mamba_d[0], SSM_HEADDIM)), _row(mamba_norm_w[0]))
    mix_b, cm_p, ss_p = _ssd_call(h, col_z, col_x, col_dt, jnp.zeros((bp,) + cache_mamba_conv.shape[2:], F32),
                                  jnp.zeros((bp, SSM_HEADS // 2, HEAD_W, SSM_STATE), F32), *ssd_args,
                                  mix_buffer(), bp, lp, 0, False)
    mix_b, cm_s, ss_s = _ssd_call(h, col_z, col_x, col_dt, cache_mamba_conv[0], pair_states(state_ssm[0]),
                                  *ssd_args, mix_b, bs, ls, tp, True)
    y_prompt, y_sample = finish_layer(1, x, mix_a, mix_b, w_out_odd[0], t_first=tp)
    y_prompt = y_prompt.reshape(bp, lp, D_MODEL)
    y_sample = y_sample.reshape(bs, ls, D_MODEL)
    return (y_prompt, y_sample, gla_p, gla_s, conf_p, conf_s, sh_p[None], sh_s[None],
            unpair_states(ss_p)[None], unpair_states(ss_s)[None], cm_p[None], cm_s[None])
```

```python
import functools

import jax
import jax.numpy as jnp
from jax import lax
from jax.experimental import pallas as pl
from jax.experimental.pallas import tpu as pltpu
from jax.experimental.pallas import tpu_sc as plsc

F32 = jnp.float32
BF16 = jnp.bfloat16
I32 = jnp.int32
U32 = jnp.uint32
HIGHEST = lax.Precision.HIGHEST

D_MODEL = 1024
DEPTH = 2
DEEPNORM_ALPHA = (2.0 * DEPTH) ** 0.25
LN_EPS = 1e-5
RMS_EPS = 1e-6
LANE = 128
HEAD_W = 128
GLA_HEADS, GLA_DK, GLA_RANK, GLA_TAU = 4, 64, 16, 16.0
CONF_DIM, CONF_WIDTH = 512, 31
HGRN_HEADS, HGRN_DK = 4, 128
SSM_HEADS, SSM_HEADDIM, SSM_STATE, SSM_GROUPS, SSM_CONV = 8, 64, 128, 2, 4
SSM_INNER = SSM_HEADS * SSM_HEADDIM
SSM_CONV_DIM = SSM_INNER + 2 * SSM_GROUPS * SSM_STATE
N_EXPERTS, TOP_K = 32, 4
SWIGLU_ALPHA, SWIGLU_LIMIT = 1.702, 7.0
SCAN_CHUNK = 64
PROMPT_TILE = 256
SAMPLE_SEQS = 16
TOKEN_TILE = 512
MOE_ROWS = 512
SC_CORES, SC_SUBCORES = 2, 16
SC_ROWS = 64
VMEM_LIMIT = 56 * 1024 * 1024


def _cparams(n_axes):
    return pltpu.CompilerParams(dimension_semantics=("arbitrary",) * n_axes, vmem_limit_bytes=VMEM_LIMIT)


def _silu(x):
    return x * jax.nn.sigmoid(x)


def _softplus(x):
    return jnp.maximum(x, 0.0) + jnp.log(1.0 + jnp.exp(-jnp.abs(x)))


def _log_sigmoid(x):
    return jnp.minimum(x, 0.0) - jnp.log(1.0 + jnp.exp(-jnp.abs(x)))


def _layernorm(y, g, b):
    mu = jnp.mean(y, axis=-1, keepdims=True)
    d = y - mu
    var = jnp.mean(d * d, axis=-1, keepdims=True)
    return d * lax.rsqrt(var + LN_EPS) * g + b


def _dot(a, b):
    return jnp.dot(a.astype(BF16), b.astype(BF16), preferred_element_type=F32)


def _dot_nt(a, b):
    return lax.dot_general(a.astype(BF16), b.astype(BF16), (((1,), (1,)), ((), ())), preferred_element_type=F32)


def _dot_tn(a, b):
    return lax.dot_general(a.astype(BF16), b.astype(BF16), (((0,), (0,)), ((), ())), preferred_element_type=F32)


def _tri(c):
    r = lax.broadcasted_iota(I32, (c, c), 0)
    k = lax.broadcasted_iota(I32, (c, c), 1)
    return r >= k


def _inproj_body(x_ref, w_ref, o_ref):
    xb = x_ref[...].astype(BF16)
    n = w_ref.shape[1]
    for c0 in range(0, n, 512):
        c1 = min(c0 + 512, n)
        o_ref[:, c0:c1] = jnp.dot(xb, w_ref[:, c0:c1], preferred_element_type=F32)


def _inproj(x, w):
    t, k = x.shape
    n = w.shape[1]
    return pl.pallas_call(
        _inproj_body,
        grid=(t // TOKEN_TILE,),
        in_specs=[pl.BlockSpec((TOKEN_TILE, k), lambda i: (i, 0)),
                  pl.BlockSpec((k, n), lambda i: (0, 0))],
        out_specs=pl.BlockSpec((TOKEN_TILE, n), lambda i: (i, 0)),
        out_shape=jax.ShapeDtypeStruct((t, n), F32),
        compiler_params=_cparams(1),
        name="inproj",
    )(x, w)


def _pack_halves(y):
    half = y.shape[1] // 2
    hi = lax.bitcast_convert_type(y[:, :half].astype(BF16).astype(F32), U32)
    lo = lax.bitcast_convert_type(y[:, half:].astype(BF16).astype(F32), U32)
    return (hi & jnp.uint32(0xFFFF0000)) | (lo >> 16)


def _unpack_halves(w):
    hi = lax.bitcast_convert_type(w & jnp.uint32(0xFFFF0000), F32)
    lo = lax.bitcast_convert_type(w << 16, F32)
    return hi, lo


def _outproj_body(a_ref, b_ref, x_ref, wa_ref, wb_ref, g_ref, be_ref, rw_ref, rb_ref,
                  x1_ref, xp_ref, gate_ref, idx_ref, rank_ref, cnt_ref, carry):
    @pl.when(pl.program_id(0) == 0)
    def _():
        carry[...] = jnp.zeros(carry.shape, F32)

    mix = (jnp.dot(a_ref[...], wa_ref[...], preferred_element_type=F32)
           + jnp.dot(b_ref[...], wb_ref[...], preferred_element_type=F32))
    x1 = _layernorm(DEEPNORM_ALPHA * x_ref[...] + mix, g_ref[...], be_ref[...])
    x1_ref[...] = x1
    xp_ref[...] = _pack_halves(x1)
    logits = _dot(x1, rw_ref[...]) + rb_ref[...]
    lane = lax.broadcasted_iota(I32, logits.shape, 1)
    vals, idxs = [], []
    for _ in range(TOP_K):
        m = jnp.max(logits, axis=-1, keepdims=True)
        sel = jnp.min(jnp.where(logits == m, lane, LANE), axis=-1, keepdims=True)
        vals.append(m)
        idxs.append(sel)
        logits = jnp.where(lane == sel, -jnp.inf, logits)
    exps = [jnp.exp(v - vals[0]) for v in vals]
    inv = 1.0 / functools.reduce(lambda p, q: p + q, exps)
    gates = jnp.zeros(logits.shape, F32)
    eidx = jnp.zeros(logits.shape, I32)
    for k in range(TOP_K):
        gates = jnp.where(lane == k, exps[k] * inv, gates)
        eidx = jnp.where(lane == k, idxs[k], eidx)
    gate_ref[...] = gates
    idx_ref[...] = eidx
    tm = logits.shape[0]
    chosen = jnp.zeros(logits.shape, F32)
    for k in range(TOP_K):
        chosen = chosen + (lane == idxs[k]).astype(F32)
    earlier = lax.broadcasted_iota(I32, (tm, tm), 0) > lax.broadcasted_iota(I32, (tm, tm), 1)
    before = carry[...] + jnp.dot(earlier.astype(BF16), chosen.astype(BF16), preferred_element_type=F32)
    ranks = jnp.zeros(logits.shape, F32)
    for k in range(TOP_K):
        rk = jnp.sum(jnp.where(lane == idxs[k], before, 0.0), axis=-1, keepdims=True)
        ranks = jnp.where(lane == k, rk, ranks)
    rank_ref[...] = ranks.astype(I32)
    carry[...] = carry[...] + jnp.sum(chosen, axis=0, keepdims=True)
    cnt_ref[...] = carry[...].astype(I32)


def _outproj_ln_router(a, b, x, wa, wb, g, be, rw, rb):
    t = x.shape[0]
    tm = TOKEN_TILE
    row = lambda i: (i, 0)
    fix = lambda i: (0, 0)
    return pl.pallas_call(
        _outproj_body,
        grid=(t // tm,),
        in_specs=[pl.BlockSpec((tm, 512), row), pl.BlockSpec((tm, 512), row), pl.BlockSpec((tm, D_MODEL), row),
                  pl.BlockSpec((512, D_MODEL), fix), pl.BlockSpec((512, D_MODEL), fix),
                  pl.BlockSpec((1, D_MODEL), fix), pl.BlockSpec((1, D_MODEL), fix),
                  pl.BlockSpec((D_MODEL, LANE), fix), pl.BlockSpec((1, LANE), fix)],
        out_specs=[pl.BlockSpec((tm, D_MODEL), row), pl.BlockSpec((tm, 512), row),
                   pl.BlockSpec((tm, LANE), row), pl.BlockSpec((tm, LANE), row), pl.BlockSpec((tm, LANE), row),
                   pl.BlockSpec((1, LANE), fix)],
        out_shape=[jax.ShapeDtypeStruct((t, D_MODEL), F32), jax.ShapeDtypeStruct((t, 512), U32),
                   jax.ShapeDtypeStruct((t, LANE), F32), jax.ShapeDtypeStruct((t, LANE), I32),
                   jax.ShapeDtypeStruct((t, LANE), I32), jax.ShapeDtypeStruct((1, LANE), I32)],
        scratch_shapes=[pltpu.VMEM((1, LANE), F32)],
        compiler_params=_cparams(1),
        name="outproj_ln_router",
    )(a, b, x, wa, wb, g, be, rw, rb)


def _sc_worker_rows(n):
    per_worker = n // (SC_CORES * SC_SUBCORES)
    worker = lax.axis_index("s") * SC_CORES + lax.axis_index("c")
    return worker * per_worker, per_worker // SC_ROWS


def _sc_mesh():
    return plsc.VectorSubcoreMesh(core_axis_name="c", subcore_axis_name="s")


def _sc_scratch(w, dtype):
    return ([pltpu.VMEM((SC_ROWS,), I32)] * 2 + [pltpu.VMEM((SC_ROWS, w), dtype)] * 2
            + [pltpu.SemaphoreType.DMA] * 4)


def _sc_chunk_pair(base, p):
    off_a = pl.multiple_of(base + 2 * p * SC_ROWS, 8)
    return off_a, pl.multiple_of(off_a + SC_ROWS, 8)


def _sc_scatter_rows(src, idx, n_out):
    n, (n_src, w) = idx.shape[0], src.shape
    assert n % (SC_CORES * SC_SUBCORES * SC_ROWS * 2) == 0 and n_src % SC_ROWS == 0

    @functools.partial(pl.kernel, mesh=_sc_mesh(), out_type=jax.ShapeDtypeStruct((n_out, w), src.dtype),
                       scratch_types=_sc_scratch(w, src.dtype))
    def scatter(src_hbm, idx_hbm, out_hbm, idx_a, idx_b, rows_a, rows_b, sem_ra, sem_rb, sem_wa, sem_wb):
        base, chunks = _sc_worker_rows(n)

        def src_rows(off):
            return src_hbm.at[pl.ds(pl.multiple_of(lax.rem(off, n_src), 8), SC_ROWS)]

        @pl.loop(0, chunks // 2)
        def _(p):
            off_a, off_b = _sc_chunk_pair(base, p)
            read_a = pltpu.async_copy(src_rows(off_a), rows_a, sem_ra)
            read_b = pltpu.async_copy(src_rows(off_b), rows_b, sem_rb)
            pltpu.sync_copy(idx_hbm.at[pl.ds(off_a, SC_ROWS)], idx_a)
            pltpu.sync_copy(idx_hbm.at[pl.ds(off_b, SC_ROWS)], idx_b)
            read_a.wait()
            write_a = pltpu.async_copy(rows_a, out_hbm.at[idx_a], sem_wa)
            read_b.wait()
            write_b = pltpu.async_copy(rows_b, out_hbm.at[idx_b], sem_wb)
            write_a.wait()
            write_b.wait()

    return scatter(src, idx)


def _sc_gather_rows(table, idx):
    n, w = idx.shape[0], table.shape[1]
    assert n % (SC_CORES * SC_SUBCORES * SC_ROWS * 2) == 0

    @functools.partial(pl.kernel, mesh=_sc_mesh(), out_type=jax.ShapeDtypeStruct((n, w), table.dtype),
                       scratch_types=_sc_scratch(w, table.dtype))
    def gather(table_hbm, idx_hbm, out_hbm, idx_a, idx_b, rows_a, rows_b, sem_ra, sem_rb, sem_wa, sem_wb):
        base, chunks = _sc_worker_rows(n)

        @pl.loop(0, chunks // 2)
        def _(p):
            off_a, off_b = _sc_chunk_pair(base, p)
            pltpu.sync_copy(idx_hbm.at[pl.ds(off_a, SC_ROWS)], idx_a)
            read_a = pltpu.async_copy(table_hbm.at[idx_a], rows_a, sem_ra)
            pltpu.sync_copy(idx_hbm.at[pl.ds(off_b, SC_ROWS)], idx_b)
            read_b = pltpu.async_copy(table_hbm.at[idx_b], rows_b, sem_rb)
            read_a.wait()
            write_a = pltpu.async_copy(rows_a, out_hbm.at[pl.ds(off_a, SC_ROWS)], sem_wa)
            read_b.wait()
            write_b = pltpu.async_copy(rows_b, out_hbm.at[pl.ds(off_b, SC_ROWS)], sem_wb)
            write_a.wait()
            write_b.wait()

    return gather(table, idx)


def _experts_body(b0_ref, nb_ref, last_ref, nt_ref, xs_hbm, wg_ref, bg_ref, wu_ref, bu_ref, wd_ref, bd_ref, o_hbm,
                  wg_s, wu_s, wd_s, xbuf, obuf, sem_in, sem_out):
    e = pl.program_id(0)
    first_blk, n_blk, last_valid, n_total = b0_ref[e], nb_ref[e], last_ref[e], nt_ref[0]
    half_rows = MOE_ROWS // 2

    def rows_of(g):
        return pl.ds(pl.multiple_of(g * MOE_ROWS, MOE_ROWS), MOE_ROWS)

    def fetch(g, slot):
        return pltpu.make_async_copy(xs_hbm.at[rows_of(g)], xbuf.at[slot], sem_in.at[slot])

    def put(g, slot):
        return pltpu.make_async_copy(obuf.at[slot], o_hbm.at[rows_of(g)], sem_out.at[slot])

    @pl.when((e == 0) & (n_total > 0))
    def _():
        fetch(0, 0).start()

    @pl.when(n_blk > 0)
    def _():
        wg_s[...] = wg_ref[...].astype(BF16)
        wu_s[...] = wu_ref[...].astype(BF16)
        wd_s[...] = wd_ref[...].astype(BF16)

    def compute(slot, rows):
        half = D_MODEL // 2
        x_hi, x_lo = _unpack_halves(xbuf[slot, 0:rows, :])
        x_hi = x_hi.astype(BF16)
        x_lo = x_lo.astype(BF16)
        g = (jnp.dot(x_hi, wg_s[:half, :], preferred_element_type=F32)
             + jnp.dot(x_lo, wg_s[half:, :], preferred_element_type=F32) + bg_ref[...])
        u = (jnp.dot(x_hi, wu_s[:half, :], preferred_element_type=F32)
             + jnp.dot(x_lo, wu_s[half:, :], preferred_element_type=F32) + bu_ref[...])
        g = jnp.minimum(g, SWIGLU_LIMIT)
        u = jnp.clip(u, -SWIGLU_LIMIT, SWIGLU_LIMIT)
        hmid = (u + 1.0) * (g * jax.nn.sigmoid(SWIGLU_ALPHA * g))
        out = jnp.dot(hmid.astype(BF16), wd_s[...], preferred_element_type=F32) + bd_ref[...]
        obuf[slot, 0:rows, :] = _pack_halves(out)

    def block(j, carry):
        g = first_blk + j
        slot = lax.rem(g, 2)
        fetch(g, slot).wait()

        @pl.when(g + 1 < n_total)
        def _():
            fetch(g + 1, 1 - slot).start()

        @pl.when(g >= 2)
        def _():
            put(g - 2, slot).wait()

        valid = jnp.where(j == n_blk - 1, last_valid, MOE_ROWS)

        @pl.when(valid > half_rows)
        def _():
            compute(slot, MOE_ROWS)

        @pl.when(valid <= half_rows)
        def _():
            compute(slot, half_rows)
            obuf[slot, half_rows:, :] = jnp.zeros((MOE_ROWS - half_rows, obuf.shape[2]), obuf.dtype)

        put(g, slot).start()
        return carry

    lax.fori_loop(0, n_blk, block, 0)

    @pl.when((e == N_EXPERTS - 1) & (n_total >= 2))
    def _():
        put(n_total - 2, lax.rem(n_total, 2)).wait()

    @pl.when((e == N_EXPERTS - 1) & (n_total >= 1))
    def _():
        put(n_total - 1, lax.rem(n_total - 1, 2)).wait()


def _experts(layer, first_blk, n_blk, last_valid, xs, wg, bg, wu, bu, wd, bd):
    n_rows, w = xs.shape
    wsel = lambda e, b0, nb, lv, nt: (layer, e, 0, 0)
    wspec = pl.BlockSpec((None, None, D_MODEL, D_MODEL), wsel)
    bspec = pl.BlockSpec((None, None, 1, D_MODEL), wsel)
    bias = lambda b: b.reshape(b.shape[0], b.shape[1], 1, b.shape[2])
    return pl.pallas_call(
        _experts_body,
        grid_spec=pltpu.PrefetchScalarGridSpec(
            num_scalar_prefetch=4,
            grid=(N_EXPERTS,),
            in_specs=[pl.BlockSpec(memory_space=pl.ANY), wspec, bspec, wspec, bspec, wspec, bspec],
            out_specs=pl.BlockSpec(memory_space=pl.ANY),
            scratch_shapes=[pltpu.VMEM((D_MODEL, D_MODEL), BF16)] * 3
            + [pltpu.VMEM((2, MOE_ROWS, w), U32)] * 2 + [pltpu.SemaphoreType.DMA((2,))] * 2,
        ),
        out_shape=jax.ShapeDtypeStruct((n_rows, w), U32),
        compiler_params=_cparams(1),
        name="experts",
    )(first_blk, n_blk, last_valid, jnp.sum(n_blk).reshape(1), xs, wg, bias(bg), wu, bias(bu), wd, bias(bd))


def _combine_body(n_first, o0_ref, o1_ref, o2_ref, o3_ref, gt_ref, x_ref, g_ref, b_ref, ya_ref, yb_ref=None):
    half = D_MODEL // 2
    gates = gt_ref[...]
    hi = jnp.zeros((x_ref.shape[0], half), F32)
    lo = jnp.zeros((x_ref.shape[0], half), F32)
    for k, o_ref in enumerate((o0_ref, o1_ref, o2_ref, o3_ref)):
        h, l = _unpack_halves(o_ref[...])
        gk = gates[:, k:k + 1]
        hi = hi + gk * h
        lo = lo + gk * l
    x = x_ref[...]
    y_hi = DEEPNORM_ALPHA * x[:, :half] + hi
    y_lo = DEEPNORM_ALPHA * x[:, half:] + lo
    mu = (jnp.sum(y_hi, axis=-1, keepdims=True) + jnp.sum(y_lo, axis=-1, keepdims=True)) * (1.0 / D_MODEL)
    d_hi = y_hi - mu
    d_lo = y_lo - mu
    var = (jnp.sum(d_hi * d_hi, axis=-1, keepdims=True) + jnp.sum(d_lo * d_lo, axis=-1, keepdims=True)) * (1.0 / D_MODEL)
    r = lax.rsqrt(var + LN_EPS)
    out_hi = d_hi * r * g_ref[:, :half] + b_ref[:, :half]
    out_lo = d_lo * r * g_ref[:, half:] + b_ref[:, half:]

    def write(y_ref):
        y_ref[:, :half] = out_hi
        y_ref[:, half:] = out_lo

    if yb_ref is None:
        write(ya_ref)
    else:
        pl.when(pl.program_id(0) < n_first)(lambda: write(ya_ref))
        pl.when(pl.program_id(0) >= n_first)(lambda: write(yb_ref))


def _combine_ln(o4, gates, x, g, b, t_first=None):
    t = x.shape[0]
    tm = TOKEN_TILE
    row = lambda i: (i, 0)
    fix = lambda i: (0, 0)
    choice = lambda k: pl.BlockSpec((tm, 512), lambda i: (k * (t // tm) + i, 0))
    if t_first is None:
        n_first = None
        out_specs = pl.BlockSpec((tm, D_MODEL), row)
        out_shape = jax.ShapeDtypeStruct((t, D_MODEL), F32)
    else:
        n_first = t_first // tm
        out_specs = [pl.BlockSpec((tm, D_MODEL), lambda i: (jnp.minimum(i, n_first - 1), 0)),
                     pl.BlockSpec((tm, D_MODEL), lambda i: (jnp.maximum(i - n_first, 0), 0))]
        out_shape = [jax.ShapeDtypeStruct((t_first, D_MODEL), F32), jax.ShapeDtypeStruct((t - t_first, D_MODEL), F32)]
    return pl.pallas_call(
        functools.partial(_combine_body, n_first),
        grid=(t // tm,),
        in_specs=[choice(0), choice(1), choice(2), choice(3), pl.BlockSpec((tm, LANE), row),
                  pl.BlockSpec((tm, D_MODEL), row), pl.BlockSpec((1, D_MODEL), fix), pl.BlockSpec((1, D_MODEL), fix)],
        out_specs=out_specs,
        out_shape=out_shape,
        compiler_params=_cparams(1),
        name="combine_ln",
    )(o4, o4, o4, o4, gates, x, g, b)


def _moe(layer, x1, xp, gates, eidx, rank, counts, ln_g, ln_b, wg, bg, wu, bu, wd, bd, t_first=None):
    t = x1.shape[0]
    bm = MOE_ROWS
    n_blocks = t * TOP_K // bm + N_EXPERTS
    n_rows = n_blocks * bm
    cnt = counts[0, :N_EXPERTS]
    padded = (cnt + bm - 1) // bm * bm
    pad_end = jnp.cumsum(padded)
    pad_start = pad_end - padded
    e = eidx[:, :TOP_K]
    start = jnp.sum(jnp.where(e[:, :, None] == jnp.arange(N_EXPERTS, dtype=I32), pad_start, 0), axis=-1)
    dest = (start + rank[:, :TOP_K]).T.reshape(-1)
    n_blk = padded // bm
    last_valid = cnt - (n_blk - 1) * bm
    xs = _sc_scatter_rows(xp, dest, n_rows)
    outs = _experts(layer, pad_start // bm, n_blk, last_valid, xs, wg, bg, wu, bu, wd, bd)
    o4 = _sc_gather_rows(outs, dest)
    return _combine_ln(o4, gates, x1, ln_g, ln_b, t_first)


def _split2(x):
    hi = x.astype(BF16)
    return hi, (x - hi.astype(F32)).astype(BF16)


def _split3(x):
    hi = x.astype(BF16)
    rem = x - hi.astype(F32)
    mid = rem.astype(BF16)
    return hi, mid, (rem - mid.astype(F32)).astype(BF16)


def _chunk_cumsum(g, C):
    rows = g.shape[0]
    r = lax.broadcasted_iota(I32, (rows, rows), 0)
    c = lax.broadcasted_iota(I32, (rows, rows), 1)
    tri = ((r >= c) & (r // C == c // C)).astype(BF16)
    hi, mid, lo = _split3(g)
    dot = lambda part: jnp.dot(tri, part, preferred_element_type=F32)
    return dot(hi) + dot(mid) + dot(lo)


def _gla_body(mode, n_seq, n_chunk, C, *refs):
    if mode == "gla":
        hq_ref, hk_ref, hv_ref, hg_ref, hlr_ref, wlr_ref, blr_ref, nw_ref, s0_ref, _, o_ref, so_ref, st, o_scr = refs
    else:
        hq_ref, hk_ref, hv_ref, hg_ref, lb_ref, nw_ref, s0_ref, _, o_ref, so_ref, st, o_scr = refs
    n_heads = 4
    tstep = pl.program_id(1)

    @pl.when(tstep == 0)
    def _():
        for s in range(n_seq):
            for h in range(n_heads):
                st[s, h] = s0_ref[s, h].T

    if mode == "gla":
        q = hq_ref[...] * (GLA_DK ** -0.5)
        k = hk_ref[...]
        z = _dot(hlr_ref[...], wlr_ref[...]) + blr_ref[...]
        g = _log_sigmoid(z) * (1.0 / GLA_TAU)
    else:
        q = _silu(hq_ref[...]) * (HGRN_DK ** -0.5)
        lb = lb_ref[...]
        f = lb + (1.0 - lb) * jax.nn.sigmoid(hk_ref[...])
        k = 1.0 - f
        g = jnp.log(f)
    v = hv_ref[...]
    causal = _tri(C)
    mid = max(C // 2 - 1, 0)
    b_all = _chunk_cumsum(g, C)
    for s in range(n_seq):
        states = [st[s, h] for h in range(n_heads)]
        for c in range(n_chunk):
            r0 = (s * n_chunk + c) * C
            rs = slice(r0, r0 + C)
            b, qc, kc = b_all[rs, :], q[rs, :], k[rs, :]
            b_last = b[C - 1:C, :]
            b_mid = b[mid:mid + 1, :]
            qe_hi, qe_lo = _split2(qc * jnp.exp(b - b_mid))
            ke_hi, ke_lo = _split2(kc * jnp.exp(b_mid - b))
            q_state = (qc * jnp.exp(b)).astype(BF16)
            k_state = (kc * jnp.exp(b_last - b)).astype(BF16)
            decay = jnp.exp(b_last)
            for h in range(n_heads):
                cs = slice(h * HEAD_W, (h + 1) * HEAD_W)
                lhs = jnp.concatenate([qe_hi[:, cs], qe_hi[:, cs], qe_lo[:, cs]], axis=1)
                rhs = jnp.concatenate([ke_hi[:, cs], ke_lo[:, cs], ke_hi[:, cs]], axis=1)
                scores = jnp.where(causal, _dot_nt(lhs, rhs), 0.0)
                vh = v[rs, cs].astype(BF16)
                o = _dot(scores, vh) + _dot_nt(q_state[:, cs], states[h])
                states[h] = states[h] * decay[:, cs] + _dot_tn(vh, k_state[:, cs])
                ms = jnp.mean(o * o, axis=-1, keepdims=True)
                o_scr[rs, cs] = o * lax.rsqrt(ms + RMS_EPS)
        for h in range(n_heads):
            st[s, h] = states[h]
    o_ref[...] = (o_scr[...] * nw_ref[...] * _silu(hg_ref[...])).astype(BF16)

    @pl.when(tstep == pl.num_programs(1) - 1)
    def _():
        for s in range(n_seq):
            for h in range(n_heads):
                so_ref[s, h] = st[s, h].T


def _seq_layout(n_batch, seq_len, row_off, sample):
    if sample:
        n_seq, n_chunk, C = SAMPLE_SEQS, 1, seq_len
        rows = n_seq * C
        grid = (n_batch // n_seq, 1)
        blk0 = row_off // rows
        rb = lambda i, t: blk0 + i
    else:
        n_seq, n_chunk, C = 1, PROMPT_TILE // SCAN_CHUNK, SCAN_CHUNK
        rows = PROMPT_TILE
        tiles = seq_len // rows
        grid = (n_batch, tiles)
        blk0 = row_off // rows
        rb = lambda i, t: blk0 + i * tiles + t
    return n_seq, n_chunk, C, rows, grid, rb


def _gla_call(mode, h, cols, extra, nw, s0, out_buf, n_batch, seq_len, row_off, sample):
    n_seq, n_chunk, C, rows, grid, rb = _seq_layout(n_batch, seq_len, row_off, sample)
    colspec = lambda c0, w: pl.BlockSpec((rows, w), lambda i, t: (rb(i, t), c0 // w))
    fix2 = lambda i, t: (0, 0)
    in_specs = [colspec(cols[0], 512), colspec(cols[1], 512), colspec(cols[2], 512), colspec(cols[3], 512)]
    args = [h, h, h, h]
    if mode == "gla":
        wlr, blr = extra
        in_specs += [colspec(cols[4], LANE), pl.BlockSpec((LANE, 512), fix2), pl.BlockSpec((1, 512), fix2)]
        args += [h, wlr, blr]
    else:
        in_specs += [pl.BlockSpec((1, 512), fix2)]
        args += [extra]
    st_spec = pl.BlockSpec((n_seq, 4, HEAD_W, HEAD_W), lambda i, t: (i, 0, 0, 0))
    in_specs += [pl.BlockSpec((1, 512), fix2), st_spec, pl.BlockSpec(memory_space=pl.ANY)]
    args += [nw, s0, out_buf]
    o_spec = pl.BlockSpec((rows, 512), lambda i, t: (rb(i, t), 0))
    return pl.pallas_call(
        functools.partial(_gla_body, mode, n_seq, n_chunk, C),
        grid=grid,
        in_specs=in_specs,
        out_specs=[o_spec, st_spec],
        out_shape=[jax.ShapeDtypeStruct(out_buf.shape, out_buf.dtype),
                   jax.ShapeDtypeStruct((n_batch, 4, HEAD_W, HEAD_W), F32)],
        scratch_shapes=[pltpu.VMEM((n_seq, 4, HEAD_W, HEAD_W), F32), pltpu.VMEM((rows, 512), F32)],
        input_output_aliases={len(args) - 1: 0},
        compiler_params=_cparams(2),
        name=mode + ("_sample" if sample else "_prompt"),
    )(*args)


def _round_bf16(x, on=True):
    return x.astype(BF16).astype(F32) if on else x


def _conf_body(n_seq, L, round_x, round_w, a_ref, gt_ref, hist_ref, w_ref, b_ref, g_ref, be_ref, _, o_ref, co_ref,
               buf, bufr, y_scr):
    tstep = pl.program_id(1)
    hist = CONF_WIDTH - 1
    pad = 32 - hist

    @pl.when(tstep == 0)
    def _():
        for s in range(n_seq):
            buf[s, pad:32, :] = hist_ref[s]
            bufr[s, pad:32, :] = _round_bf16(hist_ref[s], round_x)

    u = a_ref[...] * jax.nn.sigmoid(gt_ref[...])
    ur = _round_bf16(u, round_x)
    for s in range(n_seq):
        buf[s, 32:32 + L, :] = u[s * L:(s + 1) * L, :]
        bufr[s, 32:32 + L, :] = ur[s * L:(s + 1) * L, :]
    w = _round_bf16(w_ref[...], round_w)
    for s in range(n_seq):
        acc = jnp.zeros((L, CONF_DIM), F32)
        for j in range(CONF_WIDTH):
            acc = acc + bufr[s, pad + j:pad + j + L, :] * w[j:j + 1, :]
        y_scr[s * L:(s + 1) * L, :] = _silu(_layernorm(acc + b_ref[...], g_ref[...], be_ref[...]))
        tail = buf[s, L + pad:L + 32, :]
        buf[s, pad:32, :] = tail
        tailr = bufr[s, L + pad:L + 32, :]
        bufr[s, pad:32, :] = tailr
    o_ref[...] = y_scr[...].astype(o_ref.dtype)

    @pl.when(tstep == pl.num_programs(1) - 1)
    def _():
        for s in range(n_seq):
            co_ref[s] = buf[s, pad:32, :]


def _conf_call(h, col_a, col_g, cache, w, b, g, be, out_buf, n_batch, seq_len, row_off, sample):
    n_seq, n_chunk, C, rows, grid, rb = _seq_layout(n_batch, seq_len, row_off, sample)
    L = rows // n_seq
    hist = CONF_WIDTH - 1
    colspec = lambda c0: pl.BlockSpec((rows, 512), lambda i, t: (rb(i, t), c0 // 512))
    fix2 = lambda i, t: (0, 0)
    c_spec = pl.BlockSpec((n_seq, hist, CONF_DIM), lambda i, t: (i, 0, 0))
    return pl.pallas_call(
        functools.partial(_conf_body, n_seq, L, True, sample),
        grid=grid,
        in_specs=[colspec(col_a), colspec(col_g), c_spec,
                  pl.BlockSpec((CONF_WIDTH, CONF_DIM), fix2), pl.BlockSpec((1, CONF_DIM), fix2),
                  pl.BlockSpec((1, CONF_DIM), fix2), pl.BlockSpec((1, CONF_DIM), fix2),
                  pl.BlockSpec(memory_space=pl.ANY)],
        out_specs=[pl.BlockSpec((rows, 512), lambda i, t: (rb(i, t), 0)), c_spec],
        out_shape=[jax.ShapeDtypeStruct(out_buf.shape, out_buf.dtype),
                   jax.ShapeDtypeStruct((n_batch, hist, CONF_DIM), F32)],
        scratch_shapes=[pltpu.VMEM((n_seq, 32 + L, CONF_DIM), F32)] * 2 + [pltpu.VMEM((rows, CONF_DIM), F32)],
        input_output_aliases={7: 0},
        compiler_params=_cparams(2),
        name="conformer" + ("_sample" if sample else "_prompt"),
    )(h, h, cache, w, b, g, be, out_buf)


def _ssd_body(n_seq, n_chunk, C, round_x, round_w, hz_ref, hx_ref, hdt_ref, hist_ref, s0_ref, cw_ref, cb_ref, dtb_ref, alog_ref,
              dvec_ref, nw_ref, _, o_ref, co_ref, so_ref, st, buf, bufr, xbc, y_scr):
    tstep = pl.program_id(1)
    L = n_chunk * C
    hist = SSM_CONV - 1
    pad = 8 - hist
    n_pairs = SSM_HEADS // 2

    @pl.when(tstep == 0)
    def _():
        for s in range(n_seq):
            buf[s, pad:8, :] = hist_ref[s]
            bufr[s, pad:8, :] = _round_bf16(hist_ref[s], round_x)
            for m in range(n_pairs):
                st[s, m] = s0_ref[s, m]

    cw = _round_bf16(cw_ref[...], round_w)
    for s in range(n_seq):
        hx = hx_ref[s * L:(s + 1) * L, :]
        buf[s, 8:8 + L, :] = hx
        bufr[s, 8:8 + L, :] = _round_bf16(hx, round_x)
        acc = jnp.zeros((L, SSM_CONV_DIM), F32)
        for j in range(SSM_CONV):
            acc = acc + bufr[s, pad + j:pad + j + L, :] * cw[j:j + 1, :]
        xbc[s * L:(s + 1) * L, :] = _silu(acc + cb_ref[...])
        tail = buf[s, L + pad:L + 8, :]
        buf[s, pad:8, :] = tail
        tailr = bufr[s, L + pad:L + 8, :]
        bufr[s, pad:8, :] = tailr

    dt = _softplus(hdt_ref[...] + dtb_ref[...])
    la = dt * (-jnp.exp(alog_ref[...]))
    hrow = lax.broadcasted_iota(I32, (LANE, SSM_INNER), 0)
    hcol = lax.broadcasted_iota(I32, (LANE, SSM_INNER), 1) // SSM_HEADDIM
    expand = (hrow == hcol).astype(F32)
    dtx = jnp.dot(dt, expand, preferred_element_type=F32, precision=HIGHEST)
    causal = _tri(C)
    tri = causal.astype(BF16)
    lane = lax.broadcasted_iota(I32, (C, HEAD_W), 1)
    bcol_all = _chunk_cumsum(la, C)
    heads_per_group = SSM_HEADS // SSM_GROUPS
    for s in range(n_seq):
        states = [st[s, m] for m in range(n_pairs)]
        for c in range(n_chunk):
            r0 = (s * n_chunk + c) * C
            rs = slice(r0, r0 + C)
            bcol = bcol_all[rs, :]
            brow = functools.reduce(lambda p, q: p + q, [
                lax.dot_general(part, tri, (((0,), (1,)), ((), ())), preferred_element_type=F32)
                for part in _split3(la[rs, :])])
            xs_c = xbc[rs, 0:SSM_INNER]
            v_c = (xs_c * dtx[rs, :]).astype(BF16)
            gmats, bms, cms = [], [], []
            for grp in range(SSM_GROUPS):
                bm = xbc[rs, SSM_INNER + grp * SSM_STATE:SSM_INNER + (grp + 1) * SSM_STATE]
                cm = xbc[rs, SSM_INNER + (SSM_GROUPS + grp) * SSM_STATE:SSM_INNER + (SSM_GROUPS + grp + 1) * SSM_STATE]
                cm_hi, cm_lo = _split2(cm)
                bm_hi, bm_lo = _split2(bm)
                gmats.append(_dot_nt(jnp.concatenate([cm_hi, cm_hi, cm_lo], axis=1),
                                     jnp.concatenate([bm_hi, bm_lo, bm_hi], axis=1)))
                bms.append(bm)
                cms.append(cm)
            for m in range(n_pairs):
                grp = (2 * m) // heads_per_group
                bm, cm, gmat = bms[grp], cms[grp], gmats[grp]
                ps = slice(m * HEAD_W, (m + 1) * HEAD_W)
                vp = v_c[:, ps]
                s_t = states[m]
                o_halves, new_rows = [], []
                for hh in range(2):
                    hd = 2 * m + hh
                    bc = bcol[:, hd:hd + 1]
                    br = brow[hd:hd + 1, :]
                    dec = jnp.where(causal, jnp.exp(jnp.minimum(bc - br, 0.0)), 0.0)
                    b_last = bcol[C - 1:C, hd:hd + 1]
                    o_halves.append(_dot(gmat * dec, vp) + _dot_nt(cm * jnp.exp(bc), s_t))
                    kv = _dot_tn(vp, bm * jnp.exp(b_last - bc))
                    vs = slice(hh * SSM_HEADDIM, (hh + 1) * SSM_HEADDIM)
                    new_rows.append(s_t[vs, :] * jnp.exp(b_last) + kv[vs, :])
                states[m] = jnp.concatenate(new_rows, axis=0)
                o_pair = jnp.where(lane < SSM_HEADDIM, o_halves[0], o_halves[1])
                y_scr[rs, ps] = o_pair + dvec_ref[:, ps] * xs_c[:, ps]
        for m in range(n_pairs):
            st[s, m] = states[m]
    y = y_scr[...] * _silu(hz_ref[...])
    gw = SSM_INNER // SSM_GROUPS
    for grp in range(SSM_GROUPS):
        gs = slice(grp * gw, (grp + 1) * gw)
        yg = y[:, gs]
        ms = jnp.mean(yg * yg, axis=-1, keepdims=True)
        o_ref[:, gs] = (yg * lax.rsqrt(ms + RMS_EPS) * nw_ref[:, gs]).astype(BF16)

    @pl.when(tstep == pl.num_programs(1) - 1)
    def _():
        for s in range(n_seq):
            co_ref[s] = buf[s, pad:8, :]
            for m in range(n_pairs):
                so_ref[s, m] = st[s, m]


def _ssd_call(h, col_z, col_x, col_dt, cache, s0, cw, cb, dtb, alog, dvec, nw, out_buf, n_batch, seq_len, row_off,
              sample):
    n_seq, n_chunk, C, rows, grid, rb = _seq_layout(n_batch, seq_len, row_off, sample)
    L = rows // n_seq
    hist = SSM_CONV - 1
    n_pairs = SSM_HEADS // 2
    colspec = lambda c0, w: pl.BlockSpec((rows, w), lambda i, t: (rb(i, t), c0 // w))
    fix2 = lambda i, t: (0, 0)
    c_spec = pl.BlockSpec((n_seq, hist, SSM_CONV_DIM), lambda i, t: (i, 0, 0))
    st_spec = pl.BlockSpec((n_seq, n_pairs, HEAD_W, SSM_STATE), lambda i, t: (i, 0, 0, 0))
    return pl.pallas_call(
        functools.partial(_ssd_body, n_seq, n_chunk, C, sample, True),
        grid=grid,
        in_specs=[colspec(col_z, 512), colspec(col_x, SSM_CONV_DIM), colspec(col_dt, LANE), c_spec, st_spec,
                  pl.BlockSpec((SSM_CONV, SSM_CONV_DIM), fix2), pl.BlockSpec((1, SSM_CONV_DIM), fix2),
                  pl.BlockSpec((1, LANE), fix2), pl.BlockSpec((1, LANE), fix2),
                  pl.BlockSpec((1, SSM_INNER), fix2), pl.BlockSpec((1, SSM_INNER), fix2),
                  pl.BlockSpec(memory_space=pl.ANY)],
        out_specs=[pl.BlockSpec((rows, 512), lambda i, t: (rb(i, t), 0)), c_spec, st_spec],
        out_shape=[jax.ShapeDtypeStruct(out_buf.shape, out_buf.dtype),
                   jax.ShapeDtypeStruct((n_batch, hist, SSM_CONV_DIM), F32),
                   jax.ShapeDtypeStruct((n_batch, n_pairs, HEAD_W, SSM_STATE), F32)],
        scratch_shapes=[pltpu.VMEM((n_seq, n_pairs, HEAD_W, SSM_STATE), F32),
                        pltpu.VMEM((n_seq, 8 + L, SSM_CONV_DIM), F32),
                        pltpu.VMEM((n_seq, 8 + L, SSM_CONV_DIM), F32),
                        pltpu.VMEM((rows, SSM_CONV_DIM), F32),
                        pltpu.VMEM((rows, SSM_INNER), F32)],
        input_output_aliases={11: 0},
        compiler_params=_cparams(2),
        name="ssd" + ("_sample" if sample else "_prompt"),
    )(h, h, h, cache, s0, cw, cb, dtb, alog, dvec, nw, out_buf)


def _pad_heads(w, n_heads, width):
    lead = w.shape[:-1]
    w = w.reshape(lead + (n_heads, width))
    w = jnp.pad(w, [(0, 0)] * len(lead) + [(0, 0), (0, HEAD_W - width)])
    return w.reshape(lead + (n_heads * HEAD_W,))


def _row(v):
    return v.reshape(1, -1).astype(F32)


def kernel(x_prompt, x_sample, state_gla, cache_conformer, state_hgrn, state_ssm, cache_mamba_conv, w_in_even, w_gla_gate_lr, b_gla_gate, gla_norm_w, conf_conv_w, conf_conv_b, conf_ln_g, conf_ln_b, w_out_even, w_in_odd, hgrn_lower_bounds, hgrn_norm_w, mamba_conv_w, mamba_conv_b, mamba_dt_bias, mamba_a_log, mamba_d, mamba_norm_w, w_out_odd, ln1_g, ln1_b, ln2_g, ln2_b, router_w, router_b, expert_w_gate, expert_b_gate, expert_w_up, expert_b_up, expert_w_down, expert_b_down):
    bp, lp, _ = x_prompt.shape
    bs, ls, _ = x_sample.shape
    tp, ts = bp * lp, bs * ls
    x = jnp.concatenate([x_prompt.reshape(tp, D_MODEL), x_sample.reshape(ts, D_MODEL)], axis=0)

    def router_params(layer):
        rw = jnp.pad(router_w[layer], ((0, 0), (0, LANE - N_EXPERTS)))
        rb = jnp.pad(router_b[layer].astype(F32), (0, LANE - N_EXPERTS), constant_values=-1e30)
        return rw, rb.reshape(1, LANE)

    def finish_layer(layer, x, mix_a, mix_b, w_out, t_first=None):
        rw, rb = router_params(layer)
        x1, xp, gates, eidx, rank, counts = _outproj_ln_router(
            mix_a, mix_b, x, w_out[:512].astype(BF16), w_out[512:].astype(BF16),
            _row(ln1_g[layer]), _row(ln1_b[layer]), rw, rb)
        return _moe(layer, x1, xp, gates, eidx, rank, counts, _row(ln2_g[layer]), _row(ln2_b[layer]),
                    expert_w_gate, expert_b_gate, expert_w_up, expert_b_up, expert_w_down, expert_b_down, t_first)

    def mix_buffer():
        return jnp.zeros((tp + ts, 512), BF16)

    wi = w_in_even[0]
    wq, wk, wv, wg, wlr, wglu = jnp.split(wi, [256, 512, 1024, 1536, 1552], axis=1)
    w_even = jnp.concatenate([_pad_heads(wq, GLA_HEADS, GLA_DK), _pad_heads(wk, GLA_HEADS, GLA_DK), wv, wg, wglu,
                              jnp.pad(wlr, ((0, 0), (0, LANE - GLA_RANK)))], axis=1).astype(BF16)
    cols_gla = (0, 512, 1024, 1536, 3072)
    col_a, col_gate = 2048, 2560
    h = _inproj(x, w_even)
    wlr_p = jnp.pad(_pad_heads(w_gla_gate_lr[0], GLA_HEADS, GLA_DK), ((0, LANE - GLA_RANK), (0, 0)))
    blr_p = _row(_pad_heads(b_gla_gate[0], GLA_HEADS, GLA_DK))
    nw = _row(gla_norm_w[0])
    conf_args = (conf_conv_w[0], _row(conf_conv_b[0]), _row(conf_ln_g[0]), _row(conf_ln_b[0]))
    s0_p = jnp.zeros((bp, GLA_HEADS, HEAD_W, HEAD_W), F32)
    s0_s = jnp.pad(state_gla[0], ((0, 0), (0, 0), (0, HEAD_W - GLA_DK), (0, 0)))
    mix_a, sg_p = _gla_call("gla", h, cols_gla, (wlr_p, blr_p), nw, s0_p, mix_buffer(), bp, lp, 0, False)
    mix_a, sg_s = _gla_call("gla", h, cols_gla, (wlr_p, blr_p), nw, s0_s, mix_a, bs, ls, tp, True)
    mix_b, cc_p = _conf_call(h, col_a, col_gate, jnp.zeros((bp,) + cache_conformer.shape[2:], F32), *conf_args,
                             mix_buffer(), bp, lp, 0, False)
    mix_b, cc_s = _conf_call(h, col_a, col_gate, cache_conformer[0], *conf_args, mix_b, bs, ls, tp, True)
    x = finish_layer(0, x, mix_a, mix_b, w_out_even[0])
    gla_p, gla_s = sg_p[:, :, :GLA_DK, :][None], sg_s[:, :, :GLA_DK, :][None]
    conf_p, conf_s = cc_p[None], cc_s[None]

    lb_cum = jnp.cumsum(jax.nn.softmax(hgrn_lower_bounds.astype(F32), axis=0), axis=0)
    lower_bound = _row((lb_cum - lb_cum[0])[1])
    wo = w_in_odd[0]
    w_odd = jnp.concatenate([wo[:, 2560:3584], wo[:, :2560],
                             jnp.pad(wo[:, 3584:], ((0, 0), (0, LANE - SSM_HEADS)))], axis=1).astype(BF16)
    h = _inproj(x, w_odd)
    cols_hgrn = (1024, 1536, 2048, 2560)
    col_z, col_x, col_dt = 3072, 0, 3584
    nw = _row(hgrn_norm_w[0])
    mix_a, sh_p = _gla_call("hgrn", h, cols_hgrn, lower_bound, nw,
                            jnp.zeros((bp, HGRN_HEADS, HEAD_W, HEAD_W), F32), mix_buffer(), bp, lp, 0, False)
    mix_a, sh_s = _gla_call("hgrn", h, cols_hgrn, lower_bound, nw, state_hgrn[0], mix_a, bs, ls, tp, True)

    def pair_states(s):
        return jnp.swapaxes(s, 2, 3).reshape(s.shape[0], SSM_HEADS // 2, HEAD_W, SSM_STATE)

    def unpair_states(s):
        return jnp.swapaxes(s.reshape(s.shape[0], SSM_HEADS, SSM_HEADDIM, SSM_STATE), 2, 3)

    pad8 = lambda v: jnp.pad(v.astype(F32), (0, LANE - SSM_HEADS)).reshape(1, LANE)
    ssd_args = (mamba_conv_w[0], _row(mamba_conv_b[0]), pad8(mamba_dt_bias[0]), pad8(mamba_a_log[0]),
                _row(jnp.repeat(mamba_d[0], SSM_HEADDIM)), _row(mamba_norm_w[0]))
    mix_b, cm_p, ss_p = _ssd_call(h, col_z, col_x, col_dt, jnp.zeros((bp,) + cache_mamba_conv.shape[2:], F32),
                                  jnp.zeros((bp, SSM_HEADS // 2, HEAD_W, SSM_STATE), F32), *ssd_args,
                                  mix_buffer(), bp, lp, 0, False)
    mix_b, cm_s, ss_s = _ssd_call(h, col_z, col_x, col_dt, cache_mamba_conv[0], pair_states(state_ssm[0]),
                                  *ssd_args, mix_b, bs, ls, tp, True)
    y_prompt, y_sample = finish_layer(1, x, mix_a, mix_b, w_out_odd[0], t_first=tp)
    y_prompt = y_prompt.reshape(bp, lp, D_MODEL)
    y_sample = y_sample.reshape(bs, ls, D_MODEL)
    return (y_prompt, y_sample, gla_p, gla_s, conf_p, conf_s, sh_p[None], sh_s[None],
            unpair_states(ss_p)[None], unpair_states(ss_s)[None], cm_p[None], cm_s[None])
```

```python
import functools

import jax
import jax.numpy as jnp
from jax import lax
from jax.experimental import pallas as pl
from jax.experimental.pallas import tpu as pltpu
from jax.experimental.pallas import tpu_sc as plsc

F32 = jnp.float32
BF16 = jnp.bfloat16
I32 = jnp.int32
U32 = jnp.uint32
HIGHEST = lax.Precision.HIGHEST

D_MODEL = 1024
DEPTH = 2
DEEPNORM_ALPHA = (2.0 * DEPTH) ** 0.25
LN_EPS = 1e-5
RMS_EPS = 1e-6
LANE = 128
HEAD_W = 128
GLA_HEADS, GLA_DK, GLA_RANK, GLA_TAU = 4, 64, 16, 16.0
CONF_DIM, CONF_WIDTH = 512, 31
HGRN_HEADS, HGRN_DK = 4, 128
SSM_HEADS, SSM_HEADDIM, SSM_STATE, SSM_GROUPS, SSM_CONV = 8, 64, 128, 2, 4
SSM_INNER = SSM_HEADS * SSM_HEADDIM
SSM_CONV_DIM = SSM_INNER + 2 * SSM_GROUPS * SSM_STATE
N_EXPERTS, TOP_K = 32, 4
SWIGLU_ALPHA, SWIGLU_LIMIT = 1.702, 7.0
SCAN_CHUNK = 64
PROMPT_TILE = 256
SAMPLE_SEQS = 16
TOKEN_TILE = 512
MOE_ROWS = 512
SC_CORES, SC_SUBCORES = 2, 16
SC_ROWS = 64
SC_SCATTER_ROWS = 32
VMEM_LIMIT = 56 * 1024 * 1024


def _cparams(n_axes):
    return pltpu.CompilerParams(dimension_semantics=("arbitrary",) * n_axes, vmem_limit_bytes=VMEM_LIMIT)


def _silu(x):
    return x * jax.nn.sigmoid(x)


def _softplus(x):
    return jnp.maximum(x, 0.0) + jnp.log(1.0 + jnp.exp(-jnp.abs(x)))


def _log_sigmoid(x):
    return jnp.minimum(x, 0.0) - jnp.log(1.0 + jnp.exp(-jnp.abs(x)))


def _layernorm(y, g, b):
    mu = jnp.mean(y, axis=-1, keepdims=True)
    d = y - mu
    var = jnp.mean(d * d, axis=-1, keepdims=True)
    return d * lax.rsqrt(var + LN_EPS) * g + b


def _dot(a, b):
    return jnp.dot(a.astype(BF16), b.astype(BF16), preferred_element_type=F32)


def _dot_nt(a, b):
    return lax.dot_general(a.astype(BF16), b.astype(BF16), (((1,), (1,)), ((), ())), preferred_element_type=F32)


def _dot_tn(a, b):
    return lax.dot_general(a.astype(BF16), b.astype(BF16), (((0,), (0,)), ((), ())), preferred_element_type=F32)


def _tri(c):
    r = lax.broadcasted_iota(I32, (c, c), 0)
    k = lax.broadcasted_iota(I32, (c, c), 1)
    return r >= k


def _inproj_body(x_ref, w_ref, o_ref):
    xb = x_ref[...].astype(BF16)
    n = w_ref.shape[1]
    for c0 in range(0, n, 512):
        c1 = min(c0 + 512, n)
        o_ref[:, c0:c1] = jnp.dot(xb, w_ref[:, c0:c1], preferred_element_type=F32)


def _inproj(x, w):
    t, k = x.shape
    n = w.shape[1]
    return pl.pallas_call(
        _inproj_body,
        grid=(t // TOKEN_TILE,),
        in_specs=[pl.BlockSpec((TOKEN_TILE, k), lambda i: (i, 0)),
                  pl.BlockSpec((k, n), lambda i: (0, 0))],
        out_specs=pl.BlockSpec((TOKEN_TILE, n), lambda i: (i, 0)),
        out_shape=jax.ShapeDtypeStruct((t, n), F32),
        compiler_params=_cparams(1),
        name="inproj",
    )(x, w)


def _pack_halves(y):
    half = y.shape[1] // 2
    hi = lax.bitcast_convert_type(y[:, :half].astype(BF16).astype(F32), U32)
    lo = lax.bitcast_convert_type(y[:, half:].astype(BF16).astype(F32), U32)
    return (hi & jnp.uint32(0xFFFF0000)) | (lo >> 16)


def _unpack_halves(w):
    hi = lax.bitcast_convert_type(w & jnp.uint32(0xFFFF0000), F32)
    lo = lax.bitcast_convert_type(w << 16, F32)
    return hi, lo


def _outproj_body(a_ref, b_ref, x_ref, wa_ref, wb_ref, g_ref, be_ref, rw_ref, rb_ref,
                  x1_ref, xp_ref, gate_ref, idx_ref, rank_ref, cnt_ref, carry):
    @pl.when(pl.program_id(0) == 0)
    def _():
        carry[...] = jnp.zeros(carry.shape, F32)

    mix = (jnp.dot(a_ref[...], wa_ref[...], preferred_element_type=F32)
           + jnp.dot(b_ref[...], wb_ref[...], preferred_element_type=F32))
    x1 = _layernorm(DEEPNORM_ALPHA * x_ref[...] + mix, g_ref[...], be_ref[...])
    x1_ref[...] = x1
    xp_ref[...] = _pack_halves(x1)
    logits = _dot(x1, rw_ref[...]) + rb_ref[...]
    lane = lax.broadcasted_iota(I32, logits.shape, 1)
    vals, idxs = [], []
    for _ in range(TOP_K):
        m = jnp.max(logits, axis=-1, keepdims=True)
        sel = jnp.min(jnp.where(logits == m, lane, LANE), axis=-1, keepdims=True)
        vals.append(m)
        idxs.append(sel)
        logits = jnp.where(lane == sel, -jnp.inf, logits)
    exps = [jnp.exp(v - vals[0]) for v in vals]
    inv = 1.0 / functools.reduce(lambda p, q: p + q, exps)
    gates = jnp.zeros(logits.shape, F32)
    eidx = jnp.zeros(logits.shape, I32)
    for k in range(TOP_K):
        gates = jnp.where(lane == k, exps[k] * inv, gates)
        eidx = jnp.where(lane == k, idxs[k], eidx)
    gate_ref[...] = gates
    idx_ref[...] = eidx
    tm = logits.shape[0]
    chosen = jnp.zeros(logits.shape, F32)
    for k in range(TOP_K):
        chosen = chosen + (lane == idxs[k]).astype(F32)
    earlier = lax.broadcasted_iota(I32, (tm, tm), 0) > lax.broadcasted_iota(I32, (tm, tm), 1)
    before = carry[...] + jnp.dot(earlier.astype(BF16), chosen.astype(BF16), preferred_element_type=F32)
    ranks = jnp.zeros(logits.shape, F32)
    for k in range(TOP_K):
        rk = jnp.sum(jnp.where(lane == idxs[k], before, 0.0), axis=-1, keepdims=True)
        ranks = jnp.where(lane == k, rk, ranks)
    rank_ref[...] = ranks.astype(I32)
    carry[...] = carry[...] + jnp.sum(chosen, axis=0, keepdims=True)
    cnt_ref[...] = carry[...].astype(I32)


def _outproj_ln_router(a, b, x, wa, wb, g, be, rw, rb):
    t = x.shape[0]
    tm = TOKEN_TILE
    row = lambda i: (i, 0)
    fix = lambda i: (0, 0)
    return pl.pallas_call(
        _outproj_body,
        grid=(t // tm,),
        in_specs=[pl.BlockSpec((tm, 512), row), pl.BlockSpec((tm, 512), row), pl.BlockSpec((tm, D_MODEL), row),
                  pl.BlockSpec((512, D_MODEL), fix), pl.BlockSpec((512, D_MODEL), fix),
                  pl.BlockSpec((1, D_MODEL), fix), pl.BlockSpec((1, D_MODEL), fix),
                  pl.BlockSpec((D_MODEL, LANE), fix), pl.BlockSpec((1, LANE), fix)],
        out_specs=[pl.BlockSpec((tm, D_MODEL), row), pl.BlockSpec((tm, 512), row),
                   pl.BlockSpec((tm, LANE), row), pl.BlockSpec((tm, LANE), row), pl.BlockSpec((tm, LANE), row),
                   pl.BlockSpec((1, LANE), fix)],
        out_shape=[jax.ShapeDtypeStruct((t, D_MODEL), F32), jax.ShapeDtypeStruct((t, 512), U32),
                   jax.ShapeDtypeStruct((t, LANE), F32), jax.ShapeDtypeStruct((t, LANE), I32),
                   jax.ShapeDtypeStruct((t, LANE), I32), jax.ShapeDtypeStruct((1, LANE), I32)],
        scratch_shapes=[pltpu.VMEM((1, LANE), F32)],
        compiler_params=_cparams(1),
        name="outproj_ln_router",
    )(a, b, x, wa, wb, g, be, rw, rb)


def _sc_worker_rows(n):
    per_worker = n // (SC_CORES * SC_SUBCORES)
    worker = lax.axis_index("s") * SC_CORES + lax.axis_index("c")
    return worker * per_worker, per_worker // SC_ROWS


def _sc_mesh():
    return plsc.VectorSubcoreMesh(core_axis_name="c", subcore_axis_name="s")


def _sc_scratch(w, dtype):
    return ([pltpu.VMEM((SC_ROWS,), I32)] * 2 + [pltpu.VMEM((SC_ROWS, w), dtype)] * 2
            + [pltpu.SemaphoreType.DMA] * 4)


def _sc_chunk_pair(base, p):
    off_a = pl.multiple_of(base + 2 * p * SC_ROWS, 8)
    return off_a, pl.multiple_of(off_a + SC_ROWS, 8)


def _sc_scatter_rows(src, dest, n_out):
    n_src, w = src.shape
    n_dst = dest.shape[1]
    workers = SC_CORES * SC_SUBCORES
    per_worker = n_src // workers
    chunks = per_worker // SC_SCATTER_ROWS
    assert n_src == workers * chunks * SC_SCATTER_ROWS
    idx = dest.T.reshape(n_dst, workers, chunks, SC_SCATTER_ROWS)

    @functools.partial(pl.kernel, mesh=_sc_mesh(), out_type=jax.ShapeDtypeStruct((n_out, w), src.dtype),
                       scratch_types=[pltpu.VMEM((n_dst, chunks, SC_SCATTER_ROWS), I32)]
                       + [pltpu.VMEM((SC_SCATTER_ROWS, w), src.dtype)] * 2 + [pltpu.SemaphoreType.DMA] * 4)
    def scatter(src_hbm, idx_hbm, out_hbm, idx_v, rows_a, rows_b, sem_ra, sem_rb, sem_wa, sem_wb):
        worker = lax.axis_index("s") * SC_CORES + lax.axis_index("c")
        base = worker * per_worker
        for k in range(n_dst):
            pltpu.sync_copy(idx_hbm.at[k, worker], idx_v.at[k])

        def read(c, rows, sem):
            return pltpu.async_copy(src_hbm.at[pl.ds(pl.multiple_of(base + c * SC_SCATTER_ROWS, 8), SC_SCATTER_ROWS)],
                                    rows, sem)

        def write_all(pending_read, c, rows, sem):
            pending_read.wait()
            return [pltpu.async_copy(rows, out_hbm.at[idx_v.at[k, c]], sem) for k in range(n_dst)]

        @pl.loop(0, chunks // 2)
        def _(p):
            read_a = read(2 * p, rows_a, sem_ra)
            read_b = read(2 * p + 1, rows_b, sem_rb)
            writes = write_all(read_a, 2 * p, rows_a, sem_wa) + write_all(read_b, 2 * p + 1, rows_b, sem_wb)
            for wr in writes:
                wr.wait()

        if chunks % 2:
            for wr in write_all(read(chunks - 1, rows_a, sem_ra), chunks - 1, rows_a, sem_wa):
                wr.wait()

    return scatter(src, idx)


def _sc_gather_rows(table, idx):
    n, w = idx.shape[0], table.shape[1]
    assert n % (SC_CORES * SC_SUBCORES * SC_ROWS * 2) == 0

    @functools.partial(pl.kernel, mesh=_sc_mesh(), out_type=jax.ShapeDtypeStruct((n, w), table.dtype),
                       scratch_types=_sc_scratch(w, table.dtype))
    def gather(table_hbm, idx_hbm, out_hbm, idx_a, idx_b, rows_a, rows_b, sem_ra, sem_rb, sem_wa, sem_wb):
        base, chunks = _sc_worker_rows(n)

        @pl.loop(0, chunks // 2)
        def _(p):
            off_a, off_b = _sc_chunk_pair(base, p)
            pltpu.sync_copy(idx_hbm.at[pl.ds(off_a, SC_ROWS)], idx_a)
            read_a = pltpu.async_copy(table_hbm.at[idx_a], rows_a, sem_ra)
            pltpu.sync_copy(idx_hbm.at[pl.ds(off_b, SC_ROWS)], idx_b)
            read_b = pltpu.async_copy(table_hbm.at[idx_b], rows_b, sem_rb)
            read_a.wait()
            write_a = pltpu.async_copy(rows_a, out_hbm.at[pl.ds(off_a, SC_ROWS)], sem_wa)
            read_b.wait()
            write_b = pltpu.async_copy(rows_b, out_hbm.at[pl.ds(off_b, SC_ROWS)], sem_wb)
            write_a.wait()
            write_b.wait()

    return gather(table, idx)


def _experts_body(b0_ref, nb_ref, last_ref, nt_ref, xs_hbm, wg_ref, bg_ref, wu_ref, bu_ref, wd_ref, bd_ref, o_hbm,
                  wg_s, wu_s, wd_s, xbuf, obuf, sem_in, sem_out):
    e = pl.program_id(0)
    first_blk, n_blk, last_valid, n_total = b0_ref[e], nb_ref[e], last_ref[e], nt_ref[0]
    half_rows = MOE_ROWS // 2

    def rows_of(g):
        return pl.ds(pl.multiple_of(g * MOE_ROWS, MOE_ROWS), MOE_ROWS)

    def fetch(g, slot):
        return pltpu.make_async_copy(xs_hbm.at[rows_of(g)], xbuf.at[slot], sem_in.at[slot])

    def put(g, slot):
        return pltpu.make_async_copy(obuf.at[slot], o_hbm.at[rows_of(g)], sem_out.at[slot])

    @pl.when((e == 0) & (n_total > 0))
    def _():
        fetch(0, 0).start()

    @pl.when(n_blk > 0)
    def _():
        wg_s[...] = wg_ref[...].astype(BF16)
        wu_s[...] = wu_ref[...].astype(BF16)
        wd_s[...] = wd_ref[...].astype(BF16)

    def compute(slot, rows):
        half = D_MODEL // 2
        x_hi, x_lo = _unpack_halves(xbuf[slot, 0:rows, :])
        x_hi = x_hi.astype(BF16)
        x_lo = x_lo.astype(BF16)
        g = (jnp.dot(x_hi, wg_s[:half, :], preferred_element_type=F32)
             + jnp.dot(x_lo, wg_s[half:, :], preferred_element_type=F32) + bg_ref[...])
        u = (jnp.dot(x_hi, wu_s[:half, :], preferred_element_type=F32)
             + jnp.dot(x_lo, wu_s[half:, :], preferred_element_type=F32) + bu_ref[...])
        g = jnp.minimum(g, SWIGLU_LIMIT)
        u = jnp.clip(u, -SWIGLU_LIMIT, SWIGLU_LIMIT)
        hmid = (u + 1.0) * (g * jax.nn.sigmoid(SWIGLU_ALPHA * g))
        out = jnp.dot(hmid.astype(BF16), wd_s[...], preferred_element_type=F32) + bd_ref[...]
        obuf[slot, 0:rows, :] = _pack_halves(out)

    def block(j, carry):
        g = first_blk + j
        slot = lax.rem(g, 2)
        fetch(g, slot).wait()

        @pl.when(g + 1 < n_total)
        def _():
            fetch(g + 1, 1 - slot).start()

        @pl.when(g >= 2)
        def _():
            put(g - 2, slot).wait()

        valid = jnp.where(j == n_blk - 1, last_valid, MOE_ROWS)

        @pl.when(valid > half_rows)
        def _():
            compute(slot, MOE_ROWS)

        @pl.when(valid <= half_rows)
        def _():
            compute(slot, half_rows)
            obuf[slot, half_rows:, :] = jnp.zeros((MOE_ROWS - half_rows, obuf.shape[2]), obuf.dtype)

        put(g, slot).start()
        return carry

    lax.fori_loop(0, n_blk, block, 0)

    @pl.when((e == N_EXPERTS - 1) & (n_total >= 2))
    def _():
        put(n_total - 2, lax.rem(n_total, 2)).wait()

    @pl.when((e == N_EXPERTS - 1) & (n_total >= 1))
    def _():
        put(n_total - 1, lax.rem(n_total - 1, 2)).wait()


def _experts(layer, first_blk, n_blk, last_valid, xs, wg, bg, wu, bu, wd, bd):
    n_rows, w = xs.shape
    wsel = lambda e, b0, nb, lv, nt: (layer, e, 0, 0)
    wspec = pl.BlockSpec((None, None, D_MODEL, D_MODEL), wsel)
    bspec = pl.BlockSpec((None, None, 1, D_MODEL), wsel)
    bias = lambda b: b.reshape(b.shape[0], b.shape[1], 1, b.shape[2])
    return pl.pallas_call(
        _experts_body,
        grid_spec=pltpu.PrefetchScalarGridSpec(
            num_scalar_prefetch=4,
            grid=(N_EXPERTS,),
            in_specs=[pl.BlockSpec(memory_space=pl.ANY), wspec, bspec, wspec, bspec, wspec, bspec],
            out_specs=pl.BlockSpec(memory_space=pl.ANY),
            scratch_shapes=[pltpu.VMEM((D_MODEL, D_MODEL), BF16)] * 3
            + [pltpu.VMEM((2, MOE_ROWS, w), U32)] * 2 + [pltpu.SemaphoreType.DMA((2,))] * 2,
        ),
        out_shape=jax.ShapeDtypeStruct((n_rows, w), U32),
        compiler_params=_cparams(1),
        name="experts",
    )(first_blk, n_blk, last_valid, jnp.sum(n_blk).reshape(1), xs, wg, bias(bg), wu, bias(bu), wd, bias(bd))


def _combine_body(n_first, o0_ref, o1_ref, o2_ref, o3_ref, gt_ref, x_ref, g_ref, b_ref, ya_ref, yb_ref=None):
    half = D_MODEL // 2
    gates = gt_ref[...]
    hi = jnp.zeros((x_ref.shape[0], half), F32)
    lo = jnp.zeros((x_ref.shape[0], half), F32)
    for k, o_ref in enumerate((o0_ref, o1_ref, o2_ref, o3_ref)):
        h, l = _unpack_halves(o_ref[...])
        gk = gates[:, k:k + 1]
        hi = hi + gk * h
        lo = lo + gk * l
    x = x_ref[...]
    y_hi = DEEPNORM_ALPHA * x[:, :half] + hi
    y_lo = DEEPNORM_ALPHA * x[:, half:] + lo
    mu = (jnp.sum(y_hi, axis=-1, keepdims=True) + jnp.sum(y_lo, axis=-1, keepdims=True)) * (1.0 / D_MODEL)
    d_hi = y_hi - mu
    d_lo = y_lo - mu
    var = (jnp.sum(d_hi * d_hi, axis=-1, keepdims=True) + jnp.sum(d_lo * d_lo, axis=-1, keepdims=True)) * (1.0 / D_MODEL)
    r = lax.rsqrt(var + LN_EPS)
    out_hi = d_hi * r * g_ref[:, :half] + b_ref[:, :half]
    out_lo = d_lo * r * g_ref[:, half:] + b_ref[:, half:]

    def write(y_ref):
        y_ref[:, :half] = out_hi
        y_ref[:, half:] = out_lo

    if yb_ref is None:
        write(ya_ref)
    else:
        pl.when(pl.program_id(0) < n_first)(lambda: write(ya_ref))
        pl.when(pl.program_id(0) >= n_first)(lambda: write(yb_ref))


def _combine_ln(o4, gates, x, g, b, t_first=None):
    t = x.shape[0]
    tm = TOKEN_TILE
    row = lambda i: (i, 0)
    fix = lambda i: (0, 0)
    choice = lambda k: pl.BlockSpec((tm, 512), lambda i: (k * (t // tm) + i, 0))
    if t_first is None:
        n_first = None
        out_specs = pl.BlockSpec((tm, D_MODEL), row)
        out_shape = jax.ShapeDtypeStruct((t, D_MODEL), F32)
    else:
        n_first = t_first // tm
        out_specs = [pl.BlockSpec((tm, D_MODEL), lambda i: (jnp.minimum(i, n_first - 1), 0)),
                     pl.BlockSpec((tm, D_MODEL), lambda i: (jnp.maximum(i - n_first, 0), 0))]
        out_shape = [jax.ShapeDtypeStruct((t_first, D_MODEL), F32), jax.ShapeDtypeStruct((t - t_first, D_MODEL), F32)]
    return pl.pallas_call(
        functools.partial(_combine_body, n_first),
        grid=(t // tm,),
        in_specs=[choice(0), choice(1), choice(2), choice(3), pl.BlockSpec((tm, LANE), row),
                  pl.BlockSpec((tm, D_MODEL), row), pl.BlockSpec((1, D_MODEL), fix), pl.BlockSpec((1, D_MODEL), fix)],
        out_specs=out_specs,
        out_shape=out_shape,
        compiler_params=_cparams(1),
        name="combine_ln",
    )(o4, o4, o4, o4, gates, x, g, b)


def _moe(layer, x1, xp, gates, eidx, rank, counts, ln_g, ln_b, wg, bg, wu, bu, wd, bd, t_first=None):
    t = x1.shape[0]
    bm = MOE_ROWS
    n_blocks = t * TOP_K // bm + N_EXPERTS
    n_rows = n_blocks * bm
    cnt = counts[0, :N_EXPERTS]
    padded = (cnt + bm - 1) // bm * bm
    pad_end = jnp.cumsum(padded)
    pad_start = pad_end - padded
    e = eidx[:, :TOP_K]
    start = jnp.sum(jnp.where(e[:, :, None] == jnp.arange(N_EXPERTS, dtype=I32), pad_start, 0), axis=-1)
    dest = start + rank[:, :TOP_K]
    n_blk = padded // bm
    last_valid = cnt - (n_blk - 1) * bm
    xs = _sc_scatter_rows(xp, dest, n_rows)
    outs = _experts(layer, pad_start // bm, n_blk, last_valid, xs, wg, bg, wu, bu, wd, bd)
    o4 = _sc_gather_rows(outs, dest.T.reshape(-1))
    return _combine_ln(o4, gates, x1, ln_g, ln_b, t_first)


def _split2(x):
    hi = x.astype(BF16)
    return hi, (x - hi.astype(F32)).astype(BF16)


def _split3(x):
    hi = x.astype(BF16)
    rem = x - hi.astype(F32)
    mid = rem.astype(BF16)
    return hi, mid, (rem - mid.astype(F32)).astype(BF16)


def _chunk_cumsum(g, C):
    rows = g.shape[0]
    r = lax.broadcasted_iota(I32, (rows, rows), 0)
    c = lax.broadcasted_iota(I32, (rows, rows), 1)
    tri = ((r >= c) & (r // C == c // C)).astype(BF16)
    hi, mid, lo = _split3(g)
    dot = lambda part: jnp.dot(tri, part, preferred_element_type=F32)
    return dot(hi) + dot(mid) + dot(lo)


def _gla_body(mode, n_seq, n_chunk, C, *refs):
    if mode == "gla":
        hq_ref, hk_ref, hv_ref, hg_ref, hlr_ref, wlr_ref, blr_ref, nw_ref, s0_ref, _, o_ref, so_ref, st, o_scr = refs
    else:
        hq_ref, hk_ref, hv_ref, hg_ref, lb_ref, nw_ref, s0_ref, _, o_ref, so_ref, st, o_scr = refs
    n_heads = 4
    tstep = pl.program_id(1)

    @pl.when(tstep == 0)
    def _():
        for s in range(n_seq):
            for h in range(n_heads):
                st[s, h] = s0_ref[s, h].T

    if mode == "gla":
        q = hq_ref[...] * (GLA_DK ** -0.5)
        k = hk_ref[...]
        z = _dot(hlr_ref[...], wlr_ref[...]) + blr_ref[...]
        g = _log_sigmoid(z) * (1.0 / GLA_TAU)
    else:
        q = _silu(hq_ref[...]) * (HGRN_DK ** -0.5)
        lb = lb_ref[...]
        f = lb + (1.0 - lb) * jax.nn.sigmoid(hk_ref[...])
        k = 1.0 - f
        g = jnp.log(f)
    v = hv_ref[...]
    causal = _tri(C)
    mid = max(C // 2 - 1, 0)
    b_all = _chunk_cumsum(g, C)
    for s in range(n_seq):
        states = [st[s, h] for h in range(n_heads)]
        for c in range(n_chunk):
            r0 = (s * n_chunk + c) * C
            rs = slice(r0, r0 + C)
            b, qc, kc = b_all[rs, :], q[rs, :], k[rs, :]
            b_last = b[C - 1:C, :]
            b_mid = b[mid:mid + 1, :]
            qe_hi, qe_lo = _split2(qc * jnp.exp(b - b_mid))
            ke_hi, ke_lo = _split2(kc * jnp.exp(b_mid - b))
            q_state = (qc * jnp.exp(b)).astype(BF16)
            k_state = (kc * jnp.exp(b_last - b)).astype(BF16)
            decay = jnp.exp(b_last)
            for h in range(n_heads):
                cs = slice(h * HEAD_W, (h + 1) * HEAD_W)
                lhs = jnp.concatenate([qe_hi[:, cs], qe_hi[:, cs], qe_lo[:, cs]], axis=1)
                rhs = jnp.concatenate([ke_hi[:, cs], ke_lo[:, cs], ke_hi[:, cs]], axis=1)
                scores = jnp.where(causal, _dot_nt(lhs, rhs), 0.0)
                vh = v[rs, cs].astype(BF16)
                o = _dot(scores, vh) + _dot_nt(q_state[:, cs], states[h])
                states[h] = states[h] * decay[:, cs] + _dot_tn(vh, k_state[:, cs])
                ms = jnp.mean(o * o, axis=-1, keepdims=True)
                o_scr[rs, cs] = o * lax.rsqrt(ms + RMS_EPS)
        for h in range(n_heads):
            st[s, h] = states[h]
    o_ref[...] = (o_scr[...] * nw_ref[...] * _silu(hg_ref[...])).astype(BF16)

    @pl.when(tstep == pl.num_programs(1) - 1)
    def _():
        for s in range(n_seq):
            for h in range(n_heads):
                so_ref[s, h] = st[s, h].T


def _seq_layout(n_batch, seq_len, row_off, sample):
    if sample:
        n_seq, n_chunk, C = SAMPLE_SEQS, 1, seq_len
        rows = n_seq * C
        grid = (n_batch // n_seq, 1)
        blk0 = row_off // rows
        rb = lambda i, t: blk0 + i
    else:
        n_seq, n_chunk, C = 1, PROMPT_TILE // SCAN_CHUNK, SCAN_CHUNK
        rows = PROMPT_TILE
        tiles = seq_len // rows
        grid = (n_batch, tiles)
        blk0 = row_off // rows
        rb = lambda i, t: blk0 + i * tiles + t
    return n_seq, n_chunk, C, rows, grid, rb


def _gla_call(mode, h, cols, extra, nw, s0, out_buf, n_batch, seq_len, row_off, sample):
    n_seq, n_chunk, C, rows, grid, rb = _seq_layout(n_batch, seq_len, row_off, sample)
    colspec = lambda c0, w: pl.BlockSpec((rows, w), lambda i, t: (rb(i, t), c0 // w))
    fix2 = lambda i, t: (0, 0)
    in_specs = [colspec(cols[0], 512), colspec(cols[1], 512), colspec(cols[2], 512), colspec(cols[3], 512)]
    args = [h, h, h, h]
    if mode == "gla":
        wlr, blr = extra
        in_specs += [colspec(cols[4], LANE), pl.BlockSpec((LANE, 512), fix2), pl.BlockSpec((1, 512), fix2)]
        args += [h, wlr, blr]
    else:
        in_specs += [pl.BlockSpec((1, 512), fix2)]
        args += [extra]
    st_spec = pl.BlockSpec((n_seq, 4, HEAD_W, HEAD_W), lambda i, t: (i, 0, 0, 0))
    in_specs += [pl.BlockSpec((1, 512), fix2), st_spec, pl.BlockSpec(memory_space=pl.ANY)]
    args += [nw, s0, out_buf]
    o_spec = pl.BlockSpec((rows, 512), lambda i, t: (rb(i, t), 0))
    return pl.pallas_call(
        functools.partial(_gla_body, mode, n_seq, n_chunk, C),
        grid=grid,
        in_specs=in_specs,
        out_specs=[o_spec, st_spec],
        out_shape=[jax.ShapeDtypeStruct(out_buf.shape, out_buf.dtype),
                   jax.ShapeDtypeStruct((n_batch, 4, HEAD_W, HEAD_W), F32)],
        scratch_shapes=[pltpu.VMEM((n_seq, 4, HEAD_W, HEAD_W), F32), pltpu.VMEM((rows, 512), F32)],
        input_output_aliases={len(args) - 1: 0},
        compiler_params=_cparams(2),
        name=mode + ("_sample" if sample else "_prompt"),
    )(*args)


def _round_bf16(x, on=True):
    return x.astype(BF16).astype(F32) if on else x


def _conf_body(n_seq, L, round_x, round_w, a_ref, gt_ref, hist_ref, w_ref, b_ref, g_ref, be_ref, _, o_ref, co_ref,
               buf, bufr, y_scr, win):
    tstep = pl.program_id(1)
    hist = CONF_WIDTH - 1
    pad = 32 - hist

    @pl.when(tstep == 0)
    def _():
        for s in range(n_seq):
            buf[s, pad:32, :] = hist_ref[s]
            bufr[s, pad:32, :] = _round_bf16(hist_ref[s], round_x)

    u = a_ref[...] * jax.nn.sigmoid(gt_ref[...])
    ur = _round_bf16(u, round_x)
    for s in range(n_seq):
        buf[s, 32:32 + L, :] = u[s * L:(s + 1) * L, :]
        bufr[s, 32:32 + L, :] = ur[s * L:(s + 1) * L, :]
    w = _round_bf16(w_ref[...], round_w)
    for s in range(n_seq):
        acc = jnp.zeros((L, CONF_DIM), F32)
        for phase in range(8):
            n_taps = (CONF_WIDTH - 1 - phase) // 8 + 1
            span = L + 8 * (n_taps - 1)
            win[phase, 0:span, :] = bufr[s, pad + phase:pad + phase + span, :]
            for a in range(n_taps):
                j = 8 * a + phase
                acc = acc + win[phase, 8 * a:8 * a + L, :] * w[j:j + 1, :]
        y_scr[s * L:(s + 1) * L, :] = _silu(_layernorm(acc + b_ref[...], g_ref[...], be_ref[...]))
        tail = buf[s, L + pad:L + 32, :]
        buf[s, pad:32, :] = tail
        tailr = bufr[s, L + pad:L + 32, :]
        bufr[s, pad:32, :] = tailr
    o_ref[...] = y_scr[...].astype(o_ref.dtype)

    @pl.when(tstep == pl.num_programs(1) - 1)
    def _():
        for s in range(n_seq):
            co_ref[s] = buf[s, pad:32, :]


def _conf_call(h, col_a, col_g, cache, w, b, g, be, out_buf, n_batch, seq_len, row_off, sample):
    n_seq, n_chunk, C, rows, grid, rb = _seq_layout(n_batch, seq_len, row_off, sample)
    L = rows // n_seq
    hist = CONF_WIDTH - 1
    colspec = lambda c0: pl.BlockSpec((rows, 512), lambda i, t: (rb(i, t), c0 // 512))
    fix2 = lambda i, t: (0, 0)
    c_spec = pl.BlockSpec((n_seq, hist, CONF_DIM), lambda i, t: (i, 0, 0))
    return pl.pallas_call(
        functools.partial(_conf_body, n_seq, L, True, sample),
        grid=grid,
        in_specs=[colspec(col_a), colspec(col_g), c_spec,
                  pl.BlockSpec((CONF_WIDTH, CONF_DIM), fix2), pl.BlockSpec((1, CONF_DIM), fix2),
                  pl.BlockSpec((1, CONF_DIM), fix2), pl.BlockSpec((1, CONF_DIM), fix2),
                  pl.BlockSpec(memory_space=pl.ANY)],
        out_specs=[pl.BlockSpec((rows, 512), lambda i, t: (rb(i, t), 0)), c_spec],
        out_shape=[jax.ShapeDtypeStruct(out_buf.shape, out_buf.dtype),
                   jax.ShapeDtypeStruct((n_batch, hist, CONF_DIM), F32)],
        scratch_shapes=[pltpu.VMEM((n_seq, 32 + L, CONF_DIM), F32)] * 2 + [pltpu.VMEM((rows, CONF_DIM), F32),
                                                                           pltpu.VMEM((8, L + 24, CONF_DIM), F32)],
        input_output_aliases={7: 0},
        compiler_params=_cparams(2),
        name="conformer" + ("_sample" if sample else "_prompt"),
    )(h, h, cache, w, b, g, be, out_buf)


def _ssd_body(n_seq, n_chunk, C, round_x, round_w, hz_ref, hx_ref, hdt_ref, hist_ref, s0_ref, cw_ref, cb_ref, dtb_ref, alog_ref,
              dvec_ref, nw_ref, _, o_ref, co_ref, so_ref, st, buf, bufr, xbc, y_scr):
    tstep = pl.program_id(1)
    L = n_chunk * C
    hist = SSM_CONV - 1
    pad = 8 - hist
    n_pairs = SSM_HEADS // 2

    @pl.when(tstep == 0)
    def _():
        for s in range(n_seq):
            buf[s, pad:8, :] = hist_ref[s]
            bufr[s, pad:8, :] = _round_bf16(hist_ref[s], round_x)
            for m in range(n_pairs):
                st[s, m] = s0_ref[s, m]

    cw = _round_bf16(cw_ref[...], round_w)
    for s in range(n_seq):
        hx = hx_ref[s * L:(s + 1) * L, :]
        buf[s, 8:8 + L, :] = hx
        bufr[s, 8:8 + L, :] = _round_bf16(hx, round_x)
        acc = jnp.zeros((L, SSM_CONV_DIM), F32)
        for j in range(SSM_CONV):
            acc = acc + bufr[s, pad + j:pad + j + L, :] * cw[j:j + 1, :]
        xbc[s * L:(s + 1) * L, :] = _silu(acc + cb_ref[...])
        tail = buf[s, L + pad:L + 8, :]
        buf[s, pad:8, :] = tail
        tailr = bufr[s, L + pad:L + 8, :]
        bufr[s, pad:8, :] = tailr

    dt = _softplus(hdt_ref[...] + dtb_ref[...])
    la = dt * (-jnp.exp(alog_ref[...]))
    hrow = lax.broadcasted_iota(I32, (LANE, SSM_INNER), 0)
    hcol = lax.broadcasted_iota(I32, (LANE, SSM_INNER), 1) // SSM_HEADDIM
    expand = (hrow == hcol).astype(BF16)
    dtx = functools.reduce(lambda p, q: p + q,
                           [jnp.dot(part, expand, preferred_element_type=F32) for part in _split3(dt)])
    causal = _tri(C)
    tri = causal.astype(BF16)
    lane = lax.broadcasted_iota(I32, (C, HEAD_W), 1)
    bcol_all = _chunk_cumsum(la, C)
    heads_per_group = SSM_HEADS // SSM_GROUPS
    for s in range(n_seq):
        states = [st[s, m] for m in range(n_pairs)]
        for c in range(n_chunk):
            r0 = (s * n_chunk + c) * C
            rs = slice(r0, r0 + C)
            bcol = bcol_all[rs, :]
            brow = functools.reduce(lambda p, q: p + q, [
                lax.dot_general(part, tri, (((0,), (1,)), ((), ())), preferred_element_type=F32)
                for part in _split3(la[rs, :])])
            xs_c = xbc[rs, 0:SSM_INNER]
            v_c = (xs_c * dtx[rs, :]).astype(BF16)
            gmats, bms, cms = [], [], []
            for grp in range(SSM_GROUPS):
                bm = xbc[rs, SSM_INNER + grp * SSM_STATE:SSM_INNER + (grp + 1) * SSM_STATE]
                cm = xbc[rs, SSM_INNER + (SSM_GROUPS + grp) * SSM_STATE:SSM_INNER + (SSM_GROUPS + grp + 1) * SSM_STATE]
                cm_hi, cm_lo = _split2(cm)
                bm_hi, bm_lo = _split2(bm)
                gmats.append(_dot_nt(jnp.concatenate([cm_hi, cm_hi, cm_lo], axis=1),
                                     jnp.concatenate([bm_hi, bm_lo, bm_hi], axis=1)))
                bms.append(bm)
                cms.append(cm)
            for m in range(n_pairs):
                grp = (2 * m) // heads_per_group
                bm, cm, gmat = bms[grp], cms[grp], gmats[grp]
                ps = slice(m * HEAD_W, (m + 1) * HEAD_W)
                vp = v_c[:, ps]
                s_t = states[m]
                o_halves, new_rows = [], []
                for hh in range(2):
                    hd = 2 * m + hh
                    bc = bcol[:, hd:hd + 1]
                    br = brow[hd:hd + 1, :]
                    dec = jnp.where(causal, jnp.exp(jnp.minimum(bc - br, 0.0)), 0.0)
                    b_last = bcol[C - 1:C, hd:hd + 1]
                    o_halves.append(_dot(gmat * dec, vp) + _dot_nt(cm * jnp.exp(bc), s_t))
                    kv = _dot_tn(vp, bm * jnp.exp(b_last - bc))
                    vs = slice(hh * SSM_HEADDIM, (hh + 1) * SSM_HEADDIM)
                    new_rows.append(s_t[vs, :] * jnp.exp(b_last) + kv[vs, :])
                states[m] = jnp.concatenate(new_rows, axis=0)
                o_pair = jnp.where(lane < SSM_HEADDIM, o_halves[0], o_halves[1])
                y_scr[rs, ps] = o_pair + dvec_ref[:, ps] * xs_c[:, ps]
        for m in range(n_pairs):
            st[s, m] = states[m]
    y = y_scr[...] * _silu(hz_ref[...])
    gw = SSM_INNER // SSM_GROUPS
    for grp in range(SSM_GROUPS):
        gs = slice(grp * gw, (grp + 1) * gw)
        yg = y[:, gs]
        ms = jnp.mean(yg * yg, axis=-1, keepdims=True)
        o_ref[:, gs] = (yg * lax.rsqrt(ms + RMS_EPS) * nw_ref[:, gs]).astype(BF16)

    @pl.when(tstep == pl.num_programs(1) - 1)
    def _():
        for s in range(n_seq):
            co_ref[s] = buf[s, pad:8, :]
            for m in range(n_pairs):
                so_ref[s, m] = st[s, m]


def _ssd_call(h, col_z, col_x, col_dt, cache, s0, cw, cb, dtb, alog, dvec, nw, out_buf, n_batch, seq_len, row_off,
              sample):
    n_seq, n_chunk, C, rows, grid, rb = _seq_layout(n_batch, seq_len, row_off, sample)
    L = rows // n_seq
    hist = SSM_CONV - 1
    n_pairs = SSM_HEADS // 2
    colspec = lambda c0, w: pl.BlockSpec((rows, w), lambda i, t: (rb(i, t), c0 // w))
    fix2 = lambda i, t: (0, 0)
    c_spec = pl.BlockSpec((n_seq, hist, SSM_CONV_DIM), lambda i, t: (i, 0, 0))
    st_spec = pl.BlockSpec((n_seq, n_pairs, HEAD_W, SSM_STATE), lambda i, t: (i, 0, 0, 0))
    return pl.pallas_call(
        functools.partial(_ssd_body, n_seq, n_chunk, C, sample, True),
        grid=grid,
        in_specs=[colspec(col_z, 512), colspec(col_x, SSM_CONV_DIM), colspec(col_dt, LANE), c_spec, st_spec,
                  pl.BlockSpec((SSM_CONV, SSM_CONV_DIM), fix2), pl.BlockSpec((1, SSM_CONV_DIM), fix2),
                  pl.BlockSpec((1, LANE), fix2), pl.BlockSpec((1, LANE), fix2),
                  pl.BlockSpec((1, SSM_INNER), fix2), pl.BlockSpec((1, SSM_INNER), fix2),
                  pl.BlockSpec(memory_space=pl.ANY)],
        out_specs=[pl.BlockSpec((rows, 512), lambda i, t: (rb(i, t), 0)), c_spec, st_spec],
        out_shape=[jax.ShapeDtypeStruct(out_buf.shape, out_buf.dtype),
                   jax.ShapeDtypeStruct((n_batch, hist, SSM_CONV_DIM), F32),
                   jax.ShapeDtypeStruct((n_batch, n_pairs, HEAD_W, SSM_STATE), F32)],
        scratch_shapes=[pltpu.VMEM((n_seq, n_pairs, HEAD_W, SSM_STATE), F32),
                        pltpu.VMEM((n_seq, 8 + L, SSM_CONV_DIM), F32),
                        pltpu.VMEM((n_seq, 8 + L, SSM_CONV_DIM), F32),
                        pltpu.VMEM((rows, SSM_CONV_DIM), F32),
                        pltpu.VMEM((rows, SSM_INNER), F32)],
        input_output_aliases={11: 0},
        compiler_params=_cparams(2),
        name="ssd" + ("_sample" if sample else "_prompt"),
    )(h, h, h, cache, s0, cw, cb, dtb, alog, dvec, nw, out_buf)


def _pad_heads(w, n_heads, width):
    lead = w.shape[:-1]
    w = w.reshape(lead + (n_heads, width))
    w = jnp.pad(w, [(0, 0)] * len(lead) + [(0, 0), (0, HEAD_W - width)])
    return w.reshape(lead + (n_heads * HEAD_W,))


def _row(v):
    return v.reshape(1, -1).astype(F32)


def kernel(x_prompt, x_sample, state_gla, cache_conformer, state_hgrn, state_ssm, cache_mamba_conv, w_in_even, w_gla_gate_lr, b_gla_gate, gla_norm_w, conf_conv_w, conf_conv_b, conf_ln_g, conf_ln_b, w_out_even, w_in_odd, hgrn_lower_bounds, hgrn_norm_w, mamba_conv_w, mamba_conv_b, mamba_dt_bias, mamba_a_log, mamba_d, mamba_norm_w, w_out_odd, ln1_g, ln1_b, ln2_g, ln2_b, router_w, router_b, expert_w_gate, expert_b_gate, expert_w_up, expert_b_up, expert_w_down, expert_b_down):
    bp, lp, _ = x_prompt.shape
    bs, ls, _ = x_sample.shape
    tp, ts = bp * lp, bs * ls
    x = jnp.concatenate([x_prompt.reshape(tp, D_MODEL), x_sample.reshape(ts, D_MODEL)], axis=0)

    def router_params(layer):
        rw = jnp.pad(router_w[layer], ((0, 0), (0, LANE - N_EXPERTS)))
        rb = jnp.pad(router_b[layer].astype(F32), (0, LANE - N_EXPERTS), constant_values=-1e30)
        return rw, rb.reshape(1, LANE)

    def finish_layer(layer, x, mix_a, mix_b, w_out, t_first=None):
        rw, rb = router_params(layer)
        x1, xp, gates, eidx, rank, counts = _outproj_ln_router(
            mix_a, mix_b, x, w_out[:512].astype(BF16), w_out[512:].astype(BF16),
            _row(ln1_g[layer]), _row(ln1_b[layer]), rw, rb)
        return _moe(layer, x1, xp, gates, eidx, rank, counts, _row(ln2_g[layer]), _row(ln2_b[layer]),
                    expert_w_gate, expert_b_gate, expert_w_up, expert_b_up, expert_w_down, expert_b_down, t_first)

    def mix_buffer():
        return jnp.zeros((tp + ts, 512), BF16)

    wi = w_in_even[0]
    wq, wk, wv, wg, wlr, wglu = jnp.split(wi, [256, 512, 1024, 1536, 1552], axis=1)
    w_even = jnp.concatenate([_pad_heads(wq, GLA_HEADS, GLA_DK), _pad_heads(wk, GLA_HEADS, GLA_DK), wv, wg, wglu,
                              jnp.pad(wlr, ((0, 0), (0, LANE - GLA_RANK)))], axis=1).astype(BF16)
    cols_gla = (0, 512, 1024, 1536, 3072)
    col_a, col_gate = 2048, 2560
    h = _inproj(x, w_even)
    wlr_p = jnp.pad(_pad_heads(w_gla_gate_lr[0], GLA_HEADS, GLA_DK), ((0, LANE - GLA_RANK), (0, 0)))
    blr_p = _row(_pad_heads(b_gla_gate[0], GLA_HEADS, GLA_DK))
    nw = _row(gla_norm_w[0])
    conf_args = (conf_conv_w[0], _row(conf_conv_b[0]), _row(conf_ln_g[0]), _row(conf_ln_b[0]))
    s0_p = jnp.zeros((bp, GLA_HEADS, HEAD_W, HEAD_W), F32)
    s0_s = jnp.pad(state_gla[0], ((0, 0), (0, 0), (0, HEAD_W - GLA_DK), (0, 0)))
    mix_a, sg_p = _gla_call("gla", h, cols_gla, (wlr_p, blr_p), nw, s0_p, mix_buffer(), bp, lp, 0, False)
    mix_a, sg_s = _gla_call("gla", h, cols_gla, (wlr_p, blr_p), nw, s0_s, mix_a, bs, ls, tp, True)
    mix_b, cc_p = _conf_call(h, col_a, col_gate, jnp.zeros((bp,) + cache_conformer.shape[2:], F32), *conf_args,
                             mix_buffer(), bp, lp, 0, False)
    mix_b, cc_s = _conf_call(h, col_a, col_gate, cache_conformer[0], *conf_args, mix_b, bs, ls, tp, True)
    x = finish_layer(0, x, mix_a, mix_b, w_out_even[0])
    gla_p, gla_s = sg_p[:, :, :GLA_DK, :][None], sg_s[:, :, :GLA_DK, :][None]
    conf_p, conf_s = cc_p[None], cc_s[None]

    lb_cum = jnp.cumsum(jax.nn.softmax(hgrn_lower_bounds.astype(F32), axis=0), axis=0)
    lower_bound = _row((lb_cum - lb_cum[0])[1])
    wo = w_in_odd[0]
    w_odd = jnp.concatenate([wo[:, 2560:3584], wo[:, :2560],
                             jnp.pad(wo[:, 3584:], ((0, 0), (0, LANE - SSM_HEADS)))], axis=1).astype(BF16)
    h = _inproj(x, w_odd)
    cols_hgrn = (1024, 1536, 2048, 2560)
    col_z, col_x, col_dt = 3072, 0, 3584
    nw = _row(hgrn_norm_w[0])
    mix_a, sh_p = _gla_call("hgrn", h, cols_hgrn, lower_bound, nw,
                            jnp.zeros((bp, HGRN_HEADS, HEAD_W, HEAD_W), F32), mix_buffer(), bp, lp, 0, False)
    mix_a, sh_s = _gla_call("hgrn", h, cols_hgrn, lower_bound, nw, state_hgrn[0], mix_a, bs, ls, tp, True)

    def pair_states(s):
        return jnp.swapaxes(s, 2, 3).reshape(s.shape[0], SSM_HEADS // 2, HEAD_W, SSM_STATE)

    def unpair_states(s):
        return jnp.swapaxes(s.reshape(s.shape[0], SSM_HEADS, SSM_HEADDIM, SSM_STATE), 2, 3)

    pad8 = lambda v: jnp.pad(v.astype(F32), (0, LANE - SSM_HEADS)).reshape(1, LANE)
    ssd_args = (mamba_conv_w[0], _row(mamba_conv_b[0]), pad8(mamba_dt_bias[0]), pad8(mamba_a_log[0]),
                _row(jnp.repeat(mamba_d[0], SSM_HEADDIM)), _row(mamba_norm_w[0]))
    mix_b, cm_p, ss_p = _ssd_call(h, col_z, col_x, col_dt, jnp.zeros((bp,) + cache_mamba_conv.shape[2:], F32),
                                  jnp.zeros((bp, SSM_HEADS // 2, HEAD_W, SSM_STATE), F32), *ssd_args,
                                  mix_buffer(), bp, lp, 0, False)
    mix_b, cm_s, ss_s = _ssd_call(h, col_z, col_x, col_dt, cache_mamba_conv[0], pair_states(state_ssm[0]),
                                  *ssd_args, mix_b, bs, ls, tp, True)
    y_prompt, y_sample = finish_layer(1, x, mix_a, mix_b, w_out_odd[0], t_first=tp)
    y_prompt = y_prompt.reshape(bp, lp, D_MODEL)
    y_sample = y_sample.reshape(bs, ls, D_MODEL)
    return (y_prompt, y_sample, gla_p, gla_s, conf_p, conf_s, sh_p[None], sh_s[None],
            unpair_states(ss_p)[None], unpair_states(ss_s)[None], cm_p[None], cm_s[None])
```

```python
import functools

import jax
import jax.numpy as jnp
from jax import lax
from jax.experimental import pallas as pl
from jax.experimental.pallas import tpu as pltpu
from jax.experimental.pallas import tpu_sc as plsc

F32 = jnp.float32
BF16 = jnp.bfloat16
I32 = jnp.int32
U32 = jnp.uint32
HIGHEST = lax.Precision.HIGHEST

D_MODEL = 1024
DEPTH = 2
DEEPNORM_ALPHA = (2.0 * DEPTH) ** 0.25
LN_EPS = 1e-5
RMS_EPS = 1e-6
LANE = 128
HEAD_W = 128
GLA_HEADS, GLA_DK, GLA_RANK, GLA_TAU = 4, 64, 16, 16.0
CONF_DIM, CONF_WIDTH = 512, 31
HGRN_HEADS, HGRN_DK = 4, 128
SSM_HEADS, SSM_HEADDIM, SSM_STATE, SSM_GROUPS, SSM_CONV = 8, 64, 128, 2, 4
SSM_INNER = SSM_HEADS * SSM_HEADDIM
SSM_CONV_DIM = SSM_INNER + 2 * SSM_GROUPS * SSM_STATE
N_EXPERTS, TOP_K = 32, 4
SWIGLU_ALPHA, SWIGLU_LIMIT = 1.702, 7.0
SCAN_CHUNK = 64
PROMPT_TILE = 256
SAMPLE_SEQS = 16
TOKEN_TILE = 512
MOE_ROWS = 512
SC_CORES, SC_SUBCORES = 2, 16
SC_ROWS = 64
SC_SCATTER_ROWS = 32
VMEM_LIMIT = 56 * 1024 * 1024


def _cparams(n_axes):
    return pltpu.CompilerParams(dimension_semantics=("arbitrary",) * n_axes, vmem_limit_bytes=VMEM_LIMIT)


def _silu(x):
    return x * jax.nn.sigmoid(x)


def _softplus(x):
    return jnp.maximum(x, 0.0) + jnp.log(1.0 + jnp.exp(-jnp.abs(x)))


def _log_sigmoid(x):
    return jnp.minimum(x, 0.0) - jnp.log(1.0 + jnp.exp(-jnp.abs(x)))


def _layernorm(y, g, b):
    mu = jnp.mean(y, axis=-1, keepdims=True)
    d = y - mu
    var = jnp.mean(d * d, axis=-1, keepdims=True)
    return d * lax.rsqrt(var + LN_EPS) * g + b


def _dot(a, b):
    return jnp.dot(a.astype(BF16), b.astype(BF16), preferred_element_type=F32)


def _dot_nt(a, b):
    return lax.dot_general(a.astype(BF16), b.astype(BF16), (((1,), (1,)), ((), ())), preferred_element_type=F32)


def _dot_tn(a, b):
    return lax.dot_general(a.astype(BF16), b.astype(BF16), (((0,), (0,)), ((), ())), preferred_element_type=F32)


def _tri(c):
    r = lax.broadcasted_iota(I32, (c, c), 0)
    k = lax.broadcasted_iota(I32, (c, c), 1)
    return r >= k


def _pair_specs(tm, n_first, width):
    return [pl.BlockSpec((tm, width), lambda i: (jnp.minimum(i, n_first - 1), 0)),
            pl.BlockSpec((tm, width), lambda i: (jnp.maximum(i - n_first, 0), 0))]


def _pair_tile(n_first, xa_ref, xb_ref):
    return jnp.where(pl.program_id(0) < n_first, xa_ref[...], xb_ref[...])


def _inproj_body(n_first, xa_ref, xb_ref, w_ref, o_ref):
    xb = _pair_tile(n_first, xa_ref, xb_ref).astype(BF16)
    n = w_ref.shape[1]
    for c0 in range(0, n, 512):
        c1 = min(c0 + 512, n)
        o_ref[:, c0:c1] = jnp.dot(xb, w_ref[:, c0:c1], preferred_element_type=F32)


def _inproj(xa, xb, w):
    k, n = w.shape
    t = xa.shape[0] + xb.shape[0]
    n_first = xa.shape[0] // TOKEN_TILE
    return pl.pallas_call(
        functools.partial(_inproj_body, n_first),
        grid=(t // TOKEN_TILE,),
        in_specs=_pair_specs(TOKEN_TILE, n_first, k) + [pl.BlockSpec((k, n), lambda i: (0, 0))],
        out_specs=pl.BlockSpec((TOKEN_TILE, n), lambda i: (i, 0)),
        out_shape=jax.ShapeDtypeStruct((t, n), F32),
        compiler_params=_cparams(1),
        name="inproj",
    )(xa, xb, w)


def _pack_halves(y):
    half = y.shape[1] // 2
    hi = lax.bitcast_convert_type(y[:, :half].astype(BF16).astype(F32), U32)
    lo = lax.bitcast_convert_type(y[:, half:].astype(BF16).astype(F32), U32)
    return (hi & jnp.uint32(0xFFFF0000)) | (lo >> 16)


def _unpack_halves(w):
    hi = lax.bitcast_convert_type(w & jnp.uint32(0xFFFF0000), F32)
    lo = lax.bitcast_convert_type(w << 16, F32)
    return hi, lo


def _outproj_body(n_first, a_ref, b_ref, xa_ref, xb_ref, wa_ref, wb_ref, g_ref, be_ref, rwt_ref, rb_ref,
                  x1_ref, xp_ref, gate_ref, idx_ref, rank_ref, cnt_ref, carry):
    @pl.when(pl.program_id(0) == 0)
    def _():
        carry[...] = jnp.zeros(carry.shape, F32)

    mix = (jnp.dot(a_ref[...], wa_ref[...], preferred_element_type=F32)
           + jnp.dot(b_ref[...], wb_ref[...], preferred_element_type=F32))
    x1 = _layernorm(DEEPNORM_ALPHA * _pair_tile(n_first, xa_ref, xb_ref) + mix, g_ref[...], be_ref[...])
    x1_ref[...] = x1
    xp_ref[...] = _pack_halves(x1)
    logits = _dot_nt(rwt_ref[...], x1) + rb_ref[...]
    tm = logits.shape[1]
    expert = lax.broadcasted_iota(I32, logits.shape, 0)
    vals, idxs = [], []
    for _ in range(TOP_K):
        m = jnp.max(logits, axis=0, keepdims=True)
        sel = jnp.min(jnp.where(logits == m, expert, N_EXPERTS), axis=0, keepdims=True)
        vals.append(m)
        idxs.append(sel)
        logits = jnp.where(expert == sel, -jnp.inf, logits)
    exps = [jnp.exp(v - vals[0]) for v in vals]
    inv = 1.0 / functools.reduce(lambda p, q: p + q, exps)
    chosen = jnp.zeros(logits.shape, F32)
    for k in range(TOP_K):
        chosen = chosen + (expert == idxs[k]).astype(F32)
    earlier = lax.broadcasted_iota(I32, (tm, tm), 0) < lax.broadcasted_iota(I32, (tm, tm), 1)
    before = carry[...] + jnp.dot(chosen.astype(BF16), earlier.astype(BF16), preferred_element_type=F32)
    choice = lax.broadcasted_iota(I32, (8, tm), 0)
    gates = jnp.zeros((8, tm), F32)
    eidx = jnp.zeros((8, tm), I32)
    ranks = jnp.zeros((8, tm), F32)
    for k in range(TOP_K):
        rk = jnp.sum(jnp.where(expert == idxs[k], before, 0.0), axis=0, keepdims=True)
        gates = jnp.where(choice == k, exps[k] * inv, gates)
        eidx = jnp.where(choice == k, idxs[k], eidx)
        ranks = jnp.where(choice == k, rk, ranks)
    gate_ref[...] = gates
    idx_ref[...] = eidx
    rank_ref[...] = ranks.astype(I32)
    carry[...] = carry[...] + jnp.sum(chosen, axis=1, keepdims=True)
    cnt_ref[...] = carry[...].astype(I32)


def _outproj_ln_router(a, b, xa, xb, wa, wb, g, be, rwt, rb):
    t = xa.shape[0] + xb.shape[0]
    tm = TOKEN_TILE
    n_first = xa.shape[0] // tm
    row = lambda i: (i, 0)
    col = lambda i: (0, i)
    fix = lambda i: (0, 0)
    return pl.pallas_call(
        functools.partial(_outproj_body, n_first),
        grid=(t // tm,),
        in_specs=[pl.BlockSpec((tm, 512), row), pl.BlockSpec((tm, 512), row)] + _pair_specs(tm, n_first, D_MODEL)
        + [pl.BlockSpec((512, D_MODEL), fix), pl.BlockSpec((512, D_MODEL), fix),
           pl.BlockSpec((1, D_MODEL), fix), pl.BlockSpec((1, D_MODEL), fix),
           pl.BlockSpec((N_EXPERTS, D_MODEL), fix), pl.BlockSpec((N_EXPERTS, 1), fix)],
        out_specs=[pl.BlockSpec((tm, D_MODEL), row), pl.BlockSpec((tm, 512), row),
                   pl.BlockSpec((8, tm), col), pl.BlockSpec((8, tm), col), pl.BlockSpec((8, tm), col),
                   pl.BlockSpec((N_EXPERTS, 1), fix)],
        out_shape=[jax.ShapeDtypeStruct((t, D_MODEL), F32), jax.ShapeDtypeStruct((t, 512), U32),
                   jax.ShapeDtypeStruct((8, t), F32), jax.ShapeDtypeStruct((8, t), I32),
                   jax.ShapeDtypeStruct((8, t), I32), jax.ShapeDtypeStruct((N_EXPERTS, 1), I32)],
        scratch_shapes=[pltpu.VMEM((N_EXPERTS, 1), F32)],
        compiler_params=_cparams(1),
        name="outproj_ln_router",
    )(a, b, xa, xb, wa, wb, g, be, rwt, rb)


def _sc_worker_rows(n):
    per_worker = n // (SC_CORES * SC_SUBCORES)
    worker = lax.axis_index("s") * SC_CORES + lax.axis_index("c")
    return worker * per_worker, per_worker // SC_ROWS


def _sc_mesh():
    return plsc.VectorSubcoreMesh(core_axis_name="c", subcore_axis_name="s")


def _sc_scratch(w, dtype):
    return ([pltpu.VMEM((SC_ROWS,), I32)] * 2 + [pltpu.VMEM((SC_ROWS, w), dtype)] * 2
            + [pltpu.SemaphoreType.DMA] * 4)


def _sc_chunk_pair(base, p):
    off_a = pl.multiple_of(base + 2 * p * SC_ROWS, 8)
    return off_a, pl.multiple_of(off_a + SC_ROWS, 8)


def _sc_scatter_rows(src, dest, n_out):
    n_src, w = src.shape
    n_dst = dest.shape[1]
    workers = SC_CORES * SC_SUBCORES
    per_worker = n_src // workers
    chunks = per_worker // SC_SCATTER_ROWS
    assert n_src == workers * chunks * SC_SCATTER_ROWS
    idx = dest.T.reshape(n_dst, workers, chunks, SC_SCATTER_ROWS)

    @functools.partial(pl.kernel, mesh=_sc_mesh(), out_type=jax.ShapeDtypeStruct((n_out, w), src.dtype),
                       scratch_types=[pltpu.VMEM((n_dst, chunks, SC_SCATTER_ROWS), I32)]
                       + [pltpu.VMEM((SC_SCATTER_ROWS, w), src.dtype)] * 2 + [pltpu.SemaphoreType.DMA] * 4)
    def scatter(src_hbm, idx_hbm, out_hbm, idx_v, rows_a, rows_b, sem_ra, sem_rb, sem_wa, sem_wb):
        worker = lax.axis_index("s") * SC_CORES + lax.axis_index("c")
        base = worker * per_worker
        for k in range(n_dst):
            pltpu.sync_copy(idx_hbm.at[k, worker], idx_v.at[k])

        def read(c, rows, sem):
            return pltpu.async_copy(src_hbm.at[pl.ds(pl.multiple_of(base + c * SC_SCATTER_ROWS, 8), SC_SCATTER_ROWS)],
                                    rows, sem)

        def write_all(pending_read, c, rows, sem):
            pending_read.wait()
            return [pltpu.async_copy(rows, out_hbm.at[idx_v.at[k, c]], sem) for k in range(n_dst)]

        @pl.loop(0, chunks // 2)
        def _(p):
            read_a = read(2 * p, rows_a, sem_ra)
            read_b = read(2 * p + 1, rows_b, sem_rb)
            writes = write_all(read_a, 2 * p, rows_a, sem_wa) + write_all(read_b, 2 * p + 1, rows_b, sem_wb)
            for wr in writes:
                wr.wait()

        if chunks % 2:
            for wr in write_all(read(chunks - 1, rows_a, sem_ra), chunks - 1, rows_a, sem_wa):
                wr.wait()

    return scatter(src, idx)


def _sc_gather_rows(table, idx):
    n, w = idx.shape[0], table.shape[1]
    assert n % (SC_CORES * SC_SUBCORES * SC_ROWS * 2) == 0

    @functools.partial(pl.kernel, mesh=_sc_mesh(), out_type=jax.ShapeDtypeStruct((n, w), table.dtype),
                       scratch_types=_sc_scratch(w, table.dtype))
    def gather(table_hbm, idx_hbm, out_hbm, idx_a, idx_b, rows_a, rows_b, sem_ra, sem_rb, sem_wa, sem_wb):
        base, chunks = _sc_worker_rows(n)

        @pl.loop(0, chunks // 2)
        def _(p):
            off_a, off_b = _sc_chunk_pair(base, p)
            pltpu.sync_copy(idx_hbm.at[pl.ds(off_a, SC_ROWS)], idx_a)
            read_a = pltpu.async_copy(table_hbm.at[idx_a], rows_a, sem_ra)
            pltpu.sync_copy(idx_hbm.at[pl.ds(off_b, SC_ROWS)], idx_b)
            read_b = pltpu.async_copy(table_hbm.at[idx_b], rows_b, sem_rb)
            read_a.wait()
            write_a = pltpu.async_copy(rows_a, out_hbm.at[pl.ds(off_a, SC_ROWS)], sem_wa)
            read_b.wait()
            write_b = pltpu.async_copy(rows_b, out_hbm.at[pl.ds(off_b, SC_ROWS)], sem_wb)
            write_a.wait()
            write_b.wait()

    return gather(table, idx)


def _experts_body(b0_ref, nb_ref, last_ref, nt_ref, xs_hbm, wg_ref, bg_ref, wu_ref, bu_ref, wd_ref, bd_ref, o_hbm,
                  wg_s, wu_s, wd_s, xbuf, obuf, sem_in, sem_out):
    e = pl.program_id(0)
    first_blk, n_blk, last_valid, n_total = b0_ref[e], nb_ref[e], last_ref[e], nt_ref[0]
    half_rows = MOE_ROWS // 2

    def rows_of(g):
        return pl.ds(pl.multiple_of(g * MOE_ROWS, MOE_ROWS), MOE_ROWS)

    def fetch(g, slot):
        return pltpu.make_async_copy(xs_hbm.at[rows_of(g)], xbuf.at[slot], sem_in.at[slot])

    def put(g, slot):
        return pltpu.make_async_copy(obuf.at[slot], o_hbm.at[rows_of(g)], sem_out.at[slot])

    @pl.when((e == 0) & (n_total > 0))
    def _():
        fetch(0, 0).start()

    @pl.when(n_blk > 0)
    def _():
        wg_s[...] = wg_ref[...].astype(BF16)
        wu_s[...] = wu_ref[...].astype(BF16)
        wd_s[...] = wd_ref[...].astype(BF16)

    def compute(slot, rows):
        half = D_MODEL // 2
        x_hi, x_lo = _unpack_halves(xbuf[slot, 0:rows, :])
        x_hi = x_hi.astype(BF16)
        x_lo = x_lo.astype(BF16)
        g = (jnp.dot(x_hi, wg_s[:half, :], preferred_element_type=F32)
             + jnp.dot(x_lo, wg_s[half:, :], preferred_element_type=F32) + bg_ref[...])
        u = (jnp.dot(x_hi, wu_s[:half, :], preferred_element_type=F32)
             + jnp.dot(x_lo, wu_s[half:, :], preferred_element_type=F32) + bu_ref[...])
        g = jnp.minimum(g, SWIGLU_LIMIT)
        u = jnp.clip(u, -SWIGLU_LIMIT, SWIGLU_LIMIT)
        hmid = (u + 1.0) * (g * jax.nn.sigmoid(SWIGLU_ALPHA * g))
        out = jnp.dot(hmid.astype(BF16), wd_s[...], preferred_element_type=F32) + bd_ref[...]
        obuf[slot, 0:rows, :] = _pack_halves(out)

    def block(j, carry):
        g = first_blk + j
        slot = lax.rem(g, 2)
        fetch(g, slot).wait()

        @pl.when(g + 1 < n_total)
        def _():
            fetch(g + 1, 1 - slot).start()

        @pl.when(g >= 2)
        def _():
            put(g - 2, slot).wait()

        valid = jnp.where(j == n_blk - 1, last_valid, MOE_ROWS)

        @pl.when(valid > half_rows)
        def _():
            compute(slot, MOE_ROWS)

        @pl.when(valid <= half_rows)
        def _():
            compute(slot, half_rows)
            obuf[slot, half_rows:, :] = jnp.zeros((MOE_ROWS - half_rows, obuf.shape[2]), obuf.dtype)

        put(g, slot).start()
        return carry

    lax.fori_loop(0, n_blk, block, 0)

    @pl.when((e == N_EXPERTS - 1) & (n_total >= 2))
    def _():
        put(n_total - 2, lax.rem(n_total, 2)).wait()

    @pl.when((e == N_EXPERTS - 1) & (n_total >= 1))
    def _():
        put(n_total - 1, lax.rem(n_total - 1, 2)).wait()


def _experts(layer, first_blk, n_blk, last_valid, xs, wg, bg, wu, bu, wd, bd):
    n_rows, w = xs.shape
    wsel = lambda e, b0, nb, lv, nt: (layer, e, 0, 0)
    wspec = pl.BlockSpec((None, None, D_MODEL, D_MODEL), wsel)
    bspec = pl.BlockSpec((None, None, 1, D_MODEL), wsel)
    bias = lambda b: b.reshape(b.shape[0], b.shape[1], 1, b.shape[2])
    return pl.pallas_call(
        _experts_body,
        grid_spec=pltpu.PrefetchScalarGridSpec(
            num_scalar_prefetch=4,
            grid=(N_EXPERTS,),
            in_specs=[pl.BlockSpec(memory_space=pl.ANY), wspec, bspec, wspec, bspec, wspec, bspec],
            out_specs=pl.BlockSpec(memory_space=pl.ANY),
            scratch_shapes=[pltpu.VMEM((D_MODEL, D_MODEL), BF16)] * 3
            + [pltpu.VMEM((2, MOE_ROWS, w), U32)] * 2 + [pltpu.SemaphoreType.DMA((2,))] * 2,
        ),
        out_shape=jax.ShapeDtypeStruct((n_rows, w), U32),
        compiler_params=_cparams(1),
        name="experts",
    )(first_blk, n_blk, last_valid, jnp.sum(n_blk).reshape(1), xs, wg, bias(bg), wu, bias(bu), wd, bias(bd))


def _combine_body(n_first, o0_ref, o1_ref, o2_ref, o3_ref, gt_ref, x_ref, g_ref, b_ref, ya_ref, yb_ref=None):
    half = D_MODEL // 2
    gates = gt_ref[...]
    hi = jnp.zeros((x_ref.shape[0], half), F32)
    lo = jnp.zeros((x_ref.shape[0], half), F32)
    for k, o_ref in enumerate((o0_ref, o1_ref, o2_ref, o3_ref)):
        h, l = _unpack_halves(o_ref[...])
        gk = gates[:, k:k + 1]
        hi = hi + gk * h
        lo = lo + gk * l
    x = x_ref[...]
    y_hi = DEEPNORM_ALPHA * x[:, :half] + hi
    y_lo = DEEPNORM_ALPHA * x[:, half:] + lo
    mu = (jnp.sum(y_hi, axis=-1, keepdims=True) + jnp.sum(y_lo, axis=-1, keepdims=True)) * (1.0 / D_MODEL)
    d_hi = y_hi - mu
    d_lo = y_lo - mu
    var = (jnp.sum(d_hi * d_hi, axis=-1, keepdims=True) + jnp.sum(d_lo * d_lo, axis=-1, keepdims=True)) * (1.0 / D_MODEL)
    r = lax.rsqrt(var + LN_EPS)
    out_hi = d_hi * r * g_ref[:, :half] + b_ref[:, :half]
    out_lo = d_lo * r * g_ref[:, half:] + b_ref[:, half:]

    def write(y_ref):
        y_ref[:, :half] = out_hi
        y_ref[:, half:] = out_lo

    if yb_ref is None:
        write(ya_ref)
    else:
        pl.when(pl.program_id(0) < n_first)(lambda: write(ya_ref))
        pl.when(pl.program_id(0) >= n_first)(lambda: write(yb_ref))


def _combine_ln(o4, gates, x, g, b, t_first=None):
    t = x.shape[0]
    tm = TOKEN_TILE
    row = lambda i: (i, 0)
    fix = lambda i: (0, 0)
    choice = lambda k: pl.BlockSpec((tm, 512), lambda i: (k * (t // tm) + i, 0))
    if t_first is None:
        n_first = None
        out_specs = pl.BlockSpec((tm, D_MODEL), row)
        out_shape = jax.ShapeDtypeStruct((t, D_MODEL), F32)
    else:
        n_first = t_first // tm
        out_specs = [pl.BlockSpec((tm, D_MODEL), lambda i: (jnp.minimum(i, n_first - 1), 0)),
                     pl.BlockSpec((tm, D_MODEL), lambda i: (jnp.maximum(i - n_first, 0), 0))]
        out_shape = [jax.ShapeDtypeStruct((t_first, D_MODEL), F32), jax.ShapeDtypeStruct((t - t_first, D_MODEL), F32)]
    return pl.pallas_call(
        functools.partial(_combine_body, n_first),
        grid=(t // tm,),
        in_specs=[choice(0), choice(1), choice(2), choice(3), pl.BlockSpec((tm, TOP_K), row),
                  pl.BlockSpec((tm, D_MODEL), row), pl.BlockSpec((1, D_MODEL), fix), pl.BlockSpec((1, D_MODEL), fix)],
        out_specs=out_specs,
        out_shape=out_shape,
        compiler_params=_cparams(1),
        name="combine_ln",
    )(o4, o4, o4, o4, gates, x, g, b)


def _moe(layer, x1, xp, gates, eidx, rank, counts, ln_g, ln_b, wg, bg, wu, bu, wd, bd, t_first=None):
    t = x1.shape[0]
    bm = MOE_ROWS
    n_blocks = t * TOP_K // bm + N_EXPERTS
    n_rows = n_blocks * bm
    cnt = counts[:, 0]
    padded = (cnt + bm - 1) // bm * bm
    pad_end = jnp.cumsum(padded)
    pad_start = pad_end - padded
    e = eidx[:TOP_K]
    start = jnp.sum(jnp.where(e[:, :, None] == jnp.arange(N_EXPERTS, dtype=I32), pad_start, 0), axis=-1)
    dest = (start + rank[:TOP_K]).T
    n_blk = padded // bm
    last_valid = cnt - (n_blk - 1) * bm
    xs = _sc_scatter_rows(xp, dest, n_rows)
    outs = _experts(layer, pad_start // bm, n_blk, last_valid, xs, wg, bg, wu, bu, wd, bd)
    o4 = _sc_gather_rows(outs, dest.T.reshape(-1))
    return _combine_ln(o4, gates[:TOP_K].T, x1, ln_g, ln_b, t_first)


def _split2(x):
    hi = x.astype(BF16)
    return hi, (x - hi.astype(F32)).astype(BF16)


def _split3(x):
    hi = x.astype(BF16)
    rem = x - hi.astype(F32)
    mid = rem.astype(BF16)
    return hi, mid, (rem - mid.astype(F32)).astype(BF16)


def _chunk_cumsum(g, C):
    rows = g.shape[0]
    r = lax.broadcasted_iota(I32, (rows, rows), 0)
    c = lax.broadcasted_iota(I32, (rows, rows), 1)
    tri = ((r >= c) & (r // C == c // C)).astype(BF16)
    hi, mid, lo = _split3(g)
    dot = lambda part: jnp.dot(tri, part, preferred_element_type=F32)
    return dot(hi) + dot(mid) + dot(lo)


def _gla_body(mode, n_seq, n_chunk, C, *refs):
    if mode == "gla":
        hq_ref, hk_ref, hv_ref, hg_ref, hlr_ref, wlr_ref, blr_ref, nw_ref, s0_ref, _, o_ref, so_ref, st, o_scr = refs
    else:
        hq_ref, hk_ref, hv_ref, hg_ref, lb_ref, nw_ref, s0_ref, _, o_ref, so_ref, st, o_scr = refs
    n_heads = 4
    tstep = pl.program_id(1)

    @pl.when(tstep == 0)
    def _():
        for s in range(n_seq):
            for h in range(n_heads):
                st[s, h] = s0_ref[s, h].T

    if mode == "gla":
        q = hq_ref[...] * (GLA_DK ** -0.5)
        k = hk_ref[...]
        z = _dot(hlr_ref[...], wlr_ref[...]) + blr_ref[...]
        g = _log_sigmoid(z) * (1.0 / GLA_TAU)
    else:
        q = _silu(hq_ref[...]) * (HGRN_DK ** -0.5)
        lb = lb_ref[...]
        f = lb + (1.0 - lb) * jax.nn.sigmoid(hk_ref[...])
        k = 1.0 - f
        g = jnp.log(f)
    v = hv_ref[...]
    causal = _tri(C)
    mid = max(C // 2 - 1, 0)
    b_all = _chunk_cumsum(g, C)
    for s in range(n_seq):
        states = [st[s, h] for h in range(n_heads)]
        for c in range(n_chunk):
            r0 = (s * n_chunk + c) * C
            rs = slice(r0, r0 + C)
            b, qc, kc = b_all[rs, :], q[rs, :], k[rs, :]
            b_last = b[C - 1:C, :]
            b_mid = b[mid:mid + 1, :]
            qe_hi, qe_lo = _split2(qc * jnp.exp(b - b_mid))
            ke_hi, ke_lo = _split2(kc * jnp.exp(b_mid - b))
            q_state = (qc * jnp.exp(b)).astype(BF16)
            k_state = (kc * jnp.exp(b_last - b)).astype(BF16)
            decay = jnp.exp(b_last)
            for h in range(n_heads):
                cs = slice(h * HEAD_W, (h + 1) * HEAD_W)
                lhs = jnp.concatenate([qe_hi[:, cs], qe_hi[:, cs], qe_lo[:, cs]], axis=1)
                rhs = jnp.concatenate([ke_hi[:, cs], ke_lo[:, cs], ke_hi[:, cs]], axis=1)
                scores = jnp.where(causal, _dot_nt(lhs, rhs), 0.0)
                vh = v[rs, cs].astype(BF16)
                o = _dot(scores, vh) + _dot_nt(q_state[:, cs], states[h])
                states[h] = states[h] * decay[:, cs] + _dot_tn(vh, k_state[:, cs])
                ms = jnp.mean(o * o, axis=-1, keepdims=True)
                o_scr[rs, cs] = o * lax.rsqrt(ms + RMS_EPS)
        for h in range(n_heads):
            st[s, h] = states[h]
    o_ref[...] = (o_scr[...] * nw_ref[...] * _silu(hg_ref[...])).astype(BF16)

    @pl.when(tstep == pl.num_programs(1) - 1)
    def _():
        for s in range(n_seq):
            for h in range(n_heads):
                so_ref[s, h] = st[s, h].T


def _seq_layout(n_batch, seq_len, row_off, sample):
    if sample:
        n_seq, n_chunk, C = SAMPLE_SEQS, 1, seq_len
        rows = n_seq * C
        grid = (n_batch // n_seq, 1)
        blk0 = row_off // rows
        rb = lambda i, t: blk0 + i
    else:
        n_seq, n_chunk, C = 1, PROMPT_TILE // SCAN_CHUNK, SCAN_CHUNK
        rows = PROMPT_TILE
        tiles = seq_len // rows
        grid = (n_batch, tiles)
        blk0 = row_off // rows
        rb = lambda i, t: blk0 + i * tiles + t
    return n_seq, n_chunk, C, rows, grid, rb


def _gla_call(mode, h, cols, extra, nw, s0, out_buf, n_batch, seq_len, row_off, sample):
    n_seq, n_chunk, C, rows, grid, rb = _seq_layout(n_batch, seq_len, row_off, sample)
    colspec = lambda c0, w: pl.BlockSpec((rows, w), lambda i, t: (rb(i, t), c0 // w))
    fix2 = lambda i, t: (0, 0)
    in_specs = [colspec(cols[0], 512), colspec(cols[1], 512), colspec(cols[2], 512), colspec(cols[3], 512)]
    args = [h, h, h, h]
    if mode == "gla":
        wlr, blr = extra
        in_specs += [colspec(cols[4], LANE), pl.BlockSpec((LANE, 512), fix2), pl.BlockSpec((1, 512), fix2)]
        args += [h, wlr, blr]
    else:
        in_specs += [pl.BlockSpec((1, 512), fix2)]
        args += [extra]
    st_spec = pl.BlockSpec((n_seq, 4, HEAD_W, HEAD_W), lambda i, t: (i, 0, 0, 0))
    in_specs += [pl.BlockSpec((1, 512), fix2), st_spec, pl.BlockSpec(memory_space=pl.ANY)]
    args += [nw, s0, out_buf]
    o_spec = pl.BlockSpec((rows, 512), lambda i, t: (rb(i, t), 0))
    return pl.pallas_call(
        functools.partial(_gla_body, mode, n_seq, n_chunk, C),
        grid=grid,
        in_specs=in_specs,
        out_specs=[o_spec, st_spec],
        out_shape=[jax.ShapeDtypeStruct(out_buf.shape, out_buf.dtype),
                   jax.ShapeDtypeStruct((n_batch, 4, HEAD_W, HEAD_W), F32)],
        scratch_shapes=[pltpu.VMEM((n_seq, 4, HEAD_W, HEAD_W), F32), pltpu.VMEM((rows, 512), F32)],
        input_output_aliases={len(args) - 1: 0},
        compiler_params=_cparams(2),
        name=mode + ("_sample" if sample else "_prompt"),
    )(*args)


def _round_bf16(x, on=True):
    return x.astype(BF16).astype(F32) if on else x


def _conf_body(n_seq, L, round_x, round_w, a_ref, gt_ref, hist_ref, w_ref, b_ref, g_ref, be_ref, _, o_ref, co_ref,
               buf, bufr, y_scr, win):
    tstep = pl.program_id(1)
    hist = CONF_WIDTH - 1
    pad = 32 - hist

    @pl.when(tstep == 0)
    def _():
        for s in range(n_seq):
            buf[s, pad:32, :] = hist_ref[s]
            bufr[s, pad:32, :] = _round_bf16(hist_ref[s], round_x)

    u = a_ref[...] * jax.nn.sigmoid(gt_ref[...])
    ur = _round_bf16(u, round_x)
    for s in range(n_seq):
        buf[s, 32:32 + L, :] = u[s * L:(s + 1) * L, :]
        bufr[s, 32:32 + L, :] = ur[s * L:(s + 1) * L, :]
    w = _round_bf16(w_ref[...], round_w)
    for s in range(n_seq):
        acc = jnp.zeros((L, CONF_DIM), F32)
        for phase in range(8):
            n_taps = (CONF_WIDTH - 1 - phase) // 8 + 1
            span = L + 8 * (n_taps - 1)
            win[phase, 0:span, :] = bufr[s, pad + phase:pad + phase + span, :]
            for a in range(n_taps):
                j = 8 * a + phase
                acc = acc + win[phase, 8 * a:8 * a + L, :] * w[j:j + 1, :]
        y_scr[s * L:(s + 1) * L, :] = _silu(_layernorm(acc + b_ref[...], g_ref[...], be_ref[...]))
        tail = buf[s, L + pad:L + 32, :]
        buf[s, pad:32, :] = tail
        tailr = bufr[s, L + pad:L + 32, :]
        bufr[s, pad:32, :] = tailr
    o_ref[...] = y_scr[...].astype(o_ref.dtype)

    @pl.when(tstep == pl.num_programs(1) - 1)
    def _():
        for s in range(n_seq):
            co_ref[s] = buf[s, pad:32, :]


def _conf_call(h, col_a, col_g, cache, w, b, g, be, out_buf, n_batch, seq_len, row_off, sample):
    n_seq, n_chunk, C, rows, grid, rb = _seq_layout(n_batch, seq_len, row_off, sample)
    L = rows // n_seq
    hist = CONF_WIDTH - 1
    colspec = lambda c0: pl.BlockSpec((rows, 512), lambda i, t: (rb(i, t), c0 // 512))
    fix2 = lambda i, t: (0, 0)
    c_spec = pl.BlockSpec((n_seq, hist, CONF_DIM), lambda i, t: (i, 0, 0))
    return pl.pallas_call(
        functools.partial(_conf_body, n_seq, L, True, sample),
        grid=grid,
        in_specs=[colspec(col_a), colspec(col_g), c_spec,
                  pl.BlockSpec((CONF_WIDTH, CONF_DIM), fix2), pl.BlockSpec((1, CONF_DIM), fix2),
                  pl.BlockSpec((1, CONF_DIM), fix2), pl.BlockSpec((1, CONF_DIM), fix2),
                  pl.BlockSpec(memory_space=pl.ANY)],
        out_specs=[pl.BlockSpec((rows, 512), lambda i, t: (rb(i, t), 0)), c_spec],
        out_shape=[jax.ShapeDtypeStruct(out_buf.shape, out_buf.dtype),
                   jax.ShapeDtypeStruct((n_batch, hist, CONF_DIM), F32)],
        scratch_shapes=[pltpu.VMEM((n_seq, 32 + L, CONF_DIM), F32)] * 2 + [pltpu.VMEM((rows, CONF_DIM), F32),
                                                                           pltpu.VMEM((8, L + 24, CONF_DIM), F32)],
        input_output_aliases={7: 0},
        compiler_params=_cparams(2),
        name="conformer" + ("_sample" if sample else "_prompt"),
    )(h, h, cache, w, b, g, be, out_buf)


def _ssd_body(n_seq, n_chunk, C, round_x, round_w, hz_ref, hx_ref, hdt_ref, hist_ref, s0_ref, cw_ref, cb_ref, dtb_ref, alog_ref,
              dvec_ref, nw_ref, _, o_ref, co_ref, so_ref, st, buf, bufr, xbc, y_scr):
    tstep = pl.program_id(1)
    L = n_chunk * C
    hist = SSM_CONV - 1
    pad = 8 - hist
    n_pairs = SSM_HEADS // 2

    @pl.when(tstep == 0)
    def _():
        for s in range(n_seq):
            buf[s, pad:8, :] = hist_ref[s]
            bufr[s, pad:8, :] = _round_bf16(hist_ref[s], round_x)
            for m in range(n_pairs):
                st[s, m] = s0_ref[s, m]

    cw = _round_bf16(cw_ref[...], round_w)
    for s in range(n_seq):
        hx = hx_ref[s * L:(s + 1) * L, :]
        buf[s, 8:8 + L, :] = hx
        bufr[s, 8:8 + L, :] = _round_bf16(hx, round_x)
        acc = jnp.zeros((L, SSM_CONV_DIM), F32)
        for j in range(SSM_CONV):
            acc = acc + bufr[s, pad + j:pad + j + L, :] * cw[j:j + 1, :]
        xbc[s * L:(s + 1) * L, :] = _silu(acc + cb_ref[...])
        tail = buf[s, L + pad:L + 8, :]
        buf[s, pad:8, :] = tail
        tailr = bufr[s, L + pad:L + 8, :]
        bufr[s, pad:8, :] = tailr

    dt = _softplus(hdt_ref[...] + dtb_ref[...])
    la = dt * (-jnp.exp(alog_ref[...]))
    hrow = lax.broadcasted_iota(I32, (LANE, SSM_INNER), 0)
    hcol = lax.broadcasted_iota(I32, (LANE, SSM_INNER), 1) // SSM_HEADDIM
    expand = (hrow == hcol).astype(BF16)
    dtx = functools.reduce(lambda p, q: p + q,
                           [jnp.dot(part, expand, preferred_element_type=F32) for part in _split3(dt)])
    causal = _tri(C)
    tri = causal.astype(BF16)
    lane = lax.broadcasted_iota(I32, (C, HEAD_W), 1)
    bcol_all = _chunk_cumsum(la, C)
    heads_per_group = SSM_HEADS // SSM_GROUPS
    for s in range(n_seq):
        states = [st[s, m] for m in range(n_pairs)]
        for c in range(n_chunk):
            r0 = (s * n_chunk + c) * C
            rs = slice(r0, r0 + C)
            bcol = bcol_all[rs, :]
            brow = functools.reduce(lambda p, q: p + q, [
                lax.dot_general(part, tri, (((0,), (1,)), ((), ())), preferred_element_type=F32)
                for part in _split3(la[rs, :])])
            xs_c = xbc[rs, 0:SSM_INNER]
            v_c = (xs_c * dtx[rs, :]).astype(BF16)
            gmats, bms, cms = [], [], []
            for grp in range(SSM_GROUPS):
                bm = xbc[rs, SSM_INNER + grp * SSM_STATE:SSM_INNER + (grp + 1) * SSM_STATE]
                cm = xbc[rs, SSM_INNER + (SSM_GROUPS + grp) * SSM_STATE:SSM_INNER + (SSM_GROUPS + grp + 1) * SSM_STATE]
                cm_hi, cm_lo = _split2(cm)
                bm_hi, bm_lo = _split2(bm)
                gmats.append(_dot_nt(jnp.concatenate([cm_hi, cm_hi, cm_lo], axis=1),
                                     jnp.concatenate([bm_hi, bm_lo, bm_hi], axis=1)))
                bms.append(bm)
                cms.append(cm)
            for m in range(n_pairs):
                grp = (2 * m) // heads_per_group
                bm, cm, gmat = bms[grp], cms[grp], gmats[grp]
                ps = slice(m * HEAD_W, (m + 1) * HEAD_W)
                vp = v_c[:, ps]
                s_t = states[m]
                o_halves, new_rows = [], []
                for hh in range(2):
                    hd = 2 * m + hh
                    bc = bcol[:, hd:hd + 1]
                    br = brow[hd:hd + 1, :]
                    dec = jnp.where(causal, jnp.exp(jnp.minimum(bc - br, 0.0)), 0.0)
                    b_last = bcol[C - 1:C, hd:hd + 1]
                    o_halves.append(_dot(gmat * dec, vp) + _dot_nt(cm * jnp.exp(bc), s_t))
                    kv = _dot_tn(vp, bm * jnp.exp(b_last - bc))
                    vs = slice(hh * SSM_HEADDIM, (hh + 1) * SSM_HEADDIM)
                    new_rows.append(s_t[vs, :] * jnp.exp(b_last) + kv[vs, :])
                states[m] = jnp.concatenate(new_rows, axis=0)
                o_pair = jnp.where(lane < SSM_HEADDIM, o_halves[0], o_halves[1])
                y_scr[rs, ps] = o_pair + dvec_ref[:, ps] * xs_c[:, ps]
        for m in range(n_pairs):
            st[s, m] = states[m]
    y = y_scr[...] * _silu(hz_ref[...])
    gw = SSM_INNER // SSM_GROUPS
    for grp in range(SSM_GROUPS):
        gs = slice(grp * gw, (grp + 1) * gw)
        yg = y[:, gs]
        ms = jnp.mean(yg * yg, axis=-1, keepdims=True)
        o_ref[:, gs] = (yg * lax.rsqrt(ms + RMS_EPS) * nw_ref[:, gs]).astype(BF16)

    @pl.when(tstep == pl.num_programs(1) - 1)
    def _():
        for s in range(n_seq):
            co_ref[s] = buf[s, pad:8, :]
            for m in range(n_pairs):
                so_ref[s, m] = st[s, m]


def _ssd_call(h, col_z, col_x, col_dt, cache, s0, cw, cb, dtb, alog, dvec, nw, out_buf, n_batch, seq_len, row_off,
              sample):
    n_seq, n_chunk, C, rows, grid, rb = _seq_layout(n_batch, seq_len, row_off, sample)
    L = rows // n_seq
    hist = SSM_CONV - 1
    n_pairs = SSM_HEADS // 2
    colspec = lambda c0, w: pl.BlockSpec((rows, w), lambda i, t: (rb(i, t), c0 // w))
    fix2 = lambda i, t: (0, 0)
    c_spec = pl.BlockSpec((n_seq, hist, SSM_CONV_DIM), lambda i, t: (i, 0, 0))
    st_spec = pl.BlockSpec((n_seq, n_pairs, HEAD_W, SSM_STATE), lambda i, t: (i, 0, 0, 0))
    return pl.pallas_call(
        functools.partial(_ssd_body, n_seq, n_chunk, C, sample, True),
        grid=grid,
        in_specs=[colspec(col_z, 512), colspec(col_x, SSM_CONV_DIM), colspec(col_dt, LANE), c_spec, st_spec,
                  pl.BlockSpec((SSM_CONV, SSM_CONV_DIM), fix2), pl.BlockSpec((1, SSM_CONV_DIM), fix2),
                  pl.BlockSpec((1, LANE), fix2), pl.BlockSpec((1, LANE), fix2),
                  pl.BlockSpec((1, SSM_INNER), fix2), pl.BlockSpec((1, SSM_INNER), fix2),
                  pl.BlockSpec(memory_space=pl.ANY)],
        out_specs=[pl.BlockSpec((rows, 512), lambda i, t: (rb(i, t), 0)), c_spec, st_spec],
        out_shape=[jax.ShapeDtypeStruct(out_buf.shape, out_buf.dtype),
                   jax.ShapeDtypeStruct((n_batch, hist, SSM_CONV_DIM), F32),
                   jax.ShapeDtypeStruct((n_batch, n_pairs, HEAD_W, SSM_STATE), F32)],
        scratch_shapes=[pltpu.VMEM((n_seq, n_pairs, HEAD_W, SSM_STATE), F32),
                        pltpu.VMEM((n_seq, 8 + L, SSM_CONV_DIM), F32),
                        pltpu.VMEM((n_seq, 8 + L, SSM_CONV_DIM), F32),
                        pltpu.VMEM((rows, SSM_CONV_DIM), F32),
                        pltpu.VMEM((rows, SSM_INNER), F32)],
        input_output_aliases={11: 0},
        compiler_params=_cparams(2),
        name="ssd" + ("_sample" if sample else "_prompt"),
    )(h, h, h, cache, s0, cw, cb, dtb, alog, dvec, nw, out_buf)


def _pad_heads(w, n_heads, width):
    lead = w.shape[:-1]
    w = w.reshape(lead + (n_heads, width))
    w = jnp.pad(w, [(0, 0)] * len(lead) + [(0, 0), (0, HEAD_W - width)])
    return w.reshape(lead + (n_heads * HEAD_W,))


def _row(v):
    return v.reshape(1, -1).astype(F32)


def kernel(x_prompt, x_sample, state_gla, cache_conformer, state_hgrn, state_ssm, cache_mamba_conv, w_in_even, w_gla_gate_lr, b_gla_gate, gla_norm_w, conf_conv_w, conf_conv_b, conf_ln_g, conf_ln_b, w_out_even, w_in_odd, hgrn_lower_bounds, hgrn_norm_w, mamba_conv_w, mamba_conv_b, mamba_dt_bias, mamba_a_log, mamba_d, mamba_norm_w, w_out_odd, ln1_g, ln1_b, ln2_g, ln2_b, router_w, router_b, expert_w_gate, expert_b_gate, expert_w_up, expert_b_up, expert_w_down, expert_b_down):
    bp, lp, _ = x_prompt.shape
    bs, ls, _ = x_sample.shape
    tp, ts = bp * lp, bs * ls
    x = (x_prompt.reshape(tp, D_MODEL), x_sample.reshape(ts, D_MODEL))

    def router_params(layer):
        return router_w[layer].T.astype(BF16), router_b[layer].astype(F32).reshape(N_EXPERTS, 1)

    def finish_layer(layer, x, mix_a, mix_b, w_out):
        rwt, rb = router_params(layer)
        x1, xp, gates, eidx, rank, counts = _outproj_ln_router(
            mix_a, mix_b, x[0], x[1], w_out[:512].astype(BF16), w_out[512:].astype(BF16),
            _row(ln1_g[layer]), _row(ln1_b[layer]), rwt, rb)
        return _moe(layer, x1, xp, gates, eidx, rank, counts, _row(ln2_g[layer]), _row(ln2_b[layer]),
                    expert_w_gate, expert_b_gate, expert_w_up, expert_b_up, expert_w_down, expert_b_down, tp)

    def mix_buffer():
        return jnp.zeros((tp + ts, 512), BF16)

    wi = w_in_even[0]
    wq, wk, wv, wg, wlr, wglu = jnp.split(wi, [256, 512, 1024, 1536, 1552], axis=1)
    w_even = jnp.concatenate([_pad_heads(wq, GLA_HEADS, GLA_DK), _pad_heads(wk, GLA_HEADS, GLA_DK), wv, wg, wglu,
                              jnp.pad(wlr, ((0, 0), (0, LANE - GLA_RANK)))], axis=1).astype(BF16)
    cols_gla = (0, 512, 1024, 1536, 3072)
    col_a, col_gate = 2048, 2560
    h = _inproj(x[0], x[1], w_even)
    wlr_p = jnp.pad(_pad_heads(w_gla_gate_lr[0], GLA_HEADS, GLA_DK), ((0, LANE - GLA_RANK), (0, 0)))
    blr_p = _row(_pad_heads(b_gla_gate[0], GLA_HEADS, GLA_DK))
    nw = _row(gla_norm_w[0])
    conf_args = (conf_conv_w[0], _row(conf_conv_b[0]), _row(conf_ln_g[0]), _row(conf_ln_b[0]))
    s0_p = jnp.zeros((bp, GLA_HEADS, HEAD_W, HEAD_W), F32)
    s0_s = jnp.pad(state_gla[0], ((0, 0), (0, 0), (0, HEAD_W - GLA_DK), (0, 0)))
    mix_a, sg_p = _gla_call("gla", h, cols_gla, (wlr_p, blr_p), nw, s0_p, mix_buffer(), bp, lp, 0, False)
    mix_a, sg_s = _gla_call("gla", h, cols_gla, (wlr_p, blr_p), nw, s0_s, mix_a, bs, ls, tp, True)
    mix_b, cc_p = _conf_call(h, col_a, col_gate, jnp.zeros((bp,) + cache_conformer.shape[2:], F32), *conf_args,
                             mix_buffer(), bp, lp, 0, False)
    mix_b, cc_s = _conf_call(h, col_a, col_gate, cache_conformer[0], *conf_args, mix_b, bs, ls, tp, True)
    x = finish_layer(0, x, mix_a, mix_b, w_out_even[0])
    gla_p, gla_s = sg_p[:, :, :GLA_DK, :][None], sg_s[:, :, :GLA_DK, :][None]
    conf_p, conf_s = cc_p[None], cc_s[None]

    lb_cum = jnp.cumsum(jax.nn.softmax(hgrn_lower_bounds.astype(F32), axis=0), axis=0)
    lower_bound = _row((lb_cum - lb_cum[0])[1])
    wo = w_in_odd[0]
    w_odd = jnp.concatenate([wo[:, 2560:3584], wo[:, :2560],
                             jnp.pad(wo[:, 3584:], ((0, 0), (0, LANE - SSM_HEADS)))], axis=1).astype(BF16)
    h = _inproj(x[0], x[1], w_odd)
    cols_hgrn = (1024, 1536, 2048, 2560)
    col_z, col_x, col_dt = 3072, 0, 3584
    nw = _row(hgrn_norm_w[0])
    mix_a, sh_p = _gla_call("hgrn", h, cols_hgrn, lower_bound, nw,
                            jnp.zeros((bp, HGRN_HEADS, HEAD_W, HEAD_W), F32), mix_buffer(), bp, lp, 0, False)
    mix_a, sh_s = _gla_call("hgrn", h, cols_hgrn, lower_bound, nw, state_hgrn[0], mix_a, bs, ls, tp, True)

    def pair_states(s):
        return jnp.swapaxes(s, 2, 3).reshape(s.shape[0], SSM_HEADS // 2, HEAD_W, SSM_STATE)

    def unpair_states(s):
        return jnp.swapaxes(s.reshape(s.shape[0], SSM_HEADS, SSM_HEADDIM, SSM_STATE), 2, 3)

    pad8 = lambda v: jnp.pad(v.astype(F32), (0, LANE - SSM_HEADS)).reshape(1, LANE)
    ssd_args = (mamba_conv_w[0], _row(mamba_conv_b[0]), pad8(mamba_dt_bias[0]), pad8(mamba_a_log[0]),
                _row(jnp.repeat(mamba_d[0], SSM_HEADDIM)), _row(mamba_norm_w[0]))
    mix_b, cm_p, ss_p = _ssd_call(h, col_z, col_x, col_dt, jnp.zeros((bp,) + cache_mamba_conv.shape[2:], F32),
                                  jnp.zeros((bp, SSM_HEADS // 2, HEAD_W, SSM_STATE), F32), *ssd_args,
                                  mix_buffer(), bp, lp, 0, False)
    mix_b, cm_s, ss_s = _ssd_call(h, col_z, col_x, col_dt, cache_mamba_conv[0], pair_states(state_ssm[0]),
                                  *ssd_args, mix_b, bs, ls, tp, True)
    y_prompt, y_sample = finish_layer(1, x, mix_a, mix_b, w_out_odd[0])
    y_prompt = y_prompt.reshape(bp, lp, D_MODEL)
    y_sample = y_sample.reshape(bs, ls, D_MODEL)
    return (y_prompt, y_sample, gla_p, gla_s, conf_p, conf_s, sh_p[None], sh_s[None],
            unpair_states(ss_p)[None], unpair_states(ss_s)[None], cm_p[None], cm_s[None])
```

```python
import functools

import jax
import jax.numpy as jnp
from jax import lax
from jax.experimental import pallas as pl
from jax.experimental.pallas import tpu as pltpu
from jax.experimental.pallas import tpu_sc as plsc

F32 = jnp.float32
BF16 = jnp.bfloat16
I32 = jnp.int32
U32 = jnp.uint32
HIGHEST = lax.Precision.HIGHEST

D_MODEL = 1024
DEPTH = 2
DEEPNORM_ALPHA = (2.0 * DEPTH) ** 0.25
LN_EPS = 1e-5
RMS_EPS = 1e-6
LANE = 128
HEAD_W = 128
GLA_HEADS, GLA_DK, GLA_RANK, GLA_TAU = 4, 64, 16, 16.0
CONF_DIM, CONF_WIDTH = 512, 31
HGRN_HEADS, HGRN_DK = 4, 128
SSM_HEADS, SSM_HEADDIM, SSM_STATE, SSM_GROUPS, SSM_CONV = 8, 64, 128, 2, 4
SSM_INNER = SSM_HEADS * SSM_HEADDIM
SSM_CONV_DIM = SSM_INNER + 2 * SSM_GROUPS * SSM_STATE
N_EXPERTS, TOP_K = 32, 4
SWIGLU_ALPHA, SWIGLU_LIMIT = 1.702, 7.0
SCAN_CHUNK = 64
PROMPT_TILE = 256
SAMPLE_SEQS = 16
TOKEN_TILE = 512
MOE_ROWS = 512
SC_CORES, SC_SUBCORES = 2, 16
SC_ROWS = 64
SC_SCATTER_ROWS = 32
VMEM_LIMIT = 56 * 1024 * 1024


def _cparams(n_axes):
    return pltpu.CompilerParams(dimension_semantics=("arbitrary",) * n_axes, vmem_limit_bytes=VMEM_LIMIT)


def _silu(x):
    return x * jax.nn.sigmoid(x)


def _softplus(x):
    return jnp.maximum(x, 0.0) + jnp.log(1.0 + jnp.exp(-jnp.abs(x)))


def _log_sigmoid(x):
    return jnp.minimum(x, 0.0) - jnp.log(1.0 + jnp.exp(-jnp.abs(x)))


def _layernorm(y, g, b):
    mu = jnp.mean(y, axis=-1, keepdims=True)
    d = y - mu
    var = jnp.mean(d * d, axis=-1, keepdims=True)
    return d * lax.rsqrt(var + LN_EPS) * g + b


def _dot(a, b):
    return jnp.dot(a.astype(BF16), b.astype(BF16), preferred_element_type=F32)


def _dot_nt(a, b):
    return lax.dot_general(a.astype(BF16), b.astype(BF16), (((1,), (1,)), ((), ())), preferred_element_type=F32)


def _dot_tn(a, b):
    return lax.dot_general(a.astype(BF16), b.astype(BF16), (((0,), (0,)), ((), ())), preferred_element_type=F32)


def _tri(c):
    r = lax.broadcasted_iota(I32, (c, c), 0)
    k = lax.broadcasted_iota(I32, (c, c), 1)
    return r >= k


def _pair_specs(tm, n_first, width):
    return [pl.BlockSpec((tm, width), lambda i: (jnp.minimum(i, n_first - 1), 0)),
            pl.BlockSpec((tm, width), lambda i: (jnp.maximum(i - n_first, 0), 0))]


def _pair_tile(n_first, xa_ref, xb_ref):
    return jnp.where(pl.program_id(0) < n_first, xa_ref[...], xb_ref[...])


def _inproj_body(n_first, xa_ref, xb_ref, w_ref, o_ref):
    xb = _pair_tile(n_first, xa_ref, xb_ref).astype(BF16)
    n = w_ref.shape[1]
    for c0 in range(0, n, 512):
        c1 = min(c0 + 512, n)
        o_ref[:, c0:c1] = jnp.dot(xb, w_ref[:, c0:c1], preferred_element_type=F32)


def _inproj(xa, xb, w):
    k, n = w.shape
    t = xa.shape[0] + xb.shape[0]
    n_first = xa.shape[0] // TOKEN_TILE
    return pl.pallas_call(
        functools.partial(_inproj_body, n_first),
        grid=(t // TOKEN_TILE,),
        in_specs=_pair_specs(TOKEN_TILE, n_first, k) + [pl.BlockSpec((k, n), lambda i: (0, 0))],
        out_specs=pl.BlockSpec((TOKEN_TILE, n), lambda i: (i, 0)),
        out_shape=jax.ShapeDtypeStruct((t, n), F32),
        compiler_params=_cparams(1),
        name="inproj",
    )(xa, xb, w)


def _pack_halves(y):
    half = y.shape[1] // 2
    hi = lax.bitcast_convert_type(y[:, :half].astype(BF16).astype(F32), U32)
    lo = lax.bitcast_convert_type(y[:, half:].astype(BF16).astype(F32), U32)
    return (hi & jnp.uint32(0xFFFF0000)) | (lo >> 16)


def _unpack_halves(w):
    hi = lax.bitcast_convert_type(w & jnp.uint32(0xFFFF0000), F32)
    lo = lax.bitcast_convert_type(w << 16, F32)
    return hi, lo


def _outproj_body(n_first, a_ref, b_ref, xa_ref, xb_ref, wa_ref, wb_ref, g_ref, be_ref, rwt_ref, rb_ref,
                  x1_ref, xp_ref, gate_ref, idx_ref, rank_ref, cnt_ref, carry):
    @pl.when(pl.program_id(0) == 0)
    def _():
        carry[...] = jnp.zeros(carry.shape, F32)

    mix = (jnp.dot(a_ref[...], wa_ref[...], preferred_element_type=F32)
           + jnp.dot(b_ref[...], wb_ref[...], preferred_element_type=F32))
    x1 = _layernorm(DEEPNORM_ALPHA * _pair_tile(n_first, xa_ref, xb_ref) + mix, g_ref[...], be_ref[...])
    x1_ref[...] = x1
    xp_ref[...] = _pack_halves(x1)
    logits = _dot_nt(rwt_ref[...], x1) + rb_ref[...]
    tm = logits.shape[1]
    expert = lax.broadcasted_iota(I32, logits.shape, 0)
    vals, idxs = [], []
    for _ in range(TOP_K):
        m = jnp.max(logits, axis=0, keepdims=True)
        sel = jnp.min(jnp.where(logits == m, expert, N_EXPERTS), axis=0, keepdims=True)
        vals.append(m)
        idxs.append(sel)
        logits = jnp.where(expert == sel, -jnp.inf, logits)
    exps = [jnp.exp(v - vals[0]) for v in vals]
    inv = 1.0 / functools.reduce(lambda p, q: p + q, exps)
    chosen = jnp.zeros(logits.shape, F32)
    for k in range(TOP_K):
        chosen = chosen + (expert == idxs[k]).astype(F32)
    earlier = lax.broadcasted_iota(I32, (tm, tm), 0) < lax.broadcasted_iota(I32, (tm, tm), 1)
    before = carry[...] + jnp.dot(chosen.astype(BF16), earlier.astype(BF16), preferred_element_type=F32)
    choice = lax.broadcasted_iota(I32, (8, tm), 0)
    gates = jnp.zeros((8, tm), F32)
    eidx = jnp.zeros((8, tm), I32)
    ranks = jnp.zeros((8, tm), F32)
    for k in range(TOP_K):
        rk = jnp.sum(jnp.where(expert == idxs[k], before, 0.0), axis=0, keepdims=True)
        gates = jnp.where(choice == k, exps[k] * inv, gates)
        eidx = jnp.where(choice == k, idxs[k], eidx)
        ranks = jnp.where(choice == k, rk, ranks)
    gate_ref[...] = gates
    idx_ref[...] = eidx
    rank_ref[...] = ranks.astype(I32)
    carry[...] = carry[...] + jnp.sum(chosen, axis=1, keepdims=True)
    cnt_ref[...] = carry[...].astype(I32)


def _outproj_ln_router(a, b, xa, xb, wa, wb, g, be, rwt, rb):
    t = xa.shape[0] + xb.shape[0]
    tm = TOKEN_TILE
    n_first = xa.shape[0] // tm
    row = lambda i: (i, 0)
    col = lambda i: (0, i)
    fix = lambda i: (0, 0)
    return pl.pallas_call(
        functools.partial(_outproj_body, n_first),
        grid=(t // tm,),
        in_specs=[pl.BlockSpec((tm, 512), row), pl.BlockSpec((tm, 512), row)] + _pair_specs(tm, n_first, D_MODEL)
        + [pl.BlockSpec((512, D_MODEL), fix), pl.BlockSpec((512, D_MODEL), fix),
           pl.BlockSpec((1, D_MODEL), fix), pl.BlockSpec((1, D_MODEL), fix),
           pl.BlockSpec((N_EXPERTS, D_MODEL), fix), pl.BlockSpec((N_EXPERTS, 1), fix)],
        out_specs=[pl.BlockSpec((tm, D_MODEL), row), pl.BlockSpec((tm, 512), row),
                   pl.BlockSpec((8, tm), col), pl.BlockSpec((8, tm), col), pl.BlockSpec((8, tm), col),
                   pl.BlockSpec((N_EXPERTS, 1), fix)],
        out_shape=[jax.ShapeDtypeStruct((t, D_MODEL), F32), jax.ShapeDtypeStruct((t, 512), U32),
                   jax.ShapeDtypeStruct((8, t), F32), jax.ShapeDtypeStruct((8, t), I32),
                   jax.ShapeDtypeStruct((8, t), I32), jax.ShapeDtypeStruct((N_EXPERTS, 1), I32)],
        scratch_shapes=[pltpu.VMEM((N_EXPERTS, 1), F32)],
        compiler_params=_cparams(1),
        name="outproj_ln_router",
    )(a, b, xa, xb, wa, wb, g, be, rwt, rb)


def _sc_worker_rows(n):
    per_worker = n // (SC_CORES * SC_SUBCORES)
    worker = lax.axis_index("s") * SC_CORES + lax.axis_index("c")
    return worker * per_worker, per_worker // SC_ROWS


def _sc_mesh():
    return plsc.VectorSubcoreMesh(core_axis_name="c", subcore_axis_name="s")


def _sc_scratch(w, dtype):
    return ([pltpu.VMEM((SC_ROWS,), I32)] * 2 + [pltpu.VMEM((SC_ROWS, w), dtype)] * 2
            + [pltpu.SemaphoreType.DMA] * 4)


def _sc_chunk_pair(base, p):
    off_a = pl.multiple_of(base + 2 * p * SC_ROWS, 8)
    return off_a, pl.multiple_of(off_a + SC_ROWS, 8)


def _sc_scatter_rows(src, dest, n_out):
    n_src, w = src.shape
    n_dst = dest.shape[1]
    workers = SC_CORES * SC_SUBCORES
    per_worker = n_src // workers
    chunks = per_worker // SC_SCATTER_ROWS
    assert n_src == workers * chunks * SC_SCATTER_ROWS
    idx = dest.T.reshape(n_dst, workers, chunks, SC_SCATTER_ROWS)

    @functools.partial(pl.kernel, mesh=_sc_mesh(), out_type=jax.ShapeDtypeStruct((n_out, w), src.dtype),
                       scratch_types=[pltpu.VMEM((n_dst, chunks, SC_SCATTER_ROWS), I32)]
                       + [pltpu.VMEM((SC_SCATTER_ROWS, w), src.dtype)] * 2 + [pltpu.SemaphoreType.DMA] * 4)
    def scatter(src_hbm, idx_hbm, out_hbm, idx_v, rows_a, rows_b, sem_ra, sem_rb, sem_wa, sem_wb):
        worker = lax.axis_index("s") * SC_CORES + lax.axis_index("c")
        base = worker * per_worker
        for k in range(n_dst):
            pltpu.sync_copy(idx_hbm.at[k, worker], idx_v.at[k])

        def read(c, rows, sem):
            return pltpu.async_copy(src_hbm.at[pl.ds(pl.multiple_of(base + c * SC_SCATTER_ROWS, 8), SC_SCATTER_ROWS)],
                                    rows, sem)

        def write_all(pending_read, c, rows, sem):
            pending_read.wait()
            return [pltpu.async_copy(rows, out_hbm.at[idx_v.at[k, c]], sem) for k in range(n_dst)]

        @pl.loop(0, chunks // 2)
        def _(p):
            read_a = read(2 * p, rows_a, sem_ra)
            read_b = read(2 * p + 1, rows_b, sem_rb)
            writes = write_all(read_a, 2 * p, rows_a, sem_wa) + write_all(read_b, 2 * p + 1, rows_b, sem_wb)
            for wr in writes:
                wr.wait()

        if chunks % 2:
            for wr in write_all(read(chunks - 1, rows_a, sem_ra), chunks - 1, rows_a, sem_wa):
                wr.wait()

    return scatter(src, idx)


def _sc_gather_rows(table, idx):
    n, w = idx.shape[0], table.shape[1]
    assert n % (SC_CORES * SC_SUBCORES * SC_ROWS * 2) == 0

    @functools.partial(pl.kernel, mesh=_sc_mesh(), out_type=jax.ShapeDtypeStruct((n, w), table.dtype),
                       scratch_types=_sc_scratch(w, table.dtype))
    def gather(table_hbm, idx_hbm, out_hbm, idx_a, idx_b, rows_a, rows_b, sem_ra, sem_rb, sem_wa, sem_wb):
        base, chunks = _sc_worker_rows(n)

        @pl.loop(0, chunks // 2)
        def _(p):
            off_a, off_b = _sc_chunk_pair(base, p)
            pltpu.sync_copy(idx_hbm.at[pl.ds(off_a, SC_ROWS)], idx_a)
            read_a = pltpu.async_copy(table_hbm.at[idx_a], rows_a, sem_ra)
            pltpu.sync_copy(idx_hbm.at[pl.ds(off_b, SC_ROWS)], idx_b)
            read_b = pltpu.async_copy(table_hbm.at[idx_b], rows_b, sem_rb)
            read_a.wait()
            write_a = pltpu.async_copy(rows_a, out_hbm.at[pl.ds(off_a, SC_ROWS)], sem_wa)
            read_b.wait()
            write_b = pltpu.async_copy(rows_b, out_hbm.at[pl.ds(off_b, SC_ROWS)], sem_wb)
            write_a.wait()
            write_b.wait()

    return gather(table, idx)


def _experts_body(b0_ref, nb_ref, last_ref, nt_ref, xs_hbm, wg_ref, bg_ref, wu_ref, bu_ref, wd_ref, bd_ref, o_hbm,
                  wg_s, wu_s, wd_s, xbuf, obuf, sem_in, sem_out):
    e = pl.program_id(0)
    first_blk, n_blk, last_valid, n_total = b0_ref[e], nb_ref[e], last_ref[e], nt_ref[0]
    half_rows = MOE_ROWS // 2

    def rows_of(g):
        return pl.ds(pl.multiple_of(g * MOE_ROWS, MOE_ROWS), MOE_ROWS)

    def fetch(g, slot):
        return pltpu.make_async_copy(xs_hbm.at[rows_of(g)], xbuf.at[slot], sem_in.at[slot])

    def put(g, slot):
        return pltpu.make_async_copy(obuf.at[slot], o_hbm.at[rows_of(g)], sem_out.at[slot])

    @pl.when((e == 0) & (n_total > 0))
    def _():
        fetch(0, 0).start()

    @pl.when(n_blk > 0)
    def _():
        wg_s[...] = wg_ref[...].astype(BF16)
        wu_s[...] = wu_ref[...].astype(BF16)
        wd_s[...] = wd_ref[...].astype(BF16)

    def compute(slot, rows):
        half = D_MODEL // 2
        x_hi, x_lo = _unpack_halves(xbuf[slot, 0:rows, :])
        x_hi = x_hi.astype(BF16)
        x_lo = x_lo.astype(BF16)
        g = (jnp.dot(x_hi, wg_s[:half, :], preferred_element_type=F32)
             + jnp.dot(x_lo, wg_s[half:, :], preferred_element_type=F32) + bg_ref[...])
        u = (jnp.dot(x_hi, wu_s[:half, :], preferred_element_type=F32)
             + jnp.dot(x_lo, wu_s[half:, :], preferred_element_type=F32) + bu_ref[...])
        g = jnp.minimum(g, SWIGLU_LIMIT)
        u = jnp.clip(u, -SWIGLU_LIMIT, SWIGLU_LIMIT)
        hmid = (u + 1.0) * (g * jax.nn.sigmoid(SWIGLU_ALPHA * g))
        out = jnp.dot(hmid.astype(BF16), wd_s[...], preferred_element_type=F32) + bd_ref[...]
        obuf[slot, 0:rows, :] = _pack_halves(out)

    def block(j, carry):
        g = first_blk + j
        slot = lax.rem(g, 2)
        fetch(g, slot).wait()

        @pl.when(g + 1 < n_total)
        def _():
            fetch(g + 1, 1 - slot).start()

        @pl.when(g >= 2)
        def _():
            put(g - 2, slot).wait()

        valid = jnp.where(j == n_blk - 1, last_valid, MOE_ROWS)

        @pl.when(valid > half_rows)
        def _():
            compute(slot, MOE_ROWS)

        @pl.when(valid <= half_rows)
        def _():
            compute(slot, half_rows)
            obuf[slot, half_rows:, :] = jnp.zeros((MOE_ROWS - half_rows, obuf.shape[2]), obuf.dtype)

        put(g, slot).start()
        return carry

    lax.fori_loop(0, n_blk, block, 0)

    @pl.when((e == N_EXPERTS - 1) & (n_total >= 2))
    def _():
        put(n_total - 2, lax.rem(n_total, 2)).wait()

    @pl.when((e == N_EXPERTS - 1) & (n_total >= 1))
    def _():
        put(n_total - 1, lax.rem(n_total - 1, 2)).wait()


def _experts(layer, first_blk, n_blk, last_valid, xs, wg, bg, wu, bu, wd, bd):
    n_rows, w = xs.shape
    wsel = lambda e, b0, nb, lv, nt: (layer, e, 0, 0)
    wspec = pl.BlockSpec((None, None, D_MODEL, D_MODEL), wsel)
    bspec = pl.BlockSpec((None, None, 1, D_MODEL), wsel)
    bias = lambda b: b.reshape(b.shape[0], b.shape[1], 1, b.shape[2])
    return pl.pallas_call(
        _experts_body,
        grid_spec=pltpu.PrefetchScalarGridSpec(
            num_scalar_prefetch=4,
            grid=(N_EXPERTS,),
            in_specs=[pl.BlockSpec(memory_space=pl.ANY), wspec, bspec, wspec, bspec, wspec, bspec],
            out_specs=pl.BlockSpec(memory_space=pl.ANY),
            scratch_shapes=[pltpu.VMEM((D_MODEL, D_MODEL), BF16)] * 3
            + [pltpu.VMEM((2, MOE_ROWS, w), U32)] * 2 + [pltpu.SemaphoreType.DMA((2,))] * 2,
        ),
        out_shape=jax.ShapeDtypeStruct((n_rows, w), U32),
        compiler_params=_cparams(1),
        name="experts",
    )(first_blk, n_blk, last_valid, jnp.sum(n_blk).reshape(1), xs, wg, bias(bg), wu, bias(bu), wd, bias(bd))


def _combine_body(n_first, o0_ref, o1_ref, o2_ref, o3_ref, gt_ref, x_ref, g_ref, b_ref, ya_ref, yb_ref=None):
    half = D_MODEL // 2
    gates = gt_ref[...]
    hi = jnp.zeros((x_ref.shape[0], half), F32)
    lo = jnp.zeros((x_ref.shape[0], half), F32)
    for k, o_ref in enumerate((o0_ref, o1_ref, o2_ref, o3_ref)):
        h, l = _unpack_halves(o_ref[...])
        gk = gates[:, k:k + 1]
        hi = hi + gk * h
        lo = lo + gk * l
    x = x_ref[...]
    y_hi = DEEPNORM_ALPHA * x[:, :half] + hi
    y_lo = DEEPNORM_ALPHA * x[:, half:] + lo
    mu = (jnp.sum(y_hi, axis=-1, keepdims=True) + jnp.sum(y_lo, axis=-1, keepdims=True)) * (1.0 / D_MODEL)
    d_hi = y_hi - mu
    d_lo = y_lo - mu
    var = (jnp.sum(d_hi * d_hi, axis=-1, keepdims=True) + jnp.sum(d_lo * d_lo, axis=-1, keepdims=True)) * (1.0 / D_MODEL)
    r = lax.rsqrt(var + LN_EPS)
    out_hi = d_hi * r * g_ref[:, :half] + b_ref[:, :half]
    out_lo = d_lo * r * g_ref[:, half:] + b_ref[:, half:]

    def write(y_ref):
        y_ref[:, :half] = out_hi
        y_ref[:, half:] = out_lo

    if yb_ref is None:
        write(ya_ref)
    else:
        pl.when(pl.program_id(0) < n_first)(lambda: write(ya_ref))
        pl.when(pl.program_id(0) >= n_first)(lambda: write(yb_ref))


def _combine_ln(o4, gates, x, g, b, t_first=None):
    t = x.shape[0]
    tm = TOKEN_TILE
    row = lambda i: (i, 0)
    fix = lambda i: (0, 0)
    choice = lambda k: pl.BlockSpec((tm, 512), lambda i: (k * (t // tm) + i, 0))
    if t_first is None:
        n_first = None
        out_specs = pl.BlockSpec((tm, D_MODEL), row)
        out_shape = jax.ShapeDtypeStruct((t, D_MODEL), F32)
    else:
        n_first = t_first // tm
        out_specs = [pl.BlockSpec((tm, D_MODEL), lambda i: (jnp.minimum(i, n_first - 1), 0)),
                     pl.BlockSpec((tm, D_MODEL), lambda i: (jnp.maximum(i - n_first, 0), 0))]
        out_shape = [jax.ShapeDtypeStruct((t_first, D_MODEL), F32), jax.ShapeDtypeStruct((t - t_first, D_MODEL), F32)]
    return pl.pallas_call(
        functools.partial(_combine_body, n_first),
        grid=(t // tm,),
        in_specs=[choice(0), choice(1), choice(2), choice(3), pl.BlockSpec((tm, TOP_K), row),
                  pl.BlockSpec((tm, D_MODEL), row), pl.BlockSpec((1, D_MODEL), fix), pl.BlockSpec((1, D_MODEL), fix)],
        out_specs=out_specs,
        out_shape=out_shape,
        compiler_params=_cparams(1),
        name="combine_ln",
    )(o4, o4, o4, o4, gates, x, g, b)


def _moe(layer, x1, xp, gates, eidx, rank, counts, ln_g, ln_b, wg, bg, wu, bu, wd, bd, t_first=None):
    t = x1.shape[0]
    bm = MOE_ROWS
    n_blocks = t * TOP_K // bm + N_EXPERTS
    n_rows = n_blocks * bm
    cnt = counts[:, 0]
    padded = (cnt + bm - 1) // bm * bm
    pad_end = jnp.cumsum(padded)
    pad_start = pad_end - padded
    e = eidx[:TOP_K]
    start = jnp.sum(jnp.where(e[:, :, None] == jnp.arange(N_EXPERTS, dtype=I32), pad_start, 0), axis=-1)
    dest = (start + rank[:TOP_K]).T
    n_blk = padded // bm
    last_valid = cnt - (n_blk - 1) * bm
    xs = _sc_scatter_rows(xp, dest, n_rows)
    outs = _experts(layer, pad_start // bm, n_blk, last_valid, xs, wg, bg, wu, bu, wd, bd)
    o4 = _sc_gather_rows(outs, dest.T.reshape(-1))
    return _combine_ln(o4, gates[:TOP_K].T, x1, ln_g, ln_b, t_first)


def _split2(x):
    hi = x.astype(BF16)
    return hi, (x - hi.astype(F32)).astype(BF16)


def _split3(x):
    hi = x.astype(BF16)
    rem = x - hi.astype(F32)
    mid = rem.astype(BF16)
    return hi, mid, (rem - mid.astype(F32)).astype(BF16)


def _chunk_cumsum(g, C):
    rows = g.shape[0]
    r = lax.broadcasted_iota(I32, (rows, rows), 0)
    c = lax.broadcasted_iota(I32, (rows, rows), 1)
    tri = ((r >= c) & (r // C == c // C)).astype(BF16)
    hi, mid, lo = _split3(g)
    dot = lambda part: jnp.dot(tri, part, preferred_element_type=F32)
    return dot(hi) + dot(mid) + dot(lo)


def _gla_batched_step(q, k, v, g, C, n_seq, n_heads, state_of, o_scr):
    rows = n_seq * C
    wide = n_seq * HEAD_W
    mid = max(C // 2 - 1, 0)
    r = lax.broadcasted_iota(I32, (rows, rows), 0)
    c = lax.broadcasted_iota(I32, (rows, rows), 1)
    same = (r // C) == (c // C)
    causal = same & (r >= c)
    parts = _split3(g)
    summed = lambda mask: functools.reduce(lambda p, q_: p + q_, [
        jnp.dot(mask.astype(BF16), part, preferred_element_type=F32) for part in parts])
    b = summed(causal)
    b_mid = summed(same & ((c % C) <= mid))
    b_last = summed(same)
    qe_hi, qe_lo = _split2(q * jnp.exp(b - b_mid))
    ke_hi, ke_lo = _split2(k * jnp.exp(b_mid - b))
    q_state = (q * jnp.exp(b)).astype(BF16)
    k_state = (k * jnp.exp(b_last - b)).astype(BF16)
    decay_parts = _split3(jnp.exp(b_last))
    row_w = lax.broadcasted_iota(I32, (rows, wide), 0)
    blk_w = lax.broadcasted_iota(I32, (rows, wide), 1) // HEAD_W
    own = (row_w // C) == blk_w
    pick = (row_w == blk_w * C).astype(BF16)
    new_states = []
    for h in range(n_heads):
        cs = slice(h * HEAD_W, (h + 1) * HEAD_W)
        lhs = jnp.concatenate([qe_hi[:, cs], qe_hi[:, cs], qe_lo[:, cs]], axis=1)
        rhs = jnp.concatenate([ke_hi[:, cs], ke_lo[:, cs], ke_hi[:, cs]], axis=1)
        scores = jnp.where(causal, _dot_nt(lhs, rhs), 0.0)
        vh = v[:, cs].astype(BF16)
        s_cat = jnp.concatenate([state_of(s, h) for s in range(n_seq)], axis=1)
        o_full = _dot(q_state[:, cs], s_cat)
        o_state = jnp.concatenate([o_full[s * C:(s + 1) * C, s * HEAD_W:(s + 1) * HEAD_W] for s in range(n_seq)],
                                  axis=0)
        o = _dot(scores, vh) + o_state
        v_wide = jnp.where(own, jnp.concatenate([vh] * n_seq, axis=1), jnp.zeros((), BF16))
        kv = _dot_tn(k_state[:, cs], v_wide)
        decay = functools.reduce(lambda p, q_: p + q_, [
            lax.dot_general(part[:, cs], pick, (((0,), (0,)), ((), ())), preferred_element_type=F32)
            for part in decay_parts])
        new_states.append(s_cat * decay + kv)
        ms = jnp.mean(o * o, axis=-1, keepdims=True)
        o_scr[:, cs] = o * lax.rsqrt(ms + RMS_EPS)
    return new_states


def _gla_body(mode, n_seq, n_chunk, C, *refs):
    if mode == "gla":
        hq_ref, hk_ref, hv_ref, hg_ref, hlr_ref, wlr_ref, blr_ref, nw_ref, s0_ref, _, o_ref, so_ref, st, o_scr = refs
    else:
        hq_ref, hk_ref, hv_ref, hg_ref, lb_ref, nw_ref, s0_ref, _, o_ref, so_ref, st, o_scr = refs
    n_heads = 4
    n_keys = s0_ref.shape[2]
    tstep = pl.program_id(1)
    batched = n_chunk == 1 and n_seq > 1

    def padded_state(s, h):
        s_in = s0_ref[s, h]
        if n_keys < HEAD_W:
            s_in = jnp.concatenate([s_in, jnp.zeros((HEAD_W - n_keys, HEAD_W), F32)], axis=0)
        return s_in

    if not batched:
        @pl.when(tstep == 0)
        def _():
            for s in range(n_seq):
                for h in range(n_heads):
                    st[s, h] = padded_state(s, h).T

    if mode == "gla":
        q = hq_ref[...] * (GLA_DK ** -0.5)
        k = hk_ref[...]
        z = _dot(hlr_ref[...], wlr_ref[...]) + blr_ref[...]
        g = _log_sigmoid(z) * (1.0 / GLA_TAU)
    else:
        q = _silu(hq_ref[...]) * (HGRN_DK ** -0.5)
        lb = lb_ref[...]
        f = lb + (1.0 - lb) * jax.nn.sigmoid(hk_ref[...])
        k = 1.0 - f
        g = jnp.log(f)
    v = hv_ref[...]
    if batched:
        new_states = _gla_batched_step(q, k, v, g, C, n_seq, n_heads, padded_state, o_scr)
        for s in range(n_seq):
            for h in range(n_heads):
                so_ref[s, h] = new_states[h][0:n_keys, s * HEAD_W:(s + 1) * HEAD_W]
    else:
        causal = _tri(C)
        mid = max(C // 2 - 1, 0)
        b_all = _chunk_cumsum(g, C)
        for s in range(n_seq):
            states = [st[s, h] for h in range(n_heads)]
            for c in range(n_chunk):
                r0 = (s * n_chunk + c) * C
                rs = slice(r0, r0 + C)
                b, qc, kc = b_all[rs, :], q[rs, :], k[rs, :]
                b_last = b[C - 1:C, :]
                b_mid = b[mid:mid + 1, :]
                qe_hi, qe_lo = _split2(qc * jnp.exp(b - b_mid))
                ke_hi, ke_lo = _split2(kc * jnp.exp(b_mid - b))
                q_state = (qc * jnp.exp(b)).astype(BF16)
                k_state = (kc * jnp.exp(b_last - b)).astype(BF16)
                decay = jnp.exp(b_last)
                for h in range(n_heads):
                    cs = slice(h * HEAD_W, (h + 1) * HEAD_W)
                    lhs = jnp.concatenate([qe_hi[:, cs], qe_hi[:, cs], qe_lo[:, cs]], axis=1)
                    rhs = jnp.concatenate([ke_hi[:, cs], ke_lo[:, cs], ke_hi[:, cs]], axis=1)
                    scores = jnp.where(causal, _dot_nt(lhs, rhs), 0.0)
                    vh = v[rs, cs].astype(BF16)
                    o = _dot(scores, vh) + _dot_nt(q_state[:, cs], states[h])
                    states[h] = states[h] * decay[:, cs] + _dot_tn(vh, k_state[:, cs])
                    ms = jnp.mean(o * o, axis=-1, keepdims=True)
                    o_scr[rs, cs] = o * lax.rsqrt(ms + RMS_EPS)
            for h in range(n_heads):
                st[s, h] = states[h]
    o_ref[...] = (o_scr[...] * nw_ref[...] * _silu(hg_ref[...])).astype(BF16)

    if not batched:
        @pl.when(tstep == pl.num_programs(1) - 1)
        def _():
            for s in range(n_seq):
                for h in range(n_heads):
                    so_ref[s, h] = st[s, h].T[0:n_keys, :]


def _seq_layout(n_batch, seq_len, row_off, sample):
    if sample:
        n_seq, n_chunk, C = SAMPLE_SEQS, 1, seq_len
        rows = n_seq * C
        grid = (n_batch // n_seq, 1)
        blk0 = row_off // rows
        rb = lambda i, t: blk0 + i
    else:
        n_seq, n_chunk, C = 1, PROMPT_TILE // SCAN_CHUNK, SCAN_CHUNK
        rows = PROMPT_TILE
        tiles = seq_len // rows
        grid = (n_batch, tiles)
        blk0 = row_off // rows
        rb = lambda i, t: blk0 + i * tiles + t
    return n_seq, n_chunk, C, rows, grid, rb


def _gla_call(mode, h, cols, extra, nw, s0, out_buf, n_batch, seq_len, row_off, sample):
    n_seq, n_chunk, C, rows, grid, rb = _seq_layout(n_batch, seq_len, row_off, sample)
    colspec = lambda c0, w: pl.BlockSpec((rows, w), lambda i, t: (rb(i, t), c0 // w))
    fix2 = lambda i, t: (0, 0)
    in_specs = [colspec(cols[0], 512), colspec(cols[1], 512), colspec(cols[2], 512), colspec(cols[3], 512)]
    args = [h, h, h, h]
    if mode == "gla":
        wlr, blr = extra
        in_specs += [colspec(cols[4], LANE), pl.BlockSpec((LANE, 512), fix2), pl.BlockSpec((1, 512), fix2)]
        args += [h, wlr, blr]
    else:
        in_specs += [pl.BlockSpec((1, 512), fix2)]
        args += [extra]
    n_keys = s0.shape[2]
    st_spec = pl.BlockSpec((n_seq, 4, n_keys, HEAD_W), lambda i, t: (i, 0, 0, 0))
    in_specs += [pl.BlockSpec((1, 512), fix2), st_spec, pl.BlockSpec(memory_space=pl.ANY)]
    args += [nw, s0, out_buf]
    o_spec = pl.BlockSpec((rows, 512), lambda i, t: (rb(i, t), 0))
    return pl.pallas_call(
        functools.partial(_gla_body, mode, n_seq, n_chunk, C),
        grid=grid,
        in_specs=in_specs,
        out_specs=[o_spec, st_spec],
        out_shape=[jax.ShapeDtypeStruct(out_buf.shape, out_buf.dtype),
                   jax.ShapeDtypeStruct((n_batch, 4, n_keys, HEAD_W), F32)],
        scratch_shapes=[pltpu.VMEM((n_seq, 4, HEAD_W, HEAD_W), F32), pltpu.VMEM((rows, 512), F32)],
        input_output_aliases={len(args) - 1: 0},
        compiler_params=_cparams(2),
        name=mode + ("_sample" if sample else "_prompt"),
    )(*args)


def _round_bf16(x, on=True):
    return x.astype(BF16).astype(F32) if on else x


def _conf_body(n_seq, L, round_x, round_w, a_ref, gt_ref, hist_ref, w_ref, b_ref, g_ref, be_ref, _, o_ref, co_ref,
               buf, bufr, y_scr, win):
    tstep = pl.program_id(1)
    hist = CONF_WIDTH - 1
    pad = 32 - hist

    @pl.when(tstep == 0)
    def _():
        for s in range(n_seq):
            buf[s, pad:32, :] = hist_ref[s]
            bufr[s, pad:32, :] = _round_bf16(hist_ref[s], round_x)

    u = a_ref[...] * jax.nn.sigmoid(gt_ref[...])
    ur = _round_bf16(u, round_x)
    for s in range(n_seq):
        buf[s, 32:32 + L, :] = u[s * L:(s + 1) * L, :]
        bufr[s, 32:32 + L, :] = ur[s * L:(s + 1) * L, :]
    w = _round_bf16(w_ref[...], round_w)
    for s in range(n_seq):
        acc = jnp.zeros((L, CONF_DIM), F32)
        for phase in range(8):
            n_taps = (CONF_WIDTH - 1 - phase) // 8 + 1
            span = L + 8 * (n_taps - 1)
            win[phase, 0:span, :] = bufr[s, pad + phase:pad + phase + span, :]
            for a in range(n_taps):
                j = 8 * a + phase
                acc = acc + win[phase, 8 * a:8 * a + L, :] * w[j:j + 1, :]
        y_scr[s * L:(s + 1) * L, :] = _silu(_layernorm(acc + b_ref[...], g_ref[...], be_ref[...]))
        tail = buf[s, L + pad:L + 32, :]
        buf[s, pad:32, :] = tail
        tailr = bufr[s, L + pad:L + 32, :]
        bufr[s, pad:32, :] = tailr
    o_ref[...] = y_scr[...].astype(o_ref.dtype)

    @pl.when(tstep == pl.num_programs(1) - 1)
    def _():
        for s in range(n_seq):
            co_ref[s] = buf[s, pad:32, :]


def _conf_call(h, col_a, col_g, cache, w, b, g, be, out_buf, n_batch, seq_len, row_off, sample):
    n_seq, n_chunk, C, rows, grid, rb = _seq_layout(n_batch, seq_len, row_off, sample)
    L = rows // n_seq
    hist = CONF_WIDTH - 1
    colspec = lambda c0: pl.BlockSpec((rows, 512), lambda i, t: (rb(i, t), c0 // 512))
    fix2 = lambda i, t: (0, 0)
    c_spec = pl.BlockSpec((n_seq, hist, CONF_DIM), lambda i, t: (i, 0, 0))
    return pl.pallas_call(
        functools.partial(_conf_body, n_seq, L, True, sample),
        grid=grid,
        in_specs=[colspec(col_a), colspec(col_g), c_spec,
                  pl.BlockSpec((CONF_WIDTH, CONF_DIM), fix2), pl.BlockSpec((1, CONF_DIM), fix2),
                  pl.BlockSpec((1, CONF_DIM), fix2), pl.BlockSpec((1, CONF_DIM), fix2),
                  pl.BlockSpec(memory_space=pl.ANY)],
        out_specs=[pl.BlockSpec((rows, 512), lambda i, t: (rb(i, t), 0)), c_spec],
        out_shape=[jax.ShapeDtypeStruct(out_buf.shape, out_buf.dtype),
                   jax.ShapeDtypeStruct((n_batch, hist, CONF_DIM), F32)],
        scratch_shapes=[pltpu.VMEM((n_seq, 32 + L, CONF_DIM), F32)] * 2 + [pltpu.VMEM((rows, CONF_DIM), F32),
                                                                           pltpu.VMEM((8, L + 24, CONF_DIM), F32)],
        input_output_aliases={7: 0},
        compiler_params=_cparams(2),
        name="conformer" + ("_sample" if sample else "_prompt"),
    )(h, h, cache, w, b, g, be, out_buf)


def _ssd_body(n_seq, n_chunk, C, round_x, round_w, hz_ref, hx_ref, hdt_ref, hist_ref, s0_ref, cw_ref, cb_ref, dtb_ref, alog_ref,
              dvec_ref, nw_ref, _, o_ref, co_ref, so_ref, st, buf, bufr, xbc, y_scr):
    tstep = pl.program_id(1)
    L = n_chunk * C
    hist = SSM_CONV - 1
    pad = 8 - hist
    n_pairs = SSM_HEADS // 2

    @pl.when(tstep == 0)
    def _():
        for s in range(n_seq):
            buf[s, pad:8, :] = hist_ref[s]
            bufr[s, pad:8, :] = _round_bf16(hist_ref[s], round_x)
            for m in range(n_pairs):
                st[s, m] = s0_ref[s, m]

    cw = _round_bf16(cw_ref[...], round_w)
    for s in range(n_seq):
        hx = hx_ref[s * L:(s + 1) * L, :]
        buf[s, 8:8 + L, :] = hx
        bufr[s, 8:8 + L, :] = _round_bf16(hx, round_x)
        acc = jnp.zeros((L, SSM_CONV_DIM), F32)
        for j in range(SSM_CONV):
            acc = acc + bufr[s, pad + j:pad + j + L, :] * cw[j:j + 1, :]
        xbc[s * L:(s + 1) * L, :] = _silu(acc + cb_ref[...])
        tail = buf[s, L + pad:L + 8, :]
        buf[s, pad:8, :] = tail
        tailr = bufr[s, L + pad:L + 8, :]
        bufr[s, pad:8, :] = tailr

    dt = _softplus(hdt_ref[...] + dtb_ref[...])
    la = dt * (-jnp.exp(alog_ref[...]))
    hrow = lax.broadcasted_iota(I32, (LANE, SSM_INNER), 0)
    hcol = lax.broadcasted_iota(I32, (LANE, SSM_INNER), 1) // SSM_HEADDIM
    expand = (hrow == hcol).astype(BF16)
    dtx = functools.reduce(lambda p, q: p + q,
                           [jnp.dot(part, expand, preferred_element_type=F32) for part in _split3(dt)])
    causal = _tri(C)
    tri = causal.astype(BF16)
    lane = lax.broadcasted_iota(I32, (C, HEAD_W), 1)
    bcol_all = _chunk_cumsum(la, C)
    heads_per_group = SSM_HEADS // SSM_GROUPS
    for s in range(n_seq):
        states = [st[s, m] for m in range(n_pairs)]
        for c in range(n_chunk):
            r0 = (s * n_chunk + c) * C
            rs = slice(r0, r0 + C)
            bcol = bcol_all[rs, :]
            brow = functools.reduce(lambda p, q: p + q, [
                lax.dot_general(part, tri, (((0,), (1,)), ((), ())), preferred_element_type=F32)
                for part in _split3(la[rs, :])])
            xs_c = xbc[rs, 0:SSM_INNER]
            v_c = (xs_c * dtx[rs, :]).astype(BF16)
            gmats, bms, cms = [], [], []
            for grp in range(SSM_GROUPS):
                bm = xbc[rs, SSM_INNER + grp * SSM_STATE:SSM_INNER + (grp + 1) * SSM_STATE]
                cm = xbc[rs, SSM_INNER + (SSM_GROUPS + grp) * SSM_STATE:SSM_INNER + (SSM_GROUPS + grp + 1) * SSM_STATE]
                cm_hi, cm_lo = _split2(cm)
                bm_hi, bm_lo = _split2(bm)
                gmats.append(_dot_nt(jnp.concatenate([cm_hi, cm_hi, cm_lo], axis=1),
                                     jnp.concatenate([bm_hi, bm_lo, bm_hi], axis=1)))
                bms.append(bm)
                cms.append(cm)
            for m in range(n_pairs):
                grp = (2 * m) // heads_per_group
                bm, cm, gmat = bms[grp], cms[grp], gmats[grp]
                ps = slice(m * HEAD_W, (m + 1) * HEAD_W)
                vp = v_c[:, ps]
                s_t = states[m]
                o_halves, new_rows = [], []
                for hh in range(2):
                    hd = 2 * m + hh
                    bc = bcol[:, hd:hd + 1]
                    br = brow[hd:hd + 1, :]
                    dec = jnp.where(causal, jnp.exp(jnp.minimum(bc - br, 0.0)), 0.0)
                    b_last = bcol[C - 1:C, hd:hd + 1]
                    o_halves.append(_dot(gmat * dec, vp) + _dot_nt(cm * jnp.exp(bc), s_t))
                    kv = _dot_tn(vp, bm * jnp.exp(b_last - bc))
                    vs = slice(hh * SSM_HEADDIM, (hh + 1) * SSM_HEADDIM)
                    new_rows.append(s_t[vs, :] * jnp.exp(b_last) + kv[vs, :])
                states[m] = jnp.concatenate(new_rows, axis=0)
                o_pair = jnp.where(lane < SSM_HEADDIM, o_halves[0], o_halves[1])
                y_scr[rs, ps] = o_pair + dvec_ref[:, ps] * xs_c[:, ps]
        for m in range(n_pairs):
            st[s, m] = states[m]
    y = y_scr[...] * _silu(hz_ref[...])
    gw = SSM_INNER // SSM_GROUPS
    for grp in range(SSM_GROUPS):
        gs = slice(grp * gw, (grp + 1) * gw)
        yg = y[:, gs]
        ms = jnp.mean(yg * yg, axis=-1, keepdims=True)
        o_ref[:, gs] = (yg * lax.rsqrt(ms + RMS_EPS) * nw_ref[:, gs]).astype(BF16)

    @pl.when(tstep == pl.num_programs(1) - 1)
    def _():
        for s in range(n_seq):
            co_ref[s] = buf[s, pad:8, :]
            for m in range(n_pairs):
                so_ref[s, m] = st[s, m]


def _ssd_call(h, col_z, col_x, col_dt, cache, s0, cw, cb, dtb, alog, dvec, nw, out_buf, n_batch, seq_len, row_off,
              sample):
    n_seq, n_chunk, C, rows, grid, rb = _seq_layout(n_batch, seq_len, row_off, sample)
    L = rows // n_seq
    hist = SSM_CONV - 1
    n_pairs = SSM_HEADS // 2
    colspec = lambda c0, w: pl.BlockSpec((rows, w), lambda i, t: (rb(i, t), c0 // w))
    fix2 = lambda i, t: (0, 0)
    c_spec = pl.BlockSpec((n_seq, hist, SSM_CONV_DIM), lambda i, t: (i, 0, 0))
    st_spec = pl.BlockSpec((n_seq, n_pairs, HEAD_W, SSM_STATE), lambda i, t: (i, 0, 0, 0))
    return pl.pallas_call(
        functools.partial(_ssd_body, n_seq, n_chunk, C, sample, True),
        grid=grid,
        in_specs=[colspec(col_z, 512), colspec(col_x, SSM_CONV_DIM), colspec(col_dt, LANE), c_spec, st_spec,
                  pl.BlockSpec((SSM_CONV, SSM_CONV_DIM), fix2), pl.BlockSpec((1, SSM_CONV_DIM), fix2),
                  pl.BlockSpec((1, LANE), fix2), pl.BlockSpec((1, LANE), fix2),
                  pl.BlockSpec((1, SSM_INNER), fix2), pl.BlockSpec((1, SSM_INNER), fix2),
                  pl.BlockSpec(memory_space=pl.ANY)],
        out_specs=[pl.BlockSpec((rows, 512), lambda i, t: (rb(i, t), 0)), c_spec, st_spec],
        out_shape=[jax.ShapeDtypeStruct(out_buf.shape, out_buf.dtype),
                   jax.ShapeDtypeStruct((n_batch, hist, SSM_CONV_DIM), F32),
                   jax.ShapeDtypeStruct((n_batch, n_pairs, HEAD_W, SSM_STATE), F32)],
        scratch_shapes=[pltpu.VMEM((n_seq, n_pairs, HEAD_W, SSM_STATE), F32),
                        pltpu.VMEM((n_seq, 8 + L, SSM_CONV_DIM), F32),
                        pltpu.VMEM((n_seq, 8 + L, SSM_CONV_DIM), F32),
                        pltpu.VMEM((rows, SSM_CONV_DIM), F32),
                        pltpu.VMEM((rows, SSM_INNER), F32)],
        input_output_aliases={11: 0},
        compiler_params=_cparams(2),
        name="ssd" + ("_sample" if sample else "_prompt"),
    )(h, h, h, cache, s0, cw, cb, dtb, alog, dvec, nw, out_buf)


def _pad_heads(w, n_heads, width):
    lead = w.shape[:-1]
    w = w.reshape(lead + (n_heads, width))
    w = jnp.pad(w, [(0, 0)] * len(lead) + [(0, 0), (0, HEAD_W - width)])
    return w.reshape(lead + (n_heads * HEAD_W,))


def _row(v):
    return v.reshape(1, -1).astype(F32)


def kernel(x_prompt, x_sample, state_gla, cache_conformer, state_hgrn, state_ssm, cache_mamba_conv, w_in_even, w_gla_gate_lr, b_gla_gate, gla_norm_w, conf_conv_w, conf_conv_b, conf_ln_g, conf_ln_b, w_out_even, w_in_odd, hgrn_lower_bounds, hgrn_norm_w, mamba_conv_w, mamba_conv_b, mamba_dt_bias, mamba_a_log, mamba_d, mamba_norm_w, w_out_odd, ln1_g, ln1_b, ln2_g, ln2_b, router_w, router_b, expert_w_gate, expert_b_gate, expert_w_up, expert_b_up, expert_w_down, expert_b_down):
    bp, lp, _ = x_prompt.shape
    bs, ls, _ = x_sample.shape
    tp, ts = bp * lp, bs * ls
    x = (x_prompt.reshape(tp, D_MODEL), x_sample.reshape(ts, D_MODEL))

    def router_params(layer):
        return router_w[layer].T.astype(BF16), router_b[layer].astype(F32).reshape(N_EXPERTS, 1)

    def finish_layer(layer, x, mix_a, mix_b, w_out):
        rwt, rb = router_params(layer)
        x1, xp, gates, eidx, rank, counts = _outproj_ln_router(
            mix_a, mix_b, x[0], x[1], w_out[:512].astype(BF16), w_out[512:].astype(BF16),
            _row(ln1_g[layer]), _row(ln1_b[layer]), rwt, rb)
        return _moe(layer, x1, xp, gates, eidx, rank, counts, _row(ln2_g[layer]), _row(ln2_b[layer]),
                    expert_w_gate, expert_b_gate, expert_w_up, expert_b_up, expert_w_down, expert_b_down, tp)

    def mix_buffer():
        return jnp.zeros((tp + ts, 512), BF16)

    wi = w_in_even[0]
    wq, wk, wv, wg, wlr, wglu = jnp.split(wi, [256, 512, 1024, 1536, 1552], axis=1)
    w_even = jnp.concatenate([_pad_heads(wq, GLA_HEADS, GLA_DK), _pad_heads(wk, GLA_HEADS, GLA_DK), wv, wg, wglu,
                              jnp.pad(wlr, ((0, 0), (0, LANE - GLA_RANK)))], axis=1).astype(BF16)
    cols_gla = (0, 512, 1024, 1536, 3072)
    col_a, col_gate = 2048, 2560
    h = _inproj(x[0], x[1], w_even)
    wlr_p = jnp.pad(_pad_heads(w_gla_gate_lr[0], GLA_HEADS, GLA_DK), ((0, LANE - GLA_RANK), (0, 0)))
    blr_p = _row(_pad_heads(b_gla_gate[0], GLA_HEADS, GLA_DK))
    nw = _row(gla_norm_w[0])
    conf_args = (conf_conv_w[0], _row(conf_conv_b[0]), _row(conf_ln_g[0]), _row(conf_ln_b[0]))
    s0_p = jnp.zeros((bp, GLA_HEADS, GLA_DK, HEAD_W), F32)
    s0_s = state_gla[0]
    mix_a, sg_p = _gla_call("gla", h, cols_gla, (wlr_p, blr_p), nw, s0_p, mix_buffer(), bp, lp, 0, False)
    mix_a, sg_s = _gla_call("gla", h, cols_gla, (wlr_p, blr_p), nw, s0_s, mix_a, bs, ls, tp, True)
    mix_b, cc_p = _conf_call(h, col_a, col_gate, jnp.zeros((bp,) + cache_conformer.shape[2:], F32), *conf_args,
                             mix_buffer(), bp, lp, 0, False)
    mix_b, cc_s = _conf_call(h, col_a, col_gate, cache_conformer[0], *conf_args, mix_b, bs, ls, tp, True)
    x = finish_layer(0, x, mix_a, mix_b, w_out_even[0])
    gla_p, gla_s = sg_p[None], sg_s[None]
    conf_p, conf_s = cc_p[None], cc_s[None]

    lb_cum = jnp.cumsum(jax.nn.softmax(hgrn_lower_bounds.astype(F32), axis=0), axis=0)
    lower_bound = _row((lb_cum - lb_cum[0])[1])
    wo = w_in_odd[0]
    w_odd = jnp.concatenate([wo[:, 2560:3584], wo[:, :2560],
                             jnp.pad(wo[:, 3584:], ((0, 0), (0, LANE - SSM_HEADS)))], axis=1).astype(BF16)
    h = _inproj(x[0], x[1], w_odd)
    cols_hgrn = (1024, 1536, 2048, 2560)
    col_z, col_x, col_dt = 3072, 0, 3584
    nw = _row(hgrn_norm_w[0])
    mix_a, sh_p = _gla_call("hgrn", h, cols_hgrn, lower_bound, nw,
                            jnp.zeros((bp, HGRN_HEADS, HEAD_W, HEAD_W), F32), mix_buffer(), bp, lp, 0, False)
    mix_a, sh_s = _gla_call("hgrn", h, cols_hgrn, lower_bound, nw, state_hgrn[0], mix_a, bs, ls, tp, True)

    def pair_states(s):
        return jnp.swapaxes(s, 2, 3).reshape(s.shape[0], SSM_HEADS // 2, HEAD_W, SSM_STATE)

    def unpair_states(s):
        return jnp.swapaxes(s.reshape(s.shape[0], SSM_HEADS, SSM_HEADDIM, SSM_STATE), 2, 3)

    pad8 = lambda v: jnp.pad(v.astype(F32), (0, LANE - SSM_HEADS)).reshape(1, LANE)
    ssd_args = (mamba_conv_w[0], _row(mamba_conv_b[0]), pad8(mamba_dt_bias[0]), pad8(mamba_a_log[0]),
                _row(jnp.repeat(mamba_d[0], SSM_HEADDIM)), _row(mamba_norm_w[0]))
    mix_b, cm_p, ss_p = _ssd_call(h, col_z, col_x, col_dt, jnp.zeros((bp,) + cache_mamba_conv.shape[2:], F32),
                                  jnp.zeros((bp, SSM_HEADS // 2, HEAD_W, SSM_STATE), F32), *ssd_args,
                                  mix_buffer(), bp, lp, 0, False)
    mix_b, cm_s, ss_s = _ssd_call(h, col_z, col_x, col_dt, cache_mamba_conv[0], pair_states(state_ssm[0]),
                                  *ssd_args, mix_b, bs, ls, tp, True)
    y_prompt, y_sample = finish_layer(1, x, mix_a, mix_b, w_out_odd[0])
    y_prompt = y_prompt.reshape(bp, lp, D_MODEL)
    y_sample = y_sample.reshape(bs, ls, D_MODEL)
    return (y_prompt, y_sample, gla_p, gla_s, conf_p, conf_s, sh_p[None], sh_s[None],
            unpair_states(ss_p)[None], unpair_states(ss_s)[None], cm_p[None], cm_s[None])
```

```python
import functools

import jax
import jax.numpy as jnp
from jax import lax
from jax.experimental import pallas as pl
from jax.experimental.pallas import tpu as pltpu
from jax.experimental.pallas import tpu_sc as plsc

F32 = jnp.float32
BF16 = jnp.bfloat16
I32 = jnp.int32
U32 = jnp.uint32
HIGHEST = lax.Precision.HIGHEST

D_MODEL = 1024
DEPTH = 2
DEEPNORM_ALPHA = (2.0 * DEPTH) ** 0.25
LN_EPS = 1e-5
RMS_EPS = 1e-6
LANE = 128
HEAD_W = 128
GLA_HEADS, GLA_DK, GLA_RANK, GLA_TAU = 4, 64, 16, 16.0
CONF_DIM, CONF_WIDTH = 512, 31
HGRN_HEADS, HGRN_DK = 4, 128
SSM_HEADS, SSM_HEADDIM, SSM_STATE, SSM_GROUPS, SSM_CONV = 8, 64, 128, 2, 4
SSM_INNER = SSM_HEADS * SSM_HEADDIM
SSM_CONV_DIM = SSM_INNER + 2 * SSM_GROUPS * SSM_STATE
N_EXPERTS, TOP_K = 32, 4
SWIGLU_ALPHA, SWIGLU_LIMIT = 1.702, 7.0
SCAN_CHUNK = 64
PROMPT_TILE = 512
SAMPLE_SEQS = 16
TOKEN_TILE = 512
MOE_ROWS = 512
SC_CORES, SC_SUBCORES = 2, 16
SC_ROWS = 64
SC_SCATTER_ROWS = 32
VMEM_LIMIT = 56 * 1024 * 1024


def _cparams(n_axes):
    return pltpu.CompilerParams(dimension_semantics=("arbitrary",) * n_axes, vmem_limit_bytes=VMEM_LIMIT)


def _silu(x):
    return x * jax.nn.sigmoid(x)


def _softplus(x):
    return jnp.maximum(x, 0.0) + jnp.log(1.0 + jnp.exp(-jnp.abs(x)))


def _log_sigmoid(x):
    return jnp.minimum(x, 0.0) - jnp.log(1.0 + jnp.exp(-jnp.abs(x)))


def _layernorm(y, g, b):
    mu = jnp.mean(y, axis=-1, keepdims=True)
    d = y - mu
    var = jnp.mean(d * d, axis=-1, keepdims=True)
    return d * lax.rsqrt(var + LN_EPS) * g + b


def _dot(a, b):
    return jnp.dot(a.astype(BF16), b.astype(BF16), preferred_element_type=F32)


def _dot_nt(a, b):
    return lax.dot_general(a.astype(BF16), b.astype(BF16), (((1,), (1,)), ((), ())), preferred_element_type=F32)


def _dot_tn(a, b):
    return lax.dot_general(a.astype(BF16), b.astype(BF16), (((0,), (0,)), ((), ())), preferred_element_type=F32)


def _tri(c):
    r = lax.broadcasted_iota(I32, (c, c), 0)
    k = lax.broadcasted_iota(I32, (c, c), 1)
    return r >= k


def _pair_specs(tm, n_first, width):
    return [pl.BlockSpec((tm, width), lambda i: (jnp.minimum(i, n_first - 1), 0)),
            pl.BlockSpec((tm, width), lambda i: (jnp.maximum(i - n_first, 0), 0))]


def _pair_tile(n_first, xa_ref, xb_ref):
    return jnp.where(pl.program_id(0) < n_first, xa_ref[...], xb_ref[...])


def _inproj_body(n_first, xa_ref, xb_ref, w_ref, o_ref):
    xb = _pair_tile(n_first, xa_ref, xb_ref).astype(BF16)
    n = w_ref.shape[1]
    for c0 in range(0, n, 512):
        c1 = min(c0 + 512, n)
        o_ref[:, c0:c1] = jnp.dot(xb, w_ref[:, c0:c1], preferred_element_type=F32)


def _inproj(xa, xb, w):
    k, n = w.shape
    t = xa.shape[0] + xb.shape[0]
    n_first = xa.shape[0] // TOKEN_TILE
    return pl.pallas_call(
        functools.partial(_inproj_body, n_first),
        grid=(t // TOKEN_TILE,),
        in_specs=_pair_specs(TOKEN_TILE, n_first, k) + [pl.BlockSpec((k, n), lambda i: (0, 0))],
        out_specs=pl.BlockSpec((TOKEN_TILE, n), lambda i: (i, 0)),
        out_shape=jax.ShapeDtypeStruct((t, n), F32),
        compiler_params=_cparams(1),
        name="inproj",
    )(xa, xb, w)


def _pack_halves(y):
    half = y.shape[1] // 2
    hi = lax.bitcast_convert_type(y[:, :half].astype(BF16).astype(F32), U32)
    lo = lax.bitcast_convert_type(y[:, half:].astype(BF16).astype(F32), U32)
    return (hi & jnp.uint32(0xFFFF0000)) | (lo >> 16)


def _unpack_halves(w):
    hi = lax.bitcast_convert_type(w & jnp.uint32(0xFFFF0000), F32)
    lo = lax.bitcast_convert_type(w << 16, F32)
    return hi, lo


def _outproj_body(n_first, a_ref, b_ref, xa_ref, xb_ref, wa_ref, wb_ref, g_ref, be_ref, rwt_ref, rb_ref,
                  x1_ref, xp_ref, gate_ref, idx_ref, rank_ref, cnt_ref, carry):
    @pl.when(pl.program_id(0) == 0)
    def _():
        carry[...] = jnp.zeros(carry.shape, F32)

    mix = (jnp.dot(a_ref[...], wa_ref[...], preferred_element_type=F32)
           + jnp.dot(b_ref[...], wb_ref[...], preferred_element_type=F32))
    x1 = _layernorm(DEEPNORM_ALPHA * _pair_tile(n_first, xa_ref, xb_ref) + mix, g_ref[...], be_ref[...])
    x1_ref[...] = x1
    xp_ref[...] = _pack_halves(x1)
    logits = _dot_nt(rwt_ref[...], x1) + rb_ref[...]
    tm = logits.shape[1]
    expert = lax.broadcasted_iota(I32, logits.shape, 0)
    vals, idxs = [], []
    for _ in range(TOP_K):
        m = jnp.max(logits, axis=0, keepdims=True)
        sel = jnp.min(jnp.where(logits == m, expert, N_EXPERTS), axis=0, keepdims=True)
        vals.append(m)
        idxs.append(sel)
        logits = jnp.where(expert == sel, -jnp.inf, logits)
    exps = [jnp.exp(v - vals[0]) for v in vals]
    inv = 1.0 / functools.reduce(lambda p, q: p + q, exps)
    chosen = jnp.zeros(logits.shape, F32)
    for k in range(TOP_K):
        chosen = chosen + (expert == idxs[k]).astype(F32)
    earlier = lax.broadcasted_iota(I32, (tm, tm), 0) < lax.broadcasted_iota(I32, (tm, tm), 1)
    before = carry[...] + jnp.dot(chosen.astype(BF16), earlier.astype(BF16), preferred_element_type=F32)
    choice = lax.broadcasted_iota(I32, (8, tm), 0)
    gates = jnp.zeros((8, tm), F32)
    eidx = jnp.zeros((8, tm), I32)
    ranks = jnp.zeros((8, tm), F32)
    for k in range(TOP_K):
        rk = jnp.sum(jnp.where(expert == idxs[k], before, 0.0), axis=0, keepdims=True)
        gates = jnp.where(choice == k, exps[k] * inv, gates)
        eidx = jnp.where(choice == k, idxs[k], eidx)
        ranks = jnp.where(choice == k, rk, ranks)
    gate_ref[...] = gates
    idx_ref[...] = eidx
    rank_ref[...] = ranks.astype(I32)
    carry[...] = carry[...] + jnp.sum(chosen, axis=1, keepdims=True)
    cnt_ref[...] = carry[...].astype(I32)


def _outproj_ln_router(a, b, xa, xb, wa, wb, g, be, rwt, rb):
    t = xa.shape[0] + xb.shape[0]
    tm = TOKEN_TILE
    n_first = xa.shape[0] // tm
    row = lambda i: (i, 0)
    col = lambda i: (0, i)
    fix = lambda i: (0, 0)
    return pl.pallas_call(
        functools.partial(_outproj_body, n_first),
        grid=(t // tm,),
        in_specs=[pl.BlockSpec((tm, 512), row), pl.BlockSpec((tm, 512), row)] + _pair_specs(tm, n_first, D_MODEL)
        + [pl.BlockSpec((512, D_MODEL), fix), pl.BlockSpec((512, D_MODEL), fix),
           pl.BlockSpec((1, D_MODEL), fix), pl.BlockSpec((1, D_MODEL), fix),
           pl.BlockSpec((N_EXPERTS, D_MODEL), fix), pl.BlockSpec((N_EXPERTS, 1), fix)],
        out_specs=[pl.BlockSpec((tm, D_MODEL), row), pl.BlockSpec((tm, 512), row),
                   pl.BlockSpec((8, tm), col), pl.BlockSpec((8, tm), col), pl.BlockSpec((8, tm), col),
                   pl.BlockSpec((N_EXPERTS, 1), fix)],
        out_shape=[jax.ShapeDtypeStruct((t, D_MODEL), F32), jax.ShapeDtypeStruct((t, 512), U32),
                   jax.ShapeDtypeStruct((8, t), F32), jax.ShapeDtypeStruct((8, t), I32),
                   jax.ShapeDtypeStruct((8, t), I32), jax.ShapeDtypeStruct((N_EXPERTS, 1), I32)],
        scratch_shapes=[pltpu.VMEM((N_EXPERTS, 1), F32)],
        compiler_params=_cparams(1),
        name="outproj_ln_router",
    )(a, b, xa, xb, wa, wb, g, be, rwt, rb)


def _sc_worker_rows(n):
    per_worker = n // (SC_CORES * SC_SUBCORES)
    worker = lax.axis_index("s") * SC_CORES + lax.axis_index("c")
    return worker * per_worker, per_worker // SC_ROWS


def _sc_mesh():
    return plsc.VectorSubcoreMesh(core_axis_name="c", subcore_axis_name="s")


def _sc_scratch(w, dtype):
    return ([pltpu.VMEM((SC_ROWS,), I32)] * 2 + [pltpu.VMEM((SC_ROWS, w), dtype)] * 2
            + [pltpu.SemaphoreType.DMA] * 4)


def _sc_chunk_pair(base, p):
    off_a = pl.multiple_of(base + 2 * p * SC_ROWS, 8)
    return off_a, pl.multiple_of(off_a + SC_ROWS, 8)


def _sc_scatter_rows(src, dest, n_out):
    n_src, w = src.shape
    n_dst = dest.shape[1]
    workers = SC_CORES * SC_SUBCORES
    per_worker = n_src // workers
    chunks = per_worker // SC_SCATTER_ROWS
    assert n_src == workers * chunks * SC_SCATTER_ROWS
    idx = dest.T.reshape(n_dst, workers, chunks, SC_SCATTER_ROWS)

    @functools.partial(pl.kernel, mesh=_sc_mesh(), out_type=jax.ShapeDtypeStruct((n_out, w), src.dtype),
                       scratch_types=[pltpu.VMEM((n_dst, chunks, SC_SCATTER_ROWS), I32)]
                       + [pltpu.VMEM((SC_SCATTER_ROWS, w), src.dtype)] * 2 + [pltpu.SemaphoreType.DMA] * 4)
    def scatter(src_hbm, idx_hbm, out_hbm, idx_v, rows_a, rows_b, sem_ra, sem_rb, sem_wa, sem_wb):
        worker = lax.axis_index("s") * SC_CORES + lax.axis_index("c")
        base = worker * per_worker
        for k in range(n_dst):
            pltpu.sync_copy(idx_hbm.at[k, worker], idx_v.at[k])

        def read(c, rows, sem):
            return pltpu.async_copy(src_hbm.at[pl.ds(pl.multiple_of(base + c * SC_SCATTER_ROWS, 8), SC_SCATTER_ROWS)],
                                    rows, sem)

        def write_all(pending_read, c, rows, sem):
            pending_read.wait()
            return [pltpu.async_copy(rows, out_hbm.at[idx_v.at[k, c]], sem) for k in range(n_dst)]

        @pl.loop(0, chunks // 2)
        def _(p):
            read_a = read(2 * p, rows_a, sem_ra)
            read_b = read(2 * p + 1, rows_b, sem_rb)
            writes = write_all(read_a, 2 * p, rows_a, sem_wa) + write_all(read_b, 2 * p + 1, rows_b, sem_wb)
            for wr in writes:
                wr.wait()

        if chunks % 2:
            for wr in write_all(read(chunks - 1, rows_a, sem_ra), chunks - 1, rows_a, sem_wa):
                wr.wait()

    return scatter(src, idx)


def _sc_gather_rows(table, idx):
    n, w = idx.shape[0], table.shape[1]
    assert n % (SC_CORES * SC_SUBCORES * SC_ROWS * 2) == 0

    @functools.partial(pl.kernel, mesh=_sc_mesh(), out_type=jax.ShapeDtypeStruct((n, w), table.dtype),
                       scratch_types=_sc_scratch(w, table.dtype))
    def gather(table_hbm, idx_hbm, out_hbm, idx_a, idx_b, rows_a, rows_b, sem_ra, sem_rb, sem_wa, sem_wb):
        base, chunks = _sc_worker_rows(n)

        @pl.loop(0, chunks // 2)
        def _(p):
            off_a, off_b = _sc_chunk_pair(base, p)
            pltpu.sync_copy(idx_hbm.at[pl.ds(off_a, SC_ROWS)], idx_a)
            read_a = pltpu.async_copy(table_hbm.at[idx_a], rows_a, sem_ra)
            pltpu.sync_copy(idx_hbm.at[pl.ds(off_b, SC_ROWS)], idx_b)
            read_b = pltpu.async_copy(table_hbm.at[idx_b], rows_b, sem_rb)
            read_a.wait()
            write_a = pltpu.async_copy(rows_a, out_hbm.at[pl.ds(off_a, SC_ROWS)], sem_wa)
            read_b.wait()
            write_b = pltpu.async_copy(rows_b, out_hbm.at[pl.ds(off_b, SC_ROWS)], sem_wb)
            write_a.wait()
            write_b.wait()

    return gather(table, idx)


def _experts_body(b0_ref, nb_ref, last_ref, nt_ref, xs_hbm, wg_ref, bg_ref, wu_ref, bu_ref, wd_ref, bd_ref, o_hbm,
                  wg_s, wu_s, wd_s, xbuf, obuf, sem_in, sem_out):
    e = pl.program_id(0)
    first_blk, n_blk, last_valid, n_total = b0_ref[e], nb_ref[e], last_ref[e], nt_ref[0]
    half_rows = MOE_ROWS // 2

    def rows_of(g):
        return pl.ds(pl.multiple_of(g * MOE_ROWS, MOE_ROWS), MOE_ROWS)

    def fetch(g, slot):
        return pltpu.make_async_copy(xs_hbm.at[rows_of(g)], xbuf.at[slot], sem_in.at[slot])

    def put(g, slot):
        return pltpu.make_async_copy(obuf.at[slot], o_hbm.at[rows_of(g)], sem_out.at[slot])

    @pl.when((e == 0) & (n_total > 0))
    def _():
        fetch(0, 0).start()

    @pl.when(n_blk > 0)
    def _():
        wg_s[...] = wg_ref[...].astype(BF16)
        wu_s[...] = wu_ref[...].astype(BF16)
        wd_s[...] = wd_ref[...].astype(BF16)

    def compute(slot, rows):
        half = D_MODEL // 2
        x_hi, x_lo = _unpack_halves(xbuf[slot, 0:rows, :])
        x_hi = x_hi.astype(BF16)
        x_lo = x_lo.astype(BF16)
        g = (jnp.dot(x_hi, wg_s[:half, :], preferred_element_type=F32)
             + jnp.dot(x_lo, wg_s[half:, :], preferred_element_type=F32) + bg_ref[...])
        u = (jnp.dot(x_hi, wu_s[:half, :], preferred_element_type=F32)
             + jnp.dot(x_lo, wu_s[half:, :], preferred_element_type=F32) + bu_ref[...])
        g = jnp.minimum(g, SWIGLU_LIMIT)
        u = jnp.clip(u, -SWIGLU_LIMIT, SWIGLU_LIMIT)
        hmid = (u + 1.0) * (g * jax.nn.sigmoid(SWIGLU_ALPHA * g))
        out = jnp.dot(hmid.astype(BF16), wd_s[...], preferred_element_type=F32) + bd_ref[...]
        obuf[slot, 0:rows, :] = _pack_halves(out)

    def block(j, carry):
        g = first_blk + j
        slot = lax.rem(g, 2)
        fetch(g, slot).wait()

        @pl.when(g + 1 < n_total)
        def _():
            fetch(g + 1, 1 - slot).start()

        @pl.when(g >= 2)
        def _():
            put(g - 2, slot).wait()

        valid = jnp.where(j == n_blk - 1, last_valid, MOE_ROWS)

        @pl.when(valid > half_rows)
        def _():
            compute(slot, MOE_ROWS)

        @pl.when(valid <= half_rows)
        def _():
            compute(slot, half_rows)
            obuf[slot, half_rows:, :] = jnp.zeros((MOE_ROWS - half_rows, obuf.shape[2]), obuf.dtype)

        put(g, slot).start()
        return carry

    lax.fori_loop(0, n_blk, block, 0)

    @pl.when((e == N_EXPERTS - 1) & (n_total >= 2))
    def _():
        put(n_total - 2, lax.rem(n_total, 2)).wait()

    @pl.when((e == N_EXPERTS - 1) & (n_total >= 1))
    def _():
        put(n_total - 1, lax.rem(n_total - 1, 2)).wait()


def _experts(layer, first_blk, n_blk, last_valid, xs, wg, bg, wu, bu, wd, bd):
    n_rows, w = xs.shape
    wsel = lambda e, b0, nb, lv, nt: (layer, e, 0, 0)
    wspec = pl.BlockSpec((None, None, D_MODEL, D_MODEL), wsel)
    bspec = pl.BlockSpec((None, None, 1, D_MODEL), wsel)
    bias = lambda b: b.reshape(b.shape[0], b.shape[1], 1, b.shape[2])
    return pl.pallas_call(
        _experts_body,
        grid_spec=pltpu.PrefetchScalarGridSpec(
            num_scalar_prefetch=4,
            grid=(N_EXPERTS,),
            in_specs=[pl.BlockSpec(memory_space=pl.ANY), wspec, bspec, wspec, bspec, wspec, bspec],
            out_specs=pl.BlockSpec(memory_space=pl.ANY),
            scratch_shapes=[pltpu.VMEM((D_MODEL, D_MODEL), BF16)] * 3
            + [pltpu.VMEM((2, MOE_ROWS, w), U32)] * 2 + [pltpu.SemaphoreType.DMA((2,))] * 2,
        ),
        out_shape=jax.ShapeDtypeStruct((n_rows, w), U32),
        compiler_params=_cparams(1),
        name="experts",
    )(first_blk, n_blk, last_valid, jnp.sum(n_blk).reshape(1), xs, wg, bias(bg), wu, bias(bu), wd, bias(bd))


def _combine_body(n_first, o0_ref, o1_ref, o2_ref, o3_ref, gt_ref, x_ref, g_ref, b_ref, ya_ref, yb_ref=None):
    half = D_MODEL // 2
    gates = gt_ref[...]
    hi = jnp.zeros((x_ref.shape[0], half), F32)
    lo = jnp.zeros((x_ref.shape[0], half), F32)
    for k, o_ref in enumerate((o0_ref, o1_ref, o2_ref, o3_ref)):
        h, l = _unpack_halves(o_ref[...])
        gk = gates[:, k:k + 1]
        hi = hi + gk * h
        lo = lo + gk * l
    x = x_ref[...]
    y_hi = DEEPNORM_ALPHA * x[:, :half] + hi
    y_lo = DEEPNORM_ALPHA * x[:, half:] + lo
    mu = (jnp.sum(y_hi, axis=-1, keepdims=True) + jnp.sum(y_lo, axis=-1, keepdims=True)) * (1.0 / D_MODEL)
    d_hi = y_hi - mu
    d_lo = y_lo - mu
    var = (jnp.sum(d_hi * d_hi, axis=-1, keepdims=True) + jnp.sum(d_lo * d_lo, axis=-1, keepdims=True)) * (1.0 / D_MODEL)
    r = lax.rsqrt(var + LN_EPS)
    out_hi = d_hi * r * g_ref[:, :half] + b_ref[:, :half]
    out_lo = d_lo * r * g_ref[:, half:] + b_ref[:, half:]

    def write(y_ref):
        y_ref[:, :half] = out_hi
        y_ref[:, half:] = out_lo

    if yb_ref is None:
        write(ya_ref)
    else:
        pl.when(pl.program_id(0) < n_first)(lambda: write(ya_ref))
        pl.when(pl.program_id(0) >= n_first)(lambda: write(yb_ref))


def _combine_ln(o4, gates, x, g, b, t_first=None):
    t = x.shape[0]
    tm = TOKEN_TILE
    row = lambda i: (i, 0)
    fix = lambda i: (0, 0)
    choice = lambda k: pl.BlockSpec((tm, 512), lambda i: (k * (t // tm) + i, 0))
    if t_first is None:
        n_first = None
        out_specs = pl.BlockSpec((tm, D_MODEL), row)
        out_shape = jax.ShapeDtypeStruct((t, D_MODEL), F32)
    else:
        n_first = t_first // tm
        out_specs = [pl.BlockSpec((tm, D_MODEL), lambda i: (jnp.minimum(i, n_first - 1), 0)),
                     pl.BlockSpec((tm, D_MODEL), lambda i: (jnp.maximum(i - n_first, 0), 0))]
        out_shape = [jax.ShapeDtypeStruct((t_first, D_MODEL), F32), jax.ShapeDtypeStruct((t - t_first, D_MODEL), F32)]
    return pl.pallas_call(
        functools.partial(_combine_body, n_first),
        grid=(t // tm,),
        in_specs=[choice(0), choice(1), choice(2), choice(3), pl.BlockSpec((tm, TOP_K), row),
                  pl.BlockSpec((tm, D_MODEL), row), pl.BlockSpec((1, D_MODEL), fix), pl.BlockSpec((1, D_MODEL), fix)],
        out_specs=out_specs,
        out_shape=out_shape,
        compiler_params=_cparams(1),
        name="combine_ln",
    )(o4, o4, o4, o4, gates, x, g, b)


def _moe(layer, x1, xp, gates, eidx, rank, counts, ln_g, ln_b, wg, bg, wu, bu, wd, bd, t_first=None):
    t = x1.shape[0]
    bm = MOE_ROWS
    n_blocks = t * TOP_K // bm + N_EXPERTS
    n_rows = n_blocks * bm
    cnt = counts[:, 0]
    padded = (cnt + bm - 1) // bm * bm
    pad_end = jnp.cumsum(padded)
    pad_start = pad_end - padded
    e = eidx[:TOP_K]
    start = jnp.sum(jnp.where(e[:, :, None] == jnp.arange(N_EXPERTS, dtype=I32), pad_start, 0), axis=-1)
    dest = (start + rank[:TOP_K]).T
    n_blk = padded // bm
    last_valid = cnt - (n_blk - 1) * bm
    xs = _sc_scatter_rows(xp, dest, n_rows)
    outs = _experts(layer, pad_start // bm, n_blk, last_valid, xs, wg, bg, wu, bu, wd, bd)
    o4 = _sc_gather_rows(outs, dest.T.reshape(-1))
    return _combine_ln(o4, gates[:TOP_K].T, x1, ln_g, ln_b, t_first)


def _split2(x):
    hi = x.astype(BF16)
    return hi, (x - hi.astype(F32)).astype(BF16)


def _split3(x):
    hi = x.astype(BF16)
    rem = x - hi.astype(F32)
    mid = rem.astype(BF16)
    return hi, mid, (rem - mid.astype(F32)).astype(BF16)


def _chunk_cumsum(g, C):
    rows = g.shape[0]
    r = lax.broadcasted_iota(I32, (rows, rows), 0)
    c = lax.broadcasted_iota(I32, (rows, rows), 1)
    tri = ((r >= c) & (r // C == c // C)).astype(BF16)
    hi, mid, lo = _split3(g)
    dot = lambda part: jnp.dot(tri, part, preferred_element_type=F32)
    return dot(hi) + dot(mid) + dot(lo)


def _gla_batched_step(q, k, v, g, C, n_seq, n_heads, state_of, o_scr):
    rows = n_seq * C
    wide = n_seq * HEAD_W
    mid = max(C // 2 - 1, 0)
    r = lax.broadcasted_iota(I32, (rows, rows), 0)
    c = lax.broadcasted_iota(I32, (rows, rows), 1)
    same = (r // C) == (c // C)
    causal = same & (r >= c)
    parts = _split3(g)
    summed = lambda mask: functools.reduce(lambda p, q_: p + q_, [
        jnp.dot(mask.astype(BF16), part, preferred_element_type=F32) for part in parts])
    b = summed(causal)
    b_mid = summed(same & ((c % C) <= mid))
    b_last = summed(same)
    qe_hi, qe_lo = _split2(q * jnp.exp(b - b_mid))
    ke_hi, ke_lo = _split2(k * jnp.exp(b_mid - b))
    q_state = (q * jnp.exp(b)).astype(BF16)
    k_state = (k * jnp.exp(b_last - b)).astype(BF16)
    decay_parts = _split3(jnp.exp(b_last))
    row_w = lax.broadcasted_iota(I32, (rows, wide), 0)
    blk_w = lax.broadcasted_iota(I32, (rows, wide), 1) // HEAD_W
    own = (row_w // C) == blk_w
    pick = (row_w == blk_w * C).astype(BF16)
    new_states = []
    for h in range(n_heads):
        cs = slice(h * HEAD_W, (h + 1) * HEAD_W)
        lhs = jnp.concatenate([qe_hi[:, cs], qe_hi[:, cs], qe_lo[:, cs]], axis=1)
        rhs = jnp.concatenate([ke_hi[:, cs], ke_lo[:, cs], ke_hi[:, cs]], axis=1)
        scores = jnp.where(causal, _dot_nt(lhs, rhs), 0.0)
        vh = v[:, cs].astype(BF16)
        s_cat = jnp.concatenate([state_of(s, h) for s in range(n_seq)], axis=1)
        o_full = _dot(q_state[:, cs], s_cat)
        o_state = jnp.concatenate([o_full[s * C:(s + 1) * C, s * HEAD_W:(s + 1) * HEAD_W] for s in range(n_seq)],
                                  axis=0)
        o = _dot(scores, vh) + o_state
        v_wide = jnp.where(own, jnp.concatenate([vh] * n_seq, axis=1), jnp.zeros((), BF16))
        kv = _dot_tn(k_state[:, cs], v_wide)
        decay = functools.reduce(lambda p, q_: p + q_, [
            lax.dot_general(part[:, cs], pick, (((0,), (0,)), ((), ())), preferred_element_type=F32)
            for part in decay_parts])
        new_states.append(s_cat * decay + kv)
        ms = jnp.mean(o * o, axis=-1, keepdims=True)
        o_scr[:, cs] = o * lax.rsqrt(ms + RMS_EPS)
    return new_states


def _gla_body(mode, n_seq, n_chunk, C, *refs):
    if mode == "gla":
        hq_ref, hk_ref, hv_ref, hg_ref, hlr_ref, wlr_ref, blr_ref, nw_ref, s0_ref, _, o_ref, so_ref, st, o_scr = refs
    else:
        hq_ref, hk_ref, hv_ref, hg_ref, lb_ref, nw_ref, s0_ref, _, o_ref, so_ref, st, o_scr = refs
    n_heads = 4
    n_keys = s0_ref.shape[2]
    tstep = pl.program_id(1)
    batched = n_chunk == 1 and n_seq > 1

    def padded_state(s, h):
        s_in = s0_ref[s, h]
        if n_keys < HEAD_W:
            s_in = jnp.concatenate([s_in, jnp.zeros((HEAD_W - n_keys, HEAD_W), F32)], axis=0)
        return s_in

    if not batched:
        @pl.when(tstep == 0)
        def _():
            for s in range(n_seq):
                for h in range(n_heads):
                    st[s, h] = padded_state(s, h).T

    if mode == "gla":
        q = hq_ref[...] * (GLA_DK ** -0.5)
        k = hk_ref[...]
        z = _dot(hlr_ref[...], wlr_ref[...]) + blr_ref[...]
        g = _log_sigmoid(z) * (1.0 / GLA_TAU)
    else:
        q = _silu(hq_ref[...]) * (HGRN_DK ** -0.5)
        lb = lb_ref[...]
        f = lb + (1.0 - lb) * jax.nn.sigmoid(hk_ref[...])
        k = 1.0 - f
        g = jnp.log(f)
    v = hv_ref[...]
    if batched:
        new_states = _gla_batched_step(q, k, v, g, C, n_seq, n_heads, padded_state, o_scr)
        for s in range(n_seq):
            for h in range(n_heads):
                so_ref[s, h] = new_states[h][0:n_keys, s * HEAD_W:(s + 1) * HEAD_W]
    else:
        causal = _tri(C)
        mid = max(C // 2 - 1, 0)
        b_all = _chunk_cumsum(g, C)
        for s in range(n_seq):
            states = [st[s, h] for h in range(n_heads)]
            for c in range(n_chunk):
                r0 = (s * n_chunk + c) * C
                rs = slice(r0, r0 + C)
                b, qc, kc = b_all[rs, :], q[rs, :], k[rs, :]
                b_last = b[C - 1:C, :]
                b_mid = b[mid:mid + 1, :]
                qe_hi, qe_lo = _split2(qc * jnp.exp(b - b_mid))
                ke_hi, ke_lo = _split2(kc * jnp.exp(b_mid - b))
                q_state = (qc * jnp.exp(b)).astype(BF16)
                k_state = (kc * jnp.exp(b_last - b)).astype(BF16)
                decay = jnp.exp(b_last)
                for h in range(n_heads):
                    cs = slice(h * HEAD_W, (h + 1) * HEAD_W)
                    lhs = jnp.concatenate([qe_hi[:, cs], qe_hi[:, cs], qe_lo[:, cs]], axis=1)
                    rhs = jnp.concatenate([ke_hi[:, cs], ke_lo[:, cs], ke_hi[:, cs]], axis=1)
                    scores = jnp.where(causal, _dot_nt(lhs, rhs), 0.0)
                    vh = v[rs, cs].astype(BF16)
                    o = _dot(scores, vh) + _dot_nt(q_state[:, cs], states[h])
                    states[h] = states[h] * decay[:, cs] + _dot_tn(vh, k_state[:, cs])
                    ms = jnp.mean(o * o, axis=-1, keepdims=True)
                    o_scr[rs, cs] = o * lax.rsqrt(ms + RMS_EPS)
            for h in range(n_heads):
                st[s, h] = states[h]
    o_ref[...] = (o_scr[...] * nw_ref[...] * _silu(hg_ref[...])).astype(BF16)

    if not batched:
        @pl.when(tstep == pl.num_programs(1) - 1)
        def _():
            for s in range(n_seq):
                for h in range(n_heads):
                    so_ref[s, h] = st[s, h].T[0:n_keys, :]


def _seq_layout(n_batch, seq_len, row_off, sample):
    if sample:
        n_seq, n_chunk, C = SAMPLE_SEQS, 1, seq_len
        rows = n_seq * C
        grid = (n_batch // n_seq, 1)
        blk0 = row_off // rows
        rb = lambda i, t: blk0 + i
    else:
        n_seq, n_chunk, C = 1, PROMPT_TILE // SCAN_CHUNK, SCAN_CHUNK
        rows = PROMPT_TILE
        tiles = seq_len // rows
        grid = (n_batch, tiles)
        blk0 = row_off // rows
        rb = lambda i, t: blk0 + i * tiles + t
    return n_seq, n_chunk, C, rows, grid, rb


def _gla_call(mode, h, cols, extra, nw, s0, out_buf, n_batch, seq_len, row_off, sample):
    n_seq, n_chunk, C, rows, grid, rb = _seq_layout(n_batch, seq_len, row_off, sample)
    colspec = lambda c0, w: pl.BlockSpec((rows, w), lambda i, t: (rb(i, t), c0 // w))
    fix2 = lambda i, t: (0, 0)
    in_specs = [colspec(cols[0], 512), colspec(cols[1], 512), colspec(cols[2], 512), colspec(cols[3], 512)]
    args = [h, h, h, h]
    if mode == "gla":
        wlr, blr = extra
        in_specs += [colspec(cols[4], LANE), pl.BlockSpec((LANE, 512), fix2), pl.BlockSpec((1, 512), fix2)]
        args += [h, wlr, blr]
    else:
        in_specs += [pl.BlockSpec((1, 512), fix2)]
        args += [extra]
    n_keys = s0.shape[2]
    st_spec = pl.BlockSpec((n_seq, 4, n_keys, HEAD_W), lambda i, t: (i, 0, 0, 0))
    in_specs += [pl.BlockSpec((1, 512), fix2), st_spec, pl.BlockSpec(memory_space=pl.ANY)]
    args += [nw, s0, out_buf]
    o_spec = pl.BlockSpec((rows, 512), lambda i, t: (rb(i, t), 0))
    return pl.pallas_call(
        functools.partial(_gla_body, mode, n_seq, n_chunk, C),
        grid=grid,
        in_specs=in_specs,
        out_specs=[o_spec, st_spec],
        out_shape=[jax.ShapeDtypeStruct(out_buf.shape, out_buf.dtype),
                   jax.ShapeDtypeStruct((n_batch, 4, n_keys, HEAD_W), F32)],
        scratch_shapes=[pltpu.VMEM((n_seq, 4, HEAD_W, HEAD_W), F32), pltpu.VMEM((rows, 512), F32)],
        input_output_aliases={len(args) - 1: 0},
        compiler_params=_cparams(2),
        name=mode + ("_sample" if sample else "_prompt"),
    )(*args)


def _round_bf16(x, on=True):
    return x.astype(BF16).astype(F32) if on else x


def _conf_body(n_seq, L, round_x, round_w, a_ref, gt_ref, hist_ref, w_ref, b_ref, g_ref, be_ref, _, o_ref, co_ref,
               buf, bufr, y_scr, win):
    tstep = pl.program_id(1)
    hist = CONF_WIDTH - 1
    pad = 32 - hist

    @pl.when(tstep == 0)
    def _():
        for s in range(n_seq):
            buf[s, pad:32, :] = hist_ref[s]
            bufr[s, pad:32, :] = _round_bf16(hist_ref[s], round_x)

    u = a_ref[...] * jax.nn.sigmoid(gt_ref[...])
    ur = _round_bf16(u, round_x)
    for s in range(n_seq):
        buf[s, 32:32 + L, :] = u[s * L:(s + 1) * L, :]
        bufr[s, 32:32 + L, :] = ur[s * L:(s + 1) * L, :]
    w = _round_bf16(w_ref[...], round_w)
    for s in range(n_seq):
        acc = jnp.zeros((L, CONF_DIM), F32)
        for phase in range(8):
            n_taps = (CONF_WIDTH - 1 - phase) // 8 + 1
            span = L + 8 * (n_taps - 1)
            win[phase, 0:span, :] = bufr[s, pad + phase:pad + phase + span, :]
            for a in range(n_taps):
                j = 8 * a + phase
                acc = acc + win[phase, 8 * a:8 * a + L, :] * w[j:j + 1, :]
        y_scr[s * L:(s + 1) * L, :] = _silu(_layernorm(acc + b_ref[...], g_ref[...], be_ref[...]))
        tail = buf[s, L + pad:L + 32, :]
        buf[s, pad:32, :] = tail
        tailr = bufr[s, L + pad:L + 32, :]
        bufr[s, pad:32, :] = tailr
    o_ref[...] = y_scr[...].astype(o_ref.dtype)

    @pl.when(tstep == pl.num_programs(1) - 1)
    def _():
        for s in range(n_seq):
            co_ref[s] = buf[s, pad:32, :]


def _conf_call(h, col_a, col_g, cache, w, b, g, be, out_buf, n_batch, seq_len, row_off, sample):
    n_seq, n_chunk, C, rows, grid, rb = _seq_layout(n_batch, seq_len, row_off, sample)
    L = rows // n_seq
    hist = CONF_WIDTH - 1
    colspec = lambda c0: pl.BlockSpec((rows, 512), lambda i, t: (rb(i, t), c0 // 512))
    fix2 = lambda i, t: (0, 0)
    c_spec = pl.BlockSpec((n_seq, hist, CONF_DIM), lambda i, t: (i, 0, 0))
    return pl.pallas_call(
        functools.partial(_conf_body, n_seq, L, True, sample),
        grid=grid,
        in_specs=[colspec(col_a), colspec(col_g), c_spec,
                  pl.BlockSpec((CONF_WIDTH, CONF_DIM), fix2), pl.BlockSpec((1, CONF_DIM), fix2),
                  pl.BlockSpec((1, CONF_DIM), fix2), pl.BlockSpec((1, CONF_DIM), fix2),
                  pl.BlockSpec(memory_space=pl.ANY)],
        out_specs=[pl.BlockSpec((rows, 512), lambda i, t: (rb(i, t), 0)), c_spec],
        out_shape=[jax.ShapeDtypeStruct(out_buf.shape, out_buf.dtype),
                   jax.ShapeDtypeStruct((n_batch, hist, CONF_DIM), F32)],
        scratch_shapes=[pltpu.VMEM((n_seq, 32 + L, CONF_DIM), F32)] * 2 + [pltpu.VMEM((rows, CONF_DIM), F32),
                                                                           pltpu.VMEM((8, L + 24, CONF_DIM), F32)],
        input_output_aliases={7: 0},
        compiler_params=_cparams(2),
        name="conformer" + ("_sample" if sample else "_prompt"),
    )(h, h, cache, w, b, g, be, out_buf)


def _ssd_body(n_seq, n_chunk, C, round_x, round_w, hz_ref, hx_ref, hdt_ref, hist_ref, s0_ref, cw_ref, cb_ref, dtb_ref, alog_ref,
              dvec_ref, nw_ref, _, o_ref, co_ref, so_ref, st, buf, bufr, xbc, y_scr):
    tstep = pl.program_id(1)
    L = n_chunk * C
    hist = SSM_CONV - 1
    pad = 8 - hist
    n_pairs = SSM_HEADS // 2

    @pl.when(tstep == 0)
    def _():
        for s in range(n_seq):
            buf[s, pad:8, :] = hist_ref[s]
            bufr[s, pad:8, :] = _round_bf16(hist_ref[s], round_x)
            for m in range(n_pairs):
                st[s, m] = s0_ref[s, m]

    cw = _round_bf16(cw_ref[...], round_w)
    for s in range(n_seq):
        hx = hx_ref[s * L:(s + 1) * L, :]
        buf[s, 8:8 + L, :] = hx
        bufr[s, 8:8 + L, :] = _round_bf16(hx, round_x)
        acc = jnp.zeros((L, SSM_CONV_DIM), F32)
        for j in range(SSM_CONV):
            acc = acc + bufr[s, pad + j:pad + j + L, :] * cw[j:j + 1, :]
        xbc[s * L:(s + 1) * L, :] = _silu(acc + cb_ref[...])
        tail = buf[s, L + pad:L + 8, :]
        buf[s, pad:8, :] = tail
        tailr = bufr[s, L + pad:L + 8, :]
        bufr[s, pad:8, :] = tailr

    dt = _softplus(hdt_ref[...] + dtb_ref[...])
    la = dt * (-jnp.exp(alog_ref[...]))
    hrow = lax.broadcasted_iota(I32, (LANE, SSM_INNER), 0)
    hcol = lax.broadcasted_iota(I32, (LANE, SSM_INNER), 1) // SSM_HEADDIM
    expand = (hrow == hcol).astype(BF16)
    dtx = functools.reduce(lambda p, q: p + q,
                           [jnp.dot(part, expand, preferred_element_type=F32) for part in _split3(dt)])
    causal = _tri(C)
    tri = causal.astype(BF16)
    lane = lax.broadcasted_iota(I32, (C, HEAD_W), 1)
    bcol_all = _chunk_cumsum(la, C)
    heads_per_group = SSM_HEADS // SSM_GROUPS
    for s in range(n_seq):
        states = [st[s, m] for m in range(n_pairs)]
        for c in range(n_chunk):
            r0 = (s * n_chunk + c) * C
            rs = slice(r0, r0 + C)
            bcol = bcol_all[rs, :]
            brow = functools.reduce(lambda p, q: p + q, [
                lax.dot_general(part, tri, (((0,), (1,)), ((), ())), preferred_element_type=F32)
                for part in _split3(la[rs, :])])
            xs_c = xbc[rs, 0:SSM_INNER]
            v_c = (xs_c * dtx[rs, :]).astype(BF16)
            gmats, bms, cms = [], [], []
            for grp in range(SSM_GROUPS):
                bm = xbc[rs, SSM_INNER + grp * SSM_STATE:SSM_INNER + (grp + 1) * SSM_STATE]
                cm = xbc[rs, SSM_INNER + (SSM_GROUPS + grp) * SSM_STATE:SSM_INNER + (SSM_GROUPS + grp + 1) * SSM_STATE]
                cm_hi, cm_lo = _split2(cm)
                bm_hi, bm_lo = _split2(bm)
                gmats.append(_dot_nt(jnp.concatenate([cm_hi, cm_hi, cm_lo], axis=1),
                                     jnp.concatenate([bm_hi, bm_lo, bm_hi], axis=1)))
                bms.append(bm)
                cms.append(cm)
            for m in range(n_pairs):
                grp = (2 * m) // heads_per_group
                bm, cm, gmat = bms[grp], cms[grp], gmats[grp]
                ps = slice(m * HEAD_W, (m + 1) * HEAD_W)
                vp = v_c[:, ps]
                s_t = states[m]
                o_halves, new_rows = [], []
                for hh in range(2):
                    hd = 2 * m + hh
                    bc = bcol[:, hd:hd + 1]
                    br = brow[hd:hd + 1, :]
                    dec = jnp.where(causal, jnp.exp(jnp.minimum(bc - br, 0.0)), 0.0)
                    b_last = bcol[C - 1:C, hd:hd + 1]
                    o_halves.append(_dot(gmat * dec, vp) + _dot_nt(cm * jnp.exp(bc), s_t))
                    kv = _dot_tn(vp, bm * jnp.exp(b_last - bc))
                    vs = slice(hh * SSM_HEADDIM, (hh + 1) * SSM_HEADDIM)
                    new_rows.append(s_t[vs, :] * jnp.exp(b_last) + kv[vs, :])
                states[m] = jnp.concatenate(new_rows, axis=0)
                o_pair = jnp.where(lane < SSM_HEADDIM, o_halves[0], o_halves[1])
                y_scr[rs, ps] = o_pair + dvec_ref[:, ps] * xs_c[:, ps]
        for m in range(n_pairs):
            st[s, m] = states[m]
    y = y_scr[...] * _silu(hz_ref[...])
    gw = SSM_INNER // SSM_GROUPS
    for grp in range(SSM_GROUPS):
        gs = slice(grp * gw, (grp + 1) * gw)
        yg = y[:, gs]
        ms = jnp.mean(yg * yg, axis=-1, keepdims=True)
        o_ref[:, gs] = (yg * lax.rsqrt(ms + RMS_EPS) * nw_ref[:, gs]).astype(BF16)

    @pl.when(tstep == pl.num_programs(1) - 1)
    def _():
        for s in range(n_seq):
            co_ref[s] = buf[s, pad:8, :]
            for m in range(n_pairs):
                so_ref[s, m] = st[s, m]


def _ssd_call(h, col_z, col_x, col_dt, cache, s0, cw, cb, dtb, alog, dvec, nw, out_buf, n_batch, seq_len, row_off,
              sample):
    n_seq, n_chunk, C, rows, grid, rb = _seq_layout(n_batch, seq_len, row_off, sample)
    L = rows // n_seq
    hist = SSM_CONV - 1
    n_pairs = SSM_HEADS // 2
    colspec = lambda c0, w: pl.BlockSpec((rows, w), lambda i, t: (rb(i, t), c0 // w))
    fix2 = lambda i, t: (0, 0)
    c_spec = pl.BlockSpec((n_seq, hist, SSM_CONV_DIM), lambda i, t: (i, 0, 0))
    st_spec = pl.BlockSpec((n_seq, n_pairs, HEAD_W, SSM_STATE), lambda i, t: (i, 0, 0, 0))
    return pl.pallas_call(
        functools.partial(_ssd_body, n_seq, n_chunk, C, sample, True),
        grid=grid,
        in_specs=[colspec(col_z, 512), colspec(col_x, SSM_CONV_DIM), colspec(col_dt, LANE), c_spec, st_spec,
                  pl.BlockSpec((SSM_CONV, SSM_CONV_DIM), fix2), pl.BlockSpec((1, SSM_CONV_DIM), fix2),
                  pl.BlockSpec((1, LANE), fix2), pl.BlockSpec((1, LANE), fix2),
                  pl.BlockSpec((1, SSM_INNER), fix2), pl.BlockSpec((1, SSM_INNER), fix2),
                  pl.BlockSpec(memory_space=pl.ANY)],
        out_specs=[pl.BlockSpec((rows, 512), lambda i, t: (rb(i, t), 0)), c_spec, st_spec],
        out_shape=[jax.ShapeDtypeStruct(out_buf.shape, out_buf.dtype),
                   jax.ShapeDtypeStruct((n_batch, hist, SSM_CONV_DIM), F32),
                   jax.ShapeDtypeStruct((n_batch, n_pairs, HEAD_W, SSM_STATE), F32)],
        scratch_shapes=[pltpu.VMEM((n_seq, n_pairs, HEAD_W, SSM_STATE), F32),
                        pltpu.VMEM((n_seq, 8 + L, SSM_CONV_DIM), F32),
                        pltpu.VMEM((n_seq, 8 + L, SSM_CONV_DIM), F32),
                        pltpu.VMEM((rows, SSM_CONV_DIM), F32),
                        pltpu.VMEM((rows, SSM_INNER), F32)],
        input_output_aliases={11: 0},
        compiler_params=_cparams(2),
        name="ssd" + ("_sample" if sample else "_prompt"),
    )(h, h, h, cache, s0, cw, cb, dtb, alog, dvec, nw, out_buf)


def _pad_heads(w, n_heads, width):
    lead = w.shape[:-1]
    w = w.reshape(lead + (n_heads, width))
    w = jnp.pad(w, [(0, 0)] * len(lead) + [(0, 0), (0, HEAD_W - width)])
    return w.reshape(lead + (n_heads * HEAD_W,))


def _row(v):
    return v.reshape(1, -1).astype(F32)


def kernel(x_prompt, x_sample, state_gla, cache_conformer, state_hgrn, state_ssm, cache_mamba_conv, w_in_even, w_gla_gate_lr, b_gla_gate, gla_norm_w, conf_conv_w, conf_conv_b, conf_ln_g, conf_ln_b, w_out_even, w_in_odd, hgrn_lower_bounds, hgrn_norm_w, mamba_conv_w, mamba_conv_b, mamba_dt_bias, mamba_a_log, mamba_d, mamba_norm_w, w_out_odd, ln1_g, ln1_b, ln2_g, ln2_b, router_w, router_b, expert_w_gate, expert_b_gate, expert_w_up, expert_b_up, expert_w_down, expert_b_down):
    bp, lp, _ = x_prompt.shape
    bs, ls, _ = x_sample.shape
    tp, ts = bp * lp, bs * ls
    x = (x_prompt.reshape(tp, D_MODEL), x_sample.reshape(ts, D_MODEL))

    def router_params(layer):
        return router_w[layer].T.astype(BF16), router_b[layer].astype(F32).reshape(N_EXPERTS, 1)

    def finish_layer(layer, x, mix_a, mix_b, w_out):
        rwt, rb = router_params(layer)
        x1, xp, gates, eidx, rank, counts = _outproj_ln_router(
            mix_a, mix_b, x[0], x[1], w_out[:512].astype(BF16), w_out[512:].astype(BF16),
            _row(ln1_g[layer]), _row(ln1_b[layer]), rwt, rb)
        return _moe(layer, x1, xp, gates, eidx, rank, counts, _row(ln2_g[layer]), _row(ln2_b[layer]),
                    expert_w_gate, expert_b_gate, expert_w_up, expert_b_up, expert_w_down, expert_b_down, tp)

    def mix_buffer():
        return jnp.zeros((tp + ts, 512), BF16)

    wi = w_in_even[0]
    wq, wk, wv, wg, wlr, wglu = jnp.split(wi, [256, 512, 1024, 1536, 1552], axis=1)
    w_even = jnp.concatenate([_pad_heads(wq, GLA_HEADS, GLA_DK), _pad_heads(wk, GLA_HEADS, GLA_DK), wv, wg, wglu,
                              jnp.pad(wlr, ((0, 0), (0, LANE - GLA_RANK)))], axis=1).astype(BF16)
    cols_gla = (0, 512, 1024, 1536, 3072)
    col_a, col_gate = 2048, 2560
    h = _inproj(x[0], x[1], w_even)
    wlr_p = jnp.pad(_pad_heads(w_gla_gate_lr[0], GLA_HEADS, GLA_DK), ((0, LANE - GLA_RANK), (0, 0)))
    blr_p = _row(_pad_heads(b_gla_gate[0], GLA_HEADS, GLA_DK))
    nw = _row(gla_norm_w[0])
    conf_args = (conf_conv_w[0], _row(conf_conv_b[0]), _row(conf_ln_g[0]), _row(conf_ln_b[0]))
    s0_p = jnp.zeros((bp, GLA_HEADS, GLA_DK, HEAD_W), F32)
    s0_s = state_gla[0]
    mix_a, sg_p = _gla_call("gla", h, cols_gla, (wlr_p, blr_p), nw, s0_p, mix_buffer(), bp, lp, 0, False)
    mix_a, sg_s = _gla_call("gla", h, cols_gla, (wlr_p, blr_p), nw, s0_s, mix_a, bs, ls, tp, True)
    mix_b, cc_p = _conf_call(h, col_a, col_gate, jnp.zeros((bp,) + cache_conformer.shape[2:], F32), *conf_args,
                             mix_buffer(), bp, lp, 0, False)
    mix_b, cc_s = _conf_call(h, col_a, col_gate, cache_conformer[0], *conf_args, mix_b, bs, ls, tp, True)
    x = finish_layer(0, x, mix_a, mix_b, w_out_even[0])
    gla_p, gla_s = sg_p[None], sg_s[None]
    conf_p, conf_s = cc_p[None], cc_s[None]

    lb_cum = jnp.cumsum(jax.nn.softmax(hgrn_lower_bounds.astype(F32), axis=0), axis=0)
    lower_bound = _row((lb_cum - lb_cum[0])[1])
    wo = w_in_odd[0]
    w_odd = jnp.concatenate([wo[:, 2560:3584], wo[:, :2560],
                             jnp.pad(wo[:, 3584:], ((0, 0), (0, LANE - SSM_HEADS)))], axis=1).astype(BF16)
    h = _inproj(x[0], x[1], w_odd)
    cols_hgrn = (1024, 1536, 2048, 2560)
    col_z, col_x, col_dt = 3072, 0, 3584
    nw = _row(hgrn_norm_w[0])
    mix_a, sh_p = _gla_call("hgrn", h, cols_hgrn, lower_bound, nw,
                            jnp.zeros((bp, HGRN_HEADS, HEAD_W, HEAD_W), F32), mix_buffer(), bp, lp, 0, False)
    mix_a, sh_s = _gla_call("hgrn", h, cols_hgrn, lower_bound, nw, state_hgrn[0], mix_a, bs, ls, tp, True)

    def pair_states(s):
        return jnp.swapaxes(s, 2, 3).reshape(s.shape[0], SSM_HEADS // 2, HEAD_W, SSM_STATE)

    def unpair_states(s):
        return jnp.swapaxes(s.reshape(s.shape[0], SSM_HEADS, SSM_HEADDIM, SSM_STATE), 2, 3)

    pad8 = lambda v: jnp.pad(v.astype(F32), (0, LANE - SSM_HEADS)).reshape(1, LANE)
    ssd_args = (mamba_conv_w[0], _row(mamba_conv_b[0]), pad8(mamba_dt_bias[0]), pad8(mamba_a_log[0]),
                _row(jnp.repeat(mamba_d[0], SSM_HEADDIM)), _row(mamba_norm_w[0]))
    mix_b, cm_p, ss_p = _ssd_call(h, col_z, col_x, col_dt, jnp.zeros((bp,) + cache_mamba_conv.shape[2:], F32),
                                  jnp.zeros((bp, SSM_HEADS // 2, HEAD_W, SSM_STATE), F32), *ssd_args,
                                  mix_buffer(), bp, lp, 0, False)
    mix_b, cm_s, ss_s = _ssd_call(h, col_z, col_x, col_dt, cache_mamba_conv[0], pair_states(state_ssm[0]),
                                  *ssd_args, mix_b, bs, ls, tp, True)
    y_prompt, y_sample = finish_layer(1, x, mix_a, mix_b, w_out_odd[0])
    y_prompt = y_prompt.reshape(bp, lp, D_MODEL)
    y_sample = y_sample.reshape(bs, ls, D_MODEL)
    return (y_prompt, y_sample, gla_p, gla_s, conf_p, conf_s, sh_p[None], sh_s[None],
            unpair_states(ss_p)[None], unpair_states(ss_s)[None], cm_p[None], cm_s[None])
```

```python
import functools

import jax
import jax.numpy as jnp
from jax import lax
from jax.experimental import pallas as pl
from jax.experimental.pallas import tpu as pltpu
from jax.experimental.pallas import tpu_sc as plsc

F32 = jnp.float32
BF16 = jnp.bfloat16
I32 = jnp.int32
U32 = jnp.uint32
HIGHEST = lax.Precision.HIGHEST

D_MODEL = 1024
DEPTH = 2
DEEPNORM_ALPHA = (2.0 * DEPTH) ** 0.25
LN_EPS = 1e-5
RMS_EPS = 1e-6
LANE = 128
HEAD_W = 128
GLA_HEADS, GLA_DK, GLA_RANK, GLA_TAU = 4, 64, 16, 16.0
CONF_DIM, CONF_WIDTH = 512, 31
HGRN_HEADS, HGRN_DK = 4, 128
SSM_HEADS, SSM_HEADDIM, SSM_STATE, SSM_GROUPS, SSM_CONV = 8, 64, 128, 2, 4
SSM_INNER = SSM_HEADS * SSM_HEADDIM
SSM_CONV_DIM = SSM_INNER + 2 * SSM_GROUPS * SSM_STATE
N_EXPERTS, TOP_K = 32, 4
SWIGLU_ALPHA, SWIGLU_LIMIT = 1.702, 7.0
SCAN_CHUNK = 64
PROMPT_TILE = 512
SSD_PROMPT_TILE = 256
SAMPLE_SEQS = 16
TOKEN_TILE = 512
MOE_ROWS = 512
SC_CORES, SC_SUBCORES = 2, 16
SC_ROWS = 64
SC_SCATTER_ROWS = 32
VMEM_LIMIT = 56 * 1024 * 1024


def _cparams(n_axes):
    return pltpu.CompilerParams(dimension_semantics=("arbitrary",) * n_axes, vmem_limit_bytes=VMEM_LIMIT)


def _silu(x):
    return x * jax.nn.sigmoid(x)


def _softplus(x):
    return jnp.maximum(x, 0.0) + jnp.log(1.0 + jnp.exp(-jnp.abs(x)))


def _log_sigmoid(x):
    return jnp.minimum(x, 0.0) - jnp.log(1.0 + jnp.exp(-jnp.abs(x)))


def _layernorm(y, g, b):
    mu = jnp.mean(y, axis=-1, keepdims=True)
    d = y - mu
    var = jnp.mean(d * d, axis=-1, keepdims=True)
    return d * lax.rsqrt(var + LN_EPS) * g + b


def _dot(a, b):
    return jnp.dot(a.astype(BF16), b.astype(BF16), preferred_element_type=F32)


def _dot_nt(a, b):
    return lax.dot_general(a.astype(BF16), b.astype(BF16), (((1,), (1,)), ((), ())), preferred_element_type=F32)


def _dot_tn(a, b):
    return lax.dot_general(a.astype(BF16), b.astype(BF16), (((0,), (0,)), ((), ())), preferred_element_type=F32)


def _tri(c):
    r = lax.broadcasted_iota(I32, (c, c), 0)
    k = lax.broadcasted_iota(I32, (c, c), 1)
    return r >= k


def _pair_specs(tm, n_first, width):
    return [pl.BlockSpec((tm, width), lambda i: (jnp.minimum(i, n_first - 1), 0)),
            pl.BlockSpec((tm, width), lambda i: (jnp.maximum(i - n_first, 0), 0))]


def _pair_tile(n_first, xa_ref, xb_ref):
    return jnp.where(pl.program_id(0) < n_first, xa_ref[...], xb_ref[...])


def _inproj_body(n_first, xa_ref, xb_ref, w_ref, o_ref):
    xb = _pair_tile(n_first, xa_ref, xb_ref).astype(BF16)
    n = w_ref.shape[1]
    for c0 in range(0, n, 512):
        c1 = min(c0 + 512, n)
        o_ref[:, c0:c1] = jnp.dot(xb, w_ref[:, c0:c1], preferred_element_type=F32)


def _inproj(xa, xb, w):
    k, n = w.shape
    t = xa.shape[0] + xb.shape[0]
    n_first = xa.shape[0] // TOKEN_TILE
    return pl.pallas_call(
        functools.partial(_inproj_body, n_first),
        grid=(t // TOKEN_TILE,),
        in_specs=_pair_specs(TOKEN_TILE, n_first, k) + [pl.BlockSpec((k, n), lambda i: (0, 0))],
        out_specs=pl.BlockSpec((TOKEN_TILE, n), lambda i: (i, 0)),
        out_shape=jax.ShapeDtypeStruct((t, n), F32),
        compiler_params=_cparams(1),
        name="inproj",
    )(xa, xb, w)


def _pack_halves(y):
    half = y.shape[1] // 2
    hi = lax.bitcast_convert_type(y[:, :half].astype(BF16).astype(F32), U32)
    lo = lax.bitcast_convert_type(y[:, half:].astype(BF16).astype(F32), U32)
    return (hi & jnp.uint32(0xFFFF0000)) | (lo >> 16)


def _unpack_halves(w):
    hi = lax.bitcast_convert_type(w & jnp.uint32(0xFFFF0000), F32)
    lo = lax.bitcast_convert_type(w << 16, F32)
    return hi, lo


def _outproj_body(n_first, a_ref, b_ref, xa_ref, xb_ref, wa_ref, wb_ref, g_ref, be_ref, rwt_ref, rb_ref,
                  x1_ref, xp_ref, gate_ref, idx_ref, rank_ref, cnt_ref, carry):
    @pl.when(pl.program_id(0) == 0)
    def _():
        carry[...] = jnp.zeros(carry.shape, F32)

    mix = (jnp.dot(a_ref[...], wa_ref[...], preferred_element_type=F32)
           + jnp.dot(b_ref[...], wb_ref[...], preferred_element_type=F32))
    x1 = _layernorm(DEEPNORM_ALPHA * _pair_tile(n_first, xa_ref, xb_ref) + mix, g_ref[...], be_ref[...])
    x1_ref[...] = x1
    xp_ref[...] = _pack_halves(x1)
    logits = _dot_nt(rwt_ref[...], x1) + rb_ref[...]
    tm = logits.shape[1]
    expert = lax.broadcasted_iota(I32, logits.shape, 0)
    vals, idxs = [], []
    for _ in range(TOP_K):
        m = jnp.max(logits, axis=0, keepdims=True)
        sel = jnp.min(jnp.where(logits == m, expert, N_EXPERTS), axis=0, keepdims=True)
        vals.append(m)
        idxs.append(sel)
        logits = jnp.where(expert == sel, -jnp.inf, logits)
    exps = [jnp.exp(v - vals[0]) for v in vals]
    inv = 1.0 / functools.reduce(lambda p, q: p + q, exps)
    chosen = jnp.zeros(logits.shape, F32)
    for k in range(TOP_K):
        chosen = chosen + (expert == idxs[k]).astype(F32)
    earlier = lax.broadcasted_iota(I32, (tm, tm), 0) < lax.broadcasted_iota(I32, (tm, tm), 1)
    before = carry[...] + jnp.dot(chosen.astype(BF16), earlier.astype(BF16), preferred_element_type=F32)
    choice = lax.broadcasted_iota(I32, (8, tm), 0)
    gates = jnp.zeros((8, tm), F32)
    eidx = jnp.zeros((8, tm), I32)
    ranks = jnp.zeros((8, tm), F32)
    for k in range(TOP_K):
        rk = jnp.sum(jnp.where(expert == idxs[k], before, 0.0), axis=0, keepdims=True)
        gates = jnp.where(choice == k, exps[k] * inv, gates)
        eidx = jnp.where(choice == k, idxs[k], eidx)
        ranks = jnp.where(choice == k, rk, ranks)
    gate_ref[...] = gates
    idx_ref[...] = eidx
    rank_ref[...] = ranks.astype(I32)
    carry[...] = carry[...] + jnp.sum(chosen, axis=1, keepdims=True)
    cnt_ref[...] = carry[...].astype(I32)


def _outproj_ln_router(a, b, xa, xb, wa, wb, g, be, rwt, rb):
    t = xa.shape[0] + xb.shape[0]
    tm = TOKEN_TILE
    n_first = xa.shape[0] // tm
    row = lambda i: (i, 0)
    col = lambda i: (0, i)
    fix = lambda i: (0, 0)
    return pl.pallas_call(
        functools.partial(_outproj_body, n_first),
        grid=(t // tm,),
        in_specs=[pl.BlockSpec((tm, 512), row), pl.BlockSpec((tm, 512), row)] + _pair_specs(tm, n_first, D_MODEL)
        + [pl.BlockSpec((512, D_MODEL), fix), pl.BlockSpec((512, D_MODEL), fix),
           pl.BlockSpec((1, D_MODEL), fix), pl.BlockSpec((1, D_MODEL), fix),
           pl.BlockSpec((N_EXPERTS, D_MODEL), fix), pl.BlockSpec((N_EXPERTS, 1), fix)],
        out_specs=[pl.BlockSpec((tm, D_MODEL), row), pl.BlockSpec((tm, 512), row),
                   pl.BlockSpec((8, tm), col), pl.BlockSpec((8, tm), col), pl.BlockSpec((8, tm), col),
                   pl.BlockSpec((N_EXPERTS, 1), fix)],
        out_shape=[jax.ShapeDtypeStruct((t, D_MODEL), F32), jax.ShapeDtypeStruct((t, 512), U32),
                   jax.ShapeDtypeStruct((8, t), F32), jax.ShapeDtypeStruct((8, t), I32),
                   jax.ShapeDtypeStruct((8, t), I32), jax.ShapeDtypeStruct((N_EXPERTS, 1), I32)],
        scratch_shapes=[pltpu.VMEM((N_EXPERTS, 1), F32)],
        compiler_params=_cparams(1),
        name="outproj_ln_router",
    )(a, b, xa, xb, wa, wb, g, be, rwt, rb)


def _sc_mesh():
    return plsc.VectorSubcoreMesh(core_axis_name="c", subcore_axis_name="s")


def _sc_scatter_rows(src, dest, n_out):
    n_src, w = src.shape
    n_dst = dest.shape[1]
    workers = SC_CORES * SC_SUBCORES
    per_worker = n_src // workers
    chunks = per_worker // SC_SCATTER_ROWS
    assert n_src == workers * chunks * SC_SCATTER_ROWS
    idx = dest.T.reshape(n_dst, workers, chunks, SC_SCATTER_ROWS)

    @functools.partial(pl.kernel, mesh=_sc_mesh(), out_type=jax.ShapeDtypeStruct((n_out, w), src.dtype),
                       scratch_types=[pltpu.VMEM((n_dst, chunks, SC_SCATTER_ROWS), I32)]
                       + [pltpu.VMEM((SC_SCATTER_ROWS, w), src.dtype)] * 2 + [pltpu.SemaphoreType.DMA] * 4)
    def scatter(src_hbm, idx_hbm, out_hbm, idx_v, rows_a, rows_b, sem_ra, sem_rb, sem_wa, sem_wb):
        worker = lax.axis_index("s") * SC_CORES + lax.axis_index("c")
        base = worker * per_worker
        for k in range(n_dst):
            pltpu.sync_copy(idx_hbm.at[k, worker], idx_v.at[k])

        def read(c, rows, sem):
            return pltpu.async_copy(src_hbm.at[pl.ds(pl.multiple_of(base + c * SC_SCATTER_ROWS, 8), SC_SCATTER_ROWS)],
                                    rows, sem)

        def write_all(pending_read, c, rows, sem):
            pending_read.wait()
            return [pltpu.async_copy(rows, out_hbm.at[idx_v.at[k, c]], sem) for k in range(n_dst)]

        @pl.loop(0, chunks // 2)
        def _(p):
            read_a = read(2 * p, rows_a, sem_ra)
            read_b = read(2 * p + 1, rows_b, sem_rb)
            writes = write_all(read_a, 2 * p, rows_a, sem_wa) + write_all(read_b, 2 * p + 1, rows_b, sem_wb)
            for wr in writes:
                wr.wait()

        if chunks % 2:
            for wr in write_all(read(chunks - 1, rows_a, sem_ra), chunks - 1, rows_a, sem_wa):
                wr.wait()

    return scatter(src, idx)


def _sc_gather_rows(table, idx):
    n, w = idx.shape[0], table.shape[1]
    workers = SC_CORES * SC_SUBCORES
    chunks = n // (workers * SC_ROWS)
    assert n == workers * chunks * SC_ROWS and chunks % 2 == 0
    idx = idx.reshape(workers, chunks, SC_ROWS)

    @functools.partial(pl.kernel, mesh=_sc_mesh(), out_type=jax.ShapeDtypeStruct((n, w), table.dtype),
                       scratch_types=[pltpu.VMEM((chunks, SC_ROWS), I32)] + [pltpu.VMEM((SC_ROWS, w), table.dtype)] * 2
                       + [pltpu.SemaphoreType.DMA] * 4)
    def gather(table_hbm, idx_hbm, out_hbm, idx_v, rows_a, rows_b, sem_ra, sem_rb, sem_wa, sem_wb):
        worker = lax.axis_index("s") * SC_CORES + lax.axis_index("c")
        base = worker * (chunks * SC_ROWS)
        pltpu.sync_copy(idx_hbm.at[worker], idx_v)

        def out_rows(c):
            return out_hbm.at[pl.ds(pl.multiple_of(base + c * SC_ROWS, 8), SC_ROWS)]

        @pl.loop(0, chunks // 2)
        def _(p):
            read_a = pltpu.async_copy(table_hbm.at[idx_v.at[2 * p]], rows_a, sem_ra)
            read_b = pltpu.async_copy(table_hbm.at[idx_v.at[2 * p + 1]], rows_b, sem_rb)
            read_a.wait()
            write_a = pltpu.async_copy(rows_a, out_rows(2 * p), sem_wa)
            read_b.wait()
            write_b = pltpu.async_copy(rows_b, out_rows(2 * p + 1), sem_wb)
            write_a.wait()
            write_b.wait()

    return gather(table, idx)


def _experts_body(b0_ref, nb_ref, last_ref, nt_ref, xs_hbm, wg_ref, bg_ref, wu_ref, bu_ref, wd_ref, bd_ref, o_hbm,
                  wg_s, wu_s, wd_s, xbuf, obuf, sem_in, sem_out):
    e = pl.program_id(0)
    first_blk, n_blk, last_valid, n_total = b0_ref[e], nb_ref[e], last_ref[e], nt_ref[0]
    quarter = MOE_ROWS // 4

    def rows_of(g):
        return pl.ds(pl.multiple_of(g * MOE_ROWS, MOE_ROWS), MOE_ROWS)

    def fetch(g, slot):
        return pltpu.make_async_copy(xs_hbm.at[rows_of(g)], xbuf.at[slot], sem_in.at[slot])

    def put(g, slot):
        return pltpu.make_async_copy(obuf.at[slot], o_hbm.at[rows_of(g)], sem_out.at[slot])

    @pl.when((e == 0) & (n_total > 0))
    def _():
        fetch(0, 0).start()

    @pl.when(n_blk > 0)
    def _():
        wg_s[...] = wg_ref[...].astype(BF16)
        wu_s[...] = wu_ref[...].astype(BF16)
        wd_s[...] = wd_ref[...].astype(BF16)

    def compute(slot, rows):
        half = D_MODEL // 2
        x_hi, x_lo = _unpack_halves(xbuf[slot, 0:rows, :])
        x_hi = x_hi.astype(BF16)
        x_lo = x_lo.astype(BF16)
        g = (jnp.dot(x_hi, wg_s[:half, :], preferred_element_type=F32)
             + jnp.dot(x_lo, wg_s[half:, :], preferred_element_type=F32) + bg_ref[...])
        u = (jnp.dot(x_hi, wu_s[:half, :], preferred_element_type=F32)
             + jnp.dot(x_lo, wu_s[half:, :], preferred_element_type=F32) + bu_ref[...])
        g = jnp.minimum(g, SWIGLU_LIMIT)
        u = jnp.clip(u, -SWIGLU_LIMIT, SWIGLU_LIMIT)
        hmid = (u + 1.0) * (g * jax.nn.sigmoid(SWIGLU_ALPHA * g))
        out = jnp.dot(hmid.astype(BF16), wd_s[...], preferred_element_type=F32) + bd_ref[...]
        obuf[slot, 0:rows, :] = _pack_halves(out)

    def block(j, carry):
        g = first_blk + j
        slot = lax.rem(g, 2)
        fetch(g, slot).wait()

        @pl.when(g + 1 < n_total)
        def _():
            fetch(g + 1, 1 - slot).start()

        @pl.when(g >= 2)
        def _():
            put(g - 2, slot).wait()

        valid = jnp.where(j == n_blk - 1, last_valid, MOE_ROWS)

        for rows in range(quarter, MOE_ROWS + 1, quarter):
            @pl.when((valid > rows - quarter) & (valid <= rows))
            def _():
                compute(slot, rows)
                if rows < MOE_ROWS:
                    obuf[slot, rows:, :] = jnp.zeros((MOE_ROWS - rows, obuf.shape[2]), obuf.dtype)

        put(g, slot).start()
        return carry

    lax.fori_loop(0, n_blk, block, 0)

    @pl.when((e == N_EXPERTS - 1) & (n_total >= 2))
    def _():
        put(n_total - 2, lax.rem(n_total, 2)).wait()

    @pl.when((e == N_EXPERTS - 1) & (n_total >= 1))
    def _():
        put(n_total - 1, lax.rem(n_total - 1, 2)).wait()


def _experts(layer, first_blk, n_blk, last_valid, xs, wg, bg, wu, bu, wd, bd):
    n_rows, w = xs.shape
    wsel = lambda e, b0, nb, lv, nt: (layer, e, 0, 0)
    wspec = pl.BlockSpec((None, None, D_MODEL, D_MODEL), wsel)
    bspec = pl.BlockSpec((None, None, 1, D_MODEL), wsel)
    bias = lambda b: b.reshape(b.shape[0], b.shape[1], 1, b.shape[2])
    return pl.pallas_call(
        _experts_body,
        grid_spec=pltpu.PrefetchScalarGridSpec(
            num_scalar_prefetch=4,
            grid=(N_EXPERTS,),
            in_specs=[pl.BlockSpec(memory_space=pl.ANY), wspec, bspec, wspec, bspec, wspec, bspec],
            out_specs=pl.BlockSpec(memory_space=pl.ANY),
            scratch_shapes=[pltpu.VMEM((D_MODEL, D_MODEL), BF16)] * 3
            + [pltpu.VMEM((2, MOE_ROWS, w), U32)] * 2 + [pltpu.SemaphoreType.DMA((2,))] * 2,
        ),
        out_shape=jax.ShapeDtypeStruct((n_rows, w), U32),
        compiler_params=_cparams(1),
        name="experts",
    )(first_blk, n_blk, last_valid, jnp.sum(n_blk).reshape(1), xs, wg, bias(bg), wu, bias(bu), wd, bias(bd))


def _combine_body(n_first, o0_ref, o1_ref, o2_ref, o3_ref, gt_ref, x_ref, g_ref, b_ref, ya_ref, yb_ref=None):
    half = D_MODEL // 2
    gates = gt_ref[...]
    hi = jnp.zeros((x_ref.shape[0], half), F32)
    lo = jnp.zeros((x_ref.shape[0], half), F32)
    for k, o_ref in enumerate((o0_ref, o1_ref, o2_ref, o3_ref)):
        h, l = _unpack_halves(o_ref[...])
        gk = gates[:, k:k + 1]
        hi = hi + gk * h
        lo = lo + gk * l
    x = x_ref[...]
    y_hi = DEEPNORM_ALPHA * x[:, :half] + hi
    y_lo = DEEPNORM_ALPHA * x[:, half:] + lo
    mu = (jnp.sum(y_hi, axis=-1, keepdims=True) + jnp.sum(y_lo, axis=-1, keepdims=True)) * (1.0 / D_MODEL)
    d_hi = y_hi - mu
    d_lo = y_lo - mu
    var = (jnp.sum(d_hi * d_hi, axis=-1, keepdims=True) + jnp.sum(d_lo * d_lo, axis=-1, keepdims=True)) * (1.0 / D_MODEL)
    r = lax.rsqrt(var + LN_EPS)
    out_hi = d_hi * r * g_ref[:, :half] + b_ref[:, :half]
    out_lo = d_lo * r * g_ref[:, half:] + b_ref[:, half:]

    def write(y_ref):
        y_ref[:, :half] = out_hi
        y_ref[:, half:] = out_lo

    if yb_ref is None:
        write(ya_ref)
    else:
        pl.when(pl.program_id(0) < n_first)(lambda: write(ya_ref))
        pl.when(pl.program_id(0) >= n_first)(lambda: write(yb_ref))


def _combine_ln(o4, gates, x, g, b, t_first=None):
    t = x.shape[0]
    tm = TOKEN_TILE
    row = lambda i: (i, 0)
    fix = lambda i: (0, 0)
    choice = lambda k: pl.BlockSpec((tm, 512), lambda i: (k * (t // tm) + i, 0))
    if t_first is None:
        n_first = None
        out_specs = pl.BlockSpec((tm, D_MODEL), row)
        out_shape = jax.ShapeDtypeStruct((t, D_MODEL), F32)
    else:
        n_first = t_first // tm
        out_specs = [pl.BlockSpec((tm, D_MODEL), lambda i: (jnp.minimum(i, n_first - 1), 0)),
                     pl.BlockSpec((tm, D_MODEL), lambda i: (jnp.maximum(i - n_first, 0), 0))]
        out_shape = [jax.ShapeDtypeStruct((t_first, D_MODEL), F32), jax.ShapeDtypeStruct((t - t_first, D_MODEL), F32)]
    return pl.pallas_call(
        functools.partial(_combine_body, n_first),
        grid=(t // tm,),
        in_specs=[choice(0), choice(1), choice(2), choice(3), pl.BlockSpec((tm, TOP_K), row),
                  pl.BlockSpec((tm, D_MODEL), row), pl.BlockSpec((1, D_MODEL), fix), pl.BlockSpec((1, D_MODEL), fix)],
        out_specs=out_specs,
        out_shape=out_shape,
        compiler_params=_cparams(1),
        name="combine_ln",
    )(o4, o4, o4, o4, gates, x, g, b)


def _moe(layer, x1, xp, gates, eidx, rank, counts, ln_g, ln_b, wg, bg, wu, bu, wd, bd, t_first=None):
    t = x1.shape[0]
    bm = MOE_ROWS
    n_blocks = t * TOP_K // bm + N_EXPERTS
    n_rows = n_blocks * bm
    cnt = counts[:, 0]
    padded = (cnt + bm - 1) // bm * bm
    pad_end = jnp.cumsum(padded)
    pad_start = pad_end - padded
    e = eidx[:TOP_K]
    start = jnp.sum(jnp.where(e[:, :, None] == jnp.arange(N_EXPERTS, dtype=I32), pad_start, 0), axis=-1)
    dest = (start + rank[:TOP_K]).T
    n_blk = padded // bm
    last_valid = cnt - (n_blk - 1) * bm
    xs = _sc_scatter_rows(xp, dest, n_rows)
    outs = _experts(layer, pad_start // bm, n_blk, last_valid, xs, wg, bg, wu, bu, wd, bd)
    o4 = _sc_gather_rows(outs, dest.T.reshape(-1))
    return _combine_ln(o4, gates[:TOP_K].T, x1, ln_g, ln_b, t_first)


def _split2(x):
    hi = x.astype(BF16)
    return hi, (x - hi.astype(F32)).astype(BF16)


def _split3(x):
    hi = x.astype(BF16)
    rem = x - hi.astype(F32)
    mid = rem.astype(BF16)
    return hi, mid, (rem - mid.astype(F32)).astype(BF16)


def _chunk_cumsum(g, C):
    rows = g.shape[0]
    r = lax.broadcasted_iota(I32, (rows, rows), 0)
    c = lax.broadcasted_iota(I32, (rows, rows), 1)
    tri = ((r >= c) & (r // C == c // C)).astype(BF16)
    hi, mid, lo = _split3(g)
    dot = lambda part: jnp.dot(tri, part, preferred_element_type=F32)
    return dot(hi) + dot(mid) + dot(lo)


def _gla_batched_step(q, k, v, g, C, n_seq, n_heads, state_of, o_scr):
    rows = n_seq * C
    wide = n_seq * HEAD_W
    mid = max(C // 2 - 1, 0)
    r = lax.broadcasted_iota(I32, (rows, rows), 0)
    c = lax.broadcasted_iota(I32, (rows, rows), 1)
    same = (r // C) == (c // C)
    causal = same & (r >= c)
    parts = _split3(g)
    summed = lambda mask: functools.reduce(lambda p, q_: p + q_, [
        jnp.dot(mask.astype(BF16), part, preferred_element_type=F32) for part in parts])
    b = summed(causal)
    b_mid = summed(same & ((c % C) <= mid))
    b_last = summed(same)
    qe_hi, qe_lo = _split2(q * jnp.exp(b - b_mid))
    ke_hi, ke_lo = _split2(k * jnp.exp(b_mid - b))
    q_state = (q * jnp.exp(b)).astype(BF16)
    k_state = (k * jnp.exp(b_last - b)).astype(BF16)
    decay_parts = _split3(jnp.exp(b_last))
    row_w = lax.broadcasted_iota(I32, (rows, wide), 0)
    blk_w = lax.broadcasted_iota(I32, (rows, wide), 1) // HEAD_W
    own = (row_w // C) == blk_w
    pick = (row_w == blk_w * C).astype(BF16)
    new_states = []
    for h in range(n_heads):
        cs = slice(h * HEAD_W, (h + 1) * HEAD_W)
        lhs = jnp.concatenate([qe_hi[:, cs], qe_hi[:, cs], qe_lo[:, cs]], axis=1)
        rhs = jnp.concatenate([ke_hi[:, cs], ke_lo[:, cs], ke_hi[:, cs]], axis=1)
        scores = jnp.where(causal, _dot_nt(lhs, rhs), 0.0)
        vh = v[:, cs].astype(BF16)
        s_cat = jnp.concatenate([state_of(s, h) for s in range(n_seq)], axis=1)
        o_full = _dot(q_state[:, cs], s_cat)
        o_state = jnp.concatenate([o_full[s * C:(s + 1) * C, s * HEAD_W:(s + 1) * HEAD_W] for s in range(n_seq)],
                                  axis=0)
        o = _dot(scores, vh) + o_state
        v_wide = jnp.where(own, jnp.concatenate([vh] * n_seq, axis=1), jnp.zeros((), BF16))
        kv = _dot_tn(k_state[:, cs], v_wide)
        decay = functools.reduce(lambda p, q_: p + q_, [
            lax.dot_general(part[:, cs], pick, (((0,), (0,)), ((), ())), preferred_element_type=F32)
            for part in decay_parts])
        new_states.append(s_cat * decay + kv)
        ms = jnp.mean(o * o, axis=-1, keepdims=True)
        o_scr[:, cs] = o * lax.rsqrt(ms + RMS_EPS)
    return new_states


def _gla_body(mode, n_seq, n_chunk, C, *refs):
    if mode == "gla":
        hq_ref, hk_ref, hv_ref, hg_ref, hlr_ref, wlr_ref, blr_ref, nw_ref, s0_ref, _, o_ref, so_ref, st, o_scr = refs
    else:
        hq_ref, hk_ref, hv_ref, hg_ref, lb_ref, nw_ref, s0_ref, _, o_ref, so_ref, st, o_scr = refs
    n_heads = 4
    n_keys = s0_ref.shape[2]
    tstep = pl.program_id(1)
    batched = n_chunk == 1 and n_seq > 1

    def padded_state(s, h):
        s_in = s0_ref[s, h]
        if n_keys < HEAD_W:
            s_in = jnp.concatenate([s_in, jnp.zeros((HEAD_W - n_keys, HEAD_W), F32)], axis=0)
        return s_in

    if not batched:
        @pl.when(tstep == 0)
        def _():
            for s in range(n_seq):
                for h in range(n_heads):
                    st[s, h] = padded_state(s, h).T

    if mode == "gla":
        q = hq_ref[...] * (GLA_DK ** -0.5)
        k = hk_ref[...]
        z = _dot(hlr_ref[...], wlr_ref[...]) + blr_ref[...]
        g = _log_sigmoid(z) * (1.0 / GLA_TAU)
    else:
        q = _silu(hq_ref[...]) * (HGRN_DK ** -0.5)
        lb = lb_ref[...]
        f = lb + (1.0 - lb) * jax.nn.sigmoid(hk_ref[...])
        k = 1.0 - f
        g = jnp.log(f)
    v = hv_ref[...]
    if batched:
        new_states = _gla_batched_step(q, k, v, g, C, n_seq, n_heads, padded_state, o_scr)
        for s in range(n_seq):
            for h in range(n_heads):
                so_ref[s, h] = new_states[h][0:n_keys, s * HEAD_W:(s + 1) * HEAD_W]
    else:
        causal = _tri(C)
        mid = max(C // 2 - 1, 0)
        b_all = _chunk_cumsum(g, C)
        for s in range(n_seq):
            states = [st[s, h] for h in range(n_heads)]
            for c in range(n_chunk):
                r0 = (s * n_chunk + c) * C
                rs = slice(r0, r0 + C)
                b, qc, kc = b_all[rs, :], q[rs, :], k[rs, :]
                b_last = b[C - 1:C, :]
                b_mid = b[mid:mid + 1, :]
                qe_hi, qe_lo = _split2(qc * jnp.exp(b - b_mid))
                ke_hi, ke_lo = _split2(kc * jnp.exp(b_mid - b))
                q_state = (qc * jnp.exp(b)).astype(BF16)
                k_state = (kc * jnp.exp(b_last - b)).astype(BF16)
                decay = jnp.exp(b_last)
                for h in range(n_heads):
                    cs = slice(h * HEAD_W, (h + 1) * HEAD_W)
                    lhs = jnp.concatenate([qe_hi[:, cs], qe_hi[:, cs], qe_lo[:, cs]], axis=1)
                    rhs = jnp.concatenate([ke_hi[:, cs], ke_lo[:, cs], ke_hi[:, cs]], axis=1)
                    scores = jnp.where(causal, _dot_nt(lhs, rhs), 0.0)
                    vh = v[rs, cs].astype(BF16)
                    o = _dot(scores, vh) + _dot_nt(q_state[:, cs], states[h])
                    states[h] = states[h] * decay[:, cs] + _dot_tn(vh, k_state[:, cs])
                    ms = jnp.mean(o * o, axis=-1, keepdims=True)
                    o_scr[rs, cs] = o * lax.rsqrt(ms + RMS_EPS)
            for h in range(n_heads):
                st[s, h] = states[h]
    o_ref[...] = (o_scr[...] * nw_ref[...] * _silu(hg_ref[...])).astype(BF16)

    if not batched:
        @pl.when(tstep == pl.num_programs(1) - 1)
        def _():
            for s in range(n_seq):
                for h in range(n_heads):
                    so_ref[s, h] = st[s, h].T[0:n_keys, :]


def _seq_layout(n_batch, seq_len, row_off, sample, prompt_tile=PROMPT_TILE):
    if sample:
        n_seq, n_chunk, C = SAMPLE_SEQS, 1, seq_len
        rows = n_seq * C
        grid = (n_batch // n_seq, 1)
        blk0 = row_off // rows
        rb = lambda i, t: blk0 + i
    else:
        n_seq, n_chunk, C = 1, prompt_tile // SCAN_CHUNK, SCAN_CHUNK
        rows = prompt_tile
        tiles = seq_len // rows
        grid = (n_batch, tiles)
        blk0 = row_off // rows
        rb = lambda i, t: blk0 + i * tiles + t
    return n_seq, n_chunk, C, rows, grid, rb


def _gla_call(mode, h, cols, extra, nw, s0, out_buf, n_batch, seq_len, row_off, sample):
    n_seq, n_chunk, C, rows, grid, rb = _seq_layout(n_batch, seq_len, row_off, sample)
    colspec = lambda c0, w: pl.BlockSpec((rows, w), lambda i, t: (rb(i, t), c0 // w))
    fix2 = lambda i, t: (0, 0)
    in_specs = [colspec(cols[0], 512), colspec(cols[1], 512), colspec(cols[2], 512), colspec(cols[3], 512)]
    args = [h, h, h, h]
    if mode == "gla":
        wlr, blr = extra
        in_specs += [colspec(cols[4], LANE), pl.BlockSpec((LANE, 512), fix2), pl.BlockSpec((1, 512), fix2)]
        args += [h, wlr, blr]
    else:
        in_specs += [pl.BlockSpec((1, 512), fix2)]
        args += [extra]
    n_keys = s0.shape[2]
    st_spec = pl.BlockSpec((n_seq, 4, n_keys, HEAD_W), lambda i, t: (i, 0, 0, 0))
    in_specs += [pl.BlockSpec((1, 512), fix2), st_spec, pl.BlockSpec(memory_space=pl.ANY)]
    args += [nw, s0, out_buf]
    o_spec = pl.BlockSpec((rows, 512), lambda i, t: (rb(i, t), 0))
    return pl.pallas_call(
        functools.partial(_gla_body, mode, n_seq, n_chunk, C),
        grid=grid,
        in_specs=in_specs,
        out_specs=[o_spec, st_spec],
        out_shape=[jax.ShapeDtypeStruct(out_buf.shape, out_buf.dtype),
                   jax.ShapeDtypeStruct((n_batch, 4, n_keys, HEAD_W), F32)],
        scratch_shapes=[pltpu.VMEM((n_seq, 4, HEAD_W, HEAD_W), F32), pltpu.VMEM((rows, 512), F32)],
        input_output_aliases={len(args) - 1: 0},
        compiler_params=_cparams(2),
        name=mode + ("_sample" if sample else "_prompt"),
    )(*args)


def _round_bf16(x, on=True):
    return x.astype(BF16).astype(F32) if on else x


def _conf_body(n_seq, L, round_x, round_w, a_ref, gt_ref, hist_ref, w_ref, b_ref, g_ref, be_ref, _, o_ref, co_ref,
               buf, bufr, y_scr, win):
    tstep = pl.program_id(1)
    hist = CONF_WIDTH - 1
    pad = 32 - hist

    @pl.when(tstep == 0)
    def _():
        for s in range(n_seq):
            buf[s, pad:32, :] = hist_ref[s]
            bufr[s, pad:32, :] = _round_bf16(hist_ref[s], round_x)

    u = a_ref[...] * jax.nn.sigmoid(gt_ref[...])
    ur = _round_bf16(u, round_x)
    for s in range(n_seq):
        buf[s, 32:32 + L, :] = u[s * L:(s + 1) * L, :]
        bufr[s, 32:32 + L, :] = ur[s * L:(s + 1) * L, :]
    w = _round_bf16(w_ref[...], round_w)
    for s in range(n_seq):
        acc = jnp.zeros((L, CONF_DIM), F32)
        for phase in range(8):
            n_taps = (CONF_WIDTH - 1 - phase) // 8 + 1
            span = L + 8 * (n_taps - 1)
            win[phase, 0:span, :] = bufr[s, pad + phase:pad + phase + span, :]
            for a in range(n_taps):
                j = 8 * a + phase
                acc = acc + win[phase, 8 * a:8 * a + L, :] * w[j:j + 1, :]
        y_scr[s * L:(s + 1) * L, :] = _silu(_layernorm(acc + b_ref[...], g_ref[...], be_ref[...]))
        tail = buf[s, L + pad:L + 32, :]
        buf[s, pad:32, :] = tail
        tailr = bufr[s, L + pad:L + 32, :]
        bufr[s, pad:32, :] = tailr
    o_ref[...] = y_scr[...].astype(o_ref.dtype)

    @pl.when(tstep == pl.num_programs(1) - 1)
    def _():
        for s in range(n_seq):
            co_ref[s] = buf[s, pad:32, :]


def _conf_call(h, col_a, col_g, cache, w, b, g, be, out_buf, n_batch, seq_len, row_off, sample):
    n_seq, n_chunk, C, rows, grid, rb = _seq_layout(n_batch, seq_len, row_off, sample)
    L = rows // n_seq
    hist = CONF_WIDTH - 1
    colspec = lambda c0: pl.BlockSpec((rows, 512), lambda i, t: (rb(i, t), c0 // 512))
    fix2 = lambda i, t: (0, 0)
    c_spec = pl.BlockSpec((n_seq, hist, CONF_DIM), lambda i, t: (i, 0, 0))
    return pl.pallas_call(
        functools.partial(_conf_body, n_seq, L, True, sample),
        grid=grid,
        in_specs=[colspec(col_a), colspec(col_g), c_spec,
                  pl.BlockSpec((CONF_WIDTH, CONF_DIM), fix2), pl.BlockSpec((1, CONF_DIM), fix2),
                  pl.BlockSpec((1, CONF_DIM), fix2), pl.BlockSpec((1, CONF_DIM), fix2),
                  pl.BlockSpec(memory_space=pl.ANY)],
        out_specs=[pl.BlockSpec((rows, 512), lambda i, t: (rb(i, t), 0)), c_spec],
        out_shape=[jax.ShapeDtypeStruct(out_buf.shape, out_buf.dtype),
                   jax.ShapeDtypeStruct((n_batch, hist, CONF_DIM), F32)],
        scratch_shapes=[pltpu.VMEM((n_seq, 32 + L, CONF_DIM), F32)] * 2 + [pltpu.VMEM((rows, CONF_DIM), F32),
                                                                           pltpu.VMEM((8, L + 24, CONF_DIM), F32)],
        input_output_aliases={7: 0},
        compiler_params=_cparams(2),
        name="conformer" + ("_sample" if sample else "_prompt"),
    )(h, h, cache, w, b, g, be, out_buf)


def _ssd_body(n_seq, n_chunk, C, round_x, round_w, hz_ref, hx_ref, hdt_ref, hist_ref, s0_ref, cw_ref, cb_ref, dtb_ref, alog_ref,
              dvec_ref, nw_ref, _, o_ref, co_ref, so_ref, st, buf, bufr, xbc, y_scr):
    tstep = pl.program_id(1)
    L = n_chunk * C
    hist = SSM_CONV - 1
    pad = 8 - hist
    n_pairs = SSM_HEADS // 2

    @pl.when(tstep == 0)
    def _():
        for s in range(n_seq):
            buf[s, pad:8, :] = hist_ref[s]
            bufr[s, pad:8, :] = _round_bf16(hist_ref[s], round_x)
            for m in range(n_pairs):
                st[s, m] = s0_ref[s, m]

    cw = _round_bf16(cw_ref[...], round_w)
    for s in range(n_seq):
        hx = hx_ref[s * L:(s + 1) * L, :]
        buf[s, 8:8 + L, :] = hx
        bufr[s, 8:8 + L, :] = _round_bf16(hx, round_x)
        acc = jnp.zeros((L, SSM_CONV_DIM), F32)
        for j in range(SSM_CONV):
            acc = acc + bufr[s, pad + j:pad + j + L, :] * cw[j:j + 1, :]
        xbc[s * L:(s + 1) * L, :] = _silu(acc + cb_ref[...])
        tail = buf[s, L + pad:L + 8, :]
        buf[s, pad:8, :] = tail
        tailr = bufr[s, L + pad:L + 8, :]
        bufr[s, pad:8, :] = tailr

    dt = _softplus(hdt_ref[...] + dtb_ref[...])
    la = dt * (-jnp.exp(alog_ref[...]))
    hrow = lax.broadcasted_iota(I32, (LANE, SSM_INNER), 0)
    hcol = lax.broadcasted_iota(I32, (LANE, SSM_INNER), 1) // SSM_HEADDIM
    expand = (hrow == hcol).astype(BF16)
    dtx = functools.reduce(lambda p, q: p + q,
                           [jnp.dot(part, expand, preferred_element_type=F32) for part in _split3(dt)])
    causal = _tri(C)
    tri = causal.astype(BF16)
    lane = lax.broadcasted_iota(I32, (C, HEAD_W), 1)
    bcol_all = _chunk_cumsum(la, C)
    heads_per_group = SSM_HEADS // SSM_GROUPS
    for s in range(n_seq):
        states = [st[s, m] for m in range(n_pairs)]
        for c in range(n_chunk):
            r0 = (s * n_chunk + c) * C
            rs = slice(r0, r0 + C)
            bcol = bcol_all[rs, :]
            brow = functools.reduce(lambda p, q: p + q, [
                lax.dot_general(part, tri, (((0,), (1,)), ((), ())), preferred_element_type=F32)
                for part in _split3(la[rs, :])])
            xs_c = xbc[rs, 0:SSM_INNER]
            v_c = (xs_c * dtx[rs, :]).astype(BF16)
            gmats, bms, cms = [], [], []
            for grp in range(SSM_GROUPS):
                bm = xbc[rs, SSM_INNER + grp * SSM_STATE:SSM_INNER + (grp + 1) * SSM_STATE]
                cm = xbc[rs, SSM_INNER + (SSM_GROUPS + grp) * SSM_STATE:SSM_INNER + (SSM_GROUPS + grp + 1) * SSM_STATE]
                cm_hi, cm_lo = _split2(cm)
                bm_hi, bm_lo = _split2(bm)
                gmats.append(_dot_nt(jnp.concatenate([cm_hi, cm_hi, cm_lo], axis=1),
                                     jnp.concatenate([bm_hi, bm_lo, bm_hi], axis=1)))
                bms.append(bm)
                cms.append(cm)
            for m in range(n_pairs):
                grp = (2 * m) // heads_per_group
                bm, cm, gmat = bms[grp], cms[grp], gmats[grp]
                ps = slice(m * HEAD_W, (m + 1) * HEAD_W)
                vp = v_c[:, ps]
                s_t = states[m]
                o_halves, new_rows = [], []
                for hh in range(2):
                    hd = 2 * m + hh
                    bc = bcol[:, hd:hd + 1]
                    br = brow[hd:hd + 1, :]
                    dec = jnp.where(causal, jnp.exp(jnp.minimum(bc - br, 0.0)), 0.0)
                    b_last = bcol[C - 1:C, hd:hd + 1]
                    o_halves.append(_dot(gmat * dec, vp) + _dot_nt(cm * jnp.exp(bc), s_t))
                    kv = _dot_tn(vp, bm * jnp.exp(b_last - bc))
                    vs = slice(hh * SSM_HEADDIM, (hh + 1) * SSM_HEADDIM)
                    new_rows.append(s_t[vs, :] * jnp.exp(b_last) + kv[vs, :])
                states[m] = jnp.concatenate(new_rows, axis=0)
                o_pair = jnp.where(lane < SSM_HEADDIM, o_halves[0], o_halves[1])
                y_scr[rs, ps] = o_pair + dvec_ref[:, ps] * xs_c[:, ps]
        for m in range(n_pairs):
            st[s, m] = states[m]
    y = y_scr[...] * _silu(hz_ref[...])
    gw = SSM_INNER // SSM_GROUPS
    for grp in range(SSM_GROUPS):
        gs = slice(grp * gw, (grp + 1) * gw)
        yg = y[:, gs]
        ms = jnp.mean(yg * yg, axis=-1, keepdims=True)
        o_ref[:, gs] = (yg * lax.rsqrt(ms + RMS_EPS) * nw_ref[:, gs]).astype(BF16)

    @pl.when(tstep == pl.num_programs(1) - 1)
    def _():
        for s in range(n_seq):
            co_ref[s] = buf[s, pad:8, :]
            for m in range(n_pairs):
                so_ref[s, m] = st[s, m]


def _ssd_call(h, col_z, col_x, col_dt, cache, s0, cw, cb, dtb, alog, dvec, nw, out_buf, n_batch, seq_len, row_off,
              sample):
    n_seq, n_chunk, C, rows, grid, rb = _seq_layout(n_batch, seq_len, row_off, sample, SSD_PROMPT_TILE)
    L = rows // n_seq
    hist = SSM_CONV - 1
    n_pairs = SSM_HEADS // 2
    colspec = lambda c0, w: pl.BlockSpec((rows, w), lambda i, t: (rb(i, t), c0 // w))
    fix2 = lambda i, t: (0, 0)
    c_spec = pl.BlockSpec((n_seq, hist, SSM_CONV_DIM), lambda i, t: (i, 0, 0))
    st_spec = pl.BlockSpec((n_seq, n_pairs, HEAD_W, SSM_STATE), lambda i, t: (i, 0, 0, 0))
    return pl.pallas_call(
        functools.partial(_ssd_body, n_seq, n_chunk, C, sample, True),
        grid=grid,
        in_specs=[colspec(col_z, 512), colspec(col_x, SSM_CONV_DIM), colspec(col_dt, LANE), c_spec, st_spec,
                  pl.BlockSpec((SSM_CONV, SSM_CONV_DIM), fix2), pl.BlockSpec((1, SSM_CONV_DIM), fix2),
                  pl.BlockSpec((1, LANE), fix2), pl.BlockSpec((1, LANE), fix2),
                  pl.BlockSpec((1, SSM_INNER), fix2), pl.BlockSpec((1, SSM_INNER), fix2),
                  pl.BlockSpec(memory_space=pl.ANY)],
        out_specs=[pl.BlockSpec((rows, 512), lambda i, t: (rb(i, t), 0)), c_spec, st_spec],
        out_shape=[jax.ShapeDtypeStruct(out_buf.shape, out_buf.dtype),
                   jax.ShapeDtypeStruct((n_batch, hist, SSM_CONV_DIM), F32),
                   jax.ShapeDtypeStruct((n_batch, n_pairs, HEAD_W, SSM_STATE), F32)],
        scratch_shapes=[pltpu.VMEM((n_seq, n_pairs, HEAD_W, SSM_STATE), F32),
                        pltpu.VMEM((n_seq, 8 + L, SSM_CONV_DIM), F32),
                        pltpu.VMEM((n_seq, 8 + L, SSM_CONV_DIM), F32),
                        pltpu.VMEM((rows, SSM_CONV_DIM), F32),
                        pltpu.VMEM((rows, SSM_INNER), F32)],
        input_output_aliases={11: 0},
        compiler_params=_cparams(2),
        name="ssd" + ("_sample" if sample else "_prompt"),
    )(h, h, h, cache, s0, cw, cb, dtb, alog, dvec, nw, out_buf)


def _pad_heads(w, n_heads, width):
    lead = w.shape[:-1]
    w = w.reshape(lead + (n_heads, width))
    w = jnp.pad(w, [(0, 0)] * len(lead) + [(0, 0), (0, HEAD_W - width)])
    return w.reshape(lead + (n_heads * HEAD_W,))


def _row(v):
    return v.reshape(1, -1).astype(F32)


def kernel(x_prompt, x_sample, state_gla, cache_conformer, state_hgrn, state_ssm, cache_mamba_conv, w_in_even, w_gla_gate_lr, b_gla_gate, gla_norm_w, conf_conv_w, conf_conv_b, conf_ln_g, conf_ln_b, w_out_even, w_in_odd, hgrn_lower_bounds, hgrn_norm_w, mamba_conv_w, mamba_conv_b, mamba_dt_bias, mamba_a_log, mamba_d, mamba_norm_w, w_out_odd, ln1_g, ln1_b, ln2_g, ln2_b, router_w, router_b, expert_w_gate, expert_b_gate, expert_w_up, expert_b_up, expert_w_down, expert_b_down):
    bp, lp, _ = x_prompt.shape
    bs, ls, _ = x_sample.shape
    tp, ts = bp * lp, bs * ls
    x = (x_prompt.reshape(tp, D_MODEL), x_sample.reshape(ts, D_MODEL))

    def router_params(layer):
        return router_w[layer].T.astype(BF16), router_b[layer].astype(F32).reshape(N_EXPERTS, 1)

    def finish_layer(layer, x, mix_a, mix_b, w_out):
        rwt, rb = router_params(layer)
        x1, xp, gates, eidx, rank, counts = _outproj_ln_router(
            mix_a, mix_b, x[0], x[1], w_out[:512].astype(BF16), w_out[512:].astype(BF16),
            _row(ln1_g[layer]), _row(ln1_b[layer]), rwt, rb)
        return _moe(layer, x1, xp, gates, eidx, rank, counts, _row(ln2_g[layer]), _row(ln2_b[layer]),
                    expert_w_gate, expert_b_gate, expert_w_up, expert_b_up, expert_w_down, expert_b_down, tp)

    def mix_buffer():
        return jnp.zeros((tp + ts, 512), BF16)

    wi = w_in_even[0]
    wq, wk, wv, wg, wlr, wglu = jnp.split(wi, [256, 512, 1024, 1536, 1552], axis=1)
    w_even = jnp.concatenate([_pad_heads(wq, GLA_HEADS, GLA_DK), _pad_heads(wk, GLA_HEADS, GLA_DK), wv, wg, wglu,
                              jnp.pad(wlr, ((0, 0), (0, LANE - GLA_RANK)))], axis=1).astype(BF16)
    cols_gla = (0, 512, 1024, 1536, 3072)
    col_a, col_gate = 2048, 2560
    h = _inproj(x[0], x[1], w_even)
    wlr_p = jnp.pad(_pad_heads(w_gla_gate_lr[0], GLA_HEADS, GLA_DK), ((0, LANE - GLA_RANK), (0, 0)))
    blr_p = _row(_pad_heads(b_gla_gate[0], GLA_HEADS, GLA_DK))
    nw = _row(gla_norm_w[0])
    conf_args = (conf_conv_w[0], _row(conf_conv_b[0]), _row(conf_ln_g[0]), _row(conf_ln_b[0]))
    s0_p = jnp.zeros((bp, GLA_HEADS, GLA_DK, HEAD_W), F32)
    s0_s = state_gla[0]
    mix_a, sg_p = _gla_call("gla", h, cols_gla, (wlr_p, blr_p), nw, s0_p, mix_buffer(), bp, lp, 0, False)
    mix_a, sg_s = _gla_call("gla", h, cols_gla, (wlr_p, blr_p), nw, s0_s, mix_a, bs, ls, tp, True)
    mix_b, cc_p = _conf_call(h, col_a, col_gate, jnp.zeros((bp,) + cache_conformer.shape[2:], F32), *conf_args,
                             mix_buffer(), bp, lp, 0, False)
    mix_b, cc_s = _conf_call(h, col_a, col_gate, cache_conformer[0], *conf_args, mix_b, bs, ls, tp, True)
    x = finish_layer(0, x, mix_a, mix_b, w_out_even[0])
    gla_p, gla_s = sg_p[None], sg_s[None]
    conf_p, conf_s = cc_p[None], cc_s[None]

    lb_cum = jnp.cumsum(jax.nn.softmax(hgrn_lower_bounds.astype(F32), axis=0), axis=0)
    lower_bound = _row((lb_cum - lb_cum[0])[1])
    wo = w_in_odd[0]
    w_odd = jnp.concatenate([wo[:, 2560:3584], wo[:, :2560],
                             jnp.pad(wo[:, 3584:], ((0, 0), (0, LANE - SSM_HEADS)))], axis=1).astype(BF16)
    h = _inproj(x[0], x[1], w_odd)
    cols_hgrn = (1024, 1536, 2048, 2560)
    col_z, col_x, col_dt = 3072, 0, 3584
    nw = _row(hgrn_norm_w[0])
    mix_a, sh_p = _gla_call("hgrn", h, cols_hgrn, lower_bound, nw,
                            jnp.zeros((bp, HGRN_HEADS, HEAD_W, HEAD_W), F32), mix_buffer(), bp, lp, 0, False)
    mix_a, sh_s = _gla_call("hgrn", h, cols_hgrn, lower_bound, nw, state_hgrn[0], mix_a, bs, ls, tp, True)

    def pair_states(s):
        return jnp.swapaxes(s, 2, 3).reshape(s.shape[0], SSM_HEADS // 2, HEAD_W, SSM_STATE)

    def unpair_states(s):
        return jnp.swapaxes(s.reshape(s.shape[0], SSM_HEADS, SSM_HEADDIM, SSM_STATE), 2, 3)

    pad8 = lambda v: jnp.pad(v.astype(F32), (0, LANE - SSM_HEADS)).reshape(1, LANE)
    ssd_args = (mamba_conv_w[0], _row(mamba_conv_b[0]), pad8(mamba_dt_bias[0]), pad8(mamba_a_log[0]),
                _row(jnp.repeat(mamba_d[0], SSM_HEADDIM)), _row(mamba_norm_w[0]))
    mix_b, cm_p, ss_p = _ssd_call(h, col_z, col_x, col_dt, jnp.zeros((bp,) + cache_mamba_conv.shape[2:], F32),
                                  jnp.zeros((bp, SSM_HEADS // 2, HEAD_W, SSM_STATE), F32), *ssd_args,
                                  mix_buffer(), bp, lp, 0, False)
    mix_b, cm_s, ss_s = _ssd_call(h, col_z, col_x, col_dt, cache_mamba_conv[0], pair_states(state_ssm[0]),
                                  *ssd_args, mix_b, bs, ls, tp, True)
    y_prompt, y_sample = finish_layer(1, x, mix_a, mix_b, w_out_odd[0])
    y_prompt = y_prompt.reshape(bp, lp, D_MODEL)
    y_sample = y_sample.reshape(bs, ls, D_MODEL)
    return (y_prompt, y_sample, gla_p, gla_s, conf_p, conf_s, sh_p[None], sh_s[None],
            unpair_states(ss_p)[None], unpair_states(ss_s)[None], cm_p[None], cm_s[None])
```

```python
import functools

import jax
import jax.numpy as jnp
from jax import lax
from jax.experimental import pallas as pl
from jax.experimental.pallas import tpu as pltpu
from jax.experimental.pallas import tpu_sc as plsc

F32 = jnp.float32
BF16 = jnp.bfloat16
I32 = jnp.int32
U32 = jnp.uint32
HIGHEST = lax.Precision.HIGHEST

D_MODEL = 1024
DEPTH = 2
DEEPNORM_ALPHA = (2.0 * DEPTH) ** 0.25
LN_EPS = 1e-5
RMS_EPS = 1e-6
LANE = 128
HEAD_W = 128
GLA_HEADS, GLA_DK, GLA_RANK, GLA_TAU = 4, 64, 16, 16.0
CONF_DIM, CONF_WIDTH = 512, 31
HGRN_HEADS, HGRN_DK = 4, 128
SSM_HEADS, SSM_HEADDIM, SSM_STATE, SSM_GROUPS, SSM_CONV = 8, 64, 128, 2, 4
SSM_INNER = SSM_HEADS * SSM_HEADDIM
SSM_CONV_DIM = SSM_INNER + 2 * SSM_GROUPS * SSM_STATE
N_EXPERTS, TOP_K = 32, 4
SWIGLU_ALPHA, SWIGLU_LIMIT = 1.702, 7.0
SCAN_CHUNK = 64
PROMPT_TILE = 512
SSD_PROMPT_TILE = 256
SAMPLE_SEQS = 16
TOKEN_TILE = 512
MOE_ROWS = 512
IN_SLOTS = 3
SC_CORES, SC_SUBCORES = 2, 16
SC_ROWS = 64
SC_SCATTER_ROWS = 32
VMEM_LIMIT = 56 * 1024 * 1024


def _cparams(n_axes):
    return pltpu.CompilerParams(dimension_semantics=("arbitrary",) * n_axes, vmem_limit_bytes=VMEM_LIMIT)


def _silu(x):
    return x * jax.nn.sigmoid(x)


def _softplus(x):
    return jnp.maximum(x, 0.0) + jnp.log(1.0 + jnp.exp(-jnp.abs(x)))


def _log_sigmoid(x):
    return jnp.minimum(x, 0.0) - jnp.log(1.0 + jnp.exp(-jnp.abs(x)))


def _layernorm(y, g, b):
    mu = jnp.mean(y, axis=-1, keepdims=True)
    d = y - mu
    var = jnp.mean(d * d, axis=-1, keepdims=True)
    return d * lax.rsqrt(var + LN_EPS) * g + b


def _dot(a, b):
    return jnp.dot(a.astype(BF16), b.astype(BF16), preferred_element_type=F32)


def _dot_nt(a, b):
    return lax.dot_general(a.astype(BF16), b.astype(BF16), (((1,), (1,)), ((), ())), preferred_element_type=F32)


def _dot_tn(a, b):
    return lax.dot_general(a.astype(BF16), b.astype(BF16), (((0,), (0,)), ((), ())), preferred_element_type=F32)


def _tri(c):
    r = lax.broadcasted_iota(I32, (c, c), 0)
    k = lax.broadcasted_iota(I32, (c, c), 1)
    return r >= k


def _pair_specs(tm, n_first, width):
    return [pl.BlockSpec((tm, width), lambda i: (jnp.minimum(i, n_first - 1), 0)),
            pl.BlockSpec((tm, width), lambda i: (jnp.maximum(i - n_first, 0), 0))]


def _pair_tile(n_first, xa_ref, xb_ref):
    return jnp.where(pl.program_id(0) < n_first, xa_ref[...], xb_ref[...])


def _inproj_body(n_first, xa_ref, xb_ref, w_ref, o_ref):
    xb = _pair_tile(n_first, xa_ref, xb_ref).astype(BF16)
    n = w_ref.shape[1]
    for c0 in range(0, n, 512):
        c1 = min(c0 + 512, n)
        o_ref[:, c0:c1] = jnp.dot(xb, w_ref[:, c0:c1], preferred_element_type=F32)


def _inproj(xa, xb, w):
    k, n = w.shape
    t = xa.shape[0] + xb.shape[0]
    n_first = xa.shape[0] // TOKEN_TILE
    return pl.pallas_call(
        functools.partial(_inproj_body, n_first),
        grid=(t // TOKEN_TILE,),
        in_specs=_pair_specs(TOKEN_TILE, n_first, k) + [pl.BlockSpec((k, n), lambda i: (0, 0))],
        out_specs=pl.BlockSpec((TOKEN_TILE, n), lambda i: (i, 0)),
        out_shape=jax.ShapeDtypeStruct((t, n), F32),
        compiler_params=_cparams(1),
        name="inproj",
    )(xa, xb, w)


def _pack_halves(y):
    half = y.shape[1] // 2
    hi = lax.bitcast_convert_type(y[:, :half].astype(BF16).astype(F32), U32)
    lo = lax.bitcast_convert_type(y[:, half:].astype(BF16).astype(F32), U32)
    return (hi & jnp.uint32(0xFFFF0000)) | (lo >> 16)


def _unpack_halves(w):
    hi = lax.bitcast_convert_type(w & jnp.uint32(0xFFFF0000), F32)
    lo = lax.bitcast_convert_type(w << 16, F32)
    return hi, lo


def _outproj_body(n_first, a_ref, b_ref, xa_ref, xb_ref, wa_ref, wb_ref, g_ref, be_ref, rwt_ref, rb_ref,
                  x1_ref, xp_ref, gate_ref, idx_ref, rank_ref, cnt_ref, carry):
    @pl.when(pl.program_id(0) == 0)
    def _():
        carry[...] = jnp.zeros(carry.shape, F32)

    mix = (jnp.dot(a_ref[...], wa_ref[...], preferred_element_type=F32)
           + jnp.dot(b_ref[...], wb_ref[...], preferred_element_type=F32))
    x1 = _layernorm(DEEPNORM_ALPHA * _pair_tile(n_first, xa_ref, xb_ref) + mix, g_ref[...], be_ref[...])
    x1_ref[...] = x1
    xp_ref[...] = _pack_halves(x1)
    logits = _dot_nt(rwt_ref[...], x1) + rb_ref[...]
    tm = logits.shape[1]
    expert = lax.broadcasted_iota(I32, logits.shape, 0)
    vals, idxs = [], []
    for _ in range(TOP_K):
        m = jnp.max(logits, axis=0, keepdims=True)
        sel = jnp.min(jnp.where(logits == m, expert, N_EXPERTS), axis=0, keepdims=True)
        vals.append(m)
        idxs.append(sel)
        logits = jnp.where(expert == sel, -jnp.inf, logits)
    exps = [jnp.exp(v - vals[0]) for v in vals]
    inv = 1.0 / functools.reduce(lambda p, q: p + q, exps)
    chosen = jnp.zeros(logits.shape, F32)
    for k in range(TOP_K):
        chosen = chosen + (expert == idxs[k]).astype(F32)
    earlier = lax.broadcasted_iota(I32, (tm, tm), 0) < lax.broadcasted_iota(I32, (tm, tm), 1)
    before = carry[...] + jnp.dot(chosen.astype(BF16), earlier.astype(BF16), preferred_element_type=F32)
    choice = lax.broadcasted_iota(I32, (8, tm), 0)
    gates = jnp.zeros((8, tm), F32)
    eidx = jnp.zeros((8, tm), I32)
    ranks = jnp.zeros((8, tm), F32)
    for k in range(TOP_K):
        rk = jnp.sum(jnp.where(expert == idxs[k], before, 0.0), axis=0, keepdims=True)
        gates = jnp.where(choice == k, exps[k] * inv, gates)
        eidx = jnp.where(choice == k, idxs[k], eidx)
        ranks = jnp.where(choice == k, rk, ranks)
    gate_ref[...] = gates
    idx_ref[...] = eidx
    rank_ref[...] = ranks.astype(I32)
    carry[...] = carry[...] + jnp.sum(chosen, axis=1, keepdims=True)
    cnt_ref[...] = carry[...].astype(I32)


def _outproj_ln_router(a, b, xa, xb, wa, wb, g, be, rwt, rb):
    t = xa.shape[0] + xb.shape[0]
    tm = TOKEN_TILE
    n_first = xa.shape[0] // tm
    row = lambda i: (i, 0)
    col = lambda i: (0, i)
    fix = lambda i: (0, 0)
    return pl.pallas_call(
        functools.partial(_outproj_body, n_first),
        grid=(t // tm,),
        in_specs=[pl.BlockSpec((tm, 512), row), pl.BlockSpec((tm, 512), row)] + _pair_specs(tm, n_first, D_MODEL)
        + [pl.BlockSpec((512, D_MODEL), fix), pl.BlockSpec((512, D_MODEL), fix),
           pl.BlockSpec((1, D_MODEL), fix), pl.BlockSpec((1, D_MODEL), fix),
           pl.BlockSpec((N_EXPERTS, D_MODEL), fix), pl.BlockSpec((N_EXPERTS, 1), fix)],
        out_specs=[pl.BlockSpec((tm, D_MODEL), row), pl.BlockSpec((tm, 512), row),
                   pl.BlockSpec((8, tm), col), pl.BlockSpec((8, tm), col), pl.BlockSpec((8, tm), col),
                   pl.BlockSpec((N_EXPERTS, 1), fix)],
        out_shape=[jax.ShapeDtypeStruct((t, D_MODEL), F32), jax.ShapeDtypeStruct((t, 512), U32),
                   jax.ShapeDtypeStruct((8, t), F32), jax.ShapeDtypeStruct((8, t), I32),
                   jax.ShapeDtypeStruct((8, t), I32), jax.ShapeDtypeStruct((N_EXPERTS, 1), I32)],
        scratch_shapes=[pltpu.VMEM((N_EXPERTS, 1), F32)],
        compiler_params=_cparams(1),
        name="outproj_ln_router",
    )(a, b, xa, xb, wa, wb, g, be, rwt, rb)


def _sc_mesh():
    return plsc.VectorSubcoreMesh(core_axis_name="c", subcore_axis_name="s")


def _sc_scatter_rows(src, dest, n_out):
    n_src, w = src.shape
    n_dst = dest.shape[1]
    workers = SC_CORES * SC_SUBCORES
    per_worker = n_src // workers
    chunks = per_worker // SC_SCATTER_ROWS
    assert n_src == workers * chunks * SC_SCATTER_ROWS
    idx = dest.T.reshape(n_dst, workers, chunks, SC_SCATTER_ROWS)

    @functools.partial(pl.kernel, mesh=_sc_mesh(), out_type=jax.ShapeDtypeStruct((n_out, w), src.dtype),
                       scratch_types=[pltpu.VMEM((n_dst, chunks, SC_SCATTER_ROWS), I32)]
                       + [pltpu.VMEM((SC_SCATTER_ROWS, w), src.dtype)] * 2 + [pltpu.SemaphoreType.DMA] * 4)
    def scatter(src_hbm, idx_hbm, out_hbm, idx_v, rows_a, rows_b, sem_ra, sem_rb, sem_wa, sem_wb):
        worker = lax.axis_index("s") * SC_CORES + lax.axis_index("c")
        base = worker * per_worker
        for k in range(n_dst):
            pltpu.sync_copy(idx_hbm.at[k, worker], idx_v.at[k])

        def read(c, rows, sem):
            return pltpu.async_copy(src_hbm.at[pl.ds(pl.multiple_of(base + c * SC_SCATTER_ROWS, 8), SC_SCATTER_ROWS)],
                                    rows, sem)

        def write_all(pending_read, c, rows, sem):
            pending_read.wait()
            return [pltpu.async_copy(rows, out_hbm.at[idx_v.at[k, c]], sem) for k in range(n_dst)]

        @pl.loop(0, chunks // 2)
        def _(p):
            read_a = read(2 * p, rows_a, sem_ra)
            read_b = read(2 * p + 1, rows_b, sem_rb)
            writes = write_all(read_a, 2 * p, rows_a, sem_wa) + write_all(read_b, 2 * p + 1, rows_b, sem_wb)
            for wr in writes:
                wr.wait()

        if chunks % 2:
            for wr in write_all(read(chunks - 1, rows_a, sem_ra), chunks - 1, rows_a, sem_wa):
                wr.wait()

    return scatter(src, idx)


def _sc_gather_rows(table, idx):
    n, w = idx.shape[0], table.shape[1]
    workers = SC_CORES * SC_SUBCORES
    chunks = n // (workers * SC_ROWS)
    assert n == workers * chunks * SC_ROWS and chunks % 2 == 0
    idx = idx.reshape(workers, chunks, SC_ROWS)

    @functools.partial(pl.kernel, mesh=_sc_mesh(), out_type=jax.ShapeDtypeStruct((n, w), table.dtype),
                       scratch_types=[pltpu.VMEM((chunks, SC_ROWS), I32)] + [pltpu.VMEM((SC_ROWS, w), table.dtype)] * 2
                       + [pltpu.SemaphoreType.DMA] * 4)
    def gather(table_hbm, idx_hbm, out_hbm, idx_v, rows_a, rows_b, sem_ra, sem_rb, sem_wa, sem_wb):
        worker = lax.axis_index("s") * SC_CORES + lax.axis_index("c")
        base = worker * (chunks * SC_ROWS)
        pltpu.sync_copy(idx_hbm.at[worker], idx_v)

        def out_rows(c):
            return out_hbm.at[pl.ds(pl.multiple_of(base + c * SC_ROWS, 8), SC_ROWS)]

        @pl.loop(0, chunks // 2)
        def _(p):
            read_a = pltpu.async_copy(table_hbm.at[idx_v.at[2 * p]], rows_a, sem_ra)
            read_b = pltpu.async_copy(table_hbm.at[idx_v.at[2 * p + 1]], rows_b, sem_rb)
            read_a.wait()
            write_a = pltpu.async_copy(rows_a, out_rows(2 * p), sem_wa)
            read_b.wait()
            write_b = pltpu.async_copy(rows_b, out_rows(2 * p + 1), sem_wb)
            write_a.wait()
            write_b.wait()

    return gather(table, idx)


def _experts_body(b0_ref, nb_ref, last_ref, nt_ref, xs_hbm, wg_ref, bg_ref, wu_ref, bu_ref, wd_ref, bd_ref, o_hbm,
                  wg_s, wu_s, wd_s, xbuf, obuf, sem_in, sem_out):
    e = pl.program_id(0)
    first_blk, n_blk, last_valid, n_total = b0_ref[e], nb_ref[e], last_ref[e], nt_ref[0]
    quarter = MOE_ROWS // 4

    def rows_of(g):
        return pl.ds(pl.multiple_of(g * MOE_ROWS, MOE_ROWS), MOE_ROWS)

    def fetch(g, slot):
        return pltpu.make_async_copy(xs_hbm.at[rows_of(g)], xbuf.at[slot], sem_in.at[slot])

    def put(g, slot):
        return pltpu.make_async_copy(obuf.at[slot], o_hbm.at[rows_of(g)], sem_out.at[slot])

    @pl.when((e == 0) & (n_total > 0))
    def _():
        fetch(0, 0).start()

    @pl.when((e == 0) & (n_total > 1))
    def _():
        fetch(1, 1).start()

    @pl.when(n_blk > 0)
    def _():
        wg_s[...] = wg_ref[...].astype(BF16)
        wu_s[...] = wu_ref[...].astype(BF16)
        wd_s[...] = wd_ref[...].astype(BF16)

    def compute(islot, slot, rows):
        half = D_MODEL // 2
        x_hi, x_lo = _unpack_halves(xbuf[islot, 0:rows, :])
        x_hi = x_hi.astype(BF16)
        x_lo = x_lo.astype(BF16)
        g = (jnp.dot(x_hi, wg_s[:half, :], preferred_element_type=F32)
             + jnp.dot(x_lo, wg_s[half:, :], preferred_element_type=F32) + bg_ref[...])
        u = (jnp.dot(x_hi, wu_s[:half, :], preferred_element_type=F32)
             + jnp.dot(x_lo, wu_s[half:, :], preferred_element_type=F32) + bu_ref[...])
        g = jnp.minimum(g, SWIGLU_LIMIT)
        u = jnp.clip(u, -SWIGLU_LIMIT, SWIGLU_LIMIT)
        hmid = (u + 1.0) * (g * jax.nn.sigmoid(SWIGLU_ALPHA * g))
        out = jnp.dot(hmid.astype(BF16), wd_s[...], preferred_element_type=F32) + bd_ref[...]
        obuf[slot, 0:rows, :] = _pack_halves(out)

    def block(j, carry):
        g = first_blk + j
        slot = lax.rem(g, 2)
        islot = lax.rem(g, IN_SLOTS)
        fetch(g, islot).wait()

        @pl.when(g + 2 < n_total)
        def _():
            fetch(g + 2, lax.rem(g + 2, IN_SLOTS)).start()

        @pl.when(g >= 2)
        def _():
            put(g - 2, slot).wait()

        valid = jnp.where(j == n_blk - 1, last_valid, MOE_ROWS)

        for rows in range(quarter, MOE_ROWS + 1, quarter):
            @pl.when((valid > rows - quarter) & (valid <= rows))
            def _():
                compute(islot, slot, rows)
                if rows < MOE_ROWS:
                    obuf[slot, rows:, :] = jnp.zeros((MOE_ROWS - rows, obuf.shape[2]), obuf.dtype)

        put(g, slot).start()
        return carry

    lax.fori_loop(0, n_blk, block, 0)

    @pl.when((e == N_EXPERTS - 1) & (n_total >= 2))
    def _():
        put(n_total - 2, lax.rem(n_total, 2)).wait()

    @pl.when((e == N_EXPERTS - 1) & (n_total >= 1))
    def _():
        put(n_total - 1, lax.rem(n_total - 1, 2)).wait()


def _experts(layer, first_blk, n_blk, last_valid, xs, wg, bg, wu, bu, wd, bd):
    n_rows, w = xs.shape
    wsel = lambda e, b0, nb, lv, nt: (layer, e, 0, 0)
    wspec = pl.BlockSpec((None, None, D_MODEL, D_MODEL), wsel)
    bspec = pl.BlockSpec((None, None, 1, D_MODEL), wsel)
    bias = lambda b: b.reshape(b.shape[0], b.shape[1], 1, b.shape[2])
    return pl.pallas_call(
        _experts_body,
        grid_spec=pltpu.PrefetchScalarGridSpec(
            num_scalar_prefetch=4,
            grid=(N_EXPERTS,),
            in_specs=[pl.BlockSpec(memory_space=pl.ANY), wspec, bspec, wspec, bspec, wspec, bspec],
            out_specs=pl.BlockSpec(memory_space=pl.ANY),
            scratch_shapes=[pltpu.VMEM((D_MODEL, D_MODEL), BF16)] * 3
            + [pltpu.VMEM((IN_SLOTS, MOE_ROWS, w), U32), pltpu.VMEM((2, MOE_ROWS, w), U32),
               pltpu.SemaphoreType.DMA((IN_SLOTS,)), pltpu.SemaphoreType.DMA((2,))],
        ),
        out_shape=jax.ShapeDtypeStruct((n_rows, w), U32),
        compiler_params=_cparams(1),
        name="experts",
    )(first_blk, n_blk, last_valid, jnp.sum(n_blk).reshape(1), xs, wg, bias(bg), wu, bias(bu), wd, bias(bd))


def _combine_body(n_first, o0_ref, o1_ref, o2_ref, o3_ref, gt_ref, x_ref, g_ref, b_ref, ya_ref, yb_ref=None):
    half = D_MODEL // 2
    gates = gt_ref[...]
    hi = jnp.zeros((x_ref.shape[0], half), F32)
    lo = jnp.zeros((x_ref.shape[0], half), F32)
    for k, o_ref in enumerate((o0_ref, o1_ref, o2_ref, o3_ref)):
        h, l = _unpack_halves(o_ref[...])
        gk = gates[:, k:k + 1]
        hi = hi + gk * h
        lo = lo + gk * l
    x = x_ref[...]
    y_hi = DEEPNORM_ALPHA * x[:, :half] + hi
    y_lo = DEEPNORM_ALPHA * x[:, half:] + lo
    mu = (jnp.sum(y_hi, axis=-1, keepdims=True) + jnp.sum(y_lo, axis=-1, keepdims=True)) * (1.0 / D_MODEL)
    d_hi = y_hi - mu
    d_lo = y_lo - mu
    var = (jnp.sum(d_hi * d_hi, axis=-1, keepdims=True) + jnp.sum(d_lo * d_lo, axis=-1, keepdims=True)) * (1.0 / D_MODEL)
    r = lax.rsqrt(var + LN_EPS)
    out_hi = d_hi * r * g_ref[:, :half] + b_ref[:, :half]
    out_lo = d_lo * r * g_ref[:, half:] + b_ref[:, half:]

    def write(y_ref):
        y_ref[:, :half] = out_hi
        y_ref[:, half:] = out_lo

    if yb_ref is None:
        write(ya_ref)
    else:
        pl.when(pl.program_id(0) < n_first)(lambda: write(ya_ref))
        pl.when(pl.program_id(0) >= n_first)(lambda: write(yb_ref))


def _combine_ln(o4, gates, x, g, b, t_first=None):
    t = x.shape[0]
    tm = TOKEN_TILE
    row = lambda i: (i, 0)
    fix = lambda i: (0, 0)
    choice = lambda k: pl.BlockSpec((tm, 512), lambda i: (k * (t // tm) + i, 0))
    if t_first is None:
        n_first = None
        out_specs = pl.BlockSpec((tm, D_MODEL), row)
        out_shape = jax.ShapeDtypeStruct((t, D_MODEL), F32)
    else:
        n_first = t_first // tm
        out_specs = [pl.BlockSpec((tm, D_MODEL), lambda i: (jnp.minimum(i, n_first - 1), 0)),
                     pl.BlockSpec((tm, D_MODEL), lambda i: (jnp.maximum(i - n_first, 0), 0))]
        out_shape = [jax.ShapeDtypeStruct((t_first, D_MODEL), F32), jax.ShapeDtypeStruct((t - t_first, D_MODEL), F32)]
    return pl.pallas_call(
        functools.partial(_combine_body, n_first),
        grid=(t // tm,),
        in_specs=[choice(0), choice(1), choice(2), choice(3), pl.BlockSpec((tm, TOP_K), row),
                  pl.BlockSpec((tm, D_MODEL), row), pl.BlockSpec((1, D_MODEL), fix), pl.BlockSpec((1, D_MODEL), fix)],
        out_specs=out_specs,
        out_shape=out_shape,
        compiler_params=_cparams(1),
        name="combine_ln",
    )(o4, o4, o4, o4, gates, x, g, b)


def _moe(layer, x1, xp, gates, eidx, rank, counts, ln_g, ln_b, wg, bg, wu, bu, wd, bd, t_first=None):
    t = x1.shape[0]
    bm = MOE_ROWS
    n_blocks = t * TOP_K // bm + N_EXPERTS
    n_rows = n_blocks * bm
    cnt = counts[:, 0]
    padded = (cnt + bm - 1) // bm * bm
    pad_end = jnp.cumsum(padded)
    pad_start = pad_end - padded
    e = eidx[:TOP_K]
    start = jnp.sum(jnp.where(e[:, :, None] == jnp.arange(N_EXPERTS, dtype=I32), pad_start, 0), axis=-1)
    dest = (start + rank[:TOP_K]).T
    n_blk = padded // bm
    last_valid = cnt - (n_blk - 1) * bm
    xs = _sc_scatter_rows(xp, dest, n_rows)
    outs = _experts(layer, pad_start // bm, n_blk, last_valid, xs, wg, bg, wu, bu, wd, bd)
    o4 = _sc_gather_rows(outs, dest.T.reshape(-1))
    return _combine_ln(o4, gates[:TOP_K].T, x1, ln_g, ln_b, t_first)


def _split2(x):
    hi = x.astype(BF16)
    return hi, (x - hi.astype(F32)).astype(BF16)


def _split3(x):
    hi = x.astype(BF16)
    rem = x - hi.astype(F32)
    mid = rem.astype(BF16)
    return hi, mid, (rem - mid.astype(F32)).astype(BF16)


def _chunk_cumsum(g, C):
    rows = g.shape[0]
    r = lax.broadcasted_iota(I32, (rows, rows), 0)
    c = lax.broadcasted_iota(I32, (rows, rows), 1)
    tri = ((r >= c) & (r // C == c // C)).astype(BF16)
    hi, mid, lo = _split3(g)
    dot = lambda part: jnp.dot(tri, part, preferred_element_type=F32)
    return dot(hi) + dot(mid) + dot(lo)


def _gla_batched_step(q, k, v, g, C, n_seq, n_heads, state_of, o_scr):
    rows = n_seq * C
    wide = n_seq * HEAD_W
    mid = max(C // 2 - 1, 0)
    r = lax.broadcasted_iota(I32, (rows, rows), 0)
    c = lax.broadcasted_iota(I32, (rows, rows), 1)
    same = (r // C) == (c // C)
    causal = same & (r >= c)
    parts = _split3(g)
    summed = lambda mask: functools.reduce(lambda p, q_: p + q_, [
        jnp.dot(mask.astype(BF16), part, preferred_element_type=F32) for part in parts])
    b = summed(causal)
    b_mid = summed(same & ((c % C) <= mid))
    b_last = summed(same)
    qe_hi, qe_lo = _split2(q * jnp.exp(b - b_mid))
    ke_hi, ke_lo = _split2(k * jnp.exp(b_mid - b))
    q_state = (q * jnp.exp(b)).astype(BF16)
    k_state = (k * jnp.exp(b_last - b)).astype(BF16)
    decay_parts = _split3(jnp.exp(b_last))
    row_w = lax.broadcasted_iota(I32, (rows, wide), 0)
    blk_w = lax.broadcasted_iota(I32, (rows, wide), 1) // HEAD_W
    own = (row_w // C) == blk_w
    pick = (row_w == blk_w * C).astype(BF16)
    new_states = []
    for h in range(n_heads):
        cs = slice(h * HEAD_W, (h + 1) * HEAD_W)
        lhs = jnp.concatenate([qe_hi[:, cs], qe_hi[:, cs], qe_lo[:, cs]], axis=1)
        rhs = jnp.concatenate([ke_hi[:, cs], ke_lo[:, cs], ke_hi[:, cs]], axis=1)
        scores = jnp.where(causal, _dot_nt(lhs, rhs), 0.0)
        vh = v[:, cs].astype(BF16)
        s_cat = jnp.concatenate([state_of(s, h) for s in range(n_seq)], axis=1)
        o_full = _dot(q_state[:, cs], s_cat)
        o_state = jnp.concatenate([o_full[s * C:(s + 1) * C, s * HEAD_W:(s + 1) * HEAD_W] for s in range(n_seq)],
                                  axis=0)
        o = _dot(scores, vh) + o_state
        v_wide = jnp.where(own, jnp.concatenate([vh] * n_seq, axis=1), jnp.zeros((), BF16))
        kv = _dot_tn(k_state[:, cs], v_wide)
        decay = functools.reduce(lambda p, q_: p + q_, [
            lax.dot_general(part[:, cs], pick, (((0,), (0,)), ((), ())), preferred_element_type=F32)
            for part in decay_parts])
        new_states.append(s_cat * decay + kv)
        ms = jnp.mean(o * o, axis=-1, keepdims=True)
        o_scr[:, cs] = o * lax.rsqrt(ms + RMS_EPS)
    return new_states


def _gla_body(mode, n_seq, n_chunk, C, *refs):
    if mode == "gla":
        hq_ref, hk_ref, hv_ref, hg_ref, hlr_ref, wlr_ref, blr_ref, nw_ref, s0_ref, _, o_ref, so_ref, st, o_scr = refs
    else:
        hq_ref, hk_ref, hv_ref, hg_ref, lb_ref, nw_ref, s0_ref, _, o_ref, so_ref, st, o_scr = refs
    n_heads = 4
    n_keys = s0_ref.shape[2]
    tstep = pl.program_id(1)
    batched = n_chunk == 1 and n_seq > 1

    def padded_state(s, h):
        s_in = s0_ref[s, h]
        if n_keys < HEAD_W:
            s_in = jnp.concatenate([s_in, jnp.zeros((HEAD_W - n_keys, HEAD_W), F32)], axis=0)
        return s_in

    if not batched:
        @pl.when(tstep == 0)
        def _():
            for s in range(n_seq):
                for h in range(n_heads):
                    st[s, h] = padded_state(s, h).T

    if mode == "gla":
        q = hq_ref[...] * (GLA_DK ** -0.5)
        k = hk_ref[...]
        z = _dot(hlr_ref[...], wlr_ref[...]) + blr_ref[...]
        g = _log_sigmoid(z) * (1.0 / GLA_TAU)
    else:
        q = _silu(hq_ref[...]) * (HGRN_DK ** -0.5)
        lb = lb_ref[...]
        f = lb + (1.0 - lb) * jax.nn.sigmoid(hk_ref[...])
        k = 1.0 - f
        g = jnp.log(f)
    v = hv_ref[...]
    if batched:
        new_states = _gla_batched_step(q, k, v, g, C, n_seq, n_heads, padded_state, o_scr)
        for s in range(n_seq):
            for h in range(n_heads):
                so_ref[s, h] = new_states[h][0:n_keys, s * HEAD_W:(s + 1) * HEAD_W]
    else:
        causal = _tri(C)
        mid = max(C // 2 - 1, 0)
        b_all = _chunk_cumsum(g, C)
        for s in range(n_seq):
            states = [st[s, h] for h in range(n_heads)]
            for c in range(n_chunk):
                r0 = (s * n_chunk + c) * C
                rs = slice(r0, r0 + C)
                b, qc, kc = b_all[rs, :], q[rs, :], k[rs, :]
                b_last = b[C - 1:C, :]
                b_mid = b[mid:mid + 1, :]
                qe_hi, qe_lo = _split2(qc * jnp.exp(b - b_mid))
                ke_hi, ke_lo = _split2(kc * jnp.exp(b_mid - b))
                q_state = (qc * jnp.exp(b)).astype(BF16)
                k_state = (kc * jnp.exp(b_last - b)).astype(BF16)
                decay = jnp.exp(b_last)
                for h in range(n_heads):
                    cs = slice(h * HEAD_W, (h + 1) * HEAD_W)
                    lhs = jnp.concatenate([qe_hi[:, cs], qe_hi[:, cs], qe_lo[:, cs]], axis=1)
                    rhs = jnp.concatenate([ke_hi[:, cs], ke_lo[:, cs], ke_hi[:, cs]], axis=1)
                    scores = jnp.where(causal, _dot_nt(lhs, rhs), 0.0)
                    vh = v[rs, cs].astype(BF16)
                    o = _dot(scores, vh) + _dot_nt(q_state[:, cs], states[h])
                    states[h] = states[h] * decay[:, cs] + _dot_tn(vh, k_state[:, cs])
                    ms = jnp.mean(o * o, axis=-1, keepdims=True)
                    o_scr[rs, cs] = o * lax.rsqrt(ms + RMS_EPS)
            for h in range(n_heads):
                st[s, h] = states[h]
    o_ref[...] = (o_scr[...] * nw_ref[...] * _silu(hg_ref[...])).astype(BF16)

    if not batched:
        @pl.when(tstep == pl.num_programs(1) - 1)
        def _():
            for s in range(n_seq):
                for h in range(n_heads):
                    so_ref[s, h] = st[s, h].T[0:n_keys, :]


def _seq_layout(n_batch, seq_len, row_off, sample, prompt_tile=PROMPT_TILE):
    if sample:
        n_seq, n_chunk, C = SAMPLE_SEQS, 1, seq_len
        rows = n_seq * C
        grid = (n_batch // n_seq, 1)
        blk0 = row_off // rows
        rb = lambda i, t: blk0 + i
    else:
        n_seq, n_chunk, C = 1, prompt_tile // SCAN_CHUNK, SCAN_CHUNK
        rows = prompt_tile
        tiles = seq_len // rows
        grid = (n_batch, tiles)
        blk0 = row_off // rows
        rb = lambda i, t: blk0 + i * tiles + t
    return n_seq, n_chunk, C, rows, grid, rb


def _gla_call(mode, h, cols, extra, nw, s0, out_buf, n_batch, seq_len, row_off, sample):
    n_seq, n_chunk, C, rows, grid, rb = _seq_layout(n_batch, seq_len, row_off, sample)
    colspec = lambda c0, w: pl.BlockSpec((rows, w), lambda i, t: (rb(i, t), c0 // w))
    fix2 = lambda i, t: (0, 0)
    in_specs = [colspec(cols[0], 512), colspec(cols[1], 512), colspec(cols[2], 512), colspec(cols[3], 512)]
    args = [h, h, h, h]
    if mode == "gla":
        wlr, blr = extra
        in_specs += [colspec(cols[4], LANE), pl.BlockSpec((LANE, 512), fix2), pl.BlockSpec((1, 512), fix2)]
        args += [h, wlr, blr]
    else:
        in_specs += [pl.BlockSpec((1, 512), fix2)]
        args += [extra]
    n_keys = s0.shape[2]
    st_spec = pl.BlockSpec((n_seq, 4, n_keys, HEAD_W), lambda i, t: (i, 0, 0, 0))
    in_specs += [pl.BlockSpec((1, 512), fix2), st_spec, pl.BlockSpec(memory_space=pl.ANY)]
    args += [nw, s0, out_buf]
    o_spec = pl.BlockSpec((rows, 512), lambda i, t: (rb(i, t), 0))
    return pl.pallas_call(
        functools.partial(_gla_body, mode, n_seq, n_chunk, C),
        grid=grid,
        in_specs=in_specs,
        out_specs=[o_spec, st_spec],
        out_shape=[jax.ShapeDtypeStruct(out_buf.shape, out_buf.dtype),
                   jax.ShapeDtypeStruct((n_batch, 4, n_keys, HEAD_W), F32)],
        scratch_shapes=[pltpu.VMEM((n_seq, 4, HEAD_W, HEAD_W), F32), pltpu.VMEM((rows, 512), F32)],
        input_output_aliases={len(args) - 1: 0},
        compiler_params=_cparams(2),
        name=mode + ("_sample" if sample else "_prompt"),
    )(*args)


def _round_bf16(x, on=True):
    return x.astype(BF16).astype(F32) if on else x


def _conf_body(n_seq, L, round_x, round_w, a_ref, gt_ref, hist_ref, w_ref, b_ref, g_ref, be_ref, _, o_ref, co_ref,
               buf, bufr, y_scr, win):
    tstep = pl.program_id(1)
    hist = CONF_WIDTH - 1
    pad = 32 - hist

    @pl.when(tstep == 0)
    def _():
        for s in range(n_seq):
            buf[s, pad:32, :] = hist_ref[s]
            bufr[s, pad:32, :] = _round_bf16(hist_ref[s], round_x)

    u = a_ref[...] * jax.nn.sigmoid(gt_ref[...])
    ur = _round_bf16(u, round_x)
    for s in range(n_seq):
        buf[s, 32:32 + L, :] = u[s * L:(s + 1) * L, :]
        bufr[s, 32:32 + L, :] = ur[s * L:(s + 1) * L, :]
    w = _round_bf16(w_ref[...], round_w)
    for s in range(n_seq):
        acc = jnp.zeros((L, CONF_DIM), F32)
        for phase in range(8):
            n_taps = (CONF_WIDTH - 1 - phase) // 8 + 1
            span = L + 8 * (n_taps - 1)
            win[phase, 0:span, :] = bufr[s, pad + phase:pad + phase + span, :]
            for a in range(n_taps):
                j = 8 * a + phase
                acc = acc + win[phase, 8 * a:8 * a + L, :] * w[j:j + 1, :]
        y_scr[s * L:(s + 1) * L, :] = _silu(_layernorm(acc + b_ref[...], g_ref[...], be_ref[...]))
        tail = buf[s, L + pad:L + 32, :]
        buf[s, pad:32, :] = tail
        tailr = bufr[s, L + pad:L + 32, :]
        bufr[s, pad:32, :] = tailr
    o_ref[...] = y_scr[...].astype(o_ref.dtype)

    @pl.when(tstep == pl.num_programs(1) - 1)
    def _():
        for s in range(n_seq):
            co_ref[s] = buf[s, pad:32, :]


def _conf_call(h, col_a, col_g, cache, w, b, g, be, out_buf, n_batch, seq_len, row_off, sample):
    n_seq, n_chunk, C, rows, grid, rb = _seq_layout(n_batch, seq_len, row_off, sample)
    L = rows // n_seq
    hist = CONF_WIDTH - 1
    colspec = lambda c0: pl.BlockSpec((rows, 512), lambda i, t: (rb(i, t), c0 // 512))
    fix2 = lambda i, t: (0, 0)
    c_spec = pl.BlockSpec((n_seq, hist, CONF_DIM), lambda i, t: (i, 0, 0))
    return pl.pallas_call(
        functools.partial(_conf_body, n_seq, L, True, sample),
        grid=grid,
        in_specs=[colspec(col_a), colspec(col_g), c_spec,
                  pl.BlockSpec((CONF_WIDTH, CONF_DIM), fix2), pl.BlockSpec((1, CONF_DIM), fix2),
                  pl.BlockSpec((1, CONF_DIM), fix2), pl.BlockSpec((1, CONF_DIM), fix2),
                  pl.BlockSpec(memory_space=pl.ANY)],
        out_specs=[pl.BlockSpec((rows, 512), lambda i, t: (rb(i, t), 0)), c_spec],
        out_shape=[jax.ShapeDtypeStruct(out_buf.shape, out_buf.dtype),
                   jax.ShapeDtypeStruct((n_batch, hist, CONF_DIM), F32)],
        scratch_shapes=[pltpu.VMEM((n_seq, 32 + L, CONF_DIM), F32)] * 2 + [pltpu.VMEM((rows, CONF_DIM), F32),
                                                                           pltpu.VMEM((8, L + 24, CONF_DIM), F32)],
        input_output_aliases={7: 0},
        compiler_params=_cparams(2),
        name="conformer" + ("_sample" if sample else "_prompt"),
    )(h, h, cache, w, b, g, be, out_buf)


def _ssd_body(n_seq, n_chunk, C, round_x, round_w, hz_ref, hx_ref, hdt_ref, hist_ref, s0_ref, cw_ref, cb_ref, dtb_ref, alog_ref,
              dvec_ref, nw_ref, _, o_ref, co_ref, so_ref, st, buf, bufr, xbc, y_scr):
    tstep = pl.program_id(1)
    L = n_chunk * C
    hist = SSM_CONV - 1
    pad = 8 - hist
    n_pairs = SSM_HEADS // 2

    @pl.when(tstep == 0)
    def _():
        for s in range(n_seq):
            buf[s, pad:8, :] = hist_ref[s]
            bufr[s, pad:8, :] = _round_bf16(hist_ref[s], round_x)
            for m in range(n_pairs):
                st[s, m] = s0_ref[s, m]

    cw = _round_bf16(cw_ref[...], round_w)
    for s in range(n_seq):
        hx = hx_ref[s * L:(s + 1) * L, :]
        buf[s, 8:8 + L, :] = hx
        bufr[s, 8:8 + L, :] = _round_bf16(hx, round_x)
        acc = jnp.zeros((L, SSM_CONV_DIM), F32)
        for j in range(SSM_CONV):
            acc = acc + bufr[s, pad + j:pad + j + L, :] * cw[j:j + 1, :]
        xbc[s * L:(s + 1) * L, :] = _silu(acc + cb_ref[...])
        tail = buf[s, L + pad:L + 8, :]
        buf[s, pad:8, :] = tail
        tailr = bufr[s, L + pad:L + 8, :]
        bufr[s, pad:8, :] = tailr

    dt = _softplus(hdt_ref[...] + dtb_ref[...])
    la = dt * (-jnp.exp(alog_ref[...]))
    hrow = lax.broadcasted_iota(I32, (LANE, SSM_INNER), 0)
    hcol = lax.broadcasted_iota(I32, (LANE, SSM_INNER), 1) // SSM_HEADDIM
    expand = (hrow == hcol).astype(BF16)
    dtx = functools.reduce(lambda p, q: p + q,
                           [jnp.dot(part, expand, preferred_element_type=F32) for part in _split3(dt)])
    causal = _tri(C)
    tri = causal.astype(BF16)
    lane = lax.broadcasted_iota(I32, (C, HEAD_W), 1)
    bcol_all = _chunk_cumsum(la, C)
    heads_per_group = SSM_HEADS // SSM_GROUPS
    for s in range(n_seq):
        states = [st[s, m] for m in range(n_pairs)]
        for c in range(n_chunk):
            r0 = (s * n_chunk + c) * C
            rs = slice(r0, r0 + C)
            bcol = bcol_all[rs, :]
            brow = functools.reduce(lambda p, q: p + q, [
                lax.dot_general(part, tri, (((0,), (1,)), ((), ())), preferred_element_type=F32)
                for part in _split3(la[rs, :])])
            xs_c = xbc[rs, 0:SSM_INNER]
            v_c = (xs_c * dtx[rs, :]).astype(BF16)
            gmats, bms, cms = [], [], []
            for grp in range(SSM_GROUPS):
                bm = xbc[rs, SSM_INNER + grp * SSM_STATE:SSM_INNER + (grp + 1) * SSM_STATE]
                cm = xbc[rs, SSM_INNER + (SSM_GROUPS + grp) * SSM_STATE:SSM_INNER + (SSM_GROUPS + grp + 1) * SSM_STATE]
                cm_hi, cm_lo = _split2(cm)
                bm_hi, bm_lo = _split2(bm)
                gmats.append(_dot_nt(jnp.concatenate([cm_hi, cm_hi, cm_lo], axis=1),
                                     jnp.concatenate([bm_hi, bm_lo, bm_hi], axis=1)))
                bms.append(bm)
                cms.append(cm)
            for m in range(n_pairs):
                grp = (2 * m) // heads_per_group
                bm, cm, gmat = bms[grp], cms[grp], gmats[grp]
                ps = slice(m * HEAD_W, (m + 1) * HEAD_W)
                vp = v_c[:, ps]
                s_t = states[m]
                o_halves, new_rows = [], []
                for hh in range(2):
                    hd = 2 * m + hh
                    bc = bcol[:, hd:hd + 1]
                    br = brow[hd:hd + 1, :]
                    dec = jnp.where(causal, jnp.exp(jnp.minimum(bc - br, 0.0)), 0.0)
                    b_last = bcol[C - 1:C, hd:hd + 1]
                    o_halves.append(_dot(gmat * dec, vp) + _dot_nt(cm * jnp.exp(bc), s_t))
                    kv = _dot_tn(vp, bm * jnp.exp(b_last - bc))
                    vs = slice(hh * SSM_HEADDIM, (hh + 1) * SSM_HEADDIM)
                    new_rows.append(s_t[vs, :] * jnp.exp(b_last) + kv[vs, :])
                states[m] = jnp.concatenate(new_rows, axis=0)
                o_pair = jnp.where(lane < SSM_HEADDIM, o_halves[0], o_halves[1])
                y_scr[rs, ps] = o_pair + dvec_ref[:, ps] * xs_c[:, ps]
        for m in range(n_pairs):
            st[s, m] = states[m]
    y = y_scr[...] * _silu(hz_ref[...])
    gw = SSM_INNER // SSM_GROUPS
    for grp in range(SSM_GROUPS):
        gs = slice(grp * gw, (grp + 1) * gw)
        yg = y[:, gs]
        ms = jnp.mean(yg * yg, axis=-1, keepdims=True)
        o_ref[:, gs] = (yg * lax.rsqrt(ms + RMS_EPS) * nw_ref[:, gs]).astype(BF16)

    @pl.when(tstep == pl.num_programs(1) - 1)
    def _():
        for s in range(n_seq):
            co_ref[s] = buf[s, pad:8, :]
            for m in range(n_pairs):
                so_ref[s, m] = st[s, m]


def _ssd_call(h, col_z, col_x, col_dt, cache, s0, cw, cb, dtb, alog, dvec, nw, out_buf, n_batch, seq_len, row_off,
              sample):
    n_seq, n_chunk, C, rows, grid, rb = _seq_layout(n_batch, seq_len, row_off, sample, SSD_PROMPT_TILE)
    L = rows // n_seq
    hist = SSM_CONV - 1
    n_pairs = SSM_HEADS // 2
    colspec = lambda c0, w: pl.BlockSpec((rows, w), lambda i, t: (rb(i, t), c0 // w))
    fix2 = lambda i, t: (0, 0)
    c_spec = pl.BlockSpec((n_seq, hist, SSM_CONV_DIM), lambda i, t: (i, 0, 0))
    st_spec = pl.BlockSpec((n_seq, n_pairs, HEAD_W, SSM_STATE), lambda i, t: (i, 0, 0, 0))
    return pl.pallas_call(
        functools.partial(_ssd_body, n_seq, n_chunk, C, sample, True),
        grid=grid,
        in_specs=[colspec(col_z, 512), colspec(col_x, SSM_CONV_DIM), colspec(col_dt, LANE), c_spec, st_spec,
                  pl.BlockSpec((SSM_CONV, SSM_CONV_DIM), fix2), pl.BlockSpec((1, SSM_CONV_DIM), fix2),
                  pl.BlockSpec((1, LANE), fix2), pl.BlockSpec((1, LANE), fix2),
                  pl.BlockSpec((1, SSM_INNER), fix2), pl.BlockSpec((1, SSM_INNER), fix2),
                  pl.BlockSpec(memory_space=pl.ANY)],
        out_specs=[pl.BlockSpec((rows, 512), lambda i, t: (rb(i, t), 0)), c_spec, st_spec],
        out_shape=[jax.ShapeDtypeStruct(out_buf.shape, out_buf.dtype),
                   jax.ShapeDtypeStruct((n_batch, hist, SSM_CONV_DIM), F32),
                   jax.ShapeDtypeStruct((n_batch, n_pairs, HEAD_W, SSM_STATE), F32)],
        scratch_shapes=[pltpu.VMEM((n_seq, n_pairs, HEAD_W, SSM_STATE), F32),
                        pltpu.VMEM((n_seq, 8 + L, SSM_CONV_DIM), F32),
                        pltpu.VMEM((n_seq, 8 + L, SSM_CONV_DIM), F32),
                        pltpu.VMEM((rows, SSM_CONV_DIM), F32),
                        pltpu.VMEM((rows, SSM_INNER), F32)],
        input_output_aliases={11: 0},
        compiler_params=_cparams(2),
        name="ssd" + ("_sample" if sample else "_prompt"),
    )(h, h, h, cache, s0, cw, cb, dtb, alog, dvec, nw, out_buf)


def _pad_heads(w, n_heads, width):
    lead = w.shape[:-1]
    w = w.reshape(lead + (n_heads, width))
    w = jnp.pad(w, [(0, 0)] * len(lead) + [(0, 0), (0, HEAD_W - width)])
    return w.reshape(lead + (n_heads * HEAD_W,))


def _row(v):
    return v.reshape(1, -1).astype(F32)


def kernel(x_prompt, x_sample, state_gla, cache_conformer, state_hgrn, state_ssm, cache_mamba_conv, w_in_even, w_gla_gate_lr, b_gla_gate, gla_norm_w, conf_conv_w, conf_conv_b, conf_ln_g, conf_ln_b, w_out_even, w_in_odd, hgrn_lower_bounds, hgrn_norm_w, mamba_conv_w, mamba_conv_b, mamba_dt_bias, mamba_a_log, mamba_d, mamba_norm_w, w_out_odd, ln1_g, ln1_b, ln2_g, ln2_b, router_w, router_b, expert_w_gate, expert_b_gate, expert_w_up, expert_b_up, expert_w_down, expert_b_down):
    bp, lp, _ = x_prompt.shape
    bs, ls, _ = x_sample.shape
    tp, ts = bp * lp, bs * ls
    x = (x_prompt.reshape(tp, D_MODEL), x_sample.reshape(ts, D_MODEL))

    def router_params(layer):
        return router_w[layer].T.astype(BF16), router_b[layer].astype(F32).reshape(N_EXPERTS, 1)

    def finish_layer(layer, x, mix_a, mix_b, w_out):
        rwt, rb = router_params(layer)
        x1, xp, gates, eidx, rank, counts = _outproj_ln_router(
            mix_a, mix_b, x[0], x[1], w_out[:512].astype(BF16), w_out[512:].astype(BF16),
            _row(ln1_g[layer]), _row(ln1_b[layer]), rwt, rb)
        return _moe(layer, x1, xp, gates, eidx, rank, counts, _row(ln2_g[layer]), _row(ln2_b[layer]),
                    expert_w_gate, expert_b_gate, expert_w_up, expert_b_up, expert_w_down, expert_b_down, tp)

    def mix_buffer():
        return jnp.zeros((tp + ts, 512), BF16)

    wi = w_in_even[0]
    wq, wk, wv, wg, wlr, wglu = jnp.split(wi, [256, 512, 1024, 1536, 1552], axis=1)
    w_even = jnp.concatenate([_pad_heads(wq, GLA_HEADS, GLA_DK), _pad_heads(wk, GLA_HEADS, GLA_DK), wv, wg, wglu,
                              jnp.pad(wlr, ((0, 0), (0, LANE - GLA_RANK)))], axis=1).astype(BF16)
    cols_gla = (0, 512, 1024, 1536, 3072)
    col_a, col_gate = 2048, 2560
    h = _inproj(x[0], x[1], w_even)
    wlr_p = jnp.pad(_pad_heads(w_gla_gate_lr[0], GLA_HEADS, GLA_DK), ((0, LANE - GLA_RANK), (0, 0)))
    blr_p = _row(_pad_heads(b_gla_gate[0], GLA_HEADS, GLA_DK))
    nw = _row(gla_norm_w[0])
    conf_args = (conf_conv_w[0], _row(conf_conv_b[0]), _row(conf_ln_g[0]), _row(conf_ln_b[0]))
    s0_p = jnp.zeros((bp, GLA_HEADS, GLA_DK, HEAD_W), F32)
    s0_s = state_gla[0]
    mix_a, sg_p = _gla_call("gla", h, cols_gla, (wlr_p, blr_p), nw, s0_p, mix_buffer(), bp, lp, 0, False)
    mix_a, sg_s = _gla_call("gla", h, cols_gla, (wlr_p, blr_p), nw, s0_s, mix_a, bs, ls, tp, True)
    mix_b, cc_p = _conf_call(h, col_a, col_gate, jnp.zeros((bp,) + cache_conformer.shape[2:], F32), *conf_args,
                             mix_buffer(), bp, lp, 0, False)
    mix_b, cc_s = _conf_call(h, col_a, col_gate, cache_conformer[0], *conf_args, mix_b, bs, ls, tp, True)
    x = finish_layer(0, x, mix_a, mix_b, w_out_even[0])
    gla_p, gla_s = sg_p[None], sg_s[None]
    conf_p, conf_s = cc_p[None], cc_s[None]

    lb_cum = jnp.cumsum(jax.nn.softmax(hgrn_lower_bounds.astype(F32), axis=0), axis=0)
    lower_bound = _row((lb_cum - lb_cum[0])[1])
    wo = w_in_odd[0]
    w_odd = jnp.concatenate([wo[:, 2560:3584], wo[:, :2560],
                             jnp.pad(wo[:, 3584:], ((0, 0), (0, LANE - SSM_HEADS)))], axis=1).astype(BF16)
    h = _inproj(x[0], x[1], w_odd)
    cols_hgrn = (1024, 1536, 2048, 2560)
    col_z, col_x, col_dt = 3072, 0, 3584
    nw = _row(hgrn_norm_w[0])
    mix_a, sh_p = _gla_call("hgrn", h, cols_hgrn, lower_bound, nw,
                            jnp.zeros((bp, HGRN_HEADS, HEAD_W, HEAD_W), F32), mix_buffer(), bp, lp, 0, False)
    mix_a, sh_s = _gla_call("hgrn", h, cols_hgrn, lower_bound, nw, state_hgrn[0], mix_a, bs, ls, tp, True)

    def pair_states(s):
        return jnp.swapaxes(s, 2, 3).reshape(s.shape[0], SSM_HEADS // 2, HEAD_W, SSM_STATE)

    def unpair_states(s):
        return jnp.swapaxes(s.reshape(s.shape[0], SSM_HEADS, SSM_HEADDIM, SSM_STATE), 2, 3)

    pad8 = lambda v: jnp.pad(v.astype(F32), (0, LANE - SSM_HEADS)).reshape(1, LANE)
    ssd_args = (mamba_conv_w[0], _row(mamba_conv_b[0]), pad8(mamba_dt_bias[0]), pad8(mamba_a_log[0]),
                _row(jnp.repeat(mamba_d[0], SSM_HEADDIM)), _row(mamba_norm_w[0]))
    mix_b, cm_p, ss_p = _ssd_call(h, col_z, col_x, col_dt, jnp.zeros((bp,) + cache_mamba_conv.shape[2:], F32),
                                  jnp.zeros((bp, SSM_HEADS // 2, HEAD_W, SSM_STATE), F32), *ssd_args,
                                  mix_buffer(), bp, lp, 0, False)
    mix_b, cm_s, ss_s = _ssd_call(h, col_z, col_x, col_dt, cache_mamba_conv[0], pair_states(state_ssm[0]),
                                  *ssd_args, mix_b, bs, ls, tp, True)
    y_prompt, y_sample = finish_layer(1, x, mix_a, mix_b, w_out_odd[0])
    y_prompt = y_prompt.reshape(bp, lp, D_MODEL)
    y_sample = y_sample.reshape(bs, ls, D_MODEL)
    return (y_prompt, y_sample, gla_p, gla_s, conf_p, conf_s, sh_p[None], sh_s[None],
            unpair_states(ss_p)[None], unpair_states(ss_s)[None], cm_p[None], cm_s[None])
```

```python
import functools

import jax
import jax.numpy as jnp
from jax import lax
from jax.experimental import pallas as pl
from jax.experimental.pallas import tpu as pltpu
from jax.experimental.pallas import tpu_sc as plsc

F32 = jnp.float32
BF16 = jnp.bfloat16
I32 = jnp.int32
U32 = jnp.uint32
HIGHEST = lax.Precision.HIGHEST

D_MODEL = 1024
DEPTH = 2
DEEPNORM_ALPHA = (2.0 * DEPTH) ** 0.25
LN_EPS = 1e-5
RMS_EPS = 1e-6
LANE = 128
HEAD_W = 128
GLA_HEADS, GLA_DK, GLA_RANK, GLA_TAU = 4, 64, 16, 16.0
CONF_DIM, CONF_WIDTH = 512, 31
HGRN_HEADS, HGRN_DK = 4, 128
SSM_HEADS, SSM_HEADDIM, SSM_STATE, SSM_GROUPS, SSM_CONV = 8, 64, 128, 2, 4
SSM_INNER = SSM_HEADS * SSM_HEADDIM
SSM_CONV_DIM = SSM_INNER + 2 * SSM_GROUPS * SSM_STATE
N_EXPERTS, TOP_K = 32, 4
SWIGLU_ALPHA, SWIGLU_LIMIT = 1.702, 7.0
SCAN_CHUNK = 64
PROMPT_TILE = 512
SSD_PROMPT_TILE = 256
SAMPLE_SEQS = 16
GLA_SAMPLE_SEQS = 16
TOKEN_TILE = 512
MOE_ROWS = 512
IN_SLOTS = 4
SC_CORES, SC_SUBCORES = 2, 16
SC_ROWS = 64
SC_SCATTER_ROWS = 32
VMEM_LIMIT = 56 * 1024 * 1024


def _cparams(n_axes):
    return pltpu.CompilerParams(dimension_semantics=("arbitrary",) * n_axes, vmem_limit_bytes=VMEM_LIMIT)


def _silu(x):
    return x * jax.nn.sigmoid(x)


def _softplus(x):
    return jnp.maximum(x, 0.0) + jnp.log(1.0 + jnp.exp(-jnp.abs(x)))


def _log_sigmoid(x):
    return jnp.minimum(x, 0.0) - jnp.log(1.0 + jnp.exp(-jnp.abs(x)))


def _layernorm(y, g, b):
    mu = jnp.mean(y, axis=-1, keepdims=True)
    d = y - mu
    var = jnp.mean(d * d, axis=-1, keepdims=True)
    return d * lax.rsqrt(var + LN_EPS) * g + b


def _dot(a, b):
    return jnp.dot(a.astype(BF16), b.astype(BF16), preferred_element_type=F32)


def _dot_nt(a, b):
    return lax.dot_general(a.astype(BF16), b.astype(BF16), (((1,), (1,)), ((), ())), preferred_element_type=F32)


def _dot_tn(a, b):
    return lax.dot_general(a.astype(BF16), b.astype(BF16), (((0,), (0,)), ((), ())), preferred_element_type=F32)


def _tri(c):
    r = lax.broadcasted_iota(I32, (c, c), 0)
    k = lax.broadcasted_iota(I32, (c, c), 1)
    return r >= k


def _pair_specs(tm, n_first, width):
    return [pl.BlockSpec((tm, width), lambda i: (jnp.minimum(i, n_first - 1), 0)),
            pl.BlockSpec((tm, width), lambda i: (jnp.maximum(i - n_first, 0), 0))]


def _pair_tile(n_first, xa_ref, xb_ref):
    return jnp.where(pl.program_id(0) < n_first, xa_ref[...], xb_ref[...])


def _inproj_body(n_first, xa_ref, xb_ref, w_ref, o_ref):
    xb = _pair_tile(n_first, xa_ref, xb_ref).astype(BF16)
    n = w_ref.shape[1]
    for c0 in range(0, n, 512):
        c1 = min(c0 + 512, n)
        o_ref[:, c0:c1] = jnp.dot(xb, w_ref[:, c0:c1], preferred_element_type=F32)


def _inproj(xa, xb, w):
    k, n = w.shape
    t = xa.shape[0] + xb.shape[0]
    n_first = xa.shape[0] // TOKEN_TILE
    return pl.pallas_call(
        functools.partial(_inproj_body, n_first),
        grid=(t // TOKEN_TILE,),
        in_specs=_pair_specs(TOKEN_TILE, n_first, k) + [pl.BlockSpec((k, n), lambda i: (0, 0))],
        out_specs=pl.BlockSpec((TOKEN_TILE, n), lambda i: (i, 0)),
        out_shape=jax.ShapeDtypeStruct((t, n), F32),
        compiler_params=_cparams(1),
        name="inproj",
    )(xa, xb, w)


def _pack_halves(y):
    half = y.shape[1] // 2
    hi = lax.bitcast_convert_type(y[:, :half].astype(BF16).astype(F32), U32)
    lo = lax.bitcast_convert_type(y[:, half:].astype(BF16).astype(F32), U32)
    return (hi & jnp.uint32(0xFFFF0000)) | (lo >> 16)


def _unpack_halves(w):
    hi = lax.bitcast_convert_type(w & jnp.uint32(0xFFFF0000), F32)
    lo = lax.bitcast_convert_type(w << 16, F32)
    return hi, lo


def _outproj_body(n_first, a_ref, b_ref, xa_ref, xb_ref, wa_ref, wb_ref, g_ref, be_ref, rwt_ref, rb_ref,
                  x1_ref, xp_ref, gate_ref, idx_ref, rank_ref, cnt_ref, carry):
    @pl.when(pl.program_id(0) == 0)
    def _():
        carry[...] = jnp.zeros(carry.shape, F32)

    mix = (jnp.dot(a_ref[...], wa_ref[...], preferred_element_type=F32)
           + jnp.dot(b_ref[...], wb_ref[...], preferred_element_type=F32))
    x1 = _layernorm(DEEPNORM_ALPHA * _pair_tile(n_first, xa_ref, xb_ref) + mix, g_ref[...], be_ref[...])
    x1_ref[...] = x1
    xp_ref[...] = _pack_halves(x1)
    logits = _dot_nt(rwt_ref[...], x1) + rb_ref[...]
    tm = logits.shape[1]
    expert = lax.broadcasted_iota(I32, logits.shape, 0)
    vals, idxs = [], []
    for _ in range(TOP_K):
        m = jnp.max(logits, axis=0, keepdims=True)
        sel = jnp.min(jnp.where(logits == m, expert, N_EXPERTS), axis=0, keepdims=True)
        vals.append(m)
        idxs.append(sel)
        logits = jnp.where(expert == sel, -jnp.inf, logits)
    exps = [jnp.exp(v - vals[0]) for v in vals]
    inv = 1.0 / functools.reduce(lambda p, q: p + q, exps)
    chosen = jnp.zeros(logits.shape, F32)
    for k in range(TOP_K):
        chosen = chosen + (expert == idxs[k]).astype(F32)
    earlier = lax.broadcasted_iota(I32, (tm, tm), 0) < lax.broadcasted_iota(I32, (tm, tm), 1)
    before = carry[...] + jnp.dot(chosen.astype(BF16), earlier.astype(BF16), preferred_element_type=F32)
    choice = lax.broadcasted_iota(I32, (8, tm), 0)
    gates = jnp.zeros((8, tm), F32)
    eidx = jnp.zeros((8, tm), I32)
    ranks = jnp.zeros((8, tm), F32)
    for k in range(TOP_K):
        rk = jnp.sum(jnp.where(expert == idxs[k], before, 0.0), axis=0, keepdims=True)
        gates = jnp.where(choice == k, exps[k] * inv, gates)
        eidx = jnp.where(choice == k, idxs[k], eidx)
        ranks = jnp.where(choice == k, rk, ranks)
    gate_ref[...] = gates
    idx_ref[...] = eidx
    rank_ref[...] = ranks.astype(I32)
    carry[...] = carry[...] + jnp.sum(chosen, axis=1, keepdims=True)
    cnt_ref[...] = carry[...].astype(I32)


def _outproj_ln_router(a, b, xa, xb, wa, wb, g, be, rwt, rb):
    t = xa.shape[0] + xb.shape[0]
    tm = TOKEN_TILE
    n_first = xa.shape[0] // tm
    row = lambda i: (i, 0)
    col = lambda i: (0, i)
    fix = lambda i: (0, 0)
    return pl.pallas_call(
        functools.partial(_outproj_body, n_first),
        grid=(t // tm,),
        in_specs=[pl.BlockSpec((tm, 512), row), pl.BlockSpec((tm, 512), row)] + _pair_specs(tm, n_first, D_MODEL)
        + [pl.BlockSpec((512, D_MODEL), fix), pl.BlockSpec((512, D_MODEL), fix),
           pl.BlockSpec((1, D_MODEL), fix), pl.BlockSpec((1, D_MODEL), fix),
           pl.BlockSpec((N_EXPERTS, D_MODEL), fix), pl.BlockSpec((N_EXPERTS, 1), fix)],
        out_specs=[pl.BlockSpec((tm, D_MODEL), row), pl.BlockSpec((tm, 512), row),
                   pl.BlockSpec((8, tm), col), pl.BlockSpec((8, tm), col), pl.BlockSpec((8, tm), col),
                   pl.BlockSpec((N_EXPERTS, 1), fix)],
        out_shape=[jax.ShapeDtypeStruct((t, D_MODEL), F32), jax.ShapeDtypeStruct((t, 512), U32),
                   jax.ShapeDtypeStruct((8, t), F32), jax.ShapeDtypeStruct((8, t), I32),
                   jax.ShapeDtypeStruct((8, t), I32), jax.ShapeDtypeStruct((N_EXPERTS, 1), I32)],
        scratch_shapes=[pltpu.VMEM((N_EXPERTS, 1), F32)],
        compiler_params=_cparams(1),
        name="outproj_ln_router",
    )(a, b, xa, xb, wa, wb, g, be, rwt, rb)


def _sc_mesh():
    return plsc.VectorSubcoreMesh(core_axis_name="c", subcore_axis_name="s")


def _sc_scatter_rows(src, dest, n_out):
    n_src, w = src.shape
    n_dst = dest.shape[1]
    workers = SC_CORES * SC_SUBCORES
    per_worker = n_src // workers
    chunks = per_worker // SC_SCATTER_ROWS
    assert n_src == workers * chunks * SC_SCATTER_ROWS
    idx = dest.T.reshape(n_dst, workers, chunks, SC_SCATTER_ROWS)

    @functools.partial(pl.kernel, mesh=_sc_mesh(), out_type=jax.ShapeDtypeStruct((n_out, w), src.dtype),
                       scratch_types=[pltpu.VMEM((n_dst, chunks, SC_SCATTER_ROWS), I32)]
                       + [pltpu.VMEM((SC_SCATTER_ROWS, w), src.dtype)] * 2 + [pltpu.SemaphoreType.DMA] * 4)
    def scatter(src_hbm, idx_hbm, out_hbm, idx_v, rows_a, rows_b, sem_ra, sem_rb, sem_wa, sem_wb):
        worker = lax.axis_index("s") * SC_CORES + lax.axis_index("c")
        base = worker * per_worker
        for k in range(n_dst):
            pltpu.sync_copy(idx_hbm.at[k, worker], idx_v.at[k])

        def read(c, rows, sem):
            return pltpu.async_copy(src_hbm.at[pl.ds(pl.multiple_of(base + c * SC_SCATTER_ROWS, 8), SC_SCATTER_ROWS)],
                                    rows, sem)

        def write_all(pending_read, c, rows, sem):
            pending_read.wait()
            return [pltpu.async_copy(rows, out_hbm.at[idx_v.at[k, c]], sem) for k in range(n_dst)]

        @pl.loop(0, chunks // 2)
        def _(p):
            read_a = read(2 * p, rows_a, sem_ra)
            read_b = read(2 * p + 1, rows_b, sem_rb)
            writes = write_all(read_a, 2 * p, rows_a, sem_wa) + write_all(read_b, 2 * p + 1, rows_b, sem_wb)
            for wr in writes:
                wr.wait()

        if chunks % 2:
            for wr in write_all(read(chunks - 1, rows_a, sem_ra), chunks - 1, rows_a, sem_wa):
                wr.wait()

    return scatter(src, idx)


def _sc_gather_rows(table, idx):
    n, w = idx.shape[0], table.shape[1]
    workers = SC_CORES * SC_SUBCORES
    chunks = n // (workers * SC_ROWS)
    assert n == workers * chunks * SC_ROWS and chunks % 2 == 0
    idx = idx.reshape(workers, chunks, SC_ROWS)

    @functools.partial(pl.kernel, mesh=_sc_mesh(), out_type=jax.ShapeDtypeStruct((n, w), table.dtype),
                       scratch_types=[pltpu.VMEM((chunks, SC_ROWS), I32)] + [pltpu.VMEM((SC_ROWS, w), table.dtype)] * 2
                       + [pltpu.SemaphoreType.DMA] * 4)
    def gather(table_hbm, idx_hbm, out_hbm, idx_v, rows_a, rows_b, sem_ra, sem_rb, sem_wa, sem_wb):
        worker = lax.axis_index("s") * SC_CORES + lax.axis_index("c")
        base = worker * (chunks * SC_ROWS)
        pltpu.sync_copy(idx_hbm.at[worker], idx_v)

        def out_rows(c):
            return out_hbm.at[pl.ds(pl.multiple_of(base + c * SC_ROWS, 8), SC_ROWS)]

        @pl.loop(0, chunks // 2)
        def _(p):
            read_a = pltpu.async_copy(table_hbm.at[idx_v.at[2 * p]], rows_a, sem_ra)
            read_b = pltpu.async_copy(table_hbm.at[idx_v.at[2 * p + 1]], rows_b, sem_rb)
            read_a.wait()
            write_a = pltpu.async_copy(rows_a, out_rows(2 * p), sem_wa)
            read_b.wait()
            write_b = pltpu.async_copy(rows_b, out_rows(2 * p + 1), sem_wb)
            write_a.wait()
            write_b.wait()

    return gather(table, idx)


def _experts_body(b0_ref, nb_ref, last_ref, nt_ref, xs_hbm, wg_ref, bg_ref, wu_ref, bu_ref, wd_ref, bd_ref, o_hbm,
                  wg_s, wu_s, wd_s, xbuf, obuf, sem_in, sem_out):
    e = pl.program_id(0)
    first_blk, n_blk, last_valid, n_total = b0_ref[e], nb_ref[e], last_ref[e], nt_ref[0]
    quarter = MOE_ROWS // 4

    def rows_of(g):
        return pl.ds(pl.multiple_of(g * MOE_ROWS, MOE_ROWS), MOE_ROWS)

    def fetch(g, slot):
        return pltpu.make_async_copy(xs_hbm.at[rows_of(g)], xbuf.at[slot], sem_in.at[slot])

    def put(g, slot):
        return pltpu.make_async_copy(obuf.at[slot], o_hbm.at[rows_of(g)], sem_out.at[slot])

    lead = IN_SLOTS - 1
    for first in range(lead):
        @pl.when((e == 0) & (n_total > first))
        def _():
            fetch(first, first).start()

    @pl.when(n_blk > 0)
    def _():
        wg_s[...] = wg_ref[...].astype(BF16)
        wu_s[...] = wu_ref[...].astype(BF16)
        wd_s[...] = wd_ref[...].astype(BF16)

    def compute(islot, slot, rows):
        half = D_MODEL // 2
        x_hi, x_lo = _unpack_halves(xbuf[islot, 0:rows, :])
        x_hi = x_hi.astype(BF16)
        x_lo = x_lo.astype(BF16)
        g = (jnp.dot(x_hi, wg_s[:half, :], preferred_element_type=F32)
             + jnp.dot(x_lo, wg_s[half:, :], preferred_element_type=F32) + bg_ref[...])
        u = (jnp.dot(x_hi, wu_s[:half, :], preferred_element_type=F32)
             + jnp.dot(x_lo, wu_s[half:, :], preferred_element_type=F32) + bu_ref[...])
        g = jnp.minimum(g, SWIGLU_LIMIT)
        u = jnp.clip(u, -SWIGLU_LIMIT, SWIGLU_LIMIT)
        hmid = (u + 1.0) * (g * jax.nn.sigmoid(SWIGLU_ALPHA * g))
        out = jnp.dot(hmid.astype(BF16), wd_s[...], preferred_element_type=F32) + bd_ref[...]
        obuf[slot, 0:rows, :] = _pack_halves(out)

    def block(j, carry):
        g = first_blk + j
        slot = lax.rem(g, 2)
        islot = lax.rem(g, IN_SLOTS)
        fetch(g, islot).wait()

        @pl.when(g + lead < n_total)
        def _():
            fetch(g + lead, lax.rem(g + lead, IN_SLOTS)).start()

        @pl.when(g >= 2)
        def _():
            put(g - 2, slot).wait()

        valid = jnp.where(j == n_blk - 1, last_valid, MOE_ROWS)

        for rows in range(quarter, MOE_ROWS + 1, quarter):
            @pl.when((valid > rows - quarter) & (valid <= rows))
            def _():
                compute(islot, slot, rows)
                if rows < MOE_ROWS:
                    obuf[slot, rows:, :] = jnp.zeros((MOE_ROWS - rows, obuf.shape[2]), obuf.dtype)

        put(g, slot).start()
        return carry

    lax.fori_loop(0, n_blk, block, 0)

    @pl.when((e == N_EXPERTS - 1) & (n_total >= 2))
    def _():
        put(n_total - 2, lax.rem(n_total, 2)).wait()

    @pl.when((e == N_EXPERTS - 1) & (n_total >= 1))
    def _():
        put(n_total - 1, lax.rem(n_total - 1, 2)).wait()


def _experts(layer, first_blk, n_blk, last_valid, xs, wg, bg, wu, bu, wd, bd):
    n_rows, w = xs.shape
    wsel = lambda e, b0, nb, lv, nt: (layer, e, 0, 0)
    wspec = pl.BlockSpec((None, None, D_MODEL, D_MODEL), wsel)
    bspec = pl.BlockSpec((None, None, 1, D_MODEL), wsel)
    bias = lambda b: b.reshape(b.shape[0], b.shape[1], 1, b.shape[2])
    return pl.pallas_call(
        _experts_body,
        grid_spec=pltpu.PrefetchScalarGridSpec(
            num_scalar_prefetch=4,
            grid=(N_EXPERTS,),
            in_specs=[pl.BlockSpec(memory_space=pl.ANY), wspec, bspec, wspec, bspec, wspec, bspec],
            out_specs=pl.BlockSpec(memory_space=pl.ANY),
            scratch_shapes=[pltpu.VMEM((D_MODEL, D_MODEL), BF16)] * 3
            + [pltpu.VMEM((IN_SLOTS, MOE_ROWS, w), U32), pltpu.VMEM((2, MOE_ROWS, w), U32),
               pltpu.SemaphoreType.DMA((IN_SLOTS,)), pltpu.SemaphoreType.DMA((2,))],
        ),
        out_shape=jax.ShapeDtypeStruct((n_rows, w), U32),
        compiler_params=_cparams(1),
        name="experts",
    )(first_blk, n_blk, last_valid, jnp.sum(n_blk).reshape(1), xs, wg, bias(bg), wu, bias(bu), wd, bias(bd))


def _combine_body(n_first, o0_ref, o1_ref, o2_ref, o3_ref, gt_ref, x_ref, g_ref, b_ref, ya_ref, yb_ref=None):
    half = D_MODEL // 2
    gates = gt_ref[...]
    hi = jnp.zeros((x_ref.shape[0], half), F32)
    lo = jnp.zeros((x_ref.shape[0], half), F32)
    for k, o_ref in enumerate((o0_ref, o1_ref, o2_ref, o3_ref)):
        h, l = _unpack_halves(o_ref[...])
        gk = gates[:, k:k + 1]
        hi = hi + gk * h
        lo = lo + gk * l
    x = x_ref[...]
    y_hi = DEEPNORM_ALPHA * x[:, :half] + hi
    y_lo = DEEPNORM_ALPHA * x[:, half:] + lo
    mu = (jnp.sum(y_hi, axis=-1, keepdims=True) + jnp.sum(y_lo, axis=-1, keepdims=True)) * (1.0 / D_MODEL)
    d_hi = y_hi - mu
    d_lo = y_lo - mu
    var = (jnp.sum(d_hi * d_hi, axis=-1, keepdims=True) + jnp.sum(d_lo * d_lo, axis=-1, keepdims=True)) * (1.0 / D_MODEL)
    r = lax.rsqrt(var + LN_EPS)
    out_hi = d_hi * r * g_ref[:, :half] + b_ref[:, :half]
    out_lo = d_lo * r * g_ref[:, half:] + b_ref[:, half:]

    def write(y_ref):
        y_ref[:, :half] = out_hi
        y_ref[:, half:] = out_lo

    if yb_ref is None:
        write(ya_ref)
    else:
        pl.when(pl.program_id(0) < n_first)(lambda: write(ya_ref))
        pl.when(pl.program_id(0) >= n_first)(lambda: write(yb_ref))


def _combine_ln(o4, gates, x, g, b, t_first=None):
    t = x.shape[0]
    tm = TOKEN_TILE
    row = lambda i: (i, 0)
    fix = lambda i: (0, 0)
    choice = lambda k: pl.BlockSpec((tm, 512), lambda i: (k * (t // tm) + i, 0))
    if t_first is None:
        n_first = None
        out_specs = pl.BlockSpec((tm, D_MODEL), row)
        out_shape = jax.ShapeDtypeStruct((t, D_MODEL), F32)
    else:
        n_first = t_first // tm
        out_specs = [pl.BlockSpec((tm, D_MODEL), lambda i: (jnp.minimum(i, n_first - 1), 0)),
                     pl.BlockSpec((tm, D_MODEL), lambda i: (jnp.maximum(i - n_first, 0), 0))]
        out_shape = [jax.ShapeDtypeStruct((t_first, D_MODEL), F32), jax.ShapeDtypeStruct((t - t_first, D_MODEL), F32)]
    return pl.pallas_call(
        functools.partial(_combine_body, n_first),
        grid=(t // tm,),
        in_specs=[choice(0), choice(1), choice(2), choice(3), pl.BlockSpec((tm, TOP_K), row),
                  pl.BlockSpec((tm, D_MODEL), row), pl.BlockSpec((1, D_MODEL), fix), pl.BlockSpec((1, D_MODEL), fix)],
        out_specs=out_specs,
        out_shape=out_shape,
        compiler_params=_cparams(1),
        name="combine_ln",
    )(o4, o4, o4, o4, gates, x, g, b)


def _moe(layer, x1, xp, gates, eidx, rank, counts, ln_g, ln_b, wg, bg, wu, bu, wd, bd, t_first=None):
    t = x1.shape[0]
    bm = MOE_ROWS
    n_blocks = t * TOP_K // bm + N_EXPERTS
    n_rows = n_blocks * bm
    cnt = counts[:, 0]
    padded = (cnt + bm - 1) // bm * bm
    pad_end = jnp.cumsum(padded)
    pad_start = pad_end - padded
    e = eidx[:TOP_K]
    start = jnp.sum(jnp.where(e[:, :, None] == jnp.arange(N_EXPERTS, dtype=I32), pad_start, 0), axis=-1)
    dest = (start + rank[:TOP_K]).T
    n_blk = padded // bm
    last_valid = cnt - (n_blk - 1) * bm
    xs = _sc_scatter_rows(xp, dest, n_rows)
    outs = _experts(layer, pad_start // bm, n_blk, last_valid, xs, wg, bg, wu, bu, wd, bd)
    o4 = _sc_gather_rows(outs, dest.T.reshape(-1))
    return _combine_ln(o4, gates[:TOP_K].T, x1, ln_g, ln_b, t_first)


def _split2(x):
    hi = x.astype(BF16)
    return hi, (x - hi.astype(F32)).astype(BF16)


def _split3(x):
    hi = x.astype(BF16)
    rem = x - hi.astype(F32)
    mid = rem.astype(BF16)
    return hi, mid, (rem - mid.astype(F32)).astype(BF16)


def _chunk_cumsum(g, C):
    rows = g.shape[0]
    r = lax.broadcasted_iota(I32, (rows, rows), 0)
    c = lax.broadcasted_iota(I32, (rows, rows), 1)
    tri = ((r >= c) & (r // C == c // C)).astype(BF16)
    hi, mid, lo = _split3(g)
    dot = lambda part: jnp.dot(tri, part, preferred_element_type=F32)
    return dot(hi) + dot(mid) + dot(lo)


def _gla_batched_step(q, k, v, g, C, n_seq, n_heads, state_of, o_scr):
    rows = n_seq * C
    wide = n_seq * HEAD_W
    mid = max(C // 2 - 1, 0)
    r = lax.broadcasted_iota(I32, (rows, rows), 0)
    c = lax.broadcasted_iota(I32, (rows, rows), 1)
    same = (r // C) == (c // C)
    causal = same & (r >= c)
    parts = _split3(g)
    summed = lambda mask: functools.reduce(lambda p, q_: p + q_, [
        jnp.dot(mask.astype(BF16), part, preferred_element_type=F32) for part in parts])
    b = summed(causal)
    b_mid = summed(same & ((c % C) <= mid))
    b_last = summed(same)
    qe_hi, qe_lo = _split2(q * jnp.exp(b - b_mid))
    ke_hi, ke_lo = _split2(k * jnp.exp(b_mid - b))
    q_state = (q * jnp.exp(b)).astype(BF16)
    k_state = (k * jnp.exp(b_last - b)).astype(BF16)
    decay_parts = _split3(jnp.exp(b_last))
    row_w = lax.broadcasted_iota(I32, (rows, wide), 0)
    blk_w = lax.broadcasted_iota(I32, (rows, wide), 1) // HEAD_W
    own = (row_w // C) == blk_w
    pick = (row_w == blk_w * C).astype(BF16)
    new_states = []
    for h in range(n_heads):
        cs = slice(h * HEAD_W, (h + 1) * HEAD_W)
        lhs = jnp.concatenate([qe_hi[:, cs], qe_hi[:, cs], qe_lo[:, cs]], axis=1)
        rhs = jnp.concatenate([ke_hi[:, cs], ke_lo[:, cs], ke_hi[:, cs]], axis=1)
        scores = jnp.where(causal, _dot_nt(lhs, rhs), 0.0)
        vh = v[:, cs].astype(BF16)
        s_cat = jnp.concatenate([state_of(s, h) for s in range(n_seq)], axis=1)
        o_full = _dot(q_state[:, cs], s_cat)
        o_state = jnp.concatenate([o_full[s * C:(s + 1) * C, s * HEAD_W:(s + 1) * HEAD_W] for s in range(n_seq)],
                                  axis=0)
        o = _dot(scores, vh) + o_state
        v_wide = jnp.where(own, jnp.concatenate([vh] * n_seq, axis=1), jnp.zeros((), BF16))
        kv = _dot_tn(k_state[:, cs], v_wide)
        decay = functools.reduce(lambda p, q_: p + q_, [
            lax.dot_general(part[:, cs], pick, (((0,), (0,)), ((), ())), preferred_element_type=F32)
            for part in decay_parts])
        new_states.append(s_cat * decay + kv)
        ms = jnp.mean(o * o, axis=-1, keepdims=True)
        o_scr[:, cs] = o * lax.rsqrt(ms + RMS_EPS)
    return new_states


def _gla_body(mode, n_seq, n_chunk, C, *refs):
    if mode == "gla":
        hq_ref, hk_ref, hv_ref, hg_ref, hlr_ref, wlr_ref, blr_ref, nw_ref, s0_ref, _, o_ref, so_ref, st, o_scr = refs
    else:
        hq_ref, hk_ref, hv_ref, hg_ref, lb_ref, nw_ref, s0_ref, _, o_ref, so_ref, st, o_scr = refs
    n_heads = 4
    n_keys = s0_ref.shape[2]
    tstep = pl.program_id(1)
    batched = n_chunk == 1 and n_seq > 1

    def padded_state(s, h):
        s_in = s0_ref[s, h]
        if n_keys < HEAD_W:
            s_in = jnp.concatenate([s_in, jnp.zeros((HEAD_W - n_keys, HEAD_W), F32)], axis=0)
        return s_in

    if not batched:
        @pl.when(tstep == 0)
        def _():
            for s in range(n_seq):
                for h in range(n_heads):
                    st[s, h] = padded_state(s, h).T

    if mode == "gla":
        q = hq_ref[...] * (GLA_DK ** -0.5)
        k = hk_ref[...]
        z = _dot(hlr_ref[...], wlr_ref[...]) + blr_ref[...]
        g = _log_sigmoid(z) * (1.0 / GLA_TAU)
    else:
        q = _silu(hq_ref[...]) * (HGRN_DK ** -0.5)
        lb = lb_ref[...]
        f = lb + (1.0 - lb) * jax.nn.sigmoid(hk_ref[...])
        k = 1.0 - f
        g = jnp.log(f)
    v = hv_ref[...]
    if batched:
        new_states = _gla_batched_step(q, k, v, g, C, n_seq, n_heads, padded_state, o_scr)
        for s in range(n_seq):
            for h in range(n_heads):
                so_ref[s, h] = new_states[h][0:n_keys, s * HEAD_W:(s + 1) * HEAD_W]
    else:
        causal = _tri(C)
        mid = max(C // 2 - 1, 0)
        b_all = _chunk_cumsum(g, C)
        for s in range(n_seq):
            states = [st[s, h] for h in range(n_heads)]
            for c in range(n_chunk):
                r0 = (s * n_chunk + c) * C
                rs = slice(r0, r0 + C)
                b, qc, kc = b_all[rs, :], q[rs, :], k[rs, :]
                b_last = b[C - 1:C, :]
                b_mid = b[mid:mid + 1, :]
                qe_hi, qe_lo = _split2(qc * jnp.exp(b - b_mid))
                ke_hi, ke_lo = _split2(kc * jnp.exp(b_mid - b))
                q_state = (qc * jnp.exp(b)).astype(BF16)
                k_state = (kc * jnp.exp(b_last - b)).astype(BF16)
                decay = jnp.exp(b_last)
                for h in range(n_heads):
                    cs = slice(h * HEAD_W, (h + 1) * HEAD_W)
                    lhs = jnp.concatenate([qe_hi[:, cs], qe_hi[:, cs], qe_lo[:, cs]], axis=1)
                    rhs = jnp.concatenate([ke_hi[:, cs], ke_lo[:, cs], ke_hi[:, cs]], axis=1)
                    scores = jnp.where(causal, _dot_nt(lhs, rhs), 0.0)
                    vh = v[rs, cs].astype(BF16)
                    o = _dot(scores, vh) + _dot_nt(q_state[:, cs], states[h])
                    states[h] = states[h] * decay[:, cs] + _dot_tn(vh, k_state[:, cs])
                    ms = jnp.mean(o * o, axis=-1, keepdims=True)
                    o_scr[rs, cs] = o * lax.rsqrt(ms + RMS_EPS)
            for h in range(n_heads):
                st[s, h] = states[h]
    o_ref[...] = (o_scr[...] * nw_ref[...] * _silu(hg_ref[...])).astype(BF16)

    if not batched:
        @pl.when(tstep == pl.num_programs(1) - 1)
        def _():
            for s in range(n_seq):
                for h in range(n_heads):
                    so_ref[s, h] = st[s, h].T[0:n_keys, :]


def _alias_if_full(out_buf, mix_rows, arg_index):
    return {arg_index: 0} if out_buf.shape == (mix_rows, 512) else {}


def _seq_layout(n_batch, seq_len, row_off, sample, prompt_tile=PROMPT_TILE, sample_seqs=SAMPLE_SEQS):
    if sample:
        n_seq, n_chunk, C = sample_seqs, 1, seq_len
        rows = n_seq * C
        grid = (n_batch // n_seq, 1)
        blk0 = row_off // rows
        rb = lambda i, t: blk0 + i
    else:
        n_seq, n_chunk, C = 1, prompt_tile // SCAN_CHUNK, SCAN_CHUNK
        rows = prompt_tile
        tiles = seq_len // rows
        grid = (n_batch, tiles)
        blk0 = row_off // rows
        rb = lambda i, t: blk0 + i * tiles + t
    return n_seq, n_chunk, C, rows, grid, rb


def _gla_call(mode, h, cols, extra, nw, s0, out_buf, mix_rows, n_batch, seq_len, row_off, sample):
    n_seq, n_chunk, C, rows, grid, rb = _seq_layout(n_batch, seq_len, row_off, sample, sample_seqs=GLA_SAMPLE_SEQS)
    colspec = lambda c0, w: pl.BlockSpec((rows, w), lambda i, t: (rb(i, t), c0 // w))
    fix2 = lambda i, t: (0, 0)
    in_specs = [colspec(cols[0], 512), colspec(cols[1], 512), colspec(cols[2], 512), colspec(cols[3], 512)]
    args = [h, h, h, h]
    if mode == "gla":
        wlr, blr = extra
        in_specs += [colspec(cols[4], LANE), pl.BlockSpec((LANE, 512), fix2), pl.BlockSpec((1, 512), fix2)]
        args += [h, wlr, blr]
    else:
        in_specs += [pl.BlockSpec((1, 512), fix2)]
        args += [extra]
    n_keys = s0.shape[2]
    st_spec = pl.BlockSpec((n_seq, 4, n_keys, HEAD_W), lambda i, t: (i, 0, 0, 0))
    in_specs += [pl.BlockSpec((1, 512), fix2), st_spec, pl.BlockSpec(memory_space=pl.ANY)]
    args += [nw, s0, out_buf]
    o_spec = pl.BlockSpec((rows, 512), lambda i, t: (rb(i, t), 0))
    return pl.pallas_call(
        functools.partial(_gla_body, mode, n_seq, n_chunk, C),
        grid=grid,
        in_specs=in_specs,
        out_specs=[o_spec, st_spec],
        out_shape=[jax.ShapeDtypeStruct((mix_rows, 512), BF16),
                   jax.ShapeDtypeStruct((n_batch, 4, n_keys, HEAD_W), F32)],
        scratch_shapes=[pltpu.VMEM((n_seq, 4, HEAD_W, HEAD_W), F32), pltpu.VMEM((rows, 512), F32)],
        input_output_aliases=_alias_if_full(out_buf, mix_rows, len(args) - 1),
        compiler_params=_cparams(2),
        name=mode + ("_sample" if sample else "_prompt"),
    )(*args)


def _round_bf16(x, on=True):
    return x.astype(BF16).astype(F32) if on else x


def _conf_body(n_seq, L, round_x, round_w, a_ref, gt_ref, hist_ref, w_ref, b_ref, g_ref, be_ref, _, o_ref, co_ref,
               buf, bufr, y_scr, win):
    tstep = pl.program_id(1)
    hist = CONF_WIDTH - 1
    pad = 32 - hist

    @pl.when(tstep == 0)
    def _():
        for s in range(n_seq):
            buf[s, pad:32, :] = hist_ref[s]
            bufr[s, pad:32, :] = _round_bf16(hist_ref[s], round_x)

    u = a_ref[...] * jax.nn.sigmoid(gt_ref[...])
    ur = _round_bf16(u, round_x)
    for s in range(n_seq):
        buf[s, 32:32 + L, :] = u[s * L:(s + 1) * L, :]
        bufr[s, 32:32 + L, :] = ur[s * L:(s + 1) * L, :]
    w = _round_bf16(w_ref[...], round_w)
    for s in range(n_seq):
        acc = jnp.zeros((L, CONF_DIM), F32)
        for phase in range(8):
            n_taps = (CONF_WIDTH - 1 - phase) // 8 + 1
            span = L + 8 * (n_taps - 1)
            win[phase, 0:span, :] = bufr[s, pad + phase:pad + phase + span, :]
            for a in range(n_taps):
                j = 8 * a + phase
                acc = acc + win[phase, 8 * a:8 * a + L, :] * w[j:j + 1, :]
        y_scr[s * L:(s + 1) * L, :] = _silu(_layernorm(acc + b_ref[...], g_ref[...], be_ref[...]))
        tail = buf[s, L + pad:L + 32, :]
        buf[s, pad:32, :] = tail
        tailr = bufr[s, L + pad:L + 32, :]
        bufr[s, pad:32, :] = tailr
    o_ref[...] = y_scr[...].astype(o_ref.dtype)

    @pl.when(tstep == pl.num_programs(1) - 1)
    def _():
        for s in range(n_seq):
            co_ref[s] = buf[s, pad:32, :]


def _conf_call(h, col_a, col_g, cache, w, b, g, be, out_buf, mix_rows, n_batch, seq_len, row_off, sample):
    n_seq, n_chunk, C, rows, grid, rb = _seq_layout(n_batch, seq_len, row_off, sample)
    L = rows // n_seq
    hist = CONF_WIDTH - 1
    colspec = lambda c0: pl.BlockSpec((rows, 512), lambda i, t: (rb(i, t), c0 // 512))
    fix2 = lambda i, t: (0, 0)
    c_spec = pl.BlockSpec((n_seq, hist, CONF_DIM), lambda i, t: (i, 0, 0))
    return pl.pallas_call(
        functools.partial(_conf_body, n_seq, L, True, sample),
        grid=grid,
        in_specs=[colspec(col_a), colspec(col_g), c_spec,
                  pl.BlockSpec((CONF_WIDTH, CONF_DIM), fix2), pl.BlockSpec((1, CONF_DIM), fix2),
                  pl.BlockSpec((1, CONF_DIM), fix2), pl.BlockSpec((1, CONF_DIM), fix2),
                  pl.BlockSpec(memory_space=pl.ANY)],
        out_specs=[pl.BlockSpec((rows, 512), lambda i, t: (rb(i, t), 0)), c_spec],
        out_shape=[jax.ShapeDtypeStruct((mix_rows, 512), BF16),
                   jax.ShapeDtypeStruct((n_batch, hist, CONF_DIM), F32)],
        scratch_shapes=[pltpu.VMEM((n_seq, 32 + L, CONF_DIM), F32)] * 2 + [pltpu.VMEM((rows, CONF_DIM), F32),
                                                                           pltpu.VMEM((8, L + 24, CONF_DIM), F32)],
        input_output_aliases=_alias_if_full(out_buf, mix_rows, 7),
        compiler_params=_cparams(2),
        name="conformer" + ("_sample" if sample else "_prompt"),
    )(h, h, cache, w, b, g, be, out_buf)


def _ssd_body(n_seq, n_chunk, C, round_x, round_w, hz_ref, hx_ref, hdt_ref, hist_ref, s0_ref, cw_ref, cb_ref, dtb_ref, alog_ref,
              dvec_ref, nw_ref, _, o_ref, co_ref, so_ref, st, buf, bufr, xbc, y_scr):
    tstep = pl.program_id(1)
    L = n_chunk * C
    hist = SSM_CONV - 1
    pad = 8 - hist
    n_pairs = SSM_HEADS // 2

    @pl.when(tstep == 0)
    def _():
        for s in range(n_seq):
            buf[s, pad:8, :] = hist_ref[s]
            bufr[s, pad:8, :] = _round_bf16(hist_ref[s], round_x)
            for m in range(n_pairs):
                st[s, m] = s0_ref[s, m]

    cw = _round_bf16(cw_ref[...], round_w)
    for s in range(n_seq):
        hx = hx_ref[s * L:(s + 1) * L, :]
        buf[s, 8:8 + L, :] = hx
        bufr[s, 8:8 + L, :] = _round_bf16(hx, round_x)
        acc = jnp.zeros((L, SSM_CONV_DIM), F32)
        for j in range(SSM_CONV):
            acc = acc + bufr[s, pad + j:pad + j + L, :] * cw[j:j + 1, :]
        xbc[s * L:(s + 1) * L, :] = _silu(acc + cb_ref[...])
        tail = buf[s, L + pad:L + 8, :]
        buf[s, pad:8, :] = tail
        tailr = bufr[s, L + pad:L + 8, :]
        bufr[s, pad:8, :] = tailr

    dt = _softplus(hdt_ref[...] + dtb_ref[...])
    la = dt * (-jnp.exp(alog_ref[...]))
    hrow = lax.broadcasted_iota(I32, (LANE, SSM_INNER), 0)
    hcol = lax.broadcasted_iota(I32, (LANE, SSM_INNER), 1) // SSM_HEADDIM
    expand = (hrow == hcol).astype(BF16)
    dtx = functools.reduce(lambda p, q: p + q,
                           [jnp.dot(part, expand, preferred_element_type=F32) for part in _split3(dt)])
    causal = _tri(C)
    tri = causal.astype(BF16)
    lane = lax.broadcasted_iota(I32, (C, HEAD_W), 1)
    bcol_all = _chunk_cumsum(la, C)
    heads_per_group = SSM_HEADS // SSM_GROUPS
    for s in range(n_seq):
        states = [st[s, m] for m in range(n_pairs)]
        for c in range(n_chunk):
            r0 = (s * n_chunk + c) * C
            rs = slice(r0, r0 + C)
            bcol = bcol_all[rs, :]
            brow = functools.reduce(lambda p, q: p + q, [
                lax.dot_general(part, tri, (((0,), (1,)), ((), ())), preferred_element_type=F32)
                for part in _split3(la[rs, :])])
            xs_c = xbc[rs, 0:SSM_INNER]
            v_c = (xs_c * dtx[rs, :]).astype(BF16)
            gmats, bms, cms = [], [], []
            for grp in range(SSM_GROUPS):
                bm = xbc[rs, SSM_INNER + grp * SSM_STATE:SSM_INNER + (grp + 1) * SSM_STATE]
                cm = xbc[rs, SSM_INNER + (SSM_GROUPS + grp) * SSM_STATE:SSM_INNER + (SSM_GROUPS + grp + 1) * SSM_STATE]
                cm_hi, cm_lo = _split2(cm)
                bm_hi, bm_lo = _split2(bm)
                gmats.append(_dot_nt(jnp.concatenate([cm_hi, cm_hi, cm_lo], axis=1),
                                     jnp.concatenate([bm_hi, bm_lo, bm_hi], axis=1)))
                bms.append(bm)
                cms.append(cm)
            for m in range(n_pairs):
                grp = (2 * m) // heads_per_group
                bm, cm, gmat = bms[grp], cms[grp], gmats[grp]
                ps = slice(m * HEAD_W, (m + 1) * HEAD_W)
                vp = v_c[:, ps]
                s_t = states[m]
                o_halves, new_rows = [], []
                for hh in range(2):
                    hd = 2 * m + hh
                    bc = bcol[:, hd:hd + 1]
                    br = brow[hd:hd + 1, :]
                    dec = jnp.where(causal, jnp.exp(jnp.minimum(bc - br, 0.0)), 0.0)
                    b_last = bcol[C - 1:C, hd:hd + 1]
                    o_halves.append(_dot(gmat * dec, vp) + _dot_nt(cm * jnp.exp(bc), s_t))
                    kv = _dot_tn(vp, bm * jnp.exp(b_last - bc))
                    vs = slice(hh * SSM_HEADDIM, (hh + 1) * SSM_HEADDIM)
                    new_rows.append(s_t[vs, :] * jnp.exp(b_last) + kv[vs, :])
                states[m] = jnp.concatenate(new_rows, axis=0)
                o_pair = jnp.where(lane < SSM_HEADDIM, o_halves[0], o_halves[1])
                y_scr[rs, ps] = o_pair + dvec_ref[:, ps] * xs_c[:, ps]
        for m in range(n_pairs):
            st[s, m] = states[m]
    y = y_scr[...] * _silu(hz_ref[...])
    gw = SSM_INNER // SSM_GROUPS
    for grp in range(SSM_GROUPS):
        gs = slice(grp * gw, (grp + 1) * gw)
        yg = y[:, gs]
        ms = jnp.mean(yg * yg, axis=-1, keepdims=True)
        o_ref[:, gs] = (yg * lax.rsqrt(ms + RMS_EPS) * nw_ref[:, gs]).astype(BF16)

    @pl.when(tstep == pl.num_programs(1) - 1)
    def _():
        for s in range(n_seq):
            co_ref[s] = buf[s, pad:8, :]
            for m in range(n_pairs):
                so_ref[s, m] = st[s, m]


def _ssd_call(h, col_z, col_x, col_dt, cache, s0, cw, cb, dtb, alog, dvec, nw, out_buf, mix_rows, n_batch, seq_len,
              row_off, sample):
    n_seq, n_chunk, C, rows, grid, rb = _seq_layout(n_batch, seq_len, row_off, sample, SSD_PROMPT_TILE)
    L = rows // n_seq
    hist = SSM_CONV - 1
    n_pairs = SSM_HEADS // 2
    colspec = lambda c0, w: pl.BlockSpec((rows, w), lambda i, t: (rb(i, t), c0 // w))
    fix2 = lambda i, t: (0, 0)
    c_spec = pl.BlockSpec((n_seq, hist, SSM_CONV_DIM), lambda i, t: (i, 0, 0))
    st_spec = pl.BlockSpec((n_seq, n_pairs, HEAD_W, SSM_STATE), lambda i, t: (i, 0, 0, 0))
    return pl.pallas_call(
        functools.partial(_ssd_body, n_seq, n_chunk, C, sample, True),
        grid=grid,
        in_specs=[colspec(col_z, 512), colspec(col_x, SSM_CONV_DIM), colspec(col_dt, LANE), c_spec, st_spec,
                  pl.BlockSpec((SSM_CONV, SSM_CONV_DIM), fix2), pl.BlockSpec((1, SSM_CONV_DIM), fix2),
                  pl.BlockSpec((1, LANE), fix2), pl.BlockSpec((1, LANE), fix2),
                  pl.BlockSpec((1, SSM_INNER), fix2), pl.BlockSpec((1, SSM_INNER), fix2),
                  pl.BlockSpec(memory_space=pl.ANY)],
        out_specs=[pl.BlockSpec((rows, 512), lambda i, t: (rb(i, t), 0)), c_spec, st_spec],
        out_shape=[jax.ShapeDtypeStruct((mix_rows, 512), BF16),
                   jax.ShapeDtypeStruct((n_batch, hist, SSM_CONV_DIM), F32),
                   jax.ShapeDtypeStruct((n_batch, n_pairs, HEAD_W, SSM_STATE), F32)],
        scratch_shapes=[pltpu.VMEM((n_seq, n_pairs, HEAD_W, SSM_STATE), F32),
                        pltpu.VMEM((n_seq, 8 + L, SSM_CONV_DIM), F32),
                        pltpu.VMEM((n_seq, 8 + L, SSM_CONV_DIM), F32),
                        pltpu.VMEM((rows, SSM_CONV_DIM), F32),
                        pltpu.VMEM((rows, SSM_INNER), F32)],
        input_output_aliases=_alias_if_full(out_buf, mix_rows, 11),
        compiler_params=_cparams(2),
        name="ssd" + ("_sample" if sample else "_prompt"),
    )(h, h, h, cache, s0, cw, cb, dtb, alog, dvec, nw, out_buf)


def _pad_heads(w, n_heads, width):
    lead = w.shape[:-1]
    w = w.reshape(lead + (n_heads, width))
    w = jnp.pad(w, [(0, 0)] * len(lead) + [(0, 0), (0, HEAD_W - width)])
    return w.reshape(lead + (n_heads * HEAD_W,))


def _row(v):
    return v.reshape(1, -1).astype(F32)


def kernel(x_prompt, x_sample, state_gla, cache_conformer, state_hgrn, state_ssm, cache_mamba_conv, w_in_even, w_gla_gate_lr, b_gla_gate, gla_norm_w, conf_conv_w, conf_conv_b, conf_ln_g, conf_ln_b, w_out_even, w_in_odd, hgrn_lower_bounds, hgrn_norm_w, mamba_conv_w, mamba_conv_b, mamba_dt_bias, mamba_a_log, mamba_d, mamba_norm_w, w_out_odd, ln1_g, ln1_b, ln2_g, ln2_b, router_w, router_b, expert_w_gate, expert_b_gate, expert_w_up, expert_b_up, expert_w_down, expert_b_down):
    bp, lp, _ = x_prompt.shape
    bs, ls, _ = x_sample.shape
    tp, ts = bp * lp, bs * ls
    x = (x_prompt.reshape(tp, D_MODEL), x_sample.reshape(ts, D_MODEL))

    def router_params(layer):
        return router_w[layer].T.astype(BF16), router_b[layer].astype(F32).reshape(N_EXPERTS, 1)

    def finish_layer(layer, x, mix_a, mix_b, w_out):
        rwt, rb = router_params(layer)
        x1, xp, gates, eidx, rank, counts = _outproj_ln_router(
            mix_a, mix_b, x[0], x[1], w_out[:512].astype(BF16), w_out[512:].astype(BF16),
            _row(ln1_g[layer]), _row(ln1_b[layer]), rwt, rb)
        return _moe(layer, x1, xp, gates, eidx, rank, counts, _row(ln2_g[layer]), _row(ln2_b[layer]),
                    expert_w_gate, expert_b_gate, expert_w_up, expert_b_up, expert_w_down, expert_b_down, tp)

    t_all = tp + ts
    fresh = jnp.zeros((t_all, 512), BF16)

    wi = w_in_even[0]
    wq, wk, wv, wg, wlr, wglu = jnp.split(wi, [256, 512, 1024, 1536, 1552], axis=1)
    w_even = jnp.concatenate([_pad_heads(wq, GLA_HEADS, GLA_DK), _pad_heads(wk, GLA_HEADS, GLA_DK), wv, wg, wglu,
                              jnp.pad(wlr, ((0, 0), (0, LANE - GLA_RANK)))], axis=1).astype(BF16)
    cols_gla = (0, 512, 1024, 1536, 3072)
    col_a, col_gate = 2048, 2560
    h = _inproj(x[0], x[1], w_even)
    wlr_p = jnp.pad(_pad_heads(w_gla_gate_lr[0], GLA_HEADS, GLA_DK), ((0, LANE - GLA_RANK), (0, 0)))
    blr_p = _row(_pad_heads(b_gla_gate[0], GLA_HEADS, GLA_DK))
    nw = _row(gla_norm_w[0])
    conf_args = (conf_conv_w[0], _row(conf_conv_b[0]), _row(conf_ln_g[0]), _row(conf_ln_b[0]))
    s0_p = jnp.zeros((bp, GLA_HEADS, GLA_DK, HEAD_W), F32)
    s0_s = state_gla[0]
    mix_a, sg_p = _gla_call("gla", h, cols_gla, (wlr_p, blr_p), nw, s0_p, fresh, t_all, bp, lp, 0, False)
    mix_a, sg_s = _gla_call("gla", h, cols_gla, (wlr_p, blr_p), nw, s0_s, mix_a, t_all, bs, ls, tp, True)
    mix_b, cc_p = _conf_call(h, col_a, col_gate, jnp.zeros((bp,) + cache_conformer.shape[2:], F32), *conf_args,
                             fresh, t_all, bp, lp, 0, False)
    mix_b, cc_s = _conf_call(h, col_a, col_gate, cache_conformer[0], *conf_args, mix_b, t_all, bs, ls, tp, True)
    x = finish_layer(0, x, mix_a, mix_b, w_out_even[0])
    gla_p, gla_s = sg_p[None], sg_s[None]
    conf_p, conf_s = cc_p[None], cc_s[None]

    lb_cum = jnp.cumsum(jax.nn.softmax(hgrn_lower_bounds.astype(F32), axis=0), axis=0)
    lower_bound = _row((lb_cum - lb_cum[0])[1])
    wo = w_in_odd[0]
    w_odd = jnp.concatenate([wo[:, 2560:3584], wo[:, :2560],
                             jnp.pad(wo[:, 3584:], ((0, 0), (0, LANE - SSM_HEADS)))], axis=1).astype(BF16)
    h = _inproj(x[0], x[1], w_odd)
    cols_hgrn = (1024, 1536, 2048, 2560)
    col_z, col_x, col_dt = 3072, 0, 3584
    nw = _row(hgrn_norm_w[0])
    mix_a, sh_p = _gla_call("hgrn", h, cols_hgrn, lower_bound, nw,
                            jnp.zeros((bp, HGRN_HEADS, HEAD_W, HEAD_W), F32), fresh, t_all, bp, lp, 0, False)
    mix_a, sh_s = _gla_call("hgrn", h, cols_hgrn, lower_bound, nw, state_hgrn[0], mix_a, t_all, bs, ls, tp, True)

    def pair_states(s):
        return jnp.swapaxes(s, 2, 3).reshape(s.shape[0], SSM_HEADS // 2, HEAD_W, SSM_STATE)

    def unpair_states(s):
        return jnp.swapaxes(s.reshape(s.shape[0], SSM_HEADS, SSM_HEADDIM, SSM_STATE), 2, 3)

    pad8 = lambda v: jnp.pad(v.astype(F32), (0, LANE - SSM_HEADS)).reshape(1, LANE)
    ssd_args = (mamba_conv_w[0], _row(mamba_conv_b[0]), pad8(mamba_dt_bias[0]), pad8(mamba_a_log[0]),
                _row(jnp.repeat(mamba_d[0], SSM_HEADDIM)), _row(mamba_norm_w[0]))
    mix_b, cm_p, ss_p = _ssd_call(h, col_z, col_x, col_dt, jnp.zeros((bp,) + cache_mamba_conv.shape[2:], F32),
                                  jnp.zeros((bp, SSM_HEADS // 2, HEAD_W, SSM_STATE), F32), *ssd_args,
                                  fresh, t_all, bp, lp, 0, False)
    mix_b, cm_s, ss_s = _ssd_call(h, col_z, col_x, col_dt, cache_mamba_conv[0], pair_states(state_ssm[0]),
                                  *ssd_args, mix_b, t_all, bs, ls, tp, True)
    y_prompt, y_sample = finish_layer(1, x, mix_a, mix_b, w_out_odd[0])
    y_prompt = y_prompt.reshape(bp, lp, D_MODEL)
    y_sample = y_sample.reshape(bs, ls, D_MODEL)
    return (y_prompt, y_sample, gla_p, gla_s, conf_p, conf_s, sh_p[None], sh_s[None],
            unpair_states(ss_p)[None], unpair_states(ss_s)[None], cm_p[None], cm_s[None])
```

```python
import functools

import jax
import jax.numpy as jnp
from jax import lax
from jax.experimental import pallas as pl
from jax.experimental.pallas import tpu as pltpu
from jax.experimental.pallas import tpu_sc as plsc

F32 = jnp.float32
BF16 = jnp.bfloat16
I32 = jnp.int32
U32 = jnp.uint32

D_MODEL = 1024
DEPTH = 2
DEEPNORM_ALPHA = (2.0 * DEPTH) ** 0.25
LN_EPS = 1e-5
RMS_EPS = 1e-6
LANE = 128
HEAD_W = 128
GLA_HEADS, GLA_DK, GLA_RANK, GLA_TAU = 4, 64, 16, 16.0
CONF_DIM, CONF_WIDTH = 512, 31
HGRN_HEADS, HGRN_DK = 4, 128
SSM_HEADS, SSM_HEADDIM, SSM_STATE, SSM_GROUPS, SSM_CONV = 8, 64, 128, 2, 4
SSM_INNER = SSM_HEADS * SSM_HEADDIM
SSM_CONV_DIM = SSM_INNER + 2 * SSM_GROUPS * SSM_STATE
N_EXPERTS, TOP_K = 32, 4
SWIGLU_ALPHA, SWIGLU_LIMIT = 1.702, 7.0
SCAN_CHUNK = 64
PROMPT_TILE = 512
SSD_PROMPT_TILE = 256
SAMPLE_SEQS = 16
TOKEN_TILE = 512
WIDE_TOKEN_TILE = 1024
MOE_ROWS = 512
IN_SLOTS = 4
SC_CORES, SC_SUBCORES = 2, 16
SC_ROWS = 64
SC_SCATTER_ROWS = 32
VMEM_LIMIT = 56 * 1024 * 1024


def _cparams(n_axes):
    return pltpu.CompilerParams(dimension_semantics=("arbitrary",) * n_axes, vmem_limit_bytes=VMEM_LIMIT)


def _silu(x):
    return x * jax.nn.sigmoid(x)


def _softplus(x):
    return jnp.maximum(x, 0.0) + jnp.log(1.0 + jnp.exp(-jnp.abs(x)))


def _log_sigmoid(x):
    return jnp.minimum(x, 0.0) - jnp.log(1.0 + jnp.exp(-jnp.abs(x)))


def _layernorm(y, g, b):
    mu = jnp.mean(y, axis=-1, keepdims=True)
    d = y - mu
    var = jnp.mean(d * d, axis=-1, keepdims=True)
    return d * lax.rsqrt(var + LN_EPS) * g + b


def _dot(a, b):
    return jnp.dot(a.astype(BF16), b.astype(BF16), preferred_element_type=F32)


def _dot_nt(a, b):
    return lax.dot_general(a.astype(BF16), b.astype(BF16), (((1,), (1,)), ((), ())), preferred_element_type=F32)


def _dot_tn(a, b):
    return lax.dot_general(a.astype(BF16), b.astype(BF16), (((0,), (0,)), ((), ())), preferred_element_type=F32)


def _tri(c):
    r = lax.broadcasted_iota(I32, (c, c), 0)
    k = lax.broadcasted_iota(I32, (c, c), 1)
    return r >= k


def _pair_specs(tm, n_first, width):
    return [pl.BlockSpec((tm, width), lambda i: (jnp.minimum(i, n_first - 1), 0)),
            pl.BlockSpec((tm, width), lambda i: (jnp.maximum(i - n_first, 0), 0))]


def _pair_tile(n_first, xa_ref, xb_ref):
    return jnp.where(pl.program_id(0) < n_first, xa_ref[...], xb_ref[...])


def _inproj_body(n_first, xa_ref, xb_ref, w_ref, o_ref):
    xb = _pair_tile(n_first, xa_ref, xb_ref).astype(BF16)
    n = w_ref.shape[1]
    for c0 in range(0, n, 512):
        c1 = min(c0 + 512, n)
        o_ref[:, c0:c1] = jnp.dot(xb, w_ref[:, c0:c1], preferred_element_type=F32)


def _inproj(xa, xb, w):
    k, n = w.shape
    t = xa.shape[0] + xb.shape[0]
    n_first = xa.shape[0] // TOKEN_TILE
    return pl.pallas_call(
        functools.partial(_inproj_body, n_first),
        grid=(t // TOKEN_TILE,),
        in_specs=_pair_specs(TOKEN_TILE, n_first, k) + [pl.BlockSpec((k, n), lambda i: (0, 0))],
        out_specs=pl.BlockSpec((TOKEN_TILE, n), lambda i: (i, 0)),
        out_shape=jax.ShapeDtypeStruct((t, n), F32),
        compiler_params=_cparams(1),
        name="inproj",
    )(xa, xb, w)


def _pack_halves(y):
    half = y.shape[1] // 2
    hi = lax.bitcast_convert_type(y[:, :half].astype(BF16).astype(F32), U32)
    lo = lax.bitcast_convert_type(y[:, half:].astype(BF16).astype(F32), U32)
    return (hi & jnp.uint32(0xFFFF0000)) | (lo >> 16)


def _unpack_halves(w):
    hi = lax.bitcast_convert_type(w & jnp.uint32(0xFFFF0000), F32)
    lo = lax.bitcast_convert_type(w << 16, F32)
    return hi, lo


def _outproj_body(n_first, a_ref, b_ref, xa_ref, xb_ref, wa_ref, wb_ref, g_ref, be_ref, rwt_ref, rb_ref,
                  x1_ref, xp_ref, gate_ref, idx_ref, rank_ref, cnt_ref, carry):
    @pl.when(pl.program_id(0) == 0)
    def _():
        carry[...] = jnp.zeros(carry.shape, F32)

    mix = (jnp.dot(a_ref[...], wa_ref[...], preferred_element_type=F32)
           + jnp.dot(b_ref[...], wb_ref[...], preferred_element_type=F32))
    x1 = _layernorm(DEEPNORM_ALPHA * _pair_tile(n_first, xa_ref, xb_ref) + mix, g_ref[...], be_ref[...])
    x1_ref[...] = x1
    xp_ref[...] = _pack_halves(x1)
    logits = _dot_nt(rwt_ref[...], x1) + rb_ref[...]
    tm = logits.shape[1]
    expert = lax.broadcasted_iota(I32, logits.shape, 0)
    vals, idxs = [], []
    for _ in range(TOP_K):
        m = jnp.max(logits, axis=0, keepdims=True)
        sel = jnp.min(jnp.where(logits == m, expert, N_EXPERTS), axis=0, keepdims=True)
        vals.append(m)
        idxs.append(sel)
        logits = jnp.where(expert == sel, -jnp.inf, logits)
    exps = [jnp.exp(v - vals[0]) for v in vals]
    inv = 1.0 / functools.reduce(lambda p, q: p + q, exps)
    chosen = jnp.zeros(logits.shape, F32)
    for k in range(TOP_K):
        chosen = chosen + (expert == idxs[k]).astype(F32)
    earlier = lax.broadcasted_iota(I32, (tm, tm), 0) < lax.broadcasted_iota(I32, (tm, tm), 1)
    before = carry[...] + jnp.dot(chosen.astype(BF16), earlier.astype(BF16), preferred_element_type=F32)
    choice = lax.broadcasted_iota(I32, (8, tm), 0)
    gates = jnp.zeros((8, tm), F32)
    eidx = jnp.zeros((8, tm), I32)
    ranks = jnp.zeros((8, tm), F32)
    for k in range(TOP_K):
        rk = jnp.sum(jnp.where(expert == idxs[k], before, 0.0), axis=0, keepdims=True)
        gates = jnp.where(choice == k, exps[k] * inv, gates)
        eidx = jnp.where(choice == k, idxs[k], eidx)
        ranks = jnp.where(choice == k, rk, ranks)
    gate_ref[...] = gates
    idx_ref[...] = eidx
    rank_ref[...] = ranks.astype(I32)
    carry[...] = carry[...] + jnp.sum(chosen, axis=1, keepdims=True)
    cnt_ref[...] = carry[...].astype(I32)


def _outproj_ln_router(a, b, xa, xb, wa, wb, g, be, rwt, rb):
    t = xa.shape[0] + xb.shape[0]
    tm = WIDE_TOKEN_TILE
    n_first = xa.shape[0] // tm
    row = lambda i: (i, 0)
    col = lambda i: (0, i)
    fix = lambda i: (0, 0)
    return pl.pallas_call(
        functools.partial(_outproj_body, n_first),
        grid=(t // tm,),
        in_specs=[pl.BlockSpec((tm, 512), row), pl.BlockSpec((tm, 512), row)] + _pair_specs(tm, n_first, D_MODEL)
        + [pl.BlockSpec((512, D_MODEL), fix), pl.BlockSpec((512, D_MODEL), fix),
           pl.BlockSpec((1, D_MODEL), fix), pl.BlockSpec((1, D_MODEL), fix),
           pl.BlockSpec((N_EXPERTS, D_MODEL), fix), pl.BlockSpec((N_EXPERTS, 1), fix)],
        out_specs=[pl.BlockSpec((tm, D_MODEL), row), pl.BlockSpec((tm, 512), row),
                   pl.BlockSpec((8, tm), col), pl.BlockSpec((8, tm), col), pl.BlockSpec((8, tm), col),
                   pl.BlockSpec((N_EXPERTS, 1), fix)],
        out_shape=[jax.ShapeDtypeStruct((t, D_MODEL), F32), jax.ShapeDtypeStruct((t, 512), U32),
                   jax.ShapeDtypeStruct((8, t), F32), jax.ShapeDtypeStruct((8, t), I32),
                   jax.ShapeDtypeStruct((8, t), I32), jax.ShapeDtypeStruct((N_EXPERTS, 1), I32)],
        scratch_shapes=[pltpu.VMEM((N_EXPERTS, 1), F32)],
        compiler_params=_cparams(1),
        name="outproj_ln_router",
    )(a, b, xa, xb, wa, wb, g, be, rwt, rb)


def _sc_mesh():
    return plsc.VectorSubcoreMesh(core_axis_name="c", subcore_axis_name="s")


def _sc_scatter_rows(src, dest, n_out):
    n_src, w = src.shape
    n_dst = dest.shape[1]
    workers = SC_CORES * SC_SUBCORES
    per_worker = n_src // workers
    chunks = per_worker // SC_SCATTER_ROWS
    assert n_src == workers * chunks * SC_SCATTER_ROWS
    idx = dest.T.reshape(n_dst, workers, chunks, SC_SCATTER_ROWS)

    @functools.partial(pl.kernel, mesh=_sc_mesh(), out_type=jax.ShapeDtypeStruct((n_out, w), src.dtype),
                       scratch_types=[pltpu.VMEM((n_dst, chunks, SC_SCATTER_ROWS), I32)]
                       + [pltpu.VMEM((SC_SCATTER_ROWS, w), src.dtype)] * 2 + [pltpu.SemaphoreType.DMA] * 4)
    def scatter(src_hbm, idx_hbm, out_hbm, idx_v, rows_a, rows_b, sem_ra, sem_rb, sem_wa, sem_wb):
        worker = lax.axis_index("s") * SC_CORES + lax.axis_index("c")
        base = worker * per_worker
        for k in range(n_dst):
            pltpu.sync_copy(idx_hbm.at[k, worker], idx_v.at[k])

        def read(c, rows, sem):
            return pltpu.async_copy(src_hbm.at[pl.ds(pl.multiple_of(base + c * SC_SCATTER_ROWS, 8), SC_SCATTER_ROWS)],
                                    rows, sem)

        def write_all(pending_read, c, rows, sem):
            pending_read.wait()
            return [pltpu.async_copy(rows, out_hbm.at[idx_v.at[k, c]], sem) for k in range(n_dst)]

        @pl.loop(0, chunks // 2)
        def _(p):
            read_a = read(2 * p, rows_a, sem_ra)
            read_b = read(2 * p + 1, rows_b, sem_rb)
            writes = write_all(read_a, 2 * p, rows_a, sem_wa) + write_all(read_b, 2 * p + 1, rows_b, sem_wb)
            for wr in writes:
                wr.wait()

        if chunks % 2:
            for wr in write_all(read(chunks - 1, rows_a, sem_ra), chunks - 1, rows_a, sem_wa):
                wr.wait()

    return scatter(src, idx)


def _sc_gather_rows(table, idx):
    n, w = idx.shape[0], table.shape[1]
    workers = SC_CORES * SC_SUBCORES
    chunks = n // (workers * SC_ROWS)
    assert n == workers * chunks * SC_ROWS and chunks % 2 == 0
    idx = idx.reshape(workers, chunks, SC_ROWS)

    @functools.partial(pl.kernel, mesh=_sc_mesh(), out_type=jax.ShapeDtypeStruct((n, w), table.dtype),
                       scratch_types=[pltpu.VMEM((chunks, SC_ROWS), I32)] + [pltpu.VMEM((SC_ROWS, w), table.dtype)] * 2
                       + [pltpu.SemaphoreType.DMA] * 4)
    def gather(table_hbm, idx_hbm, out_hbm, idx_v, rows_a, rows_b, sem_ra, sem_rb, sem_wa, sem_wb):
        worker = lax.axis_index("s") * SC_CORES + lax.axis_index("c")
        base = worker * (chunks * SC_ROWS)
        pltpu.sync_copy(idx_hbm.at[worker], idx_v)

        def out_rows(c):
            return out_hbm.at[pl.ds(pl.multiple_of(base + c * SC_ROWS, 8), SC_ROWS)]

        @pl.loop(0, chunks // 2)
        def _(p):
            read_a = pltpu.async_copy(table_hbm.at[idx_v.at[2 * p]], rows_a, sem_ra)
            read_b = pltpu.async_copy(table_hbm.at[idx_v.at[2 * p + 1]], rows_b, sem_rb)
            read_a.wait()
            write_a = pltpu.async_copy(rows_a, out_rows(2 * p), sem_wa)
            read_b.wait()
            write_b = pltpu.async_copy(rows_b, out_rows(2 * p + 1), sem_wb)
            write_a.wait()
            write_b.wait()

    return gather(table, idx)


def _experts_body(b0_ref, nb_ref, last_ref, nt_ref, xs_hbm, wg_ref, bg_ref, wu_ref, bu_ref, wd_ref, bd_ref, o_hbm,
                  wg_s, wu_s, wd_s, xbuf, obuf, sem_in, sem_out):
    e = pl.program_id(0)
    first_blk, n_blk, last_valid, n_total = b0_ref[e], nb_ref[e], last_ref[e], nt_ref[0]
    quarter = MOE_ROWS // 4

    def rows_of(g):
        return pl.ds(pl.multiple_of(g * MOE_ROWS, MOE_ROWS), MOE_ROWS)

    def fetch(g, slot):
        return pltpu.make_async_copy(xs_hbm.at[rows_of(g)], xbuf.at[slot], sem_in.at[slot])

    def put(g, slot):
        return pltpu.make_async_copy(obuf.at[slot], o_hbm.at[rows_of(g)], sem_out.at[slot])

    lead = IN_SLOTS - 1
    for first in range(lead):
        @pl.when((e == 0) & (n_total > first))
        def _():
            fetch(first, first).start()

    @pl.when(n_blk > 0)
    def _():
        wg_s[...] = wg_ref[...].astype(BF16)
        wu_s[...] = wu_ref[...].astype(BF16)
        wd_s[...] = wd_ref[...].astype(BF16)

    def compute(islot, slot, rows):
        half = D_MODEL // 2
        x_hi, x_lo = _unpack_halves(xbuf[islot, 0:rows, :])
        x = jnp.concatenate([x_hi.astype(BF16), x_lo.astype(BF16)], axis=1)
        g = jnp.dot(x, wg_s[...], preferred_element_type=F32) + bg_ref[...]
        u = jnp.dot(x, wu_s[...], preferred_element_type=F32) + bu_ref[...]
        g = jnp.minimum(g, SWIGLU_LIMIT)
        u = jnp.clip(u, -SWIGLU_LIMIT, SWIGLU_LIMIT)
        hmid = (u + 1.0) * (g * jax.nn.sigmoid(SWIGLU_ALPHA * g))
        out = jnp.dot(hmid.astype(BF16), wd_s[...], preferred_element_type=F32) + bd_ref[...]
        obuf[slot, 0:rows, :] = _pack_halves(out)

    def block(j, carry):
        g = first_blk + j
        slot = lax.rem(g, 2)
        islot = lax.rem(g, IN_SLOTS)
        fetch(g, islot).wait()

        @pl.when(g + lead < n_total)
        def _():
            fetch(g + lead, lax.rem(g + lead, IN_SLOTS)).start()

        @pl.when(g >= 2)
        def _():
            put(g - 2, slot).wait()

        valid = jnp.where(j == n_blk - 1, last_valid, MOE_ROWS)

        for rows in range(quarter, MOE_ROWS + 1, quarter):
            @pl.when((valid > rows - quarter) & (valid <= rows))
            def _():
                compute(islot, slot, rows)
                if rows < MOE_ROWS:
                    obuf[slot, rows:, :] = jnp.zeros((MOE_ROWS - rows, obuf.shape[2]), obuf.dtype)

        put(g, slot).start()
        return carry

    lax.fori_loop(0, n_blk, block, 0)

    @pl.when((e == N_EXPERTS - 1) & (n_total >= 2))
    def _():
        put(n_total - 2, lax.rem(n_total, 2)).wait()

    @pl.when((e == N_EXPERTS - 1) & (n_total >= 1))
    def _():
        put(n_total - 1, lax.rem(n_total - 1, 2)).wait()


def _experts(layer, first_blk, n_blk, last_valid, xs, wg, bg, wu, bu, wd, bd):
    n_rows, w = xs.shape
    wsel = lambda e, b0, nb, lv, nt: (layer, e, 0, 0)
    wspec = pl.BlockSpec((None, None, D_MODEL, D_MODEL), wsel)
    bspec = pl.BlockSpec((None, None, 1, D_MODEL), wsel)
    bias = lambda b: b.reshape(b.shape[0], b.shape[1], 1, b.shape[2])
    return pl.pallas_call(
        _experts_body,
        grid_spec=pltpu.PrefetchScalarGridSpec(
            num_scalar_prefetch=4,
            grid=(N_EXPERTS,),
            in_specs=[pl.BlockSpec(memory_space=pl.ANY), wspec, bspec, wspec, bspec, wspec, bspec],
            out_specs=pl.BlockSpec(memory_space=pl.ANY),
            scratch_shapes=[pltpu.VMEM((D_MODEL, D_MODEL), BF16)] * 3
            + [pltpu.VMEM((IN_SLOTS, MOE_ROWS, w), U32), pltpu.VMEM((2, MOE_ROWS, w), U32),
               pltpu.SemaphoreType.DMA((IN_SLOTS,)), pltpu.SemaphoreType.DMA((2,))],
        ),
        out_shape=jax.ShapeDtypeStruct((n_rows, w), U32),
        compiler_params=_cparams(1),
        name="experts",
    )(first_blk, n_blk, last_valid, jnp.sum(n_blk).reshape(1), xs, wg, bias(bg), wu, bias(bu), wd, bias(bd))


def _combine_body(n_first, o0_ref, o1_ref, o2_ref, o3_ref, gt_ref, x_ref, g_ref, b_ref, ya_ref, yb_ref=None):
    half = D_MODEL // 2
    gates = gt_ref[...]
    hi = jnp.zeros((x_ref.shape[0], half), F32)
    lo = jnp.zeros((x_ref.shape[0], half), F32)
    for k, o_ref in enumerate((o0_ref, o1_ref, o2_ref, o3_ref)):
        h, l = _unpack_halves(o_ref[...])
        gk = gates[:, k:k + 1]
        hi = hi + gk * h
        lo = lo + gk * l
    x = x_ref[...]
    y_hi = DEEPNORM_ALPHA * x[:, :half] + hi
    y_lo = DEEPNORM_ALPHA * x[:, half:] + lo
    mu = (jnp.sum(y_hi, axis=-1, keepdims=True) + jnp.sum(y_lo, axis=-1, keepdims=True)) * (1.0 / D_MODEL)
    d_hi = y_hi - mu
    d_lo = y_lo - mu
    var = (jnp.sum(d_hi * d_hi, axis=-1, keepdims=True) + jnp.sum(d_lo * d_lo, axis=-1, keepdims=True)) * (1.0 / D_MODEL)
    r = lax.rsqrt(var + LN_EPS)
    out_hi = d_hi * r * g_ref[:, :half] + b_ref[:, :half]
    out_lo = d_lo * r * g_ref[:, half:] + b_ref[:, half:]

    def write(y_ref):
        y_ref[:, :half] = out_hi
        y_ref[:, half:] = out_lo

    if yb_ref is None:
        write(ya_ref)
    else:
        pl.when(pl.program_id(0) < n_first)(lambda: write(ya_ref))
        pl.when(pl.program_id(0) >= n_first)(lambda: write(yb_ref))


def _combine_ln(o4, gates, x, g, b, t_first=None):
    t = x.shape[0]
    tm = WIDE_TOKEN_TILE
    row = lambda i: (i, 0)
    fix = lambda i: (0, 0)
    choice = lambda k: pl.BlockSpec((tm, 512), lambda i: (k * (t // tm) + i, 0))
    if t_first is None:
        n_first = None
        out_specs = pl.BlockSpec((tm, D_MODEL), row)
        out_shape = jax.ShapeDtypeStruct((t, D_MODEL), F32)
    else:
        n_first = t_first // tm
        out_specs = [pl.BlockSpec((tm, D_MODEL), lambda i: (jnp.minimum(i, n_first - 1), 0)),
                     pl.BlockSpec((tm, D_MODEL), lambda i: (jnp.maximum(i - n_first, 0), 0))]
        out_shape = [jax.ShapeDtypeStruct((t_first, D_MODEL), F32), jax.ShapeDtypeStruct((t - t_first, D_MODEL), F32)]
    return pl.pallas_call(
        functools.partial(_combine_body, n_first),
        grid=(t // tm,),
        in_specs=[choice(0), choice(1), choice(2), choice(3), pl.BlockSpec((tm, TOP_K), row),
                  pl.BlockSpec((tm, D_MODEL), row), pl.BlockSpec((1, D_MODEL), fix), pl.BlockSpec((1, D_MODEL), fix)],
        out_specs=out_specs,
        out_shape=out_shape,
        compiler_params=_cparams(1),
        name="combine_ln",
    )(o4, o4, o4, o4, gates, x, g, b)


def _moe(layer, x1, xp, gates, eidx, rank, counts, ln_g, ln_b, wg, bg, wu, bu, wd, bd, t_first=None):
    t = x1.shape[0]
    bm = MOE_ROWS
    n_blocks = t * TOP_K // bm + N_EXPERTS
    n_rows = n_blocks * bm
    cnt = counts[:, 0]
    padded = (cnt + bm - 1) // bm * bm
    pad_end = jnp.cumsum(padded)
    pad_start = pad_end - padded
    e = eidx[:TOP_K]
    start = jnp.sum(jnp.where(e[:, :, None] == jnp.arange(N_EXPERTS, dtype=I32), pad_start, 0), axis=-1)
    dest = (start + rank[:TOP_K]).T
    n_blk = padded // bm
    last_valid = cnt - (n_blk - 1) * bm
    xs = _sc_scatter_rows(xp, dest, n_rows)
    outs = _experts(layer, pad_start // bm, n_blk, last_valid, xs, wg, bg, wu, bu, wd, bd)
    o4 = _sc_gather_rows(outs, dest.T.reshape(-1))
    return _combine_ln(o4, gates[:TOP_K].T, x1, ln_g, ln_b, t_first)


def _split2(x):
    hi = x.astype(BF16)
    return hi, (x - hi.astype(F32)).astype(BF16)


def _split3(x):
    hi = x.astype(BF16)
    rem = x - hi.astype(F32)
    mid = rem.astype(BF16)
    return hi, mid, (rem - mid.astype(F32)).astype(BF16)


def _chunk_cumsum(g, C):
    rows = g.shape[0]
    r = lax.broadcasted_iota(I32, (rows, rows), 0)
    c = lax.broadcasted_iota(I32, (rows, rows), 1)
    tri = ((r >= c) & (r // C == c // C)).astype(BF16)
    hi, mid, lo = _split3(g)
    dot = lambda part: jnp.dot(tri, part, preferred_element_type=F32)
    return dot(hi) + dot(mid) + dot(lo)


def _gla_batched_step(q, k, v, g, C, n_seq, n_heads, state_of, o_scr):
    rows = n_seq * C
    wide = n_seq * HEAD_W
    mid = max(C // 2 - 1, 0)
    r = lax.broadcasted_iota(I32, (rows, rows), 0)
    c = lax.broadcasted_iota(I32, (rows, rows), 1)
    same = (r // C) == (c // C)
    causal = same & (r >= c)
    parts = _split3(g)
    summed = lambda mask: functools.reduce(lambda p, q_: p + q_, [
        jnp.dot(mask.astype(BF16), part, preferred_element_type=F32) for part in parts])
    b = summed(causal)
    b_mid = summed(same & ((c % C) <= mid))
    b_last = summed(same)
    qe_hi, qe_lo = _split2(q * jnp.exp(b - b_mid))
    ke_hi, ke_lo = _split2(k * jnp.exp(b_mid - b))
    q_state = (q * jnp.exp(b)).astype(BF16)
    k_state = (k * jnp.exp(b_last - b)).astype(BF16)
    decay_parts = _split3(jnp.exp(b_last))
    row_w = lax.broadcasted_iota(I32, (rows, wide), 0)
    blk_w = lax.broadcasted_iota(I32, (rows, wide), 1) // HEAD_W
    own = (row_w // C) == blk_w
    pick = (row_w == blk_w * C).astype(BF16)
    new_states = []
    for h in range(n_heads):
        cs = slice(h * HEAD_W, (h + 1) * HEAD_W)
        lhs = jnp.concatenate([qe_hi[:, cs], qe_hi[:, cs], qe_lo[:, cs]], axis=1)
        rhs = jnp.concatenate([ke_hi[:, cs], ke_lo[:, cs], ke_hi[:, cs]], axis=1)
        scores = jnp.where(causal, _dot_nt(lhs, rhs), 0.0)
        vh = v[:, cs].astype(BF16)
        s_cat = jnp.concatenate([state_of(s, h) for s in range(n_seq)], axis=1)
        o_full = _dot(q_state[:, cs], s_cat)
        o_state = jnp.concatenate([o_full[s * C:(s + 1) * C, s * HEAD_W:(s + 1) * HEAD_W] for s in range(n_seq)],
                                  axis=0)
        o = _dot(scores, vh) + o_state
        v_wide = jnp.where(own, jnp.concatenate([vh] * n_seq, axis=1), jnp.zeros((), BF16))
        kv = _dot_tn(k_state[:, cs], v_wide)
        decay = functools.reduce(lambda p, q_: p + q_, [
            lax.dot_general(part[:, cs], pick, (((0,), (0,)), ((), ())), preferred_element_type=F32)
            for part in decay_parts])
        new_states.append(s_cat * decay + kv)
        ms = jnp.mean(o * o, axis=-1, keepdims=True)
        o_scr[:, cs] = o * lax.rsqrt(ms + RMS_EPS)
    return new_states


def _gla_body(mode, n_seq, n_chunk, C, *refs):
    if mode == "gla":
        hq_ref, hk_ref, hv_ref, hg_ref, hlr_ref, wlr_ref, blr_ref, nw_ref, s0_ref, _, o_ref, so_ref, st, o_scr = refs
    else:
        hq_ref, hk_ref, hv_ref, hg_ref, lb_ref, nw_ref, s0_ref, _, o_ref, so_ref, st, o_scr = refs
    n_heads = 4
    n_keys = s0_ref.shape[2]
    tstep = pl.program_id(1)
    batched = n_chunk == 1 and n_seq > 1

    def padded_state(s, h):
        s_in = s0_ref[s, h]
        if n_keys < HEAD_W:
            s_in = jnp.concatenate([s_in, jnp.zeros((HEAD_W - n_keys, HEAD_W), F32)], axis=0)
        return s_in

    if not batched:
        @pl.when(tstep == 0)
        def _():
            for s in range(n_seq):
                for h in range(n_heads):
                    st[s, h] = padded_state(s, h).T

    if mode == "gla":
        q = hq_ref[...] * (GLA_DK ** -0.5)
        k = hk_ref[...]
        z = _dot(hlr_ref[...], wlr_ref[...]) + blr_ref[...]
        g = _log_sigmoid(z) * (1.0 / GLA_TAU)
    else:
        q = _silu(hq_ref[...]) * (HGRN_DK ** -0.5)
        lb = lb_ref[...]
        f = lb + (1.0 - lb) * jax.nn.sigmoid(hk_ref[...])
        k = 1.0 - f
        g = jnp.log(f)
    v = hv_ref[...]
    if batched:
        new_states = _gla_batched_step(q, k, v, g, C, n_seq, n_heads, padded_state, o_scr)
        for s in range(n_seq):
            for h in range(n_heads):
                so_ref[s, h] = new_states[h][0:n_keys, s * HEAD_W:(s + 1) * HEAD_W]
    else:
        causal = _tri(C)
        mid = max(C // 2 - 1, 0)
        b_all = _chunk_cumsum(g, C)
        for s in range(n_seq):
            states = [st[s, h] for h in range(n_heads)]
            for c in range(n_chunk):
                r0 = (s * n_chunk + c) * C
                rs = slice(r0, r0 + C)
                b, qc, kc = b_all[rs, :], q[rs, :], k[rs, :]
                b_last = b[C - 1:C, :]
                b_mid = b[mid:mid + 1, :]
                qe_hi, qe_lo = _split2(qc * jnp.exp(b - b_mid))
                ke_hi, ke_lo = _split2(kc * jnp.exp(b_mid - b))
                q_state = (qc * jnp.exp(b)).astype(BF16)
                k_state = (kc * jnp.exp(b_last - b)).astype(BF16)
                decay = jnp.exp(b_last)
                for h in range(n_heads):
                    cs = slice(h * HEAD_W, (h + 1) * HEAD_W)
                    lhs = jnp.concatenate([qe_hi[:, cs], qe_hi[:, cs], qe_lo[:, cs]], axis=1)
                    rhs = jnp.concatenate([ke_hi[:, cs], ke_lo[:, cs], ke_hi[:, cs]], axis=1)
                    scores = jnp.where(causal, _dot_nt(lhs, rhs), 0.0)
                    vh = v[rs, cs].astype(BF16)
                    o = _dot(scores, vh) + _dot_nt(q_state[:, cs], states[h])
                    states[h] = states[h] * decay[:, cs] + _dot_tn(vh, k_state[:, cs])
                    ms = jnp.mean(o * o, axis=-1, keepdims=True)
                    o_scr[rs, cs] = o * lax.rsqrt(ms + RMS_EPS)
            for h in range(n_heads):
                st[s, h] = states[h]
    o_ref[...] = (o_scr[...] * nw_ref[...] * _silu(hg_ref[...])).astype(BF16)

    if not batched:
        @pl.when(tstep == pl.num_programs(1) - 1)
        def _():
            for s in range(n_seq):
                for h in range(n_heads):
                    so_ref[s, h] = st[s, h].T[0:n_keys, :]


def _alias_if_full(out_buf, mix_rows, arg_index):
    return {arg_index: 0} if out_buf.shape == (mix_rows, 512) else {}


def _seq_layout(n_batch, seq_len, row_off, sample, prompt_tile=PROMPT_TILE):
    if sample:
        n_seq, n_chunk, C = SAMPLE_SEQS, 1, seq_len
        rows = n_seq * C
        grid = (n_batch // n_seq, 1)
        blk0 = row_off // rows
        rb = lambda i, t: blk0 + i
    else:
        n_seq, n_chunk, C = 1, prompt_tile // SCAN_CHUNK, SCAN_CHUNK
        rows = prompt_tile
        tiles = seq_len // rows
        grid = (n_batch, tiles)
        blk0 = row_off // rows
        rb = lambda i, t: blk0 + i * tiles + t
    return n_seq, n_chunk, C, rows, grid, rb


def _gla_call(mode, h, cols, extra, nw, s0, out_buf, mix_rows, n_batch, seq_len, row_off, sample):
    n_seq, n_chunk, C, rows, grid, rb = _seq_layout(n_batch, seq_len, row_off, sample)
    colspec = lambda c0, w: pl.BlockSpec((rows, w), lambda i, t: (rb(i, t), c0 // w))
    fix2 = lambda i, t: (0, 0)
    in_specs = [colspec(cols[0], 512), colspec(cols[1], 512), colspec(cols[2], 512), colspec(cols[3], 512)]
    args = [h, h, h, h]
    if mode == "gla":
        wlr, blr = extra
        in_specs += [colspec(cols[4], LANE), pl.BlockSpec((LANE, 512), fix2), pl.BlockSpec((1, 512), fix2)]
        args += [h, wlr, blr]
    else:
        in_specs += [pl.BlockSpec((1, 512), fix2)]
        args += [extra]
    n_keys = s0.shape[2]
    st_spec = pl.BlockSpec((n_seq, 4, n_keys, HEAD_W), lambda i, t: (i, 0, 0, 0))
    in_specs += [pl.BlockSpec((1, 512), fix2), st_spec, pl.BlockSpec(memory_space=pl.ANY)]
    args += [nw, s0, out_buf]
    o_spec = pl.BlockSpec((rows, 512), lambda i, t: (rb(i, t), 0))
    return pl.pallas_call(
        functools.partial(_gla_body, mode, n_seq, n_chunk, C),
        grid=grid,
        in_specs=in_specs,
        out_specs=[o_spec, st_spec],
        out_shape=[jax.ShapeDtypeStruct((mix_rows, 512), BF16),
                   jax.ShapeDtypeStruct((n_batch, 4, n_keys, HEAD_W), F32)],
        scratch_shapes=[pltpu.VMEM((n_seq, 4, HEAD_W, HEAD_W), F32), pltpu.VMEM((rows, 512), F32)],
        input_output_aliases=_alias_if_full(out_buf, mix_rows, len(args) - 1),
        compiler_params=_cparams(2),
        name=mode + ("_sample" if sample else "_prompt"),
    )(*args)


def _round_bf16(x, on=True):
    return x.astype(BF16).astype(F32) if on else x


def _conf_body(n_seq, L, round_x, round_w, a_ref, gt_ref, hist_ref, w_ref, b_ref, g_ref, be_ref, _, o_ref, co_ref,
               buf, bufr, y_scr, win):
    tstep = pl.program_id(1)
    hist = CONF_WIDTH - 1
    pad = 32 - hist

    @pl.when(tstep == 0)
    def _():
        for s in range(n_seq):
            buf[s, pad:32, :] = hist_ref[s]
            bufr[s, pad:32, :] = _round_bf16(hist_ref[s], round_x)

    u = a_ref[...] * jax.nn.sigmoid(gt_ref[...])
    ur = _round_bf16(u, round_x)
    for s in range(n_seq):
        buf[s, 32:32 + L, :] = u[s * L:(s + 1) * L, :]
        bufr[s, 32:32 + L, :] = ur[s * L:(s + 1) * L, :]
    w = _round_bf16(w_ref[...], round_w)
    for s in range(n_seq):
        acc = jnp.zeros((L, CONF_DIM), F32)
        for phase in range(8):
            n_taps = (CONF_WIDTH - 1 - phase) // 8 + 1
            span = L + 8 * (n_taps - 1)
            win[phase, 0:span, :] = bufr[s, pad + phase:pad + phase + span, :]
            for a in range(n_taps):
                j = 8 * a + phase
                acc = acc + win[phase, 8 * a:8 * a + L, :] * w[j:j + 1, :]
        y_scr[s * L:(s + 1) * L, :] = _silu(_layernorm(acc + b_ref[...], g_ref[...], be_ref[...]))
        tail = buf[s, L + pad:L + 32, :]
        buf[s, pad:32, :] = tail
        tailr = bufr[s, L + pad:L + 32, :]
        bufr[s, pad:32, :] = tailr
    o_ref[...] = y_scr[...].astype(o_ref.dtype)

    @pl.when(tstep == pl.num_programs(1) - 1)
    def _():
        for s in range(n_seq):
            co_ref[s] = buf[s, pad:32, :]


def _conf_call(h, col_a, col_g, cache, w, b, g, be, out_buf, mix_rows, n_batch, seq_len, row_off, sample):
    n_seq, n_chunk, C, rows, grid, rb = _seq_layout(n_batch, seq_len, row_off, sample)
    L = rows // n_seq
    hist = CONF_WIDTH - 1
    colspec = lambda c0: pl.BlockSpec((rows, 512), lambda i, t: (rb(i, t), c0 // 512))
    fix2 = lambda i, t: (0, 0)
    c_spec = pl.BlockSpec((n_seq, hist, CONF_DIM), lambda i, t: (i, 0, 0))
    return pl.pallas_call(
        functools.partial(_conf_body, n_seq, L, True, sample),
        grid=grid,
        in_specs=[colspec(col_a), colspec(col_g), c_spec,
                  pl.BlockSpec((CONF_WIDTH, CONF_DIM), fix2), pl.BlockSpec((1, CONF_DIM), fix2),
                  pl.BlockSpec((1, CONF_DIM), fix2), pl.BlockSpec((1, CONF_DIM), fix2),
                  pl.BlockSpec(memory_space=pl.ANY)],
        out_specs=[pl.BlockSpec((rows, 512), lambda i, t: (rb(i, t), 0)), c_spec],
        out_shape=[jax.ShapeDtypeStruct((mix_rows, 512), BF16),
                   jax.ShapeDtypeStruct((n_batch, hist, CONF_DIM), F32)],
        scratch_shapes=[pltpu.VMEM((n_seq, 32 + L, CONF_DIM), F32)] * 2 + [pltpu.VMEM((rows, CONF_DIM), F32),
                                                                           pltpu.VMEM((8, L + 24, CONF_DIM), F32)],
        input_output_aliases=_alias_if_full(out_buf, mix_rows, 7),
        compiler_params=_cparams(2),
        name="conformer" + ("_sample" if sample else "_prompt"),
    )(h, h, cache, w, b, g, be, out_buf)


def _ssd_body(n_seq, n_chunk, C, round_x, round_w, hz_ref, hx_ref, hdt_ref, hist_ref, s0_ref, cw_ref, cb_ref, dtb_ref, alog_ref,
              dvec_ref, nw_ref, _, o_ref, co_ref, so_ref, st, buf, bufr, xbc, y_scr):
    tstep = pl.program_id(1)
    L = n_chunk * C
    hist = SSM_CONV - 1
    pad = 8 - hist
    n_pairs = SSM_HEADS // 2

    @pl.when(tstep == 0)
    def _():
        for s in range(n_seq):
            buf[s, pad:8, :] = hist_ref[s]
            bufr[s, pad:8, :] = _round_bf16(hist_ref[s], round_x)
            for m in range(n_pairs):
                st[s, m] = s0_ref[s, m]

    cw = _round_bf16(cw_ref[...], round_w)
    for s in range(n_seq):
        hx = hx_ref[s * L:(s + 1) * L, :]
        buf[s, 8:8 + L, :] = hx
        bufr[s, 8:8 + L, :] = _round_bf16(hx, round_x)
        acc = jnp.zeros((L, SSM_CONV_DIM), F32)
        for j in range(SSM_CONV):
            acc = acc + bufr[s, pad + j:pad + j + L, :] * cw[j:j + 1, :]
        xbc[s * L:(s + 1) * L, :] = _silu(acc + cb_ref[...])
        tail = buf[s, L + pad:L + 8, :]
        buf[s, pad:8, :] = tail
        tailr = bufr[s, L + pad:L + 8, :]
        bufr[s, pad:8, :] = tailr

    dt = _softplus(hdt_ref[...] + dtb_ref[...])
    la = dt * (-jnp.exp(alog_ref[...]))
    hrow = lax.broadcasted_iota(I32, (LANE, SSM_INNER), 0)
    hcol = lax.broadcasted_iota(I32, (LANE, SSM_INNER), 1) // SSM_HEADDIM
    expand = (hrow == hcol).astype(BF16)
    dtx = functools.reduce(lambda p, q: p + q,
                           [jnp.dot(part, expand, preferred_element_type=F32) for part in _split3(dt)])
    causal = _tri(C)
    tri = causal.astype(BF16)
    lane = lax.broadcasted_iota(I32, (C, HEAD_W), 1)
    bcol_all = _chunk_cumsum(la, C)
    heads_per_group = SSM_HEADS // SSM_GROUPS
    for s in range(n_seq):
        states = [st[s, m] for m in range(n_pairs)]
        for c in range(n_chunk):
            r0 = (s * n_chunk + c) * C
            rs = slice(r0, r0 + C)
            bcol = bcol_all[rs, :]
            brow = functools.reduce(lambda p, q: p + q, [
                lax.dot_general(part, tri, (((0,), (1,)), ((), ())), preferred_element_type=F32)
                for part in _split3(la[rs, :])])
            xs_c = xbc[rs, 0:SSM_INNER]
            v_c = (xs_c * dtx[rs, :]).astype(BF16)
            gmats, bms, cms = [], [], []
            for grp in range(SSM_GROUPS):
                bm = xbc[rs, SSM_INNER + grp * SSM_STATE:SSM_INNER + (grp + 1) * SSM_STATE]
                cm = xbc[rs, SSM_INNER + (SSM_GROUPS + grp) * SSM_STATE:SSM_INNER + (SSM_GROUPS + grp + 1) * SSM_STATE]
                cm_hi, cm_lo = _split2(cm)
                bm_hi, bm_lo = _split2(bm)
                gmats.append(_dot_nt(jnp.concatenate([cm_hi, cm_hi, cm_lo], axis=1),
                                     jnp.concatenate([bm_hi, bm_lo, bm_hi], axis=1)))
                bms.append(bm)
                cms.append(cm)
            for m in range(n_pairs):
                grp = (2 * m) // heads_per_group
                bm, cm, gmat = bms[grp], cms[grp], gmats[grp]
                ps = slice(m * HEAD_W, (m + 1) * HEAD_W)
                vp = v_c[:, ps]
                s_t = states[m]
                o_halves, new_rows = [], []
                for hh in range(2):
                    hd = 2 * m + hh
                    bc = bcol[:, hd:hd + 1]
                    br = brow[hd:hd + 1, :]
                    dec = jnp.where(causal, jnp.exp(jnp.minimum(bc - br, 0.0)), 0.0)
                    b_last = bcol[C - 1:C, hd:hd + 1]
                    o_halves.append(_dot(gmat * dec, vp) + _dot_nt(cm * jnp.exp(bc), s_t))
                    kv = _dot_tn(vp, bm * jnp.exp(b_last - bc))
                    vs = slice(hh * SSM_HEADDIM, (hh + 1) * SSM_HEADDIM)
                    new_rows.append(s_t[vs, :] * jnp.exp(b_last) + kv[vs, :])
                states[m] = jnp.concatenate(new_rows, axis=0)
                o_pair = jnp.where(lane < SSM_HEADDIM, o_halves[0], o_halves[1])
                y_scr[rs, ps] = o_pair + dvec_ref[:, ps] * xs_c[:, ps]
        for m in range(n_pairs):
            st[s, m] = states[m]
    y = y_scr[...] * _silu(hz_ref[...])
    gw = SSM_INNER // SSM_GROUPS
    for grp in range(SSM_GROUPS):
        gs = slice(grp * gw, (grp + 1) * gw)
        yg = y[:, gs]
        ms = jnp.mean(yg * yg, axis=-1, keepdims=True)
        o_ref[:, gs] = (yg * lax.rsqrt(ms + RMS_EPS) * nw_ref[:, gs]).astype(BF16)

    @pl.when(tstep == pl.num_programs(1) - 1)
    def _():
        for s in range(n_seq):
            co_ref[s] = buf[s, pad:8, :]
            for m in range(n_pairs):
                so_ref[s, m] = st[s, m]


def _ssd_call(h, col_z, col_x, col_dt, cache, s0, cw, cb, dtb, alog, dvec, nw, out_buf, mix_rows, n_batch, seq_len,
              row_off, sample):
    n_seq, n_chunk, C, rows, grid, rb = _seq_layout(n_batch, seq_len, row_off, sample, SSD_PROMPT_TILE)
    L = rows // n_seq
    hist = SSM_CONV - 1
    n_pairs = SSM_HEADS // 2
    colspec = lambda c0, w: pl.BlockSpec((rows, w), lambda i, t: (rb(i, t), c0 // w))
    fix2 = lambda i, t: (0, 0)
    c_spec = pl.BlockSpec((n_seq, hist, SSM_CONV_DIM), lambda i, t: (i, 0, 0))
    st_spec = pl.BlockSpec((n_seq, n_pairs, HEAD_W, SSM_STATE), lambda i, t: (i, 0, 0, 0))
    return pl.pallas_call(
        functools.partial(_ssd_body, n_seq, n_chunk, C, sample, True),
        grid=grid,
        in_specs=[colspec(col_z, 512), colspec(col_x, SSM_CONV_DIM), colspec(col_dt, LANE), c_spec, st_spec,
                  pl.BlockSpec((SSM_CONV, SSM_CONV_DIM), fix2), pl.BlockSpec((1, SSM_CONV_DIM), fix2),
                  pl.BlockSpec((1, LANE), fix2), pl.BlockSpec((1, LANE), fix2),
                  pl.BlockSpec((1, SSM_INNER), fix2), pl.BlockSpec((1, SSM_INNER), fix2),
                  pl.BlockSpec(memory_space=pl.ANY)],
        out_specs=[pl.BlockSpec((rows, 512), lambda i, t: (rb(i, t), 0)), c_spec, st_spec],
        out_shape=[jax.ShapeDtypeStruct((mix_rows, 512), BF16),
                   jax.ShapeDtypeStruct((n_batch, hist, SSM_CONV_DIM), F32),
                   jax.ShapeDtypeStruct((n_batch, n_pairs, HEAD_W, SSM_STATE), F32)],
        scratch_shapes=[pltpu.VMEM((n_seq, n_pairs, HEAD_W, SSM_STATE), F32),
                        pltpu.VMEM((n_seq, 8 + L, SSM_CONV_DIM), F32),
                        pltpu.VMEM((n_seq, 8 + L, SSM_CONV_DIM), F32),
                        pltpu.VMEM((rows, SSM_CONV_DIM), F32),
                        pltpu.VMEM((rows, SSM_INNER), F32)],
        input_output_aliases=_alias_if_full(out_buf, mix_rows, 11),
        compiler_params=_cparams(2),
        name="ssd" + ("_sample" if sample else "_prompt"),
    )(h, h, h, cache, s0, cw, cb, dtb, alog, dvec, nw, out_buf)


def _pad_heads(w, n_heads, width):
    lead = w.shape[:-1]
    w = w.reshape(lead + (n_heads, width))
    w = jnp.pad(w, [(0, 0)] * len(lead) + [(0, 0), (0, HEAD_W - width)])
    return w.reshape(lead + (n_heads * HEAD_W,))


def _row(v):
    return v.reshape(1, -1).astype(F32)


def kernel(x_prompt, x_sample, state_gla, cache_conformer, state_hgrn, state_ssm, cache_mamba_conv, w_in_even, w_gla_gate_lr, b_gla_gate, gla_norm_w, conf_conv_w, conf_conv_b, conf_ln_g, conf_ln_b, w_out_even, w_in_odd, hgrn_lower_bounds, hgrn_norm_w, mamba_conv_w, mamba_conv_b, mamba_dt_bias, mamba_a_log, mamba_d, mamba_norm_w, w_out_odd, ln1_g, ln1_b, ln2_g, ln2_b, router_w, router_b, expert_w_gate, expert_b_gate, expert_w_up, expert_b_up, expert_w_down, expert_b_down):
    bp, lp, _ = x_prompt.shape
    bs, ls, _ = x_sample.shape
    tp, ts = bp * lp, bs * ls
    x = (x_prompt.reshape(tp, D_MODEL), x_sample.reshape(ts, D_MODEL))

    def router_params(layer):
        return router_w[layer].T.astype(BF16), router_b[layer].astype(F32).reshape(N_EXPERTS, 1)

    def finish_layer(layer, x, mix_a, mix_b, w_out):
        rwt, rb = router_params(layer)
        x1, xp, gates, eidx, rank, counts = _outproj_ln_router(
            mix_a, mix_b, x[0], x[1], w_out[:512].astype(BF16), w_out[512:].astype(BF16),
            _row(ln1_g[layer]), _row(ln1_b[layer]), rwt, rb)
        return _moe(layer, x1, xp, gates, eidx, rank, counts, _row(ln2_g[layer]), _row(ln2_b[layer]),
                    expert_w_gate, expert_b_gate, expert_w_up, expert_b_up, expert_w_down, expert_b_down, tp)

    t_all = tp + ts
    fresh = jnp.zeros((t_all, 512), BF16)

    wi = w_in_even[0]
    wq, wk, wv, wg, wlr, wglu = jnp.split(wi, [256, 512, 1024, 1536, 1552], axis=1)
    w_even = jnp.concatenate([_pad_heads(wq, GLA_HEADS, GLA_DK), _pad_heads(wk, GLA_HEADS, GLA_DK), wv, wg, wglu,
                              jnp.pad(wlr, ((0, 0), (0, LANE - GLA_RANK)))], axis=1).astype(BF16)
    cols_gla = (0, 512, 1024, 1536, 3072)
    col_a, col_gate = 2048, 2560
    h = _inproj(x[0], x[1], w_even)
    wlr_p = jnp.pad(_pad_heads(w_gla_gate_lr[0], GLA_HEADS, GLA_DK), ((0, LANE - GLA_RANK), (0, 0)))
    blr_p = _row(_pad_heads(b_gla_gate[0], GLA_HEADS, GLA_DK))
    nw = _row(gla_norm_w[0])
    conf_args = (conf_conv_w[0], _row(conf_conv_b[0]), _row(conf_ln_g[0]), _row(conf_ln_b[0]))
    s0_p = jnp.zeros((bp, GLA_HEADS, GLA_DK, HEAD_W), F32)
    s0_s = state_gla[0]
    mix_a, sg_p = _gla_call("gla", h, cols_gla, (wlr_p, blr_p), nw, s0_p, fresh, t_all, bp, lp, 0, False)
    mix_a, sg_s = _gla_call("gla", h, cols_gla, (wlr_p, blr_p), nw, s0_s, mix_a, t_all, bs, ls, tp, True)
    mix_b, cc_p = _conf_call(h, col_a, col_gate, jnp.zeros((bp,) + cache_conformer.shape[2:], F32), *conf_args,
                             fresh, t_all, bp, lp, 0, False)
    mix_b, cc_s = _conf_call(h, col_a, col_gate, cache_conformer[0], *conf_args, mix_b, t_all, bs, ls, tp, True)
    x = finish_layer(0, x, mix_a, mix_b, w_out_even[0])
    gla_p, gla_s = sg_p[None], sg_s[None]
    conf_p, conf_s = cc_p[None], cc_s[None]

    lb_cum = jnp.cumsum(jax.nn.softmax(hgrn_lower_bounds.astype(F32), axis=0), axis=0)
    lower_bound = _row((lb_cum - lb_cum[0])[1])
    wo = w_in_odd[0]
    w_odd = jnp.concatenate([wo[:, 2560:3584], wo[:, :2560],
                             jnp.pad(wo[:, 3584:], ((0, 0), (0, LANE - SSM_HEADS)))], axis=1).astype(BF16)
    h = _inproj(x[0], x[1], w_odd)
    cols_hgrn = (1024, 1536, 2048, 2560)
    col_z, col_x, col_dt = 3072, 0, 3584
    nw = _row(hgrn_norm_w[0])
    mix_a, sh_p = _gla_call("hgrn", h, cols_hgrn, lower_bound, nw,
                            jnp.zeros((bp, HGRN_HEADS, HEAD_W, HEAD_W), F32), fresh, t_all, bp, lp, 0, False)
    mix_a, sh_s = _gla_call("hgrn", h, cols_hgrn, lower_bound, nw, state_hgrn[0], mix_a, t_all, bs, ls, tp, True)

    def pair_states(s):
        return jnp.swapaxes(s, 2, 3).reshape(s.shape[0], SSM_HEADS // 2, HEAD_W, SSM_STATE)

    def unpair_states(s):
        return jnp.swapaxes(s.reshape(s.shape[0], SSM_HEADS, SSM_HEADDIM, SSM_STATE), 2, 3)

    pad8 = lambda v: jnp.pad(v.astype(F32), (0, LANE - SSM_HEADS)).reshape(1, LANE)
    ssd_args = (mamba_conv_w[0], _row(mamba_conv_b[0]), pad8(mamba_dt_bias[0]), pad8(mamba_a_log[0]),
                _row(jnp.repeat(mamba_d[0], SSM_HEADDIM)), _row(mamba_norm_w[0]))
    mix_b, cm_p, ss_p = _ssd_call(h, col_z, col_x, col_dt, jnp.zeros((bp,) + cache_mamba_conv.shape[2:], F32),
                                  jnp.zeros((bp, SSM_HEADS // 2, HEAD_W, SSM_STATE), F32), *ssd_args,
                                  fresh, t_all, bp, lp, 0, False)
    mix_b, cm_s, ss_s = _ssd_call(h, col_z, col_x, col_dt, cache_mamba_conv[0], pair_states(state_ssm[0]),
                                  *ssd_args, mix_b, t_all, bs, ls, tp, True)
    y_prompt, y_sample = finish_layer(1, x, mix_a, mix_b, w_out_odd[0])
    y_prompt = y_prompt.reshape(bp, lp, D_MODEL)
    y_sample = y_sample.reshape(bs, ls, D_MODEL)
    return (y_prompt, y_sample, gla_p, gla_s, conf_p, conf_s, sh_p[None], sh_s[None],
            unpair_states(ss_p)[None], unpair_states(ss_s)[None], cm_p[None], cm_s[None])
```

```python
import functools

import jax
import jax.numpy as jnp
from jax import lax
from jax.experimental import pallas as pl
from jax.experimental.pallas import tpu as pltpu
from jax.experimental.pallas import tpu_sc as plsc

F32 = jnp.float32
BF16 = jnp.bfloat16
I32 = jnp.int32
U32 = jnp.uint32

D_MODEL = 1024
DEPTH = 2
DEEPNORM_ALPHA = (2.0 * DEPTH) ** 0.25
LN_EPS = 1e-5
RMS_EPS = 1e-6
LANE = 128
HEAD_W = 128
GLA_HEADS, GLA_DK, GLA_RANK, GLA_TAU = 4, 64, 16, 16.0
CONF_DIM, CONF_WIDTH = 512, 31
HGRN_HEADS, HGRN_DK = 4, 128
SSM_HEADS, SSM_HEADDIM, SSM_STATE, SSM_GROUPS, SSM_CONV = 8, 64, 128, 2, 4
SSM_INNER = SSM_HEADS * SSM_HEADDIM
SSM_CONV_DIM = SSM_INNER + 2 * SSM_GROUPS * SSM_STATE
N_EXPERTS, TOP_K = 32, 4
SWIGLU_ALPHA, SWIGLU_LIMIT = 1.702, 7.0
SCAN_CHUNK = 64
PROMPT_TILE = 512
SSD_PROMPT_TILE = 256
SAMPLE_SEQS = 16
TOKEN_TILE = 512
WIDE_TOKEN_TILE = 1024
MOE_ROWS = 512
IN_SLOTS = 4
SC_CORES, SC_SUBCORES = 2, 16
SC_ROWS = 64
SC_SCATTER_ROWS = 32
VMEM_LIMIT = 56 * 1024 * 1024


def _cparams(n_axes):
    return pltpu.CompilerParams(dimension_semantics=("arbitrary",) * n_axes, vmem_limit_bytes=VMEM_LIMIT)


def _silu(x):
    return x * jax.nn.sigmoid(x)


def _softplus(x):
    return jnp.maximum(x, 0.0) + jnp.log(1.0 + jnp.exp(-jnp.abs(x)))


def _log_sigmoid(x):
    return jnp.minimum(x, 0.0) - jnp.log(1.0 + jnp.exp(-jnp.abs(x)))


def _layernorm(y, g, b):
    mu = jnp.mean(y, axis=-1, keepdims=True)
    d = y - mu
    var = jnp.mean(d * d, axis=-1, keepdims=True)
    return d * lax.rsqrt(var + LN_EPS) * g + b


def _dot(a, b):
    return jnp.dot(a.astype(BF16), b.astype(BF16), preferred_element_type=F32)


def _dot_nt(a, b):
    return lax.dot_general(a.astype(BF16), b.astype(BF16), (((1,), (1,)), ((), ())), preferred_element_type=F32)


def _dot_tn(a, b):
    return lax.dot_general(a.astype(BF16), b.astype(BF16), (((0,), (0,)), ((), ())), preferred_element_type=F32)


def _tri(c):
    r = lax.broadcasted_iota(I32, (c, c), 0)
    k = lax.broadcasted_iota(I32, (c, c), 1)
    return r >= k


def _pair_specs(tm, n_first, width):
    return [pl.BlockSpec((tm, width), lambda i: (jnp.minimum(i, n_first - 1), 0)),
            pl.BlockSpec((tm, width), lambda i: (jnp.maximum(i - n_first, 0), 0))]


def _pair_tile(n_first, xa_ref, xb_ref):
    return jnp.where(pl.program_id(0) < n_first, xa_ref[...], xb_ref[...])


def _inproj_body(n_first, xa_ref, xb_ref, w_ref, o_ref):
    xb = _pair_tile(n_first, xa_ref, xb_ref).astype(BF16)
    n = w_ref.shape[1]
    for c0 in range(0, n, 512):
        c1 = min(c0 + 512, n)
        o_ref[:, c0:c1] = jnp.dot(xb, w_ref[:, c0:c1], preferred_element_type=F32)


def _inproj(xa, xb, w):
    k, n = w.shape
    t = xa.shape[0] + xb.shape[0]
    n_first = xa.shape[0] // TOKEN_TILE
    return pl.pallas_call(
        functools.partial(_inproj_body, n_first),
        grid=(t // TOKEN_TILE,),
        in_specs=_pair_specs(TOKEN_TILE, n_first, k) + [pl.BlockSpec((k, n), lambda i: (0, 0))],
        out_specs=pl.BlockSpec((TOKEN_TILE, n), lambda i: (i, 0)),
        out_shape=jax.ShapeDtypeStruct((t, n), F32),
        compiler_params=_cparams(1),
        name="inproj",
    )(xa, xb, w)


def _pack_halves(y):
    half = y.shape[1] // 2
    hi = lax.bitcast_convert_type(y[:, :half].astype(BF16).astype(F32), U32)
    lo = lax.bitcast_convert_type(y[:, half:].astype(BF16).astype(F32), U32)
    return (hi & jnp.uint32(0xFFFF0000)) | (lo >> 16)


def _unpack_halves(w):
    hi = lax.bitcast_convert_type(w & jnp.uint32(0xFFFF0000), F32)
    lo = lax.bitcast_convert_type(w << 16, F32)
    return hi, lo


def _outproj_body(n_first, a_ref, b_ref, xa_ref, xb_ref, wa_ref, wb_ref, g_ref, be_ref, rwt_ref, rb_ref,
                  x1_ref, xp_ref, gate_ref, idx_ref, rank_ref, cnt_ref, carry):
    @pl.when(pl.program_id(0) == 0)
    def _():
        carry[...] = jnp.zeros(carry.shape, F32)

    mix = (jnp.dot(a_ref[...], wa_ref[...], preferred_element_type=F32)
           + jnp.dot(b_ref[...], wb_ref[...], preferred_element_type=F32))
    x1 = _layernorm(DEEPNORM_ALPHA * _pair_tile(n_first, xa_ref, xb_ref) + mix, g_ref[...], be_ref[...])
    x1_ref[...] = x1
    xp_ref[...] = _pack_halves(x1)
    logits = _dot_nt(rwt_ref[...], x1) + rb_ref[...]
    tm = logits.shape[1]
    expert = lax.broadcasted_iota(I32, logits.shape, 0)
    vals, idxs = [], []
    for _ in range(TOP_K):
        m = jnp.max(logits, axis=0, keepdims=True)
        sel = jnp.min(jnp.where(logits == m, expert, N_EXPERTS), axis=0, keepdims=True)
        vals.append(m)
        idxs.append(sel)
        logits = jnp.where(expert == sel, -jnp.inf, logits)
    exps = [jnp.exp(v - vals[0]) for v in vals]
    inv = 1.0 / functools.reduce(lambda p, q: p + q, exps)
    chosen = jnp.zeros(logits.shape, F32)
    for k in range(TOP_K):
        chosen = chosen + (expert == idxs[k]).astype(F32)
    earlier = lax.broadcasted_iota(I32, (tm, tm), 0) < lax.broadcasted_iota(I32, (tm, tm), 1)
    before = carry[...] + jnp.dot(chosen.astype(BF16), earlier.astype(BF16), preferred_element_type=F32)
    choice = lax.broadcasted_iota(I32, (8, tm), 0)
    gates = jnp.zeros((8, tm), F32)
    eidx = jnp.zeros((8, tm), I32)
    ranks = jnp.zeros((8, tm), F32)
    for k in range(TOP_K):
        rk = jnp.sum(jnp.where(expert == idxs[k], before, 0.0), axis=0, keepdims=True)
        gates = jnp.where(choice == k, exps[k] * inv, gates)
        eidx = jnp.where(choice == k, idxs[k], eidx)
        ranks = jnp.where(choice == k, rk, ranks)
    gate_ref[...] = gates
    idx_ref[...] = eidx
    rank_ref[...] = ranks.astype(I32)
    carry[...] = carry[...] + jnp.sum(chosen, axis=1, keepdims=True)
    cnt_ref[...] = carry[...].astype(I32)


def _outproj_ln_router(a, b, xa, xb, wa, wb, g, be, rwt, rb):
    t = xa.shape[0] + xb.shape[0]
    tm = WIDE_TOKEN_TILE
    n_first = xa.shape[0] // tm
    row = lambda i: (i, 0)
    col = lambda i: (0, i)
    fix = lambda i: (0, 0)
    return pl.pallas_call(
        functools.partial(_outproj_body, n_first),
        grid=(t // tm,),
        in_specs=[pl.BlockSpec((tm, 512), row), pl.BlockSpec((tm, 512), row)] + _pair_specs(tm, n_first, D_MODEL)
        + [pl.BlockSpec((512, D_MODEL), fix), pl.BlockSpec((512, D_MODEL), fix),
           pl.BlockSpec((1, D_MODEL), fix), pl.BlockSpec((1, D_MODEL), fix),
           pl.BlockSpec((N_EXPERTS, D_MODEL), fix), pl.BlockSpec((N_EXPERTS, 1), fix)],
        out_specs=[pl.BlockSpec((tm, D_MODEL), row), pl.BlockSpec((tm, 512), row),
                   pl.BlockSpec((8, tm), col), pl.BlockSpec((8, tm), col), pl.BlockSpec((8, tm), col),
                   pl.BlockSpec((N_EXPERTS, 1), fix)],
        out_shape=[jax.ShapeDtypeStruct((t, D_MODEL), F32), jax.ShapeDtypeStruct((t, 512), U32),
                   jax.ShapeDtypeStruct((8, t), F32), jax.ShapeDtypeStruct((8, t), I32),
                   jax.ShapeDtypeStruct((8, t), I32), jax.ShapeDtypeStruct((N_EXPERTS, 1), I32)],
        scratch_shapes=[pltpu.VMEM((N_EXPERTS, 1), F32)],
        compiler_params=_cparams(1),
        name="outproj_ln_router",
    )(a, b, xa, xb, wa, wb, g, be, rwt, rb)


def _sc_mesh():
    return plsc.VectorSubcoreMesh(core_axis_name="c", subcore_axis_name="s")


def _sc_scatter_rows(src, dest, n_out):
    n_src, w = src.shape
    n_dst = dest.shape[1]
    workers = SC_CORES * SC_SUBCORES
    per_worker = n_src // workers
    chunks = per_worker // SC_SCATTER_ROWS
    assert n_src == workers * chunks * SC_SCATTER_ROWS
    idx = dest.T.reshape(n_dst, workers, chunks, SC_SCATTER_ROWS)

    @functools.partial(pl.kernel, mesh=_sc_mesh(), out_type=jax.ShapeDtypeStruct((n_out, w), src.dtype),
                       scratch_types=[pltpu.VMEM((n_dst, chunks, SC_SCATTER_ROWS), I32)]
                       + [pltpu.VMEM((SC_SCATTER_ROWS, w), src.dtype)] * 2 + [pltpu.SemaphoreType.DMA] * 4)
    def scatter(src_hbm, idx_hbm, out_hbm, idx_v, rows_a, rows_b, sem_ra, sem_rb, sem_wa, sem_wb):
        worker = lax.axis_index("s") * SC_CORES + lax.axis_index("c")
        base = worker * per_worker
        for k in range(n_dst):
            pltpu.sync_copy(idx_hbm.at[k, worker], idx_v.at[k])

        def read(c, rows, sem):
            return pltpu.async_copy(src_hbm.at[pl.ds(pl.multiple_of(base + c * SC_SCATTER_ROWS, 8), SC_SCATTER_ROWS)],
                                    rows, sem)

        def write_all(pending_read, c, rows, sem):
            pending_read.wait()
            return [pltpu.async_copy(rows, out_hbm.at[idx_v.at[k, c]], sem) for k in range(n_dst)]

        @pl.loop(0, chunks // 2)
        def _(p):
            read_a = read(2 * p, rows_a, sem_ra)
            read_b = read(2 * p + 1, rows_b, sem_rb)
            writes = write_all(read_a, 2 * p, rows_a, sem_wa) + write_all(read_b, 2 * p + 1, rows_b, sem_wb)
            for wr in writes:
                wr.wait()

        if chunks % 2:
            for wr in write_all(read(chunks - 1, rows_a, sem_ra), chunks - 1, rows_a, sem_wa):
                wr.wait()

    return scatter(src, idx)


def _sc_gather_rows(table, idx):
    n, w = idx.shape[0], table.shape[1]
    workers = SC_CORES * SC_SUBCORES
    chunks = n // (workers * SC_ROWS)
    assert n == workers * chunks * SC_ROWS and chunks % 2 == 0
    idx = idx.reshape(workers, chunks, SC_ROWS)

    @functools.partial(pl.kernel, mesh=_sc_mesh(), out_type=jax.ShapeDtypeStruct((n, w), table.dtype),
                       scratch_types=[pltpu.VMEM((chunks, SC_ROWS), I32)] + [pltpu.VMEM((SC_ROWS, w), table.dtype)] * 2
                       + [pltpu.SemaphoreType.DMA] * 4)
    def gather(table_hbm, idx_hbm, out_hbm, idx_v, rows_a, rows_b, sem_ra, sem_rb, sem_wa, sem_wb):
        worker = lax.axis_index("s") * SC_CORES + lax.axis_index("c")
        base = worker * (chunks * SC_ROWS)
        pltpu.sync_copy(idx_hbm.at[worker], idx_v)

        def out_rows(c):
            return out_hbm.at[pl.ds(pl.multiple_of(base + c * SC_ROWS, 8), SC_ROWS)]

        @pl.loop(0, chunks // 2)
        def _(p):
            read_a = pltpu.async_copy(table_hbm.at[idx_v.at[2 * p]], rows_a, sem_ra)
            read_b = pltpu.async_copy(table_hbm.at[idx_v.at[2 * p + 1]], rows_b, sem_rb)
            read_a.wait()
            write_a = pltpu.async_copy(rows_a, out_rows(2 * p), sem_wa)
            read_b.wait()
            write_b = pltpu.async_copy(rows_b, out_rows(2 * p + 1), sem_wb)
            write_a.wait()
            write_b.wait()

    return gather(table, idx)


def _experts_body(b0_ref, nb_ref, last_ref, nt_ref, xs_hbm, wg_ref, bg_ref, wu_ref, bu_ref, wd_ref, bd_ref, o_hbm,
                  wg_s, wu_s, wd_s, xbuf, obuf, sem_in, sem_out):
    e = pl.program_id(0)
    first_blk, n_blk, last_valid, n_total = b0_ref[e], nb_ref[e], last_ref[e], nt_ref[0]
    quarter = MOE_ROWS // 4

    def rows_of(g):
        return pl.ds(pl.multiple_of(g * MOE_ROWS, MOE_ROWS), MOE_ROWS)

    def fetch(g, slot):
        return pltpu.make_async_copy(xs_hbm.at[rows_of(g)], xbuf.at[slot], sem_in.at[slot])

    def put(g, slot):
        return pltpu.make_async_copy(obuf.at[slot], o_hbm.at[rows_of(g)], sem_out.at[slot])

    lead = IN_SLOTS - 1
    for first in range(lead):
        @pl.when((e == 0) & (n_total > first))
        def _():
            fetch(first, first).start()

    @pl.when(n_blk > 0)
    def _():
        wg_s[...] = wg_ref[...].astype(BF16)
        wu_s[...] = wu_ref[...].astype(BF16)
        wd_s[...] = wd_ref[...].astype(BF16)

    def compute(islot, slot, rows):
        half = D_MODEL // 2
        x_hi, x_lo = _unpack_halves(xbuf[islot, 0:rows, :])
        x_hi = x_hi.astype(BF16)
        x_lo = x_lo.astype(BF16)
        g = (jnp.dot(x_hi, wg_s[:half, :], preferred_element_type=F32)
             + jnp.dot(x_lo, wg_s[half:, :], preferred_element_type=F32) + bg_ref[...])
        u = (jnp.dot(x_hi, wu_s[:half, :], preferred_element_type=F32)
             + jnp.dot(x_lo, wu_s[half:, :], preferred_element_type=F32) + bu_ref[...])
        g = jnp.minimum(g, SWIGLU_LIMIT)
        u = jnp.clip(u, -SWIGLU_LIMIT, SWIGLU_LIMIT)
        hmid = (u + 1.0) * (g * jax.nn.sigmoid(SWIGLU_ALPHA * g))
        out = jnp.dot(hmid.astype(BF16), wd_s[...], preferred_element_type=F32) + bd_ref[...]
        obuf[slot, 0:rows, :] = _pack_halves(out)

    def block(j, carry):
        g = first_blk + j
        slot = lax.rem(g, 2)
        islot = lax.rem(g, IN_SLOTS)
        fetch(g, islot).wait()

        @pl.when(g + lead < n_total)
        def _():
            fetch(g + lead, lax.rem(g + lead, IN_SLOTS)).start()

        @pl.when(g >= 2)
        def _():
            put(g - 2, slot).wait()

        valid = jnp.where(j == n_blk - 1, last_valid, MOE_ROWS)

        for rows in range(quarter, MOE_ROWS + 1, quarter):
            @pl.when((valid > rows - quarter) & (valid <= rows))
            def _():
                compute(islot, slot, rows)
                if rows < MOE_ROWS:
                    obuf[slot, rows:, :] = jnp.zeros((MOE_ROWS - rows, obuf.shape[2]), obuf.dtype)

        put(g, slot).start()
        return carry

    lax.fori_loop(0, n_blk, block, 0)

    @pl.when((e == N_EXPERTS - 1) & (n_total >= 2))
    def _():
        put(n_total - 2, lax.rem(n_total, 2)).wait()

    @pl.when((e == N_EXPERTS - 1) & (n_total >= 1))
    def _():
        put(n_total - 1, lax.rem(n_total - 1, 2)).wait()


def _experts(layer, first_blk, n_blk, last_valid, xs, wg, bg, wu, bu, wd, bd):
    n_rows, w = xs.shape
    wsel = lambda e, b0, nb, lv, nt: (layer, e, 0, 0)
    wspec = pl.BlockSpec((None, None, D_MODEL, D_MODEL), wsel)
    bspec = pl.BlockSpec((None, None, 1, D_MODEL), wsel)
    bias = lambda b: b.reshape(b.shape[0], b.shape[1], 1, b.shape[2])
    return pl.pallas_call(
        _experts_body,
        grid_spec=pltpu.PrefetchScalarGridSpec(
            num_scalar_prefetch=4,
            grid=(N_EXPERTS,),
            in_specs=[pl.BlockSpec(memory_space=pl.ANY), wspec, bspec, wspec, bspec, wspec, bspec],
            out_specs=pl.BlockSpec(memory_space=pl.ANY),
            scratch_shapes=[pltpu.VMEM((D_MODEL, D_MODEL), BF16)] * 3
            + [pltpu.VMEM((IN_SLOTS, MOE_ROWS, w), U32), pltpu.VMEM((2, MOE_ROWS, w), U32),
               pltpu.SemaphoreType.DMA((IN_SLOTS,)), pltpu.SemaphoreType.DMA((2,))],
        ),
        out_shape=jax.ShapeDtypeStruct((n_rows, w), U32),
        compiler_params=_cparams(1),
        name="experts",
    )(first_blk, n_blk, last_valid, jnp.sum(n_blk).reshape(1), xs, wg, bias(bg), wu, bias(bu), wd, bias(bd))


def _combine_body(n_first, o0_ref, o1_ref, o2_ref, o3_ref, gt_ref, x_ref, g_ref, b_ref, ya_ref, yb_ref=None):
    half = D_MODEL // 2
    gates = gt_ref[...]
    hi = jnp.zeros((x_ref.shape[0], half), F32)
    lo = jnp.zeros((x_ref.shape[0], half), F32)
    for k, o_ref in enumerate((o0_ref, o1_ref, o2_ref, o3_ref)):
        h, l = _unpack_halves(o_ref[...])
        gk = gates[:, k:k + 1]
        hi = hi + gk * h
        lo = lo + gk * l
    x = x_ref[...]
    y_hi = DEEPNORM_ALPHA * x[:, :half] + hi
    y_lo = DEEPNORM_ALPHA * x[:, half:] + lo
    mu = (jnp.sum(y_hi, axis=-1, keepdims=True) + jnp.sum(y_lo, axis=-1, keepdims=True)) * (1.0 / D_MODEL)
    d_hi = y_hi - mu
    d_lo = y_lo - mu
    var = (jnp.sum(d_hi * d_hi, axis=-1, keepdims=True) + jnp.sum(d_lo * d_lo, axis=-1, keepdims=True)) * (1.0 / D_MODEL)
    r = lax.rsqrt(var + LN_EPS)
    out_hi = d_hi * r * g_ref[:, :half] + b_ref[:, :half]
    out_lo = d_lo * r * g_ref[:, half:] + b_ref[:, half:]

    def write(y_ref):
        y_ref[:, :half] = out_hi
        y_ref[:, half:] = out_lo

    if yb_ref is None:
        write(ya_ref)
    else:
        pl.when(pl.program_id(0) < n_first)(lambda: write(ya_ref))
        pl.when(pl.program_id(0) >= n_first)(lambda: write(yb_ref))


def _combine_ln(o4, gates, x, g, b, t_first=None):
    t = x.shape[0]
    tm = WIDE_TOKEN_TILE
    row = lambda i: (i, 0)
    fix = lambda i: (0, 0)
    choice = lambda k: pl.BlockSpec((tm, 512), lambda i: (k * (t // tm) + i, 0))
    if t_first is None:
        n_first = None
        out_specs = pl.BlockSpec((tm, D_MODEL), row)
        out_shape = jax.ShapeDtypeStruct((t, D_MODEL), F32)
    else:
        n_first = t_first // tm
        out_specs = [pl.BlockSpec((tm, D_MODEL), lambda i: (jnp.minimum(i, n_first - 1), 0)),
                     pl.BlockSpec((tm, D_MODEL), lambda i: (jnp.maximum(i - n_first, 0), 0))]
        out_shape = [jax.ShapeDtypeStruct((t_first, D_MODEL), F32), jax.ShapeDtypeStruct((t - t_first, D_MODEL), F32)]
    return pl.pallas_call(
        functools.partial(_combine_body, n_first),
        grid=(t // tm,),
        in_specs=[choice(0), choice(1), choice(2), choice(3), pl.BlockSpec((tm, TOP_K), row),
                  pl.BlockSpec((tm, D_MODEL), row), pl.BlockSpec((1, D_MODEL), fix), pl.BlockSpec((1, D_MODEL), fix)],
        out_specs=out_specs,
        out_shape=out_shape,
        compiler_params=_cparams(1),
        name="combine_ln",
    )(o4, o4, o4, o4, gates, x, g, b)


def _moe(layer, x1, xp, gates, eidx, rank, counts, ln_g, ln_b, wg, bg, wu, bu, wd, bd, t_first=None):
    t = x1.shape[0]
    bm = MOE_ROWS
    n_blocks = t * TOP_K // bm + N_EXPERTS
    n_rows = n_blocks * bm
    cnt = counts[:, 0]
    padded = (cnt + bm - 1) // bm * bm
    pad_end = jnp.cumsum(padded)
    pad_start = pad_end - padded
    e = eidx[:TOP_K]
    start = jnp.sum(jnp.where(e[:, :, None] == jnp.arange(N_EXPERTS, dtype=I32), pad_start, 0), axis=-1)
    dest = (start + rank[:TOP_K]).T
    n_blk = padded // bm
    last_valid = cnt - (n_blk - 1) * bm
    xs = _sc_scatter_rows(xp, dest, n_rows)
    outs = _experts(layer, pad_start // bm, n_blk, last_valid, xs, wg, bg, wu, bu, wd, bd)
    o4 = _sc_gather_rows(outs, dest.T.reshape(-1))
    return _combine_ln(o4, gates[:TOP_K].T, x1, ln_g, ln_b, t_first)


def _split2(x):
    hi = x.astype(BF16)
    return hi, (x - hi.astype(F32)).astype(BF16)


def _split3(x):
    hi = x.astype(BF16)
    rem = x - hi.astype(F32)
    mid = rem.astype(BF16)
    return hi, mid, (rem - mid.astype(F32)).astype(BF16)


def _chunk_cumsum(g, C):
    rows = g.shape[0]
    r = lax.broadcasted_iota(I32, (rows, rows), 0)
    c = lax.broadcasted_iota(I32, (rows, rows), 1)
    tri = ((r >= c) & (r // C == c // C)).astype(BF16)
    hi, mid, lo = _split3(g)
    dot = lambda part: jnp.dot(tri, part, preferred_element_type=F32)
    return dot(hi) + dot(mid) + dot(lo)


def _gla_batched_step(q, k, v, g, C, n_seq, n_heads, state_of, o_scr):
    rows = n_seq * C
    wide = n_seq * HEAD_W
    mid = max(C // 2 - 1, 0)
    r = lax.broadcasted_iota(I32, (rows, rows), 0)
    c = lax.broadcasted_iota(I32, (rows, rows), 1)
    same = (r // C) == (c // C)
    causal = same & (r >= c)
    parts = _split3(g)
    summed = lambda mask: functools.reduce(lambda p, q_: p + q_, [
        jnp.dot(mask.astype(BF16), part, preferred_element_type=F32) for part in parts])
    b = summed(causal)
    b_mid = summed(same & ((c % C) <= mid))
    b_last = summed(same)
    qe_hi, qe_lo = _split2(q * jnp.exp(b - b_mid))
    ke_hi, ke_lo = _split2(k * jnp.exp(b_mid - b))
    q_state = (q * jnp.exp(b)).astype(BF16)
    k_state = (k * jnp.exp(b_last - b)).astype(BF16)
    decay_parts = _split3(jnp.exp(b_last))
    row_w = lax.broadcasted_iota(I32, (rows, wide), 0)
    blk_w = lax.broadcasted_iota(I32, (rows, wide), 1) // HEAD_W
    own = (row_w // C) == blk_w
    pick = (row_w == blk_w * C).astype(BF16)
    new_states = []
    for h in range(n_heads):
        cs = slice(h * HEAD_W, (h + 1) * HEAD_W)
        lhs = jnp.concatenate([qe_hi[:, cs], qe_hi[:, cs], qe_lo[:, cs]], axis=1)
        rhs = jnp.concatenate([ke_hi[:, cs], ke_lo[:, cs], ke_hi[:, cs]], axis=1)
        scores = jnp.where(causal, _dot_nt(lhs, rhs), 0.0)
        vh = v[:, cs].astype(BF16)
        s_cat = jnp.concatenate([state_of(s, h) for s in range(n_seq)], axis=1)
        o_full = _dot(q_state[:, cs], s_cat)
        o_state = jnp.concatenate([o_full[s * C:(s + 1) * C, s * HEAD_W:(s + 1) * HEAD_W] for s in range(n_seq)],
                                  axis=0)
        o = _dot(scores, vh) + o_state
        v_wide = jnp.where(own, jnp.concatenate([vh] * n_seq, axis=1), jnp.zeros((), BF16))
        kv = _dot_tn(k_state[:, cs], v_wide)
        decay = functools.reduce(lambda p, q_: p + q_, [
            lax.dot_general(part[:, cs], pick, (((0,), (0,)), ((), ())), preferred_element_type=F32)
            for part in decay_parts])
        new_states.append(s_cat * decay + kv)
        ms = jnp.mean(o * o, axis=-1, keepdims=True)
        o_scr[:, cs] = o * lax.rsqrt(ms + RMS_EPS)
    return new_states


def _gla_body(mode, n_seq, n_chunk, C, *refs):
    if mode == "gla":
        hq_ref, hk_ref, hv_ref, hg_ref, hlr_ref, wlr_ref, blr_ref, nw_ref, s0_ref, _, o_ref, so_ref, st, o_scr = refs
    else:
        hq_ref, hk_ref, hv_ref, hg_ref, lb_ref, nw_ref, s0_ref, _, o_ref, so_ref, st, o_scr = refs
    n_heads = 4
    n_keys = s0_ref.shape[2]
    tstep = pl.program_id(1)
    batched = n_chunk == 1 and n_seq > 1

    def padded_state(s, h):
        s_in = s0_ref[s, h]
        if n_keys < HEAD_W:
            s_in = jnp.concatenate([s_in, jnp.zeros((HEAD_W - n_keys, HEAD_W), F32)], axis=0)
        return s_in

    if not batched:
        @pl.when(tstep == 0)
        def _():
            for s in range(n_seq):
                for h in range(n_heads):
                    st[s, h] = padded_state(s, h).T

    if mode == "gla":
        q = hq_ref[...] * (GLA_DK ** -0.5)
        k = hk_ref[...]
        z = _dot(hlr_ref[...], wlr_ref[...]) + blr_ref[...]
        g = _log_sigmoid(z) * (1.0 / GLA_TAU)
    else:
        q = _silu(hq_ref[...]) * (HGRN_DK ** -0.5)
        lb = lb_ref[...]
        f = lb + (1.0 - lb) * jax.nn.sigmoid(hk_ref[...])
        k = 1.0 - f
        g = jnp.log(f)
    v = hv_ref[...]
    if batched:
        new_states = _gla_batched_step(q, k, v, g, C, n_seq, n_heads, padded_state, o_scr)
        for s in range(n_seq):
            for h in range(n_heads):
                so_ref[s, h] = new_states[h][0:n_keys, s * HEAD_W:(s + 1) * HEAD_W]
    else:
        causal = _tri(C)
        mid = max(C // 2 - 1, 0)
        b_all = _chunk_cumsum(g, C)
        for s in range(n_seq):
            states = [st[s, h] for h in range(n_heads)]
            for c in range(n_chunk):
                r0 = (s * n_chunk + c) * C
                rs = slice(r0, r0 + C)
                b, qc, kc = b_all[rs, :], q[rs, :], k[rs, :]
                b_last = b[C - 1:C, :]
                b_mid = b[mid:mid + 1, :]
                qe_hi, qe_lo = _split2(qc * jnp.exp(b - b_mid))
                ke_hi, ke_lo = _split2(kc * jnp.exp(b_mid - b))
                q_state = (qc * jnp.exp(b)).astype(BF16)
                k_state = (kc * jnp.exp(b_last - b)).astype(BF16)
                decay = jnp.exp(b_last)
                for h in range(n_heads):
                    cs = slice(h * HEAD_W, (h + 1) * HEAD_W)
                    lhs = jnp.concatenate([qe_hi[:, cs], qe_hi[:, cs], qe_lo[:, cs]], axis=1)
                    rhs = jnp.concatenate([ke_hi[:, cs], ke_lo[:, cs], ke_hi[:, cs]], axis=1)
                    scores = jnp.where(causal, _dot_nt(lhs, rhs), 0.0)
                    vh = v[rs, cs].astype(BF16)
                    o = _dot(scores, vh) + _dot_nt(q_state[:, cs], states[h])
                    states[h] = states[h] * decay[:, cs] + _dot_tn(vh, k_state[:, cs])
                    ms = jnp.mean(o * o, axis=-1, keepdims=True)
                    o_scr[rs, cs] = o * lax.rsqrt(ms + RMS_EPS)
            for h in range(n_heads):
                st[s, h] = states[h]
    o_ref[...] = (o_scr[...] * nw_ref[...] * _silu(hg_ref[...])).astype(BF16)

    if not batched:
        @pl.when(tstep == pl.num_programs(1) - 1)
        def _():
            for s in range(n_seq):
                for h in range(n_heads):
                    so_ref[s, h] = st[s, h].T[0:n_keys, :]


def _alias_if_full(out_buf, mix_rows, arg_index):
    return {arg_index: 0} if out_buf.shape == (mix_rows, 512) else {}


def _seq_layout(n_batch, seq_len, row_off, sample, prompt_tile=PROMPT_TILE):
    if sample:
        n_seq, n_chunk, C = SAMPLE_SEQS, 1, seq_len
        rows = n_seq * C
        grid = (n_batch // n_seq, 1)
        blk0 = row_off // rows
        rb = lambda i, t: blk0 + i
    else:
        n_seq, n_chunk, C = 1, prompt_tile // SCAN_CHUNK, SCAN_CHUNK
        rows = prompt_tile
        tiles = seq_len // rows
        grid = (n_batch, tiles)
        blk0 = row_off // rows
        rb = lambda i, t: blk0 + i * tiles + t
    return n_seq, n_chunk, C, rows, grid, rb


def _gla_call(mode, h, cols, extra, nw, s0, out_buf, mix_rows, n_batch, seq_len, row_off, sample):
    n_seq, n_chunk, C, rows, grid, rb = _seq_layout(n_batch, seq_len, row_off, sample)
    colspec = lambda c0, w: pl.BlockSpec((rows, w), lambda i, t: (rb(i, t), c0 // w))
    fix2 = lambda i, t: (0, 0)
    in_specs = [colspec(cols[0], 512), colspec(cols[1], 512), colspec(cols[2], 512), colspec(cols[3], 512)]
    args = [h, h, h, h]
    if mode == "gla":
        wlr, blr = extra
        in_specs += [colspec(cols[4], LANE), pl.BlockSpec((LANE, 512), fix2), pl.BlockSpec((1, 512), fix2)]
        args += [h, wlr, blr]
    else:
        in_specs += [pl.BlockSpec((1, 512), fix2)]
        args += [extra]
    n_keys = s0.shape[2]
    st_spec = pl.BlockSpec((n_seq, 4, n_keys, HEAD_W), lambda i, t: (i, 0, 0, 0))
    in_specs += [pl.BlockSpec((1, 512), fix2), st_spec, pl.BlockSpec(memory_space=pl.ANY)]
    args += [nw, s0, out_buf]
    o_spec = pl.BlockSpec((rows, 512), lambda i, t: (rb(i, t), 0))
    return pl.pallas_call(
        functools.partial(_gla_body, mode, n_seq, n_chunk, C),
        grid=grid,
        in_specs=in_specs,
        out_specs=[o_spec, st_spec],
        out_shape=[jax.ShapeDtypeStruct((mix_rows, 512), BF16),
                   jax.ShapeDtypeStruct((n_batch, 4, n_keys, HEAD_W), F32)],
        scratch_shapes=[pltpu.VMEM((n_seq, 4, HEAD_W, HEAD_W), F32), pltpu.VMEM((rows, 512), F32)],
        input_output_aliases=_alias_if_full(out_buf, mix_rows, len(args) - 1),
        compiler_params=_cparams(2),
        name=mode + ("_sample" if sample else "_prompt"),
    )(*args)


def _round_bf16(x, on=True):
    return x.astype(BF16).astype(F32) if on else x


def _conf_body(n_seq, L, round_x, round_w, a_ref, gt_ref, hist_ref, w_ref, b_ref, g_ref, be_ref, _, o_ref, co_ref,
               buf, bufr, y_scr, win):
    tstep = pl.program_id(1)
    hist = CONF_WIDTH - 1
    pad = 32 - hist

    @pl.when(tstep == 0)
    def _():
        for s in range(n_seq):
            buf[s, pad:32, :] = hist_ref[s]
            bufr[s, pad:32, :] = _round_bf16(hist_ref[s], round_x)

    u = a_ref[...] * jax.nn.sigmoid(gt_ref[...])
    ur = _round_bf16(u, round_x)
    for s in range(n_seq):
        buf[s, 32:32 + L, :] = u[s * L:(s + 1) * L, :]
        bufr[s, 32:32 + L, :] = ur[s * L:(s + 1) * L, :]
    w = _round_bf16(w_ref[...], round_w)
    for s in range(n_seq):
        acc = jnp.zeros((L, CONF_DIM), F32)
        for phase in range(8):
            n_taps = (CONF_WIDTH - 1 - phase) // 8 + 1
            span = L + 8 * (n_taps - 1)
            win[phase, 0:span, :] = bufr[s, pad + phase:pad + phase + span, :]
            for a in range(n_taps):
                j = 8 * a + phase
                acc = acc + win[phase, 8 * a:8 * a + L, :] * w[j:j + 1, :]
        y_scr[s * L:(s + 1) * L, :] = _silu(_layernorm(acc + b_ref[...], g_ref[...], be_ref[...]))
        tail = buf[s, L + pad:L + 32, :]
        buf[s, pad:32, :] = tail
        tailr = bufr[s, L + pad:L + 32, :]
        bufr[s, pad:32, :] = tailr
    o_ref[...] = y_scr[...].astype(o_ref.dtype)

    @pl.when(tstep == pl.num_programs(1) - 1)
    def _():
        for s in range(n_seq):
            co_ref[s] = buf[s, pad:32, :]


def _conf_call(h, col_a, col_g, cache, w, b, g, be, out_buf, mix_rows, n_batch, seq_len, row_off, sample):
    n_seq, n_chunk, C, rows, grid, rb = _seq_layout(n_batch, seq_len, row_off, sample)
    L = rows // n_seq
    hist = CONF_WIDTH - 1
    colspec = lambda c0: pl.BlockSpec((rows, 512), lambda i, t: (rb(i, t), c0 // 512))
    fix2 = lambda i, t: (0, 0)
    c_spec = pl.BlockSpec((n_seq, hist, CONF_DIM), lambda i, t: (i, 0, 0))
    return pl.pallas_call(
        functools.partial(_conf_body, n_seq, L, True, sample),
        grid=grid,
        in_specs=[colspec(col_a), colspec(col_g), c_spec,
                  pl.BlockSpec((CONF_WIDTH, CONF_DIM), fix2), pl.BlockSpec((1, CONF_DIM), fix2),
                  pl.BlockSpec((1, CONF_DIM), fix2), pl.BlockSpec((1, CONF_DIM), fix2),
                  pl.BlockSpec(memory_space=pl.ANY)],
        out_specs=[pl.BlockSpec((rows, 512), lambda i, t: (rb(i, t), 0)), c_spec],
        out_shape=[jax.ShapeDtypeStruct((mix_rows, 512), BF16),
                   jax.ShapeDtypeStruct((n_batch, hist, CONF_DIM), F32)],
        scratch_shapes=[pltpu.VMEM((n_seq, 32 + L, CONF_DIM), F32)] * 2 + [pltpu.VMEM((rows, CONF_DIM), F32),
                                                                           pltpu.VMEM((8, L + 24, CONF_DIM), F32)],
        input_output_aliases=_alias_if_full(out_buf, mix_rows, 7),
        compiler_params=_cparams(2),
        name="conformer" + ("_sample" if sample else "_prompt"),
    )(h, h, cache, w, b, g, be, out_buf)


def _ssd_body(n_seq, n_chunk, C, round_x, round_w, hz_ref, hx_ref, hdt_ref, hist_ref, s0_ref, cw_ref, cb_ref, dtb_ref, alog_ref,
              dvec_ref, nw_ref, _, o_ref, co_ref, so_ref, st, buf, bufr, xbc, y_scr):
    tstep = pl.program_id(1)
    L = n_chunk * C
    hist = SSM_CONV - 1
    pad = 8 - hist
    n_pairs = SSM_HEADS // 2

    @pl.when(tstep == 0)
    def _():
        for s in range(n_seq):
            buf[s, pad:8, :] = hist_ref[s]
            bufr[s, pad:8, :] = _round_bf16(hist_ref[s], round_x)
            for m in range(n_pairs):
                st[s, m] = s0_ref[s, m]

    cw = _round_bf16(cw_ref[...], round_w)
    for s in range(n_seq):
        hx = hx_ref[s * L:(s + 1) * L, :]
        buf[s, 8:8 + L, :] = hx
        bufr[s, 8:8 + L, :] = _round_bf16(hx, round_x)
        acc = jnp.zeros((L, SSM_CONV_DIM), F32)
        for j in range(SSM_CONV):
            acc = acc + bufr[s, pad + j:pad + j + L, :] * cw[j:j + 1, :]
        xbc[s * L:(s + 1) * L, :] = _silu(acc + cb_ref[...])
        tail = buf[s, L + pad:L + 8, :]
        buf[s, pad:8, :] = tail
        tailr = bufr[s, L + pad:L + 8, :]
        bufr[s, pad:8, :] = tailr

    dt = _softplus(hdt_ref[...] + dtb_ref[...])
    la = dt * (-jnp.exp(alog_ref[...]))
    hrow = lax.broadcasted_iota(I32, (LANE, SSM_INNER), 0)
    hcol = lax.broadcasted_iota(I32, (LANE, SSM_INNER), 1) // SSM_HEADDIM
    expand = (hrow == hcol).astype(BF16)
    dtx = functools.reduce(lambda p, q: p + q,
                           [jnp.dot(part, expand, preferred_element_type=F32) for part in _split3(dt)])
    causal = _tri(C)
    tri = causal.astype(BF16)
    lane = lax.broadcasted_iota(I32, (C, HEAD_W), 1)
    bcol_all = _chunk_cumsum(la, C)
    heads_per_group = SSM_HEADS // SSM_GROUPS
    for s in range(n_seq):
        states = [st[s, m] for m in range(n_pairs)]
        for c in range(n_chunk):
            r0 = (s * n_chunk + c) * C
            rs = slice(r0, r0 + C)
            bcol = bcol_all[rs, :]
            brow = functools.reduce(lambda p, q: p + q, [
                lax.dot_general(part, tri, (((0,), (1,)), ((), ())), preferred_element_type=F32)
                for part in _split3(la[rs, :])])
            xs_c = xbc[rs, 0:SSM_INNER]
            v_c = (xs_c * dtx[rs, :]).astype(BF16)
            gmats, bms, cms = [], [], []
            for grp in range(SSM_GROUPS):
                bm = xbc[rs, SSM_INNER + grp * SSM_STATE:SSM_INNER + (grp + 1) * SSM_STATE]
                cm = xbc[rs, SSM_INNER + (SSM_GROUPS + grp) * SSM_STATE:SSM_INNER + (SSM_GROUPS + grp + 1) * SSM_STATE]
                cm_hi, cm_lo = _split2(cm)
                bm_hi, bm_lo = _split2(bm)
                gmats.append(_dot_nt(jnp.concatenate([cm_hi, cm_hi, cm_lo], axis=1),
                                     jnp.concatenate([bm_hi, bm_lo, bm_hi], axis=1)))
                bms.append(bm)
                cms.append(cm)
            for m in range(n_pairs):
                grp = (2 * m) // heads_per_group
                bm, cm, gmat = bms[grp], cms[grp], gmats[grp]
                ps = slice(m * HEAD_W, (m + 1) * HEAD_W)
                vp = v_c[:, ps]
                s_t = states[m]
                scores, queries, keys, decays = [], [], [], []
                for hh in range(2):
                    hd = 2 * m + hh
                    bc = bcol[:, hd:hd + 1]
                    br = brow[hd:hd + 1, :]
                    b_last = bcol[C - 1:C, hd:hd + 1]
                    scores.append(gmat * jnp.where(causal, jnp.exp(jnp.minimum(bc - br, 0.0)), 0.0))
                    queries.append(cm * jnp.exp(bc))
                    keys.append(bm * jnp.exp(b_last - bc))
                    decays.append(jnp.exp(b_last))
                o_stack = (_dot(jnp.concatenate(scores, axis=0), vp)
                           + _dot_nt(jnp.concatenate(queries, axis=0), s_t))
                kv = _dot_tn(vp, jnp.concatenate(keys, axis=1))
                d = SSM_HEADDIM
                states[m] = jnp.concatenate([s_t[0:d, :] * decays[0] + kv[0:d, 0:SSM_STATE],
                                             s_t[d:, :] * decays[1] + kv[d:, SSM_STATE:]], axis=0)
                o_pair = jnp.where(lane < SSM_HEADDIM, o_stack[0:C, :], o_stack[C:, :])
                y_scr[rs, ps] = o_pair + dvec_ref[:, ps] * xs_c[:, ps]
        for m in range(n_pairs):
            st[s, m] = states[m]
    y = y_scr[...] * _silu(hz_ref[...])
    gw = SSM_INNER // SSM_GROUPS
    for grp in range(SSM_GROUPS):
        gs = slice(grp * gw, (grp + 1) * gw)
        yg = y[:, gs]
        ms = jnp.mean(yg * yg, axis=-1, keepdims=True)
        o_ref[:, gs] = (yg * lax.rsqrt(ms + RMS_EPS) * nw_ref[:, gs]).astype(BF16)

    @pl.when(tstep == pl.num_programs(1) - 1)
    def _():
        for s in range(n_seq):
            co_ref[s] = buf[s, pad:8, :]
            for m in range(n_pairs):
                so_ref[s, m] = st[s, m]


def _ssd_call(h, col_z, col_x, col_dt, cache, s0, cw, cb, dtb, alog, dvec, nw, out_buf, mix_rows, n_batch, seq_len,
              row_off, sample):
    n_seq, n_chunk, C, rows, grid, rb = _seq_layout(n_batch, seq_len, row_off, sample, SSD_PROMPT_TILE)
    L = rows // n_seq
    hist = SSM_CONV - 1
    n_pairs = SSM_HEADS // 2
    colspec = lambda c0, w: pl.BlockSpec((rows, w), lambda i, t: (rb(i, t), c0 // w))
    fix2 = lambda i, t: (0, 0)
    c_spec = pl.BlockSpec((n_seq, hist, SSM_CONV_DIM), lambda i, t: (i, 0, 0))
    st_spec = pl.BlockSpec((n_seq, n_pairs, HEAD_W, SSM_STATE), lambda i, t: (i, 0, 0, 0))
    return pl.pallas_call(
        functools.partial(_ssd_body, n_seq, n_chunk, C, sample, True),
        grid=grid,
        in_specs=[colspec(col_z, 512), colspec(col_x, SSM_CONV_DIM), colspec(col_dt, LANE), c_spec, st_spec,
                  pl.BlockSpec((SSM_CONV, SSM_CONV_DIM), fix2), pl.BlockSpec((1, SSM_CONV_DIM), fix2),
                  pl.BlockSpec((1, LANE), fix2), pl.BlockSpec((1, LANE), fix2),
                  pl.BlockSpec((1, SSM_INNER), fix2), pl.BlockSpec((1, SSM_INNER), fix2),
                  pl.BlockSpec(memory_space=pl.ANY)],
        out_specs=[pl.BlockSpec((rows, 512), lambda i, t: (rb(i, t), 0)), c_spec, st_spec],
        out_shape=[jax.ShapeDtypeStruct((mix_rows, 512), BF16),
                   jax.ShapeDtypeStruct((n_batch, hist, SSM_CONV_DIM), F32),
                   jax.ShapeDtypeStruct((n_batch, n_pairs, HEAD_W, SSM_STATE), F32)],
        scratch_shapes=[pltpu.VMEM((n_seq, n_pairs, HEAD_W, SSM_STATE), F32),
                        pltpu.VMEM((n_seq, 8 + L, SSM_CONV_DIM), F32),
                        pltpu.VMEM((n_seq, 8 + L, SSM_CONV_DIM), F32),
                        pltpu.VMEM((rows, SSM_CONV_DIM), F32),
                        pltpu.VMEM((rows, SSM_INNER), F32)],
        input_output_aliases=_alias_if_full(out_buf, mix_rows, 11),
        compiler_params=_cparams(2),
        name="ssd" + ("_sample" if sample else "_prompt"),
    )(h, h, h, cache, s0, cw, cb, dtb, alog, dvec, nw, out_buf)


def _pad_heads(w, n_heads, width):
    lead = w.shape[:-1]
    w = w.reshape(lead + (n_heads, width))
    w = jnp.pad(w, [(0, 0)] * len(lead) + [(0, 0), (0, HEAD_W - width)])
    return w.reshape(lead + (n_heads * HEAD_W,))


def _row(v):
    return v.reshape(1, -1).astype(F32)


def kernel(x_prompt, x_sample, state_gla, cache_conformer, state_hgrn, state_ssm, cache_mamba_conv, w_in_even, w_gla_gate_lr, b_gla_gate, gla_norm_w, conf_conv_w, conf_conv_b, conf_ln_g, conf_ln_b, w_out_even, w_in_odd, hgrn_lower_bounds, hgrn_norm_w, mamba_conv_w, mamba_conv_b, mamba_dt_bias, mamba_a_log, mamba_d, mamba_norm_w, w_out_odd, ln1_g, ln1_b, ln2_g, ln2_b, router_w, router_b, expert_w_gate, expert_b_gate, expert_w_up, expert_b_up, expert_w_down, expert_b_down):
    bp, lp, _ = x_prompt.shape
    bs, ls, _ = x_sample.shape
    tp, ts = bp * lp, bs * ls
    x = (x_prompt.reshape(tp, D_MODEL), x_sample.reshape(ts, D_MODEL))

    def router_params(layer):
        return router_w[layer].T.astype(BF16), router_b[layer].astype(F32).reshape(N_EXPERTS, 1)

    def finish_layer(layer, x, mix_a, mix_b, w_out):
        rwt, rb = router_params(layer)
        x1, xp, gates, eidx, rank, counts = _outproj_ln_router(
            mix_a, mix_b, x[0], x[1], w_out[:512].astype(BF16), w_out[512:].astype(BF16),
            _row(ln1_g[layer]), _row(ln1_b[layer]), rwt, rb)
        return _moe(layer, x1, xp, gates, eidx, rank, counts, _row(ln2_g[layer]), _row(ln2_b[layer]),
                    expert_w_gate, expert_b_gate, expert_w_up, expert_b_up, expert_w_down, expert_b_down, tp)

    t_all = tp + ts
    fresh = jnp.zeros((t_all, 512), BF16)

    wi = w_in_even[0]
    wq, wk, wv, wg, wlr, wglu = jnp.split(wi, [256, 512, 1024, 1536, 1552], axis=1)
    w_even = jnp.concatenate([_pad_heads(wq, GLA_HEADS, GLA_DK), _pad_heads(wk, GLA_HEADS, GLA_DK), wv, wg, wglu,
                              jnp.pad(wlr, ((0, 0), (0, LANE - GLA_RANK)))], axis=1).astype(BF16)
    cols_gla = (0, 512, 1024, 1536, 3072)
    col_a, col_gate = 2048, 2560
    h = _inproj(x[0], x[1], w_even)
    wlr_p = jnp.pad(_pad_heads(w_gla_gate_lr[0], GLA_HEADS, GLA_DK), ((0, LANE - GLA_RANK), (0, 0)))
    blr_p = _row(_pad_heads(b_gla_gate[0], GLA_HEADS, GLA_DK))
    nw = _row(gla_norm_w[0])
    conf_args = (conf_conv_w[0], _row(conf_conv_b[0]), _row(conf_ln_g[0]), _row(conf_ln_b[0]))
    s0_p = jnp.zeros((bp, GLA_HEADS, GLA_DK, HEAD_W), F32)
    s0_s = state_gla[0]
    mix_a, sg_p = _gla_call("gla", h, cols_gla, (wlr_p, blr_p), nw, s0_p, fresh, t_all, bp, lp, 0, False)
    mix_a, sg_s = _gla_call("gla", h, cols_gla, (wlr_p, blr_p), nw, s0_s, mix_a, t_all, bs, ls, tp, True)
    mix_b, cc_p = _conf_call(h, col_a, col_gate, jnp.zeros((bp,) + cache_conformer.shape[2:], F32), *conf_args,
                             fresh, t_all, bp, lp, 0, False)
    mix_b, cc_s = _conf_call(h, col_a, col_gate, cache_conformer[0], *conf_args, mix_b, t_all, bs, ls, tp, True)
    x = finish_layer(0, x, mix_a, mix_b, w_out_even[0])
    gla_p, gla_s = sg_p[None], sg_s[None]
    conf_p, conf_s = cc_p[None], cc_s[None]

    lb_cum = jnp.cumsum(jax.nn.softmax(hgrn_lower_bounds.astype(F32), axis=0), axis=0)
    lower_bound = _row((lb_cum - lb_cum[0])[1])
    wo = w_in_odd[0]
    w_odd = jnp.concatenate([wo[:, 2560:3584], wo[:, :2560],
                             jnp.pad(wo[:, 3584:], ((0, 0), (0, LANE - SSM_HEADS)))], axis=1).astype(BF16)
    h = _inproj(x[0], x[1], w_odd)
    cols_hgrn = (1024, 1536, 2048, 2560)
    col_z, col_x, col_dt = 3072, 0, 3584
    nw = _row(hgrn_norm_w[0])
    mix_a, sh_p = _gla_call("hgrn", h, cols_hgrn, lower_bound, nw,
                            jnp.zeros((bp, HGRN_HEADS, HEAD_W, HEAD_W), F32), fresh, t_all, bp, lp, 0, False)
    mix_a, sh_s = _gla_call("hgrn", h, cols_hgrn, lower_bound, nw, state_hgrn[0], mix_a, t_all, bs, ls, tp, True)

    def pair_states(s):
        return jnp.swapaxes(s, 2, 3).reshape(s.shape[0], SSM_HEADS // 2, HEAD_W, SSM_STATE)

    def unpair_states(s):
        return jnp.swapaxes(s.reshape(s.shape[0], SSM_HEADS, SSM_HEADDIM, SSM_STATE), 2, 3)

    pad8 = lambda v: jnp.pad(v.astype(F32), (0, LANE - SSM_HEADS)).reshape(1, LANE)
    ssd_args = (mamba_conv_w[0], _row(mamba_conv_b[0]), pad8(mamba_dt_bias[0]), pad8(mamba_a_log[0]),
                _row(jnp.repeat(mamba_d[0], SSM_HEADDIM)), _row(mamba_norm_w[0]))
    mix_b, cm_p, ss_p = _ssd_call(h, col_z, col_x, col_dt, jnp.zeros((bp,) + cache_mamba_conv.shape[2:], F32),
                                  jnp.zeros((bp, SSM_HEADS // 2, HEAD_W, SSM_STATE), F32), *ssd_args,
                                  fresh, t_all, bp, lp, 0, False)
    mix_b, cm_s, ss_s = _ssd_call(h, col_z, col_x, col_dt, cache_mamba_conv[0], pair_states(state_ssm[0]),
                                  *ssd_args, mix_b, t_all, bs, ls, tp, True)
    y_prompt, y_sample = finish_layer(1, x, mix_a, mix_b, w_out_odd[0])
    y_prompt = y_prompt.reshape(bp, lp, D_MODEL)
    y_sample = y_sample.reshape(bs, ls, D_MODEL)
    return (y_prompt, y_sample, gla_p, gla_s, conf_p, conf_s, sh_p[None], sh_s[None],
            unpair_states(ss_p)[None], unpair_states(ss_s)[None], cm_p[None], cm_s[None])
```

```python
import functools

import jax
import jax.numpy as jnp
from jax import lax
from jax.experimental import pallas as pl
from jax.experimental.pallas import tpu as pltpu
from jax.experimental.pallas import tpu_sc as plsc

F32 = jnp.float32
BF16 = jnp.bfloat16
I32 = jnp.int32
U32 = jnp.uint32

D_MODEL = 1024
DEPTH = 2
DEEPNORM_ALPHA = (2.0 * DEPTH) ** 0.25
LN_EPS = 1e-5
RMS_EPS = 1e-6
LANE = 128
HEAD_W = 128
GLA_HEADS, GLA_DK, GLA_RANK, GLA_TAU = 4, 64, 16, 16.0
CONF_DIM, CONF_WIDTH = 512, 31
HGRN_HEADS, HGRN_DK = 4, 128
SSM_HEADS, SSM_HEADDIM, SSM_STATE, SSM_GROUPS, SSM_CONV = 8, 64, 128, 2, 4
SSM_INNER = SSM_HEADS * SSM_HEADDIM
SSM_CONV_DIM = SSM_INNER + 2 * SSM_GROUPS * SSM_STATE
N_EXPERTS, TOP_K = 32, 4
SWIGLU_ALPHA, SWIGLU_LIMIT = 1.702, 7.0
SCAN_CHUNK = 64
PROMPT_TILE = 512
SSD_PROMPT_TILE = 256
SAMPLE_SEQS = 16
TOKEN_TILE = 512
WIDE_TOKEN_TILE = 1024
MOE_ROWS = 512
IN_SLOTS = 4
SC_CORES, SC_SUBCORES = 2, 16
SC_ROWS = 64
SC_SCATTER_ROWS = 32
VMEM_LIMIT = 56 * 1024 * 1024


def _cparams(n_axes):
    return pltpu.CompilerParams(dimension_semantics=("arbitrary",) * n_axes, vmem_limit_bytes=VMEM_LIMIT)


def _silu(x):
    return x * jax.nn.sigmoid(x)


def _softplus(x):
    return jnp.maximum(x, 0.0) + jnp.log(1.0 + jnp.exp(-jnp.abs(x)))


def _log_sigmoid(x):
    return jnp.minimum(x, 0.0) - jnp.log(1.0 + jnp.exp(-jnp.abs(x)))


def _layernorm(y, g, b):
    mu = jnp.mean(y, axis=-1, keepdims=True)
    d = y - mu
    var = jnp.mean(d * d, axis=-1, keepdims=True)
    return d * lax.rsqrt(var + LN_EPS) * g + b


def _dot(a, b):
    return jnp.dot(a.astype(BF16), b.astype(BF16), preferred_element_type=F32)


def _dot_nt(a, b):
    return lax.dot_general(a.astype(BF16), b.astype(BF16), (((1,), (1,)), ((), ())), preferred_element_type=F32)


def _dot_tn(a, b):
    return lax.dot_general(a.astype(BF16), b.astype(BF16), (((0,), (0,)), ((), ())), preferred_element_type=F32)


def _tri(c):
    r = lax.broadcasted_iota(I32, (c, c), 0)
    k = lax.broadcasted_iota(I32, (c, c), 1)
    return r >= k


def _pair_specs(tm, n_first, width):
    return [pl.BlockSpec((tm, width), lambda i: (jnp.minimum(i, n_first - 1), 0)),
            pl.BlockSpec((tm, width), lambda i: (jnp.maximum(i - n_first, 0), 0))]


def _pair_tile(n_first, xa_ref, xb_ref):
    return jnp.where(pl.program_id(0) < n_first, xa_ref[...], xb_ref[...])


def _inproj_body(n_first, xa_ref, xb_ref, w_ref, o_ref):
    xb = _pair_tile(n_first, xa_ref, xb_ref).astype(BF16)
    n = w_ref.shape[1]
    for c0 in range(0, n, 512):
        c1 = min(c0 + 512, n)
        o_ref[:, c0:c1] = jnp.dot(xb, w_ref[:, c0:c1], preferred_element_type=F32)


def _inproj(xa, xb, w):
    k, n = w.shape
    t = xa.shape[0] + xb.shape[0]
    n_first = xa.shape[0] // TOKEN_TILE
    return pl.pallas_call(
        functools.partial(_inproj_body, n_first),
        grid=(t // TOKEN_TILE,),
        in_specs=_pair_specs(TOKEN_TILE, n_first, k) + [pl.BlockSpec((k, n), lambda i: (0, 0))],
        out_specs=pl.BlockSpec((TOKEN_TILE, n), lambda i: (i, 0)),
        out_shape=jax.ShapeDtypeStruct((t, n), F32),
        compiler_params=_cparams(1),
        name="inproj",
    )(xa, xb, w)


def _pack_halves(y):
    half = y.shape[1] // 2
    hi = lax.bitcast_convert_type(y[:, :half].astype(BF16).astype(F32), U32)
    lo = lax.bitcast_convert_type(y[:, half:].astype(BF16).astype(F32), U32)
    return (hi & jnp.uint32(0xFFFF0000)) | (lo >> 16)


def _unpack_halves(w):
    hi = lax.bitcast_convert_type(w & jnp.uint32(0xFFFF0000), F32)
    lo = lax.bitcast_convert_type(w << 16, F32)
    return hi, lo


def _outproj_body(n_first, a_ref, b_ref, xa_ref, xb_ref, wa_ref, wb_ref, g_ref, be_ref, rwt_ref, rb_ref,
                  x1_ref, xp_ref, gate_ref, idx_ref, rank_ref, cnt_ref, carry):
    @pl.when(pl.program_id(0) == 0)
    def _():
        carry[...] = jnp.zeros(carry.shape, F32)

    mix = (jnp.dot(a_ref[...], wa_ref[...], preferred_element_type=F32)
           + jnp.dot(b_ref[...], wb_ref[...], preferred_element_type=F32))
    x1 = _layernorm(DEEPNORM_ALPHA * _pair_tile(n_first, xa_ref, xb_ref) + mix, g_ref[...], be_ref[...])
    x1_ref[...] = x1
    xp_ref[...] = _pack_halves(x1)
    logits = _dot_nt(rwt_ref[...], x1) + rb_ref[...]
    tm = logits.shape[1]
    expert = lax.broadcasted_iota(I32, logits.shape, 0)
    vals, idxs = [], []
    for _ in range(TOP_K):
        m = jnp.max(logits, axis=0, keepdims=True)
        sel = jnp.min(jnp.where(logits == m, expert, N_EXPERTS), axis=0, keepdims=True)
        vals.append(m)
        idxs.append(sel)
        logits = jnp.where(expert == sel, -jnp.inf, logits)
    exps = [jnp.exp(v - vals[0]) for v in vals]
    inv = 1.0 / functools.reduce(lambda p, q: p + q, exps)
    chosen = jnp.zeros(logits.shape, F32)
    for k in range(TOP_K):
        chosen = chosen + (expert == idxs[k]).astype(F32)
    earlier = lax.broadcasted_iota(I32, (tm, tm), 0) < lax.broadcasted_iota(I32, (tm, tm), 1)
    before = carry[...] + jnp.dot(chosen.astype(BF16), earlier.astype(BF16), preferred_element_type=F32)
    choice = lax.broadcasted_iota(I32, (8, tm), 0)
    gates = jnp.zeros((8, tm), F32)
    eidx = jnp.zeros((8, tm), I32)
    ranks = jnp.zeros((8, tm), F32)
    for k in range(TOP_K):
        rk = jnp.sum(jnp.where(expert == idxs[k], before, 0.0), axis=0, keepdims=True)
        gates = jnp.where(choice == k, exps[k] * inv, gates)
        eidx = jnp.where(choice == k, idxs[k], eidx)
        ranks = jnp.where(choice == k, rk, ranks)
    gate_ref[...] = gates
    idx_ref[...] = eidx
    rank_ref[...] = ranks.astype(I32)
    carry[...] = carry[...] + jnp.sum(chosen, axis=1, keepdims=True)
    cnt_ref[...] = carry[...].astype(I32)


def _outproj_ln_router(a, b, xa, xb, wa, wb, g, be, rwt, rb):
    t = xa.shape[0] + xb.shape[0]
    tm = WIDE_TOKEN_TILE
    n_first = xa.shape[0] // tm
    row = lambda i: (i, 0)
    col = lambda i: (0, i)
    fix = lambda i: (0, 0)
    return pl.pallas_call(
        functools.partial(_outproj_body, n_first),
        grid=(t // tm,),
        in_specs=[pl.BlockSpec((tm, 512), row), pl.BlockSpec((tm, 512), row)] + _pair_specs(tm, n_first, D_MODEL)
        + [pl.BlockSpec((512, D_MODEL), fix), pl.BlockSpec((512, D_MODEL), fix),
           pl.BlockSpec((1, D_MODEL), fix), pl.BlockSpec((1, D_MODEL), fix),
           pl.BlockSpec((N_EXPERTS, D_MODEL), fix), pl.BlockSpec((N_EXPERTS, 1), fix)],
        out_specs=[pl.BlockSpec((tm, D_MODEL), row), pl.BlockSpec((tm, 512), row),
                   pl.BlockSpec((8, tm), col), pl.BlockSpec((8, tm), col), pl.BlockSpec((8, tm), col),
                   pl.BlockSpec((N_EXPERTS, 1), fix)],
        out_shape=[jax.ShapeDtypeStruct((t, D_MODEL), F32), jax.ShapeDtypeStruct((t, 512), U32),
                   jax.ShapeDtypeStruct((8, t), F32), jax.ShapeDtypeStruct((8, t), I32),
                   jax.ShapeDtypeStruct((8, t), I32), jax.ShapeDtypeStruct((N_EXPERTS, 1), I32)],
        scratch_shapes=[pltpu.VMEM((N_EXPERTS, 1), F32)],
        compiler_params=_cparams(1),
        name="outproj_ln_router",
    )(a, b, xa, xb, wa, wb, g, be, rwt, rb)


def _sc_mesh():
    return plsc.VectorSubcoreMesh(core_axis_name="c", subcore_axis_name="s")


def _sc_scatter_rows(src, dest, n_out):
    n_src, w = src.shape
    n_dst = dest.shape[1]
    workers = SC_CORES * SC_SUBCORES
    per_worker = n_src // workers
    chunks = per_worker // SC_SCATTER_ROWS
    assert n_src == workers * chunks * SC_SCATTER_ROWS
    idx = dest.T.reshape(n_dst, workers, chunks, SC_SCATTER_ROWS)

    @functools.partial(pl.kernel, mesh=_sc_mesh(), out_type=jax.ShapeDtypeStruct((n_out, w), src.dtype),
                       scratch_types=[pltpu.VMEM((n_dst, chunks, SC_SCATTER_ROWS), I32)]
                       + [pltpu.VMEM((SC_SCATTER_ROWS, w), src.dtype)] * 2 + [pltpu.SemaphoreType.DMA] * 4)
    def scatter(src_hbm, idx_hbm, out_hbm, idx_v, rows_a, rows_b, sem_ra, sem_rb, sem_wa, sem_wb):
        worker = lax.axis_index("s") * SC_CORES + lax.axis_index("c")
        base = worker * per_worker
        for k in range(n_dst):
            pltpu.sync_copy(idx_hbm.at[k, worker], idx_v.at[k])

        def read(c, rows, sem):
            return pltpu.async_copy(src_hbm.at[pl.ds(pl.multiple_of(base + c * SC_SCATTER_ROWS, 8), SC_SCATTER_ROWS)],
                                    rows, sem)

        def write_all(pending_read, c, rows, sem):
            pending_read.wait()
            return [pltpu.async_copy(rows, out_hbm.at[idx_v.at[k, c]], sem) for k in range(n_dst)]

        @pl.loop(0, chunks // 2)
        def _(p):
            read_a = read(2 * p, rows_a, sem_ra)
            read_b = read(2 * p + 1, rows_b, sem_rb)
            writes = write_all(read_a, 2 * p, rows_a, sem_wa) + write_all(read_b, 2 * p + 1, rows_b, sem_wb)
            for wr in writes:
                wr.wait()

        if chunks % 2:
            for wr in write_all(read(chunks - 1, rows_a, sem_ra), chunks - 1, rows_a, sem_wa):
                wr.wait()

    return scatter(src, idx)


def _sc_gather_rows(table, idx):
    n, w = idx.shape[0], table.shape[1]
    workers = SC_CORES * SC_SUBCORES
    chunks = n // (workers * SC_ROWS)
    assert n == workers * chunks * SC_ROWS and chunks % 2 == 0
    idx = idx.reshape(workers, chunks, SC_ROWS)

    @functools.partial(pl.kernel, mesh=_sc_mesh(), out_type=jax.ShapeDtypeStruct((n, w), table.dtype),
                       scratch_types=[pltpu.VMEM((chunks, SC_ROWS), I32)] + [pltpu.VMEM((SC_ROWS, w), table.dtype)] * 2
                       + [pltpu.SemaphoreType.DMA] * 4)
    def gather(table_hbm, idx_hbm, out_hbm, idx_v, rows_a, rows_b, sem_ra, sem_rb, sem_wa, sem_wb):
        worker = lax.axis_index("s") * SC_CORES + lax.axis_index("c")
        base = worker * (chunks * SC_ROWS)
        pltpu.sync_copy(idx_hbm.at[worker], idx_v)

        def out_rows(c):
            return out_hbm.at[pl.ds(pl.multiple_of(base + c * SC_ROWS, 8), SC_ROWS)]

        @pl.loop(0, chunks // 2)
        def _(p):
            read_a = pltpu.async_copy(table_hbm.at[idx_v.at[2 * p]], rows_a, sem_ra)
            read_b = pltpu.async_copy(table_hbm.at[idx_v.at[2 * p + 1]], rows_b, sem_rb)
            read_a.wait()
            write_a = pltpu.async_copy(rows_a, out_rows(2 * p), sem_wa)
            read_b.wait()
            write_b = pltpu.async_copy(rows_b, out_rows(2 * p + 1), sem_wb)
            write_a.wait()
            write_b.wait()

    return gather(table, idx)


def _experts_body(b0_ref, nb_ref, last_ref, nt_ref, xs_hbm, wg_ref, bg_ref, wu_ref, bu_ref, wd_ref, bd_ref, o_hbm,
                  wg_s, wu_s, wd_s, xbuf, obuf, sem_in, sem_out):
    e = pl.program_id(0)
    first_blk, n_blk, last_valid, n_total = b0_ref[e], nb_ref[e], last_ref[e], nt_ref[0]
    quarter = MOE_ROWS // 4

    def rows_of(g):
        return pl.ds(pl.multiple_of(g * MOE_ROWS, MOE_ROWS), MOE_ROWS)

    def fetch(g, slot):
        return pltpu.make_async_copy(xs_hbm.at[rows_of(g)], xbuf.at[slot], sem_in.at[slot])

    def put(g, slot):
        return pltpu.make_async_copy(obuf.at[slot], o_hbm.at[rows_of(g)], sem_out.at[slot])

    lead = IN_SLOTS - 1
    for first in range(lead):
        @pl.when((e == 0) & (n_total > first))
        def _():
            fetch(first, first).start()

    @pl.when(n_blk > 0)
    def _():
        wg_s[...] = wg_ref[...].astype(BF16)
        wu_s[...] = wu_ref[...].astype(BF16)
        wd_s[...] = wd_ref[...].astype(BF16)

    def compute(islot, slot, rows):
        half = D_MODEL // 2
        x_hi, x_lo = _unpack_halves(xbuf[islot, 0:rows, :])
        x_hi = x_hi.astype(BF16)
        x_lo = x_lo.astype(BF16)
        g = (jnp.dot(x_hi, wg_s[:half, :], preferred_element_type=F32)
             + jnp.dot(x_lo, wg_s[half:, :], preferred_element_type=F32) + bg_ref[...])
        u = (jnp.dot(x_hi, wu_s[:half, :], preferred_element_type=F32)
             + jnp.dot(x_lo, wu_s[half:, :], preferred_element_type=F32) + bu_ref[...])
        g = jnp.minimum(g, SWIGLU_LIMIT)
        u = jnp.clip(u, -SWIGLU_LIMIT, SWIGLU_LIMIT)
        hmid = (u + 1.0) * (g * jax.nn.sigmoid(SWIGLU_ALPHA * g))
        out = jnp.dot(hmid.astype(BF16), wd_s[...], preferred_element_type=F32) + bd_ref[...]
        obuf[slot, 0:rows, :] = _pack_halves(out)

    def block(j, carry):
        g = first_blk + j
        slot = lax.rem(g, 2)
        islot = lax.rem(g, IN_SLOTS)
        fetch(g, islot).wait()

        @pl.when(g + lead < n_total)
        def _():
            fetch(g + lead, lax.rem(g + lead, IN_SLOTS)).start()

        @pl.when(g >= 2)
        def _():
            put(g - 2, slot).wait()

        valid = jnp.where(j == n_blk - 1, last_valid, MOE_ROWS)

        for rows in range(quarter, MOE_ROWS + 1, quarter):
            @pl.when((valid > rows - quarter) & (valid <= rows))
            def _():
                compute(islot, slot, rows)
                if rows < MOE_ROWS:
                    obuf[slot, rows:, :] = jnp.zeros((MOE_ROWS - rows, obuf.shape[2]), obuf.dtype)

        put(g, slot).start()
        return carry

    lax.fori_loop(0, n_blk, block, 0)

    @pl.when((e == N_EXPERTS - 1) & (n_total >= 2))
    def _():
        put(n_total - 2, lax.rem(n_total, 2)).wait()

    @pl.when((e == N_EXPERTS - 1) & (n_total >= 1))
    def _():
        put(n_total - 1, lax.rem(n_total - 1, 2)).wait()


def _experts(layer, first_blk, n_blk, last_valid, xs, wg, bg, wu, bu, wd, bd):
    n_rows, w = xs.shape
    wsel = lambda e, b0, nb, lv, nt: (layer, e, 0, 0)
    wspec = pl.BlockSpec((None, None, D_MODEL, D_MODEL), wsel)
    bspec = pl.BlockSpec((None, None, 1, D_MODEL), wsel)
    bias = lambda b: b.reshape(b.shape[0], b.shape[1], 1, b.shape[2])
    return pl.pallas_call(
        _experts_body,
        grid_spec=pltpu.PrefetchScalarGridSpec(
            num_scalar_prefetch=4,
            grid=(N_EXPERTS,),
            in_specs=[pl.BlockSpec(memory_space=pl.ANY), wspec, bspec, wspec, bspec, wspec, bspec],
            out_specs=pl.BlockSpec(memory_space=pl.ANY),
            scratch_shapes=[pltpu.VMEM((D_MODEL, D_MODEL), BF16)] * 3
            + [pltpu.VMEM((IN_SLOTS, MOE_ROWS, w), U32), pltpu.VMEM((2, MOE_ROWS, w), U32),
               pltpu.SemaphoreType.DMA((IN_SLOTS,)), pltpu.SemaphoreType.DMA((2,))],
        ),
        out_shape=jax.ShapeDtypeStruct((n_rows, w), U32),
        compiler_params=_cparams(1),
        name="experts",
    )(first_blk, n_blk, last_valid, jnp.sum(n_blk).reshape(1), xs, wg, bias(bg), wu, bias(bu), wd, bias(bd))


def _combine_body(n_first, o0_ref, o1_ref, o2_ref, o3_ref, gt_ref, x_ref, g_ref, b_ref, ya_ref, yb_ref=None):
    half = D_MODEL // 2
    gates = gt_ref[...]
    hi = jnp.zeros((x_ref.shape[0], half), F32)
    lo = jnp.zeros((x_ref.shape[0], half), F32)
    for k, o_ref in enumerate((o0_ref, o1_ref, o2_ref, o3_ref)):
        h, l = _unpack_halves(o_ref[...])
        gk = gates[:, k:k + 1]
        hi = hi + gk * h
        lo = lo + gk * l
    x = x_ref[...]
    y_hi = DEEPNORM_ALPHA * x[:, :half] + hi
    y_lo = DEEPNORM_ALPHA * x[:, half:] + lo
    mu = (jnp.sum(y_hi, axis=-1, keepdims=True) + jnp.sum(y_lo, axis=-1, keepdims=True)) * (1.0 / D_MODEL)
    d_hi = y_hi - mu
    d_lo = y_lo - mu
    var = (jnp.sum(d_hi * d_hi, axis=-1, keepdims=True) + jnp.sum(d_lo * d_lo, axis=-1, keepdims=True)) * (1.0 / D_MODEL)
    r = lax.rsqrt(var + LN_EPS)
    out_hi = d_hi * r * g_ref[:, :half] + b_ref[:, :half]
    out_lo = d_lo * r * g_ref[:, half:] + b_ref[:, half:]

    def write(y_ref):
        y_ref[:, :half] = out_hi
        y_ref[:, half:] = out_lo

    if yb_ref is None:
        write(ya_ref)
    else:
        pl.when(pl.program_id(0) < n_first)(lambda: write(ya_ref))
        pl.when(pl.program_id(0) >= n_first)(lambda: write(yb_ref))


def _combine_ln(o4, gates, x, g, b, t_first=None):
    t = x.shape[0]
    tm = WIDE_TOKEN_TILE
    row = lambda i: (i, 0)
    fix = lambda i: (0, 0)
    choice = lambda k: pl.BlockSpec((tm, 512), lambda i: (k * (t // tm) + i, 0))
    if t_first is None:
        n_first = None
        out_specs = pl.BlockSpec((tm, D_MODEL), row)
        out_shape = jax.ShapeDtypeStruct((t, D_MODEL), F32)
    else:
        n_first = t_first // tm
        out_specs = [pl.BlockSpec((tm, D_MODEL), lambda i: (jnp.minimum(i, n_first - 1), 0)),
                     pl.BlockSpec((tm, D_MODEL), lambda i: (jnp.maximum(i - n_first, 0), 0))]
        out_shape = [jax.ShapeDtypeStruct((t_first, D_MODEL), F32), jax.ShapeDtypeStruct((t - t_first, D_MODEL), F32)]
    return pl.pallas_call(
        functools.partial(_combine_body, n_first),
        grid=(t // tm,),
        in_specs=[choice(0), choice(1), choice(2), choice(3), pl.BlockSpec((tm, TOP_K), row),
                  pl.BlockSpec((tm, D_MODEL), row), pl.BlockSpec((1, D_MODEL), fix), pl.BlockSpec((1, D_MODEL), fix)],
        out_specs=out_specs,
        out_shape=out_shape,
        compiler_params=_cparams(1),
        name="combine_ln",
    )(o4, o4, o4, o4, gates, x, g, b)


def _moe(layer, x1, xp, gates, eidx, rank, counts, ln_g, ln_b, wg, bg, wu, bu, wd, bd, t_first=None):
    t = x1.shape[0]
    bm = MOE_ROWS
    n_blocks = t * TOP_K // bm + N_EXPERTS
    n_rows = n_blocks * bm
    cnt = counts[:, 0]
    padded = (cnt + bm - 1) // bm * bm
    pad_end = jnp.cumsum(padded)
    pad_start = pad_end - padded
    e = eidx[:TOP_K]
    start = jnp.sum(jnp.where(e[:, :, None] == jnp.arange(N_EXPERTS, dtype=I32), pad_start, 0), axis=-1)
    dest = (start + rank[:TOP_K]).T
    n_blk = padded // bm
    last_valid = cnt - (n_blk - 1) * bm
    xs = _sc_scatter_rows(xp, dest, n_rows)
    outs = _experts(layer, pad_start // bm, n_blk, last_valid, xs, wg, bg, wu, bu, wd, bd)
    o4 = _sc_gather_rows(outs, dest.T.reshape(-1))
    return _combine_ln(o4, gates[:TOP_K].T, x1, ln_g, ln_b, t_first)


def _split2(x):
    hi = x.astype(BF16)
    return hi, (x - hi.astype(F32)).astype(BF16)


def _split3(x):
    hi = x.astype(BF16)
    rem = x - hi.astype(F32)
    mid = rem.astype(BF16)
    return hi, mid, (rem - mid.astype(F32)).astype(BF16)


def _chunk_cumsum(g, C):
    rows = g.shape[0]
    r = lax.broadcasted_iota(I32, (rows, rows), 0)
    c = lax.broadcasted_iota(I32, (rows, rows), 1)
    tri = ((r >= c) & (r // C == c // C)).astype(BF16)
    hi, mid, lo = _split3(g)
    dot = lambda part: jnp.dot(tri, part, preferred_element_type=F32)
    return dot(hi) + dot(mid) + dot(lo)


def _gla_batched_step(q, k, v, g, C, n_seq, n_heads, state_of, o_scr):
    rows = n_seq * C
    wide = n_seq * HEAD_W
    mid = max(C // 2 - 1, 0)
    r = lax.broadcasted_iota(I32, (rows, rows), 0)
    c = lax.broadcasted_iota(I32, (rows, rows), 1)
    same = (r // C) == (c // C)
    causal = same & (r >= c)
    parts = _split3(g)
    summed = lambda mask: functools.reduce(lambda p, q_: p + q_, [
        jnp.dot(mask.astype(BF16), part, preferred_element_type=F32) for part in parts])
    b = summed(causal)
    b_mid = summed(same & ((c % C) <= mid))
    b_last = summed(same)
    qe_hi, qe_lo = _split2(q * jnp.exp(b - b_mid))
    ke_hi, ke_lo = _split2(k * jnp.exp(b_mid - b))
    q_state = (q * jnp.exp(b)).astype(BF16)
    k_state = (k * jnp.exp(b_last - b)).astype(BF16)
    decay_parts = _split3(jnp.exp(b_last))
    row_w = lax.broadcasted_iota(I32, (rows, wide), 0)
    blk_w = lax.broadcasted_iota(I32, (rows, wide), 1) // HEAD_W
    own = (row_w // C) == blk_w
    pick = (row_w == blk_w * C).astype(BF16)
    new_states = []
    for h in range(n_heads):
        cs = slice(h * HEAD_W, (h + 1) * HEAD_W)
        lhs = jnp.concatenate([qe_hi[:, cs], qe_hi[:, cs], qe_lo[:, cs]], axis=1)
        rhs = jnp.concatenate([ke_hi[:, cs], ke_lo[:, cs], ke_hi[:, cs]], axis=1)
        scores = jnp.where(causal, _dot_nt(lhs, rhs), 0.0)
        vh = v[:, cs].astype(BF16)
        s_cat = jnp.concatenate([state_of(s, h) for s in range(n_seq)], axis=1)
        o_full = _dot(q_state[:, cs], s_cat)
        o_state = jnp.concatenate([o_full[s * C:(s + 1) * C, s * HEAD_W:(s + 1) * HEAD_W] for s in range(n_seq)],
                                  axis=0)
        o = _dot(scores, vh) + o_state
        v_wide = jnp.where(own, jnp.concatenate([vh] * n_seq, axis=1), jnp.zeros((), BF16))
        kv = _dot_tn(k_state[:, cs], v_wide)
        decay = functools.reduce(lambda p, q_: p + q_, [
            lax.dot_general(part[:, cs], pick, (((0,), (0,)), ((), ())), preferred_element_type=F32)
            for part in decay_parts])
        new_states.append(s_cat * decay + kv)
        ms = jnp.mean(o * o, axis=-1, keepdims=True)
        o_scr[:, cs] = o * lax.rsqrt(ms + RMS_EPS)
    return new_states


def _gla_body(mode, n_seq, n_chunk, C, *refs):
    if mode == "gla":
        hq_ref, hk_ref, hv_ref, hg_ref, hlr_ref, wlr_ref, blr_ref, nw_ref, s0_ref, _, o_ref, so_ref, st, o_scr = refs
    else:
        hq_ref, hk_ref, hv_ref, hg_ref, lb_ref, nw_ref, s0_ref, _, o_ref, so_ref, st, o_scr = refs
    n_heads = 4
    n_keys = s0_ref.shape[2]
    tstep = pl.program_id(1)
    batched = n_chunk == 1 and n_seq > 1

    def padded_state(s, h):
        s_in = s0_ref[s, h]
        if n_keys < HEAD_W:
            s_in = jnp.concatenate([s_in, jnp.zeros((HEAD_W - n_keys, HEAD_W), F32)], axis=0)
        return s_in

    if not batched:
        @pl.when(tstep == 0)
        def _():
            for s in range(n_seq):
                for h in range(n_heads):
                    st[s, h] = padded_state(s, h).T

    if mode == "gla":
        q = hq_ref[...] * (GLA_DK ** -0.5)
        k = hk_ref[...]
        z = _dot(hlr_ref[...], wlr_ref[...]) + blr_ref[...]
        g = _log_sigmoid(z) * (1.0 / GLA_TAU)
    else:
        q = _silu(hq_ref[...]) * (HGRN_DK ** -0.5)
        lb = lb_ref[...]
        f = lb + (1.0 - lb) * jax.nn.sigmoid(hk_ref[...])
        k = 1.0 - f
        g = jnp.log(f)
    v = hv_ref[...]
    if batched:
        new_states = _gla_batched_step(q, k, v, g, C, n_seq, n_heads, padded_state, o_scr)
        for s in range(n_seq):
            for h in range(n_heads):
                so_ref[s, h] = new_states[h][0:n_keys, s * HEAD_W:(s + 1) * HEAD_W]
    else:
        causal = _tri(C)
        mid = max(C // 2 - 1, 0)
        b_all = _chunk_cumsum(g, C)
        for s in range(n_seq):
            states = [st[s, h] for h in range(n_heads)]
            for c in range(n_chunk):
                r0 = (s * n_chunk + c) * C
                rs = slice(r0, r0 + C)
                b, qc, kc = b_all[rs, :], q[rs, :], k[rs, :]
                b_last = b[C - 1:C, :]
                b_mid = b[mid:mid + 1, :]
                qe_hi, qe_lo = _split2(qc * jnp.exp(b - b_mid))
                ke_hi, ke_lo = _split2(kc * jnp.exp(b_mid - b))
                q_state = (qc * jnp.exp(b)).astype(BF16)
                k_state = (kc * jnp.exp(b_last - b)).astype(BF16)
                decay = jnp.exp(b_last)
                for h in range(n_heads):
                    cs = slice(h * HEAD_W, (h + 1) * HEAD_W)
                    lhs = jnp.concatenate([qe_hi[:, cs], qe_hi[:, cs], qe_lo[:, cs]], axis=1)
                    rhs = jnp.concatenate([ke_hi[:, cs], ke_lo[:, cs], ke_hi[:, cs]], axis=1)
                    scores = jnp.where(causal, _dot_nt(lhs, rhs), 0.0)
                    vh = v[rs, cs].astype(BF16)
                    o = _dot(scores, vh) + _dot_nt(q_state[:, cs], states[h])
                    states[h] = states[h] * decay[:, cs] + _dot_tn(vh, k_state[:, cs])
                    ms = jnp.mean(o * o, axis=-1, keepdims=True)
                    o_scr[rs, cs] = o * lax.rsqrt(ms + RMS_EPS)
            for h in range(n_heads):
                st[s, h] = states[h]
    o_ref[...] = (o_scr[...] * nw_ref[...] * _silu(hg_ref[...])).astype(BF16)

    if not batched:
        @pl.when(tstep == pl.num_programs(1) - 1)
        def _():
            for s in range(n_seq):
                for h in range(n_heads):
                    so_ref[s, h] = st[s, h].T[0:n_keys, :]


def _seq_layout(n_batch, seq_len, row_off, sample, prompt_tile=PROMPT_TILE):
    if sample:
        n_seq, n_chunk, C = SAMPLE_SEQS, 1, seq_len
        rows = n_seq * C
        grid = (n_batch // n_seq, 1)
        blk0 = row_off // rows
        rb = lambda i, t: blk0 + i
    else:
        n_seq, n_chunk, C = 1, prompt_tile // SCAN_CHUNK, SCAN_CHUNK
        rows = prompt_tile
        tiles = seq_len // rows
        grid = (n_batch, tiles)
        blk0 = row_off // rows
        rb = lambda i, t: blk0 + i * tiles + t
    return n_seq, n_chunk, C, rows, grid, rb


def _gla_call(mode, h, cols, extra, nw, s0, out_buf, n_batch, seq_len, row_off, sample):
    n_seq, n_chunk, C, rows, grid, rb = _seq_layout(n_batch, seq_len, row_off, sample)
    colspec = lambda c0, w: pl.BlockSpec((rows, w), lambda i, t: (rb(i, t), c0 // w))
    fix2 = lambda i, t: (0, 0)
    in_specs = [colspec(cols[0], 512), colspec(cols[1], 512), colspec(cols[2], 512), colspec(cols[3], 512)]
    args = [h, h, h, h]
    if mode == "gla":
        wlr, blr = extra
        in_specs += [colspec(cols[4], LANE), pl.BlockSpec((LANE, 512), fix2), pl.BlockSpec((1, 512), fix2)]
        args += [h, wlr, blr]
    else:
        in_specs += [pl.BlockSpec((1, 512), fix2)]
        args += [extra]
    n_keys = s0.shape[2]
    st_spec = pl.BlockSpec((n_seq, 4, n_keys, HEAD_W), lambda i, t: (i, 0, 0, 0))
    in_specs += [pl.BlockSpec((1, 512), fix2), st_spec, pl.BlockSpec(memory_space=pl.ANY)]
    args += [nw, s0, out_buf]
    o_spec = pl.BlockSpec((rows, 512), lambda i, t: (rb(i, t), 0))
    return pl.pallas_call(
        functools.partial(_gla_body, mode, n_seq, n_chunk, C),
        grid=grid,
        in_specs=in_specs,
        out_specs=[o_spec, st_spec],
        out_shape=[jax.ShapeDtypeStruct(out_buf.shape, out_buf.dtype),
                   jax.ShapeDtypeStruct((n_batch, 4, n_keys, HEAD_W), F32)],
        scratch_shapes=[pltpu.VMEM((n_seq, 4, HEAD_W, HEAD_W), F32), pltpu.VMEM((rows, 512), F32)],
        input_output_aliases={len(args) - 1: 0},
        compiler_params=_cparams(2),
        name=mode + ("_sample" if sample else "_prompt"),
    )(*args)


def _round_bf16(x, on=True):
    return x.astype(BF16).astype(F32) if on else x


def _conf_body(n_seq, L, round_x, round_w, a_ref, gt_ref, hist_ref, w_ref, b_ref, g_ref, be_ref, _, o_ref, co_ref,
               buf, bufr, y_scr, win):
    tstep = pl.program_id(1)
    hist = CONF_WIDTH - 1
    pad = 32 - hist

    @pl.when(tstep == 0)
    def _():
        for s in range(n_seq):
            buf[s, pad:32, :] = hist_ref[s]
            bufr[s, pad:32, :] = _round_bf16(hist_ref[s], round_x)

    u = a_ref[...] * jax.nn.sigmoid(gt_ref[...])
    ur = _round_bf16(u, round_x)
    for s in range(n_seq):
        buf[s, 32:32 + L, :] = u[s * L:(s + 1) * L, :]
        bufr[s, 32:32 + L, :] = ur[s * L:(s + 1) * L, :]
    w = _round_bf16(w_ref[...], round_w)
    for s in range(n_seq):
        acc = jnp.zeros((L, CONF_DIM), F32)
        for phase in range(8):
            n_taps = (CONF_WIDTH - 1 - phase) // 8 + 1
            span = L + 8 * (n_taps - 1)
            win[phase, 0:span, :] = bufr[s, pad + phase:pad + phase + span, :]
            for a in range(n_taps):
                j = 8 * a + phase
                acc = acc + win[phase, 8 * a:8 * a + L, :] * w[j:j + 1, :]
        y_scr[s * L:(s + 1) * L, :] = _silu(_layernorm(acc + b_ref[...], g_ref[...], be_ref[...]))
        tail = buf[s, L + pad:L + 32, :]
        buf[s, pad:32, :] = tail
        tailr = bufr[s, L + pad:L + 32, :]
        bufr[s, pad:32, :] = tailr
    o_ref[...] = y_scr[...].astype(o_ref.dtype)

    @pl.when(tstep == pl.num_programs(1) - 1)
    def _():
        for s in range(n_seq):
            co_ref[s] = buf[s, pad:32, :]


def _conf_call(h, col_a, col_g, cache, w, b, g, be, out_buf, n_batch, seq_len, row_off, sample):
    n_seq, n_chunk, C, rows, grid, rb = _seq_layout(n_batch, seq_len, row_off, sample)
    L = rows // n_seq
    hist = CONF_WIDTH - 1
    colspec = lambda c0: pl.BlockSpec((rows, 512), lambda i, t: (rb(i, t), c0 // 512))
    fix2 = lambda i, t: (0, 0)
    c_spec = pl.BlockSpec((n_seq, hist, CONF_DIM), lambda i, t: (i, 0, 0))
    return pl.pallas_call(
        functools.partial(_conf_body, n_seq, L, True, sample),
        grid=grid,
        in_specs=[colspec(col_a), colspec(col_g), c_spec,
                  pl.BlockSpec((CONF_WIDTH, CONF_DIM), fix2), pl.BlockSpec((1, CONF_DIM), fix2),
                  pl.BlockSpec((1, CONF_DIM), fix2), pl.BlockSpec((1, CONF_DIM), fix2),
                  pl.BlockSpec(memory_space=pl.ANY)],
        out_specs=[pl.BlockSpec((rows, 512), lambda i, t: (rb(i, t), 0)), c_spec],
        out_shape=[jax.ShapeDtypeStruct(out_buf.shape, out_buf.dtype),
                   jax.ShapeDtypeStruct((n_batch, hist, CONF_DIM), F32)],
        scratch_shapes=[pltpu.VMEM((n_seq, 32 + L, CONF_DIM), F32)] * 2 + [pltpu.VMEM((rows, CONF_DIM), F32),
                                                                           pltpu.VMEM((8, L + 24, CONF_DIM), F32)],
        input_output_aliases={7: 0},
        compiler_params=_cparams(2),
        name="conformer" + ("_sample" if sample else "_prompt"),
    )(h, h, cache, w, b, g, be, out_buf)


def _ssd_body(n_seq, n_chunk, C, round_x, round_w, hz_ref, hx_ref, hdt_ref, hist_ref, s0_ref, cw_ref, cb_ref, dtb_ref, alog_ref,
              dvec_ref, nw_ref, _, o_ref, co_ref, so_ref, st, buf, bufr, xbc, y_scr):
    tstep = pl.program_id(1)
    L = n_chunk * C
    hist = SSM_CONV - 1
    pad = 8 - hist
    n_pairs = SSM_HEADS // 2

    @pl.when(tstep == 0)
    def _():
        for s in range(n_seq):
            buf[s, pad:8, :] = hist_ref[s]
            bufr[s, pad:8, :] = _round_bf16(hist_ref[s], round_x)
            for m in range(n_pairs):
                st[s, m] = s0_ref[s, m]

    cw = _round_bf16(cw_ref[...], round_w)
    for s in range(n_seq):
        hx = hx_ref[s * L:(s + 1) * L, :]
        buf[s, 8:8 + L, :] = hx
        bufr[s, 8:8 + L, :] = _round_bf16(hx, round_x)
        acc = jnp.zeros((L, SSM_CONV_DIM), F32)
        for j in range(SSM_CONV):
            acc = acc + bufr[s, pad + j:pad + j + L, :] * cw[j:j + 1, :]
        xbc[s * L:(s + 1) * L, :] = _silu(acc + cb_ref[...])
        tail = buf[s, L + pad:L + 8, :]
        buf[s, pad:8, :] = tail
        tailr = bufr[s, L + pad:L + 8, :]
        bufr[s, pad:8, :] = tailr

    dt = _softplus(hdt_ref[...] + dtb_ref[...])
    la = dt * (-jnp.exp(alog_ref[...]))
    hrow = lax.broadcasted_iota(I32, (LANE, SSM_INNER), 0)
    hcol = lax.broadcasted_iota(I32, (LANE, SSM_INNER), 1) // SSM_HEADDIM
    expand = (hrow == hcol).astype(BF16)
    dtx = functools.reduce(lambda p, q: p + q,
                           [jnp.dot(part, expand, preferred_element_type=F32) for part in _split3(dt)])
    causal = _tri(C)
    tri = causal.astype(BF16)
    lane = lax.broadcasted_iota(I32, (C, HEAD_W), 1)
    bcol_all = _chunk_cumsum(la, C)
    heads_per_group = SSM_HEADS // SSM_GROUPS
    for s in range(n_seq):
        states = [st[s, m] for m in range(n_pairs)]
        for c in range(n_chunk):
            r0 = (s * n_chunk + c) * C
            rs = slice(r0, r0 + C)
            bcol = bcol_all[rs, :]
            brow = functools.reduce(lambda p, q: p + q, [
                lax.dot_general(part, tri, (((0,), (1,)), ((), ())), preferred_element_type=F32)
                for part in _split3(la[rs, :])])
            xs_c = xbc[rs, 0:SSM_INNER]
            v_c = (xs_c * dtx[rs, :]).astype(BF16)
            gmats, bms, cms = [], [], []
            for grp in range(SSM_GROUPS):
                bm = xbc[rs, SSM_INNER + grp * SSM_STATE:SSM_INNER + (grp + 1) * SSM_STATE]
                cm = xbc[rs, SSM_INNER + (SSM_GROUPS + grp) * SSM_STATE:SSM_INNER + (SSM_GROUPS + grp + 1) * SSM_STATE]
                cm_hi, cm_lo = _split2(cm)
                bm_hi, bm_lo = _split2(bm)
                gmats.append(_dot_nt(jnp.concatenate([cm_hi, cm_hi, cm_lo], axis=1),
                                     jnp.concatenate([bm_hi, bm_lo, bm_hi], axis=1)))
                bms.append(bm)
                cms.append(cm)
            for m in range(n_pairs):
                grp = (2 * m) // heads_per_group
                bm, cm, gmat = bms[grp], cms[grp], gmats[grp]
                ps = slice(m * HEAD_W, (m + 1) * HEAD_W)
                vp = v_c[:, ps]
                s_t = states[m]
                scores, queries, keys, decays = [], [], [], []
                for hh in range(2):
                    hd = 2 * m + hh
                    bc = bcol[:, hd:hd + 1]
                    br = brow[hd:hd + 1, :]
                    b_last = bcol[C - 1:C, hd:hd + 1]
                    scores.append(gmat * jnp.where(causal, jnp.exp(jnp.minimum(bc - br, 0.0)), 0.0))
                    queries.append(cm * jnp.exp(bc))
                    keys.append(bm * jnp.exp(b_last - bc))
                    decays.append(jnp.exp(b_last))
                o_stack = (_dot(jnp.concatenate(scores, axis=0), vp)
                           + _dot_nt(jnp.concatenate(queries, axis=0), s_t))
                kv = _dot_tn(vp, jnp.concatenate(keys, axis=1))
                d = SSM_HEADDIM
                states[m] = jnp.concatenate([s_t[0:d, :] * decays[0] + kv[0:d, 0:SSM_STATE],
                                             s_t[d:, :] * decays[1] + kv[d:, SSM_STATE:]], axis=0)
                o_pair = jnp.where(lane < SSM_HEADDIM, o_stack[0:C, :], o_stack[C:, :])
                y_scr[rs, ps] = o_pair + dvec_ref[:, ps] * xs_c[:, ps]
        for m in range(n_pairs):
            st[s, m] = states[m]
    y = y_scr[...] * _silu(hz_ref[...])
    gw = SSM_INNER // SSM_GROUPS
    for grp in range(SSM_GROUPS):
        gs = slice(grp * gw, (grp + 1) * gw)
        yg = y[:, gs]
        ms = jnp.mean(yg * yg, axis=-1, keepdims=True)
        o_ref[:, gs] = (yg * lax.rsqrt(ms + RMS_EPS) * nw_ref[:, gs]).astype(BF16)

    @pl.when(tstep == pl.num_programs(1) - 1)
    def _():
        for s in range(n_seq):
            co_ref[s] = buf[s, pad:8, :]
            for m in range(n_pairs):
                so_ref[s, m] = st[s, m]


def _ssd_call(h, col_z, col_x, col_dt, cache, s0, cw, cb, dtb, alog, dvec, nw, out_buf, n_batch, seq_len, row_off,
              sample):
    n_seq, n_chunk, C, rows, grid, rb = _seq_layout(n_batch, seq_len, row_off, sample, SSD_PROMPT_TILE)
    L = rows // n_seq
    hist = SSM_CONV - 1
    n_pairs = SSM_HEADS // 2
    colspec = lambda c0, w: pl.BlockSpec((rows, w), lambda i, t: (rb(i, t), c0 // w))
    fix2 = lambda i, t: (0, 0)
    c_spec = pl.BlockSpec((n_seq, hist, SSM_CONV_DIM), lambda i, t: (i, 0, 0))
    st_spec = pl.BlockSpec((n_seq, n_pairs, HEAD_W, SSM_STATE), lambda i, t: (i, 0, 0, 0))
    return pl.pallas_call(
        functools.partial(_ssd_body, n_seq, n_chunk, C, sample, True),
        grid=grid,
        in_specs=[colspec(col_z, 512), colspec(col_x, SSM_CONV_DIM), colspec(col_dt, LANE), c_spec, st_spec,
                  pl.BlockSpec((SSM_CONV, SSM_CONV_DIM), fix2), pl.BlockSpec((1, SSM_CONV_DIM), fix2),
                  pl.BlockSpec((1, LANE), fix2), pl.BlockSpec((1, LANE), fix2),
                  pl.BlockSpec((1, SSM_INNER), fix2), pl.BlockSpec((1, SSM_INNER), fix2),
                  pl.BlockSpec(memory_space=pl.ANY)],
        out_specs=[pl.BlockSpec((rows, 512), lambda i, t: (rb(i, t), 0)), c_spec, st_spec],
        out_shape=[jax.ShapeDtypeStruct(out_buf.shape, out_buf.dtype),
                   jax.ShapeDtypeStruct((n_batch, hist, SSM_CONV_DIM), F32),
                   jax.ShapeDtypeStruct((n_batch, n_pairs, HEAD_W, SSM_STATE), F32)],
        scratch_shapes=[pltpu.VMEM((n_seq, n_pairs, HEAD_W, SSM_STATE), F32),
                        pltpu.VMEM((n_seq, 8 + L, SSM_CONV_DIM), F32),
                        pltpu.VMEM((n_seq, 8 + L, SSM_CONV_DIM), F32),
                        pltpu.VMEM((rows, SSM_CONV_DIM), F32),
                        pltpu.VMEM((rows, SSM_INNER), F32)],
        input_output_aliases={11: 0},
        compiler_params=_cparams(2),
        name="ssd" + ("_sample" if sample else "_prompt"),
    )(h, h, h, cache, s0, cw, cb, dtb, alog, dvec, nw, out_buf)


def _pad_heads(w, n_heads, width):
    lead = w.shape[:-1]
    w = w.reshape(lead + (n_heads, width))
    w = jnp.pad(w, [(0, 0)] * len(lead) + [(0, 0), (0, HEAD_W - width)])
    return w.reshape(lead + (n_heads * HEAD_W,))


def _row(v):
    return v.reshape(1, -1).astype(F32)


def kernel(x_prompt, x_sample, state_gla, cache_conformer, state_hgrn, state_ssm, cache_mamba_conv, w_in_even, w_gla_gate_lr, b_gla_gate, gla_norm_w, conf_conv_w, conf_conv_b, conf_ln_g, conf_ln_b, w_out_even, w_in_odd, hgrn_lower_bounds, hgrn_norm_w, mamba_conv_w, mamba_conv_b, mamba_dt_bias, mamba_a_log, mamba_d, mamba_norm_w, w_out_odd, ln1_g, ln1_b, ln2_g, ln2_b, router_w, router_b, expert_w_gate, expert_b_gate, expert_w_up, expert_b_up, expert_w_down, expert_b_down):
    bp, lp, _ = x_prompt.shape
    bs, ls, _ = x_sample.shape
    tp, ts = bp * lp, bs * ls
    x = (x_prompt.reshape(tp, D_MODEL), x_sample.reshape(ts, D_MODEL))

    def router_params(layer):
        return router_w[layer].T.astype(BF16), router_b[layer].astype(F32).reshape(N_EXPERTS, 1)

    def finish_layer(layer, x, mix_a, mix_b, w_out):
        rwt, rb = router_params(layer)
        x1, xp, gates, eidx, rank, counts = _outproj_ln_router(
            mix_a, mix_b, x[0], x[1], w_out[:512].astype(BF16), w_out[512:].astype(BF16),
            _row(ln1_g[layer]), _row(ln1_b[layer]), rwt, rb)
        return _moe(layer, x1, xp, gates, eidx, rank, counts, _row(ln2_g[layer]), _row(ln2_b[layer]),
                    expert_w_gate, expert_b_gate, expert_w_up, expert_b_up, expert_w_down, expert_b_down, tp)

    mix_init = jnp.zeros((tp + ts, 512), BF16)

    wi = w_in_even[0]
    wq, wk, wv, wg, wlr, wglu = jnp.split(wi, [256, 512, 1024, 1536, 1552], axis=1)
    w_even = jnp.concatenate([_pad_heads(wq, GLA_HEADS, GLA_DK), _pad_heads(wk, GLA_HEADS, GLA_DK), wv, wg, wglu,
                              jnp.pad(wlr, ((0, 0), (0, LANE - GLA_RANK)))], axis=1).astype(BF16)
    cols_gla = (0, 512, 1024, 1536, 3072)
    col_a, col_gate = 2048, 2560
    h = _inproj(x[0], x[1], w_even)
    wlr_p = jnp.pad(_pad_heads(w_gla_gate_lr[0], GLA_HEADS, GLA_DK), ((0, LANE - GLA_RANK), (0, 0)))
    blr_p = _row(_pad_heads(b_gla_gate[0], GLA_HEADS, GLA_DK))
    nw = _row(gla_norm_w[0])
    conf_args = (conf_conv_w[0], _row(conf_conv_b[0]), _row(conf_ln_g[0]), _row(conf_ln_b[0]))
    s0_p = jnp.zeros((bp, GLA_HEADS, GLA_DK, HEAD_W), F32)
    s0_s = state_gla[0]
    mix_a, sg_p = _gla_call("gla", h, cols_gla, (wlr_p, blr_p), nw, s0_p, mix_init, bp, lp, 0, False)
    mix_a, sg_s = _gla_call("gla", h, cols_gla, (wlr_p, blr_p), nw, s0_s, mix_a, bs, ls, tp, True)
    mix_b, cc_p = _conf_call(h, col_a, col_gate, jnp.zeros((bp,) + cache_conformer.shape[2:], F32), *conf_args,
                             mix_init, bp, lp, 0, False)
    mix_b, cc_s = _conf_call(h, col_a, col_gate, cache_conformer[0], *conf_args, mix_b, bs, ls, tp, True)
    x = finish_layer(0, x, mix_a, mix_b, w_out_even[0])
    gla_p, gla_s = sg_p[None], sg_s[None]
    conf_p, conf_s = cc_p[None], cc_s[None]

    lb_cum = jnp.cumsum(jax.nn.softmax(hgrn_lower_bounds.astype(F32), axis=0), axis=0)
    lower_bound = _row((lb_cum - lb_cum[0])[1])
    wo = w_in_odd[0]
    w_odd = jnp.concatenate([wo[:, 2560:3584], wo[:, :2560],
                             jnp.pad(wo[:, 3584:], ((0, 0), (0, LANE - SSM_HEADS)))], axis=1).astype(BF16)
    h = _inproj(x[0], x[1], w_odd)
    cols_hgrn = (1024, 1536, 2048, 2560)
    col_z, col_x, col_dt = 3072, 0, 3584
    nw = _row(hgrn_norm_w[0])
    mix_a, sh_p = _gla_call("hgrn", h, cols_hgrn, lower_bound, nw,
                            jnp.zeros((bp, HGRN_HEADS, HEAD_W, HEAD_W), F32), mix_init, bp, lp, 0, False)
    mix_a, sh_s = _gla_call("hgrn", h, cols_hgrn, lower_bound, nw, state_hgrn[0], mix_a, bs, ls, tp, True)

    def pair_states(s):
        return jnp.swapaxes(s, 2, 3).reshape(s.shape[0], SSM_HEADS // 2, HEAD_W, SSM_STATE)

    def unpair_states(s):
        return jnp.swapaxes(s.reshape(s.shape[0], SSM_HEADS, SSM_HEADDIM, SSM_STATE), 2, 3)

    pad8 = lambda v: jnp.pad(v.astype(F32), (0, LANE - SSM_HEADS)).reshape(1, LANE)
    ssd_args = (mamba_conv_w[0], _row(mamba_conv_b[0]), pad8(mamba_dt_bias[0]), pad8(mamba_a_log[0]),
                _row(jnp.repeat(mamba_d[0], SSM_HEADDIM)), _row(mamba_norm_w[0]))
    mix_b, cm_p, ss_p = _ssd_call(h, col_z, col_x, col_dt, jnp.zeros((bp,) + cache_mamba_conv.shape[2:], F32),
                                  jnp.zeros((bp, SSM_HEADS // 2, HEAD_W, SSM_STATE), F32), *ssd_args,
                                  mix_init, bp, lp, 0, False)
    mix_b, cm_s, ss_s = _ssd_call(h, col_z, col_x, col_dt, cache_mamba_conv[0], pair_states(state_ssm[0]),
                                  *ssd_args, mix_b, bs, ls, tp, True)
    y_prompt, y_sample = finish_layer(1, x, mix_a, mix_b, w_out_odd[0])
    y_prompt = y_prompt.reshape(bp, lp, D_MODEL)
    y_sample = y_sample.reshape(bs, ls, D_MODEL)
    return (y_prompt, y_sample, gla_p, gla_s, conf_p, conf_s, sh_p[None], sh_s[None],
            unpair_states(ss_p)[None], unpair_states(ss_s)[None], cm_p[None], cm_s[None])
```

```python
import functools

import jax
import jax.numpy as jnp
from jax import lax
from jax.experimental import pallas as pl
from jax.experimental.pallas import tpu as pltpu
from jax.experimental.pallas import tpu_sc as plsc

F32 = jnp.float32
BF16 = jnp.bfloat16
I32 = jnp.int32
U32 = jnp.uint32

D_MODEL = 1024
DEPTH = 2
DEEPNORM_ALPHA = (2.0 * DEPTH) ** 0.25
LN_EPS = 1e-5
RMS_EPS = 1e-6
LANE = 128
HEAD_W = 128
GLA_HEADS, GLA_DK, GLA_RANK, GLA_TAU = 4, 64, 16, 16.0
CONF_DIM, CONF_WIDTH = 512, 31
HGRN_HEADS, HGRN_DK = 4, 128
SSM_HEADS, SSM_HEADDIM, SSM_STATE, SSM_GROUPS, SSM_CONV = 8, 64, 128, 2, 4
SSM_INNER = SSM_HEADS * SSM_HEADDIM
SSM_CONV_DIM = SSM_INNER + 2 * SSM_GROUPS * SSM_STATE
N_EXPERTS, TOP_K = 32, 4
SWIGLU_ALPHA, SWIGLU_LIMIT = 1.702, 7.0
SCAN_CHUNK = 64
PROMPT_TILE = 512
SSD_PROMPT_TILE = 256
SAMPLE_SEQS = 16
TOKEN_TILE = 512
WIDE_TOKEN_TILE = 1024
MOE_ROWS = 512
IN_SLOTS = 4
SC_CORES, SC_SUBCORES = 2, 16
SC_ROWS = 64
SC_SCATTER_ROWS = 32
VMEM_LIMIT = 56 * 1024 * 1024


def _cparams(n_axes):
    return pltpu.CompilerParams(dimension_semantics=("arbitrary",) * n_axes, vmem_limit_bytes=VMEM_LIMIT)


def _silu(x):
    return x * jax.nn.sigmoid(x)


def _softplus(x):
    return jnp.maximum(x, 0.0) + jnp.log(1.0 + jnp.exp(-jnp.abs(x)))


def _log_sigmoid(x):
    return jnp.minimum(x, 0.0) - jnp.log(1.0 + jnp.exp(-jnp.abs(x)))


def _layernorm(y, g, b):
    mu = jnp.mean(y, axis=-1, keepdims=True)
    d = y - mu
    var = jnp.mean(d * d, axis=-1, keepdims=True)
    return d * lax.rsqrt(var + LN_EPS) * g + b


def _dot(a, b):
    return jnp.dot(a.astype(BF16), b.astype(BF16), preferred_element_type=F32)


def _dot_nt(a, b):
    return lax.dot_general(a.astype(BF16), b.astype(BF16), (((1,), (1,)), ((), ())), preferred_element_type=F32)


def _dot_tn(a, b):
    return lax.dot_general(a.astype(BF16), b.astype(BF16), (((0,), (0,)), ((), ())), preferred_element_type=F32)


def _tri(c):
    r = lax.broadcasted_iota(I32, (c, c), 0)
    k = lax.broadcasted_iota(I32, (c, c), 1)
    return r >= k


def _pair_specs(tm, n_first, width):
    return [pl.BlockSpec((tm, width), lambda i: (jnp.minimum(i, n_first - 1), 0)),
            pl.BlockSpec((tm, width), lambda i: (jnp.maximum(i - n_first, 0), 0))]


def _pair_tile(n_first, xa_ref, xb_ref):
    return jnp.where(pl.program_id(0) < n_first, xa_ref[...], xb_ref[...])


def _inproj_body(n_first, xa_ref, xb_ref, w_ref, o_ref):
    xb = _pair_tile(n_first, xa_ref, xb_ref).astype(BF16)
    n = w_ref.shape[1]
    for c0 in range(0, n, 512):
        c1 = min(c0 + 512, n)
        o_ref[:, c0:c1] = jnp.dot(xb, w_ref[:, c0:c1], preferred_element_type=F32)


def _inproj(xa, xb, w):
    k, n = w.shape
    t = xa.shape[0] + xb.shape[0]
    n_first = xa.shape[0] // TOKEN_TILE
    return pl.pallas_call(
        functools.partial(_inproj_body, n_first),
        grid=(t // TOKEN_TILE,),
        in_specs=_pair_specs(TOKEN_TILE, n_first, k) + [pl.BlockSpec((k, n), lambda i: (0, 0))],
        out_specs=pl.BlockSpec((TOKEN_TILE, n), lambda i: (i, 0)),
        out_shape=jax.ShapeDtypeStruct((t, n), F32),
        compiler_params=_cparams(1),
        name="inproj",
    )(xa, xb, w)


def _pack_halves(y):
    half = y.shape[1] // 2
    hi = lax.bitcast_convert_type(y[:, :half].astype(BF16).astype(F32), U32)
    lo = lax.bitcast_convert_type(y[:, half:].astype(BF16).astype(F32), U32)
    return (hi & jnp.uint32(0xFFFF0000)) | (lo >> 16)


def _unpack_halves(w):
    hi = lax.bitcast_convert_type(w & jnp.uint32(0xFFFF0000), F32)
    lo = lax.bitcast_convert_type(w << 16, F32)
    return hi, lo


def _outproj_body(n_first, a_ref, b_ref, xa_ref, xb_ref, wa_ref, wb_ref, g_ref, be_ref, rwt_ref, rb_ref,
                  x1_ref, xp_ref, gate_ref, idx_ref, rank_ref, cnt_ref, carry):
    @pl.when(pl.program_id(0) == 0)
    def _():
        carry[...] = jnp.zeros(carry.shape, F32)

    mix = (jnp.dot(a_ref[...], wa_ref[...], preferred_element_type=F32)
           + jnp.dot(b_ref[...], wb_ref[...], preferred_element_type=F32))
    x1 = _layernorm(DEEPNORM_ALPHA * _pair_tile(n_first, xa_ref, xb_ref) + mix, g_ref[...], be_ref[...])
    x1_ref[...] = x1
    xp_ref[...] = _pack_halves(x1)
    logits = _dot_nt(rwt_ref[...], x1) + rb_ref[...]
    tm = logits.shape[1]
    expert = lax.broadcasted_iota(I32, logits.shape, 0)
    vals, idxs = [], []
    for _ in range(TOP_K):
        m = jnp.max(logits, axis=0, keepdims=True)
        sel = jnp.min(jnp.where(logits == m, expert, N_EXPERTS), axis=0, keepdims=True)
        vals.append(m)
        idxs.append(sel)
        logits = jnp.where(expert == sel, -jnp.inf, logits)
    exps = [jnp.exp(v - vals[0]) for v in vals]
    inv = 1.0 / functools.reduce(lambda p, q: p + q, exps)
    chosen = jnp.zeros(logits.shape, F32)
    for k in range(TOP_K):
        chosen = chosen + (expert == idxs[k]).astype(F32)
    earlier = lax.broadcasted_iota(I32, (tm, tm), 0) < lax.broadcasted_iota(I32, (tm, tm), 1)
    before = carry[...] + jnp.dot(chosen.astype(BF16), earlier.astype(BF16), preferred_element_type=F32)
    choice = lax.broadcasted_iota(I32, (8, tm), 0)
    gates = jnp.zeros((8, tm), F32)
    eidx = jnp.zeros((8, tm), I32)
    ranks = jnp.zeros((8, tm), F32)
    for k in range(TOP_K):
        rk = jnp.sum(jnp.where(expert == idxs[k], before, 0.0), axis=0, keepdims=True)
        gates = jnp.where(choice == k, exps[k] * inv, gates)
        eidx = jnp.where(choice == k, idxs[k], eidx)
        ranks = jnp.where(choice == k, rk, ranks)
    gate_ref[...] = gates
    idx_ref[...] = eidx
    rank_ref[...] = ranks.astype(I32)
    carry[...] = carry[...] + jnp.sum(chosen, axis=1, keepdims=True)
    cnt_ref[...] = carry[...].astype(I32)


def _outproj_ln_router(a, b, xa, xb, wa, wb, g, be, rwt, rb):
    t = xa.shape[0] + xb.shape[0]
    tm = WIDE_TOKEN_TILE
    n_first = xa.shape[0] // tm
    row = lambda i: (i, 0)
    col = lambda i: (0, i)
    fix = lambda i: (0, 0)
    return pl.pallas_call(
        functools.partial(_outproj_body, n_first),
        grid=(t // tm,),
        in_specs=[pl.BlockSpec((tm, 512), row), pl.BlockSpec((tm, 512), row)] + _pair_specs(tm, n_first, D_MODEL)
        + [pl.BlockSpec((512, D_MODEL), fix), pl.BlockSpec((512, D_MODEL), fix),
           pl.BlockSpec((1, D_MODEL), fix), pl.BlockSpec((1, D_MODEL), fix),
           pl.BlockSpec((N_EXPERTS, D_MODEL), fix), pl.BlockSpec((N_EXPERTS, 1), fix)],
        out_specs=[pl.BlockSpec((tm, D_MODEL), row), pl.BlockSpec((tm, 512), row),
                   pl.BlockSpec((8, tm), col), pl.BlockSpec((8, tm), col), pl.BlockSpec((8, tm), col),
                   pl.BlockSpec((N_EXPERTS, 1), fix)],
        out_shape=[jax.ShapeDtypeStruct((t, D_MODEL), F32), jax.ShapeDtypeStruct((t, 512), U32),
                   jax.ShapeDtypeStruct((8, t), F32), jax.ShapeDtypeStruct((8, t), I32),
                   jax.ShapeDtypeStruct((8, t), I32), jax.ShapeDtypeStruct((N_EXPERTS, 1), I32)],
        scratch_shapes=[pltpu.VMEM((N_EXPERTS, 1), F32)],
        compiler_params=_cparams(1),
        name="outproj_ln_router",
    )(a, b, xa, xb, wa, wb, g, be, rwt, rb)


def _sc_mesh():
    return plsc.VectorSubcoreMesh(core_axis_name="c", subcore_axis_name="s")


def _sc_scatter_rows(src, dest, n_out):
    n_src, w = src.shape
    n_dst = dest.shape[1]
    workers = SC_CORES * SC_SUBCORES
    per_worker = n_src // workers
    chunks = per_worker // SC_SCATTER_ROWS
    assert n_src == workers * chunks * SC_SCATTER_ROWS
    idx = dest.T.reshape(n_dst, workers, chunks, SC_SCATTER_ROWS)

    @functools.partial(pl.kernel, mesh=_sc_mesh(), out_type=jax.ShapeDtypeStruct((n_out, w), src.dtype),
                       scratch_types=[pltpu.VMEM((n_dst, chunks, SC_SCATTER_ROWS), I32)]
                       + [pltpu.VMEM((SC_SCATTER_ROWS, w), src.dtype)] * 2 + [pltpu.SemaphoreType.DMA] * 4)
    def scatter(src_hbm, idx_hbm, out_hbm, idx_v, rows_a, rows_b, sem_ra, sem_rb, sem_wa, sem_wb):
        worker = lax.axis_index("s") * SC_CORES + lax.axis_index("c")
        base = worker * per_worker
        for k in range(n_dst):
            pltpu.sync_copy(idx_hbm.at[k, worker], idx_v.at[k])

        def read(c, rows, sem):
            return pltpu.async_copy(src_hbm.at[pl.ds(pl.multiple_of(base + c * SC_SCATTER_ROWS, 8), SC_SCATTER_ROWS)],
                                    rows, sem)

        def write_all(pending_read, c, rows, sem):
            pending_read.wait()
            return [pltpu.async_copy(rows, out_hbm.at[idx_v.at[k, c]], sem) for k in range(n_dst)]

        @pl.loop(0, chunks // 2)
        def _(p):
            read_a = read(2 * p, rows_a, sem_ra)
            read_b = read(2 * p + 1, rows_b, sem_rb)
            writes = write_all(read_a, 2 * p, rows_a, sem_wa) + write_all(read_b, 2 * p + 1, rows_b, sem_wb)
            for wr in writes:
                wr.wait()

        if chunks % 2:
            for wr in write_all(read(chunks - 1, rows_a, sem_ra), chunks - 1, rows_a, sem_wa):
                wr.wait()

    return scatter(src, idx)


def _sc_gather_rows(table, idx):
    n, w = idx.shape[0], table.shape[1]
    workers = SC_CORES * SC_SUBCORES
    chunks = n // (workers * SC_ROWS)
    assert n == workers * chunks * SC_ROWS and chunks % 2 == 0
    idx = idx.reshape(workers, chunks, SC_ROWS)

    @functools.partial(pl.kernel, mesh=_sc_mesh(), out_type=jax.ShapeDtypeStruct((n, w), table.dtype),
                       scratch_types=[pltpu.VMEM((chunks, SC_ROWS), I32)] + [pltpu.VMEM((SC_ROWS, w), table.dtype)] * 2
                       + [pltpu.SemaphoreType.DMA] * 4)
    def gather(table_hbm, idx_hbm, out_hbm, idx_v, rows_a, rows_b, sem_ra, sem_rb, sem_wa, sem_wb):
        worker = lax.axis_index("s") * SC_CORES + lax.axis_index("c")
        base = worker * (chunks * SC_ROWS)
        pltpu.sync_copy(idx_hbm.at[worker], idx_v)

        def out_rows(c):
            return out_hbm.at[pl.ds(pl.multiple_of(base + c * SC_ROWS, 8), SC_ROWS)]

        @pl.loop(0, chunks // 2)
        def _(p):
            read_a = pltpu.async_copy(table_hbm.at[idx_v.at[2 * p]], rows_a, sem_ra)
            read_b = pltpu.async_copy(table_hbm.at[idx_v.at[2 * p + 1]], rows_b, sem_rb)
            read_a.wait()
            write_a = pltpu.async_copy(rows_a, out_rows(2 * p), sem_wa)
            read_b.wait()
            write_b = pltpu.async_copy(rows_b, out_rows(2 * p + 1), sem_wb)
            write_a.wait()
            write_b.wait()

    return gather(table, idx)


def _experts_body(b0_ref, nb_ref, last_ref, nt_ref, xs_hbm, wg_ref, bg_ref, wu_ref, bu_ref, wd_ref, bd_ref, o_hbm,
                  wg_s, wu_s, wd_s, xbuf, obuf, sem_in, sem_out):
    e = pl.program_id(0)
    first_blk, n_blk, last_valid, n_total = b0_ref[e], nb_ref[e], last_ref[e], nt_ref[0]
    quarter = MOE_ROWS // 4

    def rows_of(g):
        return pl.ds(pl.multiple_of(g * MOE_ROWS, MOE_ROWS), MOE_ROWS)

    def fetch(g, slot):
        return pltpu.make_async_copy(xs_hbm.at[rows_of(g)], xbuf.at[slot], sem_in.at[slot])

    def put(g, slot):
        return pltpu.make_async_copy(obuf.at[slot], o_hbm.at[rows_of(g)], sem_out.at[slot])

    lead = IN_SLOTS - 1
    for first in range(lead):
        @pl.when((e == 0) & (n_total > first))
        def _():
            fetch(first, first).start()

    @pl.when(n_blk > 0)
    def _():
        wg_s[...] = wg_ref[...].astype(BF16)
        wu_s[...] = wu_ref[...].astype(BF16)
        wd_s[...] = wd_ref[...].astype(BF16)

    def compute(islot, slot, rows):
        half = D_MODEL // 2
        x_hi, x_lo = _unpack_halves(xbuf[islot, 0:rows, :])
        x_hi = x_hi.astype(BF16)
        x_lo = x_lo.astype(BF16)
        g = (jnp.dot(x_hi, wg_s[:half, :], preferred_element_type=F32)
             + jnp.dot(x_lo, wg_s[half:, :], preferred_element_type=F32) + bg_ref[...])
        u = (jnp.dot(x_hi, wu_s[:half, :], preferred_element_type=F32)
             + jnp.dot(x_lo, wu_s[half:, :], preferred_element_type=F32) + bu_ref[...])
        g = jnp.minimum(g, SWIGLU_LIMIT)
        u = jnp.clip(u, -SWIGLU_LIMIT, SWIGLU_LIMIT)
        hmid = (u + 1.0) * (g * jax.nn.sigmoid(SWIGLU_ALPHA * g))
        out = jnp.dot(hmid.astype(BF16), wd_s[...], preferred_element_type=F32) + bd_ref[...]
        obuf[slot, 0:rows, :] = _pack_halves(out)

    def block(j, carry):
        g = first_blk + j
        slot = lax.rem(g, 2)
        islot = lax.rem(g, IN_SLOTS)
        fetch(g, islot).wait()

        @pl.when(g + lead < n_total)
        def _():
            fetch(g + lead, lax.rem(g + lead, IN_SLOTS)).start()

        @pl.when(g >= 2)
        def _():
            put(g - 2, slot).wait()

        valid = jnp.where(j == n_blk - 1, last_valid, MOE_ROWS)

        for rows in range(quarter, MOE_ROWS + 1, quarter):
            @pl.when((valid > rows - quarter) & (valid <= rows))
            def _():
                compute(islot, slot, rows)
                if rows < MOE_ROWS:
                    obuf[slot, rows:, :] = jnp.zeros((MOE_ROWS - rows, obuf.shape[2]), obuf.dtype)

        put(g, slot).start()
        return carry

    lax.fori_loop(0, n_blk, block, 0)

    @pl.when((e == N_EXPERTS - 1) & (n_total >= 2))
    def _():
        put(n_total - 2, lax.rem(n_total, 2)).wait()

    @pl.when((e == N_EXPERTS - 1) & (n_total >= 1))
    def _():
        put(n_total - 1, lax.rem(n_total - 1, 2)).wait()


def _experts(layer, first_blk, n_blk, last_valid, xs, wg, bg, wu, bu, wd, bd):
    n_rows, w = xs.shape
    wsel = lambda e, b0, nb, lv, nt: (layer, e, 0, 0)
    wspec = pl.BlockSpec((None, None, D_MODEL, D_MODEL), wsel)
    bspec = pl.BlockSpec((None, None, 1, D_MODEL), wsel)
    bias = lambda b: b.reshape(b.shape[0], b.shape[1], 1, b.shape[2])
    return pl.pallas_call(
        _experts_body,
        grid_spec=pltpu.PrefetchScalarGridSpec(
            num_scalar_prefetch=4,
            grid=(N_EXPERTS,),
            in_specs=[pl.BlockSpec(memory_space=pl.ANY), wspec, bspec, wspec, bspec, wspec, bspec],
            out_specs=pl.BlockSpec(memory_space=pl.ANY),
            scratch_shapes=[pltpu.VMEM((D_MODEL, D_MODEL), BF16)] * 3
            + [pltpu.VMEM((IN_SLOTS, MOE_ROWS, w), U32), pltpu.VMEM((2, MOE_ROWS, w), U32),
               pltpu.SemaphoreType.DMA((IN_SLOTS,)), pltpu.SemaphoreType.DMA((2,))],
        ),
        out_shape=jax.ShapeDtypeStruct((n_rows, w), U32),
        compiler_params=_cparams(1),
        name="experts",
    )(first_blk, n_blk, last_valid, jnp.sum(n_blk).reshape(1), xs, wg, bias(bg), wu, bias(bu), wd, bias(bd))


def _combine_body(n_first, o0_ref, o1_ref, o2_ref, o3_ref, gt_ref, x_ref, g_ref, b_ref, ya_ref, yb_ref=None):
    half = D_MODEL // 2
    gates = gt_ref[...]
    hi = jnp.zeros((x_ref.shape[0], half), F32)
    lo = jnp.zeros((x_ref.shape[0], half), F32)
    for k, o_ref in enumerate((o0_ref, o1_ref, o2_ref, o3_ref)):
        h, l = _unpack_halves(o_ref[...])
        gk = gates[:, k:k + 1]
        hi = hi + gk * h
        lo = lo + gk * l
    x = x_ref[...]
    y_hi = DEEPNORM_ALPHA * x[:, :half] + hi
    y_lo = DEEPNORM_ALPHA * x[:, half:] + lo
    mu = (jnp.sum(y_hi, axis=-1, keepdims=True) + jnp.sum(y_lo, axis=-1, keepdims=True)) * (1.0 / D_MODEL)
    d_hi = y_hi - mu
    d_lo = y_lo - mu
    var = (jnp.sum(d_hi * d_hi, axis=-1, keepdims=True) + jnp.sum(d_lo * d_lo, axis=-1, keepdims=True)) * (1.0 / D_MODEL)
    r = lax.rsqrt(var + LN_EPS)
    out_hi = d_hi * r * g_ref[:, :half] + b_ref[:, :half]
    out_lo = d_lo * r * g_ref[:, half:] + b_ref[:, half:]

    def write(y_ref):
        y_ref[:, :half] = out_hi
        y_ref[:, half:] = out_lo

    if yb_ref is None:
        write(ya_ref)
    else:
        pl.when(pl.program_id(0) < n_first)(lambda: write(ya_ref))
        pl.when(pl.program_id(0) >= n_first)(lambda: write(yb_ref))


def _combine_ln(o4, gates, x, g, b, t_first=None):
    t = x.shape[0]
    tm = WIDE_TOKEN_TILE
    row = lambda i: (i, 0)
    fix = lambda i: (0, 0)
    choice = lambda k: pl.BlockSpec((tm, 512), lambda i: (k * (t // tm) + i, 0))
    if t_first is None:
        n_first = None
        out_specs = pl.BlockSpec((tm, D_MODEL), row)
        out_shape = jax.ShapeDtypeStruct((t, D_MODEL), F32)
    else:
        n_first = t_first // tm
        out_specs = [pl.BlockSpec((tm, D_MODEL), lambda i: (jnp.minimum(i, n_first - 1), 0)),
                     pl.BlockSpec((tm, D_MODEL), lambda i: (jnp.maximum(i - n_first, 0), 0))]
        out_shape = [jax.ShapeDtypeStruct((t_first, D_MODEL), F32), jax.ShapeDtypeStruct((t - t_first, D_MODEL), F32)]
    return pl.pallas_call(
        functools.partial(_combine_body, n_first),
        grid=(t // tm,),
        in_specs=[choice(0), choice(1), choice(2), choice(3), pl.BlockSpec((tm, TOP_K), row),
                  pl.BlockSpec((tm, D_MODEL), row), pl.BlockSpec((1, D_MODEL), fix), pl.BlockSpec((1, D_MODEL), fix)],
        out_specs=out_specs,
        out_shape=out_shape,
        compiler_params=_cparams(1),
        name="combine_ln",
    )(o4, o4, o4, o4, gates, x, g, b)


def _moe(layer, x1, xp, gates, eidx, rank, counts, ln_g, ln_b, wg, bg, wu, bu, wd, bd, t_first=None):
    t = x1.shape[0]
    bm = MOE_ROWS
    n_blocks = t * TOP_K // bm + N_EXPERTS
    n_rows = n_blocks * bm
    cnt = counts[:, 0]
    padded = (cnt + bm - 1) // bm * bm
    pad_end = jnp.cumsum(padded)
    pad_start = pad_end - padded
    e = eidx[:TOP_K]
    start = jnp.sum(jnp.where(e[:, :, None] == jnp.arange(N_EXPERTS, dtype=I32), pad_start, 0), axis=-1)
    dest = (start + rank[:TOP_K]).T
    n_blk = padded // bm
    last_valid = cnt - (n_blk - 1) * bm
    xs = _sc_scatter_rows(xp, dest, n_rows)
    outs = _experts(layer, pad_start // bm, n_blk, last_valid, xs, wg, bg, wu, bu, wd, bd)
    o4 = _sc_gather_rows(outs, dest.T.reshape(-1))
    return _combine_ln(o4, gates[:TOP_K].T, x1, ln_g, ln_b, t_first)


def _split2(x):
    hi = x.astype(BF16)
    return hi, (x - hi.astype(F32)).astype(BF16)


def _split3(x):
    hi = x.astype(BF16)
    rem = x - hi.astype(F32)
    mid = rem.astype(BF16)
    return hi, mid, (rem - mid.astype(F32)).astype(BF16)


def _chunk_cumsum(g, C):
    rows = g.shape[0]
    r = lax.broadcasted_iota(I32, (rows, rows), 0)
    c = lax.broadcasted_iota(I32, (rows, rows), 1)
    tri = ((r >= c) & (r // C == c // C)).astype(BF16)
    hi, mid, lo = _split3(g)
    dot = lambda part: jnp.dot(tri, part, preferred_element_type=F32)
    return dot(hi) + dot(mid) + dot(lo)


def _gla_batched_step(q, k, v, g, C, n_seq, n_heads, state_of, o_scr):
    rows = n_seq * C
    wide = n_seq * HEAD_W
    mid = max(C // 2 - 1, 0)
    r = lax.broadcasted_iota(I32, (rows, rows), 0)
    c = lax.broadcasted_iota(I32, (rows, rows), 1)
    same = (r // C) == (c // C)
    causal = same & (r >= c)
    parts = _split3(g)
    summed = lambda mask: functools.reduce(lambda p, q_: p + q_, [
        jnp.dot(mask.astype(BF16), part, preferred_element_type=F32) for part in parts])
    b = summed(causal)
    b_mid = summed(same & ((c % C) <= mid))
    b_last = summed(same)
    qe_hi, qe_lo = _split2(q * jnp.exp(b - b_mid))
    ke_hi, ke_lo = _split2(k * jnp.exp(b_mid - b))
    q_state = (q * jnp.exp(b)).astype(BF16)
    k_state = (k * jnp.exp(b_last - b)).astype(BF16)
    decay_parts = _split3(jnp.exp(b_last))
    row_w = lax.broadcasted_iota(I32, (rows, wide), 0)
    blk_w = lax.broadcasted_iota(I32, (rows, wide), 1) // HEAD_W
    own = (row_w // C) == blk_w
    pick = (row_w == blk_w * C).astype(BF16)
    new_states = []
    for h in range(n_heads):
        cs = slice(h * HEAD_W, (h + 1) * HEAD_W)
        lhs = jnp.concatenate([qe_hi[:, cs], qe_hi[:, cs], qe_lo[:, cs]], axis=1)
        rhs = jnp.concatenate([ke_hi[:, cs], ke_lo[:, cs], ke_hi[:, cs]], axis=1)
        scores = jnp.where(causal, _dot_nt(lhs, rhs), 0.0)
        vh = v[:, cs].astype(BF16)
        s_cat = jnp.concatenate([state_of(s, h) for s in range(n_seq)], axis=1)
        o_full = _dot(q_state[:, cs], s_cat)
        o_state = jnp.concatenate([o_full[s * C:(s + 1) * C, s * HEAD_W:(s + 1) * HEAD_W] for s in range(n_seq)],
                                  axis=0)
        o = _dot(scores, vh) + o_state
        v_wide = jnp.where(own, jnp.concatenate([vh] * n_seq, axis=1), jnp.zeros((), BF16))
        kv = _dot_tn(k_state[:, cs], v_wide)
        decay = functools.reduce(lambda p, q_: p + q_, [
            lax.dot_general(part[:, cs], pick, (((0,), (0,)), ((), ())), preferred_element_type=F32)
            for part in decay_parts])
        new_states.append(s_cat * decay + kv)
        ms = jnp.mean(o * o, axis=-1, keepdims=True)
        o_scr[:, cs] = o * lax.rsqrt(ms + RMS_EPS)
    return new_states


def _gla_body(mode, n_seq, n_chunk, C, *refs):
    if mode == "gla":
        hq_ref, hk_ref, hv_ref, hg_ref, hlr_ref, wlr_ref, blr_ref, nw_ref, s0_ref, _, o_ref, so_ref, st, o_scr = refs
    else:
        hq_ref, hk_ref, hv_ref, hg_ref, lb_ref, nw_ref, s0_ref, _, o_ref, so_ref, st, o_scr = refs
    n_heads = 4
    n_keys = s0_ref.shape[2]
    tstep = pl.program_id(1)
    batched = n_chunk == 1 and n_seq > 1

    def padded_state(s, h):
        s_in = s0_ref[s, h]
        if n_keys < HEAD_W:
            s_in = jnp.concatenate([s_in, jnp.zeros((HEAD_W - n_keys, HEAD_W), F32)], axis=0)
        return s_in

    if not batched:
        @pl.when(tstep == 0)
        def _():
            for s in range(n_seq):
                for h in range(n_heads):
                    st[s, h] = padded_state(s, h).T

    if mode == "gla":
        q = hq_ref[...] * (GLA_DK ** -0.5)
        k = hk_ref[...]
        z = _dot(hlr_ref[...], wlr_ref[...]) + blr_ref[...]
        g = _log_sigmoid(z) * (1.0 / GLA_TAU)
    else:
        q = _silu(hq_ref[...]) * (HGRN_DK ** -0.5)
        lb = lb_ref[...]
        f = lb + (1.0 - lb) * jax.nn.sigmoid(hk_ref[...])
        k = 1.0 - f
        g = jnp.log(f)
    v = hv_ref[...]
    if batched:
        new_states = _gla_batched_step(q, k, v, g, C, n_seq, n_heads, padded_state, o_scr)
        for s in range(n_seq):
            for h in range(n_heads):
                so_ref[s, h] = new_states[h][0:n_keys, s * HEAD_W:(s + 1) * HEAD_W]
    else:
        causal = _tri(C)
        mid = max(C // 2 - 1, 0)
        b_all = _chunk_cumsum(g, C)
        for s in range(n_seq):
            states = [st[s, h] for h in range(n_heads)]
            for c in range(n_chunk):
                r0 = (s * n_chunk + c) * C
                rs = slice(r0, r0 + C)
                b, qc, kc = b_all[rs, :], q[rs, :], k[rs, :]
                b_last = b[C - 1:C, :]
                b_mid = b[mid:mid + 1, :]
                qe_hi, qe_lo = _split2(qc * jnp.exp(b - b_mid))
                ke_hi, ke_lo = _split2(kc * jnp.exp(b_mid - b))
                q_state = (qc * jnp.exp(b)).astype(BF16)
                k_state = (kc * jnp.exp(b_last - b)).astype(BF16)
                decay = jnp.exp(b_last)
                for h in range(n_heads):
                    cs = slice(h * HEAD_W, (h + 1) * HEAD_W)
                    lhs = jnp.concatenate([qe_hi[:, cs], qe_hi[:, cs], qe_lo[:, cs]], axis=1)
                    rhs = jnp.concatenate([ke_hi[:, cs], ke_lo[:, cs], ke_hi[:, cs]], axis=1)
                    scores = jnp.where(causal, _dot_nt(lhs, rhs), 0.0)
                    vh = v[rs, cs].astype(BF16)
                    o = _dot(scores, vh) + _dot_nt(q_state[:, cs], states[h])
                    states[h] = states[h] * decay[:, cs] + _dot_tn(vh, k_state[:, cs])
                    ms = jnp.mean(o * o, axis=-1, keepdims=True)
                    o_scr[rs, cs] = o * lax.rsqrt(ms + RMS_EPS)
            for h in range(n_heads):
                st[s, h] = states[h]
    o_ref[...] = (o_scr[...] * nw_ref[...] * _silu(hg_ref[...])).astype(BF16)

    if not batched:
        @pl.when(tstep == pl.num_programs(1) - 1)
        def _():
            for s in range(n_seq):
                for h in range(n_heads):
                    so_ref[s, h] = st[s, h].T[0:n_keys, :]


def _seq_layout(n_batch, seq_len, row_off, sample, prompt_tile=PROMPT_TILE):
    if sample:
        n_seq, n_chunk, C = SAMPLE_SEQS, 1, seq_len
        rows = n_seq * C
        grid = (n_batch // n_seq, 1)
        blk0 = row_off // rows
        rb = lambda i, t: blk0 + i
    else:
        n_seq, n_chunk, C = 1, prompt_tile // SCAN_CHUNK, SCAN_CHUNK
        rows = prompt_tile
        tiles = seq_len // rows
        grid = (n_batch, tiles)
        blk0 = row_off // rows
        rb = lambda i, t: blk0 + i * tiles + t
    return n_seq, n_chunk, C, rows, grid, rb


def _launch(*calls):
    grid = calls[0]["grid"]
    assert all(c["grid"] == grid for c in calls)
    n_in = [len(c["args"]) for c in calls]
    n_out = [len(c["out_shape"]) for c in calls]
    n_scr = [len(c["scratch_shapes"]) for c in calls]
    aliases = {}
    for i, c in enumerate(calls):
        for src, dst in c["aliases"].items():
            aliases[sum(n_in[:i]) + src] = sum(n_out[:i]) + dst

    def body(*refs):
        ins, outs, scr = refs[:sum(n_in)], refs[sum(n_in):sum(n_in) + sum(n_out)], refs[sum(n_in) + sum(n_out):]
        for i, c in enumerate(calls):
            c["body"](*ins[sum(n_in[:i]):sum(n_in[:i + 1])], *outs[sum(n_out[:i]):sum(n_out[:i + 1])],
                      *scr[sum(n_scr[:i]):sum(n_scr[:i + 1])])

    flat = lambda key: [x for c in calls for x in c[key]]
    results = pl.pallas_call(
        body,
        grid=grid,
        in_specs=flat("in_specs"),
        out_specs=flat("out_specs"),
        out_shape=flat("out_shape"),
        scratch_shapes=flat("scratch_shapes"),
        input_output_aliases=aliases,
        compiler_params=_cparams(len(grid)),
        name="_".join(c["name"] for c in calls),
    )(*flat("args"))
    return [tuple(results[sum(n_out[:i]):sum(n_out[:i + 1])]) for i in range(len(calls))]


def _gla_call(mode, h, cols, extra, nw, s0, out_buf, n_batch, seq_len, row_off, sample):
    n_seq, n_chunk, C, rows, grid, rb = _seq_layout(n_batch, seq_len, row_off, sample)
    colspec = lambda c0, w: pl.BlockSpec((rows, w), lambda i, t: (rb(i, t), c0 // w))
    fix2 = lambda i, t: (0, 0)
    in_specs = [colspec(cols[0], 512), colspec(cols[1], 512), colspec(cols[2], 512), colspec(cols[3], 512)]
    args = [h, h, h, h]
    if mode == "gla":
        wlr, blr = extra
        in_specs += [colspec(cols[4], LANE), pl.BlockSpec((LANE, 512), fix2), pl.BlockSpec((1, 512), fix2)]
        args += [h, wlr, blr]
    else:
        in_specs += [pl.BlockSpec((1, 512), fix2)]
        args += [extra]
    n_keys = s0.shape[2]
    st_spec = pl.BlockSpec((n_seq, 4, n_keys, HEAD_W), lambda i, t: (i, 0, 0, 0))
    in_specs += [pl.BlockSpec((1, 512), fix2), st_spec, pl.BlockSpec(memory_space=pl.ANY)]
    args += [nw, s0, out_buf]
    o_spec = pl.BlockSpec((rows, 512), lambda i, t: (rb(i, t), 0))
    return dict(
        body=functools.partial(_gla_body, mode, n_seq, n_chunk, C),
        grid=grid,
        in_specs=in_specs,
        args=args,
        out_specs=[o_spec, st_spec],
        out_shape=[jax.ShapeDtypeStruct(out_buf.shape, out_buf.dtype),
                   jax.ShapeDtypeStruct((n_batch, 4, n_keys, HEAD_W), F32)],
        scratch_shapes=[pltpu.VMEM((n_seq, 4, HEAD_W, HEAD_W), F32), pltpu.VMEM((rows, 512), F32)],
        aliases={len(args) - 1: 0},
        name=mode + ("_sample" if sample else "_prompt"),
    )


def _round_bf16(x, on=True):
    return x.astype(BF16).astype(F32) if on else x


def _conf_body(n_seq, L, round_x, round_w, a_ref, gt_ref, hist_ref, w_ref, b_ref, g_ref, be_ref, _, o_ref, co_ref,
               buf, bufr, y_scr, win):
    tstep = pl.program_id(1)
    hist = CONF_WIDTH - 1
    pad = 32 - hist

    @pl.when(tstep == 0)
    def _():
        for s in range(n_seq):
            buf[s, pad:32, :] = hist_ref[s]
            bufr[s, pad:32, :] = _round_bf16(hist_ref[s], round_x)

    u = a_ref[...] * jax.nn.sigmoid(gt_ref[...])
    ur = _round_bf16(u, round_x)
    for s in range(n_seq):
        buf[s, 32:32 + L, :] = u[s * L:(s + 1) * L, :]
        bufr[s, 32:32 + L, :] = ur[s * L:(s + 1) * L, :]
    w = _round_bf16(w_ref[...], round_w)
    for s in range(n_seq):
        acc = jnp.zeros((L, CONF_DIM), F32)
        for phase in range(8):
            n_taps = (CONF_WIDTH - 1 - phase) // 8 + 1
            span = L + 8 * (n_taps - 1)
            win[phase, 0:span, :] = bufr[s, pad + phase:pad + phase + span, :]
            for a in range(n_taps):
                j = 8 * a + phase
                acc = acc + win[phase, 8 * a:8 * a + L, :] * w[j:j + 1, :]
        y_scr[s * L:(s + 1) * L, :] = _silu(_layernorm(acc + b_ref[...], g_ref[...], be_ref[...]))
        tail = buf[s, L + pad:L + 32, :]
        buf[s, pad:32, :] = tail
        tailr = bufr[s, L + pad:L + 32, :]
        bufr[s, pad:32, :] = tailr
    o_ref[...] = y_scr[...].astype(o_ref.dtype)

    @pl.when(tstep == pl.num_programs(1) - 1)
    def _():
        for s in range(n_seq):
            co_ref[s] = buf[s, pad:32, :]


def _conf_call(h, col_a, col_g, cache, w, b, g, be, out_buf, n_batch, seq_len, row_off, sample):
    n_seq, n_chunk, C, rows, grid, rb = _seq_layout(n_batch, seq_len, row_off, sample)
    L = rows // n_seq
    hist = CONF_WIDTH - 1
    colspec = lambda c0: pl.BlockSpec((rows, 512), lambda i, t: (rb(i, t), c0 // 512))
    fix2 = lambda i, t: (0, 0)
    c_spec = pl.BlockSpec((n_seq, hist, CONF_DIM), lambda i, t: (i, 0, 0))
    return dict(
        body=functools.partial(_conf_body, n_seq, L, True, sample),
        grid=grid,
        args=[h, h, cache, w, b, g, be, out_buf],
        in_specs=[colspec(col_a), colspec(col_g), c_spec,
                  pl.BlockSpec((CONF_WIDTH, CONF_DIM), fix2), pl.BlockSpec((1, CONF_DIM), fix2),
                  pl.BlockSpec((1, CONF_DIM), fix2), pl.BlockSpec((1, CONF_DIM), fix2),
                  pl.BlockSpec(memory_space=pl.ANY)],
        out_specs=[pl.BlockSpec((rows, 512), lambda i, t: (rb(i, t), 0)), c_spec],
        out_shape=[jax.ShapeDtypeStruct(out_buf.shape, out_buf.dtype),
                   jax.ShapeDtypeStruct((n_batch, hist, CONF_DIM), F32)],
        scratch_shapes=[pltpu.VMEM((n_seq, 32 + L, CONF_DIM), F32)] * 2 + [pltpu.VMEM((rows, CONF_DIM), F32),
                                                                           pltpu.VMEM((8, L + 24, CONF_DIM), F32)],
        aliases={7: 0},
        name="conformer" + ("_sample" if sample else "_prompt"),
    )


def _ssd_body(n_seq, n_chunk, C, round_x, round_w, hz_ref, hx_ref, hdt_ref, hist_ref, s0_ref, cw_ref, cb_ref, dtb_ref, alog_ref,
              dvec_ref, nw_ref, _, o_ref, co_ref, so_ref, st, buf, bufr, xbc, y_scr):
    tstep = pl.program_id(1)
    L = n_chunk * C
    hist = SSM_CONV - 1
    pad = 8 - hist
    n_pairs = SSM_HEADS // 2

    @pl.when(tstep == 0)
    def _():
        for s in range(n_seq):
            buf[s, pad:8, :] = hist_ref[s]
            bufr[s, pad:8, :] = _round_bf16(hist_ref[s], round_x)
            for m in range(n_pairs):
                st[s, m] = s0_ref[s, m]

    cw = _round_bf16(cw_ref[...], round_w)
    for s in range(n_seq):
        hx = hx_ref[s * L:(s + 1) * L, :]
        buf[s, 8:8 + L, :] = hx
        bufr[s, 8:8 + L, :] = _round_bf16(hx, round_x)
        acc = jnp.zeros((L, SSM_CONV_DIM), F32)
        for j in range(SSM_CONV):
            acc = acc + bufr[s, pad + j:pad + j + L, :] * cw[j:j + 1, :]
        xbc[s * L:(s + 1) * L, :] = _silu(acc + cb_ref[...])
        tail = buf[s, L + pad:L + 8, :]
        buf[s, pad:8, :] = tail
        tailr = bufr[s, L + pad:L + 8, :]
        bufr[s, pad:8, :] = tailr

    dt = _softplus(hdt_ref[...] + dtb_ref[...])
    la = dt * (-jnp.exp(alog_ref[...]))
    hrow = lax.broadcasted_iota(I32, (LANE, SSM_INNER), 0)
    hcol = lax.broadcasted_iota(I32, (LANE, SSM_INNER), 1) // SSM_HEADDIM
    expand = (hrow == hcol).astype(BF16)
    dtx = functools.reduce(lambda p, q: p + q,
                           [jnp.dot(part, expand, preferred_element_type=F32) for part in _split3(dt)])
    causal = _tri(C)
    tri = causal.astype(BF16)
    lane = lax.broadcasted_iota(I32, (C, HEAD_W), 1)
    bcol_all = _chunk_cumsum(la, C)
    heads_per_group = SSM_HEADS // SSM_GROUPS
    for s in range(n_seq):
        states = [st[s, m] for m in range(n_pairs)]
        for c in range(n_chunk):
            r0 = (s * n_chunk + c) * C
            rs = slice(r0, r0 + C)
            bcol = bcol_all[rs, :]
            brow = functools.reduce(lambda p, q: p + q, [
                lax.dot_general(part, tri, (((0,), (1,)), ((), ())), preferred_element_type=F32)
                for part in _split3(la[rs, :])])
            xs_c = xbc[rs, 0:SSM_INNER]
            v_c = (xs_c * dtx[rs, :]).astype(BF16)
            gmats, bms, cms = [], [], []
            for grp in range(SSM_GROUPS):
                bm = xbc[rs, SSM_INNER + grp * SSM_STATE:SSM_INNER + (grp + 1) * SSM_STATE]
                cm = xbc[rs, SSM_INNER + (SSM_GROUPS + grp) * SSM_STATE:SSM_INNER + (SSM_GROUPS + grp + 1) * SSM_STATE]
                cm_hi, cm_lo = _split2(cm)
                bm_hi, bm_lo = _split2(bm)
                gmats.append(_dot_nt(jnp.concatenate([cm_hi, cm_hi, cm_lo], axis=1),
                                     jnp.concatenate([bm_hi, bm_lo, bm_hi], axis=1)))
                bms.append(bm)
                cms.append(cm)
            for m in range(n_pairs):
                grp = (2 * m) // heads_per_group
                bm, cm, gmat = bms[grp], cms[grp], gmats[grp]
                ps = slice(m * HEAD_W, (m + 1) * HEAD_W)
                vp = v_c[:, ps]
                s_t = states[m]
                scores, queries, keys, decays = [], [], [], []
                for hh in range(2):
                    hd = 2 * m + hh
                    bc = bcol[:, hd:hd + 1]
                    br = brow[hd:hd + 1, :]
                    b_last = bcol[C - 1:C, hd:hd + 1]
                    scores.append(gmat * jnp.where(causal, jnp.exp(jnp.minimum(bc - br, 0.0)), 0.0))
                    queries.append(cm * jnp.exp(bc))
                    keys.append(bm * jnp.exp(b_last - bc))
                    decays.append(jnp.exp(b_last))
                o_stack = (_dot(jnp.concatenate(scores, axis=0), vp)
                           + _dot_nt(jnp.concatenate(queries, axis=0), s_t))
                kv = _dot_tn(vp, jnp.concatenate(keys, axis=1))
                d = SSM_HEADDIM
                states[m] = jnp.concatenate([s_t[0:d, :] * decays[0] + kv[0:d, 0:SSM_STATE],
                                             s_t[d:, :] * decays[1] + kv[d:, SSM_STATE:]], axis=0)
                o_pair = jnp.where(lane < SSM_HEADDIM, o_stack[0:C, :], o_stack[C:, :])
                y_scr[rs, ps] = o_pair + dvec_ref[:, ps] * xs_c[:, ps]
        for m in range(n_pairs):
            st[s, m] = states[m]
    y = y_scr[...] * _silu(hz_ref[...])
    gw = SSM_INNER // SSM_GROUPS
    for grp in range(SSM_GROUPS):
        gs = slice(grp * gw, (grp + 1) * gw)
        yg = y[:, gs]
        ms = jnp.mean(yg * yg, axis=-1, keepdims=True)
        o_ref[:, gs] = (yg * lax.rsqrt(ms + RMS_EPS) * nw_ref[:, gs]).astype(BF16)

    @pl.when(tstep == pl.num_programs(1) - 1)
    def _():
        for s in range(n_seq):
            co_ref[s] = buf[s, pad:8, :]
            for m in range(n_pairs):
                so_ref[s, m] = st[s, m]


def _ssd_call(h, col_z, col_x, col_dt, cache, s0, cw, cb, dtb, alog, dvec, nw, out_buf, n_batch, seq_len, row_off,
              sample):
    n_seq, n_chunk, C, rows, grid, rb = _seq_layout(n_batch, seq_len, row_off, sample, SSD_PROMPT_TILE)
    L = rows // n_seq
    hist = SSM_CONV - 1
    n_pairs = SSM_HEADS // 2
    colspec = lambda c0, w: pl.BlockSpec((rows, w), lambda i, t: (rb(i, t), c0 // w))
    fix2 = lambda i, t: (0, 0)
    c_spec = pl.BlockSpec((n_seq, hist, SSM_CONV_DIM), lambda i, t: (i, 0, 0))
    st_spec = pl.BlockSpec((n_seq, n_pairs, HEAD_W, SSM_STATE), lambda i, t: (i, 0, 0, 0))
    return dict(
        body=functools.partial(_ssd_body, n_seq, n_chunk, C, sample, True),
        grid=grid,
        args=[h, h, h, cache, s0, cw, cb, dtb, alog, dvec, nw, out_buf],
        in_specs=[colspec(col_z, 512), colspec(col_x, SSM_CONV_DIM), colspec(col_dt, LANE), c_spec, st_spec,
                  pl.BlockSpec((SSM_CONV, SSM_CONV_DIM), fix2), pl.BlockSpec((1, SSM_CONV_DIM), fix2),
                  pl.BlockSpec((1, LANE), fix2), pl.BlockSpec((1, LANE), fix2),
                  pl.BlockSpec((1, SSM_INNER), fix2), pl.BlockSpec((1, SSM_INNER), fix2),
                  pl.BlockSpec(memory_space=pl.ANY)],
        out_specs=[pl.BlockSpec((rows, 512), lambda i, t: (rb(i, t), 0)), c_spec, st_spec],
        out_shape=[jax.ShapeDtypeStruct(out_buf.shape, out_buf.dtype),
                   jax.ShapeDtypeStruct((n_batch, hist, SSM_CONV_DIM), F32),
                   jax.ShapeDtypeStruct((n_batch, n_pairs, HEAD_W, SSM_STATE), F32)],
        scratch_shapes=[pltpu.VMEM((n_seq, n_pairs, HEAD_W, SSM_STATE), F32),
                        pltpu.VMEM((n_seq, 8 + L, SSM_CONV_DIM), F32),
                        pltpu.VMEM((n_seq, 8 + L, SSM_CONV_DIM), F32),
                        pltpu.VMEM((rows, SSM_CONV_DIM), F32),
                        pltpu.VMEM((rows, SSM_INNER), F32)],
        aliases={11: 0},
        name="ssd" + ("_sample" if sample else "_prompt"),
    )


def _pad_heads(w, n_heads, width):
    lead = w.shape[:-1]
    w = w.reshape(lead + (n_heads, width))
    w = jnp.pad(w, [(0, 0)] * len(lead) + [(0, 0), (0, HEAD_W - width)])
    return w.reshape(lead + (n_heads * HEAD_W,))


def _row(v):
    return v.reshape(1, -1).astype(F32)


def kernel(x_prompt, x_sample, state_gla, cache_conformer, state_hgrn, state_ssm, cache_mamba_conv, w_in_even, w_gla_gate_lr, b_gla_gate, gla_norm_w, conf_conv_w, conf_conv_b, conf_ln_g, conf_ln_b, w_out_even, w_in_odd, hgrn_lower_bounds, hgrn_norm_w, mamba_conv_w, mamba_conv_b, mamba_dt_bias, mamba_a_log, mamba_d, mamba_norm_w, w_out_odd, ln1_g, ln1_b, ln2_g, ln2_b, router_w, router_b, expert_w_gate, expert_b_gate, expert_w_up, expert_b_up, expert_w_down, expert_b_down):
    bp, lp, _ = x_prompt.shape
    bs, ls, _ = x_sample.shape
    tp, ts = bp * lp, bs * ls
    x = (x_prompt.reshape(tp, D_MODEL), x_sample.reshape(ts, D_MODEL))

    def router_params(layer):
        return router_w[layer].T.astype(BF16), router_b[layer].astype(F32).reshape(N_EXPERTS, 1)

    def finish_layer(layer, x, mix_a, mix_b, w_out):
        rwt, rb = router_params(layer)
        x1, xp, gates, eidx, rank, counts = _outproj_ln_router(
            mix_a, mix_b, x[0], x[1], w_out[:512].astype(BF16), w_out[512:].astype(BF16),
            _row(ln1_g[layer]), _row(ln1_b[layer]), rwt, rb)
        return _moe(layer, x1, xp, gates, eidx, rank, counts, _row(ln2_g[layer]), _row(ln2_b[layer]),
                    expert_w_gate, expert_b_gate, expert_w_up, expert_b_up, expert_w_down, expert_b_down, tp)

    mix_init = jnp.zeros((tp + ts, 512), BF16)

    wi = w_in_even[0]
    wq, wk, wv, wg, wlr, wglu = jnp.split(wi, [256, 512, 1024, 1536, 1552], axis=1)
    w_even = jnp.concatenate([_pad_heads(wq, GLA_HEADS, GLA_DK), _pad_heads(wk, GLA_HEADS, GLA_DK), wv, wg, wglu,
                              jnp.pad(wlr, ((0, 0), (0, LANE - GLA_RANK)))], axis=1).astype(BF16)
    cols_gla = (0, 512, 1024, 1536, 3072)
    col_a, col_gate = 2048, 2560
    h = _inproj(x[0], x[1], w_even)
    wlr_p = jnp.pad(_pad_heads(w_gla_gate_lr[0], GLA_HEADS, GLA_DK), ((0, LANE - GLA_RANK), (0, 0)))
    blr_p = _row(_pad_heads(b_gla_gate[0], GLA_HEADS, GLA_DK))
    nw = _row(gla_norm_w[0])
    conf_args = (conf_conv_w[0], _row(conf_conv_b[0]), _row(conf_ln_g[0]), _row(conf_ln_b[0]))
    s0_p = jnp.zeros((bp, GLA_HEADS, GLA_DK, HEAD_W), F32)
    s0_s = state_gla[0]
    (mix_a, sg_p), (mix_b, cc_p) = _launch(
        _gla_call("gla", h, cols_gla, (wlr_p, blr_p), nw, s0_p, mix_init, bp, lp, 0, False),
        _conf_call(h, col_a, col_gate, jnp.zeros((bp,) + cache_conformer.shape[2:], F32), *conf_args,
                   mix_init, bp, lp, 0, False))
    (mix_a, sg_s), (mix_b, cc_s) = _launch(
        _gla_call("gla", h, cols_gla, (wlr_p, blr_p), nw, s0_s, mix_a, bs, ls, tp, True),
        _conf_call(h, col_a, col_gate, cache_conformer[0], *conf_args, mix_b, bs, ls, tp, True))
    x = finish_layer(0, x, mix_a, mix_b, w_out_even[0])
    gla_p, gla_s = sg_p[None], sg_s[None]
    conf_p, conf_s = cc_p[None], cc_s[None]

    lb_cum = jnp.cumsum(jax.nn.softmax(hgrn_lower_bounds.astype(F32), axis=0), axis=0)
    lower_bound = _row((lb_cum - lb_cum[0])[1])
    wo = w_in_odd[0]
    w_odd = jnp.concatenate([wo[:, 2560:3584], wo[:, :2560],
                             jnp.pad(wo[:, 3584:], ((0, 0), (0, LANE - SSM_HEADS)))], axis=1).astype(BF16)
    h = _inproj(x[0], x[1], w_odd)
    cols_hgrn = (1024, 1536, 2048, 2560)
    col_z, col_x, col_dt = 3072, 0, 3584
    nw = _row(hgrn_norm_w[0])
    ((mix_a, sh_p),) = _launch(_gla_call("hgrn", h, cols_hgrn, lower_bound, nw,
                                         jnp.zeros((bp, HGRN_HEADS, HEAD_W, HEAD_W), F32), mix_init, bp, lp, 0, False))
    ((mix_a, sh_s),) = _launch(_gla_call("hgrn", h, cols_hgrn, lower_bound, nw, state_hgrn[0], mix_a, bs, ls, tp, True))

    def pair_states(s):
        return jnp.swapaxes(s, 2, 3).reshape(s.shape[0], SSM_HEADS // 2, HEAD_W, SSM_STATE)

    def unpair_states(s):
        return jnp.swapaxes(s.reshape(s.shape[0], SSM_HEADS, SSM_HEADDIM, SSM_STATE), 2, 3)

    pad8 = lambda v: jnp.pad(v.astype(F32), (0, LANE - SSM_HEADS)).reshape(1, LANE)
    ssd_args = (mamba_conv_w[0], _row(mamba_conv_b[0]), pad8(mamba_dt_bias[0]), pad8(mamba_a_log[0]),
                _row(jnp.repeat(mamba_d[0], SSM_HEADDIM)), _row(mamba_norm_w[0]))
    ((mix_b, cm_p, ss_p),) = _launch(_ssd_call(
        h, col_z, col_x, col_dt, jnp.zeros((bp,) + cache_mamba_conv.shape[2:], F32),
        jnp.zeros((bp, SSM_HEADS // 2, HEAD_W, SSM_STATE), F32), *ssd_args, mix_init, bp, lp, 0, False))
    ((mix_b, cm_s, ss_s),) = _launch(_ssd_call(
        h, col_z, col_x, col_dt, cache_mamba_conv[0], pair_states(state_ssm[0]), *ssd_args, mix_b, bs, ls, tp, True))
    y_prompt, y_sample = finish_layer(1, x, mix_a, mix_b, w_out_odd[0])
    y_prompt = y_prompt.reshape(bp, lp, D_MODEL)
    y_sample = y_sample.reshape(bs, ls, D_MODEL)
    return (y_prompt, y_sample, gla_p, gla_s, conf_p, conf_s, sh_p[None], sh_s[None],
            unpair_states(ss_p)[None], unpair_states(ss_s)[None], cm_p[None], cm_s[None])
```

```python
import functools

import jax
import jax.numpy as jnp
from jax import lax
from jax.experimental import pallas as pl
from jax.experimental.pallas import tpu as pltpu
from jax.experimental.pallas import tpu_sc as plsc

F32 = jnp.float32
BF16 = jnp.bfloat16
I32 = jnp.int32
U32 = jnp.uint32

D_MODEL = 1024
DEPTH = 2
DEEPNORM_ALPHA = (2.0 * DEPTH) ** 0.25
LN_EPS = 1e-5
RMS_EPS = 1e-6
LANE = 128
HEAD_W = 128
GLA_HEADS, GLA_DK, GLA_RANK, GLA_TAU = 4, 64, 16, 16.0
CONF_DIM, CONF_WIDTH = 512, 31
HGRN_HEADS, HGRN_DK = 4, 128
SSM_HEADS, SSM_HEADDIM, SSM_STATE, SSM_GROUPS, SSM_CONV = 8, 64, 128, 2, 4
SSM_INNER = SSM_HEADS * SSM_HEADDIM
SSM_CONV_DIM = SSM_INNER + 2 * SSM_GROUPS * SSM_STATE
N_EXPERTS, TOP_K = 32, 4
SWIGLU_ALPHA, SWIGLU_LIMIT = 1.702, 7.0
SCAN_CHUNK = 64
PROMPT_TILE = 512
SSD_PROMPT_TILE = 256
SAMPLE_SEQS = 16
TOKEN_TILE = 512
WIDE_TOKEN_TILE = 1024
MOE_ROWS = 512
IN_SLOTS = 4
SC_CORES, SC_SUBCORES = 2, 16
SC_ROWS = 64
SC_SCATTER_ROWS = 32
VMEM_LIMIT = 56 * 1024 * 1024


def _cparams(n_axes):
    return pltpu.CompilerParams(dimension_semantics=("arbitrary",) * n_axes, vmem_limit_bytes=VMEM_LIMIT)


def _silu(x):
    return x * jax.nn.sigmoid(x)


def _softplus(x):
    return jnp.maximum(x, 0.0) + jnp.log(1.0 + jnp.exp(-jnp.abs(x)))


def _log_sigmoid(x):
    return jnp.minimum(x, 0.0) - jnp.log(1.0 + jnp.exp(-jnp.abs(x)))


def _layernorm(y, g, b):
    mu = jnp.mean(y, axis=-1, keepdims=True)
    d = y - mu
    var = jnp.mean(d * d, axis=-1, keepdims=True)
    return d * lax.rsqrt(var + LN_EPS) * g + b


def _dot(a, b):
    return jnp.dot(a.astype(BF16), b.astype(BF16), preferred_element_type=F32)


def _dot_nt(a, b):
    return lax.dot_general(a.astype(BF16), b.astype(BF16), (((1,), (1,)), ((), ())), preferred_element_type=F32)


def _dot_tn(a, b):
    return lax.dot_general(a.astype(BF16), b.astype(BF16), (((0,), (0,)), ((), ())), preferred_element_type=F32)


def _tri(c):
    r = lax.broadcasted_iota(I32, (c, c), 0)
    k = lax.broadcasted_iota(I32, (c, c), 1)
    return r >= k


def _tile_starts(xs, tm):
    starts, acc = [], 0
    for x in xs:
        assert x.shape[0] % tm == 0
        starts.append(acc)
        acc += x.shape[0] // tm
    return tuple(starts), acc


def _part_specs(xs, tm, width):
    starts, _ = _tile_starts(xs, tm)
    return [pl.BlockSpec((tm, width), lambda i, s=s, n=x.shape[0] // tm: (jnp.clip(i - s, 0, n - 1), 0))
            for s, x in zip(starts, xs)]


def _part_tile(starts, refs):
    x = refs[-1][...]
    for s_next, ref in zip(reversed(starts[1:]), reversed(refs[:-1])):
        x = jnp.where(pl.program_id(0) < s_next, ref[...], x)
    return x


def _inproj_body(starts, *refs):
    w_ref, o_ref = refs[-2:]
    xb = _part_tile(starts, refs[:-2]).astype(BF16)
    n = w_ref.shape[1]
    for c0 in range(0, n, 512):
        c1 = min(c0 + 512, n)
        o_ref[:, c0:c1] = jnp.dot(xb, w_ref[:, c0:c1], preferred_element_type=F32)


def _inproj(xs, w):
    k, n = w.shape
    starts, tiles = _tile_starts(xs, TOKEN_TILE)
    return pl.pallas_call(
        functools.partial(_inproj_body, starts),
        grid=(tiles,),
        in_specs=_part_specs(xs, TOKEN_TILE, k) + [pl.BlockSpec((k, n), lambda i: (0, 0))],
        out_specs=pl.BlockSpec((TOKEN_TILE, n), lambda i: (i, 0)),
        out_shape=jax.ShapeDtypeStruct((tiles * TOKEN_TILE, n), F32),
        compiler_params=_cparams(1),
        name="inproj",
    )(*xs, w)


def _pack_halves(y):
    half = y.shape[1] // 2
    hi = lax.bitcast_convert_type(y[:, :half].astype(BF16).astype(F32), U32)
    lo = lax.bitcast_convert_type(y[:, half:].astype(BF16).astype(F32), U32)
    return (hi & jnp.uint32(0xFFFF0000)) | (lo >> 16)


def _unpack_halves(w):
    hi = lax.bitcast_convert_type(w & jnp.uint32(0xFFFF0000), F32)
    lo = lax.bitcast_convert_type(w << 16, F32)
    return hi, lo


def _outproj_body(starts, a_ref, b_ref, *refs):
    x_refs = refs[:len(starts)]
    (wa_ref, wb_ref, g_ref, be_ref, rwt_ref, rb_ref,
     x1_ref, xp_ref, gate_ref, idx_ref, rank_ref, cnt_ref, carry) = refs[len(starts):]
    @pl.when(pl.program_id(0) == 0)
    def _():
        carry[...] = jnp.zeros(carry.shape, F32)

    mix = (jnp.dot(a_ref[...], wa_ref[...], preferred_element_type=F32)
           + jnp.dot(b_ref[...], wb_ref[...], preferred_element_type=F32))
    x1 = _layernorm(DEEPNORM_ALPHA * _part_tile(starts, x_refs) + mix, g_ref[...], be_ref[...])
    x1_ref[...] = x1
    xp_ref[...] = _pack_halves(x1)
    logits = _dot_nt(rwt_ref[...], x1) + rb_ref[...]
    tm = logits.shape[1]
    expert = lax.broadcasted_iota(I32, logits.shape, 0)
    vals, idxs = [], []
    for _ in range(TOP_K):
        m = jnp.max(logits, axis=0, keepdims=True)
        sel = jnp.min(jnp.where(logits == m, expert, N_EXPERTS), axis=0, keepdims=True)
        vals.append(m)
        idxs.append(sel)
        logits = jnp.where(expert == sel, -jnp.inf, logits)
    exps = [jnp.exp(v - vals[0]) for v in vals]
    inv = 1.0 / functools.reduce(lambda p, q: p + q, exps)
    chosen = jnp.zeros(logits.shape, F32)
    for k in range(TOP_K):
        chosen = chosen + (expert == idxs[k]).astype(F32)
    earlier = lax.broadcasted_iota(I32, (tm, tm), 0) < lax.broadcasted_iota(I32, (tm, tm), 1)
    before = carry[...] + jnp.dot(chosen.astype(BF16), earlier.astype(BF16), preferred_element_type=F32)
    choice = lax.broadcasted_iota(I32, (8, tm), 0)
    gates = jnp.zeros((8, tm), F32)
    eidx = jnp.zeros((8, tm), I32)
    ranks = jnp.zeros((8, tm), F32)
    for k in range(TOP_K):
        rk = jnp.sum(jnp.where(expert == idxs[k], before, 0.0), axis=0, keepdims=True)
        gates = jnp.where(choice == k, exps[k] * inv, gates)
        eidx = jnp.where(choice == k, idxs[k], eidx)
        ranks = jnp.where(choice == k, rk, ranks)
    gate_ref[...] = gates
    idx_ref[...] = eidx
    rank_ref[...] = ranks.astype(I32)
    carry[...] = carry[...] + jnp.sum(chosen, axis=1, keepdims=True)
    cnt_ref[...] = carry[...].astype(I32)


def _outproj_ln_router(a, b, xs, wa, wb, g, be, rwt, rb):
    tm = WIDE_TOKEN_TILE
    starts, tiles = _tile_starts(xs, tm)
    t = tiles * tm
    row = lambda i: (i, 0)
    col = lambda i: (0, i)
    fix = lambda i: (0, 0)
    return pl.pallas_call(
        functools.partial(_outproj_body, starts),
        grid=(tiles,),
        in_specs=[pl.BlockSpec((tm, 512), row), pl.BlockSpec((tm, 512), row)] + _part_specs(xs, tm, D_MODEL)
        + [pl.BlockSpec((512, D_MODEL), fix), pl.BlockSpec((512, D_MODEL), fix),
           pl.BlockSpec((1, D_MODEL), fix), pl.BlockSpec((1, D_MODEL), fix),
           pl.BlockSpec((N_EXPERTS, D_MODEL), fix), pl.BlockSpec((N_EXPERTS, 1), fix)],
        out_specs=[pl.BlockSpec((tm, D_MODEL), row), pl.BlockSpec((tm, 512), row),
                   pl.BlockSpec((8, tm), col), pl.BlockSpec((8, tm), col), pl.BlockSpec((8, tm), col),
                   pl.BlockSpec((N_EXPERTS, 1), fix)],
        out_shape=[jax.ShapeDtypeStruct((t, D_MODEL), F32), jax.ShapeDtypeStruct((t, 512), U32),
                   jax.ShapeDtypeStruct((8, t), F32), jax.ShapeDtypeStruct((8, t), I32),
                   jax.ShapeDtypeStruct((8, t), I32), jax.ShapeDtypeStruct((N_EXPERTS, 1), I32)],
        scratch_shapes=[pltpu.VMEM((N_EXPERTS, 1), F32)],
        compiler_params=_cparams(1),
        name="outproj_ln_router",
    )(a, b, *xs, wa, wb, g, be, rwt, rb)


def _sc_mesh():
    return plsc.VectorSubcoreMesh(core_axis_name="c", subcore_axis_name="s")


def _sc_scatter_rows(src, dest, n_out):
    n_src, w = src.shape
    n_dst = dest.shape[1]
    workers = SC_CORES * SC_SUBCORES
    per_worker = n_src // workers
    chunks = per_worker // SC_SCATTER_ROWS
    assert n_src == workers * chunks * SC_SCATTER_ROWS
    idx = dest.T.reshape(n_dst, workers, chunks, SC_SCATTER_ROWS)

    @functools.partial(pl.kernel, mesh=_sc_mesh(), out_type=jax.ShapeDtypeStruct((n_out, w), src.dtype),
                       scratch_types=[pltpu.VMEM((n_dst, chunks, SC_SCATTER_ROWS), I32)]
                       + [pltpu.VMEM((SC_SCATTER_ROWS, w), src.dtype)] * 2 + [pltpu.SemaphoreType.DMA] * 4)
    def scatter(src_hbm, idx_hbm, out_hbm, idx_v, rows_a, rows_b, sem_ra, sem_rb, sem_wa, sem_wb):
        worker = lax.axis_index("s") * SC_CORES + lax.axis_index("c")
        base = worker * per_worker
        for k in range(n_dst):
            pltpu.sync_copy(idx_hbm.at[k, worker], idx_v.at[k])

        def read(c, rows, sem):
            return pltpu.async_copy(src_hbm.at[pl.ds(pl.multiple_of(base + c * SC_SCATTER_ROWS, 8), SC_SCATTER_ROWS)],
                                    rows, sem)

        def write_all(pending_read, c, rows, sem):
            pending_read.wait()
            return [pltpu.async_copy(rows, out_hbm.at[idx_v.at[k, c]], sem) for k in range(n_dst)]

        @pl.loop(0, chunks // 2)
        def _(p):
            read_a = read(2 * p, rows_a, sem_ra)
            read_b = read(2 * p + 1, rows_b, sem_rb)
            writes = write_all(read_a, 2 * p, rows_a, sem_wa) + write_all(read_b, 2 * p + 1, rows_b, sem_wb)
            for wr in writes:
                wr.wait()

        if chunks % 2:
            for wr in write_all(read(chunks - 1, rows_a, sem_ra), chunks - 1, rows_a, sem_wa):
                wr.wait()

    return scatter(src, idx)


def _sc_gather_rows(table, idx):
    n, w = idx.shape[0], table.shape[1]
    workers = SC_CORES * SC_SUBCORES
    chunks = n // (workers * SC_ROWS)
    assert n == workers * chunks * SC_ROWS and chunks % 2 == 0
    idx = idx.reshape(workers, chunks, SC_ROWS)

    @functools.partial(pl.kernel, mesh=_sc_mesh(), out_type=jax.ShapeDtypeStruct((n, w), table.dtype),
                       scratch_types=[pltpu.VMEM((chunks, SC_ROWS), I32)] + [pltpu.VMEM((SC_ROWS, w), table.dtype)] * 2
                       + [pltpu.SemaphoreType.DMA] * 4)
    def gather(table_hbm, idx_hbm, out_hbm, idx_v, rows_a, rows_b, sem_ra, sem_rb, sem_wa, sem_wb):
        worker = lax.axis_index("s") * SC_CORES + lax.axis_index("c")
        base = worker * (chunks * SC_ROWS)
        pltpu.sync_copy(idx_hbm.at[worker], idx_v)

        def out_rows(c):
            return out_hbm.at[pl.ds(pl.multiple_of(base + c * SC_ROWS, 8), SC_ROWS)]

        @pl.loop(0, chunks // 2)
        def _(p):
            read_a = pltpu.async_copy(table_hbm.at[idx_v.at[2 * p]], rows_a, sem_ra)
            read_b = pltpu.async_copy(table_hbm.at[idx_v.at[2 * p + 1]], rows_b, sem_rb)
            read_a.wait()
            write_a = pltpu.async_copy(rows_a, out_rows(2 * p), sem_wa)
            read_b.wait()
            write_b = pltpu.async_copy(rows_b, out_rows(2 * p + 1), sem_wb)
            write_a.wait()
            write_b.wait()

    return gather(table, idx)


def _experts_body(b0_ref, nb_ref, last_ref, nt_ref, xs_hbm, wg_ref, bg_ref, wu_ref, bu_ref, wd_ref, bd_ref, o_hbm,
                  wg_s, wu_s, wd_s, xbuf, obuf, sem_in, sem_out):
    e = pl.program_id(0)
    first_blk, n_blk, last_valid, n_total = b0_ref[e], nb_ref[e], last_ref[e], nt_ref[0]
    quarter = MOE_ROWS // 4

    def rows_of(g):
        return pl.ds(pl.multiple_of(g * MOE_ROWS, MOE_ROWS), MOE_ROWS)

    def fetch(g, slot):
        return pltpu.make_async_copy(xs_hbm.at[rows_of(g)], xbuf.at[slot], sem_in.at[slot])

    def put(g, slot):
        return pltpu.make_async_copy(obuf.at[slot], o_hbm.at[rows_of(g)], sem_out.at[slot])

    lead = IN_SLOTS - 1
    for first in range(lead):
        @pl.when((e == 0) & (n_total > first))
        def _():
            fetch(first, first).start()

    @pl.when(n_blk > 0)
    def _():
        wg_s[...] = wg_ref[...].astype(BF16)
        wu_s[...] = wu_ref[...].astype(BF16)
        wd_s[...] = wd_ref[...].astype(BF16)

    def compute(islot, slot, rows):
        half = D_MODEL // 2
        x_hi, x_lo = _unpack_halves(xbuf[islot, 0:rows, :])
        x_hi = x_hi.astype(BF16)
        x_lo = x_lo.astype(BF16)
        g = (jnp.dot(x_hi, wg_s[:half, :], preferred_element_type=F32)
             + jnp.dot(x_lo, wg_s[half:, :], preferred_element_type=F32) + bg_ref[...])
        u = (jnp.dot(x_hi, wu_s[:half, :], preferred_element_type=F32)
             + jnp.dot(x_lo, wu_s[half:, :], preferred_element_type=F32) + bu_ref[...])
        g = jnp.minimum(g, SWIGLU_LIMIT)
        u = jnp.clip(u, -SWIGLU_LIMIT, SWIGLU_LIMIT)
        hmid = (u + 1.0) * (g * jax.nn.sigmoid(SWIGLU_ALPHA * g))
        out = jnp.dot(hmid.astype(BF16), wd_s[...], preferred_element_type=F32) + bd_ref[...]
        obuf[slot, 0:rows, :] = _pack_halves(out)

    def block(j, carry):
        g = first_blk + j
        slot = lax.rem(g, 2)
        islot = lax.rem(g, IN_SLOTS)
        fetch(g, islot).wait()

        @pl.when(g + lead < n_total)
        def _():
            fetch(g + lead, lax.rem(g + lead, IN_SLOTS)).start()

        @pl.when(g >= 2)
        def _():
            put(g - 2, slot).wait()

        valid = jnp.where(j == n_blk - 1, last_valid, MOE_ROWS)

        for rows in range(quarter, MOE_ROWS + 1, quarter):
            @pl.when((valid > rows - quarter) & (valid <= rows))
            def _():
                compute(islot, slot, rows)
                if rows < MOE_ROWS:
                    obuf[slot, rows:, :] = jnp.zeros((MOE_ROWS - rows, obuf.shape[2]), obuf.dtype)

        put(g, slot).start()
        return carry

    lax.fori_loop(0, n_blk, block, 0)

    @pl.when((e == N_EXPERTS - 1) & (n_total >= 2))
    def _():
        put(n_total - 2, lax.rem(n_total, 2)).wait()

    @pl.when((e == N_EXPERTS - 1) & (n_total >= 1))
    def _():
        put(n_total - 1, lax.rem(n_total - 1, 2)).wait()


def _experts(layer, first_blk, n_blk, last_valid, xs, wg, bg, wu, bu, wd, bd):
    n_rows, w = xs.shape
    wsel = lambda e, b0, nb, lv, nt: (layer, e, 0, 0)
    wspec = pl.BlockSpec((None, None, D_MODEL, D_MODEL), wsel)
    bspec = pl.BlockSpec((None, None, 1, D_MODEL), wsel)
    bias = lambda b: b.reshape(b.shape[0], b.shape[1], 1, b.shape[2])
    return pl.pallas_call(
        _experts_body,
        grid_spec=pltpu.PrefetchScalarGridSpec(
            num_scalar_prefetch=4,
            grid=(N_EXPERTS,),
            in_specs=[pl.BlockSpec(memory_space=pl.ANY), wspec, bspec, wspec, bspec, wspec, bspec],
            out_specs=pl.BlockSpec(memory_space=pl.ANY),
            scratch_shapes=[pltpu.VMEM((D_MODEL, D_MODEL), BF16)] * 3
            + [pltpu.VMEM((IN_SLOTS, MOE_ROWS, w), U32), pltpu.VMEM((2, MOE_ROWS, w), U32),
               pltpu.SemaphoreType.DMA((IN_SLOTS,)), pltpu.SemaphoreType.DMA((2,))],
        ),
        out_shape=jax.ShapeDtypeStruct((n_rows, w), U32),
        compiler_params=_cparams(1),
        name="experts",
    )(first_blk, n_blk, last_valid, jnp.sum(n_blk).reshape(1), xs, wg, bias(bg), wu, bias(bu), wd, bias(bd))


def _combine_body(n_first, o0_ref, o1_ref, o2_ref, o3_ref, gt_ref, x_ref, g_ref, b_ref, ya_ref, yb_ref=None):
    half = D_MODEL // 2
    gates = gt_ref[...]
    hi = jnp.zeros((x_ref.shape[0], half), F32)
    lo = jnp.zeros((x_ref.shape[0], half), F32)
    for k, o_ref in enumerate((o0_ref, o1_ref, o2_ref, o3_ref)):
        h, l = _unpack_halves(o_ref[...])
        gk = gates[:, k:k + 1]
        hi = hi + gk * h
        lo = lo + gk * l
    x = x_ref[...]
    y_hi = DEEPNORM_ALPHA * x[:, :half] + hi
    y_lo = DEEPNORM_ALPHA * x[:, half:] + lo
    mu = (jnp.sum(y_hi, axis=-1, keepdims=True) + jnp.sum(y_lo, axis=-1, keepdims=True)) * (1.0 / D_MODEL)
    d_hi = y_hi - mu
    d_lo = y_lo - mu
    var = (jnp.sum(d_hi * d_hi, axis=-1, keepdims=True) + jnp.sum(d_lo * d_lo, axis=-1, keepdims=True)) * (1.0 / D_MODEL)
    r = lax.rsqrt(var + LN_EPS)
    out_hi = d_hi * r * g_ref[:, :half] + b_ref[:, :half]
    out_lo = d_lo * r * g_ref[:, half:] + b_ref[:, half:]

    def write(y_ref):
        y_ref[:, :half] = out_hi
        y_ref[:, half:] = out_lo

    if yb_ref is None:
        write(ya_ref)
    else:
        pl.when(pl.program_id(0) < n_first)(lambda: write(ya_ref))
        pl.when(pl.program_id(0) >= n_first)(lambda: write(yb_ref))


def _combine_ln(o4, gates, x, g, b, t_first=None, row0=0):
    t = gates.shape[0]
    tm = WIDE_TOKEN_TILE
    row = lambda i: (i, 0)
    fix = lambda i: (0, 0)
    choice = lambda k: pl.BlockSpec((tm, 512), lambda i: (k * (t // tm) + i, 0))
    if t_first is None:
        n_first = None
        out_specs = pl.BlockSpec((tm, D_MODEL), row)
        out_shape = jax.ShapeDtypeStruct((t, D_MODEL), F32)
    else:
        n_first = t_first // tm
        out_specs = [pl.BlockSpec((tm, D_MODEL), lambda i: (jnp.minimum(i, n_first - 1), 0)),
                     pl.BlockSpec((tm, D_MODEL), lambda i: (jnp.maximum(i - n_first, 0), 0))]
        out_shape = [jax.ShapeDtypeStruct((t_first, D_MODEL), F32), jax.ShapeDtypeStruct((t - t_first, D_MODEL), F32)]
    return pl.pallas_call(
        functools.partial(_combine_body, n_first),
        grid=(t // tm,),
        in_specs=[choice(0), choice(1), choice(2), choice(3), pl.BlockSpec((tm, TOP_K), row),
                  pl.BlockSpec((tm, D_MODEL), lambda i: (row0 // tm + i, 0)),
                  pl.BlockSpec((1, D_MODEL), fix), pl.BlockSpec((1, D_MODEL), fix)],
        out_specs=out_specs,
        out_shape=out_shape,
        compiler_params=_cparams(1),
        name="combine_ln",
    )(o4, o4, o4, o4, gates, x, g, b)


def _moe(layer, x1, xp, gates, eidx, rank, counts, ln_g, ln_b, wg, bg, wu, bu, wd, bd, t_first, t_split=None):
    t = x1.shape[0]
    bm = MOE_ROWS
    n_blocks = t * TOP_K // bm + N_EXPERTS
    n_rows = n_blocks * bm
    cnt = counts[:, 0]
    padded = (cnt + bm - 1) // bm * bm
    pad_end = jnp.cumsum(padded)
    pad_start = pad_end - padded
    e = eidx[:TOP_K]
    start = jnp.sum(jnp.where(e[:, :, None] == jnp.arange(N_EXPERTS, dtype=I32), pad_start, 0), axis=-1)
    dest = (start + rank[:TOP_K]).T
    n_blk = padded // bm
    last_valid = cnt - (n_blk - 1) * bm
    xs = _sc_scatter_rows(xp, dest, n_rows)
    outs = _experts(layer, pad_start // bm, n_blk, last_valid, xs, wg, bg, wu, bu, wd, bd)
    dest_t, gates_t = dest.T, gates[:TOP_K].T
    if t_split is None:
        o4 = _sc_gather_rows(outs, dest_t.reshape(-1))
        return _combine_ln(o4, gates_t, x1, ln_g, ln_b, t_first)
    o4_a = _sc_gather_rows(outs, dest_t[:, :t_split].reshape(-1))
    o4_b = _sc_gather_rows(outs, dest_t[:, t_split:].reshape(-1))
    y_a = _combine_ln(o4_a, gates_t[:t_split], x1, ln_g, ln_b)
    y_b, y_c = _combine_ln(o4_b, gates_t[t_split:], x1, ln_g, ln_b, t_first - t_split, row0=t_split)
    return y_a, y_b, y_c


def _split2(x):
    hi = x.astype(BF16)
    return hi, (x - hi.astype(F32)).astype(BF16)


def _split3(x):
    hi = x.astype(BF16)
    rem = x - hi.astype(F32)
    mid = rem.astype(BF16)
    return hi, mid, (rem - mid.astype(F32)).astype(BF16)


def _chunk_cumsum(g, C):
    rows = g.shape[0]
    r = lax.broadcasted_iota(I32, (rows, rows), 0)
    c = lax.broadcasted_iota(I32, (rows, rows), 1)
    tri = ((r >= c) & (r // C == c // C)).astype(BF16)
    hi, mid, lo = _split3(g)
    dot = lambda part: jnp.dot(tri, part, preferred_element_type=F32)
    return dot(hi) + dot(mid) + dot(lo)


def _gla_batched_step(q, k, v, g, C, n_seq, n_heads, state_of, o_scr):
    rows = n_seq * C
    wide = n_seq * HEAD_W
    mid = max(C // 2 - 1, 0)
    r = lax.broadcasted_iota(I32, (rows, rows), 0)
    c = lax.broadcasted_iota(I32, (rows, rows), 1)
    same = (r // C) == (c // C)
    causal = same & (r >= c)
    parts = _split3(g)
    summed = lambda mask: functools.reduce(lambda p, q_: p + q_, [
        jnp.dot(mask.astype(BF16), part, preferred_element_type=F32) for part in parts])
    b = summed(causal)
    b_mid = summed(same & ((c % C) <= mid))
    b_last = summed(same)
    qe_hi, qe_lo = _split2(q * jnp.exp(b - b_mid))
    ke_hi, ke_lo = _split2(k * jnp.exp(b_mid - b))
    q_state = (q * jnp.exp(b)).astype(BF16)
    k_state = (k * jnp.exp(b_last - b)).astype(BF16)
    decay_parts = _split3(jnp.exp(b_last))
    row_w = lax.broadcasted_iota(I32, (rows, wide), 0)
    blk_w = lax.broadcasted_iota(I32, (rows, wide), 1) // HEAD_W
    own = (row_w // C) == blk_w
    pick = (row_w == blk_w * C).astype(BF16)
    new_states = []
    for h in range(n_heads):
        cs = slice(h * HEAD_W, (h + 1) * HEAD_W)
        lhs = jnp.concatenate([qe_hi[:, cs], qe_hi[:, cs], qe_lo[:, cs]], axis=1)
        rhs = jnp.concatenate([ke_hi[:, cs], ke_lo[:, cs], ke_hi[:, cs]], axis=1)
        scores = jnp.where(causal, _dot_nt(lhs, rhs), 0.0)
        vh = v[:, cs].astype(BF16)
        s_cat = jnp.concatenate([state_of(s, h) for s in range(n_seq)], axis=1)
        o_full = _dot(q_state[:, cs], s_cat)
        o_state = jnp.concatenate([o_full[s * C:(s + 1) * C, s * HEAD_W:(s + 1) * HEAD_W] for s in range(n_seq)],
                                  axis=0)
        o = _dot(scores, vh) + o_state
        v_wide = jnp.where(own, jnp.concatenate([vh] * n_seq, axis=1), jnp.zeros((), BF16))
        kv = _dot_tn(k_state[:, cs], v_wide)
        decay = functools.reduce(lambda p, q_: p + q_, [
            lax.dot_general(part[:, cs], pick, (((0,), (0,)), ((), ())), preferred_element_type=F32)
            for part in decay_parts])
        new_states.append(s_cat * decay + kv)
        ms = jnp.mean(o * o, axis=-1, keepdims=True)
        o_scr[:, cs] = o * lax.rsqrt(ms + RMS_EPS)
    return new_states


def _gla_body(mode, n_seq, n_chunk, C, *refs):
    if mode == "gla":
        hq_ref, hk_ref, hv_ref, hg_ref, hlr_ref, wlr_ref, blr_ref, nw_ref, s0_ref, _, o_ref, so_ref, st, o_scr = refs
    else:
        hq_ref, hk_ref, hv_ref, hg_ref, lb_ref, nw_ref, s0_ref, _, o_ref, so_ref, st, o_scr = refs
    n_heads = 4
    n_keys = s0_ref.shape[2]
    tstep = pl.program_id(1)
    batched = n_chunk == 1 and n_seq > 1

    def padded_state(s, h):
        s_in = s0_ref[s, h]
        if n_keys < HEAD_W:
            s_in = jnp.concatenate([s_in, jnp.zeros((HEAD_W - n_keys, HEAD_W), F32)], axis=0)
        return s_in

    if not batched:
        @pl.when(tstep == 0)
        def _():
            for s in range(n_seq):
                for h in range(n_heads):
                    st[s, h] = padded_state(s, h).T

    if mode == "gla":
        q = hq_ref[...] * (GLA_DK ** -0.5)
        k = hk_ref[...]
        z = _dot(hlr_ref[...], wlr_ref[...]) + blr_ref[...]
        g = _log_sigmoid(z) * (1.0 / GLA_TAU)
    else:
        q = _silu(hq_ref[...]) * (HGRN_DK ** -0.5)
        lb = lb_ref[...]
        f = lb + (1.0 - lb) * jax.nn.sigmoid(hk_ref[...])
        k = 1.0 - f
        g = jnp.log(f)
    v = hv_ref[...]
    if batched:
        new_states = _gla_batched_step(q, k, v, g, C, n_seq, n_heads, padded_state, o_scr)
        for s in range(n_seq):
            for h in range(n_heads):
                so_ref[s, h] = new_states[h][0:n_keys, s * HEAD_W:(s + 1) * HEAD_W]
    else:
        causal = _tri(C)
        mid = max(C // 2 - 1, 0)
        b_all = _chunk_cumsum(g, C)
        for s in range(n_seq):
            states = [st[s, h] for h in range(n_heads)]
            for c in range(n_chunk):
                r0 = (s * n_chunk + c) * C
                rs = slice(r0, r0 + C)
                b, qc, kc = b_all[rs, :], q[rs, :], k[rs, :]
                b_last = b[C - 1:C, :]
                b_mid = b[mid:mid + 1, :]
                qe_hi, qe_lo = _split2(qc * jnp.exp(b - b_mid))
                ke_hi, ke_lo = _split2(kc * jnp.exp(b_mid - b))
                q_state = (qc * jnp.exp(b)).astype(BF16)
                k_state = (kc * jnp.exp(b_last - b)).astype(BF16)
                decay = jnp.exp(b_last)
                for h in range(n_heads):
                    cs = slice(h * HEAD_W, (h + 1) * HEAD_W)
                    lhs = jnp.concatenate([qe_hi[:, cs], qe_hi[:, cs], qe_lo[:, cs]], axis=1)
                    rhs = jnp.concatenate([ke_hi[:, cs], ke_lo[:, cs], ke_hi[:, cs]], axis=1)
                    scores = jnp.where(causal, _dot_nt(lhs, rhs), 0.0)
                    vh = v[rs, cs].astype(BF16)
                    o = _dot(scores, vh) + _dot_nt(q_state[:, cs], states[h])
                    states[h] = states[h] * decay[:, cs] + _dot_tn(vh, k_state[:, cs])
                    ms = jnp.mean(o * o, axis=-1, keepdims=True)
                    o_scr[rs, cs] = o * lax.rsqrt(ms + RMS_EPS)
            for h in range(n_heads):
                st[s, h] = states[h]
    o_ref[...] = (o_scr[...] * nw_ref[...] * _silu(hg_ref[...])).astype(BF16)

    if not batched:
        @pl.when(tstep == pl.num_programs(1) - 1)
        def _():
            for s in range(n_seq):
                for h in range(n_heads):
                    so_ref[s, h] = st[s, h].T[0:n_keys, :]


def _seq_layout(n_batch, seq_len, row_off, sample, prompt_tile=PROMPT_TILE):
    if sample:
        n_seq, n_chunk, C = SAMPLE_SEQS, 1, seq_len
        rows = n_seq * C
        grid = (n_batch // n_seq, 1)
        blk0 = row_off // rows
        rb = lambda i, t: blk0 + i
    else:
        n_seq, n_chunk, C = 1, prompt_tile // SCAN_CHUNK, SCAN_CHUNK
        rows = prompt_tile
        tiles = seq_len // rows
        grid = (n_batch, tiles)
        blk0 = row_off // rows
        rb = lambda i, t: blk0 + i * tiles + t
    return n_seq, n_chunk, C, rows, grid, rb


def _gla_call(mode, h, cols, extra, nw, s0, out_buf, n_batch, seq_len, row_off, sample):
    n_seq, n_chunk, C, rows, grid, rb = _seq_layout(n_batch, seq_len, row_off, sample)
    colspec = lambda c0, w: pl.BlockSpec((rows, w), lambda i, t: (rb(i, t), c0 // w))
    fix2 = lambda i, t: (0, 0)
    in_specs = [colspec(cols[0], 512), colspec(cols[1], 512), colspec(cols[2], 512), colspec(cols[3], 512)]
    args = [h, h, h, h]
    if mode == "gla":
        wlr, blr = extra
        in_specs += [colspec(cols[4], LANE), pl.BlockSpec((LANE, 512), fix2), pl.BlockSpec((1, 512), fix2)]
        args += [h, wlr, blr]
    else:
        in_specs += [pl.BlockSpec((1, 512), fix2)]
        args += [extra]
    n_keys = s0.shape[2]
    st_spec = pl.BlockSpec((n_seq, 4, n_keys, HEAD_W), lambda i, t: (i, 0, 0, 0))
    in_specs += [pl.BlockSpec((1, 512), fix2), st_spec, pl.BlockSpec(memory_space=pl.ANY)]
    args += [nw, s0, out_buf]
    o_spec = pl.BlockSpec((rows, 512), lambda i, t: (rb(i, t), 0))
    return pl.pallas_call(
        functools.partial(_gla_body, mode, n_seq, n_chunk, C),
        grid=grid,
        in_specs=in_specs,
        out_specs=[o_spec, st_spec],
        out_shape=[jax.ShapeDtypeStruct(out_buf.shape, out_buf.dtype),
                   jax.ShapeDtypeStruct((n_batch, 4, n_keys, HEAD_W), F32)],
        scratch_shapes=[pltpu.VMEM((n_seq, 4, HEAD_W, HEAD_W), F32), pltpu.VMEM((rows, 512), F32)],
        input_output_aliases={len(args) - 1: 0},
        compiler_params=_cparams(2),
        name=mode + ("_sample" if sample else "_prompt"),
    )(*args)


def _round_bf16(x, on=True):
    return x.astype(BF16).astype(F32) if on else x


def _conf_body(n_seq, L, round_x, round_w, a_ref, gt_ref, hist_ref, w_ref, b_ref, g_ref, be_ref, _, o_ref, co_ref,
               buf, bufr, y_scr, win):
    tstep = pl.program_id(1)
    hist = CONF_WIDTH - 1
    pad = 32 - hist

    @pl.when(tstep == 0)
    def _():
        for s in range(n_seq):
            buf[s, pad:32, :] = hist_ref[s]
            bufr[s, pad:32, :] = _round_bf16(hist_ref[s], round_x)

    u = a_ref[...] * jax.nn.sigmoid(gt_ref[...])
    ur = _round_bf16(u, round_x)
    for s in range(n_seq):
        buf[s, 32:32 + L, :] = u[s * L:(s + 1) * L, :]
        bufr[s, 32:32 + L, :] = ur[s * L:(s + 1) * L, :]
    w = _round_bf16(w_ref[...], round_w)
    for s in range(n_seq):
        acc = jnp.zeros((L, CONF_DIM), F32)
        for phase in range(8):
            n_taps = (CONF_WIDTH - 1 - phase) // 8 + 1
            span = L + 8 * (n_taps - 1)
            win[phase, 0:span, :] = bufr[s, pad + phase:pad + phase + span, :]
            for a in range(n_taps):
                j = 8 * a + phase
                acc = acc + win[phase, 8 * a:8 * a + L, :] * w[j:j + 1, :]
        y_scr[s * L:(s + 1) * L, :] = _silu(_layernorm(acc + b_ref[...], g_ref[...], be_ref[...]))
        tail = buf[s, L + pad:L + 32, :]
        buf[s, pad:32, :] = tail
        tailr = bufr[s, L + pad:L + 32, :]
        bufr[s, pad:32, :] = tailr
    o_ref[...] = y_scr[...].astype(o_ref.dtype)

    @pl.when(tstep == pl.num_programs(1) - 1)
    def _():
        for s in range(n_seq):
            co_ref[s] = buf[s, pad:32, :]


def _conf_call(h, col_a, col_g, cache, w, b, g, be, out_buf, n_batch, seq_len, row_off, sample):
    n_seq, n_chunk, C, rows, grid, rb = _seq_layout(n_batch, seq_len, row_off, sample)
    L = rows // n_seq
    hist = CONF_WIDTH - 1
    colspec = lambda c0: pl.BlockSpec((rows, 512), lambda i, t: (rb(i, t), c0 // 512))
    fix2 = lambda i, t: (0, 0)
    c_spec = pl.BlockSpec((n_seq, hist, CONF_DIM), lambda i, t: (i, 0, 0))
    return pl.pallas_call(
        functools.partial(_conf_body, n_seq, L, True, sample),
        grid=grid,
        in_specs=[colspec(col_a), colspec(col_g), c_spec,
                  pl.BlockSpec((CONF_WIDTH, CONF_DIM), fix2), pl.BlockSpec((1, CONF_DIM), fix2),
                  pl.BlockSpec((1, CONF_DIM), fix2), pl.BlockSpec((1, CONF_DIM), fix2),
                  pl.BlockSpec(memory_space=pl.ANY)],
        out_specs=[pl.BlockSpec((rows, 512), lambda i, t: (rb(i, t), 0)), c_spec],
        out_shape=[jax.ShapeDtypeStruct(out_buf.shape, out_buf.dtype),
                   jax.ShapeDtypeStruct((n_batch, hist, CONF_DIM), F32)],
        scratch_shapes=[pltpu.VMEM((n_seq, 32 + L, CONF_DIM), F32)] * 2 + [pltpu.VMEM((rows, CONF_DIM), F32),
                                                                           pltpu.VMEM((8, L + 24, CONF_DIM), F32)],
        input_output_aliases={7: 0},
        compiler_params=_cparams(2),
        name="conformer" + ("_sample" if sample else "_prompt"),
    )(h, h, cache, w, b, g, be, out_buf)


def _ssd_body(n_seq, n_chunk, C, round_x, round_w, hz_ref, hx_ref, hdt_ref, hist_ref, s0_ref, cw_ref, cb_ref, dtb_ref, alog_ref,
              dvec_ref, nw_ref, _, o_ref, co_ref, so_ref, st, buf, bufr, xbc, y_scr):
    tstep = pl.program_id(1)
    L = n_chunk * C
    hist = SSM_CONV - 1
    pad = 8 - hist
    n_pairs = SSM_HEADS // 2

    @pl.when(tstep == 0)
    def _():
        for s in range(n_seq):
            buf[s, pad:8, :] = hist_ref[s]
            bufr[s, pad:8, :] = _round_bf16(hist_ref[s], round_x)
            for m in range(n_pairs):
                st[s, m] = s0_ref[s, m]

    cw = _round_bf16(cw_ref[...], round_w)
    for s in range(n_seq):
        hx = hx_ref[s * L:(s + 1) * L, :]
        buf[s, 8:8 + L, :] = hx
        bufr[s, 8:8 + L, :] = _round_bf16(hx, round_x)
        acc = jnp.zeros((L, SSM_CONV_DIM), F32)
        for j in range(SSM_CONV):
            acc = acc + bufr[s, pad + j:pad + j + L, :] * cw[j:j + 1, :]
        xbc[s * L:(s + 1) * L, :] = _silu(acc + cb_ref[...])
        tail = buf[s, L + pad:L + 8, :]
        buf[s, pad:8, :] = tail
        tailr = bufr[s, L + pad:L + 8, :]
        bufr[s, pad:8, :] = tailr

    dt = _softplus(hdt_ref[...] + dtb_ref[...])
    la = dt * (-jnp.exp(alog_ref[...]))
    hrow = lax.broadcasted_iota(I32, (LANE, SSM_INNER), 0)
    hcol = lax.broadcasted_iota(I32, (LANE, SSM_INNER), 1) // SSM_HEADDIM
    expand = (hrow == hcol).astype(BF16)
    dtx = functools.reduce(lambda p, q: p + q,
                           [jnp.dot(part, expand, preferred_element_type=F32) for part in _split3(dt)])
    causal = _tri(C)
    tri = causal.astype(BF16)
    lane = lax.broadcasted_iota(I32, (C, HEAD_W), 1)
    bcol_all = _chunk_cumsum(la, C)
    heads_per_group = SSM_HEADS // SSM_GROUPS
    for s in range(n_seq):
        states = [st[s, m] for m in range(n_pairs)]
        for c in range(n_chunk):
            r0 = (s * n_chunk + c) * C
            rs = slice(r0, r0 + C)
            bcol = bcol_all[rs, :]
            brow = functools.reduce(lambda p, q: p + q, [
                lax.dot_general(part, tri, (((0,), (1,)), ((), ())), preferred_element_type=F32)
                for part in _split3(la[rs, :])])
            xs_c = xbc[rs, 0:SSM_INNER]
            v_c = (xs_c * dtx[rs, :]).astype(BF16)
            gmats, bms, cms = [], [], []
            for grp in range(SSM_GROUPS):
                bm = xbc[rs, SSM_INNER + grp * SSM_STATE:SSM_INNER + (grp + 1) * SSM_STATE]
                cm = xbc[rs, SSM_INNER + (SSM_GROUPS + grp) * SSM_STATE:SSM_INNER + (SSM_GROUPS + grp + 1) * SSM_STATE]
                cm_hi, cm_lo = _split2(cm)
                bm_hi, bm_lo = _split2(bm)
                gmats.append(_dot_nt(jnp.concatenate([cm_hi, cm_hi, cm_lo], axis=1),
                                     jnp.concatenate([bm_hi, bm_lo, bm_hi], axis=1)))
                bms.append(bm)
                cms.append(cm)
            for m in range(n_pairs):
                grp = (2 * m) // heads_per_group
                bm, cm, gmat = bms[grp], cms[grp], gmats[grp]
                ps = slice(m * HEAD_W, (m + 1) * HEAD_W)
                vp = v_c[:, ps]
                s_t = states[m]
                scores, queries, keys, decays = [], [], [], []
                for hh in range(2):
                    hd = 2 * m + hh
                    bc = bcol[:, hd:hd + 1]
                    br = brow[hd:hd + 1, :]
                    b_last = bcol[C - 1:C, hd:hd + 1]
                    scores.append(gmat * jnp.where(causal, jnp.exp(jnp.minimum(bc - br, 0.0)), 0.0))
                    queries.append(cm * jnp.exp(bc))
                    keys.append(bm * jnp.exp(b_last - bc))
                    decays.append(jnp.exp(b_last))
                o_stack = (_dot(jnp.concatenate(scores, axis=0), vp)
                           + _dot_nt(jnp.concatenate(queries, axis=0), s_t))
                kv = _dot_tn(vp, jnp.concatenate(keys, axis=1))
                d = SSM_HEADDIM
                states[m] = jnp.concatenate([s_t[0:d, :] * decays[0] + kv[0:d, 0:SSM_STATE],
                                             s_t[d:, :] * decays[1] + kv[d:, SSM_STATE:]], axis=0)
                o_pair = jnp.where(lane < SSM_HEADDIM, o_stack[0:C, :], o_stack[C:, :])
                y_scr[rs, ps] = o_pair + dvec_ref[:, ps] * xs_c[:, ps]
        for m in range(n_pairs):
            st[s, m] = states[m]
    y = y_scr[...] * _silu(hz_ref[...])
    gw = SSM_INNER // SSM_GROUPS
    for grp in range(SSM_GROUPS):
        gs = slice(grp * gw, (grp + 1) * gw)
        yg = y[:, gs]
        ms = jnp.mean(yg * yg, axis=-1, keepdims=True)
        o_ref[:, gs] = (yg * lax.rsqrt(ms + RMS_EPS) * nw_ref[:, gs]).astype(BF16)

    @pl.when(tstep == pl.num_programs(1) - 1)
    def _():
        for s in range(n_seq):
            co_ref[s] = buf[s, pad:8, :]
            for m in range(n_pairs):
                so_ref[s, m] = st[s, m]


def _ssd_call(h, col_z, col_x, col_dt, cache, s0, cw, cb, dtb, alog, dvec, nw, out_buf, n_batch, seq_len, row_off,
              sample):
    n_seq, n_chunk, C, rows, grid, rb = _seq_layout(n_batch, seq_len, row_off, sample, SSD_PROMPT_TILE)
    L = rows // n_seq
    hist = SSM_CONV - 1
    n_pairs = SSM_HEADS // 2
    colspec = lambda c0, w: pl.BlockSpec((rows, w), lambda i, t: (rb(i, t), c0 // w))
    fix2 = lambda i, t: (0, 0)
    c_spec = pl.BlockSpec((n_seq, hist, SSM_CONV_DIM), lambda i, t: (i, 0, 0))
    st_spec = pl.BlockSpec((n_seq, n_pairs, HEAD_W, SSM_STATE), lambda i, t: (i, 0, 0, 0))
    return pl.pallas_call(
        functools.partial(_ssd_body, n_seq, n_chunk, C, sample, True),
        grid=grid,
        in_specs=[colspec(col_z, 512), colspec(col_x, SSM_CONV_DIM), colspec(col_dt, LANE), c_spec, st_spec,
                  pl.BlockSpec((SSM_CONV, SSM_CONV_DIM), fix2), pl.BlockSpec((1, SSM_CONV_DIM), fix2),
                  pl.BlockSpec((1, LANE), fix2), pl.BlockSpec((1, LANE), fix2),
                  pl.BlockSpec((1, SSM_INNER), fix2), pl.BlockSpec((1, SSM_INNER), fix2),
                  pl.BlockSpec(memory_space=pl.ANY)],
        out_specs=[pl.BlockSpec((rows, 512), lambda i, t: (rb(i, t), 0)), c_spec, st_spec],
        out_shape=[jax.ShapeDtypeStruct(out_buf.shape, out_buf.dtype),
                   jax.ShapeDtypeStruct((n_batch, hist, SSM_CONV_DIM), F32),
                   jax.ShapeDtypeStruct((n_batch, n_pairs, HEAD_W, SSM_STATE), F32)],
        scratch_shapes=[pltpu.VMEM((n_seq, n_pairs, HEAD_W, SSM_STATE), F32),
                        pltpu.VMEM((n_seq, 8 + L, SSM_CONV_DIM), F32),
                        pltpu.VMEM((n_seq, 8 + L, SSM_CONV_DIM), F32),
                        pltpu.VMEM((rows, SSM_CONV_DIM), F32),
                        pltpu.VMEM((rows, SSM_INNER), F32)],
        input_output_aliases={11: 0},
        compiler_params=_cparams(2),
        name="ssd" + ("_sample" if sample else "_prompt"),
    )(h, h, h, cache, s0, cw, cb, dtb, alog, dvec, nw, out_buf)


def _pad_heads(w, n_heads, width):
    lead = w.shape[:-1]
    w = w.reshape(lead + (n_heads, width))
    w = jnp.pad(w, [(0, 0)] * len(lead) + [(0, 0), (0, HEAD_W - width)])
    return w.reshape(lead + (n_heads * HEAD_W,))


def _row(v):
    return v.reshape(1, -1).astype(F32)


def kernel(x_prompt, x_sample, state_gla, cache_conformer, state_hgrn, state_ssm, cache_mamba_conv, w_in_even, w_gla_gate_lr, b_gla_gate, gla_norm_w, conf_conv_w, conf_conv_b, conf_ln_g, conf_ln_b, w_out_even, w_in_odd, hgrn_lower_bounds, hgrn_norm_w, mamba_conv_w, mamba_conv_b, mamba_dt_bias, mamba_a_log, mamba_d, mamba_norm_w, w_out_odd, ln1_g, ln1_b, ln2_g, ln2_b, router_w, router_b, expert_w_gate, expert_b_gate, expert_w_up, expert_b_up, expert_w_down, expert_b_down):
    bp, lp, _ = x_prompt.shape
    bs, ls, _ = x_sample.shape
    tp, ts = bp * lp, bs * ls
    x = (x_prompt.reshape(tp, D_MODEL), x_sample.reshape(ts, D_MODEL))

    def router_params(layer):
        return router_w[layer].T.astype(BF16), router_b[layer].astype(F32).reshape(N_EXPERTS, 1)

    def finish_layer(layer, x, mix_a, mix_b, w_out, t_split=None):
        rwt, rb = router_params(layer)
        x1, xp, gates, eidx, rank, counts = _outproj_ln_router(
            mix_a, mix_b, x, w_out[:512].astype(BF16), w_out[512:].astype(BF16),
            _row(ln1_g[layer]), _row(ln1_b[layer]), rwt, rb)
        return _moe(layer, x1, xp, gates, eidx, rank, counts, _row(ln2_g[layer]), _row(ln2_b[layer]),
                    expert_w_gate, expert_b_gate, expert_w_up, expert_b_up, expert_w_down, expert_b_down, tp, t_split)

    mix_init = jnp.zeros((tp + ts, 512), BF16)

    wi = w_in_even[0]
    wq, wk, wv, wg, wlr, wglu = jnp.split(wi, [256, 512, 1024, 1536, 1552], axis=1)
    w_even = jnp.concatenate([_pad_heads(wq, GLA_HEADS, GLA_DK), _pad_heads(wk, GLA_HEADS, GLA_DK), wv, wg, wglu,
                              jnp.pad(wlr, ((0, 0), (0, LANE - GLA_RANK)))], axis=1).astype(BF16)
    cols_gla = (0, 512, 1024, 1536, 3072)
    col_a, col_gate = 2048, 2560
    h = _inproj(x, w_even)
    wlr_p = jnp.pad(_pad_heads(w_gla_gate_lr[0], GLA_HEADS, GLA_DK), ((0, LANE - GLA_RANK), (0, 0)))
    blr_p = _row(_pad_heads(b_gla_gate[0], GLA_HEADS, GLA_DK))
    nw = _row(gla_norm_w[0])
    conf_args = (conf_conv_w[0], _row(conf_conv_b[0]), _row(conf_ln_g[0]), _row(conf_ln_b[0]))
    s0_p = jnp.zeros((bp, GLA_HEADS, GLA_DK, HEAD_W), F32)
    s0_s = state_gla[0]
    mix_a, sg_p = _gla_call("gla", h, cols_gla, (wlr_p, blr_p), nw, s0_p, mix_init, bp, lp, 0, False)
    mix_a, sg_s = _gla_call("gla", h, cols_gla, (wlr_p, blr_p), nw, s0_s, mix_a, bs, ls, tp, True)
    mix_b, cc_p = _conf_call(h, col_a, col_gate, jnp.zeros((bp,) + cache_conformer.shape[2:], F32), *conf_args,
                             mix_init, bp, lp, 0, False)
    mix_b, cc_s = _conf_call(h, col_a, col_gate, cache_conformer[0], *conf_args, mix_b, bs, ls, tp, True)
    x = finish_layer(0, x, mix_a, mix_b, w_out_even[0], t_split=tp // 2)
    gla_p, gla_s = sg_p[None], sg_s[None]
    conf_p, conf_s = cc_p[None], cc_s[None]

    lb_cum = jnp.cumsum(jax.nn.softmax(hgrn_lower_bounds.astype(F32), axis=0), axis=0)
    lower_bound = _row((lb_cum - lb_cum[0])[1])
    wo = w_in_odd[0]
    w_odd = jnp.concatenate([wo[:, 2560:3584], wo[:, :2560],
                             jnp.pad(wo[:, 3584:], ((0, 0), (0, LANE - SSM_HEADS)))], axis=1).astype(BF16)
    h = _inproj(x, w_odd)
    cols_hgrn = (1024, 1536, 2048, 2560)
    col_z, col_x, col_dt = 3072, 0, 3584
    nw = _row(hgrn_norm_w[0])
    mix_a, sh_p = _gla_call("hgrn", h, cols_hgrn, lower_bound, nw,
                            jnp.zeros((bp, HGRN_HEADS, HEAD_W, HEAD_W), F32), mix_init, bp, lp, 0, False)
    mix_a, sh_s = _gla_call("hgrn", h, cols_hgrn, lower_bound, nw, state_hgrn[0], mix_a, bs, ls, tp, True)

    def pair_states(s):
        return jnp.swapaxes(s, 2, 3).reshape(s.shape[0], SSM_HEADS // 2, HEAD_W, SSM_STATE)

    def unpair_states(s):
        return jnp.swapaxes(s.reshape(s.shape[0], SSM_HEADS, SSM_HEADDIM, SSM_STATE), 2, 3)

    pad8 = lambda v: jnp.pad(v.astype(F32), (0, LANE - SSM_HEADS)).reshape(1, LANE)
    ssd_args = (mamba_conv_w[0], _row(mamba_conv_b[0]), pad8(mamba_dt_bias[0]), pad8(mamba_a_log[0]),
                _row(jnp.repeat(mamba_d[0], SSM_HEADDIM)), _row(mamba_norm_w[0]))
    mix_b, cm_p, ss_p = _ssd_call(h, col_z, col_x, col_dt, jnp.zeros((bp,) + cache_mamba_conv.shape[2:], F32),
                                  jnp.zeros((bp, SSM_HEADS // 2, HEAD_W, SSM_STATE), F32), *ssd_args,
                                  mix_init, bp, lp, 0, False)
    mix_b, cm_s, ss_s = _ssd_call(h, col_z, col_x, col_dt, cache_mamba_conv[0], pair_states(state_ssm[0]),
                                  *ssd_args, mix_b, bs, ls, tp, True)
    y_prompt, y_sample = finish_layer(1, x, mix_a, mix_b, w_out_odd[0])
    y_prompt = y_prompt.reshape(bp, lp, D_MODEL)
    y_sample = y_sample.reshape(bs, ls, D_MODEL)
    return (y_prompt, y_sample, gla_p, gla_s, conf_p, conf_s, sh_p[None], sh_s[None],
            unpair_states(ss_p)[None], unpair_states(ss_s)[None], cm_p[None], cm_s[None])
```

```python
import functools

import jax
import jax.numpy as jnp
from jax import lax
from jax.experimental import pallas as pl
from jax.experimental.pallas import tpu as pltpu
from jax.experimental.pallas import tpu_sc as plsc

F32 = jnp.float32
BF16 = jnp.bfloat16
I32 = jnp.int32
U32 = jnp.uint32

D_MODEL = 1024
DEPTH = 2
DEEPNORM_ALPHA = (2.0 * DEPTH) ** 0.25
LN_EPS = 1e-5
RMS_EPS = 1e-6
LANE = 128
HEAD_W = 128
GLA_HEADS, GLA_DK, GLA_RANK, GLA_TAU = 4, 64, 16, 16.0
CONF_DIM, CONF_WIDTH = 512, 31
HGRN_HEADS, HGRN_DK = 4, 128
SSM_HEADS, SSM_HEADDIM, SSM_STATE, SSM_GROUPS, SSM_CONV = 8, 64, 128, 2, 4
SSM_INNER = SSM_HEADS * SSM_HEADDIM
SSM_CONV_DIM = SSM_INNER + 2 * SSM_GROUPS * SSM_STATE
N_EXPERTS, TOP_K = 32, 4
SWIGLU_ALPHA, SWIGLU_LIMIT = 1.702, 7.0
SCAN_CHUNK = 64
PROMPT_TILE = 512
SSD_PROMPT_TILE = 512
SAMPLE_SEQS = 16
TOKEN_TILE = 512
WIDE_TOKEN_TILE = 1024
MOE_ROWS = 512
IN_SLOTS = 4
SC_CORES, SC_SUBCORES = 2, 16
SC_ROWS = 64
SC_SCATTER_ROWS = 32
VMEM_LIMIT = 56 * 1024 * 1024


def _cparams(n_axes):
    return pltpu.CompilerParams(dimension_semantics=("arbitrary",) * n_axes, vmem_limit_bytes=VMEM_LIMIT)


def _silu(x):
    return x * jax.nn.sigmoid(x)


def _softplus(x):
    return jnp.maximum(x, 0.0) + jnp.log(1.0 + jnp.exp(-jnp.abs(x)))


def _log_sigmoid(x):
    return jnp.minimum(x, 0.0) - jnp.log(1.0 + jnp.exp(-jnp.abs(x)))


def _layernorm(y, g, b):
    mu = jnp.mean(y, axis=-1, keepdims=True)
    d = y - mu
    var = jnp.mean(d * d, axis=-1, keepdims=True)
    return d * lax.rsqrt(var + LN_EPS) * g + b


def _dot(a, b):
    return jnp.dot(a.astype(BF16), b.astype(BF16), preferred_element_type=F32)


def _dot_nt(a, b):
    return lax.dot_general(a.astype(BF16), b.astype(BF16), (((1,), (1,)), ((), ())), preferred_element_type=F32)


def _dot_tn(a, b):
    return lax.dot_general(a.astype(BF16), b.astype(BF16), (((0,), (0,)), ((), ())), preferred_element_type=F32)


def _tri(c):
    r = lax.broadcasted_iota(I32, (c, c), 0)
    k = lax.broadcasted_iota(I32, (c, c), 1)
    return r >= k


def _pair_specs(tm, n_first, width):
    return [pl.BlockSpec((tm, width), lambda i: (jnp.minimum(i, n_first - 1), 0)),
            pl.BlockSpec((tm, width), lambda i: (jnp.maximum(i - n_first, 0), 0))]


def _pair_tile(n_first, xa_ref, xb_ref):
    return jnp.where(pl.program_id(0) < n_first, xa_ref[...], xb_ref[...])


def _inproj_body(n_first, xa_ref, xb_ref, w_ref, o_ref):
    xb = _pair_tile(n_first, xa_ref, xb_ref).astype(BF16)
    n = w_ref.shape[1]
    for c0 in range(0, n, 512):
        c1 = min(c0 + 512, n)
        o_ref[:, c0:c1] = jnp.dot(xb, w_ref[:, c0:c1], preferred_element_type=F32)


def _inproj(xa, xb, w):
    k, n = w.shape
    t = xa.shape[0] + xb.shape[0]
    n_first = xa.shape[0] // TOKEN_TILE
    return pl.pallas_call(
        functools.partial(_inproj_body, n_first),
        grid=(t // TOKEN_TILE,),
        in_specs=_pair_specs(TOKEN_TILE, n_first, k) + [pl.BlockSpec((k, n), lambda i: (0, 0))],
        out_specs=pl.BlockSpec((TOKEN_TILE, n), lambda i: (i, 0)),
        out_shape=jax.ShapeDtypeStruct((t, n), F32),
        compiler_params=_cparams(1),
        name="inproj",
    )(xa, xb, w)


def _pack_halves(y):
    half = y.shape[1] // 2
    hi = lax.bitcast_convert_type(y[:, :half].astype(BF16).astype(F32), U32)
    lo = lax.bitcast_convert_type(y[:, half:].astype(BF16).astype(F32), U32)
    return (hi & jnp.uint32(0xFFFF0000)) | (lo >> 16)


def _unpack_halves(w):
    hi = lax.bitcast_convert_type(w & jnp.uint32(0xFFFF0000), F32)
    lo = lax.bitcast_convert_type(w << 16, F32)
    return hi, lo


def _outproj_body(n_first, a_ref, b_ref, xa_ref, xb_ref, wa_ref, wb_ref, g_ref, be_ref, rwt_ref, rb_ref,
                  x1_ref, xp_ref, gate_ref, idx_ref, rank_ref, cnt_ref, carry):
    @pl.when(pl.program_id(0) == 0)
    def _():
        carry[...] = jnp.zeros(carry.shape, F32)

    mix = (jnp.dot(a_ref[...], wa_ref[...], preferred_element_type=F32)
           + jnp.dot(b_ref[...], wb_ref[...], preferred_element_type=F32))
    x1 = _layernorm(DEEPNORM_ALPHA * _pair_tile(n_first, xa_ref, xb_ref) + mix, g_ref[...], be_ref[...])
    x1_ref[...] = x1
    xp_ref[...] = _pack_halves(x1)
    logits = _dot_nt(rwt_ref[...], x1) + rb_ref[...]
    tm = logits.shape[1]
    expert = lax.broadcasted_iota(I32, logits.shape, 0)
    vals, idxs = [], []
    for _ in range(TOP_K):
        m = jnp.max(logits, axis=0, keepdims=True)
        sel = jnp.min(jnp.where(logits == m, expert, N_EXPERTS), axis=0, keepdims=True)
        vals.append(m)
        idxs.append(sel)
        logits = jnp.where(expert == sel, -jnp.inf, logits)
    exps = [jnp.exp(v - vals[0]) for v in vals]
    inv = 1.0 / functools.reduce(lambda p, q: p + q, exps)
    chosen = jnp.zeros(logits.shape, F32)
    for k in range(TOP_K):
        chosen = chosen + (expert == idxs[k]).astype(F32)
    earlier = lax.broadcasted_iota(I32, (tm, tm), 0) < lax.broadcasted_iota(I32, (tm, tm), 1)
    before = carry[...] + jnp.dot(chosen.astype(BF16), earlier.astype(BF16), preferred_element_type=F32)
    choice = lax.broadcasted_iota(I32, (8, tm), 0)
    gates = jnp.zeros((8, tm), F32)
    eidx = jnp.zeros((8, tm), I32)
    ranks = jnp.zeros((8, tm), F32)
    for k in range(TOP_K):
        rk = jnp.sum(jnp.where(expert == idxs[k], before, 0.0), axis=0, keepdims=True)
        gates = jnp.where(choice == k, exps[k] * inv, gates)
        eidx = jnp.where(choice == k, idxs[k], eidx)
        ranks = jnp.where(choice == k, rk, ranks)
    gate_ref[...] = gates
    idx_ref[...] = eidx
    rank_ref[...] = ranks.astype(I32)
    carry[...] = carry[...] + jnp.sum(chosen, axis=1, keepdims=True)
    cnt_ref[...] = carry[...].astype(I32)


def _outproj_ln_router(a, b, xa, xb, wa, wb, g, be, rwt, rb):
    t = xa.shape[0] + xb.shape[0]
    tm = WIDE_TOKEN_TILE
    n_first = xa.shape[0] // tm
    row = lambda i: (i, 0)
    col = lambda i: (0, i)
    fix = lambda i: (0, 0)
    return pl.pallas_call(
        functools.partial(_outproj_body, n_first),
        grid=(t // tm,),
        in_specs=[pl.BlockSpec((tm, 512), row), pl.BlockSpec((tm, 512), row)] + _pair_specs(tm, n_first, D_MODEL)
        + [pl.BlockSpec((512, D_MODEL), fix), pl.BlockSpec((512, D_MODEL), fix),
           pl.BlockSpec((1, D_MODEL), fix), pl.BlockSpec((1, D_MODEL), fix),
           pl.BlockSpec((N_EXPERTS, D_MODEL), fix), pl.BlockSpec((N_EXPERTS, 1), fix)],
        out_specs=[pl.BlockSpec((tm, D_MODEL), row), pl.BlockSpec((tm, 512), row),
                   pl.BlockSpec((8, tm), col), pl.BlockSpec((8, tm), col), pl.BlockSpec((8, tm), col),
                   pl.BlockSpec((N_EXPERTS, 1), fix)],
        out_shape=[jax.ShapeDtypeStruct((t, D_MODEL), F32), jax.ShapeDtypeStruct((t, 512), U32),
                   jax.ShapeDtypeStruct((8, t), F32), jax.ShapeDtypeStruct((8, t), I32),
                   jax.ShapeDtypeStruct((8, t), I32), jax.ShapeDtypeStruct((N_EXPERTS, 1), I32)],
        scratch_shapes=[pltpu.VMEM((N_EXPERTS, 1), F32)],
        compiler_params=_cparams(1),
        name="outproj_ln_router",
    )(a, b, xa, xb, wa, wb, g, be, rwt, rb)


def _sc_mesh():
    return plsc.VectorSubcoreMesh(core_axis_name="c", subcore_axis_name="s")


def _sc_scatter_rows(src, dest, n_out):
    n_src, w = src.shape
    n_dst = dest.shape[1]
    workers = SC_CORES * SC_SUBCORES
    per_worker = n_src // workers
    chunks = per_worker // SC_SCATTER_ROWS
    assert n_src == workers * chunks * SC_SCATTER_ROWS
    idx = dest.T.reshape(n_dst, workers, chunks, SC_SCATTER_ROWS)

    @functools.partial(pl.kernel, mesh=_sc_mesh(), out_type=jax.ShapeDtypeStruct((n_out, w), src.dtype),
                       scratch_types=[pltpu.VMEM((n_dst, chunks, SC_SCATTER_ROWS), I32)]
                       + [pltpu.VMEM((SC_SCATTER_ROWS, w), src.dtype)] * 2 + [pltpu.SemaphoreType.DMA] * 4)
    def scatter(src_hbm, idx_hbm, out_hbm, idx_v, rows_a, rows_b, sem_ra, sem_rb, sem_wa, sem_wb):
        worker = lax.axis_index("s") * SC_CORES + lax.axis_index("c")
        base = worker * per_worker
        for k in range(n_dst):
            pltpu.sync_copy(idx_hbm.at[k, worker], idx_v.at[k])

        def read(c, rows, sem):
            return pltpu.async_copy(src_hbm.at[pl.ds(pl.multiple_of(base + c * SC_SCATTER_ROWS, 8), SC_SCATTER_ROWS)],
                                    rows, sem)

        def write_all(pending_read, c, rows, sem):
            pending_read.wait()
            return [pltpu.async_copy(rows, out_hbm.at[idx_v.at[k, c]], sem) for k in range(n_dst)]

        @pl.loop(0, chunks // 2)
        def _(p):
            read_a = read(2 * p, rows_a, sem_ra)
            read_b = read(2 * p + 1, rows_b, sem_rb)
            writes = write_all(read_a, 2 * p, rows_a, sem_wa) + write_all(read_b, 2 * p + 1, rows_b, sem_wb)
            for wr in writes:
                wr.wait()

        if chunks % 2:
            for wr in write_all(read(chunks - 1, rows_a, sem_ra), chunks - 1, rows_a, sem_wa):
                wr.wait()

    return scatter(src, idx)


def _sc_gather_rows(table, idx):
    n, w = idx.shape[0], table.shape[1]
    workers = SC_CORES * SC_SUBCORES
    chunks = n // (workers * SC_ROWS)
    assert n == workers * chunks * SC_ROWS and chunks % 2 == 0
    idx = idx.reshape(workers, chunks, SC_ROWS)

    @functools.partial(pl.kernel, mesh=_sc_mesh(), out_type=jax.ShapeDtypeStruct((n, w), table.dtype),
                       scratch_types=[pltpu.VMEM((chunks, SC_ROWS), I32)] + [pltpu.VMEM((SC_ROWS, w), table.dtype)] * 2
                       + [pltpu.SemaphoreType.DMA] * 4)
    def gather(table_hbm, idx_hbm, out_hbm, idx_v, rows_a, rows_b, sem_ra, sem_rb, sem_wa, sem_wb):
        worker = lax.axis_index("s") * SC_CORES + lax.axis_index("c")
        base = worker * (chunks * SC_ROWS)
        pltpu.sync_copy(idx_hbm.at[worker], idx_v)

        def out_rows(c):
            return out_hbm.at[pl.ds(pl.multiple_of(base + c * SC_ROWS, 8), SC_ROWS)]

        @pl.loop(0, chunks // 2)
        def _(p):
            read_a = pltpu.async_copy(table_hbm.at[idx_v.at[2 * p]], rows_a, sem_ra)
            read_b = pltpu.async_copy(table_hbm.at[idx_v.at[2 * p + 1]], rows_b, sem_rb)
            read_a.wait()
            write_a = pltpu.async_copy(rows_a, out_rows(2 * p), sem_wa)
            read_b.wait()
            write_b = pltpu.async_copy(rows_b, out_rows(2 * p + 1), sem_wb)
            write_a.wait()
            write_b.wait()

    return gather(table, idx)


def _experts_body(b0_ref, nb_ref, last_ref, nt_ref, xs_hbm, wg_ref, bg_ref, wu_ref, bu_ref, wd_ref, bd_ref, o_hbm,
                  wg_s, wu_s, wd_s, xbuf, obuf, sem_in, sem_out):
    e = pl.program_id(0)
    first_blk, n_blk, last_valid, n_total = b0_ref[e], nb_ref[e], last_ref[e], nt_ref[0]
    quarter = MOE_ROWS // 4

    def rows_of(g):
        return pl.ds(pl.multiple_of(g * MOE_ROWS, MOE_ROWS), MOE_ROWS)

    def fetch(g, slot):
        return pltpu.make_async_copy(xs_hbm.at[rows_of(g)], xbuf.at[slot], sem_in.at[slot])

    def put(g, slot):
        return pltpu.make_async_copy(obuf.at[slot], o_hbm.at[rows_of(g)], sem_out.at[slot])

    lead = IN_SLOTS - 1
    for first in range(lead):
        @pl.when((e == 0) & (n_total > first))
        def _():
            fetch(first, first).start()

    @pl.when(n_blk > 0)
    def _():
        wg_s[...] = wg_ref[...].astype(BF16)
        wu_s[...] = wu_ref[...].astype(BF16)
        wd_s[...] = wd_ref[...].astype(BF16)

    def compute(islot, slot, rows):
        half = D_MODEL // 2
        x_hi, x_lo = _unpack_halves(xbuf[islot, 0:rows, :])
        x_hi = x_hi.astype(BF16)
        x_lo = x_lo.astype(BF16)
        g = (jnp.dot(x_hi, wg_s[:half, :], preferred_element_type=F32)
             + jnp.dot(x_lo, wg_s[half:, :], preferred_element_type=F32) + bg_ref[...])
        u = (jnp.dot(x_hi, wu_s[:half, :], preferred_element_type=F32)
             + jnp.dot(x_lo, wu_s[half:, :], preferred_element_type=F32) + bu_ref[...])
        g = jnp.minimum(g, SWIGLU_LIMIT)
        u = jnp.clip(u, -SWIGLU_LIMIT, SWIGLU_LIMIT)
        hmid = (u + 1.0) * (g * jax.nn.sigmoid(SWIGLU_ALPHA * g))
        out = jnp.dot(hmid.astype(BF16), wd_s[...], preferred_element_type=F32) + bd_ref[...]
        obuf[slot, 0:rows, :] = _pack_halves(out)

    def block(j, carry):
        g = first_blk + j
        slot = lax.rem(g, 2)
        islot = lax.rem(g, IN_SLOTS)
        fetch(g, islot).wait()

        @pl.when(g + lead < n_total)
        def _():
            fetch(g + lead, lax.rem(g + lead, IN_SLOTS)).start()

        @pl.when(g >= 2)
        def _():
            put(g - 2, slot).wait()

        valid = jnp.where(j == n_blk - 1, last_valid, MOE_ROWS)

        for rows in range(quarter, MOE_ROWS + 1, quarter):
            @pl.when((valid > rows - quarter) & (valid <= rows))
            def _():
                compute(islot, slot, rows)
                if rows < MOE_ROWS:
                    obuf[slot, rows:, :] = jnp.zeros((MOE_ROWS - rows, obuf.shape[2]), obuf.dtype)

        put(g, slot).start()
        return carry

    lax.fori_loop(0, n_blk, block, 0)

    @pl.when((e == N_EXPERTS - 1) & (n_total >= 2))
    def _():
        put(n_total - 2, lax.rem(n_total, 2)).wait()

    @pl.when((e == N_EXPERTS - 1) & (n_total >= 1))
    def _():
        put(n_total - 1, lax.rem(n_total - 1, 2)).wait()


def _experts(layer, first_blk, n_blk, last_valid, xs, wg, bg, wu, bu, wd, bd):
    n_rows, w = xs.shape
    wsel = lambda e, b0, nb, lv, nt: (layer, e, 0, 0)
    wspec = pl.BlockSpec((None, None, D_MODEL, D_MODEL), wsel)
    bspec = pl.BlockSpec((None, None, 1, D_MODEL), wsel)
    bias = lambda b: b.reshape(b.shape[0], b.shape[1], 1, b.shape[2])
    return pl.pallas_call(
        _experts_body,
        grid_spec=pltpu.PrefetchScalarGridSpec(
            num_scalar_prefetch=4,
            grid=(N_EXPERTS,),
            in_specs=[pl.BlockSpec(memory_space=pl.ANY), wspec, bspec, wspec, bspec, wspec, bspec],
            out_specs=pl.BlockSpec(memory_space=pl.ANY),
            scratch_shapes=[pltpu.VMEM((D_MODEL, D_MODEL), BF16)] * 3
            + [pltpu.VMEM((IN_SLOTS, MOE_ROWS, w), U32), pltpu.VMEM((2, MOE_ROWS, w), U32),
               pltpu.SemaphoreType.DMA((IN_SLOTS,)), pltpu.SemaphoreType.DMA((2,))],
        ),
        out_shape=jax.ShapeDtypeStruct((n_rows, w), U32),
        compiler_params=_cparams(1),
        name="experts",
    )(first_blk, n_blk, last_valid, jnp.sum(n_blk).reshape(1), xs, wg, bias(bg), wu, bias(bu), wd, bias(bd))


def _combine_body(n_first, o0_ref, o1_ref, o2_ref, o3_ref, gt_ref, x_ref, g_ref, b_ref, ya_ref, yb_ref=None):
    half = D_MODEL // 2
    gates = gt_ref[...]
    hi = jnp.zeros((x_ref.shape[0], half), F32)
    lo = jnp.zeros((x_ref.shape[0], half), F32)
    for k, o_ref in enumerate((o0_ref, o1_ref, o2_ref, o3_ref)):
        h, l = _unpack_halves(o_ref[...])
        gk = gates[:, k:k + 1]
        hi = hi + gk * h
        lo = lo + gk * l
    x = x_ref[...]
    y_hi = DEEPNORM_ALPHA * x[:, :half] + hi
    y_lo = DEEPNORM_ALPHA * x[:, half:] + lo
    mu = (jnp.sum(y_hi, axis=-1, keepdims=True) + jnp.sum(y_lo, axis=-1, keepdims=True)) * (1.0 / D_MODEL)
    d_hi = y_hi - mu
    d_lo = y_lo - mu
    var = (jnp.sum(d_hi * d_hi, axis=-1, keepdims=True) + jnp.sum(d_lo * d_lo, axis=-1, keepdims=True)) * (1.0 / D_MODEL)
    r = lax.rsqrt(var + LN_EPS)
    out_hi = d_hi * r * g_ref[:, :half] + b_ref[:, :half]
    out_lo = d_lo * r * g_ref[:, half:] + b_ref[:, half:]

    def write(y_ref):
        y_ref[:, :half] = out_hi
        y_ref[:, half:] = out_lo

    if yb_ref is None:
        write(ya_ref)
    else:
        pl.when(pl.program_id(0) < n_first)(lambda: write(ya_ref))
        pl.when(pl.program_id(0) >= n_first)(lambda: write(yb_ref))


def _combine_ln(o4, gates, x, g, b, t_first=None):
    t = x.shape[0]
    tm = WIDE_TOKEN_TILE
    row = lambda i: (i, 0)
    fix = lambda i: (0, 0)
    choice = lambda k: pl.BlockSpec((tm, 512), lambda i: (k * (t // tm) + i, 0))
    if t_first is None:
        n_first = None
        out_specs = pl.BlockSpec((tm, D_MODEL), row)
        out_shape = jax.ShapeDtypeStruct((t, D_MODEL), F32)
    else:
        n_first = t_first // tm
        out_specs = [pl.BlockSpec((tm, D_MODEL), lambda i: (jnp.minimum(i, n_first - 1), 0)),
                     pl.BlockSpec((tm, D_MODEL), lambda i: (jnp.maximum(i - n_first, 0), 0))]
        out_shape = [jax.ShapeDtypeStruct((t_first, D_MODEL), F32), jax.ShapeDtypeStruct((t - t_first, D_MODEL), F32)]
    return pl.pallas_call(
        functools.partial(_combine_body, n_first),
        grid=(t // tm,),
        in_specs=[choice(0), choice(1), choice(2), choice(3), pl.BlockSpec((tm, TOP_K), row),
                  pl.BlockSpec((tm, D_MODEL), row), pl.BlockSpec((1, D_MODEL), fix), pl.BlockSpec((1, D_MODEL), fix)],
        out_specs=out_specs,
        out_shape=out_shape,
        compiler_params=_cparams(1),
        name="combine_ln",
    )(o4, o4, o4, o4, gates, x, g, b)


def _moe(layer, x1, xp, gates, eidx, rank, counts, ln_g, ln_b, wg, bg, wu, bu, wd, bd, t_first=None):
    t = x1.shape[0]
    bm = MOE_ROWS
    n_blocks = t * TOP_K // bm + N_EXPERTS
    n_rows = n_blocks * bm
    cnt = counts[:, 0]
    padded = (cnt + bm - 1) // bm * bm
    pad_end = jnp.cumsum(padded)
    pad_start = pad_end - padded
    e = eidx[:TOP_K]
    start = jnp.sum(jnp.where(e[:, :, None] == jnp.arange(N_EXPERTS, dtype=I32), pad_start, 0), axis=-1)
    dest = (start + rank[:TOP_K]).T
    n_blk = padded // bm
    last_valid = cnt - (n_blk - 1) * bm
    xs = _sc_scatter_rows(xp, dest, n_rows)
    outs = _experts(layer, pad_start // bm, n_blk, last_valid, xs, wg, bg, wu, bu, wd, bd)
    o4 = _sc_gather_rows(outs, dest.T.reshape(-1))
    return _combine_ln(o4, gates[:TOP_K].T, x1, ln_g, ln_b, t_first)


def _split2(x):
    hi = x.astype(BF16)
    return hi, (x - hi.astype(F32)).astype(BF16)


def _split3(x):
    hi = x.astype(BF16)
    rem = x - hi.astype(F32)
    mid = rem.astype(BF16)
    return hi, mid, (rem - mid.astype(F32)).astype(BF16)


def _chunk_cumsum(g, C):
    rows = g.shape[0]
    r = lax.broadcasted_iota(I32, (rows, rows), 0)
    c = lax.broadcasted_iota(I32, (rows, rows), 1)
    tri = ((r >= c) & (r // C == c // C)).astype(BF16)
    hi, mid, lo = _split3(g)
    dot = lambda part: jnp.dot(tri, part, preferred_element_type=F32)
    return dot(hi) + dot(mid) + dot(lo)


def _gla_batched_step(q, k, v, g, C, n_seq, n_heads, state_of, o_scr):
    rows = n_seq * C
    wide = n_seq * HEAD_W
    mid = max(C // 2 - 1, 0)
    r = lax.broadcasted_iota(I32, (rows, rows), 0)
    c = lax.broadcasted_iota(I32, (rows, rows), 1)
    same = (r // C) == (c // C)
    causal = same & (r >= c)
    parts = _split3(g)
    summed = lambda mask: functools.reduce(lambda p, q_: p + q_, [
        jnp.dot(mask.astype(BF16), part, preferred_element_type=F32) for part in parts])
    b = summed(causal)
    b_mid = summed(same & ((c % C) <= mid))
    b_last = summed(same)
    qe_hi, qe_lo = _split2(q * jnp.exp(b - b_mid))
    ke_hi, ke_lo = _split2(k * jnp.exp(b_mid - b))
    q_state = (q * jnp.exp(b)).astype(BF16)
    k_state = (k * jnp.exp(b_last - b)).astype(BF16)
    decay_parts = _split3(jnp.exp(b_last))
    row_w = lax.broadcasted_iota(I32, (rows, wide), 0)
    blk_w = lax.broadcasted_iota(I32, (rows, wide), 1) // HEAD_W
    own = (row_w // C) == blk_w
    pick = (row_w == blk_w * C).astype(BF16)
    new_states = []
    for h in range(n_heads):
        cs = slice(h * HEAD_W, (h + 1) * HEAD_W)
        lhs = jnp.concatenate([qe_hi[:, cs], qe_hi[:, cs], qe_lo[:, cs]], axis=1)
        rhs = jnp.concatenate([ke_hi[:, cs], ke_lo[:, cs], ke_hi[:, cs]], axis=1)
        scores = jnp.where(causal, _dot_nt(lhs, rhs), 0.0)
        vh = v[:, cs].astype(BF16)
        s_cat = jnp.concatenate([state_of(s, h) for s in range(n_seq)], axis=1)
        o_full = _dot(q_state[:, cs], s_cat)
        o_state = jnp.concatenate([o_full[s * C:(s + 1) * C, s * HEAD_W:(s + 1) * HEAD_W] for s in range(n_seq)],
                                  axis=0)
        o = _dot(scores, vh) + o_state
        v_wide = jnp.where(own, jnp.concatenate([vh] * n_seq, axis=1), jnp.zeros((), BF16))
        kv = _dot_tn(k_state[:, cs], v_wide)
        decay = functools.reduce(lambda p, q_: p + q_, [
            lax.dot_general(part[:, cs], pick, (((0,), (0,)), ((), ())), preferred_element_type=F32)
            for part in decay_parts])
        new_states.append(s_cat * decay + kv)
        ms = jnp.mean(o * o, axis=-1, keepdims=True)
        o_scr[:, cs] = o * lax.rsqrt(ms + RMS_EPS)
    return new_states


def _gla_body(mode, n_seq, n_chunk, C, *refs):
    if mode == "gla":
        hq_ref, hk_ref, hv_ref, hg_ref, hlr_ref, wlr_ref, blr_ref, nw_ref, s0_ref, _, o_ref, so_ref, st, o_scr = refs
    else:
        hq_ref, hk_ref, hv_ref, hg_ref, lb_ref, nw_ref, s0_ref, _, o_ref, so_ref, st, o_scr = refs
    n_heads = 4
    n_keys = s0_ref.shape[2]
    tstep = pl.program_id(1)
    batched = n_chunk == 1 and n_seq > 1

    def padded_state(s, h):
        s_in = s0_ref[s, h]
        if n_keys < HEAD_W:
            s_in = jnp.concatenate([s_in, jnp.zeros((HEAD_W - n_keys, HEAD_W), F32)], axis=0)
        return s_in

    if not batched:
        @pl.when(tstep == 0)
        def _():
            for s in range(n_seq):
                for h in range(n_heads):
                    st[s, h] = padded_state(s, h).T

    if mode == "gla":
        q = hq_ref[...] * (GLA_DK ** -0.5)
        k = hk_ref[...]
        z = _dot(hlr_ref[...], wlr_ref[...]) + blr_ref[...]
        g = _log_sigmoid(z) * (1.0 / GLA_TAU)
    else:
        q = _silu(hq_ref[...]) * (HGRN_DK ** -0.5)
        lb = lb_ref[...]
        f = lb + (1.0 - lb) * jax.nn.sigmoid(hk_ref[...])
        k = 1.0 - f
        g = jnp.log(f)
    v = hv_ref[...]
    if batched:
        new_states = _gla_batched_step(q, k, v, g, C, n_seq, n_heads, padded_state, o_scr)
        for s in range(n_seq):
            for h in range(n_heads):
                so_ref[s, h] = new_states[h][0:n_keys, s * HEAD_W:(s + 1) * HEAD_W]
    else:
        causal = _tri(C)
        mid = max(C // 2 - 1, 0)
        b_all = _chunk_cumsum(g, C)
        for s in range(n_seq):
            states = [st[s, h] for h in range(n_heads)]
            for c in range(n_chunk):
                r0 = (s * n_chunk + c) * C
                rs = slice(r0, r0 + C)
                b, qc, kc = b_all[rs, :], q[rs, :], k[rs, :]
                b_last = b[C - 1:C, :]
                b_mid = b[mid:mid + 1, :]
                qe_hi, qe_lo = _split2(qc * jnp.exp(b - b_mid))
                ke_hi, ke_lo = _split2(kc * jnp.exp(b_mid - b))
                q_state = (qc * jnp.exp(b)).astype(BF16)
                k_state = (kc * jnp.exp(b_last - b)).astype(BF16)
                decay = jnp.exp(b_last)
                for h in range(n_heads):
                    cs = slice(h * HEAD_W, (h + 1) * HEAD_W)
                    lhs = jnp.concatenate([qe_hi[:, cs], qe_hi[:, cs], qe_lo[:, cs]], axis=1)
                    rhs = jnp.concatenate([ke_hi[:, cs], ke_lo[:, cs], ke_hi[:, cs]], axis=1)
                    scores = jnp.where(causal, _dot_nt(lhs, rhs), 0.0)
                    vh = v[rs, cs].astype(BF16)
                    o = _dot(scores, vh) + _dot_nt(q_state[:, cs], states[h])
                    states[h] = states[h] * decay[:, cs] + _dot_tn(vh, k_state[:, cs])
                    ms = jnp.mean(o * o, axis=-1, keepdims=True)
                    o_scr[rs, cs] = o * lax.rsqrt(ms + RMS_EPS)
            for h in range(n_heads):
                st[s, h] = states[h]
    o_ref[...] = (o_scr[...] * nw_ref[...] * _silu(hg_ref[...])).astype(BF16)

    if not batched:
        @pl.when(tstep == pl.num_programs(1) - 1)
        def _():
            for s in range(n_seq):
                for h in range(n_heads):
                    so_ref[s, h] = st[s, h].T[0:n_keys, :]


def _seq_layout(n_batch, seq_len, row_off, sample, prompt_tile=PROMPT_TILE):
    if sample:
        n_seq, n_chunk, C = SAMPLE_SEQS, 1, seq_len
        rows = n_seq * C
        grid = (n_batch // n_seq, 1)
        blk0 = row_off // rows
        rb = lambda i, t: blk0 + i
    else:
        n_seq, n_chunk, C = 1, prompt_tile // SCAN_CHUNK, SCAN_CHUNK
        rows = prompt_tile
        tiles = seq_len // rows
        grid = (n_batch, tiles)
        blk0 = row_off // rows
        rb = lambda i, t: blk0 + i * tiles + t
    return n_seq, n_chunk, C, rows, grid, rb


def _gla_call(mode, h, cols, extra, nw, s0, out_buf, n_batch, seq_len, row_off, sample):
    n_seq, n_chunk, C, rows, grid, rb = _seq_layout(n_batch, seq_len, row_off, sample)
    colspec = lambda c0, w: pl.BlockSpec((rows, w), lambda i, t: (rb(i, t), c0 // w))
    fix2 = lambda i, t: (0, 0)
    in_specs = [colspec(cols[0], 512), colspec(cols[1], 512), colspec(cols[2], 512), colspec(cols[3], 512)]
    args = [h, h, h, h]
    if mode == "gla":
        wlr, blr = extra
        in_specs += [colspec(cols[4], LANE), pl.BlockSpec((LANE, 512), fix2), pl.BlockSpec((1, 512), fix2)]
        args += [h, wlr, blr]
    else:
        in_specs += [pl.BlockSpec((1, 512), fix2)]
        args += [extra]
    n_keys = s0.shape[2]
    st_spec = pl.BlockSpec((n_seq, 4, n_keys, HEAD_W), lambda i, t: (i, 0, 0, 0))
    in_specs += [pl.BlockSpec((1, 512), fix2), st_spec, pl.BlockSpec(memory_space=pl.ANY)]
    args += [nw, s0, out_buf]
    o_spec = pl.BlockSpec((rows, 512), lambda i, t: (rb(i, t), 0))
    return pl.pallas_call(
        functools.partial(_gla_body, mode, n_seq, n_chunk, C),
        grid=grid,
        in_specs=in_specs,
        out_specs=[o_spec, st_spec],
        out_shape=[jax.ShapeDtypeStruct(out_buf.shape, out_buf.dtype),
                   jax.ShapeDtypeStruct((n_batch, 4, n_keys, HEAD_W), F32)],
        scratch_shapes=[pltpu.VMEM((n_seq, 4, HEAD_W, HEAD_W), F32), pltpu.VMEM((rows, 512), F32)],
        input_output_aliases={len(args) - 1: 0},
        compiler_params=_cparams(2),
        name=mode + ("_sample" if sample else "_prompt"),
    )(*args)


def _round_bf16(x, on=True):
    return x.astype(BF16).astype(F32) if on else x


def _conf_body(n_seq, L, round_x, round_w, a_ref, gt_ref, hist_ref, w_ref, b_ref, g_ref, be_ref, _, o_ref, co_ref,
               buf, bufr, y_scr, win):
    tstep = pl.program_id(1)
    hist = CONF_WIDTH - 1
    pad = 32 - hist

    @pl.when(tstep == 0)
    def _():
        for s in range(n_seq):
            buf[s, pad:32, :] = hist_ref[s]
            bufr[s, pad:32, :] = _round_bf16(hist_ref[s], round_x)

    u = a_ref[...] * jax.nn.sigmoid(gt_ref[...])
    ur = _round_bf16(u, round_x)
    for s in range(n_seq):
        buf[s, 32:32 + L, :] = u[s * L:(s + 1) * L, :]
        bufr[s, 32:32 + L, :] = ur[s * L:(s + 1) * L, :]
    w = _round_bf16(w_ref[...], round_w)
    for s in range(n_seq):
        acc = jnp.zeros((L, CONF_DIM), F32)
        for phase in range(8):
            n_taps = (CONF_WIDTH - 1 - phase) // 8 + 1
            span = L + 8 * (n_taps - 1)
            win[phase, 0:span, :] = bufr[s, pad + phase:pad + phase + span, :]
            for a in range(n_taps):
                j = 8 * a + phase
                acc = acc + win[phase, 8 * a:8 * a + L, :] * w[j:j + 1, :]
        y_scr[s * L:(s + 1) * L, :] = _silu(_layernorm(acc + b_ref[...], g_ref[...], be_ref[...]))
        tail = buf[s, L + pad:L + 32, :]
        buf[s, pad:32, :] = tail
        tailr = bufr[s, L + pad:L + 32, :]
        bufr[s, pad:32, :] = tailr
    o_ref[...] = y_scr[...].astype(o_ref.dtype)

    @pl.when(tstep == pl.num_programs(1) - 1)
    def _():
        for s in range(n_seq):
            co_ref[s] = buf[s, pad:32, :]


def _conf_call(h, col_a, col_g, cache, w, b, g, be, out_buf, n_batch, seq_len, row_off, sample):
    n_seq, n_chunk, C, rows, grid, rb = _seq_layout(n_batch, seq_len, row_off, sample)
    L = rows // n_seq
    hist = CONF_WIDTH - 1
    colspec = lambda c0: pl.BlockSpec((rows, 512), lambda i, t: (rb(i, t), c0 // 512))
    fix2 = lambda i, t: (0, 0)
    c_spec = pl.BlockSpec((n_seq, hist, CONF_DIM), lambda i, t: (i, 0, 0))
    return pl.pallas_call(
        functools.partial(_conf_body, n_seq, L, True, sample),
        grid=grid,
        in_specs=[colspec(col_a), colspec(col_g), c_spec,
                  pl.BlockSpec((CONF_WIDTH, CONF_DIM), fix2), pl.BlockSpec((1, CONF_DIM), fix2),
                  pl.BlockSpec((1, CONF_DIM), fix2), pl.BlockSpec((1, CONF_DIM), fix2),
                  pl.BlockSpec(memory_space=pl.ANY)],
        out_specs=[pl.BlockSpec((rows, 512), lambda i, t: (rb(i, t), 0)), c_spec],
        out_shape=[jax.ShapeDtypeStruct(out_buf.shape, out_buf.dtype),
                   jax.ShapeDtypeStruct((n_batch, hist, CONF_DIM), F32)],
        scratch_shapes=[pltpu.VMEM((n_seq, 32 + L, CONF_DIM), F32)] * 2 + [pltpu.VMEM((rows, CONF_DIM), F32),
                                                                           pltpu.VMEM((8, L + 24, CONF_DIM), F32)],
        input_output_aliases={7: 0},
        compiler_params=_cparams(2),
        name="conformer" + ("_sample" if sample else "_prompt"),
    )(h, h, cache, w, b, g, be, out_buf)


def _ssd_body(n_seq, n_chunk, C, round_x, round_w, hz_ref, hx_ref, hdt_ref, hist_ref, s0_ref, cw_ref, cb_ref, dtb_ref, alog_ref,
              dvec_ref, nw_ref, _, o_ref, co_ref, so_ref, st, buf, bufr, xbc, y_scr):
    tstep = pl.program_id(1)
    L = n_chunk * C
    hist = SSM_CONV - 1
    pad = 8 - hist
    n_pairs = SSM_HEADS // 2

    @pl.when(tstep == 0)
    def _():
        for s in range(n_seq):
            buf[s, pad:8, :] = hist_ref[s]
            bufr[s, pad:8, :] = _round_bf16(hist_ref[s], round_x)
            for m in range(n_pairs):
                st[s, m] = s0_ref[s, m]

    cw = _round_bf16(cw_ref[...], round_w)
    for s in range(n_seq):
        hx = hx_ref[s * L:(s + 1) * L, :]
        buf[s, 8:8 + L, :] = hx
        bufr[s, 8:8 + L, :] = _round_bf16(hx, round_x)
        acc = jnp.zeros((L, SSM_CONV_DIM), F32)
        for j in range(SSM_CONV):
            acc = acc + bufr[s, pad + j:pad + j + L, :] * cw[j:j + 1, :]
        xbc[s * L:(s + 1) * L, :] = _silu(acc + cb_ref[...])
        tail = buf[s, L + pad:L + 8, :]
        buf[s, pad:8, :] = tail
        tailr = bufr[s, L + pad:L + 8, :]
        bufr[s, pad:8, :] = tailr

    dt = _softplus(hdt_ref[...] + dtb_ref[...])
    la = dt * (-jnp.exp(alog_ref[...]))
    hrow = lax.broadcasted_iota(I32, (LANE, SSM_INNER), 0)
    hcol = lax.broadcasted_iota(I32, (LANE, SSM_INNER), 1) // SSM_HEADDIM
    expand = (hrow == hcol).astype(BF16)
    dtx = functools.reduce(lambda p, q: p + q,
                           [jnp.dot(part, expand, preferred_element_type=F32) for part in _split3(dt)])
    causal = _tri(C)
    tri = causal.astype(BF16)
    lane = lax.broadcasted_iota(I32, (C, HEAD_W), 1)
    bcol_all = _chunk_cumsum(la, C)
    heads_per_group = SSM_HEADS // SSM_GROUPS
    for s in range(n_seq):
        states = [st[s, m] for m in range(n_pairs)]
        for c in range(n_chunk):
            r0 = (s * n_chunk + c) * C
            rs = slice(r0, r0 + C)
            bcol = bcol_all[rs, :]
            brow = functools.reduce(lambda p, q: p + q, [
                lax.dot_general(part, tri, (((0,), (1,)), ((), ())), preferred_element_type=F32)
                for part in _split3(la[rs, :])])
            xs_c = xbc[rs, 0:SSM_INNER]
            v_c = (xs_c * dtx[rs, :]).astype(BF16)
            gmats, bms, cms = [], [], []
            for grp in range(SSM_GROUPS):
                bm = xbc[rs, SSM_INNER + grp * SSM_STATE:SSM_INNER + (grp + 1) * SSM_STATE]
                cm = xbc[rs, SSM_INNER + (SSM_GROUPS + grp) * SSM_STATE:SSM_INNER + (SSM_GROUPS + grp + 1) * SSM_STATE]
                cm_hi, cm_lo = _split2(cm)
                bm_hi, bm_lo = _split2(bm)
                gmats.append(_dot_nt(jnp.concatenate([cm_hi, cm_hi, cm_lo], axis=1),
                                     jnp.concatenate([bm_hi, bm_lo, bm_hi], axis=1)))
                bms.append(bm)
                cms.append(cm)
            for m in range(n_pairs):
                grp = (2 * m) // heads_per_group
                bm, cm, gmat = bms[grp], cms[grp], gmats[grp]
                ps = slice(m * HEAD_W, (m + 1) * HEAD_W)
                vp = v_c[:, ps]
                s_t = states[m]
                scores, queries, keys, decays = [], [], [], []
                for hh in range(2):
                    hd = 2 * m + hh
                    bc = bcol[:, hd:hd + 1]
                    br = brow[hd:hd + 1, :]
                    b_last = bcol[C - 1:C, hd:hd + 1]
                    scores.append(gmat * jnp.where(causal, jnp.exp(jnp.minimum(bc - br, 0.0)), 0.0))
                    queries.append(cm * jnp.exp(bc))
                    keys.append(bm * jnp.exp(b_last - bc))
                    decays.append(jnp.exp(b_last))
                o_stack = (_dot(jnp.concatenate(scores, axis=0), vp)
                           + _dot_nt(jnp.concatenate(queries, axis=0), s_t))
                kv = _dot_tn(vp, jnp.concatenate(keys, axis=1))
                d = SSM_HEADDIM
                states[m] = jnp.concatenate([s_t[0:d, :] * decays[0] + kv[0:d, 0:SSM_STATE],
                                             s_t[d:, :] * decays[1] + kv[d:, SSM_STATE:]], axis=0)
                o_pair = jnp.where(lane < SSM_HEADDIM, o_stack[0:C, :], o_stack[C:, :])
                y_scr[rs, ps] = o_pair + dvec_ref[:, ps] * xs_c[:, ps]
        for m in range(n_pairs):
            st[s, m] = states[m]
    y = y_scr[...] * _silu(hz_ref[...])
    gw = SSM_INNER // SSM_GROUPS
    for grp in range(SSM_GROUPS):
        gs = slice(grp * gw, (grp + 1) * gw)
        yg = y[:, gs]
        ms = jnp.mean(yg * yg, axis=-1, keepdims=True)
        o_ref[:, gs] = (yg * lax.rsqrt(ms + RMS_EPS) * nw_ref[:, gs]).astype(BF16)

    @pl.when(tstep == pl.num_programs(1) - 1)
    def _():
        for s in range(n_seq):
            co_ref[s] = buf[s, pad:8, :]
            for m in range(n_pairs):
                so_ref[s, m] = st[s, m]


def _ssd_call(h, col_z, col_x, col_dt, cache, s0, cw, cb, dtb, alog, dvec, nw, out_buf, n_batch, seq_len, row_off,
              sample):
    n_seq, n_chunk, C, rows, grid, rb = _seq_layout(n_batch, seq_len, row_off, sample, SSD_PROMPT_TILE)
    L = rows // n_seq
    hist = SSM_CONV - 1
    n_pairs = SSM_HEADS // 2
    colspec = lambda c0, w: pl.BlockSpec((rows, w), lambda i, t: (rb(i, t), c0 // w))
    fix2 = lambda i, t: (0, 0)
    c_spec = pl.BlockSpec((n_seq, hist, SSM_CONV_DIM), lambda i, t: (i, 0, 0))
    st_spec = pl.BlockSpec((n_seq, n_pairs, HEAD_W, SSM_STATE), lambda i, t: (i, 0, 0, 0))
    return pl.pallas_call(
        functools.partial(_ssd_body, n_seq, n_chunk, C, sample, True),
        grid=grid,
        in_specs=[colspec(col_z, 512), colspec(col_x, SSM_CONV_DIM), colspec(col_dt, LANE), c_spec, st_spec,
                  pl.BlockSpec((SSM_CONV, SSM_CONV_DIM), fix2), pl.BlockSpec((1, SSM_CONV_DIM), fix2),
                  pl.BlockSpec((1, LANE), fix2), pl.BlockSpec((1, LANE), fix2),
                  pl.BlockSpec((1, SSM_INNER), fix2), pl.BlockSpec((1, SSM_INNER), fix2),
                  pl.BlockSpec(memory_space=pl.ANY)],
        out_specs=[pl.BlockSpec((rows, 512), lambda i, t: (rb(i, t), 0)), c_spec, st_spec],
        out_shape=[jax.ShapeDtypeStruct(out_buf.shape, out_buf.dtype),
                   jax.ShapeDtypeStruct((n_batch, hist, SSM_CONV_DIM), F32),
                   jax.ShapeDtypeStruct((n_batch, n_pairs, HEAD_W, SSM_STATE), F32)],
        scratch_shapes=[pltpu.VMEM((n_seq, n_pairs, HEAD_W, SSM_STATE), F32),
                        pltpu.VMEM((n_seq, 8 + L, SSM_CONV_DIM), F32),
                        pltpu.VMEM((n_seq, 8 + L, SSM_CONV_DIM), F32),
                        pltpu.VMEM((rows, SSM_CONV_DIM), F32),
                        pltpu.VMEM((rows, SSM_INNER), F32)],
        input_output_aliases={11: 0},
        compiler_params=_cparams(2),
        name="ssd" + ("_sample" if sample else "_prompt"),
    )(h, h, h, cache, s0, cw, cb, dtb, alog, dvec, nw, out_buf)


def _pad_heads(w, n_heads, width):
    lead = w.shape[:-1]
    w = w.reshape(lead + (n_heads, width))
    w = jnp.pad(w, [(0, 0)] * len(lead) + [(0, 0), (0, HEAD_W - width)])
    return w.reshape(lead + (n_heads * HEAD_W,))


def _row(v):
    return v.reshape(1, -1).astype(F32)


def kernel(x_prompt, x_sample, state_gla, cache_conformer, state_hgrn, state_ssm, cache_mamba_conv, w_in_even, w_gla_gate_lr, b_gla_gate, gla_norm_w, conf_conv_w, conf_conv_b, conf_ln_g, conf_ln_b, w_out_even, w_in_odd, hgrn_lower_bounds, hgrn_norm_w, mamba_conv_w, mamba_conv_b, mamba_dt_bias, mamba_a_log, mamba_d, mamba_norm_w, w_out_odd, ln1_g, ln1_b, ln2_g, ln2_b, router_w, router_b, expert_w_gate, expert_b_gate, expert_w_up, expert_b_up, expert_w_down, expert_b_down):
    bp, lp, _ = x_prompt.shape
    bs, ls, _ = x_sample.shape
    tp, ts = bp * lp, bs * ls
    x = (x_prompt.reshape(tp, D_MODEL), x_sample.reshape(ts, D_MODEL))

    def router_params(layer):
        return router_w[layer].T.astype(BF16), router_b[layer].astype(F32).reshape(N_EXPERTS, 1)

    def finish_layer(layer, x, mix_a, mix_b, w_out):
        rwt, rb = router_params(layer)
        x1, xp, gates, eidx, rank, counts = _outproj_ln_router(
            mix_a, mix_b, x[0], x[1], w_out[:512].astype(BF16), w_out[512:].astype(BF16),
            _row(ln1_g[layer]), _row(ln1_b[layer]), rwt, rb)
        return _moe(layer, x1, xp, gates, eidx, rank, counts, _row(ln2_g[layer]), _row(ln2_b[layer]),
                    expert_w_gate, expert_b_gate, expert_w_up, expert_b_up, expert_w_down, expert_b_down, tp)

    mix_init = jnp.zeros((tp + ts, 512), BF16)

    wi = w_in_even[0]
    wq, wk, wv, wg, wlr, wglu = jnp.split(wi, [256, 512, 1024, 1536, 1552], axis=1)
    w_even = jnp.concatenate([_pad_heads(wq, GLA_HEADS, GLA_DK), _pad_heads(wk, GLA_HEADS, GLA_DK), wv, wg, wglu,
                              jnp.pad(wlr, ((0, 0), (0, LANE - GLA_RANK)))], axis=1).astype(BF16)
    cols_gla = (0, 512, 1024, 1536, 3072)
    col_a, col_gate = 2048, 2560
    h = _inproj(x[0], x[1], w_even)
    wlr_p = jnp.pad(_pad_heads(w_gla_gate_lr[0], GLA_HEADS, GLA_DK), ((0, LANE - GLA_RANK), (0, 0)))
    blr_p = _row(_pad_heads(b_gla_gate[0], GLA_HEADS, GLA_DK))
    nw = _row(gla_norm_w[0])
    conf_args = (conf_conv_w[0], _row(conf_conv_b[0]), _row(conf_ln_g[0]), _row(conf_ln_b[0]))
    s0_p = jnp.zeros((bp, GLA_HEADS, GLA_DK, HEAD_W), F32)
    s0_s = state_gla[0]
    mix_a, sg_p = _gla_call("gla", h, cols_gla, (wlr_p, blr_p), nw, s0_p, mix_init, bp, lp, 0, False)
    mix_a, sg_s = _gla_call("gla", h, cols_gla, (wlr_p, blr_p), nw, s0_s, mix_a, bs, ls, tp, True)
    mix_b, cc_p = _conf_call(h, col_a, col_gate, jnp.zeros((bp,) + cache_conformer.shape[2:], F32), *conf_args,
                             mix_init, bp, lp, 0, False)
    mix_b, cc_s = _conf_call(h, col_a, col_gate, cache_conformer[0], *conf_args, mix_b, bs, ls, tp, True)
    x = finish_layer(0, x, mix_a, mix_b, w_out_even[0])
    gla_p, gla_s = sg_p[None], sg_s[None]
    conf_p, conf_s = cc_p[None], cc_s[None]

    lb_cum = jnp.cumsum(jax.nn.softmax(hgrn_lower_bounds.astype(F32), axis=0), axis=0)
    lower_bound = _row((lb_cum - lb_cum[0])[1])
    wo = w_in_odd[0]
    w_odd = jnp.concatenate([wo[:, 2560:3584], wo[:, :2560],
                             jnp.pad(wo[:, 3584:], ((0, 0), (0, LANE - SSM_HEADS)))], axis=1).astype(BF16)
    h = _inproj(x[0], x[1], w_odd)
    cols_hgrn = (1024, 1536, 2048, 2560)
    col_z, col_x, col_dt = 3072, 0, 3584
    nw = _row(hgrn_norm_w[0])
    mix_a, sh_p = _gla_call("hgrn", h, cols_hgrn, lower_bound, nw,
                            jnp.zeros((bp, HGRN_HEADS, HEAD_W, HEAD_W), F32), mix_init, bp, lp, 0, False)
    mix_a, sh_s = _gla_call("hgrn", h, cols_hgrn, lower_bound, nw, state_hgrn[0], mix_a, bs, ls, tp, True)

    def pair_states(s):
        return jnp.swapaxes(s, 2, 3).reshape(s.shape[0], SSM_HEADS // 2, HEAD_W, SSM_STATE)

    def unpair_states(s):
        return jnp.swapaxes(s.reshape(s.shape[0], SSM_HEADS, SSM_HEADDIM, SSM_STATE), 2, 3)

    pad8 = lambda v: jnp.pad(v.astype(F32), (0, LANE - SSM_HEADS)).reshape(1, LANE)
    ssd_args = (mamba_conv_w[0], _row(mamba_conv_b[0]), pad8(mamba_dt_bias[0]), pad8(mamba_a_log[0]),
                _row(jnp.repeat(mamba_d[0], SSM_HEADDIM)), _row(mamba_norm_w[0]))
    mix_b, cm_p, ss_p = _ssd_call(h, col_z, col_x, col_dt, jnp.zeros((bp,) + cache_mamba_conv.shape[2:], F32),
                                  jnp.zeros((bp, SSM_HEADS // 2, HEAD_W, SSM_STATE), F32), *ssd_args,
                                  mix_init, bp, lp, 0, False)
    mix_b, cm_s, ss_s = _ssd_call(h, col_z, col_x, col_dt, cache_mamba_conv[0], pair_states(state_ssm[0]),
                                  *ssd_args, mix_b, bs, ls, tp, True)
    y_prompt, y_sample = finish_layer(1, x, mix_a, mix_b, w_out_odd[0])
    y_prompt = y_prompt.reshape(bp, lp, D_MODEL)
    y_sample = y_sample.reshape(bs, ls, D_MODEL)
    return (y_prompt, y_sample, gla_p, gla_s, conf_p, conf_s, sh_p[None], sh_s[None],
            unpair_states(ss_p)[None], unpair_states(ss_s)[None], cm_p[None], cm_s[None])
```

```python
import functools

import jax
import jax.numpy as jnp
from jax import lax
from jax.experimental import pallas as pl
from jax.experimental.pallas import tpu as pltpu
from jax.experimental.pallas import tpu_sc as plsc

F32 = jnp.float32
BF16 = jnp.bfloat16
I32 = jnp.int32
U32 = jnp.uint32

D_MODEL = 1024
DEPTH = 2
DEEPNORM_ALPHA = (2.0 * DEPTH) ** 0.25
LN_EPS = 1e-5
RMS_EPS = 1e-6
LANE = 128
HEAD_W = 128
GLA_HEADS, GLA_DK, GLA_RANK, GLA_TAU = 4, 64, 16, 16.0
CONF_DIM, CONF_WIDTH = 512, 31
HGRN_HEADS, HGRN_DK = 4, 128
SSM_HEADS, SSM_HEADDIM, SSM_STATE, SSM_GROUPS, SSM_CONV = 8, 64, 128, 2, 4
SSM_INNER = SSM_HEADS * SSM_HEADDIM
SSM_CONV_DIM = SSM_INNER + 2 * SSM_GROUPS * SSM_STATE
N_EXPERTS, TOP_K = 32, 4
SWIGLU_ALPHA, SWIGLU_LIMIT = 1.702, 7.0
SCAN_CHUNK = 64
PROMPT_TILE = 512
SAMPLE_SEQS = 16
TOKEN_TILE = 512
WIDE_TOKEN_TILE = 1024
MOE_ROWS = 512
IN_SLOTS = 4
SC_CORES, SC_SUBCORES = 2, 16
SC_ROWS = 64
SC_SCATTER_ROWS = 32
VMEM_LIMIT = 56 * 1024 * 1024


def _cparams(n_axes):
    return pltpu.CompilerParams(dimension_semantics=("arbitrary",) * n_axes, vmem_limit_bytes=VMEM_LIMIT)


def _silu(x):
    return x * jax.nn.sigmoid(x)


def _softplus(x):
    return jnp.maximum(x, 0.0) + jnp.log(1.0 + jnp.exp(-jnp.abs(x)))


def _log_sigmoid(x):
    return jnp.minimum(x, 0.0) - jnp.log(1.0 + jnp.exp(-jnp.abs(x)))


def _layernorm(y, g, b):
    mu = jnp.mean(y, axis=-1, keepdims=True)
    d = y - mu
    var = jnp.mean(d * d, axis=-1, keepdims=True)
    return d * lax.rsqrt(var + LN_EPS) * g + b


def _dot(a, b):
    return jnp.dot(a.astype(BF16), b.astype(BF16), preferred_element_type=F32)


def _dot_nt(a, b):
    return lax.dot_general(a.astype(BF16), b.astype(BF16), (((1,), (1,)), ((), ())), preferred_element_type=F32)


def _dot_tn(a, b):
    return lax.dot_general(a.astype(BF16), b.astype(BF16), (((0,), (0,)), ((), ())), preferred_element_type=F32)


def _tri(c):
    r = lax.broadcasted_iota(I32, (c, c), 0)
    k = lax.broadcasted_iota(I32, (c, c), 1)
    return r >= k


def _pair_specs(tm, n_first, width):
    return [pl.BlockSpec((tm, width), lambda i: (jnp.minimum(i, n_first - 1), 0)),
            pl.BlockSpec((tm, width), lambda i: (jnp.maximum(i - n_first, 0), 0))]


def _pair_tile(n_first, xa_ref, xb_ref):
    return jnp.where(pl.program_id(0) < n_first, xa_ref[...], xb_ref[...])


def _inproj_body(n_first, xa_ref, xb_ref, w_ref, o_ref):
    xb = _pair_tile(n_first, xa_ref, xb_ref).astype(BF16)
    n = w_ref.shape[1]
    for c0 in range(0, n, 512):
        c1 = min(c0 + 512, n)
        o_ref[:, c0:c1] = jnp.dot(xb, w_ref[:, c0:c1], preferred_element_type=F32)


def _inproj(xa, xb, w):
    k, n = w.shape
    t = xa.shape[0] + xb.shape[0]
    n_first = xa.shape[0] // TOKEN_TILE
    return pl.pallas_call(
        functools.partial(_inproj_body, n_first),
        grid=(t // TOKEN_TILE,),
        in_specs=_pair_specs(TOKEN_TILE, n_first, k) + [pl.BlockSpec((k, n), lambda i: (0, 0))],
        out_specs=pl.BlockSpec((TOKEN_TILE, n), lambda i: (i, 0)),
        out_shape=jax.ShapeDtypeStruct((t, n), F32),
        compiler_params=_cparams(1),
        name="inproj",
    )(xa, xb, w)


def _pack_halves(y):
    half = y.shape[1] // 2
    hi = lax.bitcast_convert_type(y[:, :half].astype(BF16).astype(F32), U32)
    lo = lax.bitcast_convert_type(y[:, half:].astype(BF16).astype(F32), U32)
    return (hi & jnp.uint32(0xFFFF0000)) | (lo >> 16)


def _unpack_halves(w):
    hi = lax.bitcast_convert_type(w & jnp.uint32(0xFFFF0000), F32)
    lo = lax.bitcast_convert_type(w << 16, F32)
    return hi, lo


def _outproj_body(n_first, a_ref, b_ref, xa_ref, xb_ref, wa_ref, wb_ref, g_ref, be_ref, rwt_ref, rb_ref,
                  x1_ref, xp_ref, gate_ref, idx_ref, rank_ref, cnt_ref, carry):
    @pl.when(pl.program_id(0) == 0)
    def _():
        carry[...] = jnp.zeros(carry.shape, F32)

    mix = (jnp.dot(a_ref[...], wa_ref[...], preferred_element_type=F32)
           + jnp.dot(b_ref[...], wb_ref[...], preferred_element_type=F32))
    x1 = _layernorm(DEEPNORM_ALPHA * _pair_tile(n_first, xa_ref, xb_ref) + mix, g_ref[...], be_ref[...])
    x1_ref[...] = x1
    xp_ref[...] = _pack_halves(x1)
    logits = _dot_nt(rwt_ref[...], x1) + rb_ref[...]
    tm = logits.shape[1]
    expert = lax.broadcasted_iota(I32, logits.shape, 0)
    vals, idxs = [], []
    for _ in range(TOP_K):
        m = jnp.max(logits, axis=0, keepdims=True)
        sel = jnp.min(jnp.where(logits == m, expert, N_EXPERTS), axis=0, keepdims=True)
        vals.append(m)
        idxs.append(sel)
        logits = jnp.where(expert == sel, -jnp.inf, logits)
    exps = [jnp.exp(v - vals[0]) for v in vals]
    inv = 1.0 / functools.reduce(lambda p, q: p + q, exps)
    chosen = jnp.zeros(logits.shape, F32)
    for k in range(TOP_K):
        chosen = chosen + (expert == idxs[k]).astype(F32)
    earlier = lax.broadcasted_iota(I32, (tm, tm), 0) < lax.broadcasted_iota(I32, (tm, tm), 1)
    before = carry[...] + jnp.dot(chosen.astype(BF16), earlier.astype(BF16), preferred_element_type=F32)
    choice = lax.broadcasted_iota(I32, (8, tm), 0)
    gates = jnp.zeros((8, tm), F32)
    eidx = jnp.zeros((8, tm), I32)
    ranks = jnp.zeros((8, tm), F32)
    for k in range(TOP_K):
        rk = jnp.sum(jnp.where(expert == idxs[k], before, 0.0), axis=0, keepdims=True)
        gates = jnp.where(choice == k, exps[k] * inv, gates)
        eidx = jnp.where(choice == k, idxs[k], eidx)
        ranks = jnp.where(choice == k, rk, ranks)
    gate_ref[...] = gates
    idx_ref[...] = eidx
    rank_ref[...] = ranks.astype(I32)
    carry[...] = carry[...] + jnp.sum(chosen, axis=1, keepdims=True)
    cnt_ref[...] = carry[...].astype(I32)


def _outproj_ln_router(a, b, xa, xb, wa, wb, g, be, rwt, rb):
    t = xa.shape[0] + xb.shape[0]
    tm = WIDE_TOKEN_TILE
    n_first = xa.shape[0] // tm
    row = lambda i: (i, 0)
    col = lambda i: (0, i)
    fix = lambda i: (0, 0)
    return pl.pallas_call(
        functools.partial(_outproj_body, n_first),
        grid=(t // tm,),
        in_specs=[pl.BlockSpec((tm, 512), row), pl.BlockSpec((tm, 512), row)] + _pair_specs(tm, n_first, D_MODEL)
        + [pl.BlockSpec((512, D_MODEL), fix), pl.BlockSpec((512, D_MODEL), fix),
           pl.BlockSpec((1, D_MODEL), fix), pl.BlockSpec((1, D_MODEL), fix),
           pl.BlockSpec((N_EXPERTS, D_MODEL), fix), pl.BlockSpec((N_EXPERTS, 1), fix)],
        out_specs=[pl.BlockSpec((tm, D_MODEL), row), pl.BlockSpec((tm, 512), row),
                   pl.BlockSpec((8, tm), col), pl.BlockSpec((8, tm), col), pl.BlockSpec((8, tm), col),
                   pl.BlockSpec((N_EXPERTS, 1), fix)],
        out_shape=[jax.ShapeDtypeStruct((t, D_MODEL), F32), jax.ShapeDtypeStruct((t, 512), U32),
                   jax.ShapeDtypeStruct((8, t), F32), jax.ShapeDtypeStruct((8, t), I32),
                   jax.ShapeDtypeStruct((8, t), I32), jax.ShapeDtypeStruct((N_EXPERTS, 1), I32)],
        scratch_shapes=[pltpu.VMEM((N_EXPERTS, 1), F32)],
        compiler_params=_cparams(1),
        name="outproj_ln_router",
    )(a, b, xa, xb, wa, wb, g, be, rwt, rb)


def _sc_mesh():
    return plsc.VectorSubcoreMesh(core_axis_name="c", subcore_axis_name="s")


def _sc_scatter_rows(src, dest, n_out):
    n_src, w = src.shape
    n_dst = dest.shape[1]
    workers = SC_CORES * SC_SUBCORES
    per_worker = n_src // workers
    chunks = per_worker // SC_SCATTER_ROWS
    assert n_src == workers * chunks * SC_SCATTER_ROWS
    idx = dest.T.reshape(n_dst, workers, chunks, SC_SCATTER_ROWS)

    @functools.partial(pl.kernel, mesh=_sc_mesh(), out_type=jax.ShapeDtypeStruct((n_out, w), src.dtype),
                       scratch_types=[pltpu.VMEM((n_dst, chunks, SC_SCATTER_ROWS), I32)]
                       + [pltpu.VMEM((SC_SCATTER_ROWS, w), src.dtype)] * 2 + [pltpu.SemaphoreType.DMA] * 4)
    def scatter(src_hbm, idx_hbm, out_hbm, idx_v, rows_a, rows_b, sem_ra, sem_rb, sem_wa, sem_wb):
        worker = lax.axis_index("s") * SC_CORES + lax.axis_index("c")
        base = worker * per_worker
        for k in range(n_dst):
            pltpu.sync_copy(idx_hbm.at[k, worker], idx_v.at[k])

        def read(c, rows, sem):
            return pltpu.async_copy(src_hbm.at[pl.ds(pl.multiple_of(base + c * SC_SCATTER_ROWS, 8), SC_SCATTER_ROWS)],
                                    rows, sem)

        def write_all(pending_read, c, rows, sem):
            pending_read.wait()
            return [pltpu.async_copy(rows, out_hbm.at[idx_v.at[k, c]], sem) for k in range(n_dst)]

        @pl.loop(0, chunks // 2)
        def _(p):
            read_a = read(2 * p, rows_a, sem_ra)
            read_b = read(2 * p + 1, rows_b, sem_rb)
            writes = write_all(read_a, 2 * p, rows_a, sem_wa) + write_all(read_b, 2 * p + 1, rows_b, sem_wb)
            for wr in writes:
                wr.wait()

        if chunks % 2:
            for wr in write_all(read(chunks - 1, rows_a, sem_ra), chunks - 1, rows_a, sem_wa):
                wr.wait()

    return scatter(src, idx)


def _sc_gather_rows(table, idx):
    n, w = idx.shape[0], table.shape[1]
    workers = SC_CORES * SC_SUBCORES
    chunks = n // (workers * SC_ROWS)
    assert n == workers * chunks * SC_ROWS and chunks % 2 == 0
    idx = idx.reshape(workers, chunks, SC_ROWS)

    @functools.partial(pl.kernel, mesh=_sc_mesh(), out_type=jax.ShapeDtypeStruct((n, w), table.dtype),
                       scratch_types=[pltpu.VMEM((chunks, SC_ROWS), I32)] + [pltpu.VMEM((SC_ROWS, w), table.dtype)] * 2
                       + [pltpu.SemaphoreType.DMA] * 4)
    def gather(table_hbm, idx_hbm, out_hbm, idx_v, rows_a, rows_b, sem_ra, sem_rb, sem_wa, sem_wb):
        worker = lax.axis_index("s") * SC_CORES + lax.axis_index("c")
        base = worker * (chunks * SC_ROWS)
        pltpu.sync_copy(idx_hbm.at[worker], idx_v)

        def out_rows(c):
            return out_hbm.at[pl.ds(pl.multiple_of(base + c * SC_ROWS, 8), SC_ROWS)]

        @pl.loop(0, chunks // 2)
        def _(p):
            read_a = pltpu.async_copy(table_hbm.at[idx_v.at[2 * p]], rows_a, sem_ra)
            read_b = pltpu.async_copy(table_hbm.at[idx_v.at[2 * p + 1]], rows_b, sem_rb)
            read_a.wait()
            write_a = pltpu.async_copy(rows_a, out_rows(2 * p), sem_wa)
            read_b.wait()
            write_b = pltpu.async_copy(rows_b, out_rows(2 * p + 1), sem_wb)
            write_a.wait()
            write_b.wait()

    return gather(table, idx)


def _experts_body(b0_ref, nb_ref, last_ref, nt_ref, xs_hbm, wg_ref, bg_ref, wu_ref, bu_ref, wd_ref, bd_ref, o_hbm,
                  wg_s, wu_s, wd_s, xbuf, obuf, sem_in, sem_out):
    e = pl.program_id(0)
    first_blk, n_blk, last_valid, n_total = b0_ref[e], nb_ref[e], last_ref[e], nt_ref[0]
    quarter = MOE_ROWS // 4

    def rows_of(g):
        return pl.ds(pl.multiple_of(g * MOE_ROWS, MOE_ROWS), MOE_ROWS)

    def fetch(g, slot):
        return pltpu.make_async_copy(xs_hbm.at[rows_of(g)], xbuf.at[slot], sem_in.at[slot])

    def put(g, slot):
        return pltpu.make_async_copy(obuf.at[slot], o_hbm.at[rows_of(g)], sem_out.at[slot])

    lead = IN_SLOTS - 1
    for first in range(lead):
        @pl.when((e == 0) & (n_total > first))
        def _():
            fetch(first, first).start()

    @pl.when(n_blk > 0)
    def _():
        wg_s[...] = wg_ref[...].astype(BF16)
        wu_s[...] = wu_ref[...].astype(BF16)
        wd_s[...] = wd_ref[...].astype(BF16)

    def compute(islot, slot, rows):
        half = D_MODEL // 2
        x_hi, x_lo = _unpack_halves(xbuf[islot, 0:rows, :])
        x_hi = x_hi.astype(BF16)
        x_lo = x_lo.astype(BF16)
        g = (jnp.dot(x_hi, wg_s[:half, :], preferred_element_type=F32)
             + jnp.dot(x_lo, wg_s[half:, :], preferred_element_type=F32) + bg_ref[...])
        u = (jnp.dot(x_hi, wu_s[:half, :], preferred_element_type=F32)
             + jnp.dot(x_lo, wu_s[half:, :], preferred_element_type=F32) + bu_ref[...])
        g = jnp.minimum(g, SWIGLU_LIMIT)
        u = jnp.clip(u, -SWIGLU_LIMIT, SWIGLU_LIMIT)
        hmid = (u + 1.0) * (g * jax.nn.sigmoid(SWIGLU_ALPHA * g))
        out = jnp.dot(hmid.astype(BF16), wd_s[...], preferred_element_type=F32) + bd_ref[...]
        obuf[slot, 0:rows, :] = _pack_halves(out)

    def block(j, carry):
        g = first_blk + j
        slot = lax.rem(g, 2)
        islot = lax.rem(g, IN_SLOTS)
        fetch(g, islot).wait()

        @pl.when(g + lead < n_total)
        def _():
            fetch(g + lead, lax.rem(g + lead, IN_SLOTS)).start()

        @pl.when(g >= 2)
        def _():
            put(g - 2, slot).wait()

        valid = jnp.where(j == n_blk - 1, last_valid, MOE_ROWS)

        for rows in range(quarter, MOE_ROWS + 1, quarter):
            @pl.when((valid > rows - quarter) & (valid <= rows))
            def _():
                compute(islot, slot, rows)
                if rows < MOE_ROWS:
                    obuf[slot, rows:, :] = jnp.zeros((MOE_ROWS - rows, obuf.shape[2]), obuf.dtype)

        put(g, slot).start()
        return carry

    lax.fori_loop(0, n_blk, block, 0)

    @pl.when((e == N_EXPERTS - 1) & (n_total >= 2))
    def _():
        put(n_total - 2, lax.rem(n_total, 2)).wait()

    @pl.when((e == N_EXPERTS - 1) & (n_total >= 1))
    def _():
        put(n_total - 1, lax.rem(n_total - 1, 2)).wait()


def _experts(layer, first_blk, n_blk, last_valid, xs, wg, bg, wu, bu, wd, bd):
    n_rows, w = xs.shape
    wsel = lambda e, b0, nb, lv, nt: (layer, e, 0, 0)
    wspec = pl.BlockSpec((None, None, D_MODEL, D_MODEL), wsel)
    bspec = pl.BlockSpec((None, None, 1, D_MODEL), wsel)
    bias = lambda b: b.reshape(b.shape[0], b.shape[1], 1, b.shape[2])
    return pl.pallas_call(
        _experts_body,
        grid_spec=pltpu.PrefetchScalarGridSpec(
            num_scalar_prefetch=4,
            grid=(N_EXPERTS,),
            in_specs=[pl.BlockSpec(memory_space=pl.ANY), wspec, bspec, wspec, bspec, wspec, bspec],
            out_specs=pl.BlockSpec(memory_space=pl.ANY),
            scratch_shapes=[pltpu.VMEM((D_MODEL, D_MODEL), BF16)] * 3
            + [pltpu.VMEM((IN_SLOTS, MOE_ROWS, w), U32), pltpu.VMEM((2, MOE_ROWS, w), U32),
               pltpu.SemaphoreType.DMA((IN_SLOTS,)), pltpu.SemaphoreType.DMA((2,))],
        ),
        out_shape=jax.ShapeDtypeStruct((n_rows, w), U32),
        compiler_params=_cparams(1),
        name="experts",
    )(first_blk, n_blk, last_valid, jnp.sum(n_blk).reshape(1), xs, wg, bias(bg), wu, bias(bu), wd, bias(bd))


def _combine_body(n_first, o0_ref, o1_ref, o2_ref, o3_ref, gt_ref, x_ref, g_ref, b_ref, ya_ref, yb_ref=None):
    half = D_MODEL // 2
    gates = gt_ref[...]
    hi = jnp.zeros((x_ref.shape[0], half), F32)
    lo = jnp.zeros((x_ref.shape[0], half), F32)
    for k, o_ref in enumerate((o0_ref, o1_ref, o2_ref, o3_ref)):
        h, l = _unpack_halves(o_ref[...])
        gk = gates[:, k:k + 1]
        hi = hi + gk * h
        lo = lo + gk * l
    x = x_ref[...]
    y_hi = DEEPNORM_ALPHA * x[:, :half] + hi
    y_lo = DEEPNORM_ALPHA * x[:, half:] + lo
    mu = (jnp.sum(y_hi, axis=-1, keepdims=True) + jnp.sum(y_lo, axis=-1, keepdims=True)) * (1.0 / D_MODEL)
    d_hi = y_hi - mu
    d_lo = y_lo - mu
    var = (jnp.sum(d_hi * d_hi, axis=-1, keepdims=True) + jnp.sum(d_lo * d_lo, axis=-1, keepdims=True)) * (1.0 / D_MODEL)
    r = lax.rsqrt(var + LN_EPS)
    out_hi = d_hi * r * g_ref[:, :half] + b_ref[:, :half]
    out_lo = d_lo * r * g_ref[:, half:] + b_ref[:, half:]

    def write(y_ref):
        y_ref[:, :half] = out_hi
        y_ref[:, half:] = out_lo

    if yb_ref is None:
        write(ya_ref)
    else:
        pl.when(pl.program_id(0) < n_first)(lambda: write(ya_ref))
        pl.when(pl.program_id(0) >= n_first)(lambda: write(yb_ref))


def _combine_ln(o4, gates, x, g, b, t_first=None):
    t = x.shape[0]
    tm = WIDE_TOKEN_TILE
    row = lambda i: (i, 0)
    fix = lambda i: (0, 0)
    choice = lambda k: pl.BlockSpec((tm, 512), lambda i: (k * (t // tm) + i, 0))
    if t_first is None:
        n_first = None
        out_specs = pl.BlockSpec((tm, D_MODEL), row)
        out_shape = jax.ShapeDtypeStruct((t, D_MODEL), F32)
    else:
        n_first = t_first // tm
        out_specs = [pl.BlockSpec((tm, D_MODEL), lambda i: (jnp.minimum(i, n_first - 1), 0)),
                     pl.BlockSpec((tm, D_MODEL), lambda i: (jnp.maximum(i - n_first, 0), 0))]
        out_shape = [jax.ShapeDtypeStruct((t_first, D_MODEL), F32), jax.ShapeDtypeStruct((t - t_first, D_MODEL), F32)]
    return pl.pallas_call(
        functools.partial(_combine_body, n_first),
        grid=(t // tm,),
        in_specs=[choice(0), choice(1), choice(2), choice(3), pl.BlockSpec((tm, TOP_K), row),
                  pl.BlockSpec((tm, D_MODEL), row), pl.BlockSpec((1, D_MODEL), fix), pl.BlockSpec((1, D_MODEL), fix)],
        out_specs=out_specs,
        out_shape=out_shape,
        compiler_params=_cparams(1),
        name="combine_ln",
    )(o4, o4, o4, o4, gates, x, g, b)


def _moe(layer, x1, xp, gates, eidx, rank, counts, ln_g, ln_b, wg, bg, wu, bu, wd, bd, t_first=None):
    t = x1.shape[0]
    bm = MOE_ROWS
    n_blocks = t * TOP_K // bm + N_EXPERTS
    n_rows = n_blocks * bm
    cnt = counts[:, 0]
    padded = (cnt + bm - 1) // bm * bm
    pad_end = jnp.cumsum(padded)
    pad_start = pad_end - padded
    e = eidx[:TOP_K]
    start = jnp.sum(jnp.where(e[:, :, None] == jnp.arange(N_EXPERTS, dtype=I32), pad_start, 0), axis=-1)
    dest = (start + rank[:TOP_K]).T
    n_blk = padded // bm
    last_valid = cnt - (n_blk - 1) * bm
    xs = _sc_scatter_rows(xp, dest, n_rows)
    outs = _experts(layer, pad_start // bm, n_blk, last_valid, xs, wg, bg, wu, bu, wd, bd)
    o4 = _sc_gather_rows(outs, dest.T.reshape(-1))
    return _combine_ln(o4, gates[:TOP_K].T, x1, ln_g, ln_b, t_first)


def _split2(x):
    hi = x.astype(BF16)
    return hi, (x - hi.astype(F32)).astype(BF16)


def _split3(x):
    hi = x.astype(BF16)
    rem = x - hi.astype(F32)
    mid = rem.astype(BF16)
    return hi, mid, (rem - mid.astype(F32)).astype(BF16)


def _chunk_cumsum(g, C):
    rows = g.shape[0]
    r = lax.broadcasted_iota(I32, (rows, rows), 0)
    c = lax.broadcasted_iota(I32, (rows, rows), 1)
    tri = ((r >= c) & (r // C == c // C)).astype(BF16)
    hi, mid, lo = _split3(g)
    dot = lambda part: jnp.dot(tri, part, preferred_element_type=F32)
    return dot(hi) + dot(mid) + dot(lo)


def _gla_batched_step(q, k, v, g, C, n_seq, n_heads, state_of, o_scr):
    rows = n_seq * C
    wide = n_seq * HEAD_W
    mid = max(C // 2 - 1, 0)
    r = lax.broadcasted_iota(I32, (rows, rows), 0)
    c = lax.broadcasted_iota(I32, (rows, rows), 1)
    same = (r // C) == (c // C)
    causal = same & (r >= c)
    parts = _split3(g)
    summed = lambda mask: functools.reduce(lambda p, q_: p + q_, [
        jnp.dot(mask.astype(BF16), part, preferred_element_type=F32) for part in parts])
    b = summed(causal)
    b_mid = summed(same & ((c % C) <= mid))
    b_last = summed(same)
    qe_hi, qe_lo = _split2(q * jnp.exp(b - b_mid))
    ke_hi, ke_lo = _split2(k * jnp.exp(b_mid - b))
    q_state = (q * jnp.exp(b)).astype(BF16)
    k_state = (k * jnp.exp(b_last - b)).astype(BF16)
    decay_parts = _split3(jnp.exp(b_last))
    row_w = lax.broadcasted_iota(I32, (rows, wide), 0)
    blk_w = lax.broadcasted_iota(I32, (rows, wide), 1) // HEAD_W
    own = (row_w // C) == blk_w
    pick = (row_w == blk_w * C).astype(BF16)
    new_states = []
    for h in range(n_heads):
        cs = slice(h * HEAD_W, (h + 1) * HEAD_W)
        lhs = jnp.concatenate([qe_hi[:, cs], qe_hi[:, cs], qe_lo[:, cs]], axis=1)
        rhs = jnp.concatenate([ke_hi[:, cs], ke_lo[:, cs], ke_hi[:, cs]], axis=1)
        scores = jnp.where(causal, _dot_nt(lhs, rhs), 0.0)
        vh = v[:, cs].astype(BF16)
        s_cat = jnp.concatenate([state_of(s, h) for s in range(n_seq)], axis=1)
        o_full = _dot(q_state[:, cs], s_cat)
        o_state = jnp.concatenate([o_full[s * C:(s + 1) * C, s * HEAD_W:(s + 1) * HEAD_W] for s in range(n_seq)],
                                  axis=0)
        o = _dot(scores, vh) + o_state
        v_wide = jnp.where(own, jnp.concatenate([vh] * n_seq, axis=1), jnp.zeros((), BF16))
        kv = _dot_tn(k_state[:, cs], v_wide)
        decay = functools.reduce(lambda p, q_: p + q_, [
            lax.dot_general(part[:, cs], pick, (((0,), (0,)), ((), ())), preferred_element_type=F32)
            for part in decay_parts])
        new_states.append(s_cat * decay + kv)
        ms = jnp.mean(o * o, axis=-1, keepdims=True)
        o_scr[:, cs] = o * lax.rsqrt(ms + RMS_EPS)
    return new_states


def _gla_body(mode, n_seq, n_chunk, C, *refs):
    if mode == "gla":
        hq_ref, hk_ref, hv_ref, hg_ref, hlr_ref, wlr_ref, blr_ref, nw_ref, s0_ref, _, o_ref, so_ref, st, o_scr = refs
    else:
        hq_ref, hk_ref, hv_ref, hg_ref, lb_ref, nw_ref, s0_ref, _, o_ref, so_ref, st, o_scr = refs
    n_heads = 4
    n_keys = s0_ref.shape[2]
    tstep = pl.program_id(1)
    batched = n_chunk == 1 and n_seq > 1

    def padded_state(s, h):
        s_in = s0_ref[s, h]
        if n_keys < HEAD_W:
            s_in = jnp.concatenate([s_in, jnp.zeros((HEAD_W - n_keys, HEAD_W), F32)], axis=0)
        return s_in

    if not batched:
        @pl.when(tstep == 0)
        def _():
            for s in range(n_seq):
                for h in range(n_heads):
                    st[s, h] = padded_state(s, h).T

    if mode == "gla":
        q = hq_ref[...] * (GLA_DK ** -0.5)
        k = hk_ref[...]
        z = _dot(hlr_ref[...], wlr_ref[...]) + blr_ref[...]
        g = _log_sigmoid(z) * (1.0 / GLA_TAU)
    else:
        q = _silu(hq_ref[...]) * (HGRN_DK ** -0.5)
        lb = lb_ref[...]
        f = lb + (1.0 - lb) * jax.nn.sigmoid(hk_ref[...])
        k = 1.0 - f
        g = jnp.log(f)
    v = hv_ref[...]
    if batched:
        new_states = _gla_batched_step(q, k, v, g, C, n_seq, n_heads, padded_state, o_scr)
        for s in range(n_seq):
            for h in range(n_heads):
                so_ref[s, h] = new_states[h][0:n_keys, s * HEAD_W:(s + 1) * HEAD_W]
    else:
        causal = _tri(C)
        mid = max(C // 2 - 1, 0)
        b_all = _chunk_cumsum(g, C)
        for s in range(n_seq):
            states = [st[s, h] for h in range(n_heads)]
            for c in range(n_chunk):
                r0 = (s * n_chunk + c) * C
                rs = slice(r0, r0 + C)
                b, qc, kc = b_all[rs, :], q[rs, :], k[rs, :]
                b_last = b[C - 1:C, :]
                b_mid = b[mid:mid + 1, :]
                qe_hi, qe_lo = _split2(qc * jnp.exp(b - b_mid))
                ke_hi, ke_lo = _split2(kc * jnp.exp(b_mid - b))
                q_state = (qc * jnp.exp(b)).astype(BF16)
                k_state = (kc * jnp.exp(b_last - b)).astype(BF16)
                decay = jnp.exp(b_last)
                for h in range(n_heads):
                    cs = slice(h * HEAD_W, (h + 1) * HEAD_W)
                    lhs = jnp.concatenate([qe_hi[:, cs], qe_hi[:, cs], qe_lo[:, cs]], axis=1)
                    rhs = jnp.concatenate([ke_hi[:, cs], ke_lo[:, cs], ke_hi[:, cs]], axis=1)
                    scores = jnp.where(causal, _dot_nt(lhs, rhs), 0.0)
                    vh = v[rs, cs].astype(BF16)
                    o = _dot(scores, vh) + _dot_nt(q_state[:, cs], states[h])
                    states[h] = states[h] * decay[:, cs] + _dot_tn(vh, k_state[:, cs])
                    ms = jnp.mean(o * o, axis=-1, keepdims=True)
                    o_scr[rs, cs] = o * lax.rsqrt(ms + RMS_EPS)
            for h in range(n_heads):
                st[s, h] = states[h]
    o_ref[...] = (o_scr[...] * nw_ref[...] * _silu(hg_ref[...])).astype(BF16)

    if not batched:
        @pl.when(tstep == pl.num_programs(1) - 1)
        def _():
            for s in range(n_seq):
                for h in range(n_heads):
                    so_ref[s, h] = st[s, h].T[0:n_keys, :]


def _seq_layout(n_batch, seq_len, row_off, sample):
    if sample:
        n_seq, n_chunk, C = SAMPLE_SEQS, 1, seq_len
        rows = n_seq * C
        grid = (n_batch // n_seq, 1)
        blk0 = row_off // rows
        rb = lambda i, t: blk0 + i
    else:
        n_seq, n_chunk, C = 1, PROMPT_TILE // SCAN_CHUNK, SCAN_CHUNK
        rows = PROMPT_TILE
        tiles = seq_len // rows
        grid = (n_batch, tiles)
        blk0 = row_off // rows
        rb = lambda i, t: blk0 + i * tiles + t
    return n_seq, n_chunk, C, rows, grid, rb


def _gla_call(mode, h, cols, extra, nw, s0, out_buf, n_batch, seq_len, row_off, sample):
    n_seq, n_chunk, C, rows, grid, rb = _seq_layout(n_batch, seq_len, row_off, sample)
    colspec = lambda c0, w: pl.BlockSpec((rows, w), lambda i, t: (rb(i, t), c0 // w))
    fix2 = lambda i, t: (0, 0)
    in_specs = [colspec(cols[0], 512), colspec(cols[1], 512), colspec(cols[2], 512), colspec(cols[3], 512)]
    args = [h, h, h, h]
    if mode == "gla":
        wlr, blr = extra
        in_specs += [colspec(cols[4], LANE), pl.BlockSpec((LANE, 512), fix2), pl.BlockSpec((1, 512), fix2)]
        args += [h, wlr, blr]
    else:
        in_specs += [pl.BlockSpec((1, 512), fix2)]
        args += [extra]
    n_keys = s0.shape[2]
    st_spec = pl.BlockSpec((n_seq, 4, n_keys, HEAD_W), lambda i, t: (i, 0, 0, 0))
    in_specs += [pl.BlockSpec((1, 512), fix2), st_spec, pl.BlockSpec(memory_space=pl.ANY)]
    args += [nw, s0, out_buf]
    o_spec = pl.BlockSpec((rows, 512), lambda i, t: (rb(i, t), 0))
    return pl.pallas_call(
        functools.partial(_gla_body, mode, n_seq, n_chunk, C),
        grid=grid,
        in_specs=in_specs,
        out_specs=[o_spec, st_spec],
        out_shape=[jax.ShapeDtypeStruct(out_buf.shape, out_buf.dtype),
                   jax.ShapeDtypeStruct((n_batch, 4, n_keys, HEAD_W), F32)],
        scratch_shapes=[pltpu.VMEM((n_seq, 4, HEAD_W, HEAD_W), F32), pltpu.VMEM((rows, 512), F32)],
        input_output_aliases={len(args) - 1: 0},
        compiler_params=_cparams(2),
        name=mode + ("_sample" if sample else "_prompt"),
    )(*args)


def _round_bf16(x, on=True):
    return x.astype(BF16).astype(F32) if on else x


def _conf_body(n_seq, L, round_x, round_w, a_ref, gt_ref, hist_ref, w_ref, b_ref, g_ref, be_ref, _, o_ref, co_ref,
               buf, bufr, y_scr, win):
    tstep = pl.program_id(1)
    hist = CONF_WIDTH - 1
    pad = 32 - hist

    @pl.when(tstep == 0)
    def _():
        for s in range(n_seq):
            buf[s, pad:32, :] = hist_ref[s]
            bufr[s, pad:32, :] = _round_bf16(hist_ref[s], round_x)

    u = a_ref[...] * jax.nn.sigmoid(gt_ref[...])
    ur = _round_bf16(u, round_x)
    for s in range(n_seq):
        buf[s, 32:32 + L, :] = u[s * L:(s + 1) * L, :]
        bufr[s, 32:32 + L, :] = ur[s * L:(s + 1) * L, :]
    w = _round_bf16(w_ref[...], round_w)
    for s in range(n_seq):
        acc = jnp.zeros((L, CONF_DIM), F32)
        for phase in range(8):
            n_taps = (CONF_WIDTH - 1 - phase) // 8 + 1
            span = L + 8 * (n_taps - 1)
            win[phase, 0:span, :] = bufr[s, pad + phase:pad + phase + span, :]
            for a in range(n_taps):
                j = 8 * a + phase
                acc = acc + win[phase, 8 * a:8 * a + L, :] * w[j:j + 1, :]
        y_scr[s * L:(s + 1) * L, :] = _silu(_layernorm(acc + b_ref[...], g_ref[...], be_ref[...]))
        tail = buf[s, L + pad:L + 32, :]
        buf[s, pad:32, :] = tail
        tailr = bufr[s, L + pad:L + 32, :]
        bufr[s, pad:32, :] = tailr
    o_ref[...] = y_scr[...].astype(o_ref.dtype)

    @pl.when(tstep == pl.num_programs(1) - 1)
    def _():
        for s in range(n_seq):
            co_ref[s] = buf[s, pad:32, :]


def _conf_call(h, col_a, col_g, cache, w, b, g, be, out_buf, n_batch, seq_len, row_off, sample):
    n_seq, n_chunk, C, rows, grid, rb = _seq_layout(n_batch, seq_len, row_off, sample)
    L = rows // n_seq
    hist = CONF_WIDTH - 1
    colspec = lambda c0: pl.BlockSpec((rows, 512), lambda i, t: (rb(i, t), c0 // 512))
    fix2 = lambda i, t: (0, 0)
    c_spec = pl.BlockSpec((n_seq, hist, CONF_DIM), lambda i, t: (i, 0, 0))
    return pl.pallas_call(
        functools.partial(_conf_body, n_seq, L, True, sample),
        grid=grid,
        in_specs=[colspec(col_a), colspec(col_g), c_spec,
                  pl.BlockSpec((CONF_WIDTH, CONF_DIM), fix2), pl.BlockSpec((1, CONF_DIM), fix2),
                  pl.BlockSpec((1, CONF_DIM), fix2), pl.BlockSpec((1, CONF_DIM), fix2),
                  pl.BlockSpec(memory_space=pl.ANY)],
        out_specs=[pl.BlockSpec((rows, 512), lambda i, t: (rb(i, t), 0)), c_spec],
        out_shape=[jax.ShapeDtypeStruct(out_buf.shape, out_buf.dtype),
                   jax.ShapeDtypeStruct((n_batch, hist, CONF_DIM), F32)],
        scratch_shapes=[pltpu.VMEM((n_seq, 32 + L, CONF_DIM), F32)] * 2 + [pltpu.VMEM((rows, CONF_DIM), F32),
                                                                           pltpu.VMEM((8, L + 24, CONF_DIM), F32)],
        input_output_aliases={7: 0},
        compiler_params=_cparams(2),
        name="conformer" + ("_sample" if sample else "_prompt"),
    )(h, h, cache, w, b, g, be, out_buf)


def _ssd_body(n_seq, n_chunk, C, round_x, round_w, hz_ref, hx_ref, hdt_ref, hist_ref, s0_ref, cw_ref, cb_ref, dtb_ref, alog_ref,
              dvec_ref, nw_ref, _, o_ref, co_ref, so_ref, st, buf, bufr, xbc, y_scr):
    tstep = pl.program_id(1)
    L = n_chunk * C
    hist = SSM_CONV - 1
    pad = 8 - hist
    n_pairs = SSM_HEADS // 2

    @pl.when(tstep == 0)
    def _():
        for s in range(n_seq):
            buf[s, pad:8, :] = hist_ref[s]
            bufr[s, pad:8, :] = _round_bf16(hist_ref[s], round_x)
            for m in range(n_pairs):
                st[s, m] = s0_ref[s, m]

    cw = _round_bf16(cw_ref[...], round_w)
    for s in range(n_seq):
        hx = hx_ref[s * L:(s + 1) * L, :]
        buf[s, 8:8 + L, :] = hx
        bufr[s, 8:8 + L, :] = _round_bf16(hx, round_x)
        acc = jnp.zeros((L, SSM_CONV_DIM), F32)
        for j in range(SSM_CONV):
            acc = acc + bufr[s, pad + j:pad + j + L, :] * cw[j:j + 1, :]
        xbc[s * L:(s + 1) * L, :] = _silu(acc + cb_ref[...])
        tail = buf[s, L + pad:L + 8, :]
        buf[s, pad:8, :] = tail
        tailr = bufr[s, L + pad:L + 8, :]
        bufr[s, pad:8, :] = tailr

    dt = _softplus(hdt_ref[...] + dtb_ref[...])
    la = dt * (-jnp.exp(alog_ref[...]))
    hrow = lax.broadcasted_iota(I32, (LANE, SSM_INNER), 0)
    hcol = lax.broadcasted_iota(I32, (LANE, SSM_INNER), 1) // SSM_HEADDIM
    expand = (hrow == hcol).astype(BF16)
    dtx = functools.reduce(lambda p, q: p + q,
                           [jnp.dot(part, expand, preferred_element_type=F32) for part in _split3(dt)])
    causal = _tri(C)
    tri = causal.astype(BF16)
    lane = lax.broadcasted_iota(I32, (C, HEAD_W), 1)
    bcol_all = _chunk_cumsum(la, C)
    heads_per_group = SSM_HEADS // SSM_GROUPS
    for s in range(n_seq):
        states = [st[s, m] for m in range(n_pairs)]
        for c in range(n_chunk):
            r0 = (s * n_chunk + c) * C
            rs = slice(r0, r0 + C)
            bcol = bcol_all[rs, :]
            brow = functools.reduce(lambda p, q: p + q, [
                lax.dot_general(part, tri, (((0,), (1,)), ((), ())), preferred_element_type=F32)
                for part in _split3(la[rs, :])])
            xs_c = xbc[rs, 0:SSM_INNER]
            v_c = (xs_c * dtx[rs, :]).astype(BF16)
            gmats, bms, cms = [], [], []
            for grp in range(SSM_GROUPS):
                bm = xbc[rs, SSM_INNER + grp * SSM_STATE:SSM_INNER + (grp + 1) * SSM_STATE]
                cm = xbc[rs, SSM_INNER + (SSM_GROUPS + grp) * SSM_STATE:SSM_INNER + (SSM_GROUPS + grp + 1) * SSM_STATE]
                cm_hi, cm_lo = _split2(cm)
                bm_hi, bm_lo = _split2(bm)
                gmats.append(_dot_nt(jnp.concatenate([cm_hi, cm_hi, cm_lo], axis=1),
                                     jnp.concatenate([bm_hi, bm_lo, bm_hi], axis=1)))
                bms.append(bm)
                cms.append(cm)
            for m in range(n_pairs):
                grp = (2 * m) // heads_per_group
                bm, cm, gmat = bms[grp], cms[grp], gmats[grp]
                ps = slice(m * HEAD_W, (m + 1) * HEAD_W)
                vp = v_c[:, ps]
                s_t = states[m]
                scores, queries, keys, decays = [], [], [], []
                for hh in range(2):
                    hd = 2 * m + hh
                    bc = bcol[:, hd:hd + 1]
                    br = brow[hd:hd + 1, :]
                    b_last = bcol[C - 1:C, hd:hd + 1]
                    scores.append(gmat * jnp.where(causal, jnp.exp(jnp.minimum(bc - br, 0.0)), 0.0))
                    queries.append(cm * jnp.exp(bc))
                    keys.append(bm * jnp.exp(b_last - bc))
                    decays.append(jnp.exp(b_last))
                o_stack = (_dot(jnp.concatenate(scores, axis=0), vp)
                           + _dot_nt(jnp.concatenate(queries, axis=0), s_t))
                kv = _dot_tn(vp, jnp.concatenate(keys, axis=1))
                d = SSM_HEADDIM
                states[m] = jnp.concatenate([s_t[0:d, :] * decays[0] + kv[0:d, 0:SSM_STATE],
                                             s_t[d:, :] * decays[1] + kv[d:, SSM_STATE:]], axis=0)
                o_pair = jnp.where(lane < SSM_HEADDIM, o_stack[0:C, :], o_stack[C:, :])
                y_scr[rs, ps] = o_pair + dvec_ref[:, ps] * xs_c[:, ps]
        for m in range(n_pairs):
            st[s, m] = states[m]
    y = y_scr[...] * _silu(hz_ref[...])
    gw = SSM_INNER // SSM_GROUPS
    for grp in range(SSM_GROUPS):
        gs = slice(grp * gw, (grp + 1) * gw)
        yg = y[:, gs]
        ms = jnp.mean(yg * yg, axis=-1, keepdims=True)
        o_ref[:, gs] = (yg * lax.rsqrt(ms + RMS_EPS) * nw_ref[:, gs]).astype(BF16)

    @pl.when(tstep == pl.num_programs(1) - 1)
    def _():
        for s in range(n_seq):
            co_ref[s] = buf[s, pad:8, :]
            for m in range(n_pairs):
                so_ref[s, m] = st[s, m]


def _ssd_call(h, col_z, col_x, col_dt, cache, s0, cw, cb, dtb, alog, dvec, nw, out_buf, n_batch, seq_len, row_off,
              sample):
    n_seq, n_chunk, C, rows, grid, rb = _seq_layout(n_batch, seq_len, row_off, sample)
    L = rows // n_seq
    hist = SSM_CONV - 1
    n_pairs = SSM_HEADS // 2
    colspec = lambda c0, w: pl.BlockSpec((rows, w), lambda i, t: (rb(i, t), c0 // w))
    fix2 = lambda i, t: (0, 0)
    c_spec = pl.BlockSpec((n_seq, hist, SSM_CONV_DIM), lambda i, t: (i, 0, 0))
    st_spec = pl.BlockSpec((n_seq, n_pairs, HEAD_W, SSM_STATE), lambda i, t: (i, 0, 0, 0))
    return pl.pallas_call(
        functools.partial(_ssd_body, n_seq, n_chunk, C, sample, True),
        grid=grid,
        in_specs=[colspec(col_z, 512), colspec(col_x, SSM_CONV_DIM), colspec(col_dt, LANE), c_spec, st_spec,
                  pl.BlockSpec((SSM_CONV, SSM_CONV_DIM), fix2), pl.BlockSpec((1, SSM_CONV_DIM), fix2),
                  pl.BlockSpec((1, LANE), fix2), pl.BlockSpec((1, LANE), fix2),
                  pl.BlockSpec((1, SSM_INNER), fix2), pl.BlockSpec((1, SSM_INNER), fix2),
                  pl.BlockSpec(memory_space=pl.ANY)],
        out_specs=[pl.BlockSpec((rows, 512), lambda i, t: (rb(i, t), 0)), c_spec, st_spec],
        out_shape=[jax.ShapeDtypeStruct(out_buf.shape, out_buf.dtype),
                   jax.ShapeDtypeStruct((n_batch, hist, SSM_CONV_DIM), F32),
                   jax.ShapeDtypeStruct((n_batch, n_pairs, HEAD_W, SSM_STATE), F32)],
        scratch_shapes=[pltpu.VMEM((n_seq, n_pairs, HEAD_W, SSM_STATE), F32),
                        pltpu.VMEM((n_seq, 8 + L, SSM_CONV_DIM), F32),
                        pltpu.VMEM((n_seq, 8 + L, SSM_CONV_DIM), F32),
                        pltpu.VMEM((rows, SSM_CONV_DIM), F32),
                        pltpu.VMEM((rows, SSM_INNER), F32)],
        input_output_aliases={11: 0},
        compiler_params=_cparams(2),
        name="ssd" + ("_sample" if sample else "_prompt"),
    )(h, h, h, cache, s0, cw, cb, dtb, alog, dvec, nw, out_buf)


def _pad_heads(w, n_heads, width):
    lead = w.shape[:-1]
    w = w.reshape(lead + (n_heads, width))
    w = jnp.pad(w, [(0, 0)] * len(lead) + [(0, 0), (0, HEAD_W - width)])
    return w.reshape(lead + (n_heads * HEAD_W,))


def _row(v):
    return v.reshape(1, -1).astype(F32)


def kernel(x_prompt, x_sample, state_gla, cache_conformer, state_hgrn, state_ssm, cache_mamba_conv, w_in_even, w_gla_gate_lr, b_gla_gate, gla_norm_w, conf_conv_w, conf_conv_b, conf_ln_g, conf_ln_b, w_out_even, w_in_odd, hgrn_lower_bounds, hgrn_norm_w, mamba_conv_w, mamba_conv_b, mamba_dt_bias, mamba_a_log, mamba_d, mamba_norm_w, w_out_odd, ln1_g, ln1_b, ln2_g, ln2_b, router_w, router_b, expert_w_gate, expert_b_gate, expert_w_up, expert_b_up, expert_w_down, expert_b_down):
    bp, lp, _ = x_prompt.shape
    bs, ls, _ = x_sample.shape
    tp, ts = bp * lp, bs * ls
    x = (x_prompt.reshape(tp, D_MODEL), x_sample.reshape(ts, D_MODEL))

    def router_params(layer):
        return router_w[layer].T.astype(BF16), router_b[layer].astype(F32).reshape(N_EXPERTS, 1)

    def finish_layer(layer, x, mix_a, mix_b, w_out):
        rwt, rb = router_params(layer)
        x1, xp, gates, eidx, rank, counts = _outproj_ln_router(
            mix_a, mix_b, x[0], x[1], w_out[:512].astype(BF16), w_out[512:].astype(BF16),
            _row(ln1_g[layer]), _row(ln1_b[layer]), rwt, rb)
        return _moe(layer, x1, xp, gates, eidx, rank, counts, _row(ln2_g[layer]), _row(ln2_b[layer]),
                    expert_w_gate, expert_b_gate, expert_w_up, expert_b_up, expert_w_down, expert_b_down, tp)

    mix_init = jnp.zeros((tp + ts, 512), BF16)

    wi = w_in_even[0]
    wq, wk, wv, wg, wlr, wglu = jnp.split(wi, [256, 512, 1024, 1536, 1552], axis=1)
    w_even = jnp.concatenate([_pad_heads(wq, GLA_HEADS, GLA_DK), _pad_heads(wk, GLA_HEADS, GLA_DK), wv, wg, wglu,
                              jnp.pad(wlr, ((0, 0), (0, LANE - GLA_RANK)))], axis=1).astype(BF16)
    cols_gla = (0, 512, 1024, 1536, 3072)
    col_a, col_gate = 2048, 2560
    h = _inproj(x[0], x[1], w_even)
    wlr_p = jnp.pad(_pad_heads(w_gla_gate_lr[0], GLA_HEADS, GLA_DK), ((0, LANE - GLA_RANK), (0, 0)))
    blr_p = _row(_pad_heads(b_gla_gate[0], GLA_HEADS, GLA_DK))
    nw = _row(gla_norm_w[0])
    conf_args = (conf_conv_w[0], _row(conf_conv_b[0]), _row(conf_ln_g[0]), _row(conf_ln_b[0]))
    s0_p = jnp.zeros((bp, GLA_HEADS, GLA_DK, HEAD_W), F32)
    s0_s = state_gla[0]
    mix_a, sg_p = _gla_call("gla", h, cols_gla, (wlr_p, blr_p), nw, s0_p, mix_init, bp, lp, 0, False)
    mix_a, sg_s = _gla_call("gla", h, cols_gla, (wlr_p, blr_p), nw, s0_s, mix_a, bs, ls, tp, True)
    mix_b, cc_p = _conf_call(h, col_a, col_gate, jnp.zeros((bp,) + cache_conformer.shape[2:], F32), *conf_args,
                             mix_init, bp, lp, 0, False)
    mix_b, cc_s = _conf_call(h, col_a, col_gate, cache_conformer[0], *conf_args, mix_b, bs, ls, tp, True)
    x = finish_layer(0, x, mix_a, mix_b, w_out_even[0])
    gla_p, gla_s = sg_p[None], sg_s[None]
    conf_p, conf_s = cc_p[None], cc_s[None]

    lb_cum = jnp.cumsum(jax.nn.softmax(hgrn_lower_bounds.astype(F32), axis=0), axis=0)
    lower_bound = _row((lb_cum - lb_cum[0])[1])
    wo = w_in_odd[0]
    w_odd = jnp.concatenate([wo[:, 2560:3584], wo[:, :2560],
                             jnp.pad(wo[:, 3584:], ((0, 0), (0, LANE - SSM_HEADS)))], axis=1).astype(BF16)
    h = _inproj(x[0], x[1], w_odd)
    cols_hgrn = (1024, 1536, 2048, 2560)
    col_z, col_x, col_dt = 3072, 0, 3584
    nw = _row(hgrn_norm_w[0])
    mix_a, sh_p = _gla_call("hgrn", h, cols_hgrn, lower_bound, nw,
                            jnp.zeros((bp, HGRN_HEADS, HEAD_W, HEAD_W), F32), mix_init, bp, lp, 0, False)
    mix_a, sh_s = _gla_call("hgrn", h, cols_hgrn, lower_bound, nw, state_hgrn[0], mix_a, bs, ls, tp, True)

    def pair_states(s):
        return jnp.swapaxes(s, 2, 3).reshape(s.shape[0], SSM_HEADS // 2, HEAD_W, SSM_STATE)

    def unpair_states(s):
        return jnp.swapaxes(s.reshape(s.shape[0], SSM_HEADS, SSM_HEADDIM, SSM_STATE), 2, 3)

    pad8 = lambda v: jnp.pad(v.astype(F32), (0, LANE - SSM_HEADS)).reshape(1, LANE)
    ssd_args = (mamba_conv_w[0], _row(mamba_conv_b[0]), pad8(mamba_dt_bias[0]), pad8(mamba_a_log[0]),
                _row(jnp.repeat(mamba_d[0], SSM_HEADDIM)), _row(mamba_norm_w[0]))
    mix_b, cm_p, ss_p = _ssd_call(h, col_z, col_x, col_dt, jnp.zeros((bp,) + cache_mamba_conv.shape[2:], F32),
                                  jnp.zeros((bp, SSM_HEADS // 2, HEAD_W, SSM_STATE), F32), *ssd_args,
                                  mix_init, bp, lp, 0, False)
    mix_b, cm_s, ss_s = _ssd_call(h, col_z, col_x, col_dt, cache_mamba_conv[0], pair_states(state_ssm[0]),
                                  *ssd_args, mix_b, bs, ls, tp, True)
    y_prompt, y_sample = finish_layer(1, x, mix_a, mix_b, w_out_odd[0])
    y_prompt = y_prompt.reshape(bp, lp, D_MODEL)
    y_sample = y_sample.reshape(bs, ls, D_MODEL)
    return (y_prompt, y_sample, gla_p, gla_s, conf_p, conf_s, sh_p[None], sh_s[None],
            unpair_states(ss_p)[None], unpair_states(ss_s)[None], cm_p[None], cm_s[None])
```

```python
import functools

import jax
import jax.numpy as jnp
from jax import lax
from jax.experimental import pallas as pl
from jax.experimental.pallas import tpu as pltpu
from jax.experimental.pallas import tpu_sc as plsc

F32 = jnp.float32
BF16 = jnp.bfloat16
I32 = jnp.int32
U32 = jnp.uint32

D_MODEL = 1024
DEPTH = 2
DEEPNORM_ALPHA = (2.0 * DEPTH) ** 0.25
LN_EPS = 1e-5
RMS_EPS = 1e-6
LANE = 128
HEAD_W = 128
GLA_HEADS, GLA_DK, GLA_RANK, GLA_TAU = 4, 64, 16, 16.0
CONF_DIM, CONF_WIDTH = 512, 31
HGRN_HEADS, HGRN_DK = 4, 128
SSM_HEADS, SSM_HEADDIM, SSM_STATE, SSM_GROUPS, SSM_CONV = 8, 64, 128, 2, 4
SSM_INNER = SSM_HEADS * SSM_HEADDIM
SSM_CONV_DIM = SSM_INNER + 2 * SSM_GROUPS * SSM_STATE
N_EXPERTS, TOP_K = 32, 4
SWIGLU_ALPHA, SWIGLU_LIMIT = 1.702, 7.0
SCAN_CHUNK = 64
PROMPT_TILE = 512
SSD_PROMPT_TILE = 512
SAMPLE_SEQS = 16
TOKEN_TILE = 512
WIDE_TOKEN_TILE = 1024
MOE_ROWS = 512
IN_SLOTS = 4
SC_CORES, SC_SUBCORES = 2, 16
SC_ROWS = 64
SC_SCATTER_ROWS = 32
VMEM_LIMIT = 56 * 1024 * 1024


def _cparams(n_axes):
    return pltpu.CompilerParams(dimension_semantics=("arbitrary",) * n_axes, vmem_limit_bytes=VMEM_LIMIT)


def _silu(x):
    return x * jax.nn.sigmoid(x)


def _softplus(x):
    return jnp.maximum(x, 0.0) + jnp.log(1.0 + jnp.exp(-jnp.abs(x)))


def _log_sigmoid(x):
    return jnp.minimum(x, 0.0) - jnp.log(1.0 + jnp.exp(-jnp.abs(x)))


def _layernorm(y, g, b):
    mu = jnp.mean(y, axis=-1, keepdims=True)
    d = y - mu
    var = jnp.mean(d * d, axis=-1, keepdims=True)
    return d * lax.rsqrt(var + LN_EPS) * g + b


def _dot(a, b):
    return jnp.dot(a.astype(BF16), b.astype(BF16), preferred_element_type=F32)


def _dot_nt(a, b):
    return lax.dot_general(a.astype(BF16), b.astype(BF16), (((1,), (1,)), ((), ())), preferred_element_type=F32)


def _dot_tn(a, b):
    return lax.dot_general(a.astype(BF16), b.astype(BF16), (((0,), (0,)), ((), ())), preferred_element_type=F32)


def _tri(c):
    r = lax.broadcasted_iota(I32, (c, c), 0)
    k = lax.broadcasted_iota(I32, (c, c), 1)
    return r >= k


def _pair_specs(tm, n_first, width):
    return [pl.BlockSpec((tm, width), lambda i: (jnp.minimum(i, n_first - 1), 0)),
            pl.BlockSpec((tm, width), lambda i: (jnp.maximum(i - n_first, 0), 0))]


def _pair_tile(n_first, xa_ref, xb_ref):
    return jnp.where(pl.program_id(0) < n_first, xa_ref[...], xb_ref[...])


def _inproj_body(n_first, xa_ref, xb_ref, w_ref, o_ref):
    xb = _pair_tile(n_first, xa_ref, xb_ref).astype(BF16)
    n = w_ref.shape[1]
    for c0 in range(0, n, 512):
        c1 = min(c0 + 512, n)
        o_ref[:, c0:c1] = jnp.dot(xb, w_ref[:, c0:c1], preferred_element_type=F32)


def _inproj(xa, xb, w):
    k, n = w.shape
    t = xa.shape[0] + xb.shape[0]
    n_first = xa.shape[0] // TOKEN_TILE
    return pl.pallas_call(
        functools.partial(_inproj_body, n_first),
        grid=(t // TOKEN_TILE,),
        in_specs=_pair_specs(TOKEN_TILE, n_first, k) + [pl.BlockSpec((k, n), lambda i: (0, 0))],
        out_specs=pl.BlockSpec((TOKEN_TILE, n), lambda i: (i, 0)),
        out_shape=jax.ShapeDtypeStruct((t, n), F32),
        compiler_params=_cparams(1),
        name="inproj",
    )(xa, xb, w)


def _pack_halves(y):
    half = y.shape[1] // 2
    hi = lax.bitcast_convert_type(y[:, :half].astype(BF16).astype(F32), U32)
    lo = lax.bitcast_convert_type(y[:, half:].astype(BF16).astype(F32), U32)
    return (hi & jnp.uint32(0xFFFF0000)) | (lo >> 16)


def _unpack_halves(w):
    hi = lax.bitcast_convert_type(w & jnp.uint32(0xFFFF0000), F32)
    lo = lax.bitcast_convert_type(w << 16, F32)
    return hi, lo


def _outproj_body(n_first, a_ref, b_ref, xa_ref, xb_ref, wa_ref, wb_ref, g_ref, be_ref, rwt_ref, rb_ref,
                  x1_ref, xp_ref, gate_ref, idx_ref, rank_ref, cnt_ref, carry):
    @pl.when(pl.program_id(0) == 0)
    def _():
        carry[...] = jnp.zeros(carry.shape, F32)

    mix = (jnp.dot(a_ref[...], wa_ref[...], preferred_element_type=F32)
           + jnp.dot(b_ref[...], wb_ref[...], preferred_element_type=F32))
    x1 = _layernorm(DEEPNORM_ALPHA * _pair_tile(n_first, xa_ref, xb_ref) + mix, g_ref[...], be_ref[...])
    x1_ref[...] = x1
    xp_ref[...] = _pack_halves(x1)
    logits = _dot_nt(rwt_ref[...], x1) + rb_ref[...]
    tm = logits.shape[1]
    expert = lax.broadcasted_iota(I32, logits.shape, 0)
    vals, idxs = [], []
    for _ in range(TOP_K):
        m = jnp.max(logits, axis=0, keepdims=True)
        sel = jnp.min(jnp.where(logits == m, expert, N_EXPERTS), axis=0, keepdims=True)
        vals.append(m)
        idxs.append(sel)
        logits = jnp.where(expert == sel, -jnp.inf, logits)
    exps = [jnp.exp(v - vals[0]) for v in vals]
    inv = 1.0 / functools.reduce(lambda p, q: p + q, exps)
    chosen = jnp.zeros(logits.shape, F32)
    for k in range(TOP_K):
        chosen = chosen + (expert == idxs[k]).astype(F32)
    earlier = lax.broadcasted_iota(I32, (tm, tm), 0) < lax.broadcasted_iota(I32, (tm, tm), 1)
    before = carry[...] + jnp.dot(chosen.astype(BF16), earlier.astype(BF16), preferred_element_type=F32)
    choice = lax.broadcasted_iota(I32, (8, tm), 0)
    gates = jnp.zeros((8, tm), F32)
    eidx = jnp.zeros((8, tm), I32)
    ranks = jnp.zeros((8, tm), F32)
    for k in range(TOP_K):
        rk = jnp.sum(jnp.where(expert == idxs[k], before, 0.0), axis=0, keepdims=True)
        gates = jnp.where(choice == k, exps[k] * inv, gates)
        eidx = jnp.where(choice == k, idxs[k], eidx)
        ranks = jnp.where(choice == k, rk, ranks)
    gate_ref[...] = gates
    idx_ref[...] = eidx
    rank_ref[...] = ranks.astype(I32)
    carry[...] = carry[...] + jnp.sum(chosen, axis=1, keepdims=True)
    cnt_ref[...] = carry[...].astype(I32)


def _outproj_ln_router(a, b, xa, xb, wa, wb, g, be, rwt, rb):
    t = xa.shape[0] + xb.shape[0]
    tm = WIDE_TOKEN_TILE
    n_first = xa.shape[0] // tm
    row = lambda i: (i, 0)
    col = lambda i: (0, i)
    fix = lambda i: (0, 0)
    return pl.pallas_call(
        functools.partial(_outproj_body, n_first),
        grid=(t // tm,),
        in_specs=[pl.BlockSpec((tm, 512), row), pl.BlockSpec((tm, 512), row)] + _pair_specs(tm, n_first, D_MODEL)
        + [pl.BlockSpec((512, D_MODEL), fix), pl.BlockSpec((512, D_MODEL), fix),
           pl.BlockSpec((1, D_MODEL), fix), pl.BlockSpec((1, D_MODEL), fix),
           pl.BlockSpec((N_EXPERTS, D_MODEL), fix), pl.BlockSpec((N_EXPERTS, 1), fix)],
        out_specs=[pl.BlockSpec((tm, D_MODEL), row), pl.BlockSpec((tm, 512), row),
                   pl.BlockSpec((8, tm), col), pl.BlockSpec((8, tm), col), pl.BlockSpec((8, tm), col),
                   pl.BlockSpec((N_EXPERTS, 1), fix)],
        out_shape=[jax.ShapeDtypeStruct((t, D_MODEL), F32), jax.ShapeDtypeStruct((t, 512), U32),
                   jax.ShapeDtypeStruct((8, t), F32), jax.ShapeDtypeStruct((8, t), I32),
                   jax.ShapeDtypeStruct((8, t), I32), jax.ShapeDtypeStruct((N_EXPERTS, 1), I32)],
        scratch_shapes=[pltpu.VMEM((N_EXPERTS, 1), F32)],
        compiler_params=_cparams(1),
        name="outproj_ln_router",
    )(a, b, xa, xb, wa, wb, g, be, rwt, rb)


def _sc_mesh():
    return plsc.VectorSubcoreMesh(core_axis_name="c", subcore_axis_name="s")


def _sc_scatter_rows(src, dest, n_out):
    n_src, w = src.shape
    n_dst = dest.shape[1]
    workers = SC_CORES * SC_SUBCORES
    per_worker = n_src // workers
    chunks = per_worker // SC_SCATTER_ROWS
    assert n_src == workers * chunks * SC_SCATTER_ROWS
    idx = dest.T.reshape(n_dst, workers, chunks, SC_SCATTER_ROWS)

    @functools.partial(pl.kernel, mesh=_sc_mesh(), out_type=jax.ShapeDtypeStruct((n_out, w), src.dtype),
                       scratch_types=[pltpu.VMEM((n_dst, chunks, SC_SCATTER_ROWS), I32)]
                       + [pltpu.VMEM((SC_SCATTER_ROWS, w), src.dtype)] * 2 + [pltpu.SemaphoreType.DMA] * 4)
    def scatter(src_hbm, idx_hbm, out_hbm, idx_v, rows_a, rows_b, sem_ra, sem_rb, sem_wa, sem_wb):
        worker = lax.axis_index("s") * SC_CORES + lax.axis_index("c")
        base = worker * per_worker
        for k in range(n_dst):
            pltpu.sync_copy(idx_hbm.at[k, worker], idx_v.at[k])

        def read(c, rows, sem):
            return pltpu.async_copy(src_hbm.at[pl.ds(pl.multiple_of(base + c * SC_SCATTER_ROWS, 8), SC_SCATTER_ROWS)],
                                    rows, sem)

        def write_all(pending_read, c, rows, sem):
            pending_read.wait()
            return [pltpu.async_copy(rows, out_hbm.at[idx_v.at[k, c]], sem) for k in range(n_dst)]

        @pl.loop(0, chunks // 2)
        def _(p):
            read_a = read(2 * p, rows_a, sem_ra)
            read_b = read(2 * p + 1, rows_b, sem_rb)
            writes = write_all(read_a, 2 * p, rows_a, sem_wa) + write_all(read_b, 2 * p + 1, rows_b, sem_wb)
            for wr in writes:
                wr.wait()

        if chunks % 2:
            for wr in write_all(read(chunks - 1, rows_a, sem_ra), chunks - 1, rows_a, sem_wa):
                wr.wait()

    return scatter(src, idx)


def _sc_gather_rows(table, idx):
    n, w = idx.shape[0], table.shape[1]
    workers = SC_CORES * SC_SUBCORES
    chunks = n // (workers * SC_ROWS)
    assert n == workers * chunks * SC_ROWS and chunks % 2 == 0
    idx = idx.reshape(workers, chunks, SC_ROWS)

    @functools.partial(pl.kernel, mesh=_sc_mesh(), out_type=jax.ShapeDtypeStruct((n, w), table.dtype),
                       scratch_types=[pltpu.VMEM((chunks, SC_ROWS), I32)] + [pltpu.VMEM((SC_ROWS, w), table.dtype)] * 2
                       + [pltpu.SemaphoreType.DMA] * 4)
    def gather(table_hbm, idx_hbm, out_hbm, idx_v, rows_a, rows_b, sem_ra, sem_rb, sem_wa, sem_wb):
        worker = lax.axis_index("s") * SC_CORES + lax.axis_index("c")
        base = worker * (chunks * SC_ROWS)
        pltpu.sync_copy(idx_hbm.at[worker], idx_v)

        def out_rows(c):
            return out_hbm.at[pl.ds(pl.multiple_of(base + c * SC_ROWS, 8), SC_ROWS)]

        @pl.loop(0, chunks // 2)
        def _(p):
            read_a = pltpu.async_copy(table_hbm.at[idx_v.at[2 * p]], rows_a, sem_ra)
            read_b = pltpu.async_copy(table_hbm.at[idx_v.at[2 * p + 1]], rows_b, sem_rb)
            read_a.wait()
            write_a = pltpu.async_copy(rows_a, out_rows(2 * p), sem_wa)
            read_b.wait()
            write_b = pltpu.async_copy(rows_b, out_rows(2 * p + 1), sem_wb)
            write_a.wait()
            write_b.wait()

    return gather(table, idx)


def _experts_body(b0_ref, nb_ref, last_ref, nt_ref, xs_hbm, wg_ref, bg_ref, wu_ref, bu_ref, wd_ref, bd_ref, o_hbm,
                  wg_s, wu_s, wd_s, xbuf, obuf, sem_in, sem_out):
    e = pl.program_id(0)
    first_blk, n_blk, last_valid, n_total = b0_ref[e], nb_ref[e], last_ref[e], nt_ref[0]
    quarter = MOE_ROWS // 4

    def rows_of(g):
        return pl.ds(pl.multiple_of(g * MOE_ROWS, MOE_ROWS), MOE_ROWS)

    def fetch(g, slot):
        return pltpu.make_async_copy(xs_hbm.at[rows_of(g)], xbuf.at[slot], sem_in.at[slot])

    def put(g, slot):
        return pltpu.make_async_copy(obuf.at[slot], o_hbm.at[rows_of(g)], sem_out.at[slot])

    lead = IN_SLOTS - 1
    for first in range(lead):
        @pl.when((e == 0) & (n_total > first))
        def _():
            fetch(first, first).start()

    @pl.when(n_blk > 0)
    def _():
        wg_s[...] = wg_ref[...].astype(BF16)
        wu_s[...] = wu_ref[...].astype(BF16)
        wd_s[...] = wd_ref[...].astype(BF16)

    def compute(islot, slot, rows):
        half = D_MODEL // 2
        x_hi, x_lo = _unpack_halves(xbuf[islot, 0:rows, :])
        x_hi = x_hi.astype(BF16)
        x_lo = x_lo.astype(BF16)
        g = (jnp.dot(x_hi, wg_s[:half, :], preferred_element_type=F32)
             + jnp.dot(x_lo, wg_s[half:, :], preferred_element_type=F32) + bg_ref[...])
        u = (jnp.dot(x_hi, wu_s[:half, :], preferred_element_type=F32)
             + jnp.dot(x_lo, wu_s[half:, :], preferred_element_type=F32) + bu_ref[...])
        g = jnp.minimum(g, SWIGLU_LIMIT)
        u = jnp.clip(u, -SWIGLU_LIMIT, SWIGLU_LIMIT)
        hmid = (u + 1.0) * (g * jax.nn.sigmoid(SWIGLU_ALPHA * g))
        out = jnp.dot(hmid.astype(BF16), wd_s[...], preferred_element_type=F32) + bd_ref[...]
        obuf[slot, 0:rows, :] = _pack_halves(out)

    def block(j, carry):
        g = first_blk + j
        slot = lax.rem(g, 2)
        islot = lax.rem(g, IN_SLOTS)
        fetch(g, islot).wait()

        @pl.when(g + lead < n_total)
        def _():
            fetch(g + lead, lax.rem(g + lead, IN_SLOTS)).start()

        @pl.when(g >= 2)
        def _():
            put(g - 2, slot).wait()

        valid = jnp.where(j == n_blk - 1, last_valid, MOE_ROWS)

        for rows in range(quarter, MOE_ROWS + 1, quarter):
            @pl.when((valid > rows - quarter) & (valid <= rows))
            def _():
                compute(islot, slot, rows)
                if rows < MOE_ROWS:
                    obuf[slot, rows:, :] = jnp.zeros((MOE_ROWS - rows, obuf.shape[2]), obuf.dtype)

        put(g, slot).start()
        return carry

    lax.fori_loop(0, n_blk, block, 0)

    @pl.when((e == N_EXPERTS - 1) & (n_total >= 2))
    def _():
        put(n_total - 2, lax.rem(n_total, 2)).wait()

    @pl.when((e == N_EXPERTS - 1) & (n_total >= 1))
    def _():
        put(n_total - 1, lax.rem(n_total - 1, 2)).wait()


def _experts(layer, first_blk, n_blk, last_valid, xs, wg, bg, wu, bu, wd, bd):
    n_rows, w = xs.shape
    wsel = lambda e, b0, nb, lv, nt: (layer, e, 0, 0)
    wspec = pl.BlockSpec((None, None, D_MODEL, D_MODEL), wsel)
    bspec = pl.BlockSpec((None, None, 1, D_MODEL), wsel)
    bias = lambda b: b.reshape(b.shape[0], b.shape[1], 1, b.shape[2])
    return pl.pallas_call(
        _experts_body,
        grid_spec=pltpu.PrefetchScalarGridSpec(
            num_scalar_prefetch=4,
            grid=(N_EXPERTS,),
            in_specs=[pl.BlockSpec(memory_space=pl.ANY), wspec, bspec, wspec, bspec, wspec, bspec],
            out_specs=pl.BlockSpec(memory_space=pl.ANY),
            scratch_shapes=[pltpu.VMEM((D_MODEL, D_MODEL), BF16)] * 3
            + [pltpu.VMEM((IN_SLOTS, MOE_ROWS, w), U32), pltpu.VMEM((2, MOE_ROWS, w), U32),
               pltpu.SemaphoreType.DMA((IN_SLOTS,)), pltpu.SemaphoreType.DMA((2,))],
        ),
        out_shape=jax.ShapeDtypeStruct((n_rows, w), U32),
        compiler_params=_cparams(1),
        name="experts",
    )(first_blk, n_blk, last_valid, jnp.sum(n_blk).reshape(1), xs, wg, bias(bg), wu, bias(bu), wd, bias(bd))


def _combine_body(n_first, o0_ref, o1_ref, o2_ref, o3_ref, gt_ref, x_ref, g_ref, b_ref, ya_ref, yb_ref=None):
    half = D_MODEL // 2
    gates = gt_ref[...]
    hi = jnp.zeros((x_ref.shape[0], half), F32)
    lo = jnp.zeros((x_ref.shape[0], half), F32)
    for k, o_ref in enumerate((o0_ref, o1_ref, o2_ref, o3_ref)):
        h, l = _unpack_halves(o_ref[...])
        gk = gates[:, k:k + 1]
        hi = hi + gk * h
        lo = lo + gk * l
    x = x_ref[...]
    y_hi = DEEPNORM_ALPHA * x[:, :half] + hi
    y_lo = DEEPNORM_ALPHA * x[:, half:] + lo
    mu = (jnp.sum(y_hi, axis=-1, keepdims=True) + jnp.sum(y_lo, axis=-1, keepdims=True)) * (1.0 / D_MODEL)
    d_hi = y_hi - mu
    d_lo = y_lo - mu
    var = (jnp.sum(d_hi * d_hi, axis=-1, keepdims=True) + jnp.sum(d_lo * d_lo, axis=-1, keepdims=True)) * (1.0 / D_MODEL)
    r = lax.rsqrt(var + LN_EPS)
    out_hi = d_hi * r * g_ref[:, :half] + b_ref[:, :half]
    out_lo = d_lo * r * g_ref[:, half:] + b_ref[:, half:]

    def write(y_ref):
        y_ref[:, :half] = out_hi
        y_ref[:, half:] = out_lo

    if yb_ref is None:
        write(ya_ref)
    else:
        pl.when(pl.program_id(0) < n_first)(lambda: write(ya_ref))
        pl.when(pl.program_id(0) >= n_first)(lambda: write(yb_ref))


def _combine_ln(o4, gates, x, g, b, t_first=None):
    t = x.shape[0]
    tm = WIDE_TOKEN_TILE
    row = lambda i: (i, 0)
    fix = lambda i: (0, 0)
    choice = lambda k: pl.BlockSpec((tm, 512), lambda i: (k * (t // tm) + i, 0))
    if t_first is None:
        n_first = None
        out_specs = pl.BlockSpec((tm, D_MODEL), row)
        out_shape = jax.ShapeDtypeStruct((t, D_MODEL), F32)
    else:
        n_first = t_first // tm
        out_specs = [pl.BlockSpec((tm, D_MODEL), lambda i: (jnp.minimum(i, n_first - 1), 0)),
                     pl.BlockSpec((tm, D_MODEL), lambda i: (jnp.maximum(i - n_first, 0), 0))]
        out_shape = [jax.ShapeDtypeStruct((t_first, D_MODEL), F32), jax.ShapeDtypeStruct((t - t_first, D_MODEL), F32)]
    return pl.pallas_call(
        functools.partial(_combine_body, n_first),
        grid=(t // tm,),
        in_specs=[choice(0), choice(1), choice(2), choice(3), pl.BlockSpec((tm, TOP_K), row),
                  pl.BlockSpec((tm, D_MODEL), row), pl.BlockSpec((1, D_MODEL), fix), pl.BlockSpec((1, D_MODEL), fix)],
        out_specs=out_specs,
        out_shape=out_shape,
        compiler_params=_cparams(1),
        name="combine_ln",
    )(o4, o4, o4, o4, gates, x, g, b)


def _moe(layer, x1, xp, gates, eidx, rank, counts, ln_g, ln_b, wg, bg, wu, bu, wd, bd, t_first=None):
    t = x1.shape[0]
    bm = MOE_ROWS
    n_blocks = t * TOP_K // bm + N_EXPERTS
    n_rows = n_blocks * bm
    cnt = counts[:, 0]
    padded = (cnt + bm - 1) // bm * bm
    pad_end = jnp.cumsum(padded)
    pad_start = pad_end - padded
    e = eidx[:TOP_K]
    start = jnp.sum(jnp.where(e[:, :, None] == jnp.arange(N_EXPERTS, dtype=I32), pad_start, 0), axis=-1)
    dest = (start + rank[:TOP_K]).T
    n_blk = padded // bm
    last_valid = cnt - (n_blk - 1) * bm
    xs = _sc_scatter_rows(xp, dest, n_rows)
    outs = _experts(layer, pad_start // bm, n_blk, last_valid, xs, wg, bg, wu, bu, wd, bd)
    o4 = _sc_gather_rows(outs, dest.T.reshape(-1))
    return _combine_ln(o4, gates[:TOP_K].T, x1, ln_g, ln_b, t_first)


def _split2(x):
    hi = x.astype(BF16)
    return hi, (x - hi.astype(F32)).astype(BF16)


def _split3(x):
    hi = x.astype(BF16)
    rem = x - hi.astype(F32)
    mid = rem.astype(BF16)
    return hi, mid, (rem - mid.astype(F32)).astype(BF16)


def _chunk_cumsum(g, C):
    rows = g.shape[0]
    r = lax.broadcasted_iota(I32, (rows, rows), 0)
    c = lax.broadcasted_iota(I32, (rows, rows), 1)
    tri = ((r >= c) & (r // C == c // C)).astype(BF16)
    hi, mid, lo = _split3(g)
    dot = lambda part: jnp.dot(tri, part, preferred_element_type=F32)
    return dot(hi) + dot(mid) + dot(lo)


def _gla_batched_step(q, k, v, g, C, n_seq, n_heads, state_of, o_scr):
    rows = n_seq * C
    wide = n_seq * HEAD_W
    mid = max(C // 2 - 1, 0)
    r = lax.broadcasted_iota(I32, (rows, rows), 0)
    c = lax.broadcasted_iota(I32, (rows, rows), 1)
    same = (r // C) == (c // C)
    causal = same & (r >= c)
    parts = _split3(g)
    summed = lambda mask: functools.reduce(lambda p, q_: p + q_, [
        jnp.dot(mask.astype(BF16), part, preferred_element_type=F32) for part in parts])
    b = summed(causal)
    b_mid = summed(same & ((c % C) <= mid))
    b_last = summed(same)
    qe_hi, qe_lo = _split2(q * jnp.exp(b - b_mid))
    ke_hi, ke_lo = _split2(k * jnp.exp(b_mid - b))
    q_state = (q * jnp.exp(b)).astype(BF16)
    k_state = (k * jnp.exp(b_last - b)).astype(BF16)
    decay_parts = _split3(jnp.exp(b_last))
    row_w = lax.broadcasted_iota(I32, (rows, wide), 0)
    blk_w = lax.broadcasted_iota(I32, (rows, wide), 1) // HEAD_W
    own = (row_w // C) == blk_w
    pick = (row_w == blk_w * C).astype(BF16)
    new_states = []
    for h in range(n_heads):
        cs = slice(h * HEAD_W, (h + 1) * HEAD_W)
        lhs = jnp.concatenate([qe_hi[:, cs], qe_hi[:, cs], qe_lo[:, cs]], axis=1)
        rhs = jnp.concatenate([ke_hi[:, cs], ke_lo[:, cs], ke_hi[:, cs]], axis=1)
        scores = jnp.where(causal, _dot_nt(lhs, rhs), 0.0)
        vh = v[:, cs].astype(BF16)
        s_cat = jnp.concatenate([state_of(s, h) for s in range(n_seq)], axis=1)
        o_full = _dot(q_state[:, cs], s_cat)
        o_state = jnp.concatenate([o_full[s * C:(s + 1) * C, s * HEAD_W:(s + 1) * HEAD_W] for s in range(n_seq)],
                                  axis=0)
        o = _dot(scores, vh) + o_state
        v_wide = jnp.where(own, jnp.concatenate([vh] * n_seq, axis=1), jnp.zeros((), BF16))
        kv = _dot_tn(k_state[:, cs], v_wide)
        decay = functools.reduce(lambda p, q_: p + q_, [
            lax.dot_general(part[:, cs], pick, (((0,), (0,)), ((), ())), preferred_element_type=F32)
            for part in decay_parts])
        new_states.append(s_cat * decay + kv)
        ms = jnp.mean(o * o, axis=-1, keepdims=True)
        o_scr[:, cs] = o * lax.rsqrt(ms + RMS_EPS)
    return new_states


def _gla_body(mode, n_seq, n_chunk, C, *refs):
    if mode == "gla":
        hq_ref, hk_ref, hv_ref, hg_ref, hlr_ref, wlr_ref, blr_ref, nw_ref, s0_ref, _, o_ref, so_ref, st, o_scr = refs
    else:
        hq_ref, hk_ref, hv_ref, hg_ref, lb_ref, nw_ref, s0_ref, _, o_ref, so_ref, st, o_scr = refs
    n_heads = 4
    n_keys = s0_ref.shape[2]
    tstep = pl.program_id(1)
    batched = n_chunk == 1 and n_seq > 1

    def padded_state(s, h):
        s_in = s0_ref[s, h]
        if n_keys < HEAD_W:
            s_in = jnp.concatenate([s_in, jnp.zeros((HEAD_W - n_keys, HEAD_W), F32)], axis=0)
        return s_in

    if not batched:
        @pl.when(tstep == 0)
        def _():
            for s in range(n_seq):
                for h in range(n_heads):
                    st[s, h] = padded_state(s, h).T

    if mode == "gla":
        q = hq_ref[...] * (GLA_DK ** -0.5)
        k = hk_ref[...]
        z = _dot(hlr_ref[...], wlr_ref[...]) + blr_ref[...]
        g = _log_sigmoid(z) * (1.0 / GLA_TAU)
    else:
        q = _silu(hq_ref[...]) * (HGRN_DK ** -0.5)
        lb = lb_ref[...]
        f = lb + (1.0 - lb) * jax.nn.sigmoid(hk_ref[...])
        k = 1.0 - f
        g = jnp.log(f)
    v = hv_ref[...]
    if batched:
        new_states = _gla_batched_step(q, k, v, g, C, n_seq, n_heads, padded_state, o_scr)
        for s in range(n_seq):
            for h in range(n_heads):
                so_ref[s, h] = new_states[h][0:n_keys, s * HEAD_W:(s + 1) * HEAD_W]
    else:
        causal = _tri(C)
        mid = max(C // 2 - 1, 0)
        b_all = _chunk_cumsum(g, C)
        for s in range(n_seq):
            states = [st[s, h] for h in range(n_heads)]
            for c in range(n_chunk):
                r0 = (s * n_chunk + c) * C
                rs = slice(r0, r0 + C)
                b, qc, kc = b_all[rs, :], q[rs, :], k[rs, :]
                b_last = b[C - 1:C, :]
                b_mid = b[mid:mid + 1, :]
                qe_hi, qe_lo = _split2(qc * jnp.exp(b - b_mid))
                ke_hi, ke_lo = _split2(kc * jnp.exp(b_mid - b))
                q_state = (qc * jnp.exp(b)).astype(BF16)
                k_state = (kc * jnp.exp(b_last - b)).astype(BF16)
                decay = jnp.exp(b_last)
                for h in range(n_heads):
                    cs = slice(h * HEAD_W, (h + 1) * HEAD_W)
                    lhs = jnp.concatenate([qe_hi[:, cs], qe_hi[:, cs], qe_lo[:, cs]], axis=1)
                    rhs = jnp.concatenate([ke_hi[:, cs], ke_lo[:, cs], ke_hi[:, cs]], axis=1)
                    scores = jnp.where(causal, _dot_nt(lhs, rhs), 0.0)
                    vh = v[rs, cs].astype(BF16)
                    o = _dot(scores, vh) + _dot_nt(q_state[:, cs], states[h])
                    states[h] = states[h] * decay[:, cs] + _dot_tn(vh, k_state[:, cs])
                    ms = jnp.mean(o * o, axis=-1, keepdims=True)
                    o_scr[rs, cs] = o * lax.rsqrt(ms + RMS_EPS)
            for h in range(n_heads):
                st[s, h] = states[h]
    o_ref[...] = (o_scr[...] * nw_ref[...] * _silu(hg_ref[...])).astype(BF16)

    if not batched:
        @pl.when(tstep == pl.num_programs(1) - 1)
        def _():
            for s in range(n_seq):
                for h in range(n_heads):
                    so_ref[s, h] = st[s, h].T[0:n_keys, :]


def _with_tail_fill(body, n_real, out_index):
    def wrapped(*refs):
        pl.when(pl.program_id(0) < n_real)(lambda: body(*refs))

        @pl.when(pl.program_id(0) >= n_real)
        def _():
            refs[out_index][...] = jnp.zeros(refs[out_index].shape, refs[out_index].dtype)

    return wrapped


def _seq_layout(n_batch, seq_len, row_off, sample, prompt_tile=PROMPT_TILE, total_rows=None):
    if sample:
        n_seq, n_chunk, C = SAMPLE_SEQS, 1, seq_len
        rows = n_seq * C
        grid = (n_batch // n_seq, 1)
        blk0 = row_off // rows
        rb = lambda i, t: blk0 + i
    else:
        n_seq, n_chunk, C = 1, prompt_tile // SCAN_CHUNK, SCAN_CHUNK
        rows = prompt_tile
        tiles = seq_len // rows
        grid = (n_batch, tiles)
        blk0 = row_off // rows
        rb = lambda i, t: blk0 + i * tiles + t
    seq_blk = lambda i: i
    if total_rows is not None:
        assert 0 < total_rows - (row_off + n_batch * seq_len) <= (rows if sample else seq_len)
        n_real, last, rb_real = grid[0], total_rows // rows - 1, rb
        grid = (n_real + 1, grid[1])
        rb = lambda i, t: jnp.minimum(rb_real(i, t), last)
        seq_blk = lambda i: jnp.minimum(i, n_real - 1)
    return n_seq, n_chunk, C, rows, grid, rb, seq_blk


def _mix_target(out_buf, alias_index):
    if hasattr(out_buf, "shape"):
        return out_buf, jax.ShapeDtypeStruct(out_buf.shape, out_buf.dtype), {alias_index: 0}, None
    return jnp.zeros((8, LANE), BF16), jax.ShapeDtypeStruct((out_buf, 512), BF16), {}, out_buf


def _gla_call(mode, h, cols, extra, nw, s0, out_buf, n_batch, seq_len, row_off, sample):
    n_alias = 9 if mode == "gla" else 7
    buf_in, out_sds, aliases, total = _mix_target(out_buf, n_alias)
    n_seq, n_chunk, C, rows, grid, rb, seq_blk = _seq_layout(n_batch, seq_len, row_off, sample, total_rows=total)
    colspec = lambda c0, w: pl.BlockSpec((rows, w), lambda i, t: (rb(i, t), c0 // w))
    fix2 = lambda i, t: (0, 0)
    in_specs = [colspec(cols[0], 512), colspec(cols[1], 512), colspec(cols[2], 512), colspec(cols[3], 512)]
    args = [h, h, h, h]
    if mode == "gla":
        wlr, blr = extra
        in_specs += [colspec(cols[4], LANE), pl.BlockSpec((LANE, 512), fix2), pl.BlockSpec((1, 512), fix2)]
        args += [h, wlr, blr]
    else:
        in_specs += [pl.BlockSpec((1, 512), fix2)]
        args += [extra]
    n_keys = s0.shape[2]
    st_spec = pl.BlockSpec((n_seq, 4, n_keys, HEAD_W), lambda i, t: (seq_blk(i), 0, 0, 0))
    in_specs += [pl.BlockSpec((1, 512), fix2), st_spec, pl.BlockSpec(memory_space=pl.ANY)]
    args += [nw, s0, buf_in]
    assert len(args) - 1 == n_alias
    o_spec = pl.BlockSpec((rows, 512), lambda i, t: (rb(i, t), 0))
    body = functools.partial(_gla_body, mode, n_seq, n_chunk, C)
    if total is not None:
        body = _with_tail_fill(body, grid[0] - 1, len(args))
    return pl.pallas_call(
        body,
        grid=grid,
        in_specs=in_specs,
        out_specs=[o_spec, st_spec],
        out_shape=[out_sds, jax.ShapeDtypeStruct((n_batch, 4, n_keys, HEAD_W), F32)],
        scratch_shapes=[pltpu.VMEM((n_seq, 4, HEAD_W, HEAD_W), F32), pltpu.VMEM((rows, 512), F32)],
        input_output_aliases=aliases,
        compiler_params=_cparams(2),
        name=mode + ("_sample" if sample else "_prompt"),
    )(*args)


def _round_bf16(x, on=True):
    return x.astype(BF16).astype(F32) if on else x


def _conf_body(n_seq, L, round_x, round_w, a_ref, gt_ref, hist_ref, w_ref, b_ref, g_ref, be_ref, _, o_ref, co_ref,
               buf, bufr, y_scr, win):
    tstep = pl.program_id(1)
    hist = CONF_WIDTH - 1
    pad = 32 - hist

    @pl.when(tstep == 0)
    def _():
        for s in range(n_seq):
            buf[s, pad:32, :] = hist_ref[s]
            bufr[s, pad:32, :] = _round_bf16(hist_ref[s], round_x)

    u = a_ref[...] * jax.nn.sigmoid(gt_ref[...])
    ur = _round_bf16(u, round_x)
    for s in range(n_seq):
        buf[s, 32:32 + L, :] = u[s * L:(s + 1) * L, :]
        bufr[s, 32:32 + L, :] = ur[s * L:(s + 1) * L, :]
    w = _round_bf16(w_ref[...], round_w)
    for s in range(n_seq):
        acc = jnp.zeros((L, CONF_DIM), F32)
        for phase in range(8):
            n_taps = (CONF_WIDTH - 1 - phase) // 8 + 1
            span = L + 8 * (n_taps - 1)
            win[phase, 0:span, :] = bufr[s, pad + phase:pad + phase + span, :]
            for a in range(n_taps):
                j = 8 * a + phase
                acc = acc + win[phase, 8 * a:8 * a + L, :] * w[j:j + 1, :]
        y_scr[s * L:(s + 1) * L, :] = _silu(_layernorm(acc + b_ref[...], g_ref[...], be_ref[...]))
        tail = buf[s, L + pad:L + 32, :]
        buf[s, pad:32, :] = tail
        tailr = bufr[s, L + pad:L + 32, :]
        bufr[s, pad:32, :] = tailr
    o_ref[...] = y_scr[...].astype(o_ref.dtype)

    @pl.when(tstep == pl.num_programs(1) - 1)
    def _():
        for s in range(n_seq):
            co_ref[s] = buf[s, pad:32, :]


def _conf_call(h, col_a, col_g, cache, w, b, g, be, out_buf, n_batch, seq_len, row_off, sample):
    buf_in, out_sds, aliases, total = _mix_target(out_buf, 7)
    n_seq, n_chunk, C, rows, grid, rb, seq_blk = _seq_layout(n_batch, seq_len, row_off, sample, total_rows=total)
    L = rows // n_seq
    hist = CONF_WIDTH - 1
    body = functools.partial(_conf_body, n_seq, L, True, sample)
    if total is not None:
        body = _with_tail_fill(body, grid[0] - 1, 8)
    colspec = lambda c0: pl.BlockSpec((rows, 512), lambda i, t: (rb(i, t), c0 // 512))
    fix2 = lambda i, t: (0, 0)
    c_spec = pl.BlockSpec((n_seq, hist, CONF_DIM), lambda i, t: (seq_blk(i), 0, 0))
    return pl.pallas_call(
        body,
        grid=grid,
        in_specs=[colspec(col_a), colspec(col_g), c_spec,
                  pl.BlockSpec((CONF_WIDTH, CONF_DIM), fix2), pl.BlockSpec((1, CONF_DIM), fix2),
                  pl.BlockSpec((1, CONF_DIM), fix2), pl.BlockSpec((1, CONF_DIM), fix2),
                  pl.BlockSpec(memory_space=pl.ANY)],
        out_specs=[pl.BlockSpec((rows, 512), lambda i, t: (rb(i, t), 0)), c_spec],
        out_shape=[out_sds, jax.ShapeDtypeStruct((n_batch, hist, CONF_DIM), F32)],
        scratch_shapes=[pltpu.VMEM((n_seq, 32 + L, CONF_DIM), F32)] * 2 + [pltpu.VMEM((rows, CONF_DIM), F32),
                                                                           pltpu.VMEM((8, L + 24, CONF_DIM), F32)],
        input_output_aliases=aliases,
        compiler_params=_cparams(2),
        name="conformer" + ("_sample" if sample else "_prompt"),
    )(h, h, cache, w, b, g, be, buf_in)


def _ssd_body(n_seq, n_chunk, C, round_x, round_w, hz_ref, hx_ref, hdt_ref, hist_ref, s0_ref, cw_ref, cb_ref, dtb_ref, alog_ref,
              dvec_ref, nw_ref, _, o_ref, co_ref, so_ref, st, buf, bufr, xbc, y_scr):
    tstep = pl.program_id(1)
    L = n_chunk * C
    hist = SSM_CONV - 1
    pad = 8 - hist
    n_pairs = SSM_HEADS // 2

    @pl.when(tstep == 0)
    def _():
        for s in range(n_seq):
            buf[s, pad:8, :] = hist_ref[s]
            bufr[s, pad:8, :] = _round_bf16(hist_ref[s], round_x)
            for m in range(n_pairs):
                st[s, m] = s0_ref[s, m]

    cw = _round_bf16(cw_ref[...], round_w)
    for s in range(n_seq):
        hx = hx_ref[s * L:(s + 1) * L, :]
        buf[s, 8:8 + L, :] = hx
        bufr[s, 8:8 + L, :] = _round_bf16(hx, round_x)
        acc = jnp.zeros((L, SSM_CONV_DIM), F32)
        for j in range(SSM_CONV):
            acc = acc + bufr[s, pad + j:pad + j + L, :] * cw[j:j + 1, :]
        xbc[s * L:(s + 1) * L, :] = _silu(acc + cb_ref[...])
        tail = buf[s, L + pad:L + 8, :]
        buf[s, pad:8, :] = tail
        tailr = bufr[s, L + pad:L + 8, :]
        bufr[s, pad:8, :] = tailr

    dt = _softplus(hdt_ref[...] + dtb_ref[...])
    la = dt * (-jnp.exp(alog_ref[...]))
    hrow = lax.broadcasted_iota(I32, (LANE, SSM_INNER), 0)
    hcol = lax.broadcasted_iota(I32, (LANE, SSM_INNER), 1) // SSM_HEADDIM
    expand = (hrow == hcol).astype(BF16)
    dtx = functools.reduce(lambda p, q: p + q,
                           [jnp.dot(part, expand, preferred_element_type=F32) for part in _split3(dt)])
    causal = _tri(C)
    tri = causal.astype(BF16)
    lane = lax.broadcasted_iota(I32, (C, HEAD_W), 1)
    bcol_all = _chunk_cumsum(la, C)
    heads_per_group = SSM_HEADS // SSM_GROUPS
    for s in range(n_seq):
        states = [st[s, m] for m in range(n_pairs)]
        for c in range(n_chunk):
            r0 = (s * n_chunk + c) * C
            rs = slice(r0, r0 + C)
            bcol = bcol_all[rs, :]
            brow = functools.reduce(lambda p, q: p + q, [
                lax.dot_general(part, tri, (((0,), (1,)), ((), ())), preferred_element_type=F32)
                for part in _split3(la[rs, :])])
            xs_c = xbc[rs, 0:SSM_INNER]
            v_c = (xs_c * dtx[rs, :]).astype(BF16)
            gmats, bms, cms = [], [], []
            for grp in range(SSM_GROUPS):
                bm = xbc[rs, SSM_INNER + grp * SSM_STATE:SSM_INNER + (grp + 1) * SSM_STATE]
                cm = xbc[rs, SSM_INNER + (SSM_GROUPS + grp) * SSM_STATE:SSM_INNER + (SSM_GROUPS + grp + 1) * SSM_STATE]
                cm_hi, cm_lo = _split2(cm)
                bm_hi, bm_lo = _split2(bm)
                gmats.append(_dot_nt(jnp.concatenate([cm_hi, cm_hi, cm_lo], axis=1),
                                     jnp.concatenate([bm_hi, bm_lo, bm_hi], axis=1)))
                bms.append(bm)
                cms.append(cm)
            for m in range(n_pairs):
                grp = (2 * m) // heads_per_group
                bm, cm, gmat = bms[grp], cms[grp], gmats[grp]
                ps = slice(m * HEAD_W, (m + 1) * HEAD_W)
                vp = v_c[:, ps]
                s_t = states[m]
                scores, queries, keys, decays = [], [], [], []
                for hh in range(2):
                    hd = 2 * m + hh
                    bc = bcol[:, hd:hd + 1]
                    br = brow[hd:hd + 1, :]
                    b_last = bcol[C - 1:C, hd:hd + 1]
                    scores.append(gmat * jnp.where(causal, jnp.exp(jnp.minimum(bc - br, 0.0)), 0.0))
                    queries.append(cm * jnp.exp(bc))
                    keys.append(bm * jnp.exp(b_last - bc))
                    decays.append(jnp.exp(b_last))
                o_stack = (_dot(jnp.concatenate(scores, axis=0), vp)
                           + _dot_nt(jnp.concatenate(queries, axis=0), s_t))
                kv = _dot_tn(vp, jnp.concatenate(keys, axis=1))
                d = SSM_HEADDIM
                states[m] = jnp.concatenate([s_t[0:d, :] * decays[0] + kv[0:d, 0:SSM_STATE],
                                             s_t[d:, :] * decays[1] + kv[d:, SSM_STATE:]], axis=0)
                o_pair = jnp.where(lane < SSM_HEADDIM, o_stack[0:C, :], o_stack[C:, :])
                y_scr[rs, ps] = o_pair + dvec_ref[:, ps] * xs_c[:, ps]
        for m in range(n_pairs):
            st[s, m] = states[m]
    y = y_scr[...] * _silu(hz_ref[...])
    gw = SSM_INNER // SSM_GROUPS
    for grp in range(SSM_GROUPS):
        gs = slice(grp * gw, (grp + 1) * gw)
        yg = y[:, gs]
        ms = jnp.mean(yg * yg, axis=-1, keepdims=True)
        o_ref[:, gs] = (yg * lax.rsqrt(ms + RMS_EPS) * nw_ref[:, gs]).astype(BF16)

    @pl.when(tstep == pl.num_programs(1) - 1)
    def _():
        for s in range(n_seq):
            co_ref[s] = buf[s, pad:8, :]
            for m in range(n_pairs):
                so_ref[s, m] = st[s, m]


def _ssd_call(h, col_z, col_x, col_dt, cache, s0, cw, cb, dtb, alog, dvec, nw, out_buf, n_batch, seq_len, row_off,
              sample):
    buf_in, out_sds, aliases, total = _mix_target(out_buf, 11)
    n_seq, n_chunk, C, rows, grid, rb, seq_blk = _seq_layout(n_batch, seq_len, row_off, sample, SSD_PROMPT_TILE, total)
    L = rows // n_seq
    hist = SSM_CONV - 1
    n_pairs = SSM_HEADS // 2
    body = functools.partial(_ssd_body, n_seq, n_chunk, C, sample, True)
    if total is not None:
        body = _with_tail_fill(body, grid[0] - 1, 12)
    colspec = lambda c0, w: pl.BlockSpec((rows, w), lambda i, t: (rb(i, t), c0 // w))
    fix2 = lambda i, t: (0, 0)
    c_spec = pl.BlockSpec((n_seq, hist, SSM_CONV_DIM), lambda i, t: (seq_blk(i), 0, 0))
    st_spec = pl.BlockSpec((n_seq, n_pairs, HEAD_W, SSM_STATE), lambda i, t: (seq_blk(i), 0, 0, 0))
    return pl.pallas_call(
        body,
        grid=grid,
        in_specs=[colspec(col_z, 512), colspec(col_x, SSM_CONV_DIM), colspec(col_dt, LANE), c_spec, st_spec,
                  pl.BlockSpec((SSM_CONV, SSM_CONV_DIM), fix2), pl.BlockSpec((1, SSM_CONV_DIM), fix2),
                  pl.BlockSpec((1, LANE), fix2), pl.BlockSpec((1, LANE), fix2),
                  pl.BlockSpec((1, SSM_INNER), fix2), pl.BlockSpec((1, SSM_INNER), fix2),
                  pl.BlockSpec(memory_space=pl.ANY)],
        out_specs=[pl.BlockSpec((rows, 512), lambda i, t: (rb(i, t), 0)), c_spec, st_spec],
        out_shape=[out_sds,
                   jax.ShapeDtypeStruct((n_batch, hist, SSM_CONV_DIM), F32),
                   jax.ShapeDtypeStruct((n_batch, n_pairs, HEAD_W, SSM_STATE), F32)],
        scratch_shapes=[pltpu.VMEM((n_seq, n_pairs, HEAD_W, SSM_STATE), F32),
                        pltpu.VMEM((n_seq, 8 + L, SSM_CONV_DIM), F32),
                        pltpu.VMEM((n_seq, 8 + L, SSM_CONV_DIM), F32),
                        pltpu.VMEM((rows, SSM_CONV_DIM), F32),
                        pltpu.VMEM((rows, SSM_INNER), F32)],
        input_output_aliases=aliases,
        compiler_params=_cparams(2),
        name="ssd" + ("_sample" if sample else "_prompt"),
    )(h, h, h, cache, s0, cw, cb, dtb, alog, dvec, nw, buf_in)


def _pad_heads(w, n_heads, width):
    lead = w.shape[:-1]
    w = w.reshape(lead + (n_heads, width))
    w = jnp.pad(w, [(0, 0)] * len(lead) + [(0, 0), (0, HEAD_W - width)])
    return w.reshape(lead + (n_heads * HEAD_W,))


def _row(v):
    return v.reshape(1, -1).astype(F32)


def kernel(x_prompt, x_sample, state_gla, cache_conformer, state_hgrn, state_ssm, cache_mamba_conv, w_in_even, w_gla_gate_lr, b_gla_gate, gla_norm_w, conf_conv_w, conf_conv_b, conf_ln_g, conf_ln_b, w_out_even, w_in_odd, hgrn_lower_bounds, hgrn_norm_w, mamba_conv_w, mamba_conv_b, mamba_dt_bias, mamba_a_log, mamba_d, mamba_norm_w, w_out_odd, ln1_g, ln1_b, ln2_g, ln2_b, router_w, router_b, expert_w_gate, expert_b_gate, expert_w_up, expert_b_up, expert_w_down, expert_b_down):
    bp, lp, _ = x_prompt.shape
    bs, ls, _ = x_sample.shape
    tp, ts = bp * lp, bs * ls
    x = (x_prompt.reshape(tp, D_MODEL), x_sample.reshape(ts, D_MODEL))

    def router_params(layer):
        return router_w[layer].T.astype(BF16), router_b[layer].astype(F32).reshape(N_EXPERTS, 1)

    def finish_layer(layer, x, mix_a, mix_b, w_out):
        rwt, rb = router_params(layer)
        x1, xp, gates, eidx, rank, counts = _outproj_ln_router(
            mix_a, mix_b, x[0], x[1], w_out[:512].astype(BF16), w_out[512:].astype(BF16),
            _row(ln1_g[layer]), _row(ln1_b[layer]), rwt, rb)
        return _moe(layer, x1, xp, gates, eidx, rank, counts, _row(ln2_g[layer]), _row(ln2_b[layer]),
                    expert_w_gate, expert_b_gate, expert_w_up, expert_b_up, expert_w_down, expert_b_down, tp)

    mix_rows = tp + ts

    wi = w_in_even[0]
    wq, wk, wv, wg, wlr, wglu = jnp.split(wi, [256, 512, 1024, 1536, 1552], axis=1)
    w_even = jnp.concatenate([_pad_heads(wq, GLA_HEADS, GLA_DK), _pad_heads(wk, GLA_HEADS, GLA_DK), wv, wg, wglu,
                              jnp.pad(wlr, ((0, 0), (0, LANE - GLA_RANK)))], axis=1).astype(BF16)
    cols_gla = (0, 512, 1024, 1536, 3072)
    col_a, col_gate = 2048, 2560
    h = _inproj(x[0], x[1], w_even)
    wlr_p = jnp.pad(_pad_heads(w_gla_gate_lr[0], GLA_HEADS, GLA_DK), ((0, LANE - GLA_RANK), (0, 0)))
    blr_p = _row(_pad_heads(b_gla_gate[0], GLA_HEADS, GLA_DK))
    nw = _row(gla_norm_w[0])
    conf_args = (conf_conv_w[0], _row(conf_conv_b[0]), _row(conf_ln_g[0]), _row(conf_ln_b[0]))
    s0_p = jnp.zeros((bp, GLA_HEADS, GLA_DK, HEAD_W), F32)
    s0_s = state_gla[0]
    mix_a, sg_p = _gla_call("gla", h, cols_gla, (wlr_p, blr_p), nw, s0_p, mix_rows, bp, lp, 0, False)
    mix_a, sg_s = _gla_call("gla", h, cols_gla, (wlr_p, blr_p), nw, s0_s, mix_a, bs, ls, tp, True)
    mix_b, cc_p = _conf_call(h, col_a, col_gate, jnp.zeros((bp,) + cache_conformer.shape[2:], F32), *conf_args,
                             mix_rows, bp, lp, 0, False)
    mix_b, cc_s = _conf_call(h, col_a, col_gate, cache_conformer[0], *conf_args, mix_b, bs, ls, tp, True)
    x = finish_layer(0, x, mix_a, mix_b, w_out_even[0])
    gla_p, gla_s = sg_p[None], sg_s[None]
    conf_p, conf_s = cc_p[None], cc_s[None]

    lb_cum = jnp.cumsum(jax.nn.softmax(hgrn_lower_bounds.astype(F32), axis=0), axis=0)
    lower_bound = _row((lb_cum - lb_cum[0])[1])
    wo = w_in_odd[0]
    w_odd = jnp.concatenate([wo[:, 2560:3584], wo[:, :2560],
                             jnp.pad(wo[:, 3584:], ((0, 0), (0, LANE - SSM_HEADS)))], axis=1).astype(BF16)
    h = _inproj(x[0], x[1], w_odd)
    cols_hgrn = (1024, 1536, 2048, 2560)
    col_z, col_x, col_dt = 3072, 0, 3584
    nw = _row(hgrn_norm_w[0])
    mix_a, sh_p = _gla_call("hgrn", h, cols_hgrn, lower_bound, nw,
                            jnp.zeros((bp, HGRN_HEADS, HEAD_W, HEAD_W), F32), mix_rows, bp, lp, 0, False)
    mix_a, sh_s = _gla_call("hgrn", h, cols_hgrn, lower_bound, nw, state_hgrn[0], mix_a, bs, ls, tp, True)

    def pair_states(s):
        return jnp.swapaxes(s, 2, 3).reshape(s.shape[0], SSM_HEADS // 2, HEAD_W, SSM_STATE)

    def unpair_states(s):
        return jnp.swapaxes(s.reshape(s.shape[0], SSM_HEADS, SSM_HEADDIM, SSM_STATE), 2, 3)

    pad8 = lambda v: jnp.pad(v.astype(F32), (0, LANE - SSM_HEADS)).reshape(1, LANE)
    ssd_args = (mamba_conv_w[0], _row(mamba_conv_b[0]), pad8(mamba_dt_bias[0]), pad8(mamba_a_log[0]),
                _row(jnp.repeat(mamba_d[0], SSM_HEADDIM)), _row(mamba_norm_w[0]))
    mix_b, cm_p, ss_p = _ssd_call(h, col_z, col_x, col_dt, jnp.zeros((bp,) + cache_mamba_conv.shape[2:], F32),
                                  jnp.zeros((bp, SSM_HEADS // 2, HEAD_W, SSM_STATE), F32), *ssd_args,
                                  mix_rows, bp, lp, 0, False)
    mix_b, cm_s, ss_s = _ssd_call(h, col_z, col_x, col_dt, cache_mamba_conv[0], pair_states(state_ssm[0]),
                                  *ssd_args, mix_b, bs, ls, tp, True)
    y_prompt, y_sample = finish_layer(1, x, mix_a, mix_b, w_out_odd[0])
    y_prompt = y_prompt.reshape(bp, lp, D_MODEL)
    y_sample = y_sample.reshape(bs, ls, D_MODEL)
    return (y_prompt, y_sample, gla_p, gla_s, conf_p, conf_s, sh_p[None], sh_s[None],
            unpair_states(ss_p)[None], unpair_states(ss_s)[None], cm_p[None], cm_s[None])
```

```python
import functools

import jax
import jax.numpy as jnp
from jax import lax
from jax.experimental import pallas as pl
from jax.experimental.pallas import tpu as pltpu
from jax.experimental.pallas import tpu_sc as plsc

F32 = jnp.float32
BF16 = jnp.bfloat16
I32 = jnp.int32
U32 = jnp.uint32

D_MODEL = 1024
DEPTH = 2
DEEPNORM_ALPHA = (2.0 * DEPTH) ** 0.25
LN_EPS = 1e-5
RMS_EPS = 1e-6
LANE = 128
HEAD_W = 128
GLA_HEADS, GLA_DK, GLA_RANK, GLA_TAU = 4, 64, 16, 16.0
CONF_DIM, CONF_WIDTH = 512, 31
HGRN_HEADS, HGRN_DK = 4, 128
SSM_HEADS, SSM_HEADDIM, SSM_STATE, SSM_GROUPS, SSM_CONV = 8, 64, 128, 2, 4
SSM_INNER = SSM_HEADS * SSM_HEADDIM
SSM_CONV_DIM = SSM_INNER + 2 * SSM_GROUPS * SSM_STATE
N_EXPERTS, TOP_K = 32, 4
SWIGLU_ALPHA, SWIGLU_LIMIT = 1.702, 7.0
SCAN_CHUNK = 64
PROMPT_TILE = 512
SSD_PROMPT_TILE = 512
SAMPLE_SEQS = 16
TOKEN_TILE = 512
WIDE_TOKEN_TILE = 1024
MOE_ROWS = 512
IN_SLOTS = 4
SC_CORES, SC_SUBCORES = 2, 16
SC_ROWS = 64
SC_SCATTER_ROWS = 32
VMEM_LIMIT = 56 * 1024 * 1024


def _cparams(n_axes):
    return pltpu.CompilerParams(dimension_semantics=("arbitrary",) * n_axes, vmem_limit_bytes=VMEM_LIMIT)


def _silu(x):
    return x * jax.nn.sigmoid(x)


def _softplus(x):
    return jnp.maximum(x, 0.0) + jnp.log(1.0 + jnp.exp(-jnp.abs(x)))


def _log_sigmoid(x):
    return jnp.minimum(x, 0.0) - jnp.log(1.0 + jnp.exp(-jnp.abs(x)))


def _layernorm(y, g, b):
    mu = jnp.mean(y, axis=-1, keepdims=True)
    d = y - mu
    var = jnp.mean(d * d, axis=-1, keepdims=True)
    return d * lax.rsqrt(var + LN_EPS) * g + b


def _dot(a, b):
    return jnp.dot(a.astype(BF16), b.astype(BF16), preferred_element_type=F32)


def _dot_nt(a, b):
    return lax.dot_general(a.astype(BF16), b.astype(BF16), (((1,), (1,)), ((), ())), preferred_element_type=F32)


def _dot_tn(a, b):
    return lax.dot_general(a.astype(BF16), b.astype(BF16), (((0,), (0,)), ((), ())), preferred_element_type=F32)


def _tri(c):
    r = lax.broadcasted_iota(I32, (c, c), 0)
    k = lax.broadcasted_iota(I32, (c, c), 1)
    return r >= k


def _pair_specs(tm, n_first, width):
    return [pl.BlockSpec((tm, width), lambda i: (jnp.minimum(i, n_first - 1), 0)),
            pl.BlockSpec((tm, width), lambda i: (jnp.maximum(i - n_first, 0), 0))]


def _pair_tile(n_first, xa_ref, xb_ref):
    return jnp.where(pl.program_id(0) < n_first, xa_ref[...], xb_ref[...])


def _inproj_body(n_first, xa_ref, xb_ref, w_ref, o_ref):
    xb = _pair_tile(n_first, xa_ref, xb_ref).astype(BF16)
    n = w_ref.shape[1]
    for c0 in range(0, n, 512):
        c1 = min(c0 + 512, n)
        o_ref[:, c0:c1] = jnp.dot(xb, w_ref[:, c0:c1], preferred_element_type=F32)


def _inproj(xa, xb, w):
    k, n = w.shape
    t = xa.shape[0] + xb.shape[0]
    n_first = xa.shape[0] // TOKEN_TILE
    return pl.pallas_call(
        functools.partial(_inproj_body, n_first),
        grid=(t // TOKEN_TILE,),
        in_specs=_pair_specs(TOKEN_TILE, n_first, k) + [pl.BlockSpec((k, n), lambda i: (0, 0))],
        out_specs=pl.BlockSpec((TOKEN_TILE, n), lambda i: (i, 0)),
        out_shape=jax.ShapeDtypeStruct((t, n), F32),
        compiler_params=_cparams(1),
        name="inproj",
    )(xa, xb, w)


def _pack_halves(y):
    half = y.shape[1] // 2
    hi = lax.bitcast_convert_type(y[:, :half].astype(BF16).astype(F32), U32)
    lo = lax.bitcast_convert_type(y[:, half:].astype(BF16).astype(F32), U32)
    return (hi & jnp.uint32(0xFFFF0000)) | (lo >> 16)


def _unpack_halves(w):
    hi = lax.bitcast_convert_type(w & jnp.uint32(0xFFFF0000), F32)
    lo = lax.bitcast_convert_type(w << 16, F32)
    return hi, lo


def _outproj_body(n_first, a_ref, b_ref, xa_ref, xb_ref, wa_ref, wb_ref, g_ref, be_ref, rwt_ref, rb_ref,
                  x1_ref, xp_ref, gate_ref, idx_ref, rank_ref, cnt_ref, carry):
    @pl.when(pl.program_id(0) == 0)
    def _():
        carry[...] = jnp.zeros(carry.shape, F32)

    mix = (jnp.dot(a_ref[...], wa_ref[...], preferred_element_type=F32)
           + jnp.dot(b_ref[...], wb_ref[...], preferred_element_type=F32))
    x1 = _layernorm(DEEPNORM_ALPHA * _pair_tile(n_first, xa_ref, xb_ref) + mix, g_ref[...], be_ref[...])
    x1_ref[...] = x1
    xp_ref[...] = _pack_halves(x1)
    logits = _dot_nt(rwt_ref[...], x1) + rb_ref[...]
    tm = logits.shape[1]
    expert = lax.broadcasted_iota(I32, logits.shape, 0)
    vals, idxs = [], []
    for _ in range(TOP_K):
        m = jnp.max(logits, axis=0, keepdims=True)
        sel = jnp.min(jnp.where(logits == m, expert, N_EXPERTS), axis=0, keepdims=True)
        vals.append(m)
        idxs.append(sel)
        logits = jnp.where(expert == sel, -jnp.inf, logits)
    exps = [jnp.exp(v - vals[0]) for v in vals]
    inv = 1.0 / functools.reduce(lambda p, q: p + q, exps)
    chosen = jnp.zeros(logits.shape, F32)
    for k in range(TOP_K):
        chosen = chosen + (expert == idxs[k]).astype(F32)
    earlier = lax.broadcasted_iota(I32, (tm, tm), 0) < lax.broadcasted_iota(I32, (tm, tm), 1)
    before = carry[...] + jnp.dot(chosen.astype(BF16), earlier.astype(BF16), preferred_element_type=F32)
    choice = lax.broadcasted_iota(I32, (8, tm), 0)
    gates = jnp.zeros((8, tm), F32)
    eidx = jnp.zeros((8, tm), I32)
    ranks = jnp.zeros((8, tm), F32)
    for k in range(TOP_K):
        rk = jnp.sum(jnp.where(expert == idxs[k], before, 0.0), axis=0, keepdims=True)
        gates = jnp.where(choice == k, exps[k] * inv, gates)
        eidx = jnp.where(choice == k, idxs[k], eidx)
        ranks = jnp.where(choice == k, rk, ranks)
    gate_ref[...] = gates
    idx_ref[...] = eidx
    rank_ref[...] = ranks.astype(I32)
    carry[...] = carry[...] + jnp.sum(chosen, axis=1, keepdims=True)
    cnt_ref[...] = carry[...].astype(I32)


def _outproj_ln_router(a, b, xa, xb, wa, wb, g, be, rwt, rb):
    t = xa.shape[0] + xb.shape[0]
    tm = WIDE_TOKEN_TILE
    n_first = xa.shape[0] // tm
    row = lambda i: (i, 0)
    col = lambda i: (0, i)
    fix = lambda i: (0, 0)
    return pl.pallas_call(
        functools.partial(_outproj_body, n_first),
        grid=(t // tm,),
        in_specs=[pl.BlockSpec((tm, 512), row), pl.BlockSpec((tm, 512), row)] + _pair_specs(tm, n_first, D_MODEL)
        + [pl.BlockSpec((512, D_MODEL), fix), pl.BlockSpec((512, D_MODEL), fix),
           pl.BlockSpec((1, D_MODEL), fix), pl.BlockSpec((1, D_MODEL), fix),
           pl.BlockSpec((N_EXPERTS, D_MODEL), fix), pl.BlockSpec((N_EXPERTS, 1), fix)],
        out_specs=[pl.BlockSpec((tm, D_MODEL), row), pl.BlockSpec((tm, 512), row),
                   pl.BlockSpec((8, tm), col), pl.BlockSpec((8, tm), col), pl.BlockSpec((8, tm), col),
                   pl.BlockSpec((N_EXPERTS, 1), fix)],
        out_shape=[jax.ShapeDtypeStruct((t, D_MODEL), F32), jax.ShapeDtypeStruct((t, 512), U32),
                   jax.ShapeDtypeStruct((8, t), F32), jax.ShapeDtypeStruct((8, t), I32),
                   jax.ShapeDtypeStruct((8, t), I32), jax.ShapeDtypeStruct((N_EXPERTS, 1), I32)],
        scratch_shapes=[pltpu.VMEM((N_EXPERTS, 1), F32)],
        compiler_params=_cparams(1),
        name="outproj_ln_router",
    )(a, b, xa, xb, wa, wb, g, be, rwt, rb)


def _sc_mesh():
    return plsc.VectorSubcoreMesh(core_axis_name="c", subcore_axis_name="s")


def _sc_scatter_rows(src, dest, n_out):
    n_src, w = src.shape
    n_dst = dest.shape[1]
    workers = SC_CORES * SC_SUBCORES
    per_worker = n_src // workers
    chunks = per_worker // SC_SCATTER_ROWS
    assert n_src == workers * chunks * SC_SCATTER_ROWS
    idx = dest.T.reshape(n_dst, workers, chunks, SC_SCATTER_ROWS)

    @functools.partial(pl.kernel, mesh=_sc_mesh(), out_type=jax.ShapeDtypeStruct((n_out, w), src.dtype),
                       scratch_types=[pltpu.VMEM((n_dst, chunks, SC_SCATTER_ROWS), I32)]
                       + [pltpu.VMEM((SC_SCATTER_ROWS, w), src.dtype)] * 2 + [pltpu.SemaphoreType.DMA] * 4)
    def scatter(src_hbm, idx_hbm, out_hbm, idx_v, rows_a, rows_b, sem_ra, sem_rb, sem_wa, sem_wb):
        worker = lax.axis_index("s") * SC_CORES + lax.axis_index("c")
        base = worker * per_worker
        for k in range(n_dst):
            pltpu.sync_copy(idx_hbm.at[k, worker], idx_v.at[k])

        def read(c, rows, sem):
            return pltpu.async_copy(src_hbm.at[pl.ds(pl.multiple_of(base + c * SC_SCATTER_ROWS, 8), SC_SCATTER_ROWS)],
                                    rows, sem)

        def write_all(pending_read, c, rows, sem):
            pending_read.wait()
            return [pltpu.async_copy(rows, out_hbm.at[idx_v.at[k, c]], sem) for k in range(n_dst)]

        @pl.loop(0, chunks // 2)
        def _(p):
            read_a = read(2 * p, rows_a, sem_ra)
            read_b = read(2 * p + 1, rows_b, sem_rb)
            writes = write_all(read_a, 2 * p, rows_a, sem_wa) + write_all(read_b, 2 * p + 1, rows_b, sem_wb)
            for wr in writes:
                wr.wait()

        if chunks % 2:
            for wr in write_all(read(chunks - 1, rows_a, sem_ra), chunks - 1, rows_a, sem_wa):
                wr.wait()

    return scatter(src, idx)


def _sc_gather_rows(table, idx):
    n, w = idx.shape[0], table.shape[1]
    workers = SC_CORES * SC_SUBCORES
    chunks = n // (workers * SC_ROWS)
    assert n == workers * chunks * SC_ROWS and chunks % 2 == 0
    idx = idx.reshape(workers, chunks, SC_ROWS)

    @functools.partial(pl.kernel, mesh=_sc_mesh(), out_type=jax.ShapeDtypeStruct((n, w), table.dtype),
                       scratch_types=[pltpu.VMEM((chunks, SC_ROWS), I32)] + [pltpu.VMEM((SC_ROWS, w), table.dtype)] * 2
                       + [pltpu.SemaphoreType.DMA] * 4)
    def gather(table_hbm, idx_hbm, out_hbm, idx_v, rows_a, rows_b, sem_ra, sem_rb, sem_wa, sem_wb):
        worker = lax.axis_index("s") * SC_CORES + lax.axis_index("c")
        base = worker * (chunks * SC_ROWS)
        pltpu.sync_copy(idx_hbm.at[worker], idx_v)

        def out_rows(c):
            return out_hbm.at[pl.ds(pl.multiple_of(base + c * SC_ROWS, 8), SC_ROWS)]

        @pl.loop(0, chunks // 2)
        def _(p):
            read_a = pltpu.async_copy(table_hbm.at[idx_v.at[2 * p]], rows_a, sem_ra)
            read_b = pltpu.async_copy(table_hbm.at[idx_v.at[2 * p + 1]], rows_b, sem_rb)
            read_a.wait()
            write_a = pltpu.async_copy(rows_a, out_rows(2 * p), sem_wa)
            read_b.wait()
            write_b = pltpu.async_copy(rows_b, out_rows(2 * p + 1), sem_wb)
            write_a.wait()
            write_b.wait()

    return gather(table, idx)


def _experts_body(b0_ref, nb_ref, last_ref, nt_ref, xs_hbm, wg_ref, bg_ref, wu_ref, bu_ref, wd_ref, bd_ref, o_hbm,
                  wg_s, wu_s, wd_s, xbuf, obuf, sem_in, sem_out):
    e = pl.program_id(0)
    first_blk, n_blk, last_valid, n_total = b0_ref[e], nb_ref[e], last_ref[e], nt_ref[0]
    grain = MOE_ROWS // 8

    def rows_of(g):
        return pl.ds(pl.multiple_of(g * MOE_ROWS, MOE_ROWS), MOE_ROWS)

    def fetch(g, slot):
        return pltpu.make_async_copy(xs_hbm.at[rows_of(g)], xbuf.at[slot], sem_in.at[slot])

    def put(g, slot):
        return pltpu.make_async_copy(obuf.at[slot], o_hbm.at[rows_of(g)], sem_out.at[slot])

    lead = IN_SLOTS - 1
    for first in range(lead):
        @pl.when((e == 0) & (n_total > first))
        def _():
            fetch(first, first).start()

    @pl.when(n_blk > 0)
    def _():
        wg_s[...] = wg_ref[...].astype(BF16)
        wu_s[...] = wu_ref[...].astype(BF16)
        wd_s[...] = wd_ref[...].astype(BF16)

    def compute(islot, slot, rows):
        half = D_MODEL // 2
        x_hi, x_lo = _unpack_halves(xbuf[islot, 0:rows, :])
        x_hi = x_hi.astype(BF16)
        x_lo = x_lo.astype(BF16)
        g = (jnp.dot(x_hi, wg_s[:half, :], preferred_element_type=F32)
             + jnp.dot(x_lo, wg_s[half:, :], preferred_element_type=F32) + bg_ref[...])
        u = (jnp.dot(x_hi, wu_s[:half, :], preferred_element_type=F32)
             + jnp.dot(x_lo, wu_s[half:, :], preferred_element_type=F32) + bu_ref[...])
        g = jnp.minimum(g, SWIGLU_LIMIT)
        u = jnp.clip(u, -SWIGLU_LIMIT, SWIGLU_LIMIT)
        hmid = (u + 1.0) * (g * jax.nn.sigmoid(SWIGLU_ALPHA * g))
        out = jnp.dot(hmid.astype(BF16), wd_s[...], preferred_element_type=F32) + bd_ref[...]
        obuf[slot, 0:rows, :] = _pack_halves(out)

    def block(j, carry):
        g = first_blk + j
        slot = lax.rem(g, 2)
        islot = lax.rem(g, IN_SLOTS)
        fetch(g, islot).wait()

        @pl.when(g + lead < n_total)
        def _():
            fetch(g + lead, lax.rem(g + lead, IN_SLOTS)).start()

        @pl.when(g >= 2)
        def _():
            put(g - 2, slot).wait()

        valid = jnp.where(j == n_blk - 1, last_valid, MOE_ROWS)

        for rows in range(grain, MOE_ROWS + 1, grain):
            @pl.when((valid > rows - grain) & (valid <= rows))
            def _():
                compute(islot, slot, rows)
                if rows < MOE_ROWS:
                    obuf[slot, rows:, :] = jnp.zeros((MOE_ROWS - rows, obuf.shape[2]), obuf.dtype)

        put(g, slot).start()
        return carry

    lax.fori_loop(0, n_blk, block, 0)

    @pl.when((e == N_EXPERTS - 1) & (n_total >= 2))
    def _():
        put(n_total - 2, lax.rem(n_total, 2)).wait()

    @pl.when((e == N_EXPERTS - 1) & (n_total >= 1))
    def _():
        put(n_total - 1, lax.rem(n_total - 1, 2)).wait()


def _experts(layer, first_blk, n_blk, last_valid, xs, wg, bg, wu, bu, wd, bd):
    n_rows, w = xs.shape
    wsel = lambda e, b0, nb, lv, nt: (layer, e, 0, 0)
    wspec = pl.BlockSpec((None, None, D_MODEL, D_MODEL), wsel)
    bspec = pl.BlockSpec((None, None, 1, D_MODEL), wsel)
    bias = lambda b: b.reshape(b.shape[0], b.shape[1], 1, b.shape[2])
    return pl.pallas_call(
        _experts_body,
        grid_spec=pltpu.PrefetchScalarGridSpec(
            num_scalar_prefetch=4,
            grid=(N_EXPERTS,),
            in_specs=[pl.BlockSpec(memory_space=pl.ANY), wspec, bspec, wspec, bspec, wspec, bspec],
            out_specs=pl.BlockSpec(memory_space=pl.ANY),
            scratch_shapes=[pltpu.VMEM((D_MODEL, D_MODEL), BF16)] * 3
            + [pltpu.VMEM((IN_SLOTS, MOE_ROWS, w), U32), pltpu.VMEM((2, MOE_ROWS, w), U32),
               pltpu.SemaphoreType.DMA((IN_SLOTS,)), pltpu.SemaphoreType.DMA((2,))],
        ),
        out_shape=jax.ShapeDtypeStruct((n_rows, w), U32),
        compiler_params=_cparams(1),
        name="experts",
    )(first_blk, n_blk, last_valid, jnp.sum(n_blk).reshape(1), xs, wg, bias(bg), wu, bias(bu), wd, bias(bd))


def _combine_body(n_first, o0_ref, o1_ref, o2_ref, o3_ref, gt_ref, x_ref, g_ref, b_ref, ya_ref, yb_ref=None):
    half = D_MODEL // 2
    gates = gt_ref[...]
    hi = jnp.zeros((x_ref.shape[0], half), F32)
    lo = jnp.zeros((x_ref.shape[0], half), F32)
    for k, o_ref in enumerate((o0_ref, o1_ref, o2_ref, o3_ref)):
        h, l = _unpack_halves(o_ref[...])
        gk = gates[:, k:k + 1]
        hi = hi + gk * h
        lo = lo + gk * l
    x = x_ref[...]
    y_hi = DEEPNORM_ALPHA * x[:, :half] + hi
    y_lo = DEEPNORM_ALPHA * x[:, half:] + lo
    mu = (jnp.sum(y_hi, axis=-1, keepdims=True) + jnp.sum(y_lo, axis=-1, keepdims=True)) * (1.0 / D_MODEL)
    d_hi = y_hi - mu
    d_lo = y_lo - mu
    var = (jnp.sum(d_hi * d_hi, axis=-1, keepdims=True) + jnp.sum(d_lo * d_lo, axis=-1, keepdims=True)) * (1.0 / D_MODEL)
    r = lax.rsqrt(var + LN_EPS)
    out_hi = d_hi * r * g_ref[:, :half] + b_ref[:, :half]
    out_lo = d_lo * r * g_ref[:, half:] + b_ref[:, half:]

    def write(y_ref):
        y_ref[:, :half] = out_hi
        y_ref[:, half:] = out_lo

    if yb_ref is None:
        write(ya_ref)
    else:
        pl.when(pl.program_id(0) < n_first)(lambda: write(ya_ref))
        pl.when(pl.program_id(0) >= n_first)(lambda: write(yb_ref))


def _combine_ln(o4, gates, x, g, b, t_first=None):
    t = x.shape[0]
    tm = WIDE_TOKEN_TILE
    row = lambda i: (i, 0)
    fix = lambda i: (0, 0)
    choice = lambda k: pl.BlockSpec((tm, 512), lambda i: (k * (t // tm) + i, 0))
    if t_first is None:
        n_first = None
        out_specs = pl.BlockSpec((tm, D_MODEL), row)
        out_shape = jax.ShapeDtypeStruct((t, D_MODEL), F32)
    else:
        n_first = t_first // tm
        out_specs = [pl.BlockSpec((tm, D_MODEL), lambda i: (jnp.minimum(i, n_first - 1), 0)),
                     pl.BlockSpec((tm, D_MODEL), lambda i: (jnp.maximum(i - n_first, 0), 0))]
        out_shape = [jax.ShapeDtypeStruct((t_first, D_MODEL), F32), jax.ShapeDtypeStruct((t - t_first, D_MODEL), F32)]
    return pl.pallas_call(
        functools.partial(_combine_body, n_first),
        grid=(t // tm,),
        in_specs=[choice(0), choice(1), choice(2), choice(3), pl.BlockSpec((tm, TOP_K), row),
                  pl.BlockSpec((tm, D_MODEL), row), pl.BlockSpec((1, D_MODEL), fix), pl.BlockSpec((1, D_MODEL), fix)],
        out_specs=out_specs,
        out_shape=out_shape,
        compiler_params=_cparams(1),
        name="combine_ln",
    )(o4, o4, o4, o4, gates, x, g, b)


def _moe(layer, x1, xp, gates, eidx, rank, counts, ln_g, ln_b, wg, bg, wu, bu, wd, bd, t_first=None):
    t = x1.shape[0]
    bm = MOE_ROWS
    n_blocks = t * TOP_K // bm + N_EXPERTS
    n_rows = n_blocks * bm
    cnt = counts[:, 0]
    padded = (cnt + bm - 1) // bm * bm
    pad_end = jnp.cumsum(padded)
    pad_start = pad_end - padded
    e = eidx[:TOP_K]
    start = jnp.sum(jnp.where(e[:, :, None] == jnp.arange(N_EXPERTS, dtype=I32), pad_start, 0), axis=-1)
    dest = (start + rank[:TOP_K]).T
    n_blk = padded // bm
    last_valid = cnt - (n_blk - 1) * bm
    xs = _sc_scatter_rows(xp, dest, n_rows)
    outs = _experts(layer, pad_start // bm, n_blk, last_valid, xs, wg, bg, wu, bu, wd, bd)
    o4 = _sc_gather_rows(outs, dest.T.reshape(-1))
    return _combine_ln(o4, gates[:TOP_K].T, x1, ln_g, ln_b, t_first)


def _split2(x):
    hi = x.astype(BF16)
    return hi, (x - hi.astype(F32)).astype(BF16)


def _split3(x):
    hi = x.astype(BF16)
    rem = x - hi.astype(F32)
    mid = rem.astype(BF16)
    return hi, mid, (rem - mid.astype(F32)).astype(BF16)


def _chunk_cumsum(g, C):
    rows = g.shape[0]
    r = lax.broadcasted_iota(I32, (rows, rows), 0)
    c = lax.broadcasted_iota(I32, (rows, rows), 1)
    tri = ((r >= c) & (r // C == c // C)).astype(BF16)
    hi, mid, lo = _split3(g)
    dot = lambda part: jnp.dot(tri, part, preferred_element_type=F32)
    return dot(hi) + dot(mid) + dot(lo)


def _gla_batched_step(q, k, v, g, C, n_seq, n_heads, state_of, o_scr):
    rows = n_seq * C
    wide = n_seq * HEAD_W
    mid = max(C // 2 - 1, 0)
    r = lax.broadcasted_iota(I32, (rows, rows), 0)
    c = lax.broadcasted_iota(I32, (rows, rows), 1)
    same = (r // C) == (c // C)
    causal = same & (r >= c)
    parts = _split3(g)
    summed = lambda mask: functools.reduce(lambda p, q_: p + q_, [
        jnp.dot(mask.astype(BF16), part, preferred_element_type=F32) for part in parts])
    b = summed(causal)
    b_mid = summed(same & ((c % C) <= mid))
    b_last = summed(same)
    qe_hi, qe_lo = _split2(q * jnp.exp(b - b_mid))
    ke_hi, ke_lo = _split2(k * jnp.exp(b_mid - b))
    q_state = (q * jnp.exp(b)).astype(BF16)
    k_state = (k * jnp.exp(b_last - b)).astype(BF16)
    decay_parts = _split3(jnp.exp(b_last))
    row_w = lax.broadcasted_iota(I32, (rows, wide), 0)
    blk_w = lax.broadcasted_iota(I32, (rows, wide), 1) // HEAD_W
    own = (row_w // C) == blk_w
    pick = (row_w == blk_w * C).astype(BF16)
    new_states = []
    for h in range(n_heads):
        cs = slice(h * HEAD_W, (h + 1) * HEAD_W)
        lhs = jnp.concatenate([qe_hi[:, cs], qe_hi[:, cs], qe_lo[:, cs]], axis=1)
        rhs = jnp.concatenate([ke_hi[:, cs], ke_lo[:, cs], ke_hi[:, cs]], axis=1)
        scores = jnp.where(causal, _dot_nt(lhs, rhs), 0.0)
        vh = v[:, cs].astype(BF16)
        s_cat = jnp.concatenate([state_of(s, h) for s in range(n_seq)], axis=1)
        o_full = _dot(q_state[:, cs], s_cat)
        o_state = jnp.concatenate([o_full[s * C:(s + 1) * C, s * HEAD_W:(s + 1) * HEAD_W] for s in range(n_seq)],
                                  axis=0)
        o = _dot(scores, vh) + o_state
        v_wide = jnp.where(own, jnp.concatenate([vh] * n_seq, axis=1), jnp.zeros((), BF16))
        kv = _dot_tn(k_state[:, cs], v_wide)
        decay = functools.reduce(lambda p, q_: p + q_, [
            lax.dot_general(part[:, cs], pick, (((0,), (0,)), ((), ())), preferred_element_type=F32)
            for part in decay_parts])
        new_states.append(s_cat * decay + kv)
        ms = jnp.mean(o * o, axis=-1, keepdims=True)
        o_scr[:, cs] = o * lax.rsqrt(ms + RMS_EPS)
    return new_states


def _gla_body(mode, n_seq, n_chunk, C, *refs):
    if mode == "gla":
        hq_ref, hk_ref, hv_ref, hg_ref, hlr_ref, wlr_ref, blr_ref, nw_ref, s0_ref, _, o_ref, so_ref, st, o_scr = refs
    else:
        hq_ref, hk_ref, hv_ref, hg_ref, lb_ref, nw_ref, s0_ref, _, o_ref, so_ref, st, o_scr = refs
    n_heads = 4
    n_keys = s0_ref.shape[2]
    tstep = pl.program_id(1)
    batched = n_chunk == 1 and n_seq > 1

    def padded_state(s, h):
        s_in = s0_ref[s, h]
        if n_keys < HEAD_W:
            s_in = jnp.concatenate([s_in, jnp.zeros((HEAD_W - n_keys, HEAD_W), F32)], axis=0)
        return s_in

    if not batched:
        @pl.when(tstep == 0)
        def _():
            for s in range(n_seq):
                for h in range(n_heads):
                    st[s, h] = padded_state(s, h).T

    if mode == "gla":
        q = hq_ref[...] * (GLA_DK ** -0.5)
        k = hk_ref[...]
        z = _dot(hlr_ref[...], wlr_ref[...]) + blr_ref[...]
        g = _log_sigmoid(z) * (1.0 / GLA_TAU)
    else:
        q = _silu(hq_ref[...]) * (HGRN_DK ** -0.5)
        lb = lb_ref[...]
        f = lb + (1.0 - lb) * jax.nn.sigmoid(hk_ref[...])
        k = 1.0 - f
        g = jnp.log(f)
    v = hv_ref[...]
    if batched:
        new_states = _gla_batched_step(q, k, v, g, C, n_seq, n_heads, padded_state, o_scr)
        for s in range(n_seq):
            for h in range(n_heads):
                so_ref[s, h] = new_states[h][0:n_keys, s * HEAD_W:(s + 1) * HEAD_W]
    else:
        causal = _tri(C)
        mid = max(C // 2 - 1, 0)
        b_all = _chunk_cumsum(g, C)
        for s in range(n_seq):
            states = [st[s, h] for h in range(n_heads)]
            for c in range(n_chunk):
                r0 = (s * n_chunk + c) * C
                rs = slice(r0, r0 + C)
                b, qc, kc = b_all[rs, :], q[rs, :], k[rs, :]
                b_last = b[C - 1:C, :]
                b_mid = b[mid:mid + 1, :]
                qe_hi, qe_lo = _split2(qc * jnp.exp(b - b_mid))
                ke_hi, ke_lo = _split2(kc * jnp.exp(b_mid - b))
                q_state = (qc * jnp.exp(b)).astype(BF16)
                k_state = (kc * jnp.exp(b_last - b)).astype(BF16)
                decay = jnp.exp(b_last)
                for h in range(n_heads):
                    cs = slice(h * HEAD_W, (h + 1) * HEAD_W)
                    lhs = jnp.concatenate([qe_hi[:, cs], qe_hi[:, cs], qe_lo[:, cs]], axis=1)
                    rhs = jnp.concatenate([ke_hi[:, cs], ke_lo[:, cs], ke_hi[:, cs]], axis=1)
                    scores = jnp.where(causal, _dot_nt(lhs, rhs), 0.0)
                    vh = v[rs, cs].astype(BF16)
                    o = _dot(scores, vh) + _dot_nt(q_state[:, cs], states[h])
                    states[h] = states[h] * decay[:, cs] + _dot_tn(vh, k_state[:, cs])
                    ms = jnp.mean(o * o, axis=-1, keepdims=True)
                    o_scr[rs, cs] = o * lax.rsqrt(ms + RMS_EPS)
            for h in range(n_heads):
                st[s, h] = states[h]
    o_ref[...] = (o_scr[...] * nw_ref[...] * _silu(hg_ref[...])).astype(BF16)

    if not batched:
        @pl.when(tstep == pl.num_programs(1) - 1)
        def _():
            for s in range(n_seq):
                for h in range(n_heads):
                    so_ref[s, h] = st[s, h].T[0:n_keys, :]


def _with_tail_fill(body, n_real, out_index):
    def wrapped(*refs):
        pl.when(pl.program_id(0) < n_real)(lambda: body(*refs))

        @pl.when(pl.program_id(0) >= n_real)
        def _():
            refs[out_index][...] = jnp.zeros(refs[out_index].shape, refs[out_index].dtype)

    return wrapped


def _seq_layout(n_batch, seq_len, row_off, sample, prompt_tile=PROMPT_TILE, total_rows=None):
    if sample:
        n_seq, n_chunk, C = SAMPLE_SEQS, 1, seq_len
        rows = n_seq * C
        grid = (n_batch // n_seq, 1)
        blk0 = row_off // rows
        rb = lambda i, t: blk0 + i
    else:
        n_seq, n_chunk, C = 1, prompt_tile // SCAN_CHUNK, SCAN_CHUNK
        rows = prompt_tile
        tiles = seq_len // rows
        grid = (n_batch, tiles)
        blk0 = row_off // rows
        rb = lambda i, t: blk0 + i * tiles + t
    seq_blk = lambda i: i
    if total_rows is not None:
        assert 0 < total_rows - (row_off + n_batch * seq_len) <= (rows if sample else seq_len)
        n_real, last, rb_real = grid[0], total_rows // rows - 1, rb
        grid = (n_real + 1, grid[1])
        rb = lambda i, t: jnp.minimum(rb_real(i, t), last)
        seq_blk = lambda i: jnp.minimum(i, n_real - 1)
    return n_seq, n_chunk, C, rows, grid, rb, seq_blk


def _mix_target(out_buf, alias_index):
    if hasattr(out_buf, "shape"):
        return out_buf, jax.ShapeDtypeStruct(out_buf.shape, out_buf.dtype), {alias_index: 0}, None
    return jnp.zeros((8, LANE), BF16), jax.ShapeDtypeStruct((out_buf, 512), BF16), {}, out_buf


def _gla_call(mode, h, cols, extra, nw, s0, out_buf, n_batch, seq_len, row_off, sample):
    n_alias = 9 if mode == "gla" else 7
    buf_in, out_sds, aliases, total = _mix_target(out_buf, n_alias)
    n_seq, n_chunk, C, rows, grid, rb, seq_blk = _seq_layout(n_batch, seq_len, row_off, sample, total_rows=total)
    colspec = lambda c0, w: pl.BlockSpec((rows, w), lambda i, t: (rb(i, t), c0 // w))
    fix2 = lambda i, t: (0, 0)
    in_specs = [colspec(cols[0], 512), colspec(cols[1], 512), colspec(cols[2], 512), colspec(cols[3], 512)]
    args = [h, h, h, h]
    if mode == "gla":
        wlr, blr = extra
        in_specs += [colspec(cols[4], LANE), pl.BlockSpec((LANE, 512), fix2), pl.BlockSpec((1, 512), fix2)]
        args += [h, wlr, blr]
    else:
        in_specs += [pl.BlockSpec((1, 512), fix2)]
        args += [extra]
    n_keys = s0.shape[2]
    st_spec = pl.BlockSpec((n_seq, 4, n_keys, HEAD_W), lambda i, t: (seq_blk(i), 0, 0, 0))
    in_specs += [pl.BlockSpec((1, 512), fix2), st_spec, pl.BlockSpec(memory_space=pl.ANY)]
    args += [nw, s0, buf_in]
    assert len(args) - 1 == n_alias
    o_spec = pl.BlockSpec((rows, 512), lambda i, t: (rb(i, t), 0))
    body = functools.partial(_gla_body, mode, n_seq, n_chunk, C)
    if total is not None:
        body = _with_tail_fill(body, grid[0] - 1, len(args))
    return pl.pallas_call(
        body,
        grid=grid,
        in_specs=in_specs,
        out_specs=[o_spec, st_spec],
        out_shape=[out_sds, jax.ShapeDtypeStruct((n_batch, 4, n_keys, HEAD_W), F32)],
        scratch_shapes=[pltpu.VMEM((n_seq, 4, HEAD_W, HEAD_W), F32), pltpu.VMEM((rows, 512), F32)],
        input_output_aliases=aliases,
        compiler_params=_cparams(2),
        name=mode + ("_sample" if sample else "_prompt"),
    )(*args)


def _round_bf16(x, on=True):
    return x.astype(BF16).astype(F32) if on else x


def _conf_body(n_seq, L, round_x, round_w, a_ref, gt_ref, hist_ref, w_ref, b_ref, g_ref, be_ref, _, o_ref, co_ref,
               buf, bufr, y_scr, win):
    tstep = pl.program_id(1)
    hist = CONF_WIDTH - 1
    pad = 32 - hist

    @pl.when(tstep == 0)
    def _():
        for s in range(n_seq):
            buf[s, pad:32, :] = hist_ref[s]
            bufr[s, pad:32, :] = _round_bf16(hist_ref[s], round_x)

    u = a_ref[...] * jax.nn.sigmoid(gt_ref[...])
    ur = _round_bf16(u, round_x)
    for s in range(n_seq):
        buf[s, 32:32 + L, :] = u[s * L:(s + 1) * L, :]
        bufr[s, 32:32 + L, :] = ur[s * L:(s + 1) * L, :]
    w = _round_bf16(w_ref[...], round_w)
    for s in range(n_seq):
        acc = jnp.zeros((L, CONF_DIM), F32)
        for phase in range(8):
            n_taps = (CONF_WIDTH - 1 - phase) // 8 + 1
            span = L + 8 * (n_taps - 1)
            win[phase, 0:span, :] = bufr[s, pad + phase:pad + phase + span, :]
            for a in range(n_taps):
                j = 8 * a + phase
                acc = acc + win[phase, 8 * a:8 * a + L, :] * w[j:j + 1, :]
        y_scr[s * L:(s + 1) * L, :] = _silu(_layernorm(acc + b_ref[...], g_ref[...], be_ref[...]))
        tail = buf[s, L + pad:L + 32, :]
        buf[s, pad:32, :] = tail
        tailr = bufr[s, L + pad:L + 32, :]
        bufr[s, pad:32, :] = tailr
    o_ref[...] = y_scr[...].astype(o_ref.dtype)

    @pl.when(tstep == pl.num_programs(1) - 1)
    def _():
        for s in range(n_seq):
            co_ref[s] = buf[s, pad:32, :]


def _conf_call(h, col_a, col_g, cache, w, b, g, be, out_buf, n_batch, seq_len, row_off, sample):
    buf_in, out_sds, aliases, total = _mix_target(out_buf, 7)
    n_seq, n_chunk, C, rows, grid, rb, seq_blk = _seq_layout(n_batch, seq_len, row_off, sample, total_rows=total)
    L = rows // n_seq
    hist = CONF_WIDTH - 1
    body = functools.partial(_conf_body, n_seq, L, True, sample)
    if total is not None:
        body = _with_tail_fill(body, grid[0] - 1, 8)
    colspec = lambda c0: pl.BlockSpec((rows, 512), lambda i, t: (rb(i, t), c0 // 512))
    fix2 = lambda i, t: (0, 0)
    c_spec = pl.BlockSpec((n_seq, hist, CONF_DIM), lambda i, t: (seq_blk(i), 0, 0))
    return pl.pallas_call(
        body,
        grid=grid,
        in_specs=[colspec(col_a), colspec(col_g), c_spec,
                  pl.BlockSpec((CONF_WIDTH, CONF_DIM), fix2), pl.BlockSpec((1, CONF_DIM), fix2),
                  pl.BlockSpec((1, CONF_DIM), fix2), pl.BlockSpec((1, CONF_DIM), fix2),
                  pl.BlockSpec(memory_space=pl.ANY)],
        out_specs=[pl.BlockSpec((rows, 512), lambda i, t: (rb(i, t), 0)), c_spec],
        out_shape=[out_sds, jax.ShapeDtypeStruct((n_batch, hist, CONF_DIM), F32)],
        scratch_shapes=[pltpu.VMEM((n_seq, 32 + L, CONF_DIM), F32)] * 2 + [pltpu.VMEM((rows, CONF_DIM), F32),
                                                                           pltpu.VMEM((8, L + 24, CONF_DIM), F32)],
        input_output_aliases=aliases,
        compiler_params=_cparams(2),
        name="conformer" + ("_sample" if sample else "_prompt"),
    )(h, h, cache, w, b, g, be, buf_in)


def _ssd_body(n_seq, n_chunk, C, round_x, round_w, hz_ref, hx_ref, hdt_ref, hist_ref, s0_ref, cw_ref, cb_ref, dtb_ref, alog_ref,
              dvec_ref, nw_ref, _, o_ref, co_ref, so_ref, st, buf, bufr, xbc, y_scr):
    tstep = pl.program_id(1)
    L = n_chunk * C
    hist = SSM_CONV - 1
    pad = 8 - hist
    n_pairs = SSM_HEADS // 2

    @pl.when(tstep == 0)
    def _():
        for s in range(n_seq):
            buf[s, pad:8, :] = hist_ref[s]
            bufr[s, pad:8, :] = _round_bf16(hist_ref[s], round_x)
            for m in range(n_pairs):
                st[s, m] = s0_ref[s, m]

    cw = _round_bf16(cw_ref[...], round_w)
    for s in range(n_seq):
        hx = hx_ref[s * L:(s + 1) * L, :]
        buf[s, 8:8 + L, :] = hx
        bufr[s, 8:8 + L, :] = _round_bf16(hx, round_x)
        acc = jnp.zeros((L, SSM_CONV_DIM), F32)
        for j in range(SSM_CONV):
            acc = acc + bufr[s, pad + j:pad + j + L, :] * cw[j:j + 1, :]
        xbc[s * L:(s + 1) * L, :] = _silu(acc + cb_ref[...])
        tail = buf[s, L + pad:L + 8, :]
        buf[s, pad:8, :] = tail
        tailr = bufr[s, L + pad:L + 8, :]
        bufr[s, pad:8, :] = tailr

    dt = _softplus(hdt_ref[...] + dtb_ref[...])
    la = dt * (-jnp.exp(alog_ref[...]))
    hrow = lax.broadcasted_iota(I32, (LANE, SSM_INNER), 0)
    hcol = lax.broadcasted_iota(I32, (LANE, SSM_INNER), 1) // SSM_HEADDIM
    expand = (hrow == hcol).astype(BF16)
    dtx = functools.reduce(lambda p, q: p + q,
                           [jnp.dot(part, expand, preferred_element_type=F32) for part in _split3(dt)])
    causal = _tri(C)
    tri = causal.astype(BF16)
    lane = lax.broadcasted_iota(I32, (C, HEAD_W), 1)
    bcol_all = _chunk_cumsum(la, C)
    heads_per_group = SSM_HEADS // SSM_GROUPS
    for s in range(n_seq):
        states = [st[s, m] for m in range(n_pairs)]
        for c in range(n_chunk):
            r0 = (s * n_chunk + c) * C
            rs = slice(r0, r0 + C)
            bcol = bcol_all[rs, :]
            brow = functools.reduce(lambda p, q: p + q, [
                lax.dot_general(part, tri, (((0,), (1,)), ((), ())), preferred_element_type=F32)
                for part in _split3(la[rs, :])])
            xs_c = xbc[rs, 0:SSM_INNER]
            v_c = (xs_c * dtx[rs, :]).astype(BF16)
            gmats, bms, cms = [], [], []
            for grp in range(SSM_GROUPS):
                bm = xbc[rs, SSM_INNER + grp * SSM_STATE:SSM_INNER + (grp + 1) * SSM_STATE]
                cm = xbc[rs, SSM_INNER + (SSM_GROUPS + grp) * SSM_STATE:SSM_INNER + (SSM_GROUPS + grp + 1) * SSM_STATE]
                cm_hi, cm_lo = _split2(cm)
                bm_hi, bm_lo = _split2(bm)
                gmats.append(_dot_nt(jnp.concatenate([cm_hi, cm_hi, cm_lo], axis=1),
                                     jnp.concatenate([bm_hi, bm_lo, bm_hi], axis=1)))
                bms.append(bm)
                cms.append(cm)
            for m in range(n_pairs):
                grp = (2 * m) // heads_per_group
                bm, cm, gmat = bms[grp], cms[grp], gmats[grp]
                ps = slice(m * HEAD_W, (m + 1) * HEAD_W)
                vp = v_c[:, ps]
                s_t = states[m]
                scores, queries, keys, decays = [], [], [], []
                for hh in range(2):
                    hd = 2 * m + hh
                    bc = bcol[:, hd:hd + 1]
                    br = brow[hd:hd + 1, :]
                    b_last = bcol[C - 1:C, hd:hd + 1]
                    scores.append(gmat * jnp.where(causal, jnp.exp(jnp.minimum(bc - br, 0.0)), 0.0))
                    queries.append(cm * jnp.exp(bc))
                    keys.append(bm * jnp.exp(b_last - bc))
                    decays.append(jnp.exp(b_last))
                o_stack = (_dot(jnp.concatenate(scores, axis=0), vp)
                           + _dot_nt(jnp.concatenate(queries, axis=0), s_t))
                kv = _dot_tn(vp, jnp.concatenate(keys, axis=1))
                d = SSM_HEADDIM
                states[m] = jnp.concatenate([s_t[0:d, :] * decays[0] + kv[0:d, 0:SSM_STATE],
                                             s_t[d:, :] * decays[1] + kv[d:, SSM_STATE:]], axis=0)
                o_pair = jnp.where(lane < SSM_HEADDIM, o_stack[0:C, :], o_stack[C:, :])
                y_scr[rs, ps] = o_pair + dvec_ref[:, ps] * xs_c[:, ps]
        for m in range(n_pairs):
            st[s, m] = states[m]
    y = y_scr[...] * _silu(hz_ref[...])
    gw = SSM_INNER // SSM_GROUPS
    for grp in range(SSM_GROUPS):
        gs = slice(grp * gw, (grp + 1) * gw)
        yg = y[:, gs]
        ms = jnp.mean(yg * yg, axis=-1, keepdims=True)
        o_ref[:, gs] = (yg * lax.rsqrt(ms + RMS_EPS) * nw_ref[:, gs]).astype(BF16)

    @pl.when(tstep == pl.num_programs(1) - 1)
    def _():
        for s in range(n_seq):
            co_ref[s] = buf[s, pad:8, :]
            for m in range(n_pairs):
                so_ref[s, m] = st[s, m]


def _ssd_call(h, col_z, col_x, col_dt, cache, s0, cw, cb, dtb, alog, dvec, nw, out_buf, n_batch, seq_len, row_off,
              sample):
    buf_in, out_sds, aliases, total = _mix_target(out_buf, 11)
    n_seq, n_chunk, C, rows, grid, rb, seq_blk = _seq_layout(n_batch, seq_len, row_off, sample, SSD_PROMPT_TILE, total)
    L = rows // n_seq
    hist = SSM_CONV - 1
    n_pairs = SSM_HEADS // 2
    body = functools.partial(_ssd_body, n_seq, n_chunk, C, sample, True)
    if total is not None:
        body = _with_tail_fill(body, grid[0] - 1, 12)
    colspec = lambda c0, w: pl.BlockSpec((rows, w), lambda i, t: (rb(i, t), c0 // w))
    fix2 = lambda i, t: (0, 0)
    c_spec = pl.BlockSpec((n_seq, hist, SSM_CONV_DIM), lambda i, t: (seq_blk(i), 0, 0))
    st_spec = pl.BlockSpec((n_seq, n_pairs, HEAD_W, SSM_STATE), lambda i, t: (seq_blk(i), 0, 0, 0))
    return pl.pallas_call(
        body,
        grid=grid,
        in_specs=[colspec(col_z, 512), colspec(col_x, SSM_CONV_DIM), colspec(col_dt, LANE), c_spec, st_spec,
                  pl.BlockSpec((SSM_CONV, SSM_CONV_DIM), fix2), pl.BlockSpec((1, SSM_CONV_DIM), fix2),
                  pl.BlockSpec((1, LANE), fix2), pl.BlockSpec((1, LANE), fix2),
                  pl.BlockSpec((1, SSM_INNER), fix2), pl.BlockSpec((1, SSM_INNER), fix2),
                  pl.BlockSpec(memory_space=pl.ANY)],
        out_specs=[pl.BlockSpec((rows, 512), lambda i, t: (rb(i, t), 0)), c_spec, st_spec],
        out_shape=[out_sds,
                   jax.ShapeDtypeStruct((n_batch, hist, SSM_CONV_DIM), F32),
                   jax.ShapeDtypeStruct((n_batch, n_pairs, HEAD_W, SSM_STATE), F32)],
        scratch_shapes=[pltpu.VMEM((n_seq, n_pairs, HEAD_W, SSM_STATE), F32),
                        pltpu.VMEM((n_seq, 8 + L, SSM_CONV_DIM), F32),
                        pltpu.VMEM((n_seq, 8 + L, SSM_CONV_DIM), F32),
                        pltpu.VMEM((rows, SSM_CONV_DIM), F32),
                        pltpu.VMEM((rows, SSM_INNER), F32)],
        input_output_aliases=aliases,
        compiler_params=_cparams(2),
        name="ssd" + ("_sample" if sample else "_prompt"),
    )(h, h, h, cache, s0, cw, cb, dtb, alog, dvec, nw, buf_in)


def _pad_heads(w, n_heads, width):
    lead = w.shape[:-1]
    w = w.reshape(lead + (n_heads, width))
    w = jnp.pad(w, [(0, 0)] * len(lead) + [(0, 0), (0, HEAD_W - width)])
    return w.reshape(lead + (n_heads * HEAD_W,))


def _row(v):
    return v.reshape(1, -1).astype(F32)


def kernel(x_prompt, x_sample, state_gla, cache_conformer, state_hgrn, state_ssm, cache_mamba_conv, w_in_even, w_gla_gate_lr, b_gla_gate, gla_norm_w, conf_conv_w, conf_conv_b, conf_ln_g, conf_ln_b, w_out_even, w_in_odd, hgrn_lower_bounds, hgrn_norm_w, mamba_conv_w, mamba_conv_b, mamba_dt_bias, mamba_a_log, mamba_d, mamba_norm_w, w_out_odd, ln1_g, ln1_b, ln2_g, ln2_b, router_w, router_b, expert_w_gate, expert_b_gate, expert_w_up, expert_b_up, expert_w_down, expert_b_down):
    bp, lp, _ = x_prompt.shape
    bs, ls, _ = x_sample.shape
    tp, ts = bp * lp, bs * ls
    x = (x_prompt.reshape(tp, D_MODEL), x_sample.reshape(ts, D_MODEL))

    def router_params(layer):
        return router_w[layer].T.astype(BF16), router_b[layer].astype(F32).reshape(N_EXPERTS, 1)

    def finish_layer(layer, x, mix_a, mix_b, w_out):
        rwt, rb = router_params(layer)
        x1, xp, gates, eidx, rank, counts = _outproj_ln_router(
            mix_a, mix_b, x[0], x[1], w_out[:512].astype(BF16), w_out[512:].astype(BF16),
            _row(ln1_g[layer]), _row(ln1_b[layer]), rwt, rb)
        return _moe(layer, x1, xp, gates, eidx, rank, counts, _row(ln2_g[layer]), _row(ln2_b[layer]),
                    expert_w_gate, expert_b_gate, expert_w_up, expert_b_up, expert_w_down, expert_b_down, tp)

    mix_rows = tp + ts

    wi = w_in_even[0]
    wq, wk, wv, wg, wlr, wglu = jnp.split(wi, [256, 512, 1024, 1536, 1552], axis=1)
    w_even = jnp.concatenate([_pad_heads(wq, GLA_HEADS, GLA_DK), _pad_heads(wk, GLA_HEADS, GLA_DK), wv, wg, wglu,
                              jnp.pad(wlr, ((0, 0), (0, LANE - GLA_RANK)))], axis=1).astype(BF16)
    cols_gla = (0, 512, 1024, 1536, 3072)
    col_a, col_gate = 2048, 2560
    h = _inproj(x[0], x[1], w_even)
    wlr_p = jnp.pad(_pad_heads(w_gla_gate_lr[0], GLA_HEADS, GLA_DK), ((0, LANE - GLA_RANK), (0, 0)))
    blr_p = _row(_pad_heads(b_gla_gate[0], GLA_HEADS, GLA_DK))
    nw = _row(gla_norm_w[0])
    conf_args = (conf_conv_w[0], _row(conf_conv_b[0]), _row(conf_ln_g[0]), _row(conf_ln_b[0]))
    s0_p = jnp.zeros((bp, GLA_HEADS, GLA_DK, HEAD_W), F32)
    s0_s = state_gla[0]
    mix_a, sg_p = _gla_call("gla", h, cols_gla, (wlr_p, blr_p), nw, s0_p, mix_rows, bp, lp, 0, False)
    mix_a, sg_s = _gla_call("gla", h, cols_gla, (wlr_p, blr_p), nw, s0_s, mix_a, bs, ls, tp, True)
    mix_b, cc_p = _conf_call(h, col_a, col_gate, jnp.zeros((bp,) + cache_conformer.shape[2:], F32), *conf_args,
                             mix_rows, bp, lp, 0, False)
    mix_b, cc_s = _conf_call(h, col_a, col_gate, cache_conformer[0], *conf_args, mix_b, bs, ls, tp, True)
    x = finish_layer(0, x, mix_a, mix_b, w_out_even[0])
    gla_p, gla_s = sg_p[None], sg_s[None]
    conf_p, conf_s = cc_p[None], cc_s[None]

    lb_cum = jnp.cumsum(jax.nn.softmax(hgrn_lower_bounds.astype(F32), axis=0), axis=0)
    lower_bound = _row((lb_cum - lb_cum[0])[1])
    wo = w_in_odd[0]
    w_odd = jnp.concatenate([wo[:, 2560:3584], wo[:, :2560],
                             jnp.pad(wo[:, 3584:], ((0, 0), (0, LANE - SSM_HEADS)))], axis=1).astype(BF16)
    h = _inproj(x[0], x[1], w_odd)
    cols_hgrn = (1024, 1536, 2048, 2560)
    col_z, col_x, col_dt = 3072, 0, 3584
    nw = _row(hgrn_norm_w[0])
    mix_a, sh_p = _gla_call("hgrn", h, cols_hgrn, lower_bound, nw,
                            jnp.zeros((bp, HGRN_HEADS, HEAD_W, HEAD_W), F32), mix_rows, bp, lp, 0, False)
    mix_a, sh_s = _gla_call("hgrn", h, cols_hgrn, lower_bound, nw, state_hgrn[0], mix_a, bs, ls, tp, True)

    def pair_states(s):
        return jnp.swapaxes(s, 2, 3).reshape(s.shape[0], SSM_HEADS // 2, HEAD_W, SSM_STATE)

    def unpair_states(s):
        return jnp.swapaxes(s.reshape(s.shape[0], SSM_HEADS, SSM_HEADDIM, SSM_STATE), 2, 3)

    pad8 = lambda v: jnp.pad(v.astype(F32), (0, LANE - SSM_HEADS)).reshape(1, LANE)
    ssd_args = (mamba_conv_w[0], _row(mamba_conv_b[0]), pad8(mamba_dt_bias[0]), pad8(mamba_a_log[0]),
                _row(jnp.repeat(mamba_d[0], SSM_HEADDIM)), _row(mamba_norm_w[0]))
    mix_b, cm_p, ss_p = _ssd_call(h, col_z, col_x, col_dt, jnp.zeros((bp,) + cache_mamba_conv.shape[2:], F32),
                                  jnp.zeros((bp, SSM_HEADS // 2, HEAD_W, SSM_STATE), F32), *ssd_args,
                                  mix_rows, bp, lp, 0, False)
    mix_b, cm_s, ss_s = _ssd_call(h, col_z, col_x, col_dt, cache_mamba_conv[0], pair_states(state_ssm[0]),
                                  *ssd_args, mix_b, bs, ls, tp, True)
    y_prompt, y_sample = finish_layer(1, x, mix_a, mix_b, w_out_odd[0])
    y_prompt = y_prompt.reshape(bp, lp, D_MODEL)
    y_sample = y_sample.reshape(bs, ls, D_MODEL)
    return (y_prompt, y_sample, gla_p, gla_s, conf_p, conf_s, sh_p[None], sh_s[None],
            unpair_states(ss_p)[None], unpair_states(ss_s)[None], cm_p[None], cm_s[None])
```

```python
import functools

import jax
import jax.numpy as jnp
from jax import lax
from jax.experimental import pallas as pl
from jax.experimental.pallas import tpu as pltpu
from jax.experimental.pallas import tpu_sc as plsc

F32 = jnp.float32
BF16 = jnp.bfloat16
I32 = jnp.int32
U32 = jnp.uint32

D_MODEL = 1024
DEPTH = 2
DEEPNORM_ALPHA = (2.0 * DEPTH) ** 0.25
LN_EPS = 1e-5
RMS_EPS = 1e-6
LANE = 128
HEAD_W = 128
GLA_HEADS, GLA_DK, GLA_RANK, GLA_TAU = 4, 64, 16, 16.0
CONF_DIM, CONF_WIDTH = 512, 31
HGRN_HEADS, HGRN_DK = 4, 128
SSM_HEADS, SSM_HEADDIM, SSM_STATE, SSM_GROUPS, SSM_CONV = 8, 64, 128, 2, 4
SSM_INNER = SSM_HEADS * SSM_HEADDIM
SSM_CONV_DIM = SSM_INNER + 2 * SSM_GROUPS * SSM_STATE
N_EXPERTS, TOP_K = 32, 4
SWIGLU_ALPHA, SWIGLU_LIMIT = 1.702, 7.0
SCAN_CHUNK = 64
PROMPT_TILE = 512
SSD_PROMPT_TILE = 512
SAMPLE_SEQS = 16
TOKEN_TILE = 512
WIDE_TOKEN_TILE = 1024
MOE_ROWS = 512
IN_SLOTS = 4
SC_CORES, SC_SUBCORES = 2, 16
SC_ROWS = 64
SC_SCATTER_ROWS = 32
VMEM_LIMIT = 56 * 1024 * 1024


def _cparams(n_axes):
    return pltpu.CompilerParams(dimension_semantics=("arbitrary",) * n_axes, vmem_limit_bytes=VMEM_LIMIT)


def _silu(x):
    return x * jax.nn.sigmoid(x)


def _softplus(x):
    return jnp.maximum(x, 0.0) + jnp.log(1.0 + jnp.exp(-jnp.abs(x)))


def _log_sigmoid(x):
    return jnp.minimum(x, 0.0) - jnp.log(1.0 + jnp.exp(-jnp.abs(x)))


def _layernorm(y, g, b):
    mu = jnp.mean(y, axis=-1, keepdims=True)
    d = y - mu
    var = jnp.mean(d * d, axis=-1, keepdims=True)
    return d * lax.rsqrt(var + LN_EPS) * g + b


def _dot(a, b):
    return jnp.dot(a.astype(BF16), b.astype(BF16), preferred_element_type=F32)


def _dot_nt(a, b):
    return lax.dot_general(a.astype(BF16), b.astype(BF16), (((1,), (1,)), ((), ())), preferred_element_type=F32)


def _dot_tn(a, b):
    return lax.dot_general(a.astype(BF16), b.astype(BF16), (((0,), (0,)), ((), ())), preferred_element_type=F32)


def _tri(c):
    r = lax.broadcasted_iota(I32, (c, c), 0)
    k = lax.broadcasted_iota(I32, (c, c), 1)
    return r >= k


def _pair_specs(tm, n_first, width):
    return [pl.BlockSpec((tm, width), lambda i: (jnp.minimum(i, n_first - 1), 0)),
            pl.BlockSpec((tm, width), lambda i: (jnp.maximum(i - n_first, 0), 0))]


def _pair_tile(n_first, xa_ref, xb_ref):
    return jnp.where(pl.program_id(0) < n_first, xa_ref[...], xb_ref[...])


def _inproj_body(n_first, xa_ref, xb_ref, w_ref, o_ref):
    xb = _pair_tile(n_first, xa_ref, xb_ref).astype(BF16)
    n = w_ref.shape[1]
    for c0 in range(0, n, 512):
        c1 = min(c0 + 512, n)
        o_ref[:, c0:c1] = jnp.dot(xb, w_ref[:, c0:c1], preferred_element_type=F32)


def _inproj(xa, xb, w):
    k, n = w.shape
    t = xa.shape[0] + xb.shape[0]
    n_first = xa.shape[0] // TOKEN_TILE
    return pl.pallas_call(
        functools.partial(_inproj_body, n_first),
        grid=(t // TOKEN_TILE,),
        in_specs=_pair_specs(TOKEN_TILE, n_first, k) + [pl.BlockSpec((k, n), lambda i: (0, 0))],
        out_specs=pl.BlockSpec((TOKEN_TILE, n), lambda i: (i, 0)),
        out_shape=jax.ShapeDtypeStruct((t, n), F32),
        compiler_params=_cparams(1),
        name="inproj",
    )(xa, xb, w)


def _pack_halves(y):
    half = y.shape[1] // 2
    hi = lax.bitcast_convert_type(y[:, :half].astype(BF16).astype(F32), U32)
    lo = lax.bitcast_convert_type(y[:, half:].astype(BF16).astype(F32), U32)
    return (hi & jnp.uint32(0xFFFF0000)) | (lo >> 16)


def _unpack_halves(w):
    hi = lax.bitcast_convert_type(w & jnp.uint32(0xFFFF0000), F32)
    lo = lax.bitcast_convert_type(w << 16, F32)
    return hi, lo


def _outproj_body(n_first, a_ref, b_ref, xa_ref, xb_ref, wa_ref, wb_ref, g_ref, be_ref, rwt_ref, rb_ref,
                  x1_ref, xp_ref, gate_ref, idx_ref, rank_ref, cnt_ref, carry):
    @pl.when(pl.program_id(0) == 0)
    def _():
        carry[...] = jnp.zeros(carry.shape, F32)

    mix = (jnp.dot(a_ref[...], wa_ref[...], preferred_element_type=F32)
           + jnp.dot(b_ref[...], wb_ref[...], preferred_element_type=F32))
    x1 = _layernorm(DEEPNORM_ALPHA * _pair_tile(n_first, xa_ref, xb_ref) + mix, g_ref[...], be_ref[...])
    x1_ref[...] = x1
    xp_ref[...] = _pack_halves(x1)
    logits = _dot_nt(rwt_ref[...], x1) + rb_ref[...]
    tm = logits.shape[1]
    expert = lax.broadcasted_iota(I32, logits.shape, 0)
    vals, idxs = [], []
    for _ in range(TOP_K):
        m = jnp.max(logits, axis=0, keepdims=True)
        sel = jnp.min(jnp.where(logits == m, expert, N_EXPERTS), axis=0, keepdims=True)
        vals.append(m)
        idxs.append(sel)
        logits = jnp.where(expert == sel, -jnp.inf, logits)
    exps = [jnp.exp(v - vals[0]) for v in vals]
    inv = 1.0 / functools.reduce(lambda p, q: p + q, exps)
    chosen = jnp.zeros(logits.shape, F32)
    for k in range(TOP_K):
        chosen = chosen + (expert == idxs[k]).astype(F32)
    earlier = lax.broadcasted_iota(I32, (tm, tm), 0) < lax.broadcasted_iota(I32, (tm, tm), 1)
    before = carry[...] + jnp.dot(chosen.astype(BF16), earlier.astype(BF16), preferred_element_type=F32)
    choice = lax.broadcasted_iota(I32, (8, tm), 0)
    gates = jnp.zeros((8, tm), F32)
    eidx = jnp.zeros((8, tm), I32)
    ranks = jnp.zeros((8, tm), F32)
    for k in range(TOP_K):
        rk = jnp.sum(jnp.where(expert == idxs[k], before, 0.0), axis=0, keepdims=True)
        gates = jnp.where(choice == k, exps[k] * inv, gates)
        eidx = jnp.where(choice == k, idxs[k], eidx)
        ranks = jnp.where(choice == k, rk, ranks)
    gate_ref[...] = gates
    idx_ref[...] = eidx
    rank_ref[...] = ranks.astype(I32)
    carry[...] = carry[...] + jnp.sum(chosen, axis=1, keepdims=True)
    cnt_ref[...] = carry[...].astype(I32)


def _outproj_ln_router(a, b, xa, xb, wa, wb, g, be, rwt, rb):
    t = xa.shape[0] + xb.shape[0]
    tm = WIDE_TOKEN_TILE
    n_first = xa.shape[0] // tm
    row = lambda i: (i, 0)
    col = lambda i: (0, i)
    fix = lambda i: (0, 0)
    return pl.pallas_call(
        functools.partial(_outproj_body, n_first),
        grid=(t // tm,),
        in_specs=[pl.BlockSpec((tm, 512), row), pl.BlockSpec((tm, 512), row)] + _pair_specs(tm, n_first, D_MODEL)
        + [pl.BlockSpec((512, D_MODEL), fix), pl.BlockSpec((512, D_MODEL), fix),
           pl.BlockSpec((1, D_MODEL), fix), pl.BlockSpec((1, D_MODEL), fix),
           pl.BlockSpec((N_EXPERTS, D_MODEL), fix), pl.BlockSpec((N_EXPERTS, 1), fix)],
        out_specs=[pl.BlockSpec((tm, D_MODEL), row), pl.BlockSpec((tm, 512), row),
                   pl.BlockSpec((8, tm), col), pl.BlockSpec((8, tm), col), pl.BlockSpec((8, tm), col),
                   pl.BlockSpec((N_EXPERTS, 1), fix)],
        out_shape=[jax.ShapeDtypeStruct((t, D_MODEL), F32), jax.ShapeDtypeStruct((t, 512), U32),
                   jax.ShapeDtypeStruct((8, t), F32), jax.ShapeDtypeStruct((8, t), I32),
                   jax.ShapeDtypeStruct((8, t), I32), jax.ShapeDtypeStruct((N_EXPERTS, 1), I32)],
        scratch_shapes=[pltpu.VMEM((N_EXPERTS, 1), F32)],
        compiler_params=_cparams(1),
        name="outproj_ln_router",
    )(a, b, xa, xb, wa, wb, g, be, rwt, rb)


def _sc_mesh():
    return plsc.VectorSubcoreMesh(core_axis_name="c", subcore_axis_name="s")


def _sc_scatter_rows(src, dest, n_out):
    n_src, w = src.shape
    n_dst = dest.shape[1]
    workers = SC_CORES * SC_SUBCORES
    per_worker = n_src // workers
    chunks = per_worker // SC_SCATTER_ROWS
    assert n_src == workers * chunks * SC_SCATTER_ROWS
    idx = dest.T.reshape(n_dst, workers, chunks, SC_SCATTER_ROWS)

    @functools.partial(pl.kernel, mesh=_sc_mesh(), out_type=jax.ShapeDtypeStruct((n_out, w), src.dtype),
                       scratch_types=[pltpu.VMEM((n_dst, chunks, SC_SCATTER_ROWS), I32)]
                       + [pltpu.VMEM((SC_SCATTER_ROWS, w), src.dtype)] * 2 + [pltpu.SemaphoreType.DMA] * 4)
    def scatter(src_hbm, idx_hbm, out_hbm, idx_v, rows_a, rows_b, sem_ra, sem_rb, sem_wa, sem_wb):
        worker = lax.axis_index("s") * SC_CORES + lax.axis_index("c")
        base = worker * per_worker
        for k in range(n_dst):
            pltpu.sync_copy(idx_hbm.at[k, worker], idx_v.at[k])

        def read(c, rows, sem):
            return pltpu.async_copy(src_hbm.at[pl.ds(pl.multiple_of(base + c * SC_SCATTER_ROWS, 8), SC_SCATTER_ROWS)],
                                    rows, sem)

        def write_all(pending_read, c, rows, sem):
            pending_read.wait()
            return [pltpu.async_copy(rows, out_hbm.at[idx_v.at[k, c]], sem) for k in range(n_dst)]

        @pl.loop(0, chunks // 2)
        def _(p):
            read_a = read(2 * p, rows_a, sem_ra)
            read_b = read(2 * p + 1, rows_b, sem_rb)
            writes = write_all(read_a, 2 * p, rows_a, sem_wa) + write_all(read_b, 2 * p + 1, rows_b, sem_wb)
            for wr in writes:
                wr.wait()

        if chunks % 2:
            for wr in write_all(read(chunks - 1, rows_a, sem_ra), chunks - 1, rows_a, sem_wa):
                wr.wait()

    return scatter(src, idx)


def _sc_gather_rows(table, idx):
    n, w = idx.shape[0], table.shape[1]
    workers = SC_CORES * SC_SUBCORES
    chunks = n // (workers * SC_ROWS)
    assert n == workers * chunks * SC_ROWS and chunks % 2 == 0
    idx = idx.reshape(workers, chunks, SC_ROWS)

    @functools.partial(pl.kernel, mesh=_sc_mesh(), out_type=jax.ShapeDtypeStruct((n, w), table.dtype),
                       scratch_types=[pltpu.VMEM((chunks, SC_ROWS), I32)] + [pltpu.VMEM((SC_ROWS, w), table.dtype)] * 2
                       + [pltpu.SemaphoreType.DMA] * 4)
    def gather(table_hbm, idx_hbm, out_hbm, idx_v, rows_a, rows_b, sem_ra, sem_rb, sem_wa, sem_wb):
        worker = lax.axis_index("s") * SC_CORES + lax.axis_index("c")
        base = worker * (chunks * SC_ROWS)
        pltpu.sync_copy(idx_hbm.at[worker], idx_v)

        def out_rows(c):
            return out_hbm.at[pl.ds(pl.multiple_of(base + c * SC_ROWS, 8), SC_ROWS)]

        @pl.loop(0, chunks // 2)
        def _(p):
            read_a = pltpu.async_copy(table_hbm.at[idx_v.at[2 * p]], rows_a, sem_ra)
            read_b = pltpu.async_copy(table_hbm.at[idx_v.at[2 * p + 1]], rows_b, sem_rb)
            read_a.wait()
            write_a = pltpu.async_copy(rows_a, out_rows(2 * p), sem_wa)
            read_b.wait()
            write_b = pltpu.async_copy(rows_b, out_rows(2 * p + 1), sem_wb)
            write_a.wait()
            write_b.wait()

    return gather(table, idx)


def _experts_body(b0_ref, nb_ref, last_ref, nt_ref, xs_hbm, wg_ref, bg_ref, wu_ref, bu_ref, wd_ref, bd_ref, o_hbm,
                  wg_s, wu_s, wd_s, xbuf, obuf, sem_in, sem_out):
    e = pl.program_id(0)
    first_blk, n_blk, last_valid, n_total = b0_ref[e], nb_ref[e], last_ref[e], nt_ref[0]
    quarter = MOE_ROWS // 4

    def rows_of(g):
        return pl.ds(pl.multiple_of(g * MOE_ROWS, MOE_ROWS), MOE_ROWS)

    def fetch(g, slot):
        return pltpu.make_async_copy(xs_hbm.at[rows_of(g)], xbuf.at[slot], sem_in.at[slot])

    def put(g, slot):
        return pltpu.make_async_copy(obuf.at[slot], o_hbm.at[rows_of(g)], sem_out.at[slot])

    lead = IN_SLOTS - 1
    for first in range(lead):
        @pl.when((e == 0) & (n_total > first))
        def _():
            fetch(first, first).start()

    @pl.when(n_blk > 0)
    def _():
        wg_s[...] = wg_ref[...].astype(BF16)
        wu_s[...] = wu_ref[...].astype(BF16)
        wd_s[...] = wd_ref[...].astype(BF16)

    def compute(islot, slot, rows):
        half = D_MODEL // 2
        x_hi, x_lo = _unpack_halves(xbuf[islot, 0:rows, :])
        x_hi = x_hi.astype(BF16)
        x_lo = x_lo.astype(BF16)
        g = (jnp.dot(x_hi, wg_s[:half, :], preferred_element_type=F32)
             + jnp.dot(x_lo, wg_s[half:, :], preferred_element_type=F32) + bg_ref[...])
        u = (jnp.dot(x_hi, wu_s[:half, :], preferred_element_type=F32)
             + jnp.dot(x_lo, wu_s[half:, :], preferred_element_type=F32) + bu_ref[...])
        g = jnp.minimum(g, SWIGLU_LIMIT)
        u = jnp.clip(u, -SWIGLU_LIMIT, SWIGLU_LIMIT)
        hmid = (u + 1.0) * (g * jax.nn.sigmoid(SWIGLU_ALPHA * g))
        out = jnp.dot(hmid.astype(BF16), wd_s[...], preferred_element_type=F32) + bd_ref[...]
        obuf[slot, 0:rows, :] = _pack_halves(out)

    def block(j, carry):
        g = first_blk + j
        slot = lax.rem(g, 2)
        islot = lax.rem(g, IN_SLOTS)
        fetch(g, islot).wait()

        @pl.when(g + lead < n_total)
        def _():
            fetch(g + lead, lax.rem(g + lead, IN_SLOTS)).start()

        @pl.when(g >= 2)
        def _():
            put(g - 2, slot).wait()

        valid = jnp.where(j == n_blk - 1, last_valid, MOE_ROWS)

        for rows in range(quarter, MOE_ROWS + 1, quarter):
            @pl.when((valid > rows - quarter) & (valid <= rows))
            def _():
                compute(islot, slot, rows)
                if rows < MOE_ROWS:
                    obuf[slot, rows:, :] = jnp.zeros((MOE_ROWS - rows, obuf.shape[2]), obuf.dtype)

        put(g, slot).start()
        return carry

    lax.fori_loop(0, n_blk, block, 0)

    @pl.when((e == N_EXPERTS - 1) & (n_total >= 2))
    def _():
        put(n_total - 2, lax.rem(n_total, 2)).wait()

    @pl.when((e == N_EXPERTS - 1) & (n_total >= 1))
    def _():
        put(n_total - 1, lax.rem(n_total - 1, 2)).wait()


def _experts(layer, first_blk, n_blk, last_valid, xs, wg, bg, wu, bu, wd, bd):
    n_rows, w = xs.shape
    wsel = lambda e, b0, nb, lv, nt: (layer, e, 0, 0)
    wspec = pl.BlockSpec((None, None, D_MODEL, D_MODEL), wsel)
    bspec = pl.BlockSpec((None, None, 1, D_MODEL), wsel)
    bias = lambda b: b.reshape(b.shape[0], b.shape[1], 1, b.shape[2])
    return pl.pallas_call(
        _experts_body,
        grid_spec=pltpu.PrefetchScalarGridSpec(
            num_scalar_prefetch=4,
            grid=(N_EXPERTS,),
            in_specs=[pl.BlockSpec(memory_space=pl.ANY), wspec, bspec, wspec, bspec, wspec, bspec],
            out_specs=pl.BlockSpec(memory_space=pl.ANY),
            scratch_shapes=[pltpu.VMEM((D_MODEL, D_MODEL), BF16)] * 3
            + [pltpu.VMEM((IN_SLOTS, MOE_ROWS, w), U32), pltpu.VMEM((2, MOE_ROWS, w), U32),
               pltpu.SemaphoreType.DMA((IN_SLOTS,)), pltpu.SemaphoreType.DMA((2,))],
        ),
        out_shape=jax.ShapeDtypeStruct((n_rows, w), U32),
        compiler_params=_cparams(1),
        name="experts",
    )(first_blk, n_blk, last_valid, jnp.sum(n_blk).reshape(1), xs, wg, bias(bg), wu, bias(bu), wd, bias(bd))


def _combine_body(n_first, o0_ref, o1_ref, o2_ref, o3_ref, gt_ref, x_ref, g_ref, b_ref, ya_ref, yb_ref=None):
    half = D_MODEL // 2
    gates = gt_ref[...]
    hi = jnp.zeros((x_ref.shape[0], half), F32)
    lo = jnp.zeros((x_ref.shape[0], half), F32)
    for k, o_ref in enumerate((o0_ref, o1_ref, o2_ref, o3_ref)):
        h, l = _unpack_halves(o_ref[...])
        gk = gates[:, k:k + 1]
        hi = hi + gk * h
        lo = lo + gk * l
    x = x_ref[...]
    y_hi = DEEPNORM_ALPHA * x[:, :half] + hi
    y_lo = DEEPNORM_ALPHA * x[:, half:] + lo
    mu = (jnp.sum(y_hi, axis=-1, keepdims=True) + jnp.sum(y_lo, axis=-1, keepdims=True)) * (1.0 / D_MODEL)
    d_hi = y_hi - mu
    d_lo = y_lo - mu
    var = (jnp.sum(d_hi * d_hi, axis=-1, keepdims=True) + jnp.sum(d_lo * d_lo, axis=-1, keepdims=True)) * (1.0 / D_MODEL)
    r = lax.rsqrt(var + LN_EPS)
    out_hi = d_hi * r * g_ref[:, :half] + b_ref[:, :half]
    out_lo = d_lo * r * g_ref[:, half:] + b_ref[:, half:]

    def write(y_ref):
        y_ref[:, :half] = out_hi
        y_ref[:, half:] = out_lo

    if yb_ref is None:
        write(ya_ref)
    else:
        pl.when(pl.program_id(0) < n_first)(lambda: write(ya_ref))
        pl.when(pl.program_id(0) >= n_first)(lambda: write(yb_ref))


def _combine_ln(o4, gates, x, g, b, t_first=None):
    t = x.shape[0]
    tm = WIDE_TOKEN_TILE
    row = lambda i: (i, 0)
    fix = lambda i: (0, 0)
    choice = lambda k: pl.BlockSpec((tm, 512), lambda i: (k * (t // tm) + i, 0))
    if t_first is None:
        n_first = None
        out_specs = pl.BlockSpec((tm, D_MODEL), row)
        out_shape = jax.ShapeDtypeStruct((t, D_MODEL), F32)
    else:
        n_first = t_first // tm
        out_specs = [pl.BlockSpec((tm, D_MODEL), lambda i: (jnp.minimum(i, n_first - 1), 0)),
                     pl.BlockSpec((tm, D_MODEL), lambda i: (jnp.maximum(i - n_first, 0), 0))]
        out_shape = [jax.ShapeDtypeStruct((t_first, D_MODEL), F32), jax.ShapeDtypeStruct((t - t_first, D_MODEL), F32)]
    return pl.pallas_call(
        functools.partial(_combine_body, n_first),
        grid=(t // tm,),
        in_specs=[choice(0), choice(1), choice(2), choice(3), pl.BlockSpec((tm, TOP_K), row),
                  pl.BlockSpec((tm, D_MODEL), row), pl.BlockSpec((1, D_MODEL), fix), pl.BlockSpec((1, D_MODEL), fix)],
        out_specs=out_specs,
        out_shape=out_shape,
        compiler_params=_cparams(1),
        name="combine_ln",
    )(o4, o4, o4, o4, gates, x, g, b)


def _moe(layer, x1, xp, gates, eidx, rank, counts, ln_g, ln_b, wg, bg, wu, bu, wd, bd, t_first=None):
    t = x1.shape[0]
    bm = MOE_ROWS
    n_blocks = t * TOP_K // bm + N_EXPERTS
    n_rows = n_blocks * bm
    cnt = counts[:, 0]
    padded = (cnt + bm - 1) // bm * bm
    pad_end = jnp.cumsum(padded)
    pad_start = pad_end - padded
    e = eidx[:TOP_K]
    start = jnp.sum(jnp.where(e[:, :, None] == jnp.arange(N_EXPERTS, dtype=I32), pad_start, 0), axis=-1)
    dest = (start + rank[:TOP_K]).T
    n_blk = padded // bm
    last_valid = cnt - (n_blk - 1) * bm
    xs = _sc_scatter_rows(xp, dest, n_rows)
    outs = _experts(layer, pad_start // bm, n_blk, last_valid, xs, wg, bg, wu, bu, wd, bd)
    o4 = _sc_gather_rows(outs, dest.T.reshape(-1))
    return _combine_ln(o4, gates[:TOP_K].T, x1, ln_g, ln_b, t_first)


def _split2(x):
    hi = x.astype(BF16)
    return hi, (x - hi.astype(F32)).astype(BF16)


def _split3(x):
    hi = x.astype(BF16)
    rem = x - hi.astype(F32)
    mid = rem.astype(BF16)
    return hi, mid, (rem - mid.astype(F32)).astype(BF16)


def _chunk_cumsum(g, C):
    rows = g.shape[0]
    group = max(C, min(rows, 2 * LANE))
    r = lax.broadcasted_iota(I32, (group, group), 0)
    c = lax.broadcasted_iota(I32, (group, group), 1)
    tri = ((r >= c) & (r // C == c // C)).astype(BF16)
    parts = _split3(g)
    sums = [functools.reduce(lambda p, q: p + q,
                             [jnp.dot(tri, part[r0:r0 + group, :], preferred_element_type=F32) for part in parts])
            for r0 in range(0, rows, group)]
    return sums[0] if len(sums) == 1 else jnp.concatenate(sums, axis=0)


def _gla_batched_step(q, k, v, g, C, n_seq, n_heads, state_of, o_scr):
    rows = n_seq * C
    wide = n_seq * HEAD_W
    mid = max(C // 2 - 1, 0)
    r = lax.broadcasted_iota(I32, (rows, rows), 0)
    c = lax.broadcasted_iota(I32, (rows, rows), 1)
    same = (r // C) == (c // C)
    causal = same & (r >= c)
    parts = _split3(g)
    summed = lambda mask: functools.reduce(lambda p, q_: p + q_, [
        jnp.dot(mask.astype(BF16), part, preferred_element_type=F32) for part in parts])
    b = summed(causal)
    b_mid = summed(same & ((c % C) <= mid))
    b_last = summed(same)
    qe_hi, qe_lo = _split2(q * jnp.exp(b - b_mid))
    ke_hi, ke_lo = _split2(k * jnp.exp(b_mid - b))
    q_state = (q * jnp.exp(b)).astype(BF16)
    k_state = (k * jnp.exp(b_last - b)).astype(BF16)
    decay_parts = _split3(jnp.exp(b_last))
    row_w = lax.broadcasted_iota(I32, (rows, wide), 0)
    blk_w = lax.broadcasted_iota(I32, (rows, wide), 1) // HEAD_W
    own = (row_w // C) == blk_w
    pick = (row_w == blk_w * C).astype(BF16)
    new_states = []
    for h in range(n_heads):
        cs = slice(h * HEAD_W, (h + 1) * HEAD_W)
        lhs = jnp.concatenate([qe_hi[:, cs], qe_hi[:, cs], qe_lo[:, cs]], axis=1)
        rhs = jnp.concatenate([ke_hi[:, cs], ke_lo[:, cs], ke_hi[:, cs]], axis=1)
        scores = jnp.where(causal, _dot_nt(lhs, rhs), 0.0)
        vh = v[:, cs].astype(BF16)
        s_cat = jnp.concatenate([state_of(s, h) for s in range(n_seq)], axis=1)
        o_full = _dot(q_state[:, cs], s_cat)
        o_state = jnp.concatenate([o_full[s * C:(s + 1) * C, s * HEAD_W:(s + 1) * HEAD_W] for s in range(n_seq)],
                                  axis=0)
        o = _dot(scores, vh) + o_state
        v_wide = jnp.where(own, jnp.concatenate([vh] * n_seq, axis=1), jnp.zeros((), BF16))
        kv = _dot_tn(k_state[:, cs], v_wide)
        decay = functools.reduce(lambda p, q_: p + q_, [
            lax.dot_general(part[:, cs], pick, (((0,), (0,)), ((), ())), preferred_element_type=F32)
            for part in decay_parts])
        new_states.append(s_cat * decay + kv)
        ms = jnp.mean(o * o, axis=-1, keepdims=True)
        o_scr[:, cs] = o * lax.rsqrt(ms + RMS_EPS)
    return new_states


def _gla_body(mode, n_seq, n_chunk, C, *refs):
    if mode == "gla":
        hq_ref, hk_ref, hv_ref, hg_ref, hlr_ref, wlr_ref, blr_ref, nw_ref, s0_ref, _, o_ref, so_ref, st, o_scr = refs
    else:
        hq_ref, hk_ref, hv_ref, hg_ref, lb_ref, nw_ref, s0_ref, _, o_ref, so_ref, st, o_scr = refs
    n_heads = 4
    n_keys = s0_ref.shape[2]
    tstep = pl.program_id(1)
    batched = n_chunk == 1 and n_seq > 1

    def padded_state(s, h):
        s_in = s0_ref[s, h]
        if n_keys < HEAD_W:
            s_in = jnp.concatenate([s_in, jnp.zeros((HEAD_W - n_keys, HEAD_W), F32)], axis=0)
        return s_in

    if not batched:
        @pl.when(tstep == 0)
        def _():
            for s in range(n_seq):
                for h in range(n_heads):
                    st[s, h] = padded_state(s, h).T

    if mode == "gla":
        q = hq_ref[...] * (GLA_DK ** -0.5)
        k = hk_ref[...]
        z = _dot(hlr_ref[...], wlr_ref[...]) + blr_ref[...]
        g = _log_sigmoid(z) * (1.0 / GLA_TAU)
    else:
        q = _silu(hq_ref[...]) * (HGRN_DK ** -0.5)
        lb = lb_ref[...]
        f = lb + (1.0 - lb) * jax.nn.sigmoid(hk_ref[...])
        k = 1.0 - f
        g = jnp.log(f)
    v = hv_ref[...]
    if batched:
        new_states = _gla_batched_step(q, k, v, g, C, n_seq, n_heads, padded_state, o_scr)
        for s in range(n_seq):
            for h in range(n_heads):
                so_ref[s, h] = new_states[h][0:n_keys, s * HEAD_W:(s + 1) * HEAD_W]
    else:
        causal = _tri(C)
        mid = max(C // 2 - 1, 0)
        b_all = _chunk_cumsum(g, C)
        for s in range(n_seq):
            states = [st[s, h] for h in range(n_heads)]
            for c in range(n_chunk):
                r0 = (s * n_chunk + c) * C
                rs = slice(r0, r0 + C)
                b, qc, kc = b_all[rs, :], q[rs, :], k[rs, :]
                b_last = b[C - 1:C, :]
                b_mid = b[mid:mid + 1, :]
                qe_hi, qe_lo = _split2(qc * jnp.exp(b - b_mid))
                ke_hi, ke_lo = _split2(kc * jnp.exp(b_mid - b))
                q_state = (qc * jnp.exp(b)).astype(BF16)
                k_state = (kc * jnp.exp(b_last - b)).astype(BF16)
                decay = jnp.exp(b_last)
                for h in range(n_heads):
                    cs = slice(h * HEAD_W, (h + 1) * HEAD_W)
                    lhs = jnp.concatenate([qe_hi[:, cs], qe_hi[:, cs], qe_lo[:, cs]], axis=1)
                    rhs = jnp.concatenate([ke_hi[:, cs], ke_lo[:, cs], ke_hi[:, cs]], axis=1)
                    scores = jnp.where(causal, _dot_nt(lhs, rhs), 0.0)
                    vh = v[rs, cs].astype(BF16)
                    o = _dot(scores, vh) + _dot_nt(q_state[:, cs], states[h])
                    states[h] = states[h] * decay[:, cs] + _dot_tn(vh, k_state[:, cs])
                    ms = jnp.mean(o * o, axis=-1, keepdims=True)
                    o_scr[rs, cs] = o * lax.rsqrt(ms + RMS_EPS)
            for h in range(n_heads):
                st[s, h] = states[h]
    o_ref[...] = (o_scr[...] * nw_ref[...] * _silu(hg_ref[...])).astype(BF16)

    if not batched:
        @pl.when(tstep == pl.num_programs(1) - 1)
        def _():
            for s in range(n_seq):
                for h in range(n_heads):
                    so_ref[s, h] = st[s, h].T[0:n_keys, :]


def _with_tail_fill(body, n_real, out_index):
    def wrapped(*refs):
        pl.when(pl.program_id(0) < n_real)(lambda: body(*refs))

        @pl.when(pl.program_id(0) >= n_real)
        def _():
            refs[out_index][...] = jnp.zeros(refs[out_index].shape, refs[out_index].dtype)

    return wrapped


def _seq_layout(n_batch, seq_len, row_off, sample, prompt_tile=PROMPT_TILE, total_rows=None):
    if sample:
        n_seq, n_chunk, C = SAMPLE_SEQS, 1, seq_len
        rows = n_seq * C
        grid = (n_batch // n_seq, 1)
        blk0 = row_off // rows
        rb = lambda i, t: blk0 + i
    else:
        n_seq, n_chunk, C = 1, prompt_tile // SCAN_CHUNK, SCAN_CHUNK
        rows = prompt_tile
        tiles = seq_len // rows
        grid = (n_batch, tiles)
        blk0 = row_off // rows
        rb = lambda i, t: blk0 + i * tiles + t
    seq_blk = lambda i: i
    if total_rows is not None:
        assert 0 < total_rows - (row_off + n_batch * seq_len) <= (rows if sample else seq_len)
        n_real, last, rb_real = grid[0], total_rows // rows - 1, rb
        grid = (n_real + 1, grid[1])
        rb = lambda i, t: jnp.minimum(rb_real(i, t), last)
        seq_blk = lambda i: jnp.minimum(i, n_real - 1)
    return n_seq, n_chunk, C, rows, grid, rb, seq_blk


def _mix_target(out_buf, alias_index):
    if hasattr(out_buf, "shape"):
        return out_buf, jax.ShapeDtypeStruct(out_buf.shape, out_buf.dtype), {alias_index: 0}, None
    return jnp.zeros((8, LANE), BF16), jax.ShapeDtypeStruct((out_buf, 512), BF16), {}, out_buf


def _gla_call(mode, h, cols, extra, nw, s0, out_buf, n_batch, seq_len, row_off, sample):
    n_alias = 9 if mode == "gla" else 7
    buf_in, out_sds, aliases, total = _mix_target(out_buf, n_alias)
    n_seq, n_chunk, C, rows, grid, rb, seq_blk = _seq_layout(n_batch, seq_len, row_off, sample, total_rows=total)
    colspec = lambda c0, w: pl.BlockSpec((rows, w), lambda i, t: (rb(i, t), c0 // w))
    fix2 = lambda i, t: (0, 0)
    in_specs = [colspec(cols[0], 512), colspec(cols[1], 512), colspec(cols[2], 512), colspec(cols[3], 512)]
    args = [h, h, h, h]
    if mode == "gla":
        wlr, blr = extra
        in_specs += [colspec(cols[4], LANE), pl.BlockSpec((LANE, 512), fix2), pl.BlockSpec((1, 512), fix2)]
        args += [h, wlr, blr]
    else:
        in_specs += [pl.BlockSpec((1, 512), fix2)]
        args += [extra]
    n_keys = s0.shape[2]
    st_spec = pl.BlockSpec((n_seq, 4, n_keys, HEAD_W), lambda i, t: (seq_blk(i), 0, 0, 0))
    in_specs += [pl.BlockSpec((1, 512), fix2), st_spec, pl.BlockSpec(memory_space=pl.ANY)]
    args += [nw, s0, buf_in]
    assert len(args) - 1 == n_alias
    o_spec = pl.BlockSpec((rows, 512), lambda i, t: (rb(i, t), 0))
    body = functools.partial(_gla_body, mode, n_seq, n_chunk, C)
    if total is not None:
        body = _with_tail_fill(body, grid[0] - 1, len(args))
    return pl.pallas_call(
        body,
        grid=grid,
        in_specs=in_specs,
        out_specs=[o_spec, st_spec],
        out_shape=[out_sds, jax.ShapeDtypeStruct((n_batch, 4, n_keys, HEAD_W), F32)],
        scratch_shapes=[pltpu.VMEM((n_seq, 4, HEAD_W, HEAD_W), F32), pltpu.VMEM((rows, 512), F32)],
        input_output_aliases=aliases,
        compiler_params=_cparams(2),
        name=mode + ("_sample" if sample else "_prompt"),
    )(*args)


def _round_bf16(x, on=True):
    return x.astype(BF16).astype(F32) if on else x


def _conf_body(n_seq, L, round_x, round_w, a_ref, gt_ref, hist_ref, w_ref, b_ref, g_ref, be_ref, _, o_ref, co_ref,
               buf, bufr, y_scr, win):
    tstep = pl.program_id(1)
    hist = CONF_WIDTH - 1
    pad = 32 - hist

    @pl.when(tstep == 0)
    def _():
        for s in range(n_seq):
            buf[s, pad:32, :] = hist_ref[s]
            bufr[s, pad:32, :] = _round_bf16(hist_ref[s], round_x)

    u = a_ref[...] * jax.nn.sigmoid(gt_ref[...])
    ur = _round_bf16(u, round_x)
    for s in range(n_seq):
        buf[s, 32:32 + L, :] = u[s * L:(s + 1) * L, :]
        bufr[s, 32:32 + L, :] = ur[s * L:(s + 1) * L, :]
    w = _round_bf16(w_ref[...], round_w)
    for s in range(n_seq):
        acc = jnp.zeros((L, CONF_DIM), F32)
        for phase in range(8):
            n_taps = (CONF_WIDTH - 1 - phase) // 8 + 1
            span = L + 8 * (n_taps - 1)
            win[phase, 0:span, :] = bufr[s, pad + phase:pad + phase + span, :]
            for a in range(n_taps):
                j = 8 * a + phase
                acc = acc + win[phase, 8 * a:8 * a + L, :] * w[j:j + 1, :]
        y_scr[s * L:(s + 1) * L, :] = _silu(_layernorm(acc + b_ref[...], g_ref[...], be_ref[...]))
        tail = buf[s, L + pad:L + 32, :]
        buf[s, pad:32, :] = tail
        tailr = bufr[s, L + pad:L + 32, :]
        bufr[s, pad:32, :] = tailr
    o_ref[...] = y_scr[...].astype(o_ref.dtype)

    @pl.when(tstep == pl.num_programs(1) - 1)
    def _():
        for s in range(n_seq):
            co_ref[s] = buf[s, pad:32, :]


def _conf_call(h, col_a, col_g, cache, w, b, g, be, out_buf, n_batch, seq_len, row_off, sample):
    buf_in, out_sds, aliases, total = _mix_target(out_buf, 7)
    n_seq, n_chunk, C, rows, grid, rb, seq_blk = _seq_layout(n_batch, seq_len, row_off, sample, total_rows=total)
    L = rows // n_seq
    hist = CONF_WIDTH - 1
    body = functools.partial(_conf_body, n_seq, L, True, sample)
    if total is not None:
        body = _with_tail_fill(body, grid[0] - 1, 8)
    colspec = lambda c0: pl.BlockSpec((rows, 512), lambda i, t: (rb(i, t), c0 // 512))
    fix2 = lambda i, t: (0, 0)
    c_spec = pl.BlockSpec((n_seq, hist, CONF_DIM), lambda i, t: (seq_blk(i), 0, 0))
    return pl.pallas_call(
        body,
        grid=grid,
        in_specs=[colspec(col_a), colspec(col_g), c_spec,
                  pl.BlockSpec((CONF_WIDTH, CONF_DIM), fix2), pl.BlockSpec((1, CONF_DIM), fix2),
                  pl.BlockSpec((1, CONF_DIM), fix2), pl.BlockSpec((1, CONF_DIM), fix2),
                  pl.BlockSpec(memory_space=pl.ANY)],
        out_specs=[pl.BlockSpec((rows, 512), lambda i, t: (rb(i, t), 0)), c_spec],
        out_shape=[out_sds, jax.ShapeDtypeStruct((n_batch, hist, CONF_DIM), F32)],
        scratch_shapes=[pltpu.VMEM((n_seq, 32 + L, CONF_DIM), F32)] * 2 + [pltpu.VMEM((rows, CONF_DIM), F32),
                                                                           pltpu.VMEM((8, L + 24, CONF_DIM), F32)],
        input_output_aliases=aliases,
        compiler_params=_cparams(2),
        name="conformer" + ("_sample" if sample else "_prompt"),
    )(h, h, cache, w, b, g, be, buf_in)


def _ssd_body(n_seq, n_chunk, C, round_x, round_w, hz_ref, hx_ref, hdt_ref, hist_ref, s0_ref, cw_ref, cb_ref, dtb_ref, alog_ref,
              dvec_ref, nw_ref, _, o_ref, co_ref, so_ref, st, buf, bufr, xbc, y_scr):
    tstep = pl.program_id(1)
    L = n_chunk * C
    hist = SSM_CONV - 1
    pad = 8 - hist
    n_pairs = SSM_HEADS // 2

    @pl.when(tstep == 0)
    def _():
        for s in range(n_seq):
            buf[s, pad:8, :] = hist_ref[s]
            bufr[s, pad:8, :] = _round_bf16(hist_ref[s], round_x)
            for m in range(n_pairs):
                st[s, m] = s0_ref[s, m]

    cw = _round_bf16(cw_ref[...], round_w)
    for s in range(n_seq):
        hx = hx_ref[s * L:(s + 1) * L, :]
        buf[s, 8:8 + L, :] = hx
        bufr[s, 8:8 + L, :] = _round_bf16(hx, round_x)
        acc = jnp.zeros((L, SSM_CONV_DIM), F32)
        for j in range(SSM_CONV):
            acc = acc + bufr[s, pad + j:pad + j + L, :] * cw[j:j + 1, :]
        xbc[s * L:(s + 1) * L, :] = _silu(acc + cb_ref[...])
        tail = buf[s, L + pad:L + 8, :]
        buf[s, pad:8, :] = tail
        tailr = bufr[s, L + pad:L + 8, :]
        bufr[s, pad:8, :] = tailr

    dt = _softplus(hdt_ref[...] + dtb_ref[...])
    la = dt * (-jnp.exp(alog_ref[...]))
    hrow = lax.broadcasted_iota(I32, (LANE, SSM_INNER), 0)
    hcol = lax.broadcasted_iota(I32, (LANE, SSM_INNER), 1) // SSM_HEADDIM
    expand = (hrow == hcol).astype(BF16)
    dtx = functools.reduce(lambda p, q: p + q,
                           [jnp.dot(part, expand, preferred_element_type=F32) for part in _split3(dt)])
    causal = _tri(C)
    tri = causal.astype(BF16)
    lane = lax.broadcasted_iota(I32, (C, HEAD_W), 1)
    bcol_all = _chunk_cumsum(la, C)
    heads_per_group = SSM_HEADS // SSM_GROUPS
    for s in range(n_seq):
        states = [st[s, m] for m in range(n_pairs)]
        for c in range(n_chunk):
            r0 = (s * n_chunk + c) * C
            rs = slice(r0, r0 + C)
            bcol = bcol_all[rs, :]
            brow = functools.reduce(lambda p, q: p + q, [
                lax.dot_general(part, tri, (((0,), (1,)), ((), ())), preferred_element_type=F32)
                for part in _split3(la[rs, :])])
            xs_c = xbc[rs, 0:SSM_INNER]
            v_c = (xs_c * dtx[rs, :]).astype(BF16)
            gmats, bms, cms = [], [], []
            for grp in range(SSM_GROUPS):
                bm = xbc[rs, SSM_INNER + grp * SSM_STATE:SSM_INNER + (grp + 1) * SSM_STATE]
                cm = xbc[rs, SSM_INNER + (SSM_GROUPS + grp) * SSM_STATE:SSM_INNER + (SSM_GROUPS + grp + 1) * SSM_STATE]
                cm_hi, cm_lo = _split2(cm)
                bm_hi, bm_lo = _split2(bm)
                gmats.append(_dot_nt(jnp.concatenate([cm_hi, cm_hi, cm_lo], axis=1),
                                     jnp.concatenate([bm_hi, bm_lo, bm_hi], axis=1)))
                bms.append(bm)
                cms.append(cm)
            for m in range(n_pairs):
                grp = (2 * m) // heads_per_group
                bm, cm, gmat = bms[grp], cms[grp], gmats[grp]
                ps = slice(m * HEAD_W, (m + 1) * HEAD_W)
                vp = v_c[:, ps]
                s_t = states[m]
                scores, queries, keys, decays = [], [], [], []
                for hh in range(2):
                    hd = 2 * m + hh
                    bc = bcol[:, hd:hd + 1]
                    br = brow[hd:hd + 1, :]
                    b_last = bcol[C - 1:C, hd:hd + 1]
                    scores.append(gmat * jnp.where(causal, jnp.exp(jnp.minimum(bc - br, 0.0)), 0.0))
                    queries.append(cm * jnp.exp(bc))
                    keys.append(bm * jnp.exp(b_last - bc))
                    decays.append(jnp.exp(b_last))
                o_stack = (_dot(jnp.concatenate(scores, axis=0), vp)
                           + _dot_nt(jnp.concatenate(queries, axis=0), s_t))
                kv = _dot_tn(vp, jnp.concatenate(keys, axis=1))
                d = SSM_HEADDIM
                states[m] = jnp.concatenate([s_t[0:d, :] * decays[0] + kv[0:d, 0:SSM_STATE],
                                             s_t[d:, :] * decays[1] + kv[d:, SSM_STATE:]], axis=0)
                o_pair = jnp.where(lane < SSM_HEADDIM, o_stack[0:C, :], o_stack[C:, :])
                y_scr[rs, ps] = o_pair + dvec_ref[:, ps] * xs_c[:, ps]
        for m in range(n_pairs):
            st[s, m] = states[m]
    y = y_scr[...] * _silu(hz_ref[...])
    gw = SSM_INNER // SSM_GROUPS
    for grp in range(SSM_GROUPS):
        gs = slice(grp * gw, (grp + 1) * gw)
        yg = y[:, gs]
        ms = jnp.mean(yg * yg, axis=-1, keepdims=True)
        o_ref[:, gs] = (yg * lax.rsqrt(ms + RMS_EPS) * nw_ref[:, gs]).astype(BF16)

    @pl.when(tstep == pl.num_programs(1) - 1)
    def _():
        for s in range(n_seq):
            co_ref[s] = buf[s, pad:8, :]
            for m in range(n_pairs):
                so_ref[s, m] = st[s, m]


def _ssd_call(h, col_z, col_x, col_dt, cache, s0, cw, cb, dtb, alog, dvec, nw, out_buf, n_batch, seq_len, row_off,
              sample):
    buf_in, out_sds, aliases, total = _mix_target(out_buf, 11)
    n_seq, n_chunk, C, rows, grid, rb, seq_blk = _seq_layout(n_batch, seq_len, row_off, sample, SSD_PROMPT_TILE, total)
    L = rows // n_seq
    hist = SSM_CONV - 1
    n_pairs = SSM_HEADS // 2
    body = functools.partial(_ssd_body, n_seq, n_chunk, C, sample, True)
    if total is not None:
        body = _with_tail_fill(body, grid[0] - 1, 12)
    colspec = lambda c0, w: pl.BlockSpec((rows, w), lambda i, t: (rb(i, t), c0 // w))
    fix2 = lambda i, t: (0, 0)
    c_spec = pl.BlockSpec((n_seq, hist, SSM_CONV_DIM), lambda i, t: (seq_blk(i), 0, 0))
    st_spec = pl.BlockSpec((n_seq, n_pairs, HEAD_W, SSM_STATE), lambda i, t: (seq_blk(i), 0, 0, 0))
    return pl.pallas_call(
        body,
        grid=grid,
        in_specs=[colspec(col_z, 512), colspec(col_x, SSM_CONV_DIM), colspec(col_dt, LANE), c_spec, st_spec,
                  pl.BlockSpec((SSM_CONV, SSM_CONV_DIM), fix2), pl.BlockSpec((1, SSM_CONV_DIM), fix2),
                  pl.BlockSpec((1, LANE), fix2), pl.BlockSpec((1, LANE), fix2),
                  pl.BlockSpec((1, SSM_INNER), fix2), pl.BlockSpec((1, SSM_INNER), fix2),
                  pl.BlockSpec(memory_space=pl.ANY)],
        out_specs=[pl.BlockSpec((rows, 512), lambda i, t: (rb(i, t), 0)), c_spec, st_spec],
        out_shape=[out_sds,
                   jax.ShapeDtypeStruct((n_batch, hist, SSM_CONV_DIM), F32),
                   jax.ShapeDtypeStruct((n_batch, n_pairs, HEAD_W, SSM_STATE), F32)],
        scratch_shapes=[pltpu.VMEM((n_seq, n_pairs, HEAD_W, SSM_STATE), F32),
                        pltpu.VMEM((n_seq, 8 + L, SSM_CONV_DIM), F32),
                        pltpu.VMEM((n_seq, 8 + L, SSM_CONV_DIM), F32),
                        pltpu.VMEM((rows, SSM_CONV_DIM), F32),
                        pltpu.VMEM((rows, SSM_INNER), F32)],
        input_output_aliases=aliases,
        compiler_params=_cparams(2),
        name="ssd" + ("_sample" if sample else "_prompt"),
    )(h, h, h, cache, s0, cw, cb, dtb, alog, dvec, nw, buf_in)


def _pad_heads(w, n_heads, width):
    lead = w.shape[:-1]
    w = w.reshape(lead + (n_heads, width))
    w = jnp.pad(w, [(0, 0)] * len(lead) + [(0, 0), (0, HEAD_W - width)])
    return w.reshape(lead + (n_heads * HEAD_W,))


def _row(v):
    return v.reshape(1, -1).astype(F32)


def kernel(x_prompt, x_sample, state_gla, cache_conformer, state_hgrn, state_ssm, cache_mamba_conv, w_in_even, w_gla_gate_lr, b_gla_gate, gla_norm_w, conf_conv_w, conf_conv_b, conf_ln_g, conf_ln_b, w_out_even, w_in_odd, hgrn_lower_bounds, hgrn_norm_w, mamba_conv_w, mamba_conv_b, mamba_dt_bias, mamba_a_log, mamba_d, mamba_norm_w, w_out_odd, ln1_g, ln1_b, ln2_g, ln2_b, router_w, router_b, expert_w_gate, expert_b_gate, expert_w_up, expert_b_up, expert_w_down, expert_b_down):
    bp, lp, _ = x_prompt.shape
    bs, ls, _ = x_sample.shape
    tp, ts = bp * lp, bs * ls
    x = (x_prompt.reshape(tp, D_MODEL), x_sample.reshape(ts, D_MODEL))

    def router_params(layer):
        return router_w[layer].T.astype(BF16), router_b[layer].astype(F32).reshape(N_EXPERTS, 1)

    def finish_layer(layer, x, mix_a, mix_b, w_out):
        rwt, rb = router_params(layer)
        x1, xp, gates, eidx, rank, counts = _outproj_ln_router(
            mix_a, mix_b, x[0], x[1], w_out[:512].astype(BF16), w_out[512:].astype(BF16),
            _row(ln1_g[layer]), _row(ln1_b[layer]), rwt, rb)
        return _moe(layer, x1, xp, gates, eidx, rank, counts, _row(ln2_g[layer]), _row(ln2_b[layer]),
                    expert_w_gate, expert_b_gate, expert_w_up, expert_b_up, expert_w_down, expert_b_down, tp)

    mix_rows = tp + ts

    wi = w_in_even[0]
    wq, wk, wv, wg, wlr, wglu = jnp.split(wi, [256, 512, 1024, 1536, 1552], axis=1)
    w_even = jnp.concatenate([_pad_heads(wq, GLA_HEADS, GLA_DK), _pad_heads(wk, GLA_HEADS, GLA_DK), wv, wg, wglu,
                              jnp.pad(wlr, ((0, 0), (0, LANE - GLA_RANK)))], axis=1).astype(BF16)
    cols_gla = (0, 512, 1024, 1536, 3072)
    col_a, col_gate = 2048, 2560
    h = _inproj(x[0], x[1], w_even)
    wlr_p = jnp.pad(_pad_heads(w_gla_gate_lr[0], GLA_HEADS, GLA_DK), ((0, LANE - GLA_RANK), (0, 0)))
    blr_p = _row(_pad_heads(b_gla_gate[0], GLA_HEADS, GLA_DK))
    nw = _row(gla_norm_w[0])
    conf_args = (conf_conv_w[0], _row(conf_conv_b[0]), _row(conf_ln_g[0]), _row(conf_ln_b[0]))
    s0_p = jnp.zeros((bp, GLA_HEADS, GLA_DK, HEAD_W), F32)
    s0_s = state_gla[0]
    mix_a, sg_p = _gla_call("gla", h, cols_gla, (wlr_p, blr_p), nw, s0_p, mix_rows, bp, lp, 0, False)
    mix_a, sg_s = _gla_call("gla", h, cols_gla, (wlr_p, blr_p), nw, s0_s, mix_a, bs, ls, tp, True)
    mix_b, cc_p = _conf_call(h, col_a, col_gate, jnp.zeros((bp,) + cache_conformer.shape[2:], F32), *conf_args,
                             mix_rows, bp, lp, 0, False)
    mix_b, cc_s = _conf_call(h, col_a, col_gate, cache_conformer[0], *conf_args, mix_b, bs, ls, tp, True)
    x = finish_layer(0, x, mix_a, mix_b, w_out_even[0])
    gla_p, gla_s = sg_p[None], sg_s[None]
    conf_p, conf_s = cc_p[None], cc_s[None]

    lb_cum = jnp.cumsum(jax.nn.softmax(hgrn_lower_bounds.astype(F32), axis=0), axis=0)
    lower_bound = _row((lb_cum - lb_cum[0])[1])
    wo = w_in_odd[0]
    w_odd = jnp.concatenate([wo[:, 2560:3584], wo[:, :2560],
                             jnp.pad(wo[:, 3584:], ((0, 0), (0, LANE - SSM_HEADS)))], axis=1).astype(BF16)
    h = _inproj(x[0], x[1], w_odd)
    cols_hgrn = (1024, 1536, 2048, 2560)
    col_z, col_x, col_dt = 3072, 0, 3584
    nw = _row(hgrn_norm_w[0])
    mix_a, sh_p = _gla_call("hgrn", h, cols_hgrn, lower_bound, nw,
                            jnp.zeros((bp, HGRN_HEADS, HEAD_W, HEAD_W), F32), mix_rows, bp, lp, 0, False)
    mix_a, sh_s = _gla_call("hgrn", h, cols_hgrn, lower_bound, nw, state_hgrn[0], mix_a, bs, ls, tp, True)

    def pair_states(s):
        return jnp.swapaxes(s, 2, 3).reshape(s.shape[0], SSM_HEADS // 2, HEAD_W, SSM_STATE)

    def unpair_states(s):
        return jnp.swapaxes(s.reshape(s.shape[0], SSM_HEADS, SSM_HEADDIM, SSM_STATE), 2, 3)

    pad8 = lambda v: jnp.pad(v.astype(F32), (0, LANE - SSM_HEADS)).reshape(1, LANE)
    ssd_args = (mamba_conv_w[0], _row(mamba_conv_b[0]), pad8(mamba_dt_bias[0]), pad8(mamba_a_log[0]),
                _row(jnp.repeat(mamba_d[0], SSM_HEADDIM)), _row(mamba_norm_w[0]))
    mix_b, cm_p, ss_p = _ssd_call(h, col_z, col_x, col_dt, jnp.zeros((bp,) + cache_mamba_conv.shape[2:], F32),
                                  jnp.zeros((bp, SSM_HEADS // 2, HEAD_W, SSM_STATE), F32), *ssd_args,
                                  mix_rows, bp, lp, 0, False)
    mix_b, cm_s, ss_s = _ssd_call(h, col_z, col_x, col_dt, cache_mamba_conv[0], pair_states(state_ssm[0]),
                                  *ssd_args, mix_b, bs, ls, tp, True)
    y_prompt, y_sample = finish_layer(1, x, mix_a, mix_b, w_out_odd[0])
    y_prompt = y_prompt.reshape(bp, lp, D_MODEL)
    y_sample = y_sample.reshape(bs, ls, D_MODEL)
    return (y_prompt, y_sample, gla_p, gla_s, conf_p, conf_s, sh_p[None], sh_s[None],
            unpair_states(ss_p)[None], unpair_states(ss_s)[None], cm_p[None], cm_s[None])
```

```python
import functools

import jax
import jax.numpy as jnp
from jax import lax
from jax.experimental import pallas as pl
from jax.experimental.pallas import tpu as pltpu
from jax.experimental.pallas import tpu_sc as plsc

F32 = jnp.float32
BF16 = jnp.bfloat16
I32 = jnp.int32
U32 = jnp.uint32

D_MODEL = 1024
DEPTH = 2
DEEPNORM_ALPHA = (2.0 * DEPTH) ** 0.25
LN_EPS = 1e-5
RMS_EPS = 1e-6
LANE = 128
HEAD_W = 128
GLA_HEADS, GLA_DK, GLA_RANK, GLA_TAU = 4, 64, 16, 16.0
CONF_DIM, CONF_WIDTH = 512, 31
HGRN_HEADS, HGRN_DK = 4, 128
SSM_HEADS, SSM_HEADDIM, SSM_STATE, SSM_GROUPS, SSM_CONV = 8, 64, 128, 2, 4
SSM_INNER = SSM_HEADS * SSM_HEADDIM
SSM_CONV_DIM = SSM_INNER + 2 * SSM_GROUPS * SSM_STATE
N_EXPERTS, TOP_K = 32, 4
SWIGLU_ALPHA, SWIGLU_LIMIT = 1.702, 7.0
SCAN_CHUNK = 64
PROMPT_TILE = 512
SSD_PROMPT_TILE = 512
SAMPLE_SEQS = 16
TOKEN_TILE = 512
WIDE_TOKEN_TILE = 1024
MOE_ROWS = 512
IN_SLOTS = 4
SC_CORES, SC_SUBCORES = 2, 16
SC_ROWS = 64
SC_SCATTER_ROWS = 32
VMEM_LIMIT = 56 * 1024 * 1024


def _cparams(n_axes):
    return pltpu.CompilerParams(dimension_semantics=("arbitrary",) * n_axes, vmem_limit_bytes=VMEM_LIMIT)


def _silu(x):
    return x * jax.nn.sigmoid(x)


def _softplus(x):
    return jnp.maximum(x, 0.0) + jnp.log(1.0 + jnp.exp(-jnp.abs(x)))


def _log_sigmoid(x):
    return jnp.minimum(x, 0.0) - jnp.log(1.0 + jnp.exp(-jnp.abs(x)))


def _layernorm(y, g, b):
    mu = jnp.mean(y, axis=-1, keepdims=True)
    d = y - mu
    var = jnp.mean(d * d, axis=-1, keepdims=True)
    return d * lax.rsqrt(var + LN_EPS) * g + b


def _dot(a, b):
    return jnp.dot(a.astype(BF16), b.astype(BF16), preferred_element_type=F32)


def _dot_nt(a, b):
    return lax.dot_general(a.astype(BF16), b.astype(BF16), (((1,), (1,)), ((), ())), preferred_element_type=F32)


def _dot_tn(a, b):
    return lax.dot_general(a.astype(BF16), b.astype(BF16), (((0,), (0,)), ((), ())), preferred_element_type=F32)


def _tri(c):
    r = lax.broadcasted_iota(I32, (c, c), 0)
    k = lax.broadcasted_iota(I32, (c, c), 1)
    return r >= k


def _pair_specs(tm, n_first, width):
    return [pl.BlockSpec((tm, width), lambda i: (jnp.minimum(i, n_first - 1), 0)),
            pl.BlockSpec((tm, width), lambda i: (jnp.maximum(i - n_first, 0), 0))]


def _pair_tile(n_first, xa_ref, xb_ref):
    return jnp.where(pl.program_id(0) < n_first, xa_ref[...], xb_ref[...])


def _inproj_body(n_first, xa_ref, xb_ref, w_ref, o_ref):
    xb = _pair_tile(n_first, xa_ref, xb_ref).astype(BF16)
    n = w_ref.shape[1]
    for c0 in range(0, n, 512):
        c1 = min(c0 + 512, n)
        o_ref[:, c0:c1] = jnp.dot(xb, w_ref[:, c0:c1], preferred_element_type=F32)


def _inproj(xa, xb, w):
    k, n = w.shape
    t = xa.shape[0] + xb.shape[0]
    n_first = xa.shape[0] // TOKEN_TILE
    return pl.pallas_call(
        functools.partial(_inproj_body, n_first),
        grid=(t // TOKEN_TILE,),
        in_specs=_pair_specs(TOKEN_TILE, n_first, k) + [pl.BlockSpec((k, n), lambda i: (0, 0))],
        out_specs=pl.BlockSpec((TOKEN_TILE, n), lambda i: (i, 0)),
        out_shape=jax.ShapeDtypeStruct((t, n), F32),
        compiler_params=_cparams(1),
        name="inproj",
    )(xa, xb, w)


def _pack_halves(y):
    half = y.shape[1] // 2
    hi = lax.bitcast_convert_type(y[:, :half].astype(BF16).astype(F32), U32)
    lo = lax.bitcast_convert_type(y[:, half:].astype(BF16).astype(F32), U32)
    return (hi & jnp.uint32(0xFFFF0000)) | (lo >> 16)


def _unpack_halves(w):
    hi = lax.bitcast_convert_type(w & jnp.uint32(0xFFFF0000), F32)
    lo = lax.bitcast_convert_type(w << 16, F32)
    return hi, lo


def _outproj_body(n_first, a_ref, b_ref, xa_ref, xb_ref, wa_ref, wb_ref, g_ref, be_ref, rwt_ref, rb_ref,
                  x1_ref, xp_ref, gate_ref, idx_ref, rank_ref, cnt_ref, carry):
    @pl.when(pl.program_id(0) == 0)
    def _():
        carry[...] = jnp.zeros(carry.shape, F32)

    mix = (jnp.dot(a_ref[...], wa_ref[...], preferred_element_type=F32)
           + jnp.dot(b_ref[...], wb_ref[...], preferred_element_type=F32))
    x1 = _layernorm(DEEPNORM_ALPHA * _pair_tile(n_first, xa_ref, xb_ref) + mix, g_ref[...], be_ref[...])
    x1_ref[...] = x1
    xp_ref[...] = _pack_halves(x1)
    logits = _dot_nt(rwt_ref[...], x1) + rb_ref[...]
    tm = logits.shape[1]
    expert = lax.broadcasted_iota(I32, logits.shape, 0)
    vals, idxs = [], []
    for _ in range(TOP_K):
        m = jnp.max(logits, axis=0, keepdims=True)
        sel = jnp.min(jnp.where(logits == m, expert, N_EXPERTS), axis=0, keepdims=True)
        vals.append(m)
        idxs.append(sel)
        logits = jnp.where(expert == sel, -jnp.inf, logits)
    exps = [jnp.exp(v - vals[0]) for v in vals]
    inv = 1.0 / functools.reduce(lambda p, q: p + q, exps)
    chosen = jnp.zeros(logits.shape, F32)
    for k in range(TOP_K):
        chosen = chosen + (expert == idxs[k]).astype(F32)
    earlier = lax.broadcasted_iota(I32, (tm, tm), 0) < lax.broadcasted_iota(I32, (tm, tm), 1)
    before = carry[...] + jnp.dot(chosen.astype(BF16), earlier.astype(BF16), preferred_element_type=F32)
    choice = lax.broadcasted_iota(I32, (8, tm), 0)
    gates = jnp.zeros((8, tm), F32)
    eidx = jnp.zeros((8, tm), I32)
    ranks = jnp.zeros((8, tm), F32)
    for k in range(TOP_K):
        rk = jnp.sum(jnp.where(expert == idxs[k], before, 0.0), axis=0, keepdims=True)
        gates = jnp.where(choice == k, exps[k] * inv, gates)
        eidx = jnp.where(choice == k, idxs[k], eidx)
        ranks = jnp.where(choice == k, rk, ranks)
    gate_ref[...] = gates
    idx_ref[...] = eidx
    rank_ref[...] = ranks.astype(I32)
    carry[...] = carry[...] + jnp.sum(chosen, axis=1, keepdims=True)
    cnt_ref[...] = carry[...].astype(I32)


def _outproj_ln_router(a, b, xa, xb, wa, wb, g, be, rwt, rb):
    t = xa.shape[0] + xb.shape[0]
    tm = WIDE_TOKEN_TILE
    n_first = xa.shape[0] // tm
    row = lambda i: (i, 0)
    col = lambda i: (0, i)
    fix = lambda i: (0, 0)
    return pl.pallas_call(
        functools.partial(_outproj_body, n_first),
        grid=(t // tm,),
        in_specs=[pl.BlockSpec((tm, 512), row), pl.BlockSpec((tm, 512), row)] + _pair_specs(tm, n_first, D_MODEL)
        + [pl.BlockSpec((512, D_MODEL), fix), pl.BlockSpec((512, D_MODEL), fix),
           pl.BlockSpec((1, D_MODEL), fix), pl.BlockSpec((1, D_MODEL), fix),
           pl.BlockSpec((N_EXPERTS, D_MODEL), fix), pl.BlockSpec((N_EXPERTS, 1), fix)],
        out_specs=[pl.BlockSpec((tm, D_MODEL), row), pl.BlockSpec((tm, 512), row),
                   pl.BlockSpec((8, tm), col), pl.BlockSpec((8, tm), col), pl.BlockSpec((8, tm), col),
                   pl.BlockSpec((N_EXPERTS, 1), fix)],
        out_shape=[jax.ShapeDtypeStruct((t, D_MODEL), F32), jax.ShapeDtypeStruct((t, 512), U32),
                   jax.ShapeDtypeStruct((8, t), F32), jax.ShapeDtypeStruct((8, t), I32),
                   jax.ShapeDtypeStruct((8, t), I32), jax.ShapeDtypeStruct((N_EXPERTS, 1), I32)],
        scratch_shapes=[pltpu.VMEM((N_EXPERTS, 1), F32)],
        compiler_params=_cparams(1),
        name="outproj_ln_router",
    )(a, b, xa, xb, wa, wb, g, be, rwt, rb)


def _sc_mesh():
    return plsc.VectorSubcoreMesh(core_axis_name="c", subcore_axis_name="s")


def _sc_scatter_rows(src, dest, n_out):
    n_src, w = src.shape
    n_dst = dest.shape[1]
    workers = SC_CORES * SC_SUBCORES
    per_worker = n_src // workers
    chunks = per_worker // SC_SCATTER_ROWS
    assert n_src == workers * chunks * SC_SCATTER_ROWS
    idx = dest.T.reshape(n_dst, workers, chunks, SC_SCATTER_ROWS)

    @functools.partial(pl.kernel, mesh=_sc_mesh(), out_type=jax.ShapeDtypeStruct((n_out, w), src.dtype),
                       scratch_types=[pltpu.VMEM((n_dst, chunks, SC_SCATTER_ROWS), I32)]
                       + [pltpu.VMEM((SC_SCATTER_ROWS, w), src.dtype)] * 2 + [pltpu.SemaphoreType.DMA] * 4)
    def scatter(src_hbm, idx_hbm, out_hbm, idx_v, rows_a, rows_b, sem_ra, sem_rb, sem_wa, sem_wb):
        worker = lax.axis_index("s") * SC_CORES + lax.axis_index("c")
        base = worker * per_worker
        for k in range(n_dst):
            pltpu.sync_copy(idx_hbm.at[k, worker], idx_v.at[k])

        def read(c, rows, sem):
            return pltpu.async_copy(src_hbm.at[pl.ds(pl.multiple_of(base + c * SC_SCATTER_ROWS, 8), SC_SCATTER_ROWS)],
                                    rows, sem)

        def write_all(pending_read, c, rows, sem):
            pending_read.wait()
            return [pltpu.async_copy(rows, out_hbm.at[idx_v.at[k, c]], sem) for k in range(n_dst)]

        @pl.loop(0, chunks // 2)
        def _(p):
            read_a = read(2 * p, rows_a, sem_ra)
            read_b = read(2 * p + 1, rows_b, sem_rb)
            writes = write_all(read_a, 2 * p, rows_a, sem_wa) + write_all(read_b, 2 * p + 1, rows_b, sem_wb)
            for wr in writes:
                wr.wait()

        if chunks % 2:
            for wr in write_all(read(chunks - 1, rows_a, sem_ra), chunks - 1, rows_a, sem_wa):
                wr.wait()

    return scatter(src, idx)


def _sc_gather_rows(table, idx):
    n, w = idx.shape[0], table.shape[1]
    workers = SC_CORES * SC_SUBCORES
    chunks = n // (workers * SC_ROWS)
    assert n == workers * chunks * SC_ROWS and chunks % 2 == 0
    idx = idx.reshape(workers, chunks, SC_ROWS)

    @functools.partial(pl.kernel, mesh=_sc_mesh(), out_type=jax.ShapeDtypeStruct((n, w), table.dtype),
                       scratch_types=[pltpu.VMEM((chunks, SC_ROWS), I32)] + [pltpu.VMEM((SC_ROWS, w), table.dtype)] * 2
                       + [pltpu.SemaphoreType.DMA] * 4)
    def gather(table_hbm, idx_hbm, out_hbm, idx_v, rows_a, rows_b, sem_ra, sem_rb, sem_wa, sem_wb):
        worker = lax.axis_index("s") * SC_CORES + lax.axis_index("c")
        base = worker * (chunks * SC_ROWS)
        pltpu.sync_copy(idx_hbm.at[worker], idx_v)

        def out_rows(c):
            return out_hbm.at[pl.ds(pl.multiple_of(base + c * SC_ROWS, 8), SC_ROWS)]

        @pl.loop(0, chunks // 2)
        def _(p):
            read_a = pltpu.async_copy(table_hbm.at[idx_v.at[2 * p]], rows_a, sem_ra)
            read_b = pltpu.async_copy(table_hbm.at[idx_v.at[2 * p + 1]], rows_b, sem_rb)
            read_a.wait()
            write_a = pltpu.async_copy(rows_a, out_rows(2 * p), sem_wa)
            read_b.wait()
            write_b = pltpu.async_copy(rows_b, out_rows(2 * p + 1), sem_wb)
            write_a.wait()
            write_b.wait()

    return gather(table, idx)


def _experts_body(b0_ref, nb_ref, last_ref, nt_ref, xs_hbm, wg_ref, bg_ref, wu_ref, bu_ref, wd_ref, bd_ref, o_hbm,
                  wg_s, wu_s, wd_s, xbuf, obuf, sem_in, sem_out):
    e = pl.program_id(0)
    first_blk, n_blk, last_valid, n_total = b0_ref[e], nb_ref[e], last_ref[e], nt_ref[0]
    quarter = MOE_ROWS // 4

    def rows_of(g):
        return pl.ds(pl.multiple_of(g * MOE_ROWS, MOE_ROWS), MOE_ROWS)

    def fetch(g, slot):
        return pltpu.make_async_copy(xs_hbm.at[rows_of(g)], xbuf.at[slot], sem_in.at[slot])

    def put(g, slot):
        return pltpu.make_async_copy(obuf.at[slot], o_hbm.at[rows_of(g)], sem_out.at[slot])

    lead = IN_SLOTS - 1
    for first in range(lead):
        @pl.when((e == 0) & (n_total > first))
        def _():
            fetch(first, first).start()

    @pl.when(n_blk > 0)
    def _():
        wg_s[...] = wg_ref[...].astype(BF16)
        wu_s[...] = wu_ref[...].astype(BF16)
        wd_s[...] = wd_ref[...].astype(BF16)

    def compute(islot, slot, rows):
        half = D_MODEL // 2
        x_hi, x_lo = _unpack_halves(xbuf[islot, 0:rows, :])
        x_hi = x_hi.astype(BF16)
        x_lo = x_lo.astype(BF16)
        g = (jnp.dot(x_hi, wg_s[:half, :], preferred_element_type=F32)
             + jnp.dot(x_lo, wg_s[half:, :], preferred_element_type=F32) + bg_ref[...])
        u = (jnp.dot(x_hi, wu_s[:half, :], preferred_element_type=F32)
             + jnp.dot(x_lo, wu_s[half:, :], preferred_element_type=F32) + bu_ref[...])
        g = jnp.minimum(g, SWIGLU_LIMIT)
        u = jnp.clip(u, -SWIGLU_LIMIT, SWIGLU_LIMIT)
        hmid = (u + 1.0) * (g * jax.nn.sigmoid(SWIGLU_ALPHA * g))
        out = jnp.dot(hmid.astype(BF16), wd_s[...], preferred_element_type=F32) + bd_ref[...]
        obuf[slot, 0:rows, :] = _pack_halves(out)

    def block(j, carry):
        g = first_blk + j
        slot = lax.rem(g, 2)
        islot = lax.rem(g, IN_SLOTS)
        fetch(g, islot).wait()

        @pl.when(g + lead < n_total)
        def _():
            fetch(g + lead, lax.rem(g + lead, IN_SLOTS)).start()

        @pl.when(g >= 2)
        def _():
            put(g - 2, slot).wait()

        valid = jnp.where(j == n_blk - 1, last_valid, MOE_ROWS)

        for rows in range(quarter, MOE_ROWS + 1, quarter):
            @pl.when((valid > rows - quarter) & (valid <= rows))
            def _():
                compute(islot, slot, rows)
                if rows < MOE_ROWS:
                    obuf[slot, rows:, :] = jnp.zeros((MOE_ROWS - rows, obuf.shape[2]), obuf.dtype)

        put(g, slot).start()
        return carry

    lax.fori_loop(0, n_blk, block, 0)

    @pl.when((e == N_EXPERTS - 1) & (n_total >= 2))
    def _():
        put(n_total - 2, lax.rem(n_total, 2)).wait()

    @pl.when((e == N_EXPERTS - 1) & (n_total >= 1))
    def _():
        put(n_total - 1, lax.rem(n_total - 1, 2)).wait()


def _experts(layer, first_blk, n_blk, last_valid, xs, wg, bg, wu, bu, wd, bd):
    n_rows, w = xs.shape
    wsel = lambda e, b0, nb, lv, nt: (layer, e, 0, 0)
    wspec = pl.BlockSpec((None, None, D_MODEL, D_MODEL), wsel)
    bspec = pl.BlockSpec((None, None, 1, D_MODEL), wsel)
    bias = lambda b: b.reshape(b.shape[0], b.shape[1], 1, b.shape[2])
    return pl.pallas_call(
        _experts_body,
        grid_spec=pltpu.PrefetchScalarGridSpec(
            num_scalar_prefetch=4,
            grid=(N_EXPERTS,),
            in_specs=[pl.BlockSpec(memory_space=pl.ANY), wspec, bspec, wspec, bspec, wspec, bspec],
            out_specs=pl.BlockSpec(memory_space=pl.ANY),
            scratch_shapes=[pltpu.VMEM((D_MODEL, D_MODEL), BF16)] * 3
            + [pltpu.VMEM((IN_SLOTS, MOE_ROWS, w), U32), pltpu.VMEM((2, MOE_ROWS, w), U32),
               pltpu.SemaphoreType.DMA((IN_SLOTS,)), pltpu.SemaphoreType.DMA((2,))],
        ),
        out_shape=jax.ShapeDtypeStruct((n_rows, w), U32),
        compiler_params=_cparams(1),
        name="experts",
    )(first_blk, n_blk, last_valid, jnp.sum(n_blk).reshape(1), xs, wg, bias(bg), wu, bias(bu), wd, bias(bd))


def _combine_body(n_first, o0_ref, o1_ref, o2_ref, o3_ref, gt_ref, x_ref, g_ref, b_ref, ya_ref, yb_ref=None):
    half = D_MODEL // 2
    gates = gt_ref[...]
    hi = jnp.zeros((x_ref.shape[0], half), F32)
    lo = jnp.zeros((x_ref.shape[0], half), F32)
    for k, o_ref in enumerate((o0_ref, o1_ref, o2_ref, o3_ref)):
        h, l = _unpack_halves(o_ref[...])
        gk = gates[:, k:k + 1]
        hi = hi + gk * h
        lo = lo + gk * l
    x = x_ref[...]
    y_hi = DEEPNORM_ALPHA * x[:, :half] + hi
    y_lo = DEEPNORM_ALPHA * x[:, half:] + lo
    mu = (jnp.sum(y_hi, axis=-1, keepdims=True) + jnp.sum(y_lo, axis=-1, keepdims=True)) * (1.0 / D_MODEL)
    d_hi = y_hi - mu
    d_lo = y_lo - mu
    var = (jnp.sum(d_hi * d_hi, axis=-1, keepdims=True) + jnp.sum(d_lo * d_lo, axis=-1, keepdims=True)) * (1.0 / D_MODEL)
    r = lax.rsqrt(var + LN_EPS)
    out_hi = d_hi * r * g_ref[:, :half] + b_ref[:, :half]
    out_lo = d_lo * r * g_ref[:, half:] + b_ref[:, half:]

    def write(y_ref):
        y_ref[:, :half] = out_hi
        y_ref[:, half:] = out_lo

    if yb_ref is None:
        write(ya_ref)
    else:
        pl.when(pl.program_id(0) < n_first)(lambda: write(ya_ref))
        pl.when(pl.program_id(0) >= n_first)(lambda: write(yb_ref))


def _combine_ln(o4, gates, x, g, b, t_first=None):
    t = x.shape[0]
    tm = WIDE_TOKEN_TILE
    row = lambda i: (i, 0)
    fix = lambda i: (0, 0)
    choice = lambda k: pl.BlockSpec((tm, 512), lambda i: (k * (t // tm) + i, 0))
    if t_first is None:
        n_first = None
        out_specs = pl.BlockSpec((tm, D_MODEL), row)
        out_shape = jax.ShapeDtypeStruct((t, D_MODEL), F32)
    else:
        n_first = t_first // tm
        out_specs = [pl.BlockSpec((tm, D_MODEL), lambda i: (jnp.minimum(i, n_first - 1), 0)),
                     pl.BlockSpec((tm, D_MODEL), lambda i: (jnp.maximum(i - n_first, 0), 0))]
        out_shape = [jax.ShapeDtypeStruct((t_first, D_MODEL), F32), jax.ShapeDtypeStruct((t - t_first, D_MODEL), F32)]
    return pl.pallas_call(
        functools.partial(_combine_body, n_first),
        grid=(t // tm,),
        in_specs=[choice(0), choice(1), choice(2), choice(3), pl.BlockSpec((tm, TOP_K), row),
                  pl.BlockSpec((tm, D_MODEL), row), pl.BlockSpec((1, D_MODEL), fix), pl.BlockSpec((1, D_MODEL), fix)],
        out_specs=out_specs,
        out_shape=out_shape,
        compiler_params=_cparams(1),
        name="combine_ln",
    )(o4, o4, o4, o4, gates, x, g, b)


def _moe(layer, x1, xp, gates, eidx, rank, counts, ln_g, ln_b, wg, bg, wu, bu, wd, bd, t_first=None):
    t = x1.shape[0]
    bm = MOE_ROWS
    n_blocks = t * TOP_K // bm + N_EXPERTS
    n_rows = n_blocks * bm
    cnt = counts[:, 0]
    padded = (cnt + bm - 1) // bm * bm
    pad_end = jnp.cumsum(padded)
    pad_start = pad_end - padded
    e = eidx[:TOP_K]
    start = jnp.sum(jnp.where(e[:, :, None] == jnp.arange(N_EXPERTS, dtype=I32), pad_start, 0), axis=-1)
    dest = (start + rank[:TOP_K]).T
    n_blk = padded // bm
    last_valid = cnt - (n_blk - 1) * bm
    xs = _sc_scatter_rows(xp, dest, n_rows)
    outs = _experts(layer, pad_start // bm, n_blk, last_valid, xs, wg, bg, wu, bu, wd, bd)
    o4 = _sc_gather_rows(outs, dest.T.reshape(-1))
    return _combine_ln(o4, gates[:TOP_K].T, x1, ln_g, ln_b, t_first)


def _split2(x):
    hi = x.astype(BF16)
    return hi, (x - hi.astype(F32)).astype(BF16)


def _split3(x):
    hi = x.astype(BF16)
    rem = x - hi.astype(F32)
    mid = rem.astype(BF16)
    return hi, mid, (rem - mid.astype(F32)).astype(BF16)


def _chunk_cumsum(g, C):
    rows = g.shape[0]
    group = max(C, min(rows, LANE))
    r = lax.broadcasted_iota(I32, (group, group), 0)
    c = lax.broadcasted_iota(I32, (group, group), 1)
    tri = ((r >= c) & (r // C == c // C)).astype(BF16)
    parts = _split3(g)
    sums = [functools.reduce(lambda p, q: p + q,
                             [jnp.dot(tri, part[r0:r0 + group, :], preferred_element_type=F32) for part in parts])
            for r0 in range(0, rows, group)]
    return sums[0] if len(sums) == 1 else jnp.concatenate(sums, axis=0)


def _gla_batched_step(q, k, v, g, C, n_seq, n_heads, state_of, o_scr):
    rows = n_seq * C
    wide = n_seq * HEAD_W
    mid = max(C // 2 - 1, 0)
    r = lax.broadcasted_iota(I32, (rows, rows), 0)
    c = lax.broadcasted_iota(I32, (rows, rows), 1)
    same = (r // C) == (c // C)
    causal = same & (r >= c)
    parts = _split3(g)
    summed = lambda mask: functools.reduce(lambda p, q_: p + q_, [
        jnp.dot(mask.astype(BF16), part, preferred_element_type=F32) for part in parts])
    b = summed(causal)
    b_mid = summed(same & ((c % C) <= mid))
    b_last = summed(same)
    qe_hi, qe_lo = _split2(q * jnp.exp(b - b_mid))
    ke_hi, ke_lo = _split2(k * jnp.exp(b_mid - b))
    q_state = (q * jnp.exp(b)).astype(BF16)
    k_state = (k * jnp.exp(b_last - b)).astype(BF16)
    decay_parts = _split3(jnp.exp(b_last))
    row_w = lax.broadcasted_iota(I32, (rows, wide), 0)
    blk_w = lax.broadcasted_iota(I32, (rows, wide), 1) // HEAD_W
    own = (row_w // C) == blk_w
    pick = (row_w == blk_w * C).astype(BF16)
    new_states = []
    for h in range(n_heads):
        cs = slice(h * HEAD_W, (h + 1) * HEAD_W)
        lhs = jnp.concatenate([qe_hi[:, cs], qe_hi[:, cs], qe_lo[:, cs]], axis=1)
        rhs = jnp.concatenate([ke_hi[:, cs], ke_lo[:, cs], ke_hi[:, cs]], axis=1)
        scores = jnp.where(causal, _dot_nt(lhs, rhs), 0.0)
        vh = v[:, cs].astype(BF16)
        s_cat = jnp.concatenate([state_of(s, h) for s in range(n_seq)], axis=1)
        o_full = _dot(q_state[:, cs], s_cat)
        o_state = jnp.concatenate([o_full[s * C:(s + 1) * C, s * HEAD_W:(s + 1) * HEAD_W] for s in range(n_seq)],
                                  axis=0)
        o = _dot(scores, vh) + o_state
        v_wide = jnp.where(own, jnp.concatenate([vh] * n_seq, axis=1), jnp.zeros((), BF16))
        kv = _dot_tn(k_state[:, cs], v_wide)
        decay = functools.reduce(lambda p, q_: p + q_, [
            lax.dot_general(part[:, cs], pick, (((0,), (0,)), ((), ())), preferred_element_type=F32)
            for part in decay_parts])
        new_states.append(s_cat * decay + kv)
        ms = jnp.mean(o * o, axis=-1, keepdims=True)
        o_scr[:, cs] = o * lax.rsqrt(ms + RMS_EPS)
    return new_states


def _gla_body(mode, n_seq, n_chunk, C, *refs):
    if mode == "gla":
        hq_ref, hk_ref, hv_ref, hg_ref, hlr_ref, wlr_ref, blr_ref, nw_ref, s0_ref, _, o_ref, so_ref, st, o_scr = refs
    else:
        hq_ref, hk_ref, hv_ref, hg_ref, lb_ref, nw_ref, s0_ref, _, o_ref, so_ref, st, o_scr = refs
    n_heads = 4
    n_keys = s0_ref.shape[2]
    tstep = pl.program_id(1)
    batched = n_chunk == 1 and n_seq > 1

    def padded_state(s, h):
        s_in = s0_ref[s, h]
        if n_keys < HEAD_W:
            s_in = jnp.concatenate([s_in, jnp.zeros((HEAD_W - n_keys, HEAD_W), F32)], axis=0)
        return s_in

    if not batched:
        @pl.when(tstep == 0)
        def _():
            for s in range(n_seq):
                for h in range(n_heads):
                    st[s, h] = padded_state(s, h).T

    if mode == "gla":
        q = hq_ref[...] * (GLA_DK ** -0.5)
        k = hk_ref[...]
        z = _dot(hlr_ref[...], wlr_ref[...]) + blr_ref[...]
        g = _log_sigmoid(z) * (1.0 / GLA_TAU)
    else:
        q = _silu(hq_ref[...]) * (HGRN_DK ** -0.5)
        lb = lb_ref[...]
        f = lb + (1.0 - lb) * jax.nn.sigmoid(hk_ref[...])
        k = 1.0 - f
        g = jnp.log(f)
    v = hv_ref[...]
    if batched:
        new_states = _gla_batched_step(q, k, v, g, C, n_seq, n_heads, padded_state, o_scr)
        for s in range(n_seq):
            for h in range(n_heads):
                so_ref[s, h] = new_states[h][0:n_keys, s * HEAD_W:(s + 1) * HEAD_W]
    else:
        causal = _tri(C)
        mid = max(C // 2 - 1, 0)
        b_all = _chunk_cumsum(g, C)
        for s in range(n_seq):
            states = [st[s, h] for h in range(n_heads)]
            for c in range(n_chunk):
                r0 = (s * n_chunk + c) * C
                rs = slice(r0, r0 + C)
                b, qc, kc = b_all[rs, :], q[rs, :], k[rs, :]
                b_last = b[C - 1:C, :]
                b_mid = b[mid:mid + 1, :]
                qe_hi, qe_lo = _split2(qc * jnp.exp(b - b_mid))
                ke_hi, ke_lo = _split2(kc * jnp.exp(b_mid - b))
                q_state = (qc * jnp.exp(b)).astype(BF16)
                k_state = (kc * jnp.exp(b_last - b)).astype(BF16)
                decay = jnp.exp(b_last)
                for h in range(n_heads):
                    cs = slice(h * HEAD_W, (h + 1) * HEAD_W)
                    lhs = jnp.concatenate([qe_hi[:, cs], qe_hi[:, cs], qe_lo[:, cs]], axis=1)
                    rhs = jnp.concatenate([ke_hi[:, cs], ke_lo[:, cs], ke_hi[:, cs]], axis=1)
                    scores = jnp.where(causal, _dot_nt(lhs, rhs), 0.0)
                    vh = v[rs, cs].astype(BF16)
                    o = _dot(scores, vh) + _dot_nt(q_state[:, cs], states[h])
                    states[h] = states[h] * decay[:, cs] + _dot_tn(vh, k_state[:, cs])
                    ms = jnp.mean(o * o, axis=-1, keepdims=True)
                    o_scr[rs, cs] = o * lax.rsqrt(ms + RMS_EPS)
            for h in range(n_heads):
                st[s, h] = states[h]
    o_ref[...] = (o_scr[...] * nw_ref[...] * _silu(hg_ref[...])).astype(BF16)

    if not batched:
        @pl.when(tstep == pl.num_programs(1) - 1)
        def _():
            for s in range(n_seq):
                for h in range(n_heads):
                    so_ref[s, h] = st[s, h].T[0:n_keys, :]


def _with_tail_fill(body, n_real, out_index):
    def wrapped(*refs):
        pl.when(pl.program_id(0) < n_real)(lambda: body(*refs))

        @pl.when(pl.program_id(0) >= n_real)
        def _():
            refs[out_index][...] = jnp.zeros(refs[out_index].shape, refs[out_index].dtype)

    return wrapped


def _seq_layout(n_batch, seq_len, row_off, sample, prompt_tile=PROMPT_TILE, total_rows=None):
    if sample:
        n_seq, n_chunk, C = SAMPLE_SEQS, 1, seq_len
        rows = n_seq * C
        grid = (n_batch // n_seq, 1)
        blk0 = row_off // rows
        rb = lambda i, t: blk0 + i
    else:
        n_seq, n_chunk, C = 1, prompt_tile // SCAN_CHUNK, SCAN_CHUNK
        rows = prompt_tile
        tiles = seq_len // rows
        grid = (n_batch, tiles)
        blk0 = row_off // rows
        rb = lambda i, t: blk0 + i * tiles + t
    seq_blk = lambda i: i
    if total_rows is not None:
        assert 0 < total_rows - (row_off + n_batch * seq_len) <= (rows if sample else seq_len)
        n_real, last, rb_real = grid[0], total_rows // rows - 1, rb
        grid = (n_real + 1, grid[1])
        rb = lambda i, t: jnp.minimum(rb_real(i, t), last)
        seq_blk = lambda i: jnp.minimum(i, n_real - 1)
    return n_seq, n_chunk, C, rows, grid, rb, seq_blk


def _mix_target(out_buf, alias_index):
    if hasattr(out_buf, "shape"):
        return out_buf, jax.ShapeDtypeStruct(out_buf.shape, out_buf.dtype), {alias_index: 0}, None
    return jnp.zeros((8, LANE), BF16), jax.ShapeDtypeStruct((out_buf, 512), BF16), {}, out_buf


def _gla_call(mode, h, cols, extra, nw, s0, out_buf, n_batch, seq_len, row_off, sample):
    n_alias = 9 if mode == "gla" else 7
    buf_in, out_sds, aliases, total = _mix_target(out_buf, n_alias)
    n_seq, n_chunk, C, rows, grid, rb, seq_blk = _seq_layout(n_batch, seq_len, row_off, sample, total_rows=total)
    colspec = lambda c0, w: pl.BlockSpec((rows, w), lambda i, t: (rb(i, t), c0 // w))
    fix2 = lambda i, t: (0, 0)
    in_specs = [colspec(cols[0], 512), colspec(cols[1], 512), colspec(cols[2], 512), colspec(cols[3], 512)]
    args = [h, h, h, h]
    if mode == "gla":
        wlr, blr = extra
        in_specs += [colspec(cols[4], LANE), pl.BlockSpec((LANE, 512), fix2), pl.BlockSpec((1, 512), fix2)]
        args += [h, wlr, blr]
    else:
        in_specs += [pl.BlockSpec((1, 512), fix2)]
        args += [extra]
    n_keys = s0.shape[2]
    st_spec = pl.BlockSpec((n_seq, 4, n_keys, HEAD_W), lambda i, t: (seq_blk(i), 0, 0, 0))
    in_specs += [pl.BlockSpec((1, 512), fix2), st_spec, pl.BlockSpec(memory_space=pl.ANY)]
    args += [nw, s0, buf_in]
    assert len(args) - 1 == n_alias
    o_spec = pl.BlockSpec((rows, 512), lambda i, t: (rb(i, t), 0))
    body = functools.partial(_gla_body, mode, n_seq, n_chunk, C)
    if total is not None:
        body = _with_tail_fill(body, grid[0] - 1, len(args))
    return pl.pallas_call(
        body,
        grid=grid,
        in_specs=in_specs,
        out_specs=[o_spec, st_spec],
        out_shape=[out_sds, jax.ShapeDtypeStruct((n_batch, 4, n_keys, HEAD_W), F32)],
        scratch_shapes=[pltpu.VMEM((n_seq, 4, HEAD_W, HEAD_W), F32), pltpu.VMEM((rows, 512), F32)],
        input_output_aliases=aliases,
        compiler_params=_cparams(2),
        name=mode + ("_sample" if sample else "_prompt"),
    )(*args)


def _round_bf16(x, on=True):
    return x.astype(BF16).astype(F32) if on else x


def _conf_body(n_seq, L, round_x, round_w, a_ref, gt_ref, hist_ref, w_ref, b_ref, g_ref, be_ref, _, o_ref, co_ref,
               buf, bufr, y_scr, win):
    tstep = pl.program_id(1)
    hist = CONF_WIDTH - 1
    pad = 32 - hist

    @pl.when(tstep == 0)
    def _():
        for s in range(n_seq):
            buf[s, pad:32, :] = hist_ref[s]
            bufr[s, pad:32, :] = _round_bf16(hist_ref[s], round_x)

    u = a_ref[...] * jax.nn.sigmoid(gt_ref[...])
    ur = _round_bf16(u, round_x)
    for s in range(n_seq):
        buf[s, 32:32 + L, :] = u[s * L:(s + 1) * L, :]
        bufr[s, 32:32 + L, :] = ur[s * L:(s + 1) * L, :]
    w = _round_bf16(w_ref[...], round_w)
    for s in range(n_seq):
        acc = jnp.zeros((L, CONF_DIM), F32)
        for phase in range(8):
            n_taps = (CONF_WIDTH - 1 - phase) // 8 + 1
            span = L + 8 * (n_taps - 1)
            win[phase, 0:span, :] = bufr[s, pad + phase:pad + phase + span, :]
            for a in range(n_taps):
                j = 8 * a + phase
                acc = acc + win[phase, 8 * a:8 * a + L, :] * w[j:j + 1, :]
        y_scr[s * L:(s + 1) * L, :] = _silu(_layernorm(acc + b_ref[...], g_ref[...], be_ref[...]))
        tail = buf[s, L + pad:L + 32, :]
        buf[s, pad:32, :] = tail
        tailr = bufr[s, L + pad:L + 32, :]
        bufr[s, pad:32, :] = tailr
    o_ref[...] = y_scr[...].astype(o_ref.dtype)

    @pl.when(tstep == pl.num_programs(1) - 1)
    def _():
        for s in range(n_seq):
            co_ref[s] = buf[s, pad:32, :]


def _conf_call(h, col_a, col_g, cache, w, b, g, be, out_buf, n_batch, seq_len, row_off, sample):
    buf_in, out_sds, aliases, total = _mix_target(out_buf, 7)
    n_seq, n_chunk, C, rows, grid, rb, seq_blk = _seq_layout(n_batch, seq_len, row_off, sample, total_rows=total)
    L = rows // n_seq
    hist = CONF_WIDTH - 1
    body = functools.partial(_conf_body, n_seq, L, True, sample)
    if total is not None:
        body = _with_tail_fill(body, grid[0] - 1, 8)
    colspec = lambda c0: pl.BlockSpec((rows, 512), lambda i, t: (rb(i, t), c0 // 512))
    fix2 = lambda i, t: (0, 0)
    c_spec = pl.BlockSpec((n_seq, hist, CONF_DIM), lambda i, t: (seq_blk(i), 0, 0))
    return pl.pallas_call(
        body,
        grid=grid,
        in_specs=[colspec(col_a), colspec(col_g), c_spec,
                  pl.BlockSpec((CONF_WIDTH, CONF_DIM), fix2), pl.BlockSpec((1, CONF_DIM), fix2),
                  pl.BlockSpec((1, CONF_DIM), fix2), pl.BlockSpec((1, CONF_DIM), fix2),
                  pl.BlockSpec(memory_space=pl.ANY)],
        out_specs=[pl.BlockSpec((rows, 512), lambda i, t: (rb(i, t), 0)), c_spec],
        out_shape=[out_sds, jax.ShapeDtypeStruct((n_batch, hist, CONF_DIM), F32)],
        scratch_shapes=[pltpu.VMEM((n_seq, 32 + L, CONF_DIM), F32)] * 2 + [pltpu.VMEM((rows, CONF_DIM), F32),
                                                                           pltpu.VMEM((8, L + 24, CONF_DIM), F32)],
        input_output_aliases=aliases,
        compiler_params=_cparams(2),
        name="conformer" + ("_sample" if sample else "_prompt"),
    )(h, h, cache, w, b, g, be, buf_in)


def _ssd_body(n_seq, n_chunk, C, round_x, round_w, hz_ref, hx_ref, hdt_ref, hist_ref, s0_ref, cw_ref, cb_ref, dtb_ref, alog_ref,
              dvec_ref, nw_ref, _, o_ref, co_ref, so_ref, st, buf, bufr, xbc, y_scr):
    tstep = pl.program_id(1)
    L = n_chunk * C
    hist = SSM_CONV - 1
    pad = 8 - hist
    n_pairs = SSM_HEADS // 2

    @pl.when(tstep == 0)
    def _():
        for s in range(n_seq):
            buf[s, pad:8, :] = hist_ref[s]
            bufr[s, pad:8, :] = _round_bf16(hist_ref[s], round_x)
            for m in range(n_pairs):
                st[s, m] = s0_ref[s, m]

    cw = _round_bf16(cw_ref[...], round_w)
    for s in range(n_seq):
        hx = hx_ref[s * L:(s + 1) * L, :]
        buf[s, 8:8 + L, :] = hx
        bufr[s, 8:8 + L, :] = _round_bf16(hx, round_x)
        acc = jnp.zeros((L, SSM_CONV_DIM), F32)
        for j in range(SSM_CONV):
            acc = acc + bufr[s, pad + j:pad + j + L, :] * cw[j:j + 1, :]
        xbc[s * L:(s + 1) * L, :] = _silu(acc + cb_ref[...])
        tail = buf[s, L + pad:L + 8, :]
        buf[s, pad:8, :] = tail
        tailr = bufr[s, L + pad:L + 8, :]
        bufr[s, pad:8, :] = tailr

    dt = _softplus(hdt_ref[...] + dtb_ref[...])
    la = dt * (-jnp.exp(alog_ref[...]))
    hrow = lax.broadcasted_iota(I32, (LANE, SSM_INNER), 0)
    hcol = lax.broadcasted_iota(I32, (LANE, SSM_INNER), 1) // SSM_HEADDIM
    expand = (hrow == hcol).astype(BF16)
    dtx = functools.reduce(lambda p, q: p + q,
                           [jnp.dot(part, expand, preferred_element_type=F32) for part in _split3(dt)])
    causal = _tri(C)
    tri = causal.astype(BF16)
    lane = lax.broadcasted_iota(I32, (C, HEAD_W), 1)
    bcol_all = _chunk_cumsum(la, C)
    heads_per_group = SSM_HEADS // SSM_GROUPS
    for s in range(n_seq):
        states = [st[s, m] for m in range(n_pairs)]
        for c in range(n_chunk):
            r0 = (s * n_chunk + c) * C
            rs = slice(r0, r0 + C)
            bcol = bcol_all[rs, :]
            brow = functools.reduce(lambda p, q: p + q, [
                lax.dot_general(part, tri, (((0,), (1,)), ((), ())), preferred_element_type=F32)
                for part in _split3(la[rs, :])])
            xs_c = xbc[rs, 0:SSM_INNER]
            v_c = (xs_c * dtx[rs, :]).astype(BF16)
            gmats, bms, cms = [], [], []
            for grp in range(SSM_GROUPS):
                bm = xbc[rs, SSM_INNER + grp * SSM_STATE:SSM_INNER + (grp + 1) * SSM_STATE]
                cm = xbc[rs, SSM_INNER + (SSM_GROUPS + grp) * SSM_STATE:SSM_INNER + (SSM_GROUPS + grp + 1) * SSM_STATE]
                cm_hi, cm_lo = _split2(cm)
                bm_hi, bm_lo = _split2(bm)
                gmats.append(_dot_nt(jnp.concatenate([cm_hi, cm_hi, cm_lo], axis=1),
                                     jnp.concatenate([bm_hi, bm_lo, bm_hi], axis=1)))
                bms.append(bm)
                cms.append(cm)
            for m in range(n_pairs):
                grp = (2 * m) // heads_per_group
                bm, cm, gmat = bms[grp], cms[grp], gmats[grp]
                ps = slice(m * HEAD_W, (m + 1) * HEAD_W)
                vp = v_c[:, ps]
                s_t = states[m]
                scores, queries, keys, decays = [], [], [], []
                for hh in range(2):
                    hd = 2 * m + hh
                    bc = bcol[:, hd:hd + 1]
                    br = brow[hd:hd + 1, :]
                    b_last = bcol[C - 1:C, hd:hd + 1]
                    scores.append(gmat * jnp.where(causal, jnp.exp(jnp.minimum(bc - br, 0.0)), 0.0))
                    queries.append(cm * jnp.exp(bc))
                    keys.append(bm * jnp.exp(b_last - bc))
                    decays.append(jnp.exp(b_last))
                o_stack = (_dot(jnp.concatenate(scores, axis=0), vp)
                           + _dot_nt(jnp.concatenate(queries, axis=0), s_t))
                kv = _dot_tn(vp, jnp.concatenate(keys, axis=1))
                d = SSM_HEADDIM
                states[m] = jnp.concatenate([s_t[0:d, :] * decays[0] + kv[0:d, 0:SSM_STATE],
                                             s_t[d:, :] * decays[1] + kv[d:, SSM_STATE:]], axis=0)
                o_pair = jnp.where(lane < SSM_HEADDIM, o_stack[0:C, :], o_stack[C:, :])
                y_scr[rs, ps] = o_pair + dvec_ref[:, ps] * xs_c[:, ps]
        for m in range(n_pairs):
            st[s, m] = states[m]
    y = y_scr[...] * _silu(hz_ref[...])
    gw = SSM_INNER // SSM_GROUPS
    for grp in range(SSM_GROUPS):
        gs = slice(grp * gw, (grp + 1) * gw)
        yg = y[:, gs]
        ms = jnp.mean(yg * yg, axis=-1, keepdims=True)
        o_ref[:, gs] = (yg * lax.rsqrt(ms + RMS_EPS) * nw_ref[:, gs]).astype(BF16)

    @pl.when(tstep == pl.num_programs(1) - 1)
    def _():
        for s in range(n_seq):
            co_ref[s] = buf[s, pad:8, :]
            for m in range(n_pairs):
                so_ref[s, m] = st[s, m]


def _ssd_call(h, col_z, col_x, col_dt, cache, s0, cw, cb, dtb, alog, dvec, nw, out_buf, n_batch, seq_len, row_off,
              sample):
    buf_in, out_sds, aliases, total = _mix_target(out_buf, 11)
    n_seq, n_chunk, C, rows, grid, rb, seq_blk = _seq_layout(n_batch, seq_len, row_off, sample, SSD_PROMPT_TILE, total)
    L = rows // n_seq
    hist = SSM_CONV - 1
    n_pairs = SSM_HEADS // 2
    body = functools.partial(_ssd_body, n_seq, n_chunk, C, sample, True)
    if total is not None:
        body = _with_tail_fill(body, grid[0] - 1, 12)
    colspec = lambda c0, w: pl.BlockSpec((rows, w), lambda i, t: (rb(i, t), c0 // w))
    fix2 = lambda i, t: (0, 0)
    c_spec = pl.BlockSpec((n_seq, hist, SSM_CONV_DIM), lambda i, t: (seq_blk(i), 0, 0))
    st_spec = pl.BlockSpec((n_seq, n_pairs, HEAD_W, SSM_STATE), lambda i, t: (seq_blk(i), 0, 0, 0))
    return pl.pallas_call(
        body,
        grid=grid,
        in_specs=[colspec(col_z, 512), colspec(col_x, SSM_CONV_DIM), colspec(col_dt, LANE), c_spec, st_spec,
                  pl.BlockSpec((SSM_CONV, SSM_CONV_DIM), fix2), pl.BlockSpec((1, SSM_CONV_DIM), fix2),
                  pl.BlockSpec((1, LANE), fix2), pl.BlockSpec((1, LANE), fix2),
                  pl.BlockSpec((1, SSM_INNER), fix2), pl.BlockSpec((1, SSM_INNER), fix2),
                  pl.BlockSpec(memory_space=pl.ANY)],
        out_specs=[pl.BlockSpec((rows, 512), lambda i, t: (rb(i, t), 0)), c_spec, st_spec],
        out_shape=[out_sds,
                   jax.ShapeDtypeStruct((n_batch, hist, SSM_CONV_DIM), F32),
                   jax.ShapeDtypeStruct((n_batch, n_pairs, HEAD_W, SSM_STATE), F32)],
        scratch_shapes=[pltpu.VMEM((n_seq, n_pairs, HEAD_W, SSM_STATE), F32),
                        pltpu.VMEM((n_seq, 8 + L, SSM_CONV_DIM), F32),
                        pltpu.VMEM((n_seq, 8 + L, SSM_CONV_DIM), F32),
                        pltpu.VMEM((rows, SSM_CONV_DIM), F32),
                        pltpu.VMEM((rows, SSM_INNER), F32)],
        input_output_aliases=aliases,
        compiler_params=_cparams(2),
        name="ssd" + ("_sample" if sample else "_prompt"),
    )(h, h, h, cache, s0, cw, cb, dtb, alog, dvec, nw, buf_in)


def _pad_heads(w, n_heads, width):
    lead = w.shape[:-1]
    w = w.reshape(lead + (n_heads, width))
    w = jnp.pad(w, [(0, 0)] * len(lead) + [(0, 0), (0, HEAD_W - width)])
    return w.reshape(lead + (n_heads * HEAD_W,))


def _row(v):
    return v.reshape(1, -1).astype(F32)


def kernel(x_prompt, x_sample, state_gla, cache_conformer, state_hgrn, state_ssm, cache_mamba_conv, w_in_even, w_gla_gate_lr, b_gla_gate, gla_norm_w, conf_conv_w, conf_conv_b, conf_ln_g, conf_ln_b, w_out_even, w_in_odd, hgrn_lower_bounds, hgrn_norm_w, mamba_conv_w, mamba_conv_b, mamba_dt_bias, mamba_a_log, mamba_d, mamba_norm_w, w_out_odd, ln1_g, ln1_b, ln2_g, ln2_b, router_w, router_b, expert_w_gate, expert_b_gate, expert_w_up, expert_b_up, expert_w_down, expert_b_down):
    bp, lp, _ = x_prompt.shape
    bs, ls, _ = x_sample.shape
    tp, ts = bp * lp, bs * ls
    x = (x_prompt.reshape(tp, D_MODEL), x_sample.reshape(ts, D_MODEL))

    def router_params(layer):
        return router_w[layer].T.astype(BF16), router_b[layer].astype(F32).reshape(N_EXPERTS, 1)

    def finish_layer(layer, x, mix_a, mix_b, w_out):
        rwt, rb = router_params(layer)
        x1, xp, gates, eidx, rank, counts = _outproj_ln_router(
            mix_a, mix_b, x[0], x[1], w_out[:512].astype(BF16), w_out[512:].astype(BF16),
            _row(ln1_g[layer]), _row(ln1_b[layer]), rwt, rb)
        return _moe(layer, x1, xp, gates, eidx, rank, counts, _row(ln2_g[layer]), _row(ln2_b[layer]),
                    expert_w_gate, expert_b_gate, expert_w_up, expert_b_up, expert_w_down, expert_b_down, tp)

    mix_rows = tp + ts

    wi = w_in_even[0]
    wq, wk, wv, wg, wlr, wglu = jnp.split(wi, [256, 512, 1024, 1536, 1552], axis=1)
    w_even = jnp.concatenate([_pad_heads(wq, GLA_HEADS, GLA_DK), _pad_heads(wk, GLA_HEADS, GLA_DK), wv, wg, wglu,
                              jnp.pad(wlr, ((0, 0), (0, LANE - GLA_RANK)))], axis=1).astype(BF16)
    cols_gla = (0, 512, 1024, 1536, 3072)
    col_a, col_gate = 2048, 2560
    h = _inproj(x[0], x[1], w_even)
    wlr_p = jnp.pad(_pad_heads(w_gla_gate_lr[0], GLA_HEADS, GLA_DK), ((0, LANE - GLA_RANK), (0, 0)))
    blr_p = _row(_pad_heads(b_gla_gate[0], GLA_HEADS, GLA_DK))
    nw = _row(gla_norm_w[0])
    conf_args = (conf_conv_w[0], _row(conf_conv_b[0]), _row(conf_ln_g[0]), _row(conf_ln_b[0]))
    s0_p = jnp.zeros((bp, GLA_HEADS, GLA_DK, HEAD_W), F32)
    s0_s = state_gla[0]
    mix_a, sg_p = _gla_call("gla", h, cols_gla, (wlr_p, blr_p), nw, s0_p, mix_rows, bp, lp, 0, False)
    mix_a, sg_s = _gla_call("gla", h, cols_gla, (wlr_p, blr_p), nw, s0_s, mix_a, bs, ls, tp, True)
    mix_b, cc_p = _conf_call(h, col_a, col_gate, jnp.zeros((bp,) + cache_conformer.shape[2:], F32), *conf_args,
                             mix_rows, bp, lp, 0, False)
    mix_b, cc_s = _conf_call(h, col_a, col_gate, cache_conformer[0], *conf_args, mix_b, bs, ls, tp, True)
    x = finish_layer(0, x, mix_a, mix_b, w_out_even[0])
    gla_p, gla_s = sg_p[None], sg_s[None]
    conf_p, conf_s = cc_p[None], cc_s[None]

    lb_cum = jnp.cumsum(jax.nn.softmax(hgrn_lower_bounds.astype(F32), axis=0), axis=0)
    lower_bound = _row((lb_cum - lb_cum[0])[1])
    wo = w_in_odd[0]
    w_odd = jnp.concatenate([wo[:, 2560:3584], wo[:, :2560],
                             jnp.pad(wo[:, 3584:], ((0, 0), (0, LANE - SSM_HEADS)))], axis=1).astype(BF16)
    h = _inproj(x[0], x[1], w_odd)
    cols_hgrn = (1024, 1536, 2048, 2560)
    col_z, col_x, col_dt = 3072, 0, 3584
    nw = _row(hgrn_norm_w[0])
    mix_a, sh_p = _gla_call("hgrn", h, cols_hgrn, lower_bound, nw,
                            jnp.zeros((bp, HGRN_HEADS, HEAD_W, HEAD_W), F32), mix_rows, bp, lp, 0, False)
    mix_a, sh_s = _gla_call("hgrn", h, cols_hgrn, lower_bound, nw, state_hgrn[0], mix_a, bs, ls, tp, True)

    def pair_states(s):
        return jnp.swapaxes(s, 2, 3).reshape(s.shape[0], SSM_HEADS // 2, HEAD_W, SSM_STATE)

    def unpair_states(s):
        return jnp.swapaxes(s.reshape(s.shape[0], SSM_HEADS, SSM_HEADDIM, SSM_STATE), 2, 3)

    pad8 = lambda v: jnp.pad(v.astype(F32), (0, LANE - SSM_HEADS)).reshape(1, LANE)
    ssd_args = (mamba_conv_w[0], _row(mamba_conv_b[0]), pad8(mamba_dt_bias[0]), pad8(mamba_a_log[0]),
                _row(jnp.repeat(mamba_d[0], SSM_HEADDIM)), _row(mamba_norm_w[0]))
    mix_b, cm_p, ss_p = _ssd_call(h, col_z, col_x, col_dt, jnp.zeros((bp,) + cache_mamba_conv.shape[2:], F32),
                                  jnp.zeros((bp, SSM_HEADS // 2, HEAD_W, SSM_STATE), F32), *ssd_args,
                                  mix_rows, bp, lp, 0, False)
    mix_b, cm_s, ss_s = _ssd_call(h, col_z, col_x, col_dt, cache_mamba_conv[0], pair_states(state_ssm[0]),
                                  *ssd_args, mix_b, bs, ls, tp, True)
    y_prompt, y_sample = finish_layer(1, x, mix_a, mix_b, w_out_odd[0])
    y_prompt = y_prompt.reshape(bp, lp, D_MODEL)
    y_sample = y_sample.reshape(bs, ls, D_MODEL)
    return (y_prompt, y_sample, gla_p, gla_s, conf_p, conf_s, sh_p[None], sh_s[None],
            unpair_states(ss_p)[None], unpair_states(ss_s)[None], cm_p[None], cm_s[None])
```
